```python
import math
import jax, jax.numpy as jnp
from jax import lax
import numpy as np

D_MODEL = 1024
BATCH = 8
SEQ = 2048
DEPTH = 1

HEAD_DIM = 64
NSA_HEADS = 8
NSA_KV_HEADS = 2
NSA_WIDTH = NSA_HEADS * HEAD_DIM
KV_WIDTH = NSA_KV_HEADS * HEAD_DIM
CONV_CH = D_MODEL - NSA_WIDTH
CMP_LEN = 32
CMP_STRIDE = 16
CMP_HIDDEN = 4 * HEAD_DIM
SLC_BLOCK = 64
SLC_TOPN = 8
WINDOW = 512
Q_BLOCK = 128
CONV_WIDTH = 31
FFN_HIDDEN = -(-8 * D_MODEL // (3 * 256)) * 256
ROPE_THETA = 10000.0
EPS = 1e-6
NEG = -1e30
FORCE = 1e6
IN_WIDTH = NSA_WIDTH + 6 * KV_WIDTH + 3 * NSA_HEADS + 2 * CONV_CH

kernel_name = "hymba_nsa_conformer_conv_hybrid"


def rms_norm(x, g):
    xf = x.astype(jnp.float32)
    y = xf * lax.rsqrt(jnp.mean(xf * xf, axis=-1, keepdims=True) + EPS)
    return (y * g.astype(jnp.float32)).astype(x.dtype)


def rope(x, pos):
    half = HEAD_DIM // 2
    inv = ROPE_THETA ** (-jnp.arange(half, dtype=jnp.float32) / half)
    ang = pos.astype(jnp.float32)[:, None] * inv[None, :]
    cos, sin = jnp.cos(ang), jnp.sin(ang)
    xf = x.astype(jnp.float32)
    x1, x2 = xf[..., :half], xf[..., half:]
    return jnp.concatenate([x1 * cos - x2 * sin, x2 * cos + x1 * sin], axis=-1).astype(x.dtype)


def masked_softmax(s, m):
    p = jax.nn.softmax(jnp.where(m, s, NEG), axis=-1)
    return jnp.where(m, p, 0.0)


def n_cmp_blocks(seq):
    return (seq - CMP_LEN) // CMP_STRIDE + 1


def cmp_to_slc_overlap(seq):
    cs = np.arange(n_cmp_blocks(seq))[:, None] * CMP_STRIDE
    ss = np.arange(seq // SLC_BLOCK)[None, :] * SLC_BLOCK
    ov = np.clip(np.minimum(cs + CMP_LEN, ss + SLC_BLOCK) - np.maximum(cs, ss), 0, None)
    return (ov / CMP_LEN).astype(np.float32)


def compress(t, pe, w1, w2):
    b, g, s, hd = t.shape
    ncmp = n_cmp_blocks(s)
    idx = np.arange(ncmp)[:, None] * CMP_STRIDE + np.arange(CMP_LEN)[None, :]
    blocks = t[:, :, idx] + pe
    flat = blocks.reshape(b, g, ncmp, CMP_LEN * hd)
    return jax.nn.silu(flat @ w1) @ w2


def nsa_attention(q, kc, vc, ks, vs, kw, vw, gates):
    b, h, s, hd = q.shape
    g = kc.shape[1]
    r = h // g
    nb = s // SLC_BLOCK
    n_sel = min(SLC_TOPN, nb)
    n_qb = s // Q_BLOCK
    ncmp = kc.shape[2]
    overlap = jnp.asarray(cmp_to_slc_overlap(s))
    cmp_end = jnp.arange(ncmp) * CMP_STRIDE + CMP_LEN - 1
    scale = HEAD_DIM ** -0.5
    kcf, vcf = kc.astype(jnp.float32), vc.astype(jnp.float32)
    ks_blk = ks.reshape(b, g, nb, SLC_BLOCK, hd)
    vs_blk = vs.reshape(b, g, nb, SLC_BLOCK, hd)
    kw_pad = jnp.pad(kw, ((0, 0), (0, 0), (WINDOW, 0), (0, 0)))
    vw_pad = jnp.pad(vw, ((0, 0), (0, 0), (WINDOW, 0), (0, 0)))
    qb = q.reshape(b, g, r, n_qb, Q_BLOCK, hd).transpose(3, 0, 1, 2, 4, 5)
    gb = gates.reshape(b, g, r, n_qb, Q_BLOCK, 3).transpose(3, 0, 1, 2, 4, 5)
    bi = jnp.arange(b)[:, None, None, None]
    gi = jnp.arange(g)[None, :, None, None]
    blk = jnp.arange(nb)

    def block_fn(args):
        c, qc, gc = args
        t = c * Q_BLOCK + jnp.arange(Q_BLOCK)
        qf = qc.astype(jnp.float32) * scale
        s_c = jnp.einsum('bgrqd,bgnd->bgrqn', qf, kcf)
        p_c = masked_softmax(s_c, cmp_end[None, :] <= t[:, None])
        o_c = jnp.einsum('bgrqn,bgnd->bgrqd', p_c, vcf)
        imp = jnp.einsum('bgrqn,nj->bgqj', p_c, overlap)
        cur = t // SLC_BLOCK
        valid = blk[None, :] <= cur[:, None]
        forced = (blk[None, :] == 0) | (blk[None, :] == cur[:, None]) | (blk[None, :] == cur[:, None] - 1)
        score = jnp.where(valid, imp + jnp.where(forced, FORCE, 0.0), -FORCE)
        top_val, top_idx = lax.top_k(score, n_sel)
        k_sel = ks_blk[bi, gi, top_idx].astype(jnp.float32)
        v_sel = vs_blk[bi, gi, top_idx].astype(jnp.float32)
        tok = top_idx[..., None] * SLC_BLOCK + jnp.arange(SLC_BLOCK)
        m_s = (tok <= t[None, None, :, None, None]) & (top_val > -1.0)[..., None]
        s_s = jnp.einsum('bgrqd,bgqnld->bgrqnl', qf, k_sel).reshape(b, g, r, Q_BLOCK, n_sel * SLC_BLOCK)
        p_s = masked_softmax(s_s, m_s.reshape(b, g, 1, Q_BLOCK, n_sel * SLC_BLOCK))
        o_s = jnp.einsum('bgrqk,bgqkd->bgrqd', p_s, v_sel.reshape(b, g, Q_BLOCK, n_sel * SLC_BLOCK, hd))
        start = c * Q_BLOCK
        k_win = lax.dynamic_slice_in_dim(kw_pad, start, WINDOW + Q_BLOCK, axis=2).astype(jnp.float32)
        v_win = lax.dynamic_slice_in_dim(vw_pad, start, WINDOW + Q_BLOCK, axis=2).astype(jnp.float32)
        kpos = start - WINDOW + jnp.arange(WINDOW + Q_BLOCK)
        m_w = (kpos[None, :] <= t[:, None]) & (kpos[None, :] > t[:, None] - WINDOW) & (kpos[None, :] >= 0)
        s_w = jnp.einsum('bgrqd,bgkd->bgrqk', qf, k_win)
        o_w = jnp.einsum('bgrqk,bgkd->bgrqd', masked_softmax(s_w, m_w), v_win)
        o = gc[..., 0:1] * o_c + gc[..., 1:2] * o_s + gc[..., 2:3] * o_w
        return o.astype(q.dtype)

    out = lax.map(block_fn, (jnp.arange(n_qb), qb, gb))
    out = out.transpose(1, 2, 3, 0, 4, 5).reshape(b, h, s, hd)
    return out.transpose(0, 2, 1, 3).reshape(b, s, h * hd)


def conformer_conv(u, w_dw, b_dw, ln_g, ln_b):
    a, gate = jnp.split(u, 2, axis=-1)
    hcv = a * jax.nn.sigmoid(gate)
    hcv = lax.conv_general_dilated(hcv, w_dw, window_strides=(1,), padding=[(CONV_WIDTH - 1, 0)],
                                   dimension_numbers=('NWC', 'WIO', 'NWC'), feature_group_count=CONV_CH) + b_dw
    hf = hcv.astype(jnp.float32)
    mu = jnp.mean(hf, axis=-1, keepdims=True)
    var = jnp.mean(jnp.square(hf - mu), axis=-1, keepdims=True)
    hn = (hf - mu) * lax.rsqrt(var + EPS) * ln_g.astype(jnp.float32) + ln_b.astype(jnp.float32)
    return jax.nn.silu(hn).astype(u.dtype)


def setup_inputs(seed: int = 0) -> dict:
    key = jax.random.key(seed)
    ks = jax.random.split(key, 24)
    f32 = jnp.float32
    L = DEPTH

    def nrm(k, shape, scale):
        return jax.random.normal(k, shape, f32) * scale

    def gain(k, n):
        return 1.0 + 0.02 * jax.random.normal(k, (L, n), f32)

    return {
        "x": jax.random.normal(ks[0], (BATCH, SEQ, D_MODEL), f32),
        "attn_norm_g": gain(ks[1], D_MODEL),
        "w_in": nrm(ks[2], (L, D_MODEL, IN_WIDTH), D_MODEL ** -0.5),
        "q_norm_g": gain(ks[3], HEAD_DIM),
        "k_norm_cmp_g": gain(ks[4], HEAD_DIM),
        "k_norm_slc_g": gain(ks[5], HEAD_DIM),
        "k_norm_win_g": gain(ks[6], HEAD_DIM),
        "cmp_pe_k": nrm(ks[7], (L, CMP_LEN, HEAD_DIM), 0.1),
        "cmp_w1_k": nrm(ks[8], (L, CMP_LEN * HEAD_DIM, CMP_HIDDEN), (CMP_LEN * HEAD_DIM) ** -0.5),
        "cmp_w2_k": nrm(ks[9], (L, CMP_HIDDEN, HEAD_DIM), CMP_HIDDEN ** -0.5),
        "cmp_pe_v": nrm(ks[10], (L, CMP_LEN, HEAD_DIM), 0.1),
        "cmp_w1_v": nrm(ks[11], (L, CMP_LEN * HEAD_DIM, CMP_HIDDEN), (CMP_LEN * HEAD_DIM) ** -0.5),
        "cmp_w2_v": nrm(ks[12], (L, CMP_HIDDEN, HEAD_DIM), CMP_HIDDEN ** -0.5),
        "conv_dw_w": nrm(ks[13], (L, CONV_WIDTH, 1, CONV_CH), CONV_WIDTH ** -0.5),
        "conv_dw_b": nrm(ks[14], (L, CONV_CH), 0.02),
        "conv_ln_g": gain(ks[15], CONV_CH),
        "conv_ln_b": nrm(ks[16], (L, CONV_CH), 0.02),
        "out_norm_nsa_g": gain(ks[17], NSA_WIDTH),
        "out_norm_conv_g": gain(ks[18], CONV_CH),
        "w_out": nrm(ks[19], (L, D_MODEL, D_MODEL), D_MODEL ** -0.5),
        "ffn_norm_g": gain(ks[20], D_MODEL),
        "w_gate_up": nrm(ks[21], (L, D_MODEL, 2 * FFN_HIDDEN), D_MODEL ** -0.5),
        "w_down": nrm(ks[22], (L, FFN_HIDDEN, D_MODEL), FFN_HIDDEN ** -0.5),
    }


def reference(x, attn_norm_g, w_in, q_norm_g, k_norm_cmp_g, k_norm_slc_g, k_norm_win_g,
              cmp_pe_k, cmp_w1_k, cmp_w2_k, cmp_pe_v, cmp_w1_v, cmp_w2_v,
              conv_dw_w, conv_dw_b, conv_ln_g, conv_ln_b, out_norm_nsa_g, out_norm_conv_g,
              w_out, ffn_norm_g, w_gate_up, w_down):
    b, s, _ = x.shape
    pos = jnp.arange(s)
    cmp_pos = jnp.arange(n_cmp_blocks(s)) * CMP_STRIDE + CMP_LEN - 1
    offs = list(np.cumsum([NSA_WIDTH] + [KV_WIDTH] * 6 + [3 * NSA_HEADS]))

    def heads(t, n):
        return t.reshape(b, s, n, HEAD_DIM).transpose(0, 2, 1, 3)

    for l in range(DEPTH):
        xn = rms_norm(x, attn_norm_g[l])
        proj = xn @ w_in[l]
        q, kc_raw, vc_raw, ks_, vs_, kw_, vw_, gate_logits, conv_in = jnp.split(proj, offs, axis=-1)
        q = rope(rms_norm(heads(q, NSA_HEADS), q_norm_g[l]), pos)
        k_slc = rope(rms_norm(heads(ks_, NSA_KV_HEADS), k_norm_slc_g[l]), pos)
        k_win = rope(rms_norm(heads(kw_, NSA_KV_HEADS), k_norm_win_g[l]), pos)
        k_cmp = compress(heads(kc_raw, NSA_KV_HEADS), cmp_pe_k[l], cmp_w1_k[l], cmp_w2_k[l])
        k_cmp = rope(rms_norm(k_cmp, k_norm_cmp_g[l]), cmp_pos)
        v_cmp = compress(heads(vc_raw, NSA_KV_HEADS), cmp_pe_v[l], cmp_w1_v[l], cmp_w2_v[l])
        gates = jax.nn.sigmoid(gate_logits.astype(jnp.float32)).reshape(b, s, NSA_HEADS, 3).transpose(0, 2, 1, 3)
        o_nsa = nsa_attention(q, k_cmp, v_cmp, k_slc, heads(vs_, NSA_KV_HEADS), k_win,
                              heads(vw_, NSA_KV_HEADS), gates)
        o_conv = conformer_conv(conv_in, conv_dw_w[l], conv_dw_b[l], conv_ln_g[l], conv_ln_b[l])
        mix = jnp.concatenate([rms_norm(o_nsa, out_norm_nsa_g[l]), rms_norm(o_conv, out_norm_conv_g[l])], axis=-1)
        x = x + mix @ w_out[l]
        hn = rms_norm(x, ffn_norm_g[l])
        g_ff, u_ff = jnp.split(hn @ w_gate_up[l], 2, axis=-1)
        x = x + (jax.nn.silu(g_ff) * u_ff) @ w_down[l]
    return x
```

```python
import functools

import numpy as np
import jax
import jax.numpy as jnp
from jax import lax
from jax.experimental import pallas as pl
from jax.experimental.pallas import tpu as pltpu

F32 = jnp.float32
BF16 = jnp.bfloat16

D_MODEL = 1024
HEAD_DIM = 64
NSA_HEADS = 8
NSA_KV_HEADS = 2
GQA_REP = NSA_HEADS // NSA_KV_HEADS
NSA_WIDTH = NSA_HEADS * HEAD_DIM
KV_WIDTH = NSA_KV_HEADS * HEAD_DIM
CONV_CH = D_MODEL - NSA_WIDTH
CMP_LEN = 32
CMP_STRIDE = 16
CMP_HIDDEN = 4 * HEAD_DIM
SLC_BLOCK = 64
SLC_TOPN = 8
WINDOW = 512
CONV_WIDTH = 31
FFN_HIDDEN = 2816
ROPE_THETA = 10000.0
EPS = 1e-6
NEG = -1e30
FORCE = 1e6

LANES = 128
Q_TILE = 128
SLC_CHUNK = 256
WIN_KEYS = WINDOW + Q_TILE
PROJ_TILE = 512
CONV_TILE = 512
CONV_HALO = 32
CONV_ROWS = 128
FFN_TILE = 512
FFN_CHUNK = 256
VMEM_LIMIT = 56 * 1024 * 1024

C_Q = 0
C_KS = 512
C_KW = 640
C_VS = 768
C_VW = 896
C_KC = 1024
C_VC = 1152
C_CA = 1280
C_CG = 1792
C_GL = 2304
PROJ_COLS = 2432

_TRANS_B = (((1,), (1,)), ((), ()))


def _dot(a, b):
    return jnp.dot(a, b, preferred_element_type=F32)


def _dot_tb(a, b):
    return lax.dot_general(a, b, _TRANS_B, preferred_element_type=F32)


def _split_bf16(x):
    hi = x.astype(BF16)
    lo = (x - hi.astype(F32)).astype(BF16)
    return hi, lo


def _sigmoid(x):
    return 1.0 / (1.0 + jnp.exp(-x))


def _head_norm_rope(y, gain, bd, cos, sin, first_half, scale):
    hi, lo = _split_bf16(y * y)
    ss = _dot(hi, bd) + _dot(lo, bd)
    yn = y * lax.rsqrt(ss * (1.0 / HEAD_DIM) + EPS) * gain
    outs = []
    for j in range(y.shape[1] // LANES):
        blk = yn[:, j * LANES:(j + 1) * LANES]
        rot = jnp.where(first_half, pltpu.roll(blk, LANES - HEAD_DIM // 2, 1),
                        pltpu.roll(blk, HEAD_DIM // 2, 1))
        outs.append((blk * cos + rot * sin) * scale)
    return outs


def _proj_body(x_ref, gin_ref, w_ref, cos_ref, sin_ref, gq_ref, gk_ref, bd_ref,
               q_ref, ks_ref, kw_ref, vst_ref, vwt_ref, kc_ref, vc_ref, gt_ref, hcv_ref):
    x = x_ref[...]
    ms = jnp.mean(x * x, axis=-1, keepdims=True)
    xn = (x * lax.rsqrt(ms + EPS) * gin_ref[...]).astype(BF16)
    cos = cos_ref[...]
    sin = sin_ref[...]
    bd = bd_ref[...]
    lane = lax.broadcasted_iota(jnp.int32, (PROJ_TILE, LANES), 1)
    first_half = (lane & (HEAD_DIM // 2)) == 0

    for half in range(2):
        c0 = C_Q + half * 256
        qh = _dot(xn, w_ref[:, c0:c0 + 256])
        blks = _head_norm_rope(qh, gq_ref[...], bd, cos, sin, first_half, HEAD_DIM ** -0.5)
        for j in range(2):
            q_ref[:, (2 * half + j) * LANES:(2 * half + j + 1) * LANES] = blks[j].astype(BF16)

    kk = _dot(xn, w_ref[:, C_KS:C_KS + 256])
    kblks = _head_norm_rope(kk, gk_ref[...], bd, cos, sin, first_half, 1.0)
    ks_ref[...] = kblks[0].astype(BF16)
    kw_ref[...] = kblks[1].astype(BF16)

    vv = _dot(xn, w_ref[:, C_VS:C_VS + 256])
    for i in range(PROJ_TILE // LANES):
        rows = vv[i * LANES:(i + 1) * LANES, :]
        vst_ref[i] = rows[:, :LANES].T.astype(BF16)
        vwt_ref[i] = rows[:, LANES:].T.astype(BF16)

    cc = _dot(xn, w_ref[:, C_KC:C_KC + 256])
    kc_ref[...] = cc[:, :LANES]
    vc_ref[...] = cc[:, LANES:]

    ca = _dot(xn, w_ref[:, C_CA:C_CA + CONV_CH])
    cg = _dot(xn, w_ref[:, C_CG:C_CG + CONV_CH])
    hcv_ref[...] = ca * _sigmoid(cg)

    sg = _sigmoid(_dot(xn, w_ref[:, C_GL:C_GL + LANES]))
    for i in range(PROJ_TILE // LANES):
        gt_ref[i] = sg[i * LANES:(i + 1) * LANES, :].T[0:32, :]


def _proj_call(x2, gin, w_cat, cos, sin, gq, gk, bd, seq):
    t = x2.shape[0]
    nt = t // PROJ_TILE
    tiles_per_seq = seq // PROJ_TILE
    row = lambda i: (i, 0)
    const = lambda i: (0, 0)
    tab = lambda i: (i % tiles_per_seq, 0)
    blk3 = lambda i: (i, 0, 0)
    nb = PROJ_TILE // LANES
    return pl.pallas_call(
        _proj_body,
        grid=(nt,),
        in_specs=[
            pl.BlockSpec((PROJ_TILE, D_MODEL), row),
            pl.BlockSpec((1, D_MODEL), const),
            pl.BlockSpec((D_MODEL, PROJ_COLS), const),
            pl.BlockSpec((PROJ_TILE, LANES), tab),
            pl.BlockSpec((PROJ_TILE, LANES), tab),
            pl.BlockSpec((1, 256), const),
            pl.BlockSpec((1, 256), const),
            pl.BlockSpec((256, 256), const),
        ],
        out_specs=[
            pl.BlockSpec((PROJ_TILE, NSA_WIDTH), row),
            pl.BlockSpec((PROJ_TILE, LANES), row),
            pl.BlockSpec((PROJ_TILE, LANES), row),
            pl.BlockSpec((nb, LANES, LANES), blk3),
            pl.BlockSpec((nb, LANES, LANES), blk3),
            pl.BlockSpec((PROJ_TILE, LANES), row),
            pl.BlockSpec((PROJ_TILE, LANES), row),
            pl.BlockSpec((nb, 32, LANES), blk3),
            pl.BlockSpec((PROJ_TILE, CONV_CH), row),
        ],
        out_shape=[
            jax.ShapeDtypeStruct((t, NSA_WIDTH), BF16),
            jax.ShapeDtypeStruct((t, LANES), BF16),
            jax.ShapeDtypeStruct((t, LANES), BF16),
            jax.ShapeDtypeStruct((t // LANES, LANES, LANES), BF16),
            jax.ShapeDtypeStruct((t // LANES, LANES, LANES), BF16),
            jax.ShapeDtypeStruct((t, LANES), F32),
            jax.ShapeDtypeStruct((t, LANES), F32),
            jax.ShapeDtypeStruct((t // LANES, 32, LANES), F32),
            jax.ShapeDtypeStruct((t, CONV_CH), F32),
        ],
        compiler_params=pltpu.CompilerParams(
            dimension_semantics=("parallel",), vmem_limit_bytes=VMEM_LIMIT),
        name="proj",
    )(x2, gin, w_cat, cos, sin, gq, gk, bd)


def _cmp_mlp(t_ref, pe_ref, w1_ref, w2_ref):
    half = CMP_LEN // 2
    nchunk = t_ref.shape[0] // CMP_STRIDE
    p = jnp.zeros((nchunk, 2 * CMP_HIDDEN), F32)
    q = jnp.zeros((nchunk, 2 * CMP_HIDDEN), F32)
    for l in range(half):
        xl = t_ref[pl.ds(l, nchunk, stride=CMP_STRIDE), :]
        p = p + _dot((xl + pe_ref[l:l + 1, :]).astype(BF16), w1_ref[l])
        q = q + _dot((xl + pe_ref[half + l:half + l + 1, :]).astype(BF16), w1_ref[half + l])
    h1 = p + pltpu.roll(q, nchunk - 1, 0)
    act = (h1 * _sigmoid(h1)).astype(BF16)
    return _dot(act, w2_ref[...])


def _cmp_body(kc_ref, vc_ref, pek_ref, w1k_ref, w2k_ref, pev_ref, w1v_ref, w2v_ref,
              cos_ref, sin_ref, gk_ref, bd_ref, ko_ref, vto_ref):
    kraw = _cmp_mlp(kc_ref, pek_ref, w1k_ref, w2k_ref)
    lane = lax.broadcasted_iota(jnp.int32, kraw.shape, 1)
    first_half = (lane & (HEAD_DIM // 2)) == 0
    (kr,) = _head_norm_rope(kraw, gk_ref[...], bd_ref[...], cos_ref[...], sin_ref[...],
                            first_half, 1.0)
    ko_ref[...] = kr.astype(BF16)
    vraw = _cmp_mlp(vc_ref, pev_ref, w1v_ref, w2v_ref)
    vto_ref[0] = vraw.T.astype(BF16)


def _cmp_call(kc, vc, pek, w1k, w2k, pev, w1v, w2v, cos, sin, gk, bd, batch, seq):
    seqblk = lambda b: (b, 0)
    c2 = lambda b: (0, 0)
    c3 = lambda b: (0, 0, 0)
    ncmp_pad = seq // CMP_STRIDE
    return pl.pallas_call(
        _cmp_body,
        grid=(batch,),
        in_specs=[
            pl.BlockSpec((seq, LANES), seqblk),
            pl.BlockSpec((seq, LANES), seqblk),
            pl.BlockSpec((CMP_LEN, LANES), c2),
            pl.BlockSpec((CMP_LEN, LANES, 2 * CMP_HIDDEN), c3),
            pl.BlockSpec((2 * CMP_HIDDEN, LANES), c2),
            pl.BlockSpec((CMP_LEN, LANES), c2),
            pl.BlockSpec((CMP_LEN, LANES, 2 * CMP_HIDDEN), c3),
            pl.BlockSpec((2 * CMP_HIDDEN, LANES), c2),
            pl.BlockSpec((ncmp_pad, LANES), c2),
            pl.BlockSpec((ncmp_pad, LANES), c2),
            pl.BlockSpec((1, LANES), c2),
            pl.BlockSpec((LANES, LANES), c2),
        ],
        out_specs=[
            pl.BlockSpec((ncmp_pad, LANES), seqblk),
            pl.BlockSpec((1, LANES, ncmp_pad), lambda b: (b, 0, 0)),
        ],
        out_shape=[
            jax.ShapeDtypeStruct((batch * ncmp_pad, LANES), BF16),
            jax.ShapeDtypeStruct((batch, LANES, ncmp_pad), BF16),
        ],
        compiler_params=pltpu.CompilerParams(
            dimension_semantics=("parallel",), vmem_limit_bytes=VMEM_LIMIT),
        name="compress",
    )(kc, vc, pek, w1k, w2k, pev, w1v, w2v, cos, sin, gk, bd)


def _lane_cat(blocks):
    return jnp.concatenate(blocks, axis=1)


def _masked_exp(s, mask, m_floor=None):
    es, ms, ls = [], [], []
    for r in range(GQA_REP):
        sm = jnp.where(mask, s[:, r * LANES:(r + 1) * LANES], NEG)
        m = jnp.max(sm, axis=0, keepdims=True)
        if m_floor is not None:
            m = jnp.maximum(m, m_floor[:, r * LANES:(r + 1) * LANES])
        e = jnp.where(mask, jnp.exp(sm - m), 0.0)
        es.append(e)
        ms.append(m)
        ls.append(jnp.sum(e, axis=0, keepdims=True))
    return _lane_cat(es), _lane_cat(ms), _lane_cat(ls)


def _safe_inv(l):
    return 1.0 / jnp.where(l > 0.0, l, 1.0)


def _attn_body(q_ref, ks_ref, kw_ref, vst_ref, vwt_ref, kc_ref, vct_ref, gt_ref,
               ovl_ref, et_ref, gn_ref, out_ref, acc_ref):
    c = pl.program_id(1)
    t_lane = c * Q_TILE + lax.broadcasted_iota(jnp.int32, (1, LANES), 1)
    lane_sq = lax.broadcasted_iota(jnp.int32, (Q_TILE, LANES), 1)
    nblk = ovl_ref.shape[0]
    nsel = 32
    j_idx = lax.broadcasted_iota(jnp.int32, (nsel, LANES), 0)
    cur = lax.shift_right_logical(t_lane, 6)
    n_idx = lax.broadcasted_iota(jnp.int32, (LANES, LANES), 0)
    cmask = (n_idx * CMP_STRIDE + (CMP_LEN - 1) <= t_lane) & (n_idx < LANES - 1)
    win_start = jnp.maximum(c - WINDOW // Q_TILE, 0) * Q_TILE
    win_blk = jnp.maximum(c - WINDOW // Q_TILE, 0)
    kpos_w = win_start + lax.broadcasted_iota(jnp.int32, (WIN_KEYS, LANES), 0)
    wmask = (kpos_w <= t_lane) & (kpos_w > t_lane - WINDOW)
    row_sl = lax.broadcasted_iota(jnp.int32, (SLC_CHUNK, LANES), 0)

    o_groups = []
    for g in range(NSA_KV_HEADS):
        in_g = (lane_sq >= HEAD_DIM) if g else (lane_sq < HEAD_DIM)
        zero = jnp.zeros((Q_TILE, LANES), BF16)
        qg = jnp.concatenate(
            [jnp.where(in_g, q_ref[:, r * LANES:(r + 1) * LANES], zero) for r in range(GQA_REP)],
            axis=0)

        sc = _dot_tb(kc_ref[...], qg)
        ec, _, lc = _masked_exp(sc, cmask)
        inv_lc = _safe_inv(lc)
        oc = _dot(vct_ref[0], ec.astype(BF16))
        pc = ec * inv_lc
        psum = (pc[:, 0:LANES] + pc[:, LANES:2 * LANES]
                + pc[:, 2 * LANES:3 * LANES] + pc[:, 3 * LANES:4 * LANES])
        p_hi, p_lo = _split_bf16(psum)
        imp = (_dot(ovl_ref[...], p_hi) + _dot(ovl_ref[...], p_lo))[0:nsel, :]

        valid = j_idx <= cur
        forced = (j_idx == 0) | (j_idx == cur) | (j_idx == cur - 1)
        score = jnp.where(valid, imp + jnp.where(forced, FORCE, 0.0), -FORCE)
        rank = jnp.zeros((nsel, LANES), F32)
        for jp in range(nsel):
            row = score[jp:jp + 1, :]
            beats = (row > score) | ((row == score) & (j_idx > jp))
            rank = rank + jnp.where(beats, 1.0, 0.0)
        sel = jnp.where((rank < float(SLC_TOPN)) & (score > -1.0), 1.0, 0.0)
        sel_t = jnp.concatenate([sel, jnp.zeros((nblk - nsel, LANES), F32)], axis=0).astype(BF16)

        acc_ref[...] = jnp.zeros_like(acc_ref)

        def slc_step(kc, carry):
            m_run, l_run = carry
            k0 = pl.multiple_of(kc * SLC_CHUNK, SLC_CHUNK)
            s = _dot_tb(ks_ref[pl.ds(k0, SLC_CHUNK), :], qg)
            picked = _dot(et_ref[kc], sel_t) > 0.5
            mask = picked & (kc * SLC_CHUNK + row_sl <= t_lane)
            e, m_new, l_new = _masked_exp(s, mask, m_floor=m_run)
            alpha = jnp.exp(m_run - m_new)
            eb = e.astype(BF16)
            pv = (_dot(vst_ref[2 * kc], eb[0:LANES, :])
                  + _dot(vst_ref[2 * kc + 1], eb[LANES:2 * LANES, :]))
            acc_ref[...] = acc_ref[...] * alpha + pv
            return m_new, alpha * l_run + l_new

        n_chunks = lax.shift_right_logical(c, 1) + 1
        init = (jnp.full((1, GQA_REP * LANES), NEG, F32), jnp.zeros((1, GQA_REP * LANES), F32))
        _, ls = lax.fori_loop(0, n_chunks, slc_step, init)
        os_ = acc_ref[...]
        inv_ls = _safe_inv(ls)

        sw = _dot_tb(kw_ref[pl.ds(pl.multiple_of(win_start, Q_TILE), WIN_KEYS), :], qg)
        ew, _, lw = _masked_exp(sw, wmask)
        ewb = ew.astype(BF16)
        ow = _dot(vwt_ref[win_blk], ewb[0:LANES, :])
        for i in range(1, WIN_KEYS // LANES):
            ow = ow + _dot(vwt_ref[win_blk + i], ewb[i * LANES:(i + 1) * LANES, :])
        inv_lw = _safe_inv(lw)

        def gate_row(branch):
            return _lane_cat([gt_ref[0, (g * GQA_REP + r) * 3 + branch:(g * GQA_REP + r) * 3 + branch + 1, :]
                              for r in range(GQA_REP)])
        o_groups.append(oc * (gate_row(0) * inv_lc) + os_ * (gate_row(1) * inv_ls)
                        + ow * (gate_row(2) * inv_lw))

    row_d = lax.broadcasted_iota(jnp.int32, (LANES, GQA_REP * LANES), 0)
    o_t = jnp.where(row_d < HEAD_DIM, o_groups[0], o_groups[1])
    blks = [o_t[:, r * LANES:(r + 1) * LANES].T for r in range(GQA_REP)]
    ss = blks[0] * blks[0]
    for r in range(1, GQA_REP):
        ss = ss + blks[r] * blks[r]
    rs = lax.rsqrt(jnp.sum(ss, axis=-1, keepdims=True) * (1.0 / NSA_WIDTH) + EPS)
    for r in range(GQA_REP):
        out_ref[:, r * LANES:(r + 1) * LANES] = (
            blks[r] * rs * gn_ref[:, r * LANES:(r + 1) * LANES]).astype(BF16)


def _attn_call(q, ks, kw, vst, vwt, kcmp, vcmpt, gt, ovl, et, gn, batch, seq):
    nq = seq // Q_TILE
    nkb = seq // LANES
    return pl.pallas_call(
        _attn_body,
        grid=(batch, nq),
        in_specs=[
            pl.BlockSpec((Q_TILE, NSA_WIDTH), lambda b, c: (b * nq + c, 0)),
            pl.BlockSpec((seq, LANES), lambda b, c: (b, 0)),
            pl.BlockSpec((seq, LANES), lambda b, c: (b, 0)),
            pl.BlockSpec((nkb, LANES, LANES), lambda b, c: (b, 0, 0)),
            pl.BlockSpec((nkb, LANES, LANES), lambda b, c: (b, 0, 0)),
            pl.BlockSpec((LANES, LANES), lambda b, c: (b, 0)),
            pl.BlockSpec((1, LANES, LANES), lambda b, c: (b, 0, 0)),
            pl.BlockSpec((1, 32, LANES), lambda b, c: (b * nq + c, 0, 0)),
            pl.BlockSpec((LANES, LANES), lambda b, c: (0, 0)),
            pl.BlockSpec((seq // SLC_CHUNK, SLC_CHUNK, LANES), lambda b, c: (0, 0, 0)),
            pl.BlockSpec((1, NSA_WIDTH), lambda b, c: (0, 0)),
        ],
        out_specs=pl.BlockSpec((Q_TILE, NSA_WIDTH), lambda b, c: (b * nq + c, 0)),
        out_shape=jax.ShapeDtypeStruct((batch * seq, NSA_WIDTH), BF16),
        scratch_shapes=[pltpu.VMEM((LANES, GQA_REP * LANES), F32)],
        compiler_params=pltpu.CompilerParams(
            dimension_semantics=("parallel", "arbitrary"), vmem_limit_bytes=VMEM_LIMIT),
        name="attn",
    )(q, ks, kw, vst, vwt, kcmp, vcmpt, gt, ovl, et, gn)


def _conv_body(tiles_per_seq, cur_ref, halo_ref, w_ref, b_ref, lng_ref, lnb_ref, gn_ref, out_ref,
               buf_ref, cv_ref):
    first = (pl.program_id(0) % tiles_per_seq) == 0
    buf_ref[0:CONV_HALO, :] = jnp.where(first, 0.0, halo_ref[...])
    buf_ref[CONV_HALO:, :] = cur_ref[...]
    lead = CONV_HALO - (CONV_WIDTH - 1)
    for rb in range(CONV_TILE // CONV_ROWS):
        for cb in range(CONV_CH // LANES):
            cols = slice(cb * LANES, (cb + 1) * LANES)
            acc = jnp.zeros((CONV_ROWS, LANES), F32) + b_ref[:, cols]
            for k in range(CONV_WIDTH):
                r0 = rb * CONV_ROWS + lead + k
                acc = acc + w_ref[k:k + 1, cols] * buf_ref[r0:r0 + CONV_ROWS, cols]
            cv_ref[rb * CONV_ROWS:(rb + 1) * CONV_ROWS, cols] = acc
    h = cv_ref[...]
    mu = jnp.mean(h, axis=-1, keepdims=True)
    d = h - mu
    var = jnp.mean(d * d, axis=-1, keepdims=True)
    hn = d * lax.rsqrt(var + EPS) * lng_ref[...] + lnb_ref[...]
    o = hn * _sigmoid(hn)
    ms = jnp.mean(o * o, axis=-1, keepdims=True)
    out_ref[...] = (o * lax.rsqrt(ms + EPS) * gn_ref[...]).astype(BF16)


def _conv_call(hcv, w, b, lng, lnb, gn, seq):
    t = hcv.shape[0]
    nt = t // CONV_TILE
    halo_per_tile = CONV_TILE // CONV_HALO
    row = lambda i: (i, 0)
    const = lambda i: (0, 0)
    return pl.pallas_call(
        functools.partial(_conv_body, seq // CONV_TILE),
        grid=(nt,),
        in_specs=[
            pl.BlockSpec((CONV_TILE, CONV_CH), row),
            pl.BlockSpec((CONV_HALO, CONV_CH), lambda i: (jnp.maximum(i * halo_per_tile - 1, 0), 0)),
            pl.BlockSpec((CONV_HALO, CONV_CH), const),
            pl.BlockSpec((1, CONV_CH), const),
            pl.BlockSpec((1, CONV_CH), const),
            pl.BlockSpec((1, CONV_CH), const),
            pl.BlockSpec((1, CONV_CH), const),
        ],
        out_specs=pl.BlockSpec((CONV_TILE, CONV_CH), row),
        out_shape=jax.ShapeDtypeStruct((t, CONV_CH), BF16),
        scratch_shapes=[pltpu.VMEM((CONV_HALO + CONV_TILE, CONV_CH), F32),
                        pltpu.VMEM((CONV_TILE, CONV_CH), F32)],
        compiler_params=pltpu.CompilerParams(
            dimension_semantics=("parallel",), vmem_limit_bytes=VMEM_LIMIT),
        name="conv",
    )(hcv, hcv, w, b, lng, lnb, gn)


def _ffn_body(x_ref, ma_ref, mb_ref, woa_ref, wob_ref, gf_ref, wg_ref, wu_ref, wd_ref, out_ref):
    h = x_ref[...] + _dot(ma_ref[...], woa_ref[...]) + _dot(mb_ref[...], wob_ref[...])
    ms = jnp.mean(h * h, axis=-1, keepdims=True)
    hn = (h * lax.rsqrt(ms + EPS) * gf_ref[...]).astype(BF16)
    acc = jnp.zeros_like(h)
    for j in range(FFN_HIDDEN // FFN_CHUNK):
        cols = slice(j * FFN_CHUNK, (j + 1) * FFN_CHUNK)
        gte = _dot(hn, wg_ref[:, cols])
        up = _dot(hn, wu_ref[:, cols])
        act = (gte * _sigmoid(gte) * up).astype(BF16)
        acc = acc + _dot(act, wd_ref[cols, :])
    out_ref[...] = h + acc


def _ffn_call(x2, ma, mb, woa, wob, gf, wg, wu, wd):
    t = x2.shape[0]
    nt = t // FFN_TILE
    row = lambda i: (i, 0)
    const = lambda i: (0, 0)
    once = pl.Buffered(1)
    return pl.pallas_call(
        _ffn_body,
        grid=(nt,),
        in_specs=[
            pl.BlockSpec((FFN_TILE, D_MODEL), row),
            pl.BlockSpec((FFN_TILE, NSA_WIDTH), row),
            pl.BlockSpec((FFN_TILE, CONV_CH), row),
            pl.BlockSpec((NSA_WIDTH, D_MODEL), const, pipeline_mode=once),
            pl.BlockSpec((CONV_CH, D_MODEL), const, pipeline_mode=once),
            pl.BlockSpec((1, D_MODEL), const),
            pl.BlockSpec((D_MODEL, FFN_HIDDEN), const, pipeline_mode=once),
            pl.BlockSpec((D_MODEL, FFN_HIDDEN), const, pipeline_mode=once),
            pl.BlockSpec((FFN_HIDDEN, D_MODEL), const, pipeline_mode=once),
        ],
        out_specs=pl.BlockSpec((FFN_TILE, D_MODEL), row),
        out_shape=jax.ShapeDtypeStruct((t, D_MODEL), F32),
        compiler_params=pltpu.CompilerParams(
            dimension_semantics=("parallel",), vmem_limit_bytes=VMEM_LIMIT),
        name="ffn",
    )(x2, ma, mb, woa, wob, gf, wg, wu, wd)


def _q_perm():
    return np.array([(g * GQA_REP + r) * HEAD_DIM + d
                     for r in range(GQA_REP) for g in range(NSA_KV_HEADS) for d in range(HEAD_DIM)])


def _rope_tables(pos):
    half = HEAD_DIM // 2
    inv = ROPE_THETA ** (-np.arange(half, dtype=np.float64) / half)
    ang = np.asarray(pos, np.float64)[:, None] * inv[None, :]
    cos = np.concatenate([np.cos(ang), np.cos(ang)], axis=1)
    sin = np.concatenate([-np.sin(ang), np.sin(ang)], axis=1)
    return (np.tile(cos, (1, NSA_KV_HEADS)).astype(np.float32),
            np.tile(sin, (1, NSA_KV_HEADS)).astype(np.float32))


def _block_diag_ones(width):
    idx = np.arange(width) // HEAD_DIM
    return (idx[:, None] == idx[None, :]).astype(np.float32)


def _overlap_t(seq):
    ncmp = (seq - CMP_LEN) // CMP_STRIDE + 1
    cs = np.arange(ncmp)[:, None] * CMP_STRIDE
    ss = np.arange(seq // SLC_BLOCK)[None, :] * SLC_BLOCK
    ov = np.clip(np.minimum(cs + CMP_LEN, ss + SLC_BLOCK) - np.maximum(cs, ss), 0, None) / CMP_LEN
    out = np.zeros((LANES, LANES), np.float32)
    out[:seq // SLC_BLOCK, :ncmp] = ov.T
    return out


def _expand_t(seq):
    key = np.arange(seq)
    e = (key[:, None] // SLC_BLOCK == np.arange(LANES)[None, :]).astype(np.float32)
    return e.reshape(seq // SLC_CHUNK, SLC_CHUNK, LANES)


def _cmp_weights(pe, w1, w2):
    pe2 = jnp.concatenate([pe, pe], axis=1)
    w1r = w1.reshape(CMP_LEN, HEAD_DIM, CMP_HIDDEN)
    z1 = jnp.zeros_like(w1r)
    w1b = jnp.concatenate([jnp.concatenate([w1r, z1], axis=2),
                           jnp.concatenate([z1, w1r], axis=2)], axis=1).astype(BF16)
    z2 = jnp.zeros_like(w2)
    w2b = jnp.concatenate([jnp.concatenate([w2, z2], axis=1),
                           jnp.concatenate([z2, w2], axis=1)], axis=0).astype(BF16)
    return pe2, w1b, w2b


def kernel(x, attn_norm_g, w_in, q_norm_g, k_norm_cmp_g, k_norm_slc_g, k_norm_win_g, cmp_pe_k, cmp_w1_k, cmp_w2_k, cmp_pe_v, cmp_w1_v, cmp_w2_v, conv_dw_w, conv_dw_b, conv_ln_g, conv_ln_b, out_norm_nsa_g, out_norm_conv_g, w_out, ffn_norm_g, w_gate_up, w_down):
    batch, seq, d_model = x.shape
    assert d_model == D_MODEL and seq % PROJ_TILE == 0 and seq // SLC_BLOCK == 32
    depth = w_in.shape[0]
    perm = _q_perm()
    cos_np, sin_np = _rope_tables(np.arange(seq))
    ccos_np, csin_np = _rope_tables(np.arange(seq // CMP_STRIDE) * CMP_STRIDE + CMP_LEN - 1)
    cos, sin, ccos, csin = map(jnp.asarray, (cos_np, sin_np, ccos_np, csin_np))
    bd256 = jnp.asarray(_block_diag_ones(256), BF16)
    bd128 = jnp.asarray(_block_diag_ones(LANES), BF16)
    ovl = jnp.asarray(_overlap_t(seq), BF16)
    et = jnp.asarray(_expand_t(seq), BF16)

    x2 = x.reshape(batch * seq, d_model)
    for l in range(depth):
        w = w_in[l]
        o_q, o_kc, o_vc, o_ks, o_vs, o_kw, o_vw, o_gl, o_cv = (
            0, 512, 640, 768, 896, 1024, 1152, 1280, 1304)
        gl_pad = jnp.zeros((d_model, LANES - 3 * NSA_HEADS), w.dtype)
        w_cat = jnp.concatenate([
            w[:, perm],
            w[:, o_ks:o_ks + KV_WIDTH], w[:, o_kw:o_kw + KV_WIDTH],
            w[:, o_vs:o_vs + KV_WIDTH], w[:, o_vw:o_vw + KV_WIDTH],
            w[:, o_kc:o_kc + KV_WIDTH], w[:, o_vc:o_vc + KV_WIDTH],
            w[:, o_cv:o_cv + CONV_CH], w[:, o_cv + CONV_CH:o_cv + 2 * CONV_CH],
            w[:, o_gl:o_gl + 3 * NSA_HEADS], gl_pad], axis=1).astype(BF16)
        gq = jnp.tile(q_norm_g[l], 4)[None, :]
        gk = jnp.concatenate([jnp.tile(k_norm_slc_g[l], 2), jnp.tile(k_norm_win_g[l], 2)])[None, :]
        q, ks, kw, vst, vwt, kc, vc, gt, hcv = _proj_call(
            x2, attn_norm_g[l][None, :], w_cat, cos, sin, gq, gk, bd256, seq)

        pek, w1k, w2k = _cmp_weights(cmp_pe_k[l], cmp_w1_k[l], cmp_w2_k[l])
        pev, w1v, w2v = _cmp_weights(cmp_pe_v[l], cmp_w1_v[l], cmp_w2_v[l])
        kcmp, vcmpt = _cmp_call(kc, vc, pek, w1k, w2k, pev, w1v, w2v, ccos, csin,
                                jnp.tile(k_norm_cmp_g[l], 2)[None, :], bd128, batch, seq)

        mix_a = _attn_call(q, ks, kw, vst, vwt, kcmp, vcmpt, gt, ovl, et,
                           out_norm_nsa_g[l][perm][None, :], batch, seq)

        w_dw = jnp.concatenate([conv_dw_w[l][:, 0, :],
                                jnp.zeros((CONV_HALO - CONV_WIDTH, CONV_CH), F32)], axis=0)
        mix_b = _conv_call(hcv, w_dw, conv_dw_b[l][None, :], conv_ln_g[l][None, :],
                           conv_ln_b[l][None, :], out_norm_conv_g[l][None, :], seq)

        wo = w_out[l]
        x2 = _ffn_call(x2, mix_a, mix_b, wo[:NSA_WIDTH][perm].astype(BF16),
                       wo[NSA_WIDTH:].astype(BF16), ffn_norm_g[l][None, :],
                       w_gate_up[l][:, :FFN_HIDDEN].astype(BF16),
                       w_gate_up[l][:, FFN_HIDDEN:].astype(BF16), w_down[l].astype(BF16))
    return x2.reshape(batch, seq, d_model)
```

```python
import functools

import numpy as np
import jax
import jax.numpy as jnp
from jax import lax
from jax.experimental import pallas as pl
from jax.experimental.pallas import tpu as pltpu

F32 = jnp.float32
BF16 = jnp.bfloat16

D_MODEL = 1024
HEAD_DIM = 64
NSA_HEADS = 8
NSA_KV_HEADS = 2
GQA_REP = NSA_HEADS // NSA_KV_HEADS
NSA_WIDTH = NSA_HEADS * HEAD_DIM
KV_WIDTH = NSA_KV_HEADS * HEAD_DIM
CONV_CH = D_MODEL - NSA_WIDTH
CMP_LEN = 32
CMP_STRIDE = 16
CMP_HIDDEN = 4 * HEAD_DIM
SLC_BLOCK = 64
SLC_TOPN = 8
WINDOW = 512
CONV_WIDTH = 31
FFN_HIDDEN = 2816
ROPE_THETA = 10000.0
EPS = 1e-6
NEG = -1e30
LOG2E = 1.4426950408889634
FORCE = 1e6

LANES = 128
Q_TILE = 128
SLC_CHUNK = 512
WIN_KEYS = WINDOW + Q_TILE
PROJ_TILE = 512
CONV_TILE = 512
CONV_HALO = 32
CONV_ROWS = 128
FFN_TILE = 512
FFN_CHUNK = 256
VMEM_LIMIT = 56 * 1024 * 1024

C_Q = 0
C_KS = 512
C_KW = 640
C_VS = 768
C_VW = 896
C_KC = 1024
C_VC = 1152
C_CA = 1280
C_CG = 1792
C_GL = 2304
PROJ_COLS = 2432

_TRANS_B = (((1,), (1,)), ((), ()))


def _dot(a, b):
    return jnp.dot(a, b, preferred_element_type=F32)


def _dot_tb(a, b):
    return lax.dot_general(a, b, _TRANS_B, preferred_element_type=F32)


def _split_bf16(x):
    hi = x.astype(BF16)
    lo = (x - hi.astype(F32)).astype(BF16)
    return hi, lo


def _sigmoid(x):
    return 1.0 / (1.0 + jnp.exp(-x))


def _head_norm_rope(y, gain, bd, cos, sin, first_half, scale):
    hi, lo = _split_bf16(y * y)
    ss = _dot(hi, bd) + _dot(lo, bd)
    yn = y * lax.rsqrt(ss * (1.0 / HEAD_DIM) + EPS) * gain
    outs = []
    for j in range(y.shape[1] // LANES):
        blk = yn[:, j * LANES:(j + 1) * LANES]
        rot = jnp.where(first_half, pltpu.roll(blk, LANES - HEAD_DIM // 2, 1),
                        pltpu.roll(blk, HEAD_DIM // 2, 1))
        outs.append((blk * cos + rot * sin) * scale)
    return outs


def _value_t_with_ones(v):
    vt = v.T
    row = lax.broadcasted_iota(jnp.int32, vt.shape, 0)
    return (jnp.where(row < HEAD_DIM, vt, 1.0).astype(BF16),
            jnp.where(row < HEAD_DIM, 1.0, vt).astype(BF16))


def _proj_body(x_ref, gin_ref, w_ref, cos_ref, sin_ref, gq_ref, gk_ref, bd_ref,
               q_ref, ks_ref, kw_ref, vs0_ref, vs1_ref, vw0_ref, vw1_ref, kc_ref, vc_ref, gt_ref,
               hcv_ref):
    x = x_ref[...]
    ms = jnp.mean(x * x, axis=-1, keepdims=True)
    xn = (x * lax.rsqrt(ms + EPS) * gin_ref[...]).astype(BF16)
    cos = cos_ref[...]
    sin = sin_ref[...]
    bd = bd_ref[...]
    lane = lax.broadcasted_iota(jnp.int32, (PROJ_TILE, LANES), 1)
    first_half = (lane & (HEAD_DIM // 2)) == 0

    for half in range(2):
        c0 = C_Q + half * 256
        qh = _dot(xn, w_ref[:, c0:c0 + 256])
        blks = _head_norm_rope(qh, gq_ref[...], bd, cos, sin, first_half, HEAD_DIM ** -0.5 * LOG2E)
        for j in range(2):
            q_ref[:, (2 * half + j) * LANES:(2 * half + j + 1) * LANES] = blks[j].astype(BF16)

    kk = _dot(xn, w_ref[:, C_KS:C_KS + 256])
    kblks = _head_norm_rope(kk, gk_ref[...], bd, cos, sin, first_half, 1.0)
    ks_ref[...] = kblks[0].astype(BF16)
    kw_ref[...] = kblks[1].astype(BF16)

    vv = _dot(xn, w_ref[:, C_VS:C_VS + 256])
    for i in range(PROJ_TILE // LANES):
        rows = vv[i * LANES:(i + 1) * LANES, :]
        vs0_ref[i], vs1_ref[i] = _value_t_with_ones(rows[:, :LANES])
        vw0_ref[i], vw1_ref[i] = _value_t_with_ones(rows[:, LANES:])

    cc = _dot(xn, w_ref[:, C_KC:C_KC + 256])
    kc_ref[...] = cc[:, :LANES]
    vc_ref[...] = cc[:, LANES:]

    ca = _dot(xn, w_ref[:, C_CA:C_CA + CONV_CH])
    cg = _dot(xn, w_ref[:, C_CG:C_CG + CONV_CH])
    hcv_ref[...] = ca * _sigmoid(cg)

    sg = _sigmoid(_dot(xn, w_ref[:, C_GL:C_GL + LANES]))
    for i in range(PROJ_TILE // LANES):
        gt_ref[i] = sg[i * LANES:(i + 1) * LANES, :].T[0:32, :]


def _proj_call(x2, gin, w_cat, cos, sin, gq, gk, bd, seq):
    t = x2.shape[0]
    nt = t // PROJ_TILE
    tiles_per_seq = seq // PROJ_TILE
    row = lambda i: (i, 0)
    const = lambda i: (0, 0)
    tab = lambda i: (i % tiles_per_seq, 0)
    blk3 = lambda i: (i, 0, 0)
    nb = PROJ_TILE // LANES
    return pl.pallas_call(
        _proj_body,
        grid=(nt,),
        in_specs=[
            pl.BlockSpec((PROJ_TILE, D_MODEL), row),
            pl.BlockSpec((1, D_MODEL), const),
            pl.BlockSpec((D_MODEL, PROJ_COLS), const),
            pl.BlockSpec((PROJ_TILE, LANES), tab),
            pl.BlockSpec((PROJ_TILE, LANES), tab),
            pl.BlockSpec((1, 256), const),
            pl.BlockSpec((1, 256), const),
            pl.BlockSpec((256, 256), const),
        ],
        out_specs=[
            pl.BlockSpec((PROJ_TILE, NSA_WIDTH), row),
            pl.BlockSpec((PROJ_TILE, LANES), row),
            pl.BlockSpec((PROJ_TILE, LANES), row),
            pl.BlockSpec((nb, LANES, LANES), blk3),
            pl.BlockSpec((nb, LANES, LANES), blk3),
            pl.BlockSpec((nb, LANES, LANES), blk3),
            pl.BlockSpec((nb, LANES, LANES), blk3),
            pl.BlockSpec((PROJ_TILE, LANES), row),
            pl.BlockSpec((PROJ_TILE, LANES), row),
            pl.BlockSpec((nb, 32, LANES), blk3),
            pl.BlockSpec((PROJ_TILE, CONV_CH), row),
        ],
        out_shape=[
            jax.ShapeDtypeStruct((t, NSA_WIDTH), BF16),
            jax.ShapeDtypeStruct((t, LANES), BF16),
            jax.ShapeDtypeStruct((t, LANES), BF16),
            jax.ShapeDtypeStruct((t // LANES, LANES, LANES), BF16),
            jax.ShapeDtypeStruct((t // LANES, LANES, LANES), BF16),
            jax.ShapeDtypeStruct((t // LANES, LANES, LANES), BF16),
            jax.ShapeDtypeStruct((t // LANES, LANES, LANES), BF16),
            jax.ShapeDtypeStruct((t, LANES), F32),
            jax.ShapeDtypeStruct((t, LANES), F32),
            jax.ShapeDtypeStruct((t // LANES, 32, LANES), F32),
            jax.ShapeDtypeStruct((t, CONV_CH), F32),
        ],
        compiler_params=pltpu.CompilerParams(
            dimension_semantics=("parallel",), vmem_limit_bytes=VMEM_LIMIT),
        name="proj",
    )(x2, gin, w_cat, cos, sin, gq, gk, bd)


def _cmp_mlp(t_ref, pe_ref, w1_ref, w2_ref):
    half = CMP_LEN // 2
    nchunk = t_ref.shape[0] // CMP_STRIDE
    p = jnp.zeros((nchunk, 2 * CMP_HIDDEN), F32)
    q = jnp.zeros((nchunk, 2 * CMP_HIDDEN), F32)
    for l in range(half):
        xl = t_ref[pl.ds(l, nchunk, stride=CMP_STRIDE), :]
        p = p + _dot((xl + pe_ref[l:l + 1, :]).astype(BF16), w1_ref[l])
        q = q + _dot((xl + pe_ref[half + l:half + l + 1, :]).astype(BF16), w1_ref[half + l])
    h1 = p + pltpu.roll(q, nchunk - 1, 0)
    act = (h1 * _sigmoid(h1)).astype(BF16)
    return _dot(act, w2_ref[...])


def _cmp_body(kc_ref, vc_ref, pek_ref, w1k_ref, w2k_ref, pev_ref, w1v_ref, w2v_ref,
              cos_ref, sin_ref, gk_ref, bd_ref, ko_ref, vt0_ref, vt1_ref):
    kraw = _cmp_mlp(kc_ref, pek_ref, w1k_ref, w2k_ref)
    lane = lax.broadcasted_iota(jnp.int32, kraw.shape, 1)
    first_half = (lane & (HEAD_DIM // 2)) == 0
    (kr,) = _head_norm_rope(kraw, gk_ref[...], bd_ref[...], cos_ref[...], sin_ref[...],
                            first_half, 1.0)
    ko_ref[...] = kr.astype(BF16)
    vraw = _cmp_mlp(vc_ref, pev_ref, w1v_ref, w2v_ref)
    vt0_ref[0], vt1_ref[0] = _value_t_with_ones(vraw)


def _cmp_call(kc, vc, pek, w1k, w2k, pev, w1v, w2v, cos, sin, gk, bd, batch, seq):
    seqblk = lambda b: (b, 0)
    c2 = lambda b: (0, 0)
    c3 = lambda b: (0, 0, 0)
    ncmp_pad = seq // CMP_STRIDE
    return pl.pallas_call(
        _cmp_body,
        grid=(batch,),
        in_specs=[
            pl.BlockSpec((seq, LANES), seqblk),
            pl.BlockSpec((seq, LANES), seqblk),
            pl.BlockSpec((CMP_LEN, LANES), c2),
            pl.BlockSpec((CMP_LEN, LANES, 2 * CMP_HIDDEN), c3),
            pl.BlockSpec((2 * CMP_HIDDEN, LANES), c2),
            pl.BlockSpec((CMP_LEN, LANES), c2),
            pl.BlockSpec((CMP_LEN, LANES, 2 * CMP_HIDDEN), c3),
            pl.BlockSpec((2 * CMP_HIDDEN, LANES), c2),
            pl.BlockSpec((ncmp_pad, LANES), c2),
            pl.BlockSpec((ncmp_pad, LANES), c2),
            pl.BlockSpec((1, LANES), c2),
            pl.BlockSpec((LANES, LANES), c2),
        ],
        out_specs=[
            pl.BlockSpec((ncmp_pad, LANES), seqblk),
            pl.BlockSpec((1, LANES, ncmp_pad), lambda b: (b, 0, 0)),
            pl.BlockSpec((1, LANES, ncmp_pad), lambda b: (b, 0, 0)),
        ],
        out_shape=[
            jax.ShapeDtypeStruct((batch * ncmp_pad, LANES), BF16),
            jax.ShapeDtypeStruct((batch, LANES, ncmp_pad), BF16),
            jax.ShapeDtypeStruct((batch, LANES, ncmp_pad), BF16),
        ],
        compiler_params=pltpu.CompilerParams(
            dimension_semantics=("parallel",), vmem_limit_bytes=VMEM_LIMIT),
        name="compress",
    )(kc, vc, pek, w1k, w2k, pev, w1v, w2v, cos, sin, gk, bd)


def _lane_cat(blocks):
    return jnp.concatenate(blocks, axis=1)


def _softmax_numer(s, bias, m_run=None):
    es, ms = [], []
    for r in range(GQA_REP):
        sm = s[:, r * LANES:(r + 1) * LANES] + bias
        m = jnp.max(sm, axis=0, keepdims=True)
        if m_run is not None:
            m = jnp.maximum(m, m_run[:, r * LANES:(r + 1) * LANES])
        es.append(jnp.exp2(sm - m))
        ms.append(m)
    return _lane_cat(es), _lane_cat(ms)


def _safe_inv(l):
    return 1.0 / jnp.where(l > 0.0, l, 1.0)


def _attn_body(q_ref, ks_ref, kw_ref, vs0_ref, vs1_ref, vw0_ref, vw1_ref, kc_ref, vc0_ref, vc1_ref,
               gt_ref, ovl_ref, et_ref, gn_ref, out_ref, acc_ref):
    groups = range(NSA_KV_HEADS)
    vs_refs, vw_refs, vc_refs = (vs0_ref, vs1_ref), (vw0_ref, vw1_ref), (vc0_ref, vc1_ref)
    sum_row = (HEAD_DIM, 0)
    width = GQA_REP * LANES
    c = pl.program_id(1)
    t_lane = c * Q_TILE + lax.broadcasted_iota(jnp.int32, (1, LANES), 1)
    lane_sq = lax.broadcasted_iota(jnp.int32, (Q_TILE, LANES), 1)
    nblk = ovl_ref.shape[0]
    nsel = 32
    j_idx = lax.broadcasted_iota(jnp.int32, (nsel, LANES), 0)
    cur = lax.shift_right_logical(t_lane, 6)
    n_idx = lax.broadcasted_iota(jnp.int32, (LANES, LANES), 0)
    cbias = jnp.where((n_idx * CMP_STRIDE + (CMP_LEN - 1) <= t_lane) & (n_idx < LANES - 1), 0.0, NEG)
    has_cmp = jnp.where(t_lane >= CMP_LEN - 1, 1.0, 0.0)
    has_cmp = _lane_cat([has_cmp] * GQA_REP)
    win_blk = jnp.maximum(c - WINDOW // Q_TILE, 0)
    win_start = pl.multiple_of(win_blk * Q_TILE, Q_TILE)
    kpos_w = win_start + lax.broadcasted_iota(jnp.int32, (WIN_KEYS, LANES), 0)
    wbias = jnp.where((kpos_w <= t_lane) & (kpos_w > t_lane - WINDOW), 0.0, NEG)
    row_sl = lax.broadcasted_iota(jnp.int32, (SLC_CHUNK, LANES), 0)
    zero_q = jnp.zeros((Q_TILE, LANES), BF16)

    def gate_row(g, branch):
        rows = [(g * GQA_REP + r) * 3 + branch for r in range(GQA_REP)]
        return _lane_cat([gt_ref[0, i:i + 1, :] for i in rows])

    qg = []
    for g in groups:
        in_g = (lane_sq >= HEAD_DIM) if g else (lane_sq < HEAD_DIM)
        qg.append(jnp.concatenate(
            [jnp.where(in_g, q_ref[:, r * LANES:(r + 1) * LANES], zero_q) for r in range(GQA_REP)],
            axis=0))

    o_cmp, sel_bias = [], []
    for g in groups:
        sc = _dot_tb(kc_ref[...], qg[g])
        ec, _ = _softmax_numer(sc, cbias)
        coef = has_cmp * _safe_inv(jnp.sum(ec, axis=0, keepdims=True))
        o_cmp.append(_dot(vc_refs[g][0], ec.astype(BF16)) * (gate_row(g, 0) * coef))
        pc = ec * coef
        psum = (pc[:, 0:LANES] + pc[:, LANES:2 * LANES]
                + pc[:, 2 * LANES:3 * LANES] + pc[:, 3 * LANES:4 * LANES])
        p_hi, p_lo = _split_bf16(psum)
        imp = (_dot(ovl_ref[...], p_hi) + _dot(ovl_ref[...], p_lo))[0:nsel, :]
        valid = j_idx <= cur
        forced = (j_idx == 0) | (j_idx == cur) | (j_idx == cur - 1)
        score = jnp.where(valid, imp + jnp.where(forced, FORCE, 0.0), -FORCE)
        rank = jnp.zeros((nsel, LANES), F32)
        for jp in range(nsel):
            row = score[jp:jp + 1, :]
            beats = (row > score) | ((row == score) & (j_idx > jp))
            rank = rank + jnp.where(beats, 1.0, 0.0)
        unsel = jnp.where((rank < float(SLC_TOPN)) & (score > -1.0), 0.0, -1.0)
        sel_bias.append(jnp.concatenate(
            [unsel, jnp.zeros((nblk - nsel, LANES), F32)], axis=0).astype(BF16))
    sel_bias = _lane_cat(sel_bias)

    acc_ref[...] = jnp.zeros_like(acc_ref)

    def slc_step(kc, m_run):
        k0 = pl.multiple_of(kc * SLC_CHUNK, SLC_CHUNK)
        kblk = ks_ref[pl.ds(k0, SLC_CHUNK), :]
        causal = k0 + row_sl <= t_lane
        pick = _dot(et_ref[kc], sel_bias)
        nsub = SLC_CHUNK // LANES
        m_out = []
        for g in groups:
            bias = jnp.where(causal, pick[:, g * LANES:(g + 1) * LANES], NEG)
            e, m_new = _softmax_numer(_dot_tb(kblk, qg[g]), bias, m_run[g])
            alpha = jnp.exp2(m_run[g] - m_new)
            vt = _lane_cat([vs_refs[g][nsub * kc + i] for i in range(nsub)])
            acc_ref[g] = acc_ref[g] * alpha + _dot(vt, e.astype(BF16))
            m_out.append(m_new)
        return tuple(m_out)

    n_chunks = (c * Q_TILE) // SLC_CHUNK + 1
    lax.fori_loop(0, n_chunks, slc_step, tuple(jnp.full((1, width), NEG, F32) for _ in groups))

    kwin = kw_ref[pl.ds(win_start, WIN_KEYS), :]
    o_groups = []
    for g in groups:
        ew, _ = _softmax_numer(_dot_tb(kwin, qg[g]), wbias)
        vt = _lane_cat([vw_refs[g][win_blk + i] for i in range(WIN_KEYS // LANES)])
        ow = _dot(vt, ew.astype(BF16))
        os_ = acc_ref[g]
        r0 = sum_row[g]
        o_groups.append(o_cmp[g]
                        + os_ * (gate_row(g, 1) * _safe_inv(os_[r0:r0 + 1, :]))
                        + ow * (gate_row(g, 2) * _safe_inv(ow[r0:r0 + 1, :])))

    row_d = lax.broadcasted_iota(jnp.int32, (LANES, width), 0)
    o_t = jnp.where(row_d < HEAD_DIM, o_groups[0], o_groups[1])
    blks = [o_t[:, r * LANES:(r + 1) * LANES].T for r in range(GQA_REP)]
    ss = blks[0] * blks[0]
    for r in range(1, GQA_REP):
        ss = ss + blks[r] * blks[r]
    rs = lax.rsqrt(jnp.sum(ss, axis=-1, keepdims=True) * (1.0 / NSA_WIDTH) + EPS)
    for r in range(GQA_REP):
        out_ref[:, r * LANES:(r + 1) * LANES] = (
            blks[r] * rs * gn_ref[:, r * LANES:(r + 1) * LANES]).astype(BF16)


def _attn_call(q, ks, kw, vs0, vs1, vw0, vw1, kcmp, vc0, vc1, gt, ovl, et, gn, batch, seq):
    nq = seq // Q_TILE
    nkb = seq // LANES
    qrow = lambda b, c: (b * nq + c, 0)
    per_b = lambda b, c: (b, 0)
    per_b3 = lambda b, c: (b, 0, 0)
    vspec = pl.BlockSpec((nkb, LANES, LANES), per_b3)
    cspec = pl.BlockSpec((1, LANES, LANES), per_b3)
    return pl.pallas_call(
        _attn_body,
        grid=(batch, nq),
        in_specs=[
            pl.BlockSpec((Q_TILE, NSA_WIDTH), qrow),
            pl.BlockSpec((seq, LANES), per_b),
            pl.BlockSpec((seq, LANES), per_b),
            vspec, vspec, vspec, vspec,
            pl.BlockSpec((LANES, LANES), per_b),
            cspec, cspec,
            pl.BlockSpec((1, 32, LANES), lambda b, c: (b * nq + c, 0, 0)),
            pl.BlockSpec((LANES, LANES), lambda b, c: (0, 0)),
            pl.BlockSpec((seq // SLC_CHUNK, SLC_CHUNK, LANES), lambda b, c: (0, 0, 0)),
            pl.BlockSpec((1, NSA_WIDTH), lambda b, c: (0, 0)),
        ],
        out_specs=pl.BlockSpec((Q_TILE, NSA_WIDTH), qrow),
        out_shape=jax.ShapeDtypeStruct((batch * seq, NSA_WIDTH), BF16),
        scratch_shapes=[pltpu.VMEM((NSA_KV_HEADS, LANES, GQA_REP * LANES), F32)],
        compiler_params=pltpu.CompilerParams(
            dimension_semantics=("parallel", "arbitrary"), vmem_limit_bytes=VMEM_LIMIT),
        name="attn",
    )(q, ks, kw, vs0, vs1, vw0, vw1, kcmp, vc0, vc1, gt, ovl, et, gn)


def _conv_body(tiles_per_seq, cur_ref, halo_ref, w_ref, b_ref, lng_ref, lnb_ref, gn_ref, out_ref,
               buf_ref, sh_ref, cv_ref):
    first = (pl.program_id(0) % tiles_per_seq) == 0
    buf_ref[0:CONV_HALO, :] = jnp.where(first, 0.0, halo_ref[...])
    buf_ref[CONV_HALO:, :] = cur_ref[...]
    lead = CONV_HALO - (CONV_WIDTH - 1)
    ncb = CONV_CH // LANES
    nrb = CONV_TILE // CONV_ROWS
    for s in range(8):
        span = CONV_TILE + 8 * ((CONV_WIDTH - 1 - s) // 8)
        for cb in range(ncb):
            sh_ref[s, cb, 0:span, :] = buf_ref[lead + s:lead + s + span, cb * LANES:(cb + 1) * LANES]

    def conv_block(i, carry):
        cb = i % ncb
        base = pl.multiple_of((i // ncb) * CONV_ROWS, CONV_ROWS)
        acc = jnp.zeros((CONV_ROWS, LANES), F32) + b_ref[cb]
        for k in range(CONV_WIDTH):
            acc = acc + w_ref[cb, k:k + 1, :] * sh_ref[k % 8, cb, pl.ds(base + 8 * (k // 8), CONV_ROWS), :]
        cv_ref[cb, pl.ds(base, CONV_ROWS), :] = acc
        return carry

    lax.fori_loop(0, nrb * ncb, conv_block, 0)
    h = _lane_cat([cv_ref[cb] for cb in range(ncb)])
    mu = jnp.mean(h, axis=-1, keepdims=True)
    d = h - mu
    var = jnp.mean(d * d, axis=-1, keepdims=True)
    hn = d * lax.rsqrt(var + EPS) * lng_ref[...] + lnb_ref[...]
    o = hn * _sigmoid(hn)
    ms = jnp.mean(o * o, axis=-1, keepdims=True)
    out_ref[...] = (o * lax.rsqrt(ms + EPS) * gn_ref[...]).astype(BF16)


def _conv_call(hcv, w, b, lng, lnb, gn, seq):
    t = hcv.shape[0]
    nt = t // CONV_TILE
    halo_per_tile = CONV_TILE // CONV_HALO
    ncb = CONV_CH // LANES
    row = lambda i: (i, 0)
    const = lambda i: (0, 0)
    return pl.pallas_call(
        functools.partial(_conv_body, seq // CONV_TILE),
        grid=(nt,),
        in_specs=[
            pl.BlockSpec((CONV_TILE, CONV_CH), row),
            pl.BlockSpec((CONV_HALO, CONV_CH), lambda i: (jnp.maximum(i * halo_per_tile - 1, 0), 0)),
            pl.BlockSpec((ncb, CONV_HALO, LANES), lambda i: (0, 0, 0)),
            pl.BlockSpec((ncb, 1, LANES), lambda i: (0, 0, 0)),
            pl.BlockSpec((1, CONV_CH), const),
            pl.BlockSpec((1, CONV_CH), const),
            pl.BlockSpec((1, CONV_CH), const),
        ],
        out_specs=pl.BlockSpec((CONV_TILE, CONV_CH), row),
        out_shape=jax.ShapeDtypeStruct((t, CONV_CH), BF16),
        scratch_shapes=[pltpu.VMEM((CONV_HALO + CONV_TILE, CONV_CH), F32),
                        pltpu.VMEM((8, ncb, CONV_TILE + 8 * ((CONV_WIDTH - 1) // 8), LANES), F32),
                        pltpu.VMEM((ncb, CONV_TILE, LANES), F32)],
        compiler_params=pltpu.CompilerParams(
            dimension_semantics=("parallel",), vmem_limit_bytes=VMEM_LIMIT),
        name="conv",
    )(hcv, hcv, w, b, lng, lnb, gn)


def _ffn_body(x_ref, ma_ref, mb_ref, woa_ref, wob_ref, gf_ref, wg_ref, wu_ref, wd_ref, out_ref):
    h = x_ref[...] + _dot(ma_ref[...], woa_ref[...]) + _dot(mb_ref[...], wob_ref[...])
    ms = jnp.mean(h * h, axis=-1, keepdims=True)
    hn = (h * lax.rsqrt(ms + EPS) * gf_ref[...]).astype(BF16)
    acc = jnp.zeros_like(h)
    for j in range(FFN_HIDDEN // FFN_CHUNK):
        cols = slice(j * FFN_CHUNK, (j + 1) * FFN_CHUNK)
        gte = _dot(hn, wg_ref[:, cols])
        up = _dot(hn, wu_ref[:, cols])
        act = (gte * _sigmoid(gte) * up).astype(BF16)
        acc = acc + _dot(act, wd_ref[cols, :])
    out_ref[...] = h + acc


def _ffn_call(x2, ma, mb, woa, wob, gf, wg, wu, wd):
    t = x2.shape[0]
    nt = t // FFN_TILE
    row = lambda i: (i, 0)
    const = lambda i: (0, 0)
    once = pl.Buffered(1)
    return pl.pallas_call(
        _ffn_body,
        grid=(nt,),
        in_specs=[
            pl.BlockSpec((FFN_TILE, D_MODEL), row),
            pl.BlockSpec((FFN_TILE, NSA_WIDTH), row),
            pl.BlockSpec((FFN_TILE, CONV_CH), row),
            pl.BlockSpec((NSA_WIDTH, D_MODEL), const, pipeline_mode=once),
            pl.BlockSpec((CONV_CH, D_MODEL), const, pipeline_mode=once),
            pl.BlockSpec((1, D_MODEL), const),
            pl.BlockSpec((D_MODEL, FFN_HIDDEN), const, pipeline_mode=once),
            pl.BlockSpec((D_MODEL, FFN_HIDDEN), const, pipeline_mode=once),
            pl.BlockSpec((FFN_HIDDEN, D_MODEL), const, pipeline_mode=once),
        ],
        out_specs=pl.BlockSpec((FFN_TILE, D_MODEL), row),
        out_shape=jax.ShapeDtypeStruct((t, D_MODEL), F32),
        compiler_params=pltpu.CompilerParams(
            dimension_semantics=("parallel",), vmem_limit_bytes=VMEM_LIMIT),
        name="ffn",
    )(x2, ma, mb, woa, wob, gf, wg, wu, wd)


def _q_perm():
    return np.array([(g * GQA_REP + r) * HEAD_DIM + d
                     for r in range(GQA_REP) for g in range(NSA_KV_HEADS) for d in range(HEAD_DIM)])


def _rope_tables(pos):
    half = HEAD_DIM // 2
    inv = ROPE_THETA ** (-np.arange(half, dtype=np.float64) / half)
    ang = np.asarray(pos, np.float64)[:, None] * inv[None, :]
    cos = np.concatenate([np.cos(ang), np.cos(ang)], axis=1)
    sin = np.concatenate([-np.sin(ang), np.sin(ang)], axis=1)
    return (np.tile(cos, (1, NSA_KV_HEADS)).astype(np.float32),
            np.tile(sin, (1, NSA_KV_HEADS)).astype(np.float32))


def _block_diag_ones(width):
    idx = np.arange(width) // HEAD_DIM
    return (idx[:, None] == idx[None, :]).astype(np.float32)


def _overlap_t(seq):
    ncmp = (seq - CMP_LEN) // CMP_STRIDE + 1
    cs = np.arange(ncmp)[:, None] * CMP_STRIDE
    ss = np.arange(seq // SLC_BLOCK)[None, :] * SLC_BLOCK
    ov = np.clip(np.minimum(cs + CMP_LEN, ss + SLC_BLOCK) - np.maximum(cs, ss), 0, None) / CMP_LEN
    out = np.zeros((LANES, LANES), np.float32)
    out[:seq // SLC_BLOCK, :ncmp] = ov.T
    return out


def _expand_t(seq):
    key = np.arange(seq)
    e = (key[:, None] // SLC_BLOCK == np.arange(LANES)[None, :]).astype(np.float32) * -NEG
    return e.reshape(seq // SLC_CHUNK, SLC_CHUNK, LANES)


def _cmp_weights(pe, w1, w2):
    pe2 = jnp.concatenate([pe, pe], axis=1)
    w1r = w1.reshape(CMP_LEN, HEAD_DIM, CMP_HIDDEN)
    z1 = jnp.zeros_like(w1r)
    w1b = jnp.concatenate([jnp.concatenate([w1r, z1], axis=2),
                           jnp.concatenate([z1, w1r], axis=2)], axis=1).astype(BF16)
    z2 = jnp.zeros_like(w2)
    w2b = jnp.concatenate([jnp.concatenate([w2, z2], axis=1),
                           jnp.concatenate([z2, w2], axis=1)], axis=0).astype(BF16)
    return pe2, w1b, w2b


def kernel(x, attn_norm_g, w_in, q_norm_g, k_norm_cmp_g, k_norm_slc_g, k_norm_win_g, cmp_pe_k, cmp_w1_k, cmp_w2_k, cmp_pe_v, cmp_w1_v, cmp_w2_v, conv_dw_w, conv_dw_b, conv_ln_g, conv_ln_b, out_norm_nsa_g, out_norm_conv_g, w_out, ffn_norm_g, w_gate_up, w_down):
    batch, seq, d_model = x.shape
    assert d_model == D_MODEL and seq % PROJ_TILE == 0 and seq // SLC_BLOCK == 32
    depth = w_in.shape[0]
    perm = _q_perm()
    cos_np, sin_np = _rope_tables(np.arange(seq))
    ccos_np, csin_np = _rope_tables(np.arange(seq // CMP_STRIDE) * CMP_STRIDE + CMP_LEN - 1)
    cos, sin, ccos, csin = map(jnp.asarray, (cos_np, sin_np, ccos_np, csin_np))
    bd256 = jnp.asarray(_block_diag_ones(256), BF16)
    bd128 = jnp.asarray(_block_diag_ones(LANES), BF16)
    ovl = jnp.asarray(_overlap_t(seq), BF16)
    et = jnp.asarray(_expand_t(seq), BF16)

    x2 = x.reshape(batch * seq, d_model)
    for l in range(depth):
        w = w_in[l]
        o_q, o_kc, o_vc, o_ks, o_vs, o_kw, o_vw, o_gl, o_cv = (
            0, 512, 640, 768, 896, 1024, 1152, 1280, 1304)
        gl_pad = jnp.zeros((d_model, LANES - 3 * NSA_HEADS), w.dtype)
        w_cat = jnp.concatenate([
            w[:, perm],
            w[:, o_ks:o_ks + KV_WIDTH], w[:, o_kw:o_kw + KV_WIDTH],
            w[:, o_vs:o_vs + KV_WIDTH], w[:, o_vw:o_vw + KV_WIDTH],
            w[:, o_kc:o_kc + KV_WIDTH], w[:, o_vc:o_vc + KV_WIDTH],
            w[:, o_cv:o_cv + CONV_CH], w[:, o_cv + CONV_CH:o_cv + 2 * CONV_CH],
            w[:, o_gl:o_gl + 3 * NSA_HEADS], gl_pad], axis=1).astype(BF16)
        gq = jnp.tile(q_norm_g[l], 4)[None, :]
        gk = jnp.concatenate([jnp.tile(k_norm_slc_g[l], 2), jnp.tile(k_norm_win_g[l], 2)])[None, :]
        q, ks, kw, vs0, vs1, vw0, vw1, kc, vc, gt, hcv = _proj_call(
            x2, attn_norm_g[l][None, :], w_cat, cos, sin, gq, gk, bd256, seq)

        pek, w1k, w2k = _cmp_weights(cmp_pe_k[l], cmp_w1_k[l], cmp_w2_k[l])
        pev, w1v, w2v = _cmp_weights(cmp_pe_v[l], cmp_w1_v[l], cmp_w2_v[l])
        kcmp, vc0, vc1 = _cmp_call(kc, vc, pek, w1k, w2k, pev, w1v, w2v, ccos, csin,
                                jnp.tile(k_norm_cmp_g[l], 2)[None, :], bd128, batch, seq)

        mix_a = _attn_call(q, ks, kw, vs0, vs1, vw0, vw1, kcmp, vc0, vc1, gt, ovl, et,
                           out_norm_nsa_g[l][perm][None, :], batch, seq)

        w_dw = jnp.concatenate([conv_dw_w[l][:, 0, :],
                                jnp.zeros((CONV_HALO - CONV_WIDTH, CONV_CH), F32)], axis=0)
        w_dw = w_dw.reshape(CONV_HALO, CONV_CH // LANES, LANES).transpose(1, 0, 2)
        mix_b = _conv_call(hcv, w_dw, conv_dw_b[l].reshape(CONV_CH // LANES, 1, LANES),
                           conv_ln_g[l][None, :],
                           conv_ln_b[l][None, :], out_norm_conv_g[l][None, :], seq)

        wo = w_out[l]
        x2 = _ffn_call(x2, mix_a, mix_b, wo[:NSA_WIDTH][perm].astype(BF16),
                       wo[NSA_WIDTH:].astype(BF16), ffn_norm_g[l][None, :],
                       w_gate_up[l][:, :FFN_HIDDEN].astype(BF16),
                       w_gate_up[l][:, FFN_HIDDEN:].astype(BF16), w_down[l].astype(BF16))
    return x2.reshape(batch, seq, d_model)
```

```python
import functools

import numpy as np
import jax
import jax.numpy as jnp
from jax import lax
from jax.experimental import pallas as pl
from jax.experimental.pallas import tpu as pltpu

F32 = jnp.float32
BF16 = jnp.bfloat16

D_MODEL = 1024
HEAD_DIM = 64
NSA_HEADS = 8
NSA_KV_HEADS = 2
GQA_REP = NSA_HEADS // NSA_KV_HEADS
NSA_WIDTH = NSA_HEADS * HEAD_DIM
KV_WIDTH = NSA_KV_HEADS * HEAD_DIM
CONV_CH = D_MODEL - NSA_WIDTH
CMP_LEN = 32
CMP_STRIDE = 16
CMP_HIDDEN = 4 * HEAD_DIM
SLC_BLOCK = 64
SLC_TOPN = 8
WINDOW = 512
CONV_WIDTH = 31
FFN_HIDDEN = 2816
ROPE_THETA = 10000.0
EPS = 1e-6
NEG = -1e30
LOG2E = 1.4426950408889634
FORCE = 1e6

LANES = 128
Q_TILE = 128
SLC_CHUNK = 512
WIN_KEYS = WINDOW + Q_TILE
PROJ_TILE = 512
CONV_TILE = 512
CONV_HALO = 32
CONV_ROWS = 128
FFN_TILE = 512
FFN_CHUNK = 256
VMEM_LIMIT = 56 * 1024 * 1024

C_Q = 0
C_KS = 512
C_KW = 640
C_VS = 768
C_VW = 896
C_KC = 1024
C_VC = 1152
C_CA = 1280
C_CG = 1792
C_GL = 2304
PROJ_COLS = 2432

_TRANS_B = (((1,), (1,)), ((), ()))


def _dot(a, b):
    return jnp.dot(a, b, preferred_element_type=F32)


def _dot_tb(a, b):
    return lax.dot_general(a, b, _TRANS_B, preferred_element_type=F32)


def _split_bf16(x):
    hi = x.astype(BF16)
    lo = (x - hi.astype(F32)).astype(BF16)
    return hi, lo


def _sigmoid(x):
    return 1.0 / (1.0 + jnp.exp(-x))


def _head_norm_rope(y, gain, bd, cos, sin, first_half, scale):
    hi, lo = _split_bf16(y * y)
    ss = _dot(hi, bd) + _dot(lo, bd)
    yn = y * lax.rsqrt(ss * (1.0 / HEAD_DIM) + EPS) * gain
    outs = []
    for j in range(y.shape[1] // LANES):
        blk = yn[:, j * LANES:(j + 1) * LANES]
        rot = jnp.where(first_half, pltpu.roll(blk, LANES - HEAD_DIM // 2, 1),
                        pltpu.roll(blk, HEAD_DIM // 2, 1))
        outs.append((blk * cos + rot * sin) * scale)
    return outs


def _value_t_with_ones(v):
    vt = v.T
    row = lax.broadcasted_iota(jnp.int32, vt.shape, 0)
    return (jnp.where(row < HEAD_DIM, vt, 1.0).astype(BF16),
            jnp.where(row < HEAD_DIM, 1.0, vt).astype(BF16))


def _proj_body(x_ref, gin_ref, w_ref, cos_ref, sin_ref, gq_ref, gk_ref, bd_ref,
               q_ref, ks_ref, kw_ref, vs0_ref, vs1_ref, vw0_ref, vw1_ref, kc_ref, vc_ref, gt_ref,
               hcv_ref):
    x = x_ref[...]
    ms = jnp.mean(x * x, axis=-1, keepdims=True)
    xn = (x * lax.rsqrt(ms + EPS) * gin_ref[...]).astype(BF16)
    cos = cos_ref[...]
    sin = sin_ref[...]
    bd = bd_ref[...]
    lane = lax.broadcasted_iota(jnp.int32, (PROJ_TILE, LANES), 1)
    first_half = (lane & (HEAD_DIM // 2)) == 0

    for half in range(2):
        c0 = C_Q + half * 256
        qh = _dot(xn, w_ref[:, c0:c0 + 256])
        blks = _head_norm_rope(qh, gq_ref[...], bd, cos, sin, first_half, HEAD_DIM ** -0.5 * LOG2E)
        for j in range(2):
            q_ref[:, (2 * half + j) * LANES:(2 * half + j + 1) * LANES] = blks[j].astype(BF16)

    kk = _dot(xn, w_ref[:, C_KS:C_KS + 256])
    kblks = _head_norm_rope(kk, gk_ref[...], bd, cos, sin, first_half, 1.0)
    ks_ref[...] = kblks[0].astype(BF16)
    kw_ref[...] = kblks[1].astype(BF16)

    vv = _dot(xn, w_ref[:, C_VS:C_VS + 256])
    for i in range(PROJ_TILE // LANES):
        rows = vv[i * LANES:(i + 1) * LANES, :]
        vs0_ref[i], vs1_ref[i] = _value_t_with_ones(rows[:, :LANES])
        vw0_ref[i], vw1_ref[i] = _value_t_with_ones(rows[:, LANES:])

    cc = _dot(xn, w_ref[:, C_KC:C_KC + 256])
    kc_ref[...] = cc[:, :LANES]
    vc_ref[...] = cc[:, LANES:]

    ca = _dot(xn, w_ref[:, C_CA:C_CA + CONV_CH])
    cg = _dot(xn, w_ref[:, C_CG:C_CG + CONV_CH])
    hcv_ref[...] = ca * _sigmoid(cg)

    sg = _sigmoid(_dot(xn, w_ref[:, C_GL:C_GL + LANES]))
    for i in range(PROJ_TILE // LANES):
        gt_ref[i] = sg[i * LANES:(i + 1) * LANES, :].T[0:32, :]


def _proj_call(x2, gin, w_cat, cos, sin, gq, gk, bd, seq):
    t = x2.shape[0]
    nt = t // PROJ_TILE
    tiles_per_seq = seq // PROJ_TILE
    row = lambda i: (i, 0)
    const = lambda i: (0, 0)
    tab = lambda i: (i % tiles_per_seq, 0)
    blk3 = lambda i: (i, 0, 0)
    nb = PROJ_TILE // LANES
    return pl.pallas_call(
        _proj_body,
        grid=(nt,),
        in_specs=[
            pl.BlockSpec((PROJ_TILE, D_MODEL), row),
            pl.BlockSpec((1, D_MODEL), const),
            pl.BlockSpec((D_MODEL, PROJ_COLS), const),
            pl.BlockSpec((PROJ_TILE, LANES), tab),
            pl.BlockSpec((PROJ_TILE, LANES), tab),
            pl.BlockSpec((1, 256), const),
            pl.BlockSpec((1, 256), const),
            pl.BlockSpec((256, 256), const),
        ],
        out_specs=[
            pl.BlockSpec((PROJ_TILE, NSA_WIDTH), row),
            pl.BlockSpec((PROJ_TILE, LANES), row),
            pl.BlockSpec((PROJ_TILE, LANES), row),
            pl.BlockSpec((nb, LANES, LANES), blk3),
            pl.BlockSpec((nb, LANES, LANES), blk3),
            pl.BlockSpec((nb, LANES, LANES), blk3),
            pl.BlockSpec((nb, LANES, LANES), blk3),
            pl.BlockSpec((PROJ_TILE, LANES), row),
            pl.BlockSpec((PROJ_TILE, LANES), row),
            pl.BlockSpec((nb, 32, LANES), blk3),
            pl.BlockSpec((PROJ_TILE, CONV_CH), row),
        ],
        out_shape=[
            jax.ShapeDtypeStruct((t, NSA_WIDTH), BF16),
            jax.ShapeDtypeStruct((t, LANES), BF16),
            jax.ShapeDtypeStruct((t, LANES), BF16),
            jax.ShapeDtypeStruct((t // LANES, LANES, LANES), BF16),
            jax.ShapeDtypeStruct((t // LANES, LANES, LANES), BF16),
            jax.ShapeDtypeStruct((t // LANES, LANES, LANES), BF16),
            jax.ShapeDtypeStruct((t // LANES, LANES, LANES), BF16),
            jax.ShapeDtypeStruct((t, LANES), F32),
            jax.ShapeDtypeStruct((t, LANES), F32),
            jax.ShapeDtypeStruct((t // LANES, 32, LANES), F32),
            jax.ShapeDtypeStruct((t, CONV_CH), F32),
        ],
        compiler_params=pltpu.CompilerParams(
            dimension_semantics=("parallel",), vmem_limit_bytes=VMEM_LIMIT),
        name="proj",
    )(x2, gin, w_cat, cos, sin, gq, gk, bd)


def _cmp_mlp(t_ref, pe_ref, w1_ref, w2_ref):
    half = CMP_LEN // 2
    nchunk = t_ref.shape[0] // CMP_STRIDE
    p = jnp.zeros((nchunk, 2 * CMP_HIDDEN), F32)
    q = jnp.zeros((nchunk, 2 * CMP_HIDDEN), F32)
    for l in range(half):
        xl = t_ref[pl.ds(l, nchunk, stride=CMP_STRIDE), :]
        p = p + _dot((xl + pe_ref[l:l + 1, :]).astype(BF16), w1_ref[l])
        q = q + _dot((xl + pe_ref[half + l:half + l + 1, :]).astype(BF16), w1_ref[half + l])
    h1 = p + pltpu.roll(q, nchunk - 1, 0)
    act = (h1 * _sigmoid(h1)).astype(BF16)
    return _dot(act, w2_ref[...])


def _cmp_body(kc_ref, vc_ref, pek_ref, w1k_ref, w2k_ref, pev_ref, w1v_ref, w2v_ref,
              cos_ref, sin_ref, gk_ref, bd_ref, ko_ref, vt0_ref, vt1_ref):
    kraw = _cmp_mlp(kc_ref, pek_ref, w1k_ref, w2k_ref)
    lane = lax.broadcasted_iota(jnp.int32, kraw.shape, 1)
    first_half = (lane & (HEAD_DIM // 2)) == 0
    (kr,) = _head_norm_rope(kraw, gk_ref[...], bd_ref[...], cos_ref[...], sin_ref[...],
                            first_half, 1.0)
    ko_ref[...] = kr.astype(BF16)
    vraw = _cmp_mlp(vc_ref, pev_ref, w1v_ref, w2v_ref)
    vt0_ref[0], vt1_ref[0] = _value_t_with_ones(vraw)


def _cmp_call(kc, vc, pek, w1k, w2k, pev, w1v, w2v, cos, sin, gk, bd, batch, seq):
    seqblk = lambda b: (b, 0)
    c2 = lambda b: (0, 0)
    c3 = lambda b: (0, 0, 0)
    ncmp_pad = seq // CMP_STRIDE
    return pl.pallas_call(
        _cmp_body,
        grid=(batch,),
        in_specs=[
            pl.BlockSpec((seq, LANES), seqblk),
            pl.BlockSpec((seq, LANES), seqblk),
            pl.BlockSpec((CMP_LEN, LANES), c2),
            pl.BlockSpec((CMP_LEN, LANES, 2 * CMP_HIDDEN), c3),
            pl.BlockSpec((2 * CMP_HIDDEN, LANES), c2),
            pl.BlockSpec((CMP_LEN, LANES), c2),
            pl.BlockSpec((CMP_LEN, LANES, 2 * CMP_HIDDEN), c3),
            pl.BlockSpec((2 * CMP_HIDDEN, LANES), c2),
            pl.BlockSpec((ncmp_pad, LANES), c2),
            pl.BlockSpec((ncmp_pad, LANES), c2),
            pl.BlockSpec((1, LANES), c2),
            pl.BlockSpec((LANES, LANES), c2),
        ],
        out_specs=[
            pl.BlockSpec((ncmp_pad, LANES), seqblk),
            pl.BlockSpec((1, LANES, ncmp_pad), lambda b: (b, 0, 0)),
            pl.BlockSpec((1, LANES, ncmp_pad), lambda b: (b, 0, 0)),
        ],
        out_shape=[
            jax.ShapeDtypeStruct((batch * ncmp_pad, LANES), BF16),
            jax.ShapeDtypeStruct((batch, LANES, ncmp_pad), BF16),
            jax.ShapeDtypeStruct((batch, LANES, ncmp_pad), BF16),
        ],
        compiler_params=pltpu.CompilerParams(
            dimension_semantics=("parallel",), vmem_limit_bytes=VMEM_LIMIT),
        name="compress",
    )(kc, vc, pek, w1k, w2k, pev, w1v, w2v, cos, sin, gk, bd)


def _lane_cat(blocks):
    return jnp.concatenate(blocks, axis=1)


def _softmax_numer(s, bias, m_run=None):
    es, ms = [], []
    for r in range(GQA_REP):
        sm = s[:, r * LANES:(r + 1) * LANES] + bias
        m = jnp.max(sm, axis=0, keepdims=True)
        if m_run is not None:
            m = jnp.maximum(m, m_run[:, r * LANES:(r + 1) * LANES])
        es.append(jnp.exp2(sm - m))
        ms.append(m)
    return _lane_cat(es), _lane_cat(ms)


def _safe_inv(l):
    return 1.0 / jnp.where(l > 0.0, l, 1.0)


def _attn_body(*refs):
    c = pl.program_id(1)
    n_max = refs[1].shape[0] // SLC_CHUNK
    for n in range(1, n_max + 1):
        pl.when((c * Q_TILE) // SLC_CHUNK + 1 == n)(functools.partial(_attn_tile, n, c, *refs))


def _attn_tile(n_chunks, c, q_ref, ks_ref, kw_ref, vs0_ref, vs1_ref, vw0_ref, vw1_ref, kc_ref, vc0_ref,
               vc1_ref, gt_ref, ovl_ref, gn_ref, out_ref):
    groups = range(NSA_KV_HEADS)
    vs_refs, vw_refs, vc_refs = (vs0_ref, vs1_ref), (vw0_ref, vw1_ref), (vc0_ref, vc1_ref)
    sum_row = (HEAD_DIM, 0)
    width = GQA_REP * LANES
    t_lane = c * Q_TILE + lax.broadcasted_iota(jnp.int32, (1, LANES), 1)
    lane_sq = lax.broadcasted_iota(jnp.int32, (Q_TILE, LANES), 1)
    nsel = 32
    j_idx = lax.broadcasted_iota(jnp.int32, (nsel, LANES), 0)
    cur = lax.shift_right_logical(t_lane, 6)
    n_idx = lax.broadcasted_iota(jnp.int32, (LANES, LANES), 0)
    cbias = jnp.where((n_idx * CMP_STRIDE + (CMP_LEN - 1) <= t_lane) & (n_idx < LANES - 1), 0.0, NEG)
    has_cmp = jnp.where(t_lane >= CMP_LEN - 1, 1.0, 0.0)
    has_cmp = _lane_cat([has_cmp] * GQA_REP)
    win_blk = jnp.maximum(c - WINDOW // Q_TILE, 0)
    win_start = pl.multiple_of(win_blk * Q_TILE, Q_TILE)
    kpos_w = win_start + lax.broadcasted_iota(jnp.int32, (WIN_KEYS, LANES), 0)
    wbias = jnp.where((kpos_w <= t_lane) & (kpos_w > t_lane - WINDOW), 0.0, NEG)
    row_sl = lax.broadcasted_iota(jnp.int32, (SLC_CHUNK, LANES), 0)
    zero_q = jnp.zeros((Q_TILE, LANES), BF16)

    def gate_row(g, branch):
        rows = [(g * GQA_REP + r) * 3 + branch for r in range(GQA_REP)]
        return _lane_cat([gt_ref[0, i:i + 1, :] for i in rows])

    qg = []
    for g in groups:
        in_g = (lane_sq >= HEAD_DIM) if g else (lane_sq < HEAD_DIM)
        qg.append(jnp.concatenate(
            [jnp.where(in_g, q_ref[:, r * LANES:(r + 1) * LANES], zero_q) for r in range(GQA_REP)],
            axis=0))

    o_cmp, sel_bias = [], []
    for g in groups:
        sc = _dot_tb(kc_ref[...], qg[g])
        ec, _ = _softmax_numer(sc, cbias)
        coef = has_cmp * _safe_inv(jnp.sum(ec, axis=0, keepdims=True))
        o_cmp.append(_dot(vc_refs[g][0], ec.astype(BF16)) * (gate_row(g, 0) * coef))
        pc = ec * coef
        psum = (pc[:, 0:LANES] + pc[:, LANES:2 * LANES]
                + pc[:, 2 * LANES:3 * LANES] + pc[:, 3 * LANES:4 * LANES])
        p_hi, p_lo = _split_bf16(psum)
        imp = (_dot(ovl_ref[...], p_hi) + _dot(ovl_ref[...], p_lo))[0:nsel, :]
        valid = j_idx <= cur
        forced = (j_idx == 0) | (j_idx == cur) | (j_idx == cur - 1)
        score = jnp.where(valid, imp + jnp.where(forced, FORCE, 0.0), -FORCE)
        rank = jnp.zeros((nsel, LANES), F32)
        for jp in range(nsel):
            row = score[jp:jp + 1, :]
            beats = (row > score) | ((row == score) & (j_idx > jp))
            rank = rank + jnp.where(beats, 1.0, 0.0)
        sel_bias.append(jnp.where((rank < float(SLC_TOPN)) & (score > -1.0), 0.0, NEG))

    nsub = SLC_CHUNK // LANES
    blocks_per_chunk = SLC_CHUNK // SLC_BLOCK
    m_run = [None] * NSA_KV_HEADS
    acc = [None] * NSA_KV_HEADS
    for kc in range(n_chunks):
        kblk = ks_ref[kc * SLC_CHUNK:(kc + 1) * SLC_CHUNK, :]
        for g in groups:
            j0 = kc * blocks_per_chunk
            bias = jnp.concatenate(
                [jnp.broadcast_to(sel_bias[g][j:j + 1, :], (SLC_BLOCK, LANES))
                 for j in range(j0, j0 + blocks_per_chunk)], axis=0)
            if kc == n_chunks - 1:
                bias = jnp.where(kc * SLC_CHUNK + row_sl <= t_lane, bias, NEG)
            e, m_new = _softmax_numer(_dot_tb(kblk, qg[g]), bias, m_run[g])
            vt = _lane_cat([vs_refs[g][nsub * kc + i] for i in range(nsub)])
            pv = _dot(vt, e.astype(BF16))
            acc[g] = pv if kc == 0 else acc[g] * jnp.exp2(m_run[g] - m_new) + pv
            m_run[g] = m_new

    kwin = kw_ref[pl.ds(win_start, WIN_KEYS), :]
    o_groups = []
    for g in groups:
        ew, _ = _softmax_numer(_dot_tb(kwin, qg[g]), wbias)
        vt = _lane_cat([vw_refs[g][win_blk + i] for i in range(WIN_KEYS // LANES)])
        ow = _dot(vt, ew.astype(BF16))
        os_ = acc[g]
        r0 = sum_row[g]
        o_groups.append(o_cmp[g]
                        + os_ * (gate_row(g, 1) * _safe_inv(os_[r0:r0 + 1, :]))
                        + ow * (gate_row(g, 2) * _safe_inv(ow[r0:r0 + 1, :])))

    row_d = lax.broadcasted_iota(jnp.int32, (LANES, width), 0)
    o_t = jnp.where(row_d < HEAD_DIM, o_groups[0], o_groups[1])
    blks = [o_t[:, r * LANES:(r + 1) * LANES].T for r in range(GQA_REP)]
    ss = blks[0] * blks[0]
    for r in range(1, GQA_REP):
        ss = ss + blks[r] * blks[r]
    rs = lax.rsqrt(jnp.sum(ss, axis=-1, keepdims=True) * (1.0 / NSA_WIDTH) + EPS)
    for r in range(GQA_REP):
        out_ref[:, r * LANES:(r + 1) * LANES] = (
            blks[r] * rs * gn_ref[:, r * LANES:(r + 1) * LANES]).astype(BF16)


def _attn_call(q, ks, kw, vs0, vs1, vw0, vw1, kcmp, vc0, vc1, gt, ovl, gn, batch, seq):
    nq = seq // Q_TILE
    nkb = seq // LANES
    qrow = lambda b, c: (b * nq + c, 0)
    per_b = lambda b, c: (b, 0)
    per_b3 = lambda b, c: (b, 0, 0)
    vspec = pl.BlockSpec((nkb, LANES, LANES), per_b3)
    cspec = pl.BlockSpec((1, LANES, LANES), per_b3)
    return pl.pallas_call(
        _attn_body,
        grid=(batch, nq),
        in_specs=[
            pl.BlockSpec((Q_TILE, NSA_WIDTH), qrow),
            pl.BlockSpec((seq, LANES), per_b),
            pl.BlockSpec((seq, LANES), per_b),
            vspec, vspec, vspec, vspec,
            pl.BlockSpec((LANES, LANES), per_b),
            cspec, cspec,
            pl.BlockSpec((1, 32, LANES), lambda b, c: (b * nq + c, 0, 0)),
            pl.BlockSpec((LANES, LANES), lambda b, c: (0, 0)),
            pl.BlockSpec((1, NSA_WIDTH), lambda b, c: (0, 0)),
        ],
        out_specs=pl.BlockSpec((Q_TILE, NSA_WIDTH), qrow),
        out_shape=jax.ShapeDtypeStruct((batch * seq, NSA_WIDTH), BF16),
        compiler_params=pltpu.CompilerParams(
            dimension_semantics=("parallel", "arbitrary"), vmem_limit_bytes=VMEM_LIMIT),
        name="attn",
    )(q, ks, kw, vs0, vs1, vw0, vw1, kcmp, vc0, vc1, gt, ovl, gn)


def _conv_body(tiles_per_seq, cur_ref, halo_ref, w_ref, b_ref, lng_ref, lnb_ref, gn_ref, out_ref,
               buf_ref, sh_ref, cv_ref):
    first = (pl.program_id(0) % tiles_per_seq) == 0
    buf_ref[0:CONV_HALO, :] = jnp.where(first, 0.0, halo_ref[...])
    buf_ref[CONV_HALO:, :] = cur_ref[...]
    lead = CONV_HALO - (CONV_WIDTH - 1)
    ncb = CONV_CH // LANES
    nrb = CONV_TILE // CONV_ROWS
    for s in range(8):
        span = CONV_TILE + 8 * ((CONV_WIDTH - 1 - s) // 8)
        for cb in range(ncb):
            sh_ref[s, cb, 0:span, :] = buf_ref[lead + s:lead + s + span, cb * LANES:(cb + 1) * LANES]

    def conv_block(i, carry):
        cb = i % ncb
        base = pl.multiple_of((i // ncb) * CONV_ROWS, CONV_ROWS)
        acc = jnp.zeros((CONV_ROWS, LANES), F32) + b_ref[cb]
        for k in range(CONV_WIDTH):
            acc = acc + w_ref[cb, k:k + 1, :] * sh_ref[k % 8, cb, pl.ds(base + 8 * (k // 8), CONV_ROWS), :]
        cv_ref[cb, pl.ds(base, CONV_ROWS), :] = acc
        return carry

    lax.fori_loop(0, nrb * ncb, conv_block, 0)
    h = _lane_cat([cv_ref[cb] for cb in range(ncb)])
    mu = jnp.mean(h, axis=-1, keepdims=True)
    d = h - mu
    var = jnp.mean(d * d, axis=-1, keepdims=True)
    hn = d * lax.rsqrt(var + EPS) * lng_ref[...] + lnb_ref[...]
    o = hn * _sigmoid(hn)
    ms = jnp.mean(o * o, axis=-1, keepdims=True)
    out_ref[...] = (o * lax.rsqrt(ms + EPS) * gn_ref[...]).astype(BF16)


def _conv_call(hcv, w, b, lng, lnb, gn, seq):
    t = hcv.shape[0]
    nt = t // CONV_TILE
    halo_per_tile = CONV_TILE // CONV_HALO
    ncb = CONV_CH // LANES
    row = lambda i: (i, 0)
    const = lambda i: (0, 0)
    return pl.pallas_call(
        functools.partial(_conv_body, seq // CONV_TILE),
        grid=(nt,),
        in_specs=[
            pl.BlockSpec((CONV_TILE, CONV_CH), row),
            pl.BlockSpec((CONV_HALO, CONV_CH), lambda i: (jnp.maximum(i * halo_per_tile - 1, 0), 0)),
            pl.BlockSpec((ncb, CONV_HALO, LANES), lambda i: (0, 0, 0)),
            pl.BlockSpec((ncb, 1, LANES), lambda i: (0, 0, 0)),
            pl.BlockSpec((1, CONV_CH), const),
            pl.BlockSpec((1, CONV_CH), const),
            pl.BlockSpec((1, CONV_CH), const),
        ],
        out_specs=pl.BlockSpec((CONV_TILE, CONV_CH), row),
        out_shape=jax.ShapeDtypeStruct((t, CONV_CH), BF16),
        scratch_shapes=[pltpu.VMEM((CONV_HALO + CONV_TILE, CONV_CH), F32),
                        pltpu.VMEM((8, ncb, CONV_TILE + 8 * ((CONV_WIDTH - 1) // 8), LANES), F32),
                        pltpu.VMEM((ncb, CONV_TILE, LANES), F32)],
        compiler_params=pltpu.CompilerParams(
            dimension_semantics=("parallel",), vmem_limit_bytes=VMEM_LIMIT),
        name="conv",
    )(hcv, hcv, w, b, lng, lnb, gn)


def _ffn_body(x_ref, ma_ref, mb_ref, woa_ref, wob_ref, gf_ref, wg_ref, wu_ref, wd_ref, out_ref):
    h = x_ref[...] + _dot(ma_ref[...], woa_ref[...]) + _dot(mb_ref[...], wob_ref[...])
    ms = jnp.mean(h * h, axis=-1, keepdims=True)
    hn = (h * lax.rsqrt(ms + EPS) * gf_ref[...]).astype(BF16)
    acc = jnp.zeros_like(h)
    for j in range(FFN_HIDDEN // FFN_CHUNK):
        cols = slice(j * FFN_CHUNK, (j + 1) * FFN_CHUNK)
        gte = _dot(hn, wg_ref[:, cols])
        up = _dot(hn, wu_ref[:, cols])
        act = (gte * _sigmoid(gte) * up).astype(BF16)
        acc = acc + _dot(act, wd_ref[cols, :])
    out_ref[...] = h + acc


def _ffn_call(x2, ma, mb, woa, wob, gf, wg, wu, wd):
    t = x2.shape[0]
    nt = t // FFN_TILE
    row = lambda i: (i, 0)
    const = lambda i: (0, 0)
    once = pl.Buffered(1)
    return pl.pallas_call(
        _ffn_body,
        grid=(nt,),
        in_specs=[
            pl.BlockSpec((FFN_TILE, D_MODEL), row),
            pl.BlockSpec((FFN_TILE, NSA_WIDTH), row),
            pl.BlockSpec((FFN_TILE, CONV_CH), row),
            pl.BlockSpec((NSA_WIDTH, D_MODEL), const, pipeline_mode=once),
            pl.BlockSpec((CONV_CH, D_MODEL), const, pipeline_mode=once),
            pl.BlockSpec((1, D_MODEL), const),
            pl.BlockSpec((D_MODEL, FFN_HIDDEN), const, pipeline_mode=once),
            pl.BlockSpec((D_MODEL, FFN_HIDDEN), const, pipeline_mode=once),
            pl.BlockSpec((FFN_HIDDEN, D_MODEL), const, pipeline_mode=once),
        ],
        out_specs=pl.BlockSpec((FFN_TILE, D_MODEL), row),
        out_shape=jax.ShapeDtypeStruct((t, D_MODEL), F32),
        compiler_params=pltpu.CompilerParams(
            dimension_semantics=("parallel",), vmem_limit_bytes=VMEM_LIMIT),
        name="ffn",
    )(x2, ma, mb, woa, wob, gf, wg, wu, wd)


def _q_perm():
    return np.array([(g * GQA_REP + r) * HEAD_DIM + d
                     for r in range(GQA_REP) for g in range(NSA_KV_HEADS) for d in range(HEAD_DIM)])


def _rope_tables(pos):
    half = HEAD_DIM // 2
    inv = ROPE_THETA ** (-np.arange(half, dtype=np.float64) / half)
    ang = np.asarray(pos, np.float64)[:, None] * inv[None, :]
    cos = np.concatenate([np.cos(ang), np.cos(ang)], axis=1)
    sin = np.concatenate([-np.sin(ang), np.sin(ang)], axis=1)
    return (np.tile(cos, (1, NSA_KV_HEADS)).astype(np.float32),
            np.tile(sin, (1, NSA_KV_HEADS)).astype(np.float32))


def _block_diag_ones(width):
    idx = np.arange(width) // HEAD_DIM
    return (idx[:, None] == idx[None, :]).astype(np.float32)


def _overlap_t(seq):
    ncmp = (seq - CMP_LEN) // CMP_STRIDE + 1
    cs = np.arange(ncmp)[:, None] * CMP_STRIDE
    ss = np.arange(seq // SLC_BLOCK)[None, :] * SLC_BLOCK
    ov = np.clip(np.minimum(cs + CMP_LEN, ss + SLC_BLOCK) - np.maximum(cs, ss), 0, None) / CMP_LEN
    out = np.zeros((LANES, LANES), np.float32)
    out[:seq // SLC_BLOCK, :ncmp] = ov.T
    return out


def _cmp_weights(pe, w1, w2):
    pe2 = jnp.concatenate([pe, pe], axis=1)
    w1r = w1.reshape(CMP_LEN, HEAD_DIM, CMP_HIDDEN)
    z1 = jnp.zeros_like(w1r)
    w1b = jnp.concatenate([jnp.concatenate([w1r, z1], axis=2),
                           jnp.concatenate([z1, w1r], axis=2)], axis=1).astype(BF16)
    z2 = jnp.zeros_like(w2)
    w2b = jnp.concatenate([jnp.concatenate([w2, z2], axis=1),
                           jnp.concatenate([z2, w2], axis=1)], axis=0).astype(BF16)
    return pe2, w1b, w2b


def kernel(x, attn_norm_g, w_in, q_norm_g, k_norm_cmp_g, k_norm_slc_g, k_norm_win_g, cmp_pe_k, cmp_w1_k, cmp_w2_k, cmp_pe_v, cmp_w1_v, cmp_w2_v, conv_dw_w, conv_dw_b, conv_ln_g, conv_ln_b, out_norm_nsa_g, out_norm_conv_g, w_out, ffn_norm_g, w_gate_up, w_down):
    batch, seq, d_model = x.shape
    assert d_model == D_MODEL and seq % PROJ_TILE == 0 and seq // SLC_BLOCK == 32
    depth = w_in.shape[0]
    perm = _q_perm()
    cos_np, sin_np = _rope_tables(np.arange(seq))
    ccos_np, csin_np = _rope_tables(np.arange(seq // CMP_STRIDE) * CMP_STRIDE + CMP_LEN - 1)
    cos, sin, ccos, csin = map(jnp.asarray, (cos_np, sin_np, ccos_np, csin_np))
    bd256 = jnp.asarray(_block_diag_ones(256), BF16)
    bd128 = jnp.asarray(_block_diag_ones(LANES), BF16)
    ovl = jnp.asarray(_overlap_t(seq), BF16)

    x2 = x.reshape(batch * seq, d_model)
    for l in range(depth):
        w = w_in[l]
        o_q, o_kc, o_vc, o_ks, o_vs, o_kw, o_vw, o_gl, o_cv = (
            0, 512, 640, 768, 896, 1024, 1152, 1280, 1304)
        gl_pad = jnp.zeros((d_model, LANES - 3 * NSA_HEADS), w.dtype)
        w_cat = jnp.concatenate([
            w[:, perm],
            w[:, o_ks:o_ks + KV_WIDTH], w[:, o_kw:o_kw + KV_WIDTH],
            w[:, o_vs:o_vs + KV_WIDTH], w[:, o_vw:o_vw + KV_WIDTH],
            w[:, o_kc:o_kc + KV_WIDTH], w[:, o_vc:o_vc + KV_WIDTH],
            w[:, o_cv:o_cv + CONV_CH], w[:, o_cv + CONV_CH:o_cv + 2 * CONV_CH],
            w[:, o_gl:o_gl + 3 * NSA_HEADS], gl_pad], axis=1).astype(BF16)
        gq = jnp.tile(q_norm_g[l], 4)[None, :]
        gk = jnp.concatenate([jnp.tile(k_norm_slc_g[l], 2), jnp.tile(k_norm_win_g[l], 2)])[None, :]
        q, ks, kw, vs0, vs1, vw0, vw1, kc, vc, gt, hcv = _proj_call(
            x2, attn_norm_g[l][None, :], w_cat, cos, sin, gq, gk, bd256, seq)

        pek, w1k, w2k = _cmp_weights(cmp_pe_k[l], cmp_w1_k[l], cmp_w2_k[l])
        pev, w1v, w2v = _cmp_weights(cmp_pe_v[l], cmp_w1_v[l], cmp_w2_v[l])
        kcmp, vc0, vc1 = _cmp_call(kc, vc, pek, w1k, w2k, pev, w1v, w2v, ccos, csin,
                                jnp.tile(k_norm_cmp_g[l], 2)[None, :], bd128, batch, seq)

        mix_a = _attn_call(q, ks, kw, vs0, vs1, vw0, vw1, kcmp, vc0, vc1, gt, ovl,
                           out_norm_nsa_g[l][perm][None, :], batch, seq)

        w_dw = jnp.concatenate([conv_dw_w[l][:, 0, :],
                                jnp.zeros((CONV_HALO - CONV_WIDTH, CONV_CH), F32)], axis=0)
        w_dw = w_dw.reshape(CONV_HALO, CONV_CH // LANES, LANES).transpose(1, 0, 2)
        mix_b = _conv_call(hcv, w_dw, conv_dw_b[l].reshape(CONV_CH // LANES, 1, LANES),
                           conv_ln_g[l][None, :],
                           conv_ln_b[l][None, :], out_norm_conv_g[l][None, :], seq)

        wo = w_out[l]
        x2 = _ffn_call(x2, mix_a, mix_b, wo[:NSA_WIDTH][perm].astype(BF16),
                       wo[NSA_WIDTH:].astype(BF16), ffn_norm_g[l][None, :],
                       w_gate_up[l][:, :FFN_HIDDEN].astype(BF16),
                       w_gate_up[l][:, FFN_HIDDEN:].astype(BF16), w_down[l].astype(BF16))
    return x2.reshape(batch, seq, d_model)
```

```python
import functools

import numpy as np
import jax
import jax.numpy as jnp
from jax import lax
from jax.experimental import pallas as pl
from jax.experimental.pallas import tpu as pltpu

F32 = jnp.float32
BF16 = jnp.bfloat16

D_MODEL = 1024
HEAD_DIM = 64
NSA_HEADS = 8
NSA_KV_HEADS = 2
GQA_REP = NSA_HEADS // NSA_KV_HEADS
NSA_WIDTH = NSA_HEADS * HEAD_DIM
KV_WIDTH = NSA_KV_HEADS * HEAD_DIM
CONV_CH = D_MODEL - NSA_WIDTH
CMP_LEN = 32
CMP_STRIDE = 16
CMP_HIDDEN = 4 * HEAD_DIM
SLC_BLOCK = 64
SLC_TOPN = 8
WINDOW = 512
CONV_WIDTH = 31
FFN_HIDDEN = 2816
ROPE_THETA = 10000.0
EPS = 1e-6
NEG = -1e30
LOG2E = 1.4426950408889634
FORCE = 1e6

LANES = 128
Q_TILE = 128
SLC_CHUNK = 512
WIN_KEYS = WINDOW + Q_TILE
PROJ_TILE = 512
CONV_TILE = 512
CONV_HALO = 32
CONV_ROWS = 128
FFN_TILE = 512
FFN_CHUNK = 256
VMEM_LIMIT = 56 * 1024 * 1024

C_Q = 0
C_KS = 512
C_KW = 640
C_VS = 768
C_VW = 896
C_KC = 1024
C_VC = 1152
C_CA = 1280
C_CG = 1792
C_GL = 2304
PROJ_COLS = 2432

_TRANS_B = (((1,), (1,)), ((), ()))


def _dot(a, b):
    return jnp.dot(a, b, preferred_element_type=F32)


def _dot_tb(a, b):
    return lax.dot_general(a, b, _TRANS_B, preferred_element_type=F32)


def _split_bf16(x):
    hi = x.astype(BF16)
    lo = (x - hi.astype(F32)).astype(BF16)
    return hi, lo


def _sigmoid(x):
    return 1.0 / (1.0 + jnp.exp(-x))


def _head_norm_rope(y, gain, bd, cos, sin, first_half, scale):
    hi, lo = _split_bf16(y * y)
    ss = _dot(hi, bd) + _dot(lo, bd)
    yn = y * lax.rsqrt(ss * (1.0 / HEAD_DIM) + EPS) * gain
    outs = []
    for j in range(y.shape[1] // LANES):
        blk = yn[:, j * LANES:(j + 1) * LANES]
        rot = jnp.where(first_half, pltpu.roll(blk, LANES - HEAD_DIM // 2, 1),
                        pltpu.roll(blk, HEAD_DIM // 2, 1))
        outs.append((blk * cos + rot * sin) * scale)
    return outs


def _value_t_with_ones(v):
    vt = v.T
    row = lax.broadcasted_iota(jnp.int32, vt.shape, 0)
    return (jnp.where(row < HEAD_DIM, vt, 1.0).astype(BF16),
            jnp.where(row < HEAD_DIM, 1.0, vt).astype(BF16))


def _proj_body(x_ref, gin_ref, w_ref, cos_ref, sin_ref, gq_ref, gk_ref, bd_ref,
               q_ref, ks_ref, kw_ref, vs0_ref, vs1_ref, vw0_ref, vw1_ref, kc_ref, vc_ref, gt_ref,
               hcv_ref):
    x = x_ref[...]
    ms = jnp.mean(x * x, axis=-1, keepdims=True)
    xn = (x * lax.rsqrt(ms + EPS) * gin_ref[...]).astype(BF16)
    cos = cos_ref[...]
    sin = sin_ref[...]
    bd = bd_ref[...]
    lane = lax.broadcasted_iota(jnp.int32, (PROJ_TILE, LANES), 1)
    first_half = (lane & (HEAD_DIM // 2)) == 0

    for half in range(2):
        c0 = C_Q + half * 256
        qh = _dot(xn, w_ref[:, c0:c0 + 256])
        blks = _head_norm_rope(qh, gq_ref[...], bd, cos, sin, first_half, HEAD_DIM ** -0.5 * LOG2E)
        for j in range(2):
            q_ref[:, (2 * half + j) * LANES:(2 * half + j + 1) * LANES] = blks[j].astype(BF16)

    kk = _dot(xn, w_ref[:, C_KS:C_KS + 256])
    kblks = _head_norm_rope(kk, gk_ref[...], bd, cos, sin, first_half, 1.0)
    ks_ref[...] = kblks[0].astype(BF16)
    kw_ref[...] = kblks[1].astype(BF16)

    vv = _dot(xn, w_ref[:, C_VS:C_VS + 256])
    for i in range(PROJ_TILE // LANES):
        rows = vv[i * LANES:(i + 1) * LANES, :]
        vs0_ref[i], vs1_ref[i] = _value_t_with_ones(rows[:, :LANES])
        vw0_ref[i], vw1_ref[i] = _value_t_with_ones(rows[:, LANES:])

    cc = _dot(xn, w_ref[:, C_KC:C_KC + 256])
    kc_ref[...] = cc[:, :LANES]
    vc_ref[...] = cc[:, LANES:]

    ca = _dot(xn, w_ref[:, C_CA:C_CA + CONV_CH])
    cg = _dot(xn, w_ref[:, C_CG:C_CG + CONV_CH])
    hcv_ref[...] = ca * _sigmoid(cg)

    sg = _sigmoid(_dot(xn, w_ref[:, C_GL:C_GL + LANES]))
    for i in range(PROJ_TILE // LANES):
        gt_ref[i] = sg[i * LANES:(i + 1) * LANES, :].T[0:32, :]


def _proj_call(x2, gin, w_cat, cos, sin, gq, gk, bd, seq):
    t = x2.shape[0]
    nt = t // PROJ_TILE
    tiles_per_seq = seq // PROJ_TILE
    row = lambda i: (i, 0)
    const = lambda i: (0, 0)
    tab = lambda i: (i % tiles_per_seq, 0)
    blk3 = lambda i: (i, 0, 0)
    nb = PROJ_TILE // LANES
    return pl.pallas_call(
        _proj_body,
        grid=(nt,),
        in_specs=[
            pl.BlockSpec((PROJ_TILE, D_MODEL), row),
            pl.BlockSpec((1, D_MODEL), const),
            pl.BlockSpec((D_MODEL, PROJ_COLS), const),
            pl.BlockSpec((PROJ_TILE, LANES), tab),
            pl.BlockSpec((PROJ_TILE, LANES), tab),
            pl.BlockSpec((1, 256), const),
            pl.BlockSpec((1, 256), const),
            pl.BlockSpec((256, 256), const),
        ],
        out_specs=[
            pl.BlockSpec((PROJ_TILE, NSA_WIDTH), row),
            pl.BlockSpec((PROJ_TILE, LANES), row),
            pl.BlockSpec((PROJ_TILE, LANES), row),
            pl.BlockSpec((nb, LANES, LANES), blk3),
            pl.BlockSpec((nb, LANES, LANES), blk3),
            pl.BlockSpec((nb, LANES, LANES), blk3),
            pl.BlockSpec((nb, LANES, LANES), blk3),
            pl.BlockSpec((PROJ_TILE, LANES), row),
            pl.BlockSpec((PROJ_TILE, LANES), row),
            pl.BlockSpec((nb, 32, LANES), blk3),
            pl.BlockSpec((PROJ_TILE, CONV_CH), row),
        ],
        out_shape=[
            jax.ShapeDtypeStruct((t, NSA_WIDTH), BF16),
            jax.ShapeDtypeStruct((t, LANES), BF16),
            jax.ShapeDtypeStruct((t, LANES), BF16),
            jax.ShapeDtypeStruct((t // LANES, LANES, LANES), BF16),
            jax.ShapeDtypeStruct((t // LANES, LANES, LANES), BF16),
            jax.ShapeDtypeStruct((t // LANES, LANES, LANES), BF16),
            jax.ShapeDtypeStruct((t // LANES, LANES, LANES), BF16),
            jax.ShapeDtypeStruct((t, LANES), F32),
            jax.ShapeDtypeStruct((t, LANES), F32),
            jax.ShapeDtypeStruct((t // LANES, 32, LANES), F32),
            jax.ShapeDtypeStruct((t, CONV_CH), F32),
        ],
        compiler_params=pltpu.CompilerParams(
            dimension_semantics=("parallel",), vmem_limit_bytes=VMEM_LIMIT),
        name="proj",
    )(x2, gin, w_cat, cos, sin, gq, gk, bd)


def _cmp_mlp(t_ref, pe_ref, w1_ref, w2_ref):
    half = CMP_LEN // 2
    nchunk = t_ref.shape[0] // CMP_STRIDE
    p = jnp.zeros((nchunk, 2 * CMP_HIDDEN), F32)
    q = jnp.zeros((nchunk, 2 * CMP_HIDDEN), F32)
    for l in range(half):
        xl = t_ref[pl.ds(l, nchunk, stride=CMP_STRIDE), :]
        p = p + _dot((xl + pe_ref[l:l + 1, :]).astype(BF16), w1_ref[l])
        q = q + _dot((xl + pe_ref[half + l:half + l + 1, :]).astype(BF16), w1_ref[half + l])
    h1 = p + pltpu.roll(q, nchunk - 1, 0)
    act = (h1 * _sigmoid(h1)).astype(BF16)
    return _dot(act, w2_ref[...])


def _cmp_body(kc_ref, vc_ref, pek_ref, w1k_ref, w2k_ref, pev_ref, w1v_ref, w2v_ref,
              cos_ref, sin_ref, gk_ref, bd_ref, ko_ref, vt0_ref, vt1_ref):
    kraw = _cmp_mlp(kc_ref, pek_ref, w1k_ref, w2k_ref)
    lane = lax.broadcasted_iota(jnp.int32, kraw.shape, 1)
    first_half = (lane & (HEAD_DIM // 2)) == 0
    (kr,) = _head_norm_rope(kraw, gk_ref[...], bd_ref[...], cos_ref[...], sin_ref[...],
                            first_half, 1.0)
    ko_ref[...] = kr.astype(BF16)
    vraw = _cmp_mlp(vc_ref, pev_ref, w1v_ref, w2v_ref)
    vt0_ref[0], vt1_ref[0] = _value_t_with_ones(vraw)


def _cmp_call(kc, vc, pek, w1k, w2k, pev, w1v, w2v, cos, sin, gk, bd, batch, seq):
    seqblk = lambda b: (b, 0)
    c2 = lambda b: (0, 0)
    c3 = lambda b: (0, 0, 0)
    ncmp_pad = seq // CMP_STRIDE
    return pl.pallas_call(
        _cmp_body,
        grid=(batch,),
        in_specs=[
            pl.BlockSpec((seq, LANES), seqblk),
            pl.BlockSpec((seq, LANES), seqblk),
            pl.BlockSpec((CMP_LEN, LANES), c2),
            pl.BlockSpec((CMP_LEN, LANES, 2 * CMP_HIDDEN), c3),
            pl.BlockSpec((2 * CMP_HIDDEN, LANES), c2),
            pl.BlockSpec((CMP_LEN, LANES), c2),
            pl.BlockSpec((CMP_LEN, LANES, 2 * CMP_HIDDEN), c3),
            pl.BlockSpec((2 * CMP_HIDDEN, LANES), c2),
            pl.BlockSpec((ncmp_pad, LANES), c2),
            pl.BlockSpec((ncmp_pad, LANES), c2),
            pl.BlockSpec((1, LANES), c2),
            pl.BlockSpec((LANES, LANES), c2),
        ],
        out_specs=[
            pl.BlockSpec((ncmp_pad, LANES), seqblk),
            pl.BlockSpec((1, LANES, ncmp_pad), lambda b: (b, 0, 0)),
            pl.BlockSpec((1, LANES, ncmp_pad), lambda b: (b, 0, 0)),
        ],
        out_shape=[
            jax.ShapeDtypeStruct((batch * ncmp_pad, LANES), BF16),
            jax.ShapeDtypeStruct((batch, LANES, ncmp_pad), BF16),
            jax.ShapeDtypeStruct((batch, LANES, ncmp_pad), BF16),
        ],
        compiler_params=pltpu.CompilerParams(
            dimension_semantics=("parallel",), vmem_limit_bytes=VMEM_LIMIT),
        name="compress",
    )(kc, vc, pek, w1k, w2k, pev, w1v, w2v, cos, sin, gk, bd)


def _lane_cat(blocks):
    return jnp.concatenate(blocks, axis=1)


def _softmax_numer(s, bias, m_run=None):
    es, ms = [], []
    for r in range(GQA_REP):
        sm = s[:, r * LANES:(r + 1) * LANES] + bias
        m = jnp.max(sm, axis=0, keepdims=True)
        if m_run is not None:
            m = jnp.maximum(m, m_run[:, r * LANES:(r + 1) * LANES])
        es.append(jnp.exp2(sm - m))
        ms.append(m)
    return _lane_cat(es), _lane_cat(ms)


def _safe_inv(l):
    return 1.0 / jnp.where(l > 0.0, l, 1.0)


def _attn_body(*refs):
    c = pl.program_id(1)
    n_max = refs[1].shape[0] // SLC_CHUNK
    for n in range(1, n_max + 1):
        pl.when((c * Q_TILE) // SLC_CHUNK + 1 == n)(functools.partial(_attn_tile, n, c, *refs))


def _attn_tile(n_chunks, c, q_ref, ks_ref, kw_ref, vs0_ref, vs1_ref, vw0_ref, vw1_ref, kc_ref, vc0_ref,
               vc1_ref, gt_ref, ovl_ref, gn_ref, out_ref):
    groups = range(NSA_KV_HEADS)
    vs_refs, vw_refs, vc_refs = (vs0_ref, vs1_ref), (vw0_ref, vw1_ref), (vc0_ref, vc1_ref)
    sum_row = (HEAD_DIM, 0)
    width = GQA_REP * LANES
    t_lane = c * Q_TILE + lax.broadcasted_iota(jnp.int32, (1, LANES), 1)
    lane_sq = lax.broadcasted_iota(jnp.int32, (Q_TILE, LANES), 1)
    nsel = 32
    j_idx = lax.broadcasted_iota(jnp.int32, (nsel, LANES), 0)
    cur = lax.shift_right_logical(t_lane, 6)
    n_idx = lax.broadcasted_iota(jnp.int32, (LANES, LANES), 0)
    cbias = jnp.where((n_idx * CMP_STRIDE + (CMP_LEN - 1) <= t_lane) & (n_idx < LANES - 1), 0.0, NEG)
    has_cmp = jnp.where(t_lane >= CMP_LEN - 1, 1.0, 0.0)
    has_cmp = _lane_cat([has_cmp] * GQA_REP)
    win_blk = jnp.maximum(c - WINDOW // Q_TILE, 0)
    win_start = pl.multiple_of(win_blk * Q_TILE, Q_TILE)
    kpos_w = win_start + lax.broadcasted_iota(jnp.int32, (WIN_KEYS, LANES), 0)
    wbias = jnp.where((kpos_w <= t_lane) & (kpos_w > t_lane - WINDOW), 0.0, NEG)
    row_sl = lax.broadcasted_iota(jnp.int32, (SLC_CHUNK, LANES), 0)
    zero_q = jnp.zeros((Q_TILE, LANES), BF16)

    def gate_row(g, branch):
        rows = [(g * GQA_REP + r) * 3 + branch for r in range(GQA_REP)]
        return _lane_cat([gt_ref[0, i:i + 1, :] for i in rows])

    qg = []
    for g in groups:
        in_g = (lane_sq >= HEAD_DIM) if g else (lane_sq < HEAD_DIM)
        qg.append(jnp.concatenate(
            [jnp.where(in_g, q_ref[:, r * LANES:(r + 1) * LANES], zero_q) for r in range(GQA_REP)],
            axis=0))

    o_cmp = [None] * NSA_KV_HEADS
    sel_bias = [None] * NSA_KV_HEADS

    def compressed_softmax(g, sc):
        ec, _ = _softmax_numer(sc, cbias)
        coef = has_cmp * _safe_inv(jnp.sum(ec, axis=0, keepdims=True))
        pc = ec * coef
        psum = (pc[:, 0:LANES] + pc[:, LANES:2 * LANES]
                + pc[:, 2 * LANES:3 * LANES] + pc[:, 3 * LANES:4 * LANES])
        p_hi, p_lo = _split_bf16(psum)
        imp = (_dot(ovl_ref[...], p_hi) + _dot(ovl_ref[...], p_lo))[0:nsel, :]
        valid = j_idx <= cur
        forced = (j_idx == 0) | (j_idx == cur) | (j_idx == cur - 1)
        score = jnp.where(valid, imp + jnp.where(forced, FORCE, 0.0), -FORCE)
        rank = jnp.zeros((nsel, LANES), F32)
        for jp in range(nsel):
            row = score[jp:jp + 1, :]
            beats = (row > score) | ((row == score) & (j_idx > jp))
            rank = rank + jnp.where(beats, 1.0, 0.0)
        sel_bias[g] = jnp.where((rank < float(SLC_TOPN)) & (score > -1.0), 0.0, NEG)
        return ec.astype(BF16), coef

    nsub = SLC_CHUNK // LANES
    blocks_per_chunk = SLC_CHUNK // SLC_BLOCK
    kwin = kw_ref[pl.ds(win_start, WIN_KEYS), :]
    units = [("cmp", g, 0) for g in groups] + [("win", g, 0) for g in groups]
    units += [("slc", g, kc) for kc in range(n_chunks) for g in groups]
    m_run = [None] * NSA_KV_HEADS
    acc_s = [None] * NSA_KV_HEADS
    acc_w = [None] * NSA_KV_HEADS

    def scores(unit):
        kind, g, kc = unit
        if kind == "cmp":
            keys = kc_ref[...]
        elif kind == "win":
            keys = kwin
        else:
            keys = ks_ref[kc * SLC_CHUNK:(kc + 1) * SLC_CHUNK, :]
        return _dot_tb(keys, qg[g])

    def softmax(unit, s):
        kind, g, kc = unit
        if kind == "cmp":
            return compressed_softmax(g, s)
        if kind == "win":
            e, _ = _softmax_numer(s, wbias)
            return e.astype(BF16), None
        j0 = kc * blocks_per_chunk
        bias = jnp.concatenate(
            [jnp.broadcast_to(sel_bias[g][j:j + 1, :], (SLC_BLOCK, LANES))
             for j in range(j0, j0 + blocks_per_chunk)], axis=0)
        if kc == n_chunks - 1:
            bias = jnp.where(kc * SLC_CHUNK + row_sl <= t_lane, bias, NEG)
        e, m_new = _softmax_numer(s, bias, m_run[g])
        alpha = None if kc == 0 else jnp.exp2(m_run[g] - m_new)
        m_run[g] = m_new
        return e.astype(BF16), alpha

    def weighted_values(unit, e, alpha):
        kind, g, kc = unit
        if kind == "cmp":
            o_cmp[g] = _dot(vc_refs[g][0], e) * (gate_row(g, 0) * alpha)
            return
        if kind == "win":
            vt = _lane_cat([vw_refs[g][win_blk + i] for i in range(WIN_KEYS // LANES)])
            acc_w[g] = _dot(vt, e)
            return
        vt = _lane_cat([vs_refs[g][nsub * kc + i] for i in range(nsub)])
        pv = _dot(vt, e)
        acc_s[g] = pv if alpha is None else acc_s[g] * alpha + pv

    s_next = scores(units[0])
    pending = None
    for i, unit in enumerate(units):
        s_cur = s_next
        if i + 1 < len(units):
            s_next = scores(units[i + 1])
        e, alpha = softmax(unit, s_cur)
        if pending is not None:
            weighted_values(*pending)
        pending = (unit, e, alpha)
    weighted_values(*pending)

    o_groups = []
    for g in groups:
        r0 = sum_row[g]
        o_groups.append(o_cmp[g]
                        + acc_s[g] * (gate_row(g, 1) * _safe_inv(acc_s[g][r0:r0 + 1, :]))
                        + acc_w[g] * (gate_row(g, 2) * _safe_inv(acc_w[g][r0:r0 + 1, :])))

    row_d = lax.broadcasted_iota(jnp.int32, (LANES, width), 0)
    o_t = jnp.where(row_d < HEAD_DIM, o_groups[0], o_groups[1])
    blks = [o_t[:, r * LANES:(r + 1) * LANES].T for r in range(GQA_REP)]
    ss = blks[0] * blks[0]
    for r in range(1, GQA_REP):
        ss = ss + blks[r] * blks[r]
    rs = lax.rsqrt(jnp.sum(ss, axis=-1, keepdims=True) * (1.0 / NSA_WIDTH) + EPS)
    for r in range(GQA_REP):
        out_ref[:, r * LANES:(r + 1) * LANES] = (
            blks[r] * rs * gn_ref[:, r * LANES:(r + 1) * LANES]).astype(BF16)


def _attn_call(q, ks, kw, vs0, vs1, vw0, vw1, kcmp, vc0, vc1, gt, ovl, gn, batch, seq):
    nq = seq // Q_TILE
    nkb = seq // LANES
    qrow = lambda b, c: (b * nq + c, 0)
    per_b = lambda b, c: (b, 0)
    per_b3 = lambda b, c: (b, 0, 0)
    vspec = pl.BlockSpec((nkb, LANES, LANES), per_b3)
    cspec = pl.BlockSpec((1, LANES, LANES), per_b3)
    return pl.pallas_call(
        _attn_body,
        grid=(batch, nq),
        in_specs=[
            pl.BlockSpec((Q_TILE, NSA_WIDTH), qrow),
            pl.BlockSpec((seq, LANES), per_b),
            pl.BlockSpec((seq, LANES), per_b),
            vspec, vspec, vspec, vspec,
            pl.BlockSpec((LANES, LANES), per_b),
            cspec, cspec,
            pl.BlockSpec((1, 32, LANES), lambda b, c: (b * nq + c, 0, 0)),
            pl.BlockSpec((LANES, LANES), lambda b, c: (0, 0)),
            pl.BlockSpec((1, NSA_WIDTH), lambda b, c: (0, 0)),
        ],
        out_specs=pl.BlockSpec((Q_TILE, NSA_WIDTH), qrow),
        out_shape=jax.ShapeDtypeStruct((batch * seq, NSA_WIDTH), BF16),
        compiler_params=pltpu.CompilerParams(
            dimension_semantics=("parallel", "arbitrary"), vmem_limit_bytes=VMEM_LIMIT),
        name="attn",
    )(q, ks, kw, vs0, vs1, vw0, vw1, kcmp, vc0, vc1, gt, ovl, gn)


def _conv_body(tiles_per_seq, cur_ref, halo_ref, w_ref, b_ref, lng_ref, lnb_ref, gn_ref, out_ref,
               buf_ref, sh_ref, cv_ref):
    first = (pl.program_id(0) % tiles_per_seq) == 0
    buf_ref[0:CONV_HALO, :] = jnp.where(first, 0.0, halo_ref[...])
    buf_ref[CONV_HALO:, :] = cur_ref[...]
    lead = CONV_HALO - (CONV_WIDTH - 1)
    ncb = CONV_CH // LANES
    nrb = CONV_TILE // CONV_ROWS
    for s in range(8):
        span = CONV_TILE + 8 * ((CONV_WIDTH - 1 - s) // 8)
        for cb in range(ncb):
            sh_ref[s, cb, 0:span, :] = buf_ref[lead + s:lead + s + span, cb * LANES:(cb + 1) * LANES]

    def conv_block(i, carry):
        cb = i % ncb
        base = pl.multiple_of((i // ncb) * CONV_ROWS, CONV_ROWS)
        acc = jnp.zeros((CONV_ROWS, LANES), F32) + b_ref[cb]
        for k in range(CONV_WIDTH):
            acc = acc + w_ref[cb, k:k + 1, :] * sh_ref[k % 8, cb, pl.ds(base + 8 * (k // 8), CONV_ROWS), :]
        cv_ref[cb, pl.ds(base, CONV_ROWS), :] = acc
        return carry

    lax.fori_loop(0, nrb * ncb, conv_block, 0)
    h = _lane_cat([cv_ref[cb] for cb in range(ncb)])
    mu = jnp.mean(h, axis=-1, keepdims=True)
    d = h - mu
    var = jnp.mean(d * d, axis=-1, keepdims=True)
    hn = d * lax.rsqrt(var + EPS) * lng_ref[...] + lnb_ref[...]
    o = hn * _sigmoid(hn)
    ms = jnp.mean(o * o, axis=-1, keepdims=True)
    out_ref[...] = (o * lax.rsqrt(ms + EPS) * gn_ref[...]).astype(BF16)


def _conv_call(hcv, w, b, lng, lnb, gn, seq):
    t = hcv.shape[0]
    nt = t // CONV_TILE
    halo_per_tile = CONV_TILE // CONV_HALO
    ncb = CONV_CH // LANES
    row = lambda i: (i, 0)
    const = lambda i: (0, 0)
    return pl.pallas_call(
        functools.partial(_conv_body, seq // CONV_TILE),
        grid=(nt,),
        in_specs=[
            pl.BlockSpec((CONV_TILE, CONV_CH), row),
            pl.BlockSpec((CONV_HALO, CONV_CH), lambda i: (jnp.maximum(i * halo_per_tile - 1, 0), 0)),
            pl.BlockSpec((ncb, CONV_HALO, LANES), lambda i: (0, 0, 0)),
            pl.BlockSpec((ncb, 1, LANES), lambda i: (0, 0, 0)),
            pl.BlockSpec((1, CONV_CH), const),
            pl.BlockSpec((1, CONV_CH), const),
            pl.BlockSpec((1, CONV_CH), const),
        ],
        out_specs=pl.BlockSpec((CONV_TILE, CONV_CH), row),
        out_shape=jax.ShapeDtypeStruct((t, CONV_CH), BF16),
        scratch_shapes=[pltpu.VMEM((CONV_HALO + CONV_TILE, CONV_CH), F32),
                        pltpu.VMEM((8, ncb, CONV_TILE + 8 * ((CONV_WIDTH - 1) // 8), LANES), F32),
                        pltpu.VMEM((ncb, CONV_TILE, LANES), F32)],
        compiler_params=pltpu.CompilerParams(
            dimension_semantics=("parallel",), vmem_limit_bytes=VMEM_LIMIT),
        name="conv",
    )(hcv, hcv, w, b, lng, lnb, gn)


def _ffn_body(x_ref, ma_ref, mb_ref, woa_ref, wob_ref, gf_ref, wg_ref, wu_ref, wd_ref, out_ref):
    h = x_ref[...] + _dot(ma_ref[...], woa_ref[...]) + _dot(mb_ref[...], wob_ref[...])
    ms = jnp.mean(h * h, axis=-1, keepdims=True)
    hn = (h * lax.rsqrt(ms + EPS) * gf_ref[...]).astype(BF16)
    acc = jnp.zeros_like(h)
    for j in range(FFN_HIDDEN // FFN_CHUNK):
        cols = slice(j * FFN_CHUNK, (j + 1) * FFN_CHUNK)
        gte = _dot(hn, wg_ref[:, cols])
        up = _dot(hn, wu_ref[:, cols])
        act = (gte * _sigmoid(gte) * up).astype(BF16)
        acc = acc + _dot(act, wd_ref[cols, :])
    out_ref[...] = h + acc


def _ffn_call(x2, ma, mb, woa, wob, gf, wg, wu, wd):
    t = x2.shape[0]
    nt = t // FFN_TILE
    row = lambda i: (i, 0)
    const = lambda i: (0, 0)
    once = pl.Buffered(1)
    return pl.pallas_call(
        _ffn_body,
        grid=(nt,),
        in_specs=[
            pl.BlockSpec((FFN_TILE, D_MODEL), row),
            pl.BlockSpec((FFN_TILE, NSA_WIDTH), row),
            pl.BlockSpec((FFN_TILE, CONV_CH), row),
            pl.BlockSpec((NSA_WIDTH, D_MODEL), const, pipeline_mode=once),
            pl.BlockSpec((CONV_CH, D_MODEL), const, pipeline_mode=once),
            pl.BlockSpec((1, D_MODEL), const),
            pl.BlockSpec((D_MODEL, FFN_HIDDEN), const, pipeline_mode=once),
            pl.BlockSpec((D_MODEL, FFN_HIDDEN), const, pipeline_mode=once),
            pl.BlockSpec((FFN_HIDDEN, D_MODEL), const, pipeline_mode=once),
        ],
        out_specs=pl.BlockSpec((FFN_TILE, D_MODEL), row),
        out_shape=jax.ShapeDtypeStruct((t, D_MODEL), F32),
        compiler_params=pltpu.CompilerParams(
            dimension_semantics=("parallel",), vmem_limit_bytes=VMEM_LIMIT),
        name="ffn",
    )(x2, ma, mb, woa, wob, gf, wg, wu, wd)


def _q_perm():
    return np.array([(g * GQA_REP + r) * HEAD_DIM + d
                     for r in range(GQA_REP) for g in range(NSA_KV_HEADS) for d in range(HEAD_DIM)])


def _rope_tables(pos):
    half = HEAD_DIM // 2
    inv = ROPE_THETA ** (-np.arange(half, dtype=np.float64) / half)
    ang = np.asarray(pos, np.float64)[:, None] * inv[None, :]
    cos = np.concatenate([np.cos(ang), np.cos(ang)], axis=1)
    sin = np.concatenate([-np.sin(ang), np.sin(ang)], axis=1)
    return (np.tile(cos, (1, NSA_KV_HEADS)).astype(np.float32),
            np.tile(sin, (1, NSA_KV_HEADS)).astype(np.float32))


def _block_diag_ones(width):
    idx = np.arange(width) // HEAD_DIM
    return (idx[:, None] == idx[None, :]).astype(np.float32)


def _overlap_t(seq):
    ncmp = (seq - CMP_LEN) // CMP_STRIDE + 1
    cs = np.arange(ncmp)[:, None] * CMP_STRIDE
    ss = np.arange(seq // SLC_BLOCK)[None, :] * SLC_BLOCK
    ov = np.clip(np.minimum(cs + CMP_LEN, ss + SLC_BLOCK) - np.maximum(cs, ss), 0, None) / CMP_LEN
    out = np.zeros((LANES, LANES), np.float32)
    out[:seq // SLC_BLOCK, :ncmp] = ov.T
    return out


def _cmp_weights(pe, w1, w2):
    pe2 = jnp.concatenate([pe, pe], axis=1)
    w1r = w1.reshape(CMP_LEN, HEAD_DIM, CMP_HIDDEN)
    z1 = jnp.zeros_like(w1r)
    w1b = jnp.concatenate([jnp.concatenate([w1r, z1], axis=2),
                           jnp.concatenate([z1, w1r], axis=2)], axis=1).astype(BF16)
    z2 = jnp.zeros_like(w2)
    w2b = jnp.concatenate([jnp.concatenate([w2, z2], axis=1),
                           jnp.concatenate([z2, w2], axis=1)], axis=0).astype(BF16)
    return pe2, w1b, w2b


def kernel(x, attn_norm_g, w_in, q_norm_g, k_norm_cmp_g, k_norm_slc_g, k_norm_win_g, cmp_pe_k, cmp_w1_k, cmp_w2_k, cmp_pe_v, cmp_w1_v, cmp_w2_v, conv_dw_w, conv_dw_b, conv_ln_g, conv_ln_b, out_norm_nsa_g, out_norm_conv_g, w_out, ffn_norm_g, w_gate_up, w_down):
    batch, seq, d_model = x.shape
    assert d_model == D_MODEL and seq % PROJ_TILE == 0 and seq // SLC_BLOCK == 32
    depth = w_in.shape[0]
    perm = _q_perm()
    cos_np, sin_np = _rope_tables(np.arange(seq))
    ccos_np, csin_np = _rope_tables(np.arange(seq // CMP_STRIDE) * CMP_STRIDE + CMP_LEN - 1)
    cos, sin, ccos, csin = map(jnp.asarray, (cos_np, sin_np, ccos_np, csin_np))
    bd256 = jnp.asarray(_block_diag_ones(256), BF16)
    bd128 = jnp.asarray(_block_diag_ones(LANES), BF16)
    ovl = jnp.asarray(_overlap_t(seq), BF16)

    x2 = x.reshape(batch * seq, d_model)
    for l in range(depth):
        w = w_in[l]
        o_q, o_kc, o_vc, o_ks, o_vs, o_kw, o_vw, o_gl, o_cv = (
            0, 512, 640, 768, 896, 1024, 1152, 1280, 1304)
        gl_pad = jnp.zeros((d_model, LANES - 3 * NSA_HEADS), w.dtype)
        w_cat = jnp.concatenate([
            w[:, perm],
            w[:, o_ks:o_ks + KV_WIDTH], w[:, o_kw:o_kw + KV_WIDTH],
            w[:, o_vs:o_vs + KV_WIDTH], w[:, o_vw:o_vw + KV_WIDTH],
            w[:, o_kc:o_kc + KV_WIDTH], w[:, o_vc:o_vc + KV_WIDTH],
            w[:, o_cv:o_cv + CONV_CH], w[:, o_cv + CONV_CH:o_cv + 2 * CONV_CH],
            w[:, o_gl:o_gl + 3 * NSA_HEADS], gl_pad], axis=1).astype(BF16)
        gq = jnp.tile(q_norm_g[l], 4)[None, :]
        gk = jnp.concatenate([jnp.tile(k_norm_slc_g[l], 2), jnp.tile(k_norm_win_g[l], 2)])[None, :]
        q, ks, kw, vs0, vs1, vw0, vw1, kc, vc, gt, hcv = _proj_call(
            x2, attn_norm_g[l][None, :], w_cat, cos, sin, gq, gk, bd256, seq)

        pek, w1k, w2k = _cmp_weights(cmp_pe_k[l], cmp_w1_k[l], cmp_w2_k[l])
        pev, w1v, w2v = _cmp_weights(cmp_pe_v[l], cmp_w1_v[l], cmp_w2_v[l])
        kcmp, vc0, vc1 = _cmp_call(kc, vc, pek, w1k, w2k, pev, w1v, w2v, ccos, csin,
                                jnp.tile(k_norm_cmp_g[l], 2)[None, :], bd128, batch, seq)

        mix_a = _attn_call(q, ks, kw, vs0, vs1, vw0, vw1, kcmp, vc0, vc1, gt, ovl,
                           out_norm_nsa_g[l][perm][None, :], batch, seq)

        w_dw = jnp.concatenate([conv_dw_w[l][:, 0, :],
                                jnp.zeros((CONV_HALO - CONV_WIDTH, CONV_CH), F32)], axis=0)
        w_dw = w_dw.reshape(CONV_HALO, CONV_CH // LANES, LANES).transpose(1, 0, 2)
        mix_b = _conv_call(hcv, w_dw, conv_dw_b[l].reshape(CONV_CH // LANES, 1, LANES),
                           conv_ln_g[l][None, :],
                           conv_ln_b[l][None, :], out_norm_conv_g[l][None, :], seq)

        wo = w_out[l]
        x2 = _ffn_call(x2, mix_a, mix_b, wo[:NSA_WIDTH][perm].astype(BF16),
                       wo[NSA_WIDTH:].astype(BF16), ffn_norm_g[l][None, :],
                       w_gate_up[l][:, :FFN_HIDDEN].astype(BF16),
                       w_gate_up[l][:, FFN_HIDDEN:].astype(BF16), w_down[l].astype(BF16))
    return x2.reshape(batch, seq, d_model)
```

```python
import functools

import numpy as np
import jax
import jax.numpy as jnp
from jax import lax
from jax.experimental import pallas as pl
from jax.experimental.pallas import tpu as pltpu

F32 = jnp.float32
BF16 = jnp.bfloat16

D_MODEL = 1024
HEAD_DIM = 64
NSA_HEADS = 8
NSA_KV_HEADS = 2
GQA_REP = NSA_HEADS // NSA_KV_HEADS
NSA_WIDTH = NSA_HEADS * HEAD_DIM
KV_WIDTH = NSA_KV_HEADS * HEAD_DIM
CONV_CH = D_MODEL - NSA_WIDTH
CMP_LEN = 32
CMP_STRIDE = 16
CMP_HIDDEN = 4 * HEAD_DIM
SLC_BLOCK = 64
SLC_TOPN = 8
WINDOW = 512
CONV_WIDTH = 31
FFN_HIDDEN = 2816
ROPE_THETA = 10000.0
EPS = 1e-6
NEG = -1e30
LOG2E = 1.4426950408889634
FORCE = 1e6

LANES = 128
Q_TILE = 128
SLC_CHUNK = 512
WIN_KEYS = WINDOW + Q_TILE
PROJ_TILE = 512
CONV_TILE = 512
CONV_HALO = 32
CONV_ROWS = 128
FFN_TILE = 512
FFN_CHUNK = 256
VMEM_LIMIT = 56 * 1024 * 1024

C_Q = 0
C_KS = 512
C_KW = 640
C_VS = 768
C_VW = 896
C_KC = 1024
C_VC = 1152
C_CA = 1280
C_CG = 1792
C_GL = 2304
PROJ_COLS = 2432

_TRANS_B = (((1,), (1,)), ((), ()))


def _dot(a, b):
    return jnp.dot(a, b, preferred_element_type=F32)


def _dot_tb(a, b):
    return lax.dot_general(a, b, _TRANS_B, preferred_element_type=F32)


def _split_bf16(x):
    hi = x.astype(BF16)
    lo = (x - hi.astype(F32)).astype(BF16)
    return hi, lo


def _sigmoid(x):
    return 1.0 / (1.0 + jnp.exp(-x))


def _head_norm_rope(y, gain, bd, cos, sin, first_half, scale):
    hi, lo = _split_bf16(y * y)
    ss = _dot(hi, bd) + _dot(lo, bd)
    yn = y * lax.rsqrt(ss * (1.0 / HEAD_DIM) + EPS) * gain
    outs = []
    for j in range(y.shape[1] // LANES):
        blk = yn[:, j * LANES:(j + 1) * LANES]
        rot = jnp.where(first_half, pltpu.roll(blk, LANES - HEAD_DIM // 2, 1),
                        pltpu.roll(blk, HEAD_DIM // 2, 1))
        outs.append((blk * cos + rot * sin) * scale)
    return outs


def _value_t_with_ones(v):
    vt = v.T
    row = lax.broadcasted_iota(jnp.int32, vt.shape, 0)
    return (jnp.where(row < HEAD_DIM, vt, 1.0).astype(BF16),
            jnp.where(row < HEAD_DIM, 1.0, vt).astype(BF16))


def _proj_body(x_ref, gin_ref, w_ref, cos_ref, sin_ref, gq_ref, gk_ref, bd_ref,
               q_ref, ks_ref, kw_ref, vs0_ref, vs1_ref, vw0_ref, vw1_ref, kc_ref, vc_ref, gt_ref,
               hcv_ref):
    x = x_ref[...]
    xg = (x * gin_ref[...]).astype(BF16)
    rs = lax.rsqrt(jnp.mean(x * x, axis=-1, keepdims=True) + EPS)
    cos = cos_ref[...]
    sin = sin_ref[...]
    bd = bd_ref[...]
    lane = lax.broadcasted_iota(jnp.int32, (PROJ_TILE, LANES), 1)
    first_half = (lane & (HEAD_DIM // 2)) == 0
    held = {}

    def queries(half, y):
        blks = _head_norm_rope(y, gq_ref[...], bd, cos, sin, first_half, HEAD_DIM ** -0.5 * LOG2E)
        for j in range(2):
            q_ref[:, (2 * half + j) * LANES:(2 * half + j + 1) * LANES] = blks[j].astype(BF16)

    def keys(y):
        kblks = _head_norm_rope(y, gk_ref[...], bd, cos, sin, first_half, 1.0)
        ks_ref[...] = kblks[0].astype(BF16)
        kw_ref[...] = kblks[1].astype(BF16)

    def values(y):
        for i in range(PROJ_TILE // LANES):
            rows = y[i * LANES:(i + 1) * LANES, :]
            vs0_ref[i], vs1_ref[i] = _value_t_with_ones(rows[:, :LANES])
            vw0_ref[i], vw1_ref[i] = _value_t_with_ones(rows[:, LANES:])

    def cmp_inputs(y):
        kc_ref[...] = y[:, :LANES]
        vc_ref[...] = y[:, LANES:]

    def conv_lin(y):
        held["ca"] = y

    def conv_glu(y):
        hcv_ref[...] = held["ca"] * _sigmoid(y)

    def gates(y):
        sg = _sigmoid(y)
        for i in range(PROJ_TILE // LANES):
            gt_ref[i] = sg[i * LANES:(i + 1) * LANES, :].T[0:32, :]

    sections = [
        (C_Q, 256, functools.partial(queries, 0)),
        (C_CA, CONV_CH, conv_lin),
        (C_Q + 256, 256, functools.partial(queries, 1)),
        (C_CG, CONV_CH, conv_glu),
        (C_KS, 256, keys),
        (C_VS, 256, values),
        (C_KC, 256, cmp_inputs),
        (C_GL, LANES, gates),
    ]

    def project(sec):
        c0, width, _ = sec
        return _dot(xg, w_ref[:, c0:c0 + width])

    y_next = project(sections[0])
    for i, sec in enumerate(sections):
        y_cur = y_next
        if i + 1 < len(sections):
            y_next = project(sections[i + 1])
        sec[2](y_cur * rs)


def _proj_call(x2, gin, w_cat, cos, sin, gq, gk, bd, seq):
    t = x2.shape[0]
    nt = t // PROJ_TILE
    tiles_per_seq = seq // PROJ_TILE
    row = lambda i: (i, 0)
    const = lambda i: (0, 0)
    tab = lambda i: (i % tiles_per_seq, 0)
    blk3 = lambda i: (i, 0, 0)
    nb = PROJ_TILE // LANES
    return pl.pallas_call(
        _proj_body,
        grid=(nt,),
        in_specs=[
            pl.BlockSpec((PROJ_TILE, D_MODEL), row),
            pl.BlockSpec((1, D_MODEL), const),
            pl.BlockSpec((D_MODEL, PROJ_COLS), const),
            pl.BlockSpec((PROJ_TILE, LANES), tab),
            pl.BlockSpec((PROJ_TILE, LANES), tab),
            pl.BlockSpec((1, 256), const),
            pl.BlockSpec((1, 256), const),
            pl.BlockSpec((256, 256), const),
        ],
        out_specs=[
            pl.BlockSpec((PROJ_TILE, NSA_WIDTH), row),
            pl.BlockSpec((PROJ_TILE, LANES), row),
            pl.BlockSpec((PROJ_TILE, LANES), row),
            pl.BlockSpec((nb, LANES, LANES), blk3),
            pl.BlockSpec((nb, LANES, LANES), blk3),
            pl.BlockSpec((nb, LANES, LANES), blk3),
            pl.BlockSpec((nb, LANES, LANES), blk3),
            pl.BlockSpec((PROJ_TILE, LANES), row),
            pl.BlockSpec((PROJ_TILE, LANES), row),
            pl.BlockSpec((nb, 32, LANES), blk3),
            pl.BlockSpec((PROJ_TILE, CONV_CH), row),
        ],
        out_shape=[
            jax.ShapeDtypeStruct((t, NSA_WIDTH), BF16),
            jax.ShapeDtypeStruct((t, LANES), BF16),
            jax.ShapeDtypeStruct((t, LANES), BF16),
            jax.ShapeDtypeStruct((t // LANES, LANES, LANES), BF16),
            jax.ShapeDtypeStruct((t // LANES, LANES, LANES), BF16),
            jax.ShapeDtypeStruct((t // LANES, LANES, LANES), BF16),
            jax.ShapeDtypeStruct((t // LANES, LANES, LANES), BF16),
            jax.ShapeDtypeStruct((t, LANES), F32),
            jax.ShapeDtypeStruct((t, LANES), F32),
            jax.ShapeDtypeStruct((t // LANES, 32, LANES), F32),
            jax.ShapeDtypeStruct((t, CONV_CH), F32),
        ],
        compiler_params=pltpu.CompilerParams(
            dimension_semantics=("parallel",), vmem_limit_bytes=VMEM_LIMIT),
        name="proj",
    )(x2, gin, w_cat, cos, sin, gq, gk, bd)


def _cmp_mlp(t_ref, pe_ref, w1_ref, w2_ref):
    half = CMP_LEN // 2
    nchunk = t_ref.shape[0] // CMP_STRIDE
    p = jnp.zeros((nchunk, 2 * CMP_HIDDEN), F32)
    q = jnp.zeros((nchunk, 2 * CMP_HIDDEN), F32)
    for l in range(half):
        xl = t_ref[pl.ds(l, nchunk, stride=CMP_STRIDE), :]
        p = p + _dot((xl + pe_ref[l:l + 1, :]).astype(BF16), w1_ref[l])
        q = q + _dot((xl + pe_ref[half + l:half + l + 1, :]).astype(BF16), w1_ref[half + l])
    h1 = p + pltpu.roll(q, nchunk - 1, 0)
    act = (h1 * _sigmoid(h1)).astype(BF16)
    return _dot(act, w2_ref[...])


def _cmp_body(kc_ref, vc_ref, pek_ref, w1k_ref, w2k_ref, pev_ref, w1v_ref, w2v_ref,
              cos_ref, sin_ref, gk_ref, bd_ref, ko_ref, vt0_ref, vt1_ref):
    kraw = _cmp_mlp(kc_ref, pek_ref, w1k_ref, w2k_ref)
    lane = lax.broadcasted_iota(jnp.int32, kraw.shape, 1)
    first_half = (lane & (HEAD_DIM // 2)) == 0
    (kr,) = _head_norm_rope(kraw, gk_ref[...], bd_ref[...], cos_ref[...], sin_ref[...],
                            first_half, 1.0)
    ko_ref[...] = kr.astype(BF16)
    vraw = _cmp_mlp(vc_ref, pev_ref, w1v_ref, w2v_ref)
    vt0_ref[0], vt1_ref[0] = _value_t_with_ones(vraw)


def _cmp_call(kc, vc, pek, w1k, w2k, pev, w1v, w2v, cos, sin, gk, bd, batch, seq):
    seqblk = lambda b: (b, 0)
    c2 = lambda b: (0, 0)
    c3 = lambda b: (0, 0, 0)
    ncmp_pad = seq // CMP_STRIDE
    return pl.pallas_call(
        _cmp_body,
        grid=(batch,),
        in_specs=[
            pl.BlockSpec((seq, LANES), seqblk),
            pl.BlockSpec((seq, LANES), seqblk),
            pl.BlockSpec((CMP_LEN, LANES), c2),
            pl.BlockSpec((CMP_LEN, LANES, 2 * CMP_HIDDEN), c3),
            pl.BlockSpec((2 * CMP_HIDDEN, LANES), c2),
            pl.BlockSpec((CMP_LEN, LANES), c2),
            pl.BlockSpec((CMP_LEN, LANES, 2 * CMP_HIDDEN), c3),
            pl.BlockSpec((2 * CMP_HIDDEN, LANES), c2),
            pl.BlockSpec((ncmp_pad, LANES), c2),
            pl.BlockSpec((ncmp_pad, LANES), c2),
            pl.BlockSpec((1, LANES), c2),
            pl.BlockSpec((LANES, LANES), c2),
        ],
        out_specs=[
            pl.BlockSpec((ncmp_pad, LANES), seqblk),
            pl.BlockSpec((1, LANES, ncmp_pad), lambda b: (b, 0, 0)),
            pl.BlockSpec((1, LANES, ncmp_pad), lambda b: (b, 0, 0)),
        ],
        out_shape=[
            jax.ShapeDtypeStruct((batch * ncmp_pad, LANES), BF16),
            jax.ShapeDtypeStruct((batch, LANES, ncmp_pad), BF16),
            jax.ShapeDtypeStruct((batch, LANES, ncmp_pad), BF16),
        ],
        compiler_params=pltpu.CompilerParams(
            dimension_semantics=("parallel",), vmem_limit_bytes=VMEM_LIMIT),
        name="compress",
    )(kc, vc, pek, w1k, w2k, pev, w1v, w2v, cos, sin, gk, bd)


def _lane_cat(blocks):
    return jnp.concatenate(blocks, axis=1)


def _softmax_numer(s, bias, m_run=None):
    es, ms = [], []
    for r in range(GQA_REP):
        sm = s[:, r * LANES:(r + 1) * LANES] + bias
        m = jnp.max(sm, axis=0, keepdims=True)
        if m_run is not None:
            m = jnp.maximum(m, m_run[:, r * LANES:(r + 1) * LANES])
        es.append(jnp.exp2(sm - m))
        ms.append(m)
    return _lane_cat(es), _lane_cat(ms)


def _safe_inv(l):
    return 1.0 / jnp.where(l > 0.0, l, 1.0)


def _attn_body(*refs):
    c = pl.program_id(1)
    n_max = refs[1].shape[0] // SLC_CHUNK
    for n in range(1, n_max + 1):
        pl.when((c * Q_TILE) // SLC_CHUNK + 1 == n)(functools.partial(_attn_tile, n, c, *refs))


def _attn_tile(n_chunks, c, q_ref, ks_ref, kw_ref, vs0_ref, vs1_ref, vw0_ref, vw1_ref, kc_ref, vc0_ref,
               vc1_ref, gt_ref, ovl_ref, gn_ref, out_ref):
    groups = range(NSA_KV_HEADS)
    vs_refs, vw_refs, vc_refs = (vs0_ref, vs1_ref), (vw0_ref, vw1_ref), (vc0_ref, vc1_ref)
    sum_row = (HEAD_DIM, 0)
    width = GQA_REP * LANES
    t_lane = c * Q_TILE + lax.broadcasted_iota(jnp.int32, (1, LANES), 1)
    lane_sq = lax.broadcasted_iota(jnp.int32, (Q_TILE, LANES), 1)
    nsel = 32
    j_idx = lax.broadcasted_iota(jnp.int32, (nsel, LANES), 0)
    cur = lax.shift_right_logical(t_lane, 6)
    n_idx = lax.broadcasted_iota(jnp.int32, (LANES, LANES), 0)
    cbias = jnp.where((n_idx * CMP_STRIDE + (CMP_LEN - 1) <= t_lane) & (n_idx < LANES - 1), 0.0, NEG)
    has_cmp = jnp.where(t_lane >= CMP_LEN - 1, 1.0, 0.0)
    has_cmp = _lane_cat([has_cmp] * GQA_REP)
    win_blk = jnp.maximum(c - WINDOW // Q_TILE, 0)
    win_start = pl.multiple_of(win_blk * Q_TILE, Q_TILE)
    kpos_w = win_start + lax.broadcasted_iota(jnp.int32, (WIN_KEYS, LANES), 0)
    wbias = jnp.where((kpos_w <= t_lane) & (kpos_w > t_lane - WINDOW), 0.0, NEG)
    row_sl = lax.broadcasted_iota(jnp.int32, (SLC_CHUNK, LANES), 0)
    zero_q = jnp.zeros((Q_TILE, LANES), BF16)

    def gate_row(g, branch):
        rows = [(g * GQA_REP + r) * 3 + branch for r in range(GQA_REP)]
        return _lane_cat([gt_ref[0, i:i + 1, :] for i in rows])

    qg = []
    for g in groups:
        in_g = (lane_sq >= HEAD_DIM) if g else (lane_sq < HEAD_DIM)
        qg.append(jnp.concatenate(
            [jnp.where(in_g, q_ref[:, r * LANES:(r + 1) * LANES], zero_q) for r in range(GQA_REP)],
            axis=0))

    o_cmp = [None] * NSA_KV_HEADS
    sel_bias = [None] * NSA_KV_HEADS

    def compressed_softmax(g, sc):
        ec, _ = _softmax_numer(sc, cbias)
        coef = has_cmp * _safe_inv(jnp.sum(ec, axis=0, keepdims=True))
        pc = ec * coef
        psum = (pc[:, 0:LANES] + pc[:, LANES:2 * LANES]
                + pc[:, 2 * LANES:3 * LANES] + pc[:, 3 * LANES:4 * LANES])
        p_hi, p_lo = _split_bf16(psum)
        imp = (_dot(ovl_ref[...], p_hi) + _dot(ovl_ref[...], p_lo))[0:nsel, :]
        valid = j_idx <= cur
        forced = (j_idx == 0) | (j_idx == cur) | (j_idx == cur - 1)
        score = jnp.where(valid, imp + jnp.where(forced, FORCE, 0.0), -FORCE)
        rank = jnp.zeros((nsel, LANES), F32)
        for jp in range(nsel):
            row = score[jp:jp + 1, :]
            beats = (row > score) | ((row == score) & (j_idx > jp))
            rank = rank + jnp.where(beats, 1.0, 0.0)
        sel_bias[g] = jnp.where((rank < float(SLC_TOPN)) & (score > -1.0), 0.0, NEG)
        return ec.astype(BF16), coef

    nsub = SLC_CHUNK // LANES
    blocks_per_chunk = SLC_CHUNK // SLC_BLOCK
    kwin = kw_ref[pl.ds(win_start, WIN_KEYS), :]
    units = [("cmp", g, 0) for g in groups] + [("win", g, 0) for g in groups]
    units += [("slc", g, kc) for kc in range(n_chunks) for g in groups]
    m_run = [None] * NSA_KV_HEADS
    acc_s = [None] * NSA_KV_HEADS
    acc_w = [None] * NSA_KV_HEADS

    def scores(unit):
        kind, g, kc = unit
        if kind == "cmp":
            keys = kc_ref[...]
        elif kind == "win":
            keys = kwin
        else:
            keys = ks_ref[kc * SLC_CHUNK:(kc + 1) * SLC_CHUNK, :]
        return _dot_tb(keys, qg[g])

    def softmax(unit, s):
        kind, g, kc = unit
        if kind == "cmp":
            return compressed_softmax(g, s)
        if kind == "win":
            e, _ = _softmax_numer(s, wbias)
            return e.astype(BF16), None
        j0 = kc * blocks_per_chunk
        bias = jnp.concatenate(
            [jnp.broadcast_to(sel_bias[g][j:j + 1, :], (SLC_BLOCK, LANES))
             for j in range(j0, j0 + blocks_per_chunk)], axis=0)
        if kc == n_chunks - 1:
            bias = jnp.where(kc * SLC_CHUNK + row_sl <= t_lane, bias, NEG)
        e, m_new = _softmax_numer(s, bias, m_run[g])
        alpha = None if kc == 0 else jnp.exp2(m_run[g] - m_new)
        m_run[g] = m_new
        return e.astype(BF16), alpha

    def weighted_values(unit, e, alpha):
        kind, g, kc = unit
        if kind == "cmp":
            o_cmp[g] = _dot(vc_refs[g][0], e) * (gate_row(g, 0) * alpha)
            return
        if kind == "win":
            vt = _lane_cat([vw_refs[g][win_blk + i] for i in range(WIN_KEYS // LANES)])
            acc_w[g] = _dot(vt, e)
            return
        vt = _lane_cat([vs_refs[g][nsub * kc + i] for i in range(nsub)])
        pv = _dot(vt, e)
        acc_s[g] = pv if alpha is None else acc_s[g] * alpha + pv

    s_next = scores(units[0])
    pending = None
    for i, unit in enumerate(units):
        s_cur = s_next
        if i + 1 < len(units):
            s_next = scores(units[i + 1])
        e, alpha = softmax(unit, s_cur)
        if pending is not None:
            weighted_values(*pending)
        pending = (unit, e, alpha)
    weighted_values(*pending)

    o_groups = []
    for g in groups:
        r0 = sum_row[g]
        o_groups.append(o_cmp[g]
                        + acc_s[g] * (gate_row(g, 1) * _safe_inv(acc_s[g][r0:r0 + 1, :]))
                        + acc_w[g] * (gate_row(g, 2) * _safe_inv(acc_w[g][r0:r0 + 1, :])))

    row_d = lax.broadcasted_iota(jnp.int32, (LANES, width), 0)
    o_t = jnp.where(row_d < HEAD_DIM, o_groups[0], o_groups[1])
    blks = [o_t[:, r * LANES:(r + 1) * LANES].T for r in range(GQA_REP)]
    ss = blks[0] * blks[0]
    for r in range(1, GQA_REP):
        ss = ss + blks[r] * blks[r]
    rs = lax.rsqrt(jnp.sum(ss, axis=-1, keepdims=True) * (1.0 / NSA_WIDTH) + EPS)
    for r in range(GQA_REP):
        out_ref[:, r * LANES:(r + 1) * LANES] = (
            blks[r] * rs * gn_ref[:, r * LANES:(r + 1) * LANES]).astype(BF16)


def _attn_call(q, ks, kw, vs0, vs1, vw0, vw1, kcmp, vc0, vc1, gt, ovl, gn, batch, seq):
    nq = seq // Q_TILE
    nkb = seq // LANES
    qrow = lambda b, c: (b * nq + c, 0)
    per_b = lambda b, c: (b, 0)
    per_b3 = lambda b, c: (b, 0, 0)
    vspec = pl.BlockSpec((nkb, LANES, LANES), per_b3)
    cspec = pl.BlockSpec((1, LANES, LANES), per_b3)
    return pl.pallas_call(
        _attn_body,
        grid=(batch, nq),
        in_specs=[
            pl.BlockSpec((Q_TILE, NSA_WIDTH), qrow),
            pl.BlockSpec((seq, LANES), per_b),
            pl.BlockSpec((seq, LANES), per_b),
            vspec, vspec, vspec, vspec,
            pl.BlockSpec((LANES, LANES), per_b),
            cspec, cspec,
            pl.BlockSpec((1, 32, LANES), lambda b, c: (b * nq + c, 0, 0)),
            pl.BlockSpec((LANES, LANES), lambda b, c: (0, 0)),
            pl.BlockSpec((1, NSA_WIDTH), lambda b, c: (0, 0)),
        ],
        out_specs=pl.BlockSpec((Q_TILE, NSA_WIDTH), qrow),
        out_shape=jax.ShapeDtypeStruct((batch * seq, NSA_WIDTH), BF16),
        compiler_params=pltpu.CompilerParams(
            dimension_semantics=("parallel", "arbitrary"), vmem_limit_bytes=VMEM_LIMIT),
        name="attn",
    )(q, ks, kw, vs0, vs1, vw0, vw1, kcmp, vc0, vc1, gt, ovl, gn)


def _conv_body(tiles_per_seq, cur_ref, halo_ref, w_ref, b_ref, lng_ref, lnb_ref, gn_ref, out_ref,
               buf_ref, sh_ref, cv_ref):
    first = (pl.program_id(0) % tiles_per_seq) == 0
    buf_ref[0:CONV_HALO, :] = jnp.where(first, 0.0, halo_ref[...])
    buf_ref[CONV_HALO:, :] = cur_ref[...]
    lead = CONV_HALO - (CONV_WIDTH - 1)
    ncb = CONV_CH // LANES
    nrb = CONV_TILE // CONV_ROWS
    for s in range(8):
        span = CONV_TILE + 8 * ((CONV_WIDTH - 1 - s) // 8)
        for cb in range(ncb):
            sh_ref[s, cb, 0:span, :] = buf_ref[lead + s:lead + s + span, cb * LANES:(cb + 1) * LANES]

    def conv_block(i, carry):
        cb = i % ncb
        base = pl.multiple_of((i // ncb) * CONV_ROWS, CONV_ROWS)
        acc = jnp.zeros((CONV_ROWS, LANES), F32) + b_ref[cb]
        for k in range(CONV_WIDTH):
            acc = acc + w_ref[cb, k:k + 1, :] * sh_ref[k % 8, cb, pl.ds(base + 8 * (k // 8), CONV_ROWS), :]
        cv_ref[cb, pl.ds(base, CONV_ROWS), :] = acc
        return carry

    lax.fori_loop(0, nrb * ncb, conv_block, 0)
    h = _lane_cat([cv_ref[cb] for cb in range(ncb)])
    mu = jnp.mean(h, axis=-1, keepdims=True)
    d = h - mu
    var = jnp.mean(d * d, axis=-1, keepdims=True)
    hn = d * lax.rsqrt(var + EPS) * lng_ref[...] + lnb_ref[...]
    o = hn * _sigmoid(hn)
    ms = jnp.mean(o * o, axis=-1, keepdims=True)
    out_ref[...] = (o * lax.rsqrt(ms + EPS) * gn_ref[...]).astype(BF16)


def _conv_call(hcv, w, b, lng, lnb, gn, seq):
    t = hcv.shape[0]
    nt = t // CONV_TILE
    halo_per_tile = CONV_TILE // CONV_HALO
    ncb = CONV_CH // LANES
    row = lambda i: (i, 0)
    const = lambda i: (0, 0)
    return pl.pallas_call(
        functools.partial(_conv_body, seq // CONV_TILE),
        grid=(nt,),
        in_specs=[
            pl.BlockSpec((CONV_TILE, CONV_CH), row),
            pl.BlockSpec((CONV_HALO, CONV_CH), lambda i: (jnp.maximum(i * halo_per_tile - 1, 0), 0)),
            pl.BlockSpec((ncb, CONV_HALO, LANES), lambda i: (0, 0, 0)),
            pl.BlockSpec((ncb, 1, LANES), lambda i: (0, 0, 0)),
            pl.BlockSpec((1, CONV_CH), const),
            pl.BlockSpec((1, CONV_CH), const),
            pl.BlockSpec((1, CONV_CH), const),
        ],
        out_specs=pl.BlockSpec((CONV_TILE, CONV_CH), row),
        out_shape=jax.ShapeDtypeStruct((t, CONV_CH), BF16),
        scratch_shapes=[pltpu.VMEM((CONV_HALO + CONV_TILE, CONV_CH), F32),
                        pltpu.VMEM((8, ncb, CONV_TILE + 8 * ((CONV_WIDTH - 1) // 8), LANES), F32),
                        pltpu.VMEM((ncb, CONV_TILE, LANES), F32)],
        compiler_params=pltpu.CompilerParams(
            dimension_semantics=("parallel",), vmem_limit_bytes=VMEM_LIMIT),
        name="conv",
    )(hcv, hcv, w, b, lng, lnb, gn)


def _ffn_body(x_ref, ma_ref, mb_ref, woa_ref, wob_ref, gf_ref, wg_ref, wu_ref, wd_ref, out_ref):
    h = x_ref[...] + _dot(ma_ref[...], woa_ref[...]) + _dot(mb_ref[...], wob_ref[...])
    ms = jnp.mean(h * h, axis=-1, keepdims=True)
    hn = (h * lax.rsqrt(ms + EPS) * gf_ref[...]).astype(BF16)
    acc = jnp.zeros_like(h)
    for j in range(FFN_HIDDEN // FFN_CHUNK):
        cols = slice(j * FFN_CHUNK, (j + 1) * FFN_CHUNK)
        gte = _dot(hn, wg_ref[:, cols])
        up = _dot(hn, wu_ref[:, cols])
        act = (gte * _sigmoid(gte) * up).astype(BF16)
        acc = acc + _dot(act, wd_ref[cols, :])
    out_ref[...] = h + acc


def _ffn_call(x2, ma, mb, woa, wob, gf, wg, wu, wd):
    t = x2.shape[0]
    nt = t // FFN_TILE
    row = lambda i: (i, 0)
    const = lambda i: (0, 0)
    once = pl.Buffered(1)
    return pl.pallas_call(
        _ffn_body,
        grid=(nt,),
        in_specs=[
            pl.BlockSpec((FFN_TILE, D_MODEL), row),
            pl.BlockSpec((FFN_TILE, NSA_WIDTH), row),
            pl.BlockSpec((FFN_TILE, CONV_CH), row),
            pl.BlockSpec((NSA_WIDTH, D_MODEL), const, pipeline_mode=once),
            pl.BlockSpec((CONV_CH, D_MODEL), const, pipeline_mode=once),
            pl.BlockSpec((1, D_MODEL), const),
            pl.BlockSpec((D_MODEL, FFN_HIDDEN), const, pipeline_mode=once),
            pl.BlockSpec((D_MODEL, FFN_HIDDEN), const, pipeline_mode=once),
            pl.BlockSpec((FFN_HIDDEN, D_MODEL), const, pipeline_mode=once),
        ],
        out_specs=pl.BlockSpec((FFN_TILE, D_MODEL), row),
        out_shape=jax.ShapeDtypeStruct((t, D_MODEL), F32),
        compiler_params=pltpu.CompilerParams(
            dimension_semantics=("parallel",), vmem_limit_bytes=VMEM_LIMIT),
        name="ffn",
    )(x2, ma, mb, woa, wob, gf, wg, wu, wd)


def _pair_heads(a, axis):
    shape = a.shape
    a = a.reshape(shape[:axis] + (NSA_KV_HEADS, GQA_REP, HEAD_DIM) + shape[axis + 1:])
    return jnp.swapaxes(a, axis, axis + 1).reshape(shape)


def _rope_tables(pos):
    half = HEAD_DIM // 2
    inv = ROPE_THETA ** (-np.arange(half, dtype=np.float64) / half)
    ang = np.asarray(pos, np.float64)[:, None] * inv[None, :]
    cos = np.concatenate([np.cos(ang), np.cos(ang)], axis=1)
    sin = np.concatenate([-np.sin(ang), np.sin(ang)], axis=1)
    return (np.tile(cos, (1, NSA_KV_HEADS)).astype(np.float32),
            np.tile(sin, (1, NSA_KV_HEADS)).astype(np.float32))


def _block_diag_ones(width):
    idx = np.arange(width) // HEAD_DIM
    return (idx[:, None] == idx[None, :]).astype(np.float32)


def _overlap_t(seq):
    ncmp = (seq - CMP_LEN) // CMP_STRIDE + 1
    cs = np.arange(ncmp)[:, None] * CMP_STRIDE
    ss = np.arange(seq // SLC_BLOCK)[None, :] * SLC_BLOCK
    ov = np.clip(np.minimum(cs + CMP_LEN, ss + SLC_BLOCK) - np.maximum(cs, ss), 0, None) / CMP_LEN
    out = np.zeros((LANES, LANES), np.float32)
    out[:seq // SLC_BLOCK, :ncmp] = ov.T
    return out


def _cmp_weights(pe, w1, w2):
    pe2 = jnp.concatenate([pe, pe], axis=1)
    w1r = w1.reshape(CMP_LEN, HEAD_DIM, CMP_HIDDEN)
    z1 = jnp.zeros_like(w1r)
    w1b = jnp.concatenate([jnp.concatenate([w1r, z1], axis=2),
                           jnp.concatenate([z1, w1r], axis=2)], axis=1).astype(BF16)
    z2 = jnp.zeros_like(w2)
    w2b = jnp.concatenate([jnp.concatenate([w2, z2], axis=1),
                           jnp.concatenate([z2, w2], axis=1)], axis=0).astype(BF16)
    return pe2, w1b, w2b


def kernel(x, attn_norm_g, w_in, q_norm_g, k_norm_cmp_g, k_norm_slc_g, k_norm_win_g, cmp_pe_k, cmp_w1_k, cmp_w2_k, cmp_pe_v, cmp_w1_v, cmp_w2_v, conv_dw_w, conv_dw_b, conv_ln_g, conv_ln_b, out_norm_nsa_g, out_norm_conv_g, w_out, ffn_norm_g, w_gate_up, w_down):
    batch, seq, d_model = x.shape
    assert d_model == D_MODEL and seq % PROJ_TILE == 0 and seq // SLC_BLOCK == 32
    depth = w_in.shape[0]
    cos_np, sin_np = _rope_tables(np.arange(seq))
    ccos_np, csin_np = _rope_tables(np.arange(seq // CMP_STRIDE) * CMP_STRIDE + CMP_LEN - 1)
    cos, sin, ccos, csin = map(jnp.asarray, (cos_np, sin_np, ccos_np, csin_np))
    bd256 = jnp.asarray(_block_diag_ones(256), BF16)
    bd128 = jnp.asarray(_block_diag_ones(LANES), BF16)
    ovl = jnp.asarray(_overlap_t(seq), BF16)

    x2 = x.reshape(batch * seq, d_model)
    for l in range(depth):
        w = w_in[l]
        o_q, o_kc, o_vc, o_ks, o_vs, o_kw, o_vw, o_gl, o_cv = (
            0, 512, 640, 768, 896, 1024, 1152, 1280, 1304)
        gl_pad = jnp.zeros((d_model, LANES - 3 * NSA_HEADS), BF16)
        w_q = _pair_heads(w[:, o_q:o_q + NSA_WIDTH], 1)
        w_cat = jnp.concatenate([piece.astype(BF16) for piece in (
            w_q,
            w[:, o_ks:o_ks + KV_WIDTH], w[:, o_kw:o_kw + KV_WIDTH],
            w[:, o_vs:o_vs + KV_WIDTH], w[:, o_vw:o_vw + KV_WIDTH],
            w[:, o_kc:o_kc + KV_WIDTH], w[:, o_vc:o_vc + KV_WIDTH],
            w[:, o_cv:o_cv + CONV_CH], w[:, o_cv + CONV_CH:o_cv + 2 * CONV_CH],
            w[:, o_gl:o_gl + 3 * NSA_HEADS], gl_pad)], axis=1)
        gq = jnp.tile(q_norm_g[l], 4)[None, :]
        gk = jnp.concatenate([jnp.tile(k_norm_slc_g[l], 2), jnp.tile(k_norm_win_g[l], 2)])[None, :]
        q, ks, kw, vs0, vs1, vw0, vw1, kc, vc, gt, hcv = _proj_call(
            x2, attn_norm_g[l][None, :], w_cat, cos, sin, gq, gk, bd256, seq)

        pek, w1k, w2k = _cmp_weights(cmp_pe_k[l], cmp_w1_k[l], cmp_w2_k[l])
        pev, w1v, w2v = _cmp_weights(cmp_pe_v[l], cmp_w1_v[l], cmp_w2_v[l])
        kcmp, vc0, vc1 = _cmp_call(kc, vc, pek, w1k, w2k, pev, w1v, w2v, ccos, csin,
                                jnp.tile(k_norm_cmp_g[l], 2)[None, :], bd128, batch, seq)

        mix_a = _attn_call(q, ks, kw, vs0, vs1, vw0, vw1, kcmp, vc0, vc1, gt, ovl,
                           _pair_heads(out_norm_nsa_g[l], 0)[None, :], batch, seq)

        w_dw = jnp.concatenate([conv_dw_w[l][:, 0, :],
                                jnp.zeros((CONV_HALO - CONV_WIDTH, CONV_CH), F32)], axis=0)
        w_dw = w_dw.reshape(CONV_HALO, CONV_CH // LANES, LANES).transpose(1, 0, 2)
        mix_b = _conv_call(hcv, w_dw, conv_dw_b[l].reshape(CONV_CH // LANES, 1, LANES),
                           conv_ln_g[l][None, :],
                           conv_ln_b[l][None, :], out_norm_conv_g[l][None, :], seq)

        wo = w_out[l]
        x2 = _ffn_call(x2, mix_a, mix_b, _pair_heads(wo[:NSA_WIDTH], 0).astype(BF16),
                       wo[NSA_WIDTH:].astype(BF16), ffn_norm_g[l][None, :],
                       w_gate_up[l][:, :FFN_HIDDEN].astype(BF16),
                       w_gate_up[l][:, FFN_HIDDEN:].astype(BF16), w_down[l].astype(BF16))
    return x2.reshape(batch, seq, d_model)
```

```python
import functools

import numpy as np
import jax
import jax.numpy as jnp
from jax import lax
from jax.experimental import pallas as pl
from jax.experimental.pallas import tpu as pltpu

F32 = jnp.float32
BF16 = jnp.bfloat16

D_MODEL = 1024
HEAD_DIM = 64
NSA_HEADS = 8
NSA_KV_HEADS = 2
GQA_REP = NSA_HEADS // NSA_KV_HEADS
NSA_WIDTH = NSA_HEADS * HEAD_DIM
KV_WIDTH = NSA_KV_HEADS * HEAD_DIM
CONV_CH = D_MODEL - NSA_WIDTH
CMP_LEN = 32
CMP_STRIDE = 16
CMP_HIDDEN = 4 * HEAD_DIM
SLC_BLOCK = 64
SLC_TOPN = 8
WINDOW = 512
CONV_WIDTH = 31
FFN_HIDDEN = 2816
ROPE_THETA = 10000.0
EPS = 1e-6
NEG = -1e30
LOG2E = 1.4426950408889634
FORCE = 1e6

LANES = 128
Q_TILE = 128
SLC_CHUNK = 512
WIN_KEYS = WINDOW + Q_TILE
PROJ_TILE = 512
CONV_TILE = 512
CONV_HALO = 32
CONV_ROWS = 128
FFN_TILE = 512
FFN_CHUNK = 256
VMEM_LIMIT = 56 * 1024 * 1024

C_Q = 0
C_KS = 512
C_KW = 640
C_VS = 768
C_VW = 896
C_KC = 1024
C_VC = 1152
C_CA = 1280
C_CG = 1792
C_GL = 2304
PROJ_COLS = 2432

_TRANS_B = (((1,), (1,)), ((), ()))


def _dot(a, b):
    return jnp.dot(a, b, preferred_element_type=F32)


def _dot_tb(a, b):
    return lax.dot_general(a, b, _TRANS_B, preferred_element_type=F32)


def _split_bf16(x):
    hi = x.astype(BF16)
    lo = (x - hi.astype(F32)).astype(BF16)
    return hi, lo


def _sigmoid(x):
    return 1.0 / (1.0 + jnp.exp(-x))


def _head_norm_rope(y, gain, bd, cos, sin, first_half, scale):
    hi, lo = _split_bf16(y * y)
    ss = _dot(hi, bd) + _dot(lo, bd)
    yn = y * lax.rsqrt(ss * (1.0 / HEAD_DIM) + EPS) * gain
    outs = []
    for j in range(y.shape[1] // LANES):
        blk = yn[:, j * LANES:(j + 1) * LANES]
        rot = jnp.where(first_half, pltpu.roll(blk, LANES - HEAD_DIM // 2, 1),
                        pltpu.roll(blk, HEAD_DIM // 2, 1))
        outs.append((blk * cos + rot * sin) * scale)
    return outs


def _value_t_with_ones(v):
    vt = v.T
    row = lax.broadcasted_iota(jnp.int32, vt.shape, 0)
    return (jnp.where(row < HEAD_DIM, vt, 1.0).astype(BF16),
            jnp.where(row < HEAD_DIM, 1.0, vt).astype(BF16))


def _proj_body(x_ref, gin_ref, w_ref, cos_ref, sin_ref, gq_ref, gk_ref, bd_ref,
               q_ref, ks_ref, kw_ref, vs0_ref, vs1_ref, vw0_ref, vw1_ref, kc_ref, vc_ref, gt_ref,
               hcv_ref):
    x = x_ref[...]
    xg = (x * gin_ref[...]).astype(BF16)
    rs = lax.rsqrt(jnp.mean(x * x, axis=-1, keepdims=True) + EPS)
    cos = cos_ref[...]
    sin = sin_ref[...]
    bd = bd_ref[...]
    lane = lax.broadcasted_iota(jnp.int32, (PROJ_TILE, LANES), 1)
    first_half = (lane & (HEAD_DIM // 2)) == 0
    held = {}

    def queries(half, y):
        blks = _head_norm_rope(y, gq_ref[...], bd, cos, sin, first_half, HEAD_DIM ** -0.5 * LOG2E)
        for j in range(2):
            q_ref[:, (2 * half + j) * LANES:(2 * half + j + 1) * LANES] = blks[j].astype(BF16)

    def keys(y):
        kblks = _head_norm_rope(y, gk_ref[...], bd, cos, sin, first_half, 1.0)
        ks_ref[...] = kblks[0].astype(BF16)
        kw_ref[...] = kblks[1].astype(BF16)

    def values(y):
        for i in range(PROJ_TILE // LANES):
            rows = y[i * LANES:(i + 1) * LANES, :]
            vs0_ref[i], vs1_ref[i] = _value_t_with_ones(rows[:, :LANES])
            vw0_ref[i], vw1_ref[i] = _value_t_with_ones(rows[:, LANES:])

    def cmp_inputs(y):
        kc_ref[...] = y[:, :LANES]
        vc_ref[...] = y[:, LANES:]

    def conv_lin(y):
        held["ca"] = y

    def conv_glu(y):
        hcv_ref[...] = held["ca"] * _sigmoid(y)

    def gates(y):
        sg = _sigmoid(y)
        for i in range(PROJ_TILE // LANES):
            gt_ref[i] = sg[i * LANES:(i + 1) * LANES, :].T[0:32, :]

    sections = [
        (C_Q, 256, functools.partial(queries, 0)),
        (C_CA, CONV_CH, conv_lin),
        (C_Q + 256, 256, functools.partial(queries, 1)),
        (C_CG, CONV_CH, conv_glu),
        (C_KS, 256, keys),
        (C_VS, 256, values),
        (C_KC, 256, cmp_inputs),
        (C_GL, LANES, gates),
    ]

    def project(sec):
        c0, width, _ = sec
        return _dot(xg, w_ref[:, c0:c0 + width])

    y_next = project(sections[0])
    for i, sec in enumerate(sections):
        y_cur = y_next
        if i + 1 < len(sections):
            y_next = project(sections[i + 1])
        sec[2](y_cur * rs)


def _proj_call(x2, gin, w_cat, cos, sin, gq, gk, bd, seq):
    t = x2.shape[0]
    nt = t // PROJ_TILE
    tiles_per_seq = seq // PROJ_TILE
    row = lambda i: (i, 0)
    const = lambda i: (0, 0)
    tab = lambda i: (i % tiles_per_seq, 0)
    blk3 = lambda i: (i, 0, 0)
    nb = PROJ_TILE // LANES
    return pl.pallas_call(
        _proj_body,
        grid=(nt,),
        in_specs=[
            pl.BlockSpec((PROJ_TILE, D_MODEL), row),
            pl.BlockSpec((1, D_MODEL), const),
            pl.BlockSpec((D_MODEL, PROJ_COLS), const),
            pl.BlockSpec((PROJ_TILE, LANES), tab),
            pl.BlockSpec((PROJ_TILE, LANES), tab),
            pl.BlockSpec((1, 256), const),
            pl.BlockSpec((1, 256), const),
            pl.BlockSpec((256, 256), const),
        ],
        out_specs=[
            pl.BlockSpec((PROJ_TILE, NSA_WIDTH), row),
            pl.BlockSpec((PROJ_TILE, LANES), row),
            pl.BlockSpec((PROJ_TILE, LANES), row),
            pl.BlockSpec((nb, LANES, LANES), blk3),
            pl.BlockSpec((nb, LANES, LANES), blk3),
            pl.BlockSpec((nb, LANES, LANES), blk3),
            pl.BlockSpec((nb, LANES, LANES), blk3),
            pl.BlockSpec((PROJ_TILE, LANES), row),
            pl.BlockSpec((PROJ_TILE, LANES), row),
            pl.BlockSpec((nb, 32, LANES), blk3),
            pl.BlockSpec((PROJ_TILE, CONV_CH), row),
        ],
        out_shape=[
            jax.ShapeDtypeStruct((t, NSA_WIDTH), BF16),
            jax.ShapeDtypeStruct((t, LANES), BF16),
            jax.ShapeDtypeStruct((t, LANES), BF16),
            jax.ShapeDtypeStruct((t // LANES, LANES, LANES), BF16),
            jax.ShapeDtypeStruct((t // LANES, LANES, LANES), BF16),
            jax.ShapeDtypeStruct((t // LANES, LANES, LANES), BF16),
            jax.ShapeDtypeStruct((t // LANES, LANES, LANES), BF16),
            jax.ShapeDtypeStruct((t, LANES), F32),
            jax.ShapeDtypeStruct((t, LANES), F32),
            jax.ShapeDtypeStruct((t // LANES, 32, LANES), F32),
            jax.ShapeDtypeStruct((t, CONV_CH), F32),
        ],
        compiler_params=pltpu.CompilerParams(
            dimension_semantics=("parallel",), vmem_limit_bytes=VMEM_LIMIT),
        name="proj",
    )(x2, gin, w_cat, cos, sin, gq, gk, bd)


def _cmp_mlp(t_ref, pe_ref, w1_ref, w2_ref):
    half = CMP_LEN // 2
    nchunk = t_ref.shape[0] // CMP_STRIDE
    p = jnp.zeros((nchunk, 2 * CMP_HIDDEN), F32)
    q = jnp.zeros((nchunk, 2 * CMP_HIDDEN), F32)
    for l in range(half):
        xl = t_ref[pl.ds(l, nchunk, stride=CMP_STRIDE), :]
        p = p + _dot((xl + pe_ref[l:l + 1, :]).astype(BF16), w1_ref[l])
        q = q + _dot((xl + pe_ref[half + l:half + l + 1, :]).astype(BF16), w1_ref[half + l])
    h1 = p + pltpu.roll(q, nchunk - 1, 0)
    act = (h1 * _sigmoid(h1)).astype(BF16)
    return _dot(act, w2_ref[...])


def _cmp_body(kc_ref, vc_ref, pek_ref, w1k_ref, w2k_ref, pev_ref, w1v_ref, w2v_ref,
              cos_ref, sin_ref, gk_ref, bd_ref, ko_ref, vt0_ref, vt1_ref):
    kraw = _cmp_mlp(kc_ref, pek_ref, w1k_ref, w2k_ref)
    lane = lax.broadcasted_iota(jnp.int32, kraw.shape, 1)
    first_half = (lane & (HEAD_DIM // 2)) == 0
    (kr,) = _head_norm_rope(kraw, gk_ref[...], bd_ref[...], cos_ref[...], sin_ref[...],
                            first_half, 1.0)
    ko_ref[...] = kr.astype(BF16)
    vraw = _cmp_mlp(vc_ref, pev_ref, w1v_ref, w2v_ref)
    vt0_ref[0], vt1_ref[0] = _value_t_with_ones(vraw)


def _cmp_call(kc, vc, pek, w1k, w2k, pev, w1v, w2v, cos, sin, gk, bd, batch, seq):
    seqblk = lambda b: (b, 0)
    c2 = lambda b: (0, 0)
    c3 = lambda b: (0, 0, 0)
    ncmp_pad = seq // CMP_STRIDE
    return pl.pallas_call(
        _cmp_body,
        grid=(batch,),
        in_specs=[
            pl.BlockSpec((seq, LANES), seqblk),
            pl.BlockSpec((seq, LANES), seqblk),
            pl.BlockSpec((CMP_LEN, LANES), c2),
            pl.BlockSpec((CMP_LEN, LANES, 2 * CMP_HIDDEN), c3),
            pl.BlockSpec((2 * CMP_HIDDEN, LANES), c2),
            pl.BlockSpec((CMP_LEN, LANES), c2),
            pl.BlockSpec((CMP_LEN, LANES, 2 * CMP_HIDDEN), c3),
            pl.BlockSpec((2 * CMP_HIDDEN, LANES), c2),
            pl.BlockSpec((ncmp_pad, LANES), c2),
            pl.BlockSpec((ncmp_pad, LANES), c2),
            pl.BlockSpec((1, LANES), c2),
            pl.BlockSpec((LANES, LANES), c2),
        ],
        out_specs=[
            pl.BlockSpec((ncmp_pad, LANES), seqblk),
            pl.BlockSpec((1, LANES, ncmp_pad), lambda b: (b, 0, 0)),
            pl.BlockSpec((1, LANES, ncmp_pad), lambda b: (b, 0, 0)),
        ],
        out_shape=[
            jax.ShapeDtypeStruct((batch * ncmp_pad, LANES), BF16),
            jax.ShapeDtypeStruct((batch, LANES, ncmp_pad), BF16),
            jax.ShapeDtypeStruct((batch, LANES, ncmp_pad), BF16),
        ],
        compiler_params=pltpu.CompilerParams(
            dimension_semantics=("parallel",), vmem_limit_bytes=VMEM_LIMIT),
        name="compress",
    )(kc, vc, pek, w1k, w2k, pev, w1v, w2v, cos, sin, gk, bd)


def _lane_cat(blocks):
    return jnp.concatenate(blocks, axis=1)


def _softmax_numer(s, bias, m_run=None):
    es, ms = [], []
    for r in range(GQA_REP):
        sm = s[:, r * LANES:(r + 1) * LANES] + bias
        m = jnp.max(sm, axis=0, keepdims=True)
        if m_run is not None:
            m = jnp.maximum(m, m_run[:, r * LANES:(r + 1) * LANES])
        es.append(jnp.exp2(sm - m))
        ms.append(m)
    return _lane_cat(es), _lane_cat(ms)


def _safe_inv(l):
    return 1.0 / jnp.where(l > 0.0, l, 1.0)


def _attn_body(*refs):
    c = pl.program_id(1)
    n_max = refs[1].shape[0] // SLC_CHUNK
    for n in range(1, n_max + 1):
        pl.when((c * Q_TILE) // SLC_CHUNK + 1 == n)(functools.partial(_attn_tile, n, c, *refs))


def _attn_tile(n_chunks, c, q_ref, ks_ref, kw_ref, vs0_ref, vs1_ref, vw0_ref, vw1_ref, kc_ref, vc0_ref,
               vc1_ref, gt_ref, ovl_ref, gn_ref, out_ref):
    groups = range(NSA_KV_HEADS)
    vs_refs, vw_refs, vc_refs = (vs0_ref, vs1_ref), (vw0_ref, vw1_ref), (vc0_ref, vc1_ref)
    sum_row = (HEAD_DIM, 0)
    width = GQA_REP * LANES
    t_lane = c * Q_TILE + lax.broadcasted_iota(jnp.int32, (1, LANES), 1)
    lane_sq = lax.broadcasted_iota(jnp.int32, (Q_TILE, LANES), 1)
    nsel = 32
    j_idx = lax.broadcasted_iota(jnp.int32, (nsel, LANES), 0)
    cur = lax.shift_right_logical(t_lane, 6)
    n_idx = lax.broadcasted_iota(jnp.int32, (LANES, LANES), 0)
    cbias = jnp.where((n_idx * CMP_STRIDE + (CMP_LEN - 1) <= t_lane) & (n_idx < LANES - 1), 0.0, NEG)
    has_cmp = jnp.where(t_lane >= CMP_LEN - 1, 1.0, 0.0)
    has_cmp = _lane_cat([has_cmp] * GQA_REP)
    win_blk = jnp.maximum(c - WINDOW // Q_TILE, 0)
    win_start = pl.multiple_of(win_blk * Q_TILE, Q_TILE)
    kpos_w = win_start + lax.broadcasted_iota(jnp.int32, (WIN_KEYS, LANES), 0)
    wbias = jnp.where((kpos_w <= t_lane) & (kpos_w > t_lane - WINDOW), 0.0, NEG)
    row_sl = lax.broadcasted_iota(jnp.int32, (SLC_CHUNK, LANES), 0)
    zero_q = jnp.zeros((Q_TILE, LANES), BF16)

    def gate_row(g, branch):
        rows = [(g * GQA_REP + r) * 3 + branch for r in range(GQA_REP)]
        return _lane_cat([gt_ref[0, i:i + 1, :] for i in rows])

    qg = []
    for g in groups:
        in_g = (lane_sq >= HEAD_DIM) if g else (lane_sq < HEAD_DIM)
        qg.append(jnp.concatenate(
            [jnp.where(in_g, q_ref[:, r * LANES:(r + 1) * LANES], zero_q) for r in range(GQA_REP)],
            axis=0))

    o_cmp = [None] * NSA_KV_HEADS
    sel_bias = [None] * NSA_KV_HEADS

    def compressed_softmax(g, sc):
        ec, _ = _softmax_numer(sc, cbias)
        coef = has_cmp * _safe_inv(jnp.sum(ec, axis=0, keepdims=True))
        pc = ec * coef
        psum = (pc[:, 0:LANES] + pc[:, LANES:2 * LANES]
                + pc[:, 2 * LANES:3 * LANES] + pc[:, 3 * LANES:4 * LANES])
        p_hi, p_lo = _split_bf16(psum)
        imp = (_dot(ovl_ref[...], p_hi) + _dot(ovl_ref[...], p_lo))[0:nsel, :]
        valid = j_idx <= cur
        forced = (j_idx == 0) | (j_idx == cur) | (j_idx == cur - 1)
        score = jnp.where(valid, imp + jnp.where(forced, FORCE, 0.0), -FORCE)
        rank = jnp.zeros((nsel, LANES), F32)
        for jp in range(nsel):
            row = score[jp:jp + 1, :]
            beats = (row > score) | ((row == score) & (j_idx > jp))
            rank = rank + jnp.where(beats, 1.0, 0.0)
        sel_bias[g] = jnp.where((rank < float(SLC_TOPN)) & (score > -1.0), 0.0, NEG)
        return ec.astype(BF16), coef

    nsub = SLC_CHUNK // LANES
    blocks_per_chunk = SLC_CHUNK // SLC_BLOCK
    kwin = kw_ref[pl.ds(win_start, WIN_KEYS), :]
    units = [("cmp", g, 0) for g in groups] + [("win", g, 0) for g in groups]
    units += [("slc", g, kc) for kc in range(n_chunks) for g in groups]
    m_run = [None] * NSA_KV_HEADS
    acc_s = [None] * NSA_KV_HEADS
    acc_w = [None] * NSA_KV_HEADS

    def scores(unit):
        kind, g, kc = unit
        if kind == "cmp":
            keys = kc_ref[...]
        elif kind == "win":
            keys = kwin
        else:
            keys = ks_ref[kc * SLC_CHUNK:(kc + 1) * SLC_CHUNK, :]
        return _dot_tb(keys, qg[g])

    def softmax(unit, s):
        kind, g, kc = unit
        if kind == "cmp":
            return compressed_softmax(g, s)
        if kind == "win":
            e, _ = _softmax_numer(s, wbias)
            return e.astype(BF16), None
        j0 = kc * blocks_per_chunk
        bias = jnp.concatenate(
            [jnp.broadcast_to(sel_bias[g][j:j + 1, :], (SLC_BLOCK, LANES))
             for j in range(j0, j0 + blocks_per_chunk)], axis=0)
        if kc == n_chunks - 1:
            bias = jnp.where(kc * SLC_CHUNK + row_sl <= t_lane, bias, NEG)
        e, m_new = _softmax_numer(s, bias, m_run[g])
        alpha = None if kc == 0 else jnp.exp2(m_run[g] - m_new)
        m_run[g] = m_new
        return e.astype(BF16), alpha

    def weighted_values(unit, e, alpha):
        kind, g, kc = unit
        if kind == "cmp":
            o_cmp[g] = _dot(vc_refs[g][0], e) * (gate_row(g, 0) * alpha)
            return
        if kind == "win":
            vt = _lane_cat([vw_refs[g][win_blk + i] for i in range(WIN_KEYS // LANES)])
            acc_w[g] = _dot(vt, e)
            return
        vt = _lane_cat([vs_refs[g][nsub * kc + i] for i in range(nsub)])
        pv = _dot(vt, e)
        acc_s[g] = pv if alpha is None else acc_s[g] * alpha + pv

    s_next = scores(units[0])
    pending = None
    for i, unit in enumerate(units):
        s_cur = s_next
        if i + 1 < len(units):
            s_next = scores(units[i + 1])
        e, alpha = softmax(unit, s_cur)
        if pending is not None:
            weighted_values(*pending)
        pending = (unit, e, alpha)
    weighted_values(*pending)

    o_groups = []
    for g in groups:
        r0 = sum_row[g]
        o_groups.append(o_cmp[g]
                        + acc_s[g] * (gate_row(g, 1) * _safe_inv(acc_s[g][r0:r0 + 1, :]))
                        + acc_w[g] * (gate_row(g, 2) * _safe_inv(acc_w[g][r0:r0 + 1, :])))

    row_d = lax.broadcasted_iota(jnp.int32, (LANES, width), 0)
    o_t = jnp.where(row_d < HEAD_DIM, o_groups[0], o_groups[1])
    blks = [o_t[:, r * LANES:(r + 1) * LANES].T for r in range(GQA_REP)]
    ss = blks[0] * blks[0]
    for r in range(1, GQA_REP):
        ss = ss + blks[r] * blks[r]
    rs = lax.rsqrt(jnp.sum(ss, axis=-1, keepdims=True) * (1.0 / NSA_WIDTH) + EPS)
    for r in range(GQA_REP):
        out_ref[:, r * LANES:(r + 1) * LANES] = (
            blks[r] * rs * gn_ref[:, r * LANES:(r + 1) * LANES]).astype(BF16)


def _attn_call(q, ks, kw, vs0, vs1, vw0, vw1, kcmp, vc0, vc1, gt, ovl, gn, batch, seq):
    nq = seq // Q_TILE
    nkb = seq // LANES
    qrow = lambda b, c: (b * nq + c, 0)
    per_b = lambda b, c: (b, 0)
    per_b3 = lambda b, c: (b, 0, 0)
    vspec = pl.BlockSpec((nkb, LANES, LANES), per_b3)
    cspec = pl.BlockSpec((1, LANES, LANES), per_b3)
    return pl.pallas_call(
        _attn_body,
        grid=(batch, nq),
        in_specs=[
            pl.BlockSpec((Q_TILE, NSA_WIDTH), qrow),
            pl.BlockSpec((seq, LANES), per_b),
            pl.BlockSpec((seq, LANES), per_b),
            vspec, vspec, vspec, vspec,
            pl.BlockSpec((LANES, LANES), per_b),
            cspec, cspec,
            pl.BlockSpec((1, 32, LANES), lambda b, c: (b * nq + c, 0, 0)),
            pl.BlockSpec((LANES, LANES), lambda b, c: (0, 0)),
            pl.BlockSpec((1, NSA_WIDTH), lambda b, c: (0, 0)),
        ],
        out_specs=pl.BlockSpec((Q_TILE, NSA_WIDTH), qrow),
        out_shape=jax.ShapeDtypeStruct((batch * seq, NSA_WIDTH), BF16),
        compiler_params=pltpu.CompilerParams(
            dimension_semantics=("parallel", "arbitrary"), vmem_limit_bytes=VMEM_LIMIT),
        name="attn",
    )(q, ks, kw, vs0, vs1, vw0, vw1, kcmp, vc0, vc1, gt, ovl, gn)


def _conv_pieces(cur_ref, halo_ref, w_ref, b_ref, lng_ref, lnb_ref, gn_ref, dst_ref,
                 buf_ref, sh_ref, cv_ref):
    lead = CONV_HALO - (CONV_WIDTH - 1)
    ncb = CONV_CH // LANES

    def load():
        if halo_ref is None:
            buf_ref[0:CONV_HALO, :] = jnp.zeros((CONV_HALO, CONV_CH), F32)
        else:
            buf_ref[0:CONV_HALO, :] = halo_ref[...]
        buf_ref[CONV_HALO:, :] = cur_ref[...]

    def realign(s):
        span = CONV_TILE + 8 * ((CONV_WIDTH - 1 - s) // 8)
        for cb in range(ncb):
            sh_ref[s, cb, 0:span, :] = buf_ref[lead + s:lead + s + span, cb * LANES:(cb + 1) * LANES]

    def block(rb, cb):
        acc = jnp.zeros((CONV_ROWS, LANES), F32) + b_ref[cb]
        for k in range(CONV_WIDTH):
            r0 = rb * CONV_ROWS + 8 * (k // 8)
            acc = acc + w_ref[cb, k:k + 1, :] * sh_ref[k % 8, cb, r0:r0 + CONV_ROWS, :]
        cv_ref[cb, rb * CONV_ROWS:(rb + 1) * CONV_ROWS, :] = acc

    def finish(half):
        rows = slice(half * (CONV_TILE // 2), (half + 1) * (CONV_TILE // 2))
        h = _lane_cat([cv_ref[cb, rows, :] for cb in range(ncb)])
        mu = jnp.mean(h, axis=-1, keepdims=True)
        d = h - mu
        var = jnp.mean(d * d, axis=-1, keepdims=True)
        hn = d * lax.rsqrt(var + EPS) * lng_ref[...] + lnb_ref[...]
        o = hn * _sigmoid(hn)
        ms = jnp.mean(o * o, axis=-1, keepdims=True)
        dst_ref[rows, :] = (o * lax.rsqrt(ms + EPS) * gn_ref[...]).astype(BF16)

    pieces = [load] + [functools.partial(realign, s) for s in range(8)]
    pieces += [functools.partial(block, rb, cb)
               for rb in range(CONV_TILE // CONV_ROWS) for cb in range(ncb)]
    pieces += [functools.partial(finish, half) for half in range(2)]
    return pieces


def _ffn_body(tiles_per_seq, x_ref, ma_ref, hfirst_ref, hnext_ref, halo_ref, cw_ref, cb_ref,
              lng_ref, lnb_ref, gnc_ref, woa_ref, wob_ref, gf_ref, wg_ref, wu_ref, wd_ref,
              out_ref, mixb_ref, nextb_ref, buf_ref, sh_ref, cv_ref):
    i = pl.program_id(0)
    conv_args = (cw_ref, cb_ref, lng_ref, lnb_ref, gnc_ref)
    scratch = (buf_ref, sh_ref, cv_ref)

    @pl.when(i == 0)
    def _():
        for piece in _conv_pieces(hfirst_ref, None, *conv_args, mixb_ref, *scratch):
            piece()

    @pl.when(i > 0)
    def _():
        mixb_ref[...] = nextb_ref[...]

    h = x_ref[...] + _dot(ma_ref[...], woa_ref[...]) + _dot(mixb_ref[...], wob_ref[...])
    ms = jnp.mean(h * h, axis=-1, keepdims=True)
    hn = (h * lax.rsqrt(ms + EPS) * gf_ref[...]).astype(BF16)

    pieces = _conv_pieces(hnext_ref, halo_ref, *conv_args, nextb_ref, *scratch)
    fresh = ((i + 1) % tiles_per_seq) == 0
    load_tile = pieces[0]

    def load_next():
        load_tile()
        buf_ref[0:CONV_HALO, :] = jnp.where(fresh, 0.0, buf_ref[0:CONV_HALO, :])

    pieces[0] = load_next
    nchunk = FFN_HIDDEN // FFN_CHUNK
    per_chunk = -(-len(pieces) // nchunk)
    acc = jnp.zeros_like(h)
    for j in range(nchunk):
        cols = slice(j * FFN_CHUNK, (j + 1) * FFN_CHUNK)
        gte = _dot(hn, wg_ref[:, cols])
        up = _dot(hn, wu_ref[:, cols])
        for piece in pieces[j * per_chunk:(j + 1) * per_chunk]:
            piece()
        act = (gte * _sigmoid(gte) * up).astype(BF16)
        acc = acc + _dot(act, wd_ref[cols, :])
    out_ref[...] = h + acc


def _ffn_call(x2, ma, hcv, cw, cb, lng, lnb, gnc, woa, wob, gf, wg, wu, wd, seq):
    t = x2.shape[0]
    nt = t // FFN_TILE
    assert FFN_TILE == CONV_TILE
    halo_per_tile = CONV_TILE // CONV_HALO
    ncb = CONV_CH // LANES
    row = lambda i: (i, 0)
    const = lambda i: (0, 0)
    const3 = lambda i: (0, 0, 0)
    nxt = lambda i: (jnp.minimum(i + 1, nt - 1), 0)
    once = pl.Buffered(1)
    return pl.pallas_call(
        functools.partial(_ffn_body, seq // FFN_TILE),
        grid=(nt,),
        in_specs=[
            pl.BlockSpec((FFN_TILE, D_MODEL), row),
            pl.BlockSpec((FFN_TILE, NSA_WIDTH), row),
            pl.BlockSpec((CONV_TILE, CONV_CH), const),
            pl.BlockSpec((CONV_TILE, CONV_CH), nxt),
            pl.BlockSpec((CONV_HALO, CONV_CH),
                         lambda i: ((i + 1) * halo_per_tile - 1, 0)),
            pl.BlockSpec((ncb, CONV_HALO, LANES), const3),
            pl.BlockSpec((ncb, 1, LANES), const3),
            pl.BlockSpec((1, CONV_CH), const),
            pl.BlockSpec((1, CONV_CH), const),
            pl.BlockSpec((1, CONV_CH), const),
            pl.BlockSpec((NSA_WIDTH, D_MODEL), const, pipeline_mode=once),
            pl.BlockSpec((CONV_CH, D_MODEL), const, pipeline_mode=once),
            pl.BlockSpec((1, D_MODEL), const),
            pl.BlockSpec((D_MODEL, FFN_HIDDEN), const, pipeline_mode=once),
            pl.BlockSpec((D_MODEL, FFN_HIDDEN), const, pipeline_mode=once),
            pl.BlockSpec((FFN_HIDDEN, D_MODEL), const, pipeline_mode=once),
        ],
        out_specs=pl.BlockSpec((FFN_TILE, D_MODEL), row),
        out_shape=jax.ShapeDtypeStruct((t, D_MODEL), F32),
        scratch_shapes=[
            pltpu.VMEM((CONV_TILE, CONV_CH), BF16),
            pltpu.VMEM((CONV_TILE, CONV_CH), BF16),
            pltpu.VMEM((CONV_HALO + CONV_TILE, CONV_CH), F32),
            pltpu.VMEM((8, ncb, CONV_TILE + 8 * ((CONV_WIDTH - 1) // 8), LANES), F32),
            pltpu.VMEM((ncb, CONV_TILE, LANES), F32),
        ],
        compiler_params=pltpu.CompilerParams(
            dimension_semantics=("arbitrary",), vmem_limit_bytes=VMEM_LIMIT),
        name="ffn",
    )(x2, ma, hcv, hcv, hcv, cw, cb, lng, lnb, gnc, woa, wob, gf, wg, wu, wd)


def _pair_heads(a, axis):
    shape = a.shape
    a = a.reshape(shape[:axis] + (NSA_KV_HEADS, GQA_REP, HEAD_DIM) + shape[axis + 1:])
    return jnp.swapaxes(a, axis, axis + 1).reshape(shape)


def _rope_tables(pos):
    half = HEAD_DIM // 2
    inv = ROPE_THETA ** (-np.arange(half, dtype=np.float64) / half)
    ang = np.asarray(pos, np.float64)[:, None] * inv[None, :]
    cos = np.concatenate([np.cos(ang), np.cos(ang)], axis=1)
    sin = np.concatenate([-np.sin(ang), np.sin(ang)], axis=1)
    return (np.tile(cos, (1, NSA_KV_HEADS)).astype(np.float32),
            np.tile(sin, (1, NSA_KV_HEADS)).astype(np.float32))


def _block_diag_ones(width):
    idx = np.arange(width) // HEAD_DIM
    return (idx[:, None] == idx[None, :]).astype(np.float32)


def _overlap_t(seq):
    ncmp = (seq - CMP_LEN) // CMP_STRIDE + 1
    cs = np.arange(ncmp)[:, None] * CMP_STRIDE
    ss = np.arange(seq // SLC_BLOCK)[None, :] * SLC_BLOCK
    ov = np.clip(np.minimum(cs + CMP_LEN, ss + SLC_BLOCK) - np.maximum(cs, ss), 0, None) / CMP_LEN
    out = np.zeros((LANES, LANES), np.float32)
    out[:seq // SLC_BLOCK, :ncmp] = ov.T
    return out


def _cmp_weights(pe, w1, w2):
    pe2 = jnp.concatenate([pe, pe], axis=1)
    w1r = w1.reshape(CMP_LEN, HEAD_DIM, CMP_HIDDEN)
    z1 = jnp.zeros_like(w1r)
    w1b = jnp.concatenate([jnp.concatenate([w1r, z1], axis=2),
                           jnp.concatenate([z1, w1r], axis=2)], axis=1).astype(BF16)
    z2 = jnp.zeros_like(w2)
    w2b = jnp.concatenate([jnp.concatenate([w2, z2], axis=1),
                           jnp.concatenate([z2, w2], axis=1)], axis=0).astype(BF16)
    return pe2, w1b, w2b


def kernel(x, attn_norm_g, w_in, q_norm_g, k_norm_cmp_g, k_norm_slc_g, k_norm_win_g, cmp_pe_k, cmp_w1_k, cmp_w2_k, cmp_pe_v, cmp_w1_v, cmp_w2_v, conv_dw_w, conv_dw_b, conv_ln_g, conv_ln_b, out_norm_nsa_g, out_norm_conv_g, w_out, ffn_norm_g, w_gate_up, w_down):
    batch, seq, d_model = x.shape
    assert d_model == D_MODEL and seq % PROJ_TILE == 0 and seq // SLC_BLOCK == 32
    depth = w_in.shape[0]
    cos_np, sin_np = _rope_tables(np.arange(seq))
    ccos_np, csin_np = _rope_tables(np.arange(seq // CMP_STRIDE) * CMP_STRIDE + CMP_LEN - 1)
    cos, sin, ccos, csin = map(jnp.asarray, (cos_np, sin_np, ccos_np, csin_np))
    bd256 = jnp.asarray(_block_diag_ones(256), BF16)
    bd128 = jnp.asarray(_block_diag_ones(LANES), BF16)
    ovl = jnp.asarray(_overlap_t(seq), BF16)

    x2 = x.reshape(batch * seq, d_model)
    for l in range(depth):
        w = w_in[l]
        o_q, o_kc, o_vc, o_ks, o_vs, o_kw, o_vw, o_gl, o_cv = (
            0, 512, 640, 768, 896, 1024, 1152, 1280, 1304)
        gl_pad = jnp.zeros((d_model, LANES - 3 * NSA_HEADS), BF16)
        w_q = _pair_heads(w[:, o_q:o_q + NSA_WIDTH], 1)
        w_cat = jnp.concatenate([piece.astype(BF16) for piece in (
            w_q,
            w[:, o_ks:o_ks + KV_WIDTH], w[:, o_kw:o_kw + KV_WIDTH],
            w[:, o_vs:o_vs + KV_WIDTH], w[:, o_vw:o_vw + KV_WIDTH],
            w[:, o_kc:o_kc + KV_WIDTH], w[:, o_vc:o_vc + KV_WIDTH],
            w[:, o_cv:o_cv + CONV_CH], w[:, o_cv + CONV_CH:o_cv + 2 * CONV_CH],
            w[:, o_gl:o_gl + 3 * NSA_HEADS], gl_pad)], axis=1)
        gq = jnp.tile(q_norm_g[l], 4)[None, :]
        gk = jnp.concatenate([jnp.tile(k_norm_slc_g[l], 2), jnp.tile(k_norm_win_g[l], 2)])[None, :]
        q, ks, kw, vs0, vs1, vw0, vw1, kc, vc, gt, hcv = _proj_call(
            x2, attn_norm_g[l][None, :], w_cat, cos, sin, gq, gk, bd256, seq)

        pek, w1k, w2k = _cmp_weights(cmp_pe_k[l], cmp_w1_k[l], cmp_w2_k[l])
        pev, w1v, w2v = _cmp_weights(cmp_pe_v[l], cmp_w1_v[l], cmp_w2_v[l])
        kcmp, vc0, vc1 = _cmp_call(kc, vc, pek, w1k, w2k, pev, w1v, w2v, ccos, csin,
                                jnp.tile(k_norm_cmp_g[l], 2)[None, :], bd128, batch, seq)

        mix_a = _attn_call(q, ks, kw, vs0, vs1, vw0, vw1, kcmp, vc0, vc1, gt, ovl,
                           _pair_heads(out_norm_nsa_g[l], 0)[None, :], batch, seq)

        w_dw = jnp.concatenate([conv_dw_w[l][:, 0, :],
                                jnp.zeros((CONV_HALO - CONV_WIDTH, CONV_CH), F32)], axis=0)
        w_dw = w_dw.reshape(CONV_HALO, CONV_CH // LANES, LANES).transpose(1, 0, 2)
        wo = w_out[l]
        x2 = _ffn_call(x2, mix_a, hcv, w_dw, conv_dw_b[l].reshape(CONV_CH // LANES, 1, LANES),
                       conv_ln_g[l][None, :], conv_ln_b[l][None, :], out_norm_conv_g[l][None, :],
                       _pair_heads(wo[:NSA_WIDTH], 0).astype(BF16), wo[NSA_WIDTH:].astype(BF16),
                       ffn_norm_g[l][None, :], w_gate_up[l][:, :FFN_HIDDEN].astype(BF16),
                       w_gate_up[l][:, FFN_HIDDEN:].astype(BF16), w_down[l].astype(BF16), seq)
    return x2.reshape(batch, seq, d_model)
```

```python
import functools

import numpy as np
import jax
import jax.numpy as jnp
from jax import lax
from jax.experimental import pallas as pl
from jax.experimental.pallas import tpu as pltpu

F32 = jnp.float32
BF16 = jnp.bfloat16

D_MODEL = 1024
HEAD_DIM = 64
NSA_HEADS = 8
NSA_KV_HEADS = 2
GQA_REP = NSA_HEADS // NSA_KV_HEADS
NSA_WIDTH = NSA_HEADS * HEAD_DIM
KV_WIDTH = NSA_KV_HEADS * HEAD_DIM
CONV_CH = D_MODEL - NSA_WIDTH
CMP_LEN = 32
CMP_STRIDE = 16
CMP_HIDDEN = 4 * HEAD_DIM
SLC_BLOCK = 64
SLC_TOPN = 8
WINDOW = 512
CONV_WIDTH = 31
FFN_HIDDEN = 2816
ROPE_THETA = 10000.0
EPS = 1e-6
NEG = -1e30
LOG2E = 1.4426950408889634
FORCE = 1e6

LANES = 128
Q_TILE = 128
SLC_CHUNK = 256
WIN_KEYS = WINDOW + Q_TILE
PROJ_TILE = 512
CONV_TILE = 512
CONV_HALO = 32
CONV_ROWS = 128
FFN_TILE = 512
FFN_CHUNK = 256
VMEM_LIMIT = 56 * 1024 * 1024

C_Q = 0
C_KS = 512
C_KW = 640
C_VS = 768
C_VW = 896
C_KC = 1024
C_VC = 1152
C_CA = 1280
C_CG = 1792
C_GL = 2304
PROJ_COLS = 2432

_TRANS_B = (((1,), (1,)), ((), ()))


def _dot(a, b):
    return jnp.dot(a, b, preferred_element_type=F32)


def _dot_tb(a, b):
    return lax.dot_general(a, b, _TRANS_B, preferred_element_type=F32)


def _split_bf16(x):
    hi = x.astype(BF16)
    lo = (x - hi.astype(F32)).astype(BF16)
    return hi, lo


def _sigmoid(x):
    return 1.0 / (1.0 + jnp.exp(-x))


def _head_norm_rope(y, gain, bd, cos, sin, first_half, scale):
    hi, lo = _split_bf16(y * y)
    ss = _dot(hi, bd) + _dot(lo, bd)
    yn = y * lax.rsqrt(ss * (1.0 / HEAD_DIM) + EPS) * gain
    outs = []
    for j in range(y.shape[1] // LANES):
        blk = yn[:, j * LANES:(j + 1) * LANES]
        rot = jnp.where(first_half, pltpu.roll(blk, LANES - HEAD_DIM // 2, 1),
                        pltpu.roll(blk, HEAD_DIM // 2, 1))
        outs.append((blk * cos + rot * sin) * scale)
    return outs


def _value_t_with_ones(v):
    vt = v.T
    row = lax.broadcasted_iota(jnp.int32, vt.shape, 0)
    return (jnp.where(row < HEAD_DIM, vt, 1.0).astype(BF16),
            jnp.where(row < HEAD_DIM, 1.0, vt).astype(BF16))


def _proj_body(x_ref, gin_ref, w_ref, cos_ref, sin_ref, gq_ref, gk_ref, bd_ref,
               q_ref, ks_ref, kw_ref, vs0_ref, vs1_ref, vw0_ref, vw1_ref, kc_ref, vc_ref, gt_ref,
               hcv_ref):
    x = x_ref[...]
    xg = (x * gin_ref[...]).astype(BF16)
    rs = lax.rsqrt(jnp.mean(x * x, axis=-1, keepdims=True) + EPS)
    cos = cos_ref[...]
    sin = sin_ref[...]
    bd = bd_ref[...]
    lane = lax.broadcasted_iota(jnp.int32, (PROJ_TILE, LANES), 1)
    first_half = (lane & (HEAD_DIM // 2)) == 0
    held = {}

    def queries(half, y):
        blks = _head_norm_rope(y, gq_ref[...], bd, cos, sin, first_half, HEAD_DIM ** -0.5 * LOG2E)
        for j in range(2):
            q_ref[:, (2 * half + j) * LANES:(2 * half + j + 1) * LANES] = blks[j].astype(BF16)

    def keys(y):
        kblks = _head_norm_rope(y, gk_ref[...], bd, cos, sin, first_half, 1.0)
        ks_ref[...] = kblks[0].astype(BF16)
        kw_ref[...] = kblks[1].astype(BF16)

    def values(y):
        for i in range(PROJ_TILE // LANES):
            rows = y[i * LANES:(i + 1) * LANES, :]
            vs0_ref[i], vs1_ref[i] = _value_t_with_ones(rows[:, :LANES])
            vw0_ref[i], vw1_ref[i] = _value_t_with_ones(rows[:, LANES:])

    def cmp_inputs(y):
        kc_ref[...] = y[:, :LANES]
        vc_ref[...] = y[:, LANES:]

    def conv_lin(y):
        held["ca"] = y

    def conv_glu(y):
        hcv_ref[...] = held["ca"] * _sigmoid(y)

    def gates(y):
        sg = _sigmoid(y)
        for i in range(PROJ_TILE // LANES):
            gt_ref[i] = sg[i * LANES:(i + 1) * LANES, :].T[0:32, :]

    sections = [
        (C_Q, 256, functools.partial(queries, 0)),
        (C_CA, CONV_CH, conv_lin),
        (C_Q + 256, 256, functools.partial(queries, 1)),
        (C_CG, CONV_CH, conv_glu),
        (C_KS, 256, keys),
        (C_VS, 256, values),
        (C_KC, 256, cmp_inputs),
        (C_GL, LANES, gates),
    ]

    def project(sec):
        c0, width, _ = sec
        return _dot(xg, w_ref[:, c0:c0 + width])

    y_next = project(sections[0])
    for i, sec in enumerate(sections):
        y_cur = y_next
        if i + 1 < len(sections):
            y_next = project(sections[i + 1])
        sec[2](y_cur * rs)


def _proj_call(x2, gin, w_cat, cos, sin, gq, gk, bd, seq):
    t = x2.shape[0]
    nt = t // PROJ_TILE
    tiles_per_seq = seq // PROJ_TILE
    row = lambda i: (i, 0)
    const = lambda i: (0, 0)
    tab = lambda i: (i % tiles_per_seq, 0)
    blk3 = lambda i: (i, 0, 0)
    nb = PROJ_TILE // LANES
    return pl.pallas_call(
        _proj_body,
        grid=(nt,),
        in_specs=[
            pl.BlockSpec((PROJ_TILE, D_MODEL), row),
            pl.BlockSpec((1, D_MODEL), const),
            pl.BlockSpec((D_MODEL, PROJ_COLS), const),
            pl.BlockSpec((PROJ_TILE, LANES), tab),
            pl.BlockSpec((PROJ_TILE, LANES), tab),
            pl.BlockSpec((1, 256), const),
            pl.BlockSpec((1, 256), const),
            pl.BlockSpec((256, 256), const),
        ],
        out_specs=[
            pl.BlockSpec((PROJ_TILE, NSA_WIDTH), row),
            pl.BlockSpec((PROJ_TILE, LANES), row),
            pl.BlockSpec((PROJ_TILE, LANES), row),
            pl.BlockSpec((nb, LANES, LANES), blk3),
            pl.BlockSpec((nb, LANES, LANES), blk3),
            pl.BlockSpec((nb, LANES, LANES), blk3),
            pl.BlockSpec((nb, LANES, LANES), blk3),
            pl.BlockSpec((PROJ_TILE, LANES), row),
            pl.BlockSpec((PROJ_TILE, LANES), row),
            pl.BlockSpec((nb, 32, LANES), blk3),
            pl.BlockSpec((PROJ_TILE, CONV_CH), row),
        ],
        out_shape=[
            jax.ShapeDtypeStruct((t, NSA_WIDTH), BF16),
            jax.ShapeDtypeStruct((t, LANES), BF16),
            jax.ShapeDtypeStruct((t, LANES), BF16),
            jax.ShapeDtypeStruct((t // LANES, LANES, LANES), BF16),
            jax.ShapeDtypeStruct((t // LANES, LANES, LANES), BF16),
            jax.ShapeDtypeStruct((t // LANES, LANES, LANES), BF16),
            jax.ShapeDtypeStruct((t // LANES, LANES, LANES), BF16),
            jax.ShapeDtypeStruct((t, LANES), F32),
            jax.ShapeDtypeStruct((t, LANES), F32),
            jax.ShapeDtypeStruct((t // LANES, 32, LANES), F32),
            jax.ShapeDtypeStruct((t, CONV_CH), F32),
        ],
        compiler_params=pltpu.CompilerParams(
            dimension_semantics=("parallel",), vmem_limit_bytes=VMEM_LIMIT),
        name="proj",
    )(x2, gin, w_cat, cos, sin, gq, gk, bd)


def _cmp_mlp(t_ref, pe_ref, w1_ref, w2_ref):
    half = CMP_LEN // 2
    nchunk = t_ref.shape[0] // CMP_STRIDE
    p = jnp.zeros((nchunk, 2 * CMP_HIDDEN), F32)
    q = jnp.zeros((nchunk, 2 * CMP_HIDDEN), F32)
    for l in range(half):
        xl = t_ref[pl.ds(l, nchunk, stride=CMP_STRIDE), :]
        p = p + _dot((xl + pe_ref[l:l + 1, :]).astype(BF16), w1_ref[l])
        q = q + _dot((xl + pe_ref[half + l:half + l + 1, :]).astype(BF16), w1_ref[half + l])
    h1 = p + pltpu.roll(q, nchunk - 1, 0)
    act = (h1 * _sigmoid(h1)).astype(BF16)
    return _dot(act, w2_ref[...])


def _cmp_body(kc_ref, vc_ref, pek_ref, w1k_ref, w2k_ref, pev_ref, w1v_ref, w2v_ref,
              cos_ref, sin_ref, gk_ref, bd_ref, ko_ref, vt0_ref, vt1_ref):
    kraw = _cmp_mlp(kc_ref, pek_ref, w1k_ref, w2k_ref)
    lane = lax.broadcasted_iota(jnp.int32, kraw.shape, 1)
    first_half = (lane & (HEAD_DIM // 2)) == 0
    (kr,) = _head_norm_rope(kraw, gk_ref[...], bd_ref[...], cos_ref[...], sin_ref[...],
                            first_half, 1.0)
    ko_ref[...] = kr.astype(BF16)
    vraw = _cmp_mlp(vc_ref, pev_ref, w1v_ref, w2v_ref)
    vt0_ref[0], vt1_ref[0] = _value_t_with_ones(vraw)


def _cmp_call(kc, vc, pek, w1k, w2k, pev, w1v, w2v, cos, sin, gk, bd, batch, seq):
    seqblk = lambda b: (b, 0)
    c2 = lambda b: (0, 0)
    c3 = lambda b: (0, 0, 0)
    ncmp_pad = seq // CMP_STRIDE
    return pl.pallas_call(
        _cmp_body,
        grid=(batch,),
        in_specs=[
            pl.BlockSpec((seq, LANES), seqblk),
            pl.BlockSpec((seq, LANES), seqblk),
            pl.BlockSpec((CMP_LEN, LANES), c2),
            pl.BlockSpec((CMP_LEN, LANES, 2 * CMP_HIDDEN), c3),
            pl.BlockSpec((2 * CMP_HIDDEN, LANES), c2),
            pl.BlockSpec((CMP_LEN, LANES), c2),
            pl.BlockSpec((CMP_LEN, LANES, 2 * CMP_HIDDEN), c3),
            pl.BlockSpec((2 * CMP_HIDDEN, LANES), c2),
            pl.BlockSpec((ncmp_pad, LANES), c2),
            pl.BlockSpec((ncmp_pad, LANES), c2),
            pl.BlockSpec((1, LANES), c2),
            pl.BlockSpec((LANES, LANES), c2),
        ],
        out_specs=[
            pl.BlockSpec((ncmp_pad, LANES), seqblk),
            pl.BlockSpec((1, LANES, ncmp_pad), lambda b: (b, 0, 0)),
            pl.BlockSpec((1, LANES, ncmp_pad), lambda b: (b, 0, 0)),
        ],
        out_shape=[
            jax.ShapeDtypeStruct((batch * ncmp_pad, LANES), BF16),
            jax.ShapeDtypeStruct((batch, LANES, ncmp_pad), BF16),
            jax.ShapeDtypeStruct((batch, LANES, ncmp_pad), BF16),
        ],
        compiler_params=pltpu.CompilerParams(
            dimension_semantics=("parallel",), vmem_limit_bytes=VMEM_LIMIT),
        name="compress",
    )(kc, vc, pek, w1k, w2k, pev, w1v, w2v, cos, sin, gk, bd)


def _lane_cat(blocks):
    return jnp.concatenate(blocks, axis=1)


def _softmax_numer(s, bias, m_run=None):
    es, ms = [], []
    for r in range(GQA_REP):
        sm = s[:, r * LANES:(r + 1) * LANES] + bias
        m = jnp.max(sm, axis=0, keepdims=True)
        if m_run is not None:
            m = jnp.maximum(m, m_run[:, r * LANES:(r + 1) * LANES])
        es.append(jnp.exp2(sm - m))
        ms.append(m)
    return _lane_cat(es), _lane_cat(ms)


def _safe_inv(l):
    return 1.0 / jnp.where(l > 0.0, l, 1.0)


def _attn_body(*refs):
    c = pl.program_id(1)
    n_max = refs[1].shape[0] // SLC_CHUNK
    for n in range(1, n_max + 1):
        pl.when((c * Q_TILE) // SLC_CHUNK + 1 == n)(functools.partial(_attn_tile, n, c, *refs))


def _attn_tile(n_chunks, c, q_ref, ks_ref, kw_ref, vs0_ref, vs1_ref, vw0_ref, vw1_ref, kc_ref, vc0_ref,
               vc1_ref, gt_ref, ovl_ref, gn_ref, out_ref):
    groups = range(NSA_KV_HEADS)
    vs_refs, vw_refs, vc_refs = (vs0_ref, vs1_ref), (vw0_ref, vw1_ref), (vc0_ref, vc1_ref)
    sum_row = (HEAD_DIM, 0)
    width = GQA_REP * LANES
    t_lane = c * Q_TILE + lax.broadcasted_iota(jnp.int32, (1, LANES), 1)
    lane_sq = lax.broadcasted_iota(jnp.int32, (Q_TILE, LANES), 1)
    nsel = 32
    j_idx = lax.broadcasted_iota(jnp.int32, (nsel, LANES), 0)
    cur = lax.shift_right_logical(t_lane, 6)
    n_idx = lax.broadcasted_iota(jnp.int32, (LANES, LANES), 0)
    cbias = jnp.where((n_idx * CMP_STRIDE + (CMP_LEN - 1) <= t_lane) & (n_idx < LANES - 1), 0.0, NEG)
    has_cmp = jnp.where(t_lane >= CMP_LEN - 1, 1.0, 0.0)
    has_cmp = _lane_cat([has_cmp] * GQA_REP)
    win_blk = jnp.maximum(c - WINDOW // Q_TILE, 0)
    win_start = pl.multiple_of(win_blk * Q_TILE, Q_TILE)
    kpos_w = win_start + lax.broadcasted_iota(jnp.int32, (WIN_KEYS, LANES), 0)
    wbias = jnp.where((kpos_w <= t_lane) & (kpos_w > t_lane - WINDOW), 0.0, NEG)
    row_sl = lax.broadcasted_iota(jnp.int32, (SLC_CHUNK, LANES), 0)
    zero_q = jnp.zeros((Q_TILE, LANES), BF16)

    def gate_row(g, branch):
        rows = [(g * GQA_REP + r) * 3 + branch for r in range(GQA_REP)]
        return _lane_cat([gt_ref[0, i:i + 1, :] for i in rows])

    qg = []
    for g in groups:
        in_g = (lane_sq >= HEAD_DIM) if g else (lane_sq < HEAD_DIM)
        qg.append(jnp.concatenate(
            [jnp.where(in_g, q_ref[:, r * LANES:(r + 1) * LANES], zero_q) for r in range(GQA_REP)],
            axis=0))

    o_cmp = [None] * NSA_KV_HEADS
    sel_bias = [None] * NSA_KV_HEADS

    def compressed_softmax(g, sc):
        ec, _ = _softmax_numer(sc, cbias)
        coef = has_cmp * _safe_inv(jnp.sum(ec, axis=0, keepdims=True))
        pc = ec * coef
        psum = (pc[:, 0:LANES] + pc[:, LANES:2 * LANES]
                + pc[:, 2 * LANES:3 * LANES] + pc[:, 3 * LANES:4 * LANES])
        p_hi, p_lo = _split_bf16(psum)
        imp = (_dot(ovl_ref[...], p_hi) + _dot(ovl_ref[...], p_lo))[0:nsel, :]
        valid = j_idx <= cur
        forced = (j_idx == 0) | (j_idx == cur) | (j_idx == cur - 1)
        score = jnp.where(valid, imp + jnp.where(forced, FORCE, 0.0), -FORCE)
        rank = jnp.zeros((nsel, LANES), F32)
        for jp in range(nsel):
            row = score[jp:jp + 1, :]
            beats = (row > score) | ((row == score) & (j_idx > jp))
            rank = rank + jnp.where(beats, 1.0, 0.0)
        sel_bias[g] = jnp.where((rank < float(SLC_TOPN)) & (score > -1.0), 0.0, NEG)
        return ec.astype(BF16), coef

    nsub = SLC_CHUNK // LANES
    blocks_per_chunk = SLC_CHUNK // SLC_BLOCK
    kwin = kw_ref[pl.ds(win_start, WIN_KEYS), :]
    units = [("cmp", g, 0) for g in groups] + [("win", g, 0) for g in groups]
    units += [("slc", g, kc) for kc in range(n_chunks) for g in groups]
    m_run = [None] * NSA_KV_HEADS
    acc_s = [None] * NSA_KV_HEADS
    acc_w = [None] * NSA_KV_HEADS

    def scores(unit):
        kind, g, kc = unit
        if kind == "cmp":
            keys = kc_ref[...]
        elif kind == "win":
            keys = kwin
        else:
            keys = ks_ref[kc * SLC_CHUNK:(kc + 1) * SLC_CHUNK, :]
        return _dot_tb(keys, qg[g])

    def softmax(unit, s):
        kind, g, kc = unit
        if kind == "cmp":
            return compressed_softmax(g, s)
        if kind == "win":
            e, _ = _softmax_numer(s, wbias)
            return e.astype(BF16), None
        j0 = kc * blocks_per_chunk
        bias = jnp.concatenate(
            [jnp.broadcast_to(sel_bias[g][j:j + 1, :], (SLC_BLOCK, LANES))
             for j in range(j0, j0 + blocks_per_chunk)], axis=0)
        if kc == n_chunks - 1:
            bias = jnp.where(kc * SLC_CHUNK + row_sl <= t_lane, bias, NEG)
        e, m_new = _softmax_numer(s, bias, m_run[g])
        alpha = None if kc == 0 else jnp.exp2(m_run[g] - m_new)
        m_run[g] = m_new
        return e.astype(BF16), alpha

    def weighted_values(unit, e, alpha):
        kind, g, kc = unit
        if kind == "cmp":
            o_cmp[g] = _dot(vc_refs[g][0], e) * (gate_row(g, 0) * alpha)
            return
        if kind == "win":
            vt = _lane_cat([vw_refs[g][win_blk + i] for i in range(WIN_KEYS // LANES)])
            acc_w[g] = _dot(vt, e)
            return
        vt = _lane_cat([vs_refs[g][nsub * kc + i] for i in range(nsub)])
        pv = _dot(vt, e)
        acc_s[g] = pv if alpha is None else acc_s[g] * alpha + pv

    s_next = scores(units[0])
    pending = None
    for i, unit in enumerate(units):
        s_cur = s_next
        if i + 1 < len(units):
            s_next = scores(units[i + 1])
        e, alpha = softmax(unit, s_cur)
        if pending is not None:
            weighted_values(*pending)
        pending = (unit, e, alpha)
    weighted_values(*pending)

    o_groups = []
    for g in groups:
        r0 = sum_row[g]
        o_groups.append(o_cmp[g]
                        + acc_s[g] * (gate_row(g, 1) * _safe_inv(acc_s[g][r0:r0 + 1, :]))
                        + acc_w[g] * (gate_row(g, 2) * _safe_inv(acc_w[g][r0:r0 + 1, :])))

    row_d = lax.broadcasted_iota(jnp.int32, (LANES, width), 0)
    o_t = jnp.where(row_d < HEAD_DIM, o_groups[0], o_groups[1])
    sq = jnp.sum(o_t * o_t, axis=0, keepdims=True)
    ss = (sq[:, 0:LANES] + sq[:, LANES:2 * LANES]
          + sq[:, 2 * LANES:3 * LANES] + sq[:, 3 * LANES:4 * LANES])
    rs = lax.rsqrt(ss * (1.0 / NSA_WIDTH) + EPS)
    for r in range(GQA_REP):
        cols = slice(r * LANES, (r + 1) * LANES)
        out_ref[:, cols] = (o_t[:, cols] * rs * gn_ref[:, cols]).T.astype(BF16)


def _attn_call(q, ks, kw, vs0, vs1, vw0, vw1, kcmp, vc0, vc1, gt, ovl, gn, batch, seq):
    nq = seq // Q_TILE
    nkb = seq // LANES
    qrow = lambda b, c: (b * nq + c, 0)
    per_b = lambda b, c: (b, 0)
    per_b3 = lambda b, c: (b, 0, 0)
    vspec = pl.BlockSpec((nkb, LANES, LANES), per_b3)
    cspec = pl.BlockSpec((1, LANES, LANES), per_b3)
    return pl.pallas_call(
        _attn_body,
        grid=(batch, nq),
        in_specs=[
            pl.BlockSpec((Q_TILE, NSA_WIDTH), qrow),
            pl.BlockSpec((seq, LANES), per_b),
            pl.BlockSpec((seq, LANES), per_b),
            vspec, vspec, vspec, vspec,
            pl.BlockSpec((LANES, LANES), per_b),
            cspec, cspec,
            pl.BlockSpec((1, 32, LANES), lambda b, c: (b * nq + c, 0, 0)),
            pl.BlockSpec((LANES, LANES), lambda b, c: (0, 0)),
            pl.BlockSpec((LANES, NSA_WIDTH), lambda b, c: (0, 0)),
        ],
        out_specs=pl.BlockSpec((Q_TILE, NSA_WIDTH), qrow),
        out_shape=jax.ShapeDtypeStruct((batch * seq, NSA_WIDTH), BF16),
        compiler_params=pltpu.CompilerParams(
            dimension_semantics=("parallel", "arbitrary"), vmem_limit_bytes=VMEM_LIMIT),
        name="attn",
    )(q, ks, kw, vs0, vs1, vw0, vw1, kcmp, vc0, vc1, gt, ovl, gn)


def _conv_body(tiles_per_seq, cur_ref, halo_ref, w_ref, b_ref, lng_ref, lnb_ref, gn_ref, out_ref,
               buf_ref, sh_ref, cv_ref):
    first = (pl.program_id(0) % tiles_per_seq) == 0
    buf_ref[0:CONV_HALO, :] = jnp.where(first, 0.0, halo_ref[...])
    buf_ref[CONV_HALO:, :] = cur_ref[...]
    lead = CONV_HALO - (CONV_WIDTH - 1)
    ncb = CONV_CH // LANES
    nrb = CONV_TILE // CONV_ROWS
    for s in range(8):
        span = CONV_TILE + 8 * ((CONV_WIDTH - 1 - s) // 8)
        for cb in range(ncb):
            sh_ref[s, cb, 0:span, :] = buf_ref[lead + s:lead + s + span, cb * LANES:(cb + 1) * LANES]

    def conv_block(i, carry):
        cb = i % ncb
        base = pl.multiple_of((i // ncb) * CONV_ROWS, CONV_ROWS)
        acc = jnp.zeros((CONV_ROWS, LANES), F32) + b_ref[cb]
        for k in range(CONV_WIDTH):
            acc = acc + w_ref[cb, k:k + 1, :] * sh_ref[k % 8, cb, pl.ds(base + 8 * (k // 8), CONV_ROWS), :]
        cv_ref[cb, pl.ds(base, CONV_ROWS), :] = acc
        return carry

    lax.fori_loop(0, nrb * ncb, conv_block, 0)
    h = _lane_cat([cv_ref[cb] for cb in range(ncb)])
    mu = jnp.mean(h, axis=-1, keepdims=True)
    d = h - mu
    var = jnp.mean(d * d, axis=-1, keepdims=True)
    hn = d * lax.rsqrt(var + EPS) * lng_ref[...] + lnb_ref[...]
    o = hn * _sigmoid(hn)
    ms = jnp.mean(o * o, axis=-1, keepdims=True)
    out_ref[...] = (o * lax.rsqrt(ms + EPS) * gn_ref[...]).astype(BF16)


def _conv_call(hcv, w, b, lng, lnb, gn, seq):
    t = hcv.shape[0]
    nt = t // CONV_TILE
    halo_per_tile = CONV_TILE // CONV_HALO
    ncb = CONV_CH // LANES
    row = lambda i: (i, 0)
    const = lambda i: (0, 0)
    return pl.pallas_call(
        functools.partial(_conv_body, seq // CONV_TILE),
        grid=(nt,),
        in_specs=[
            pl.BlockSpec((CONV_TILE, CONV_CH), row),
            pl.BlockSpec((CONV_HALO, CONV_CH), lambda i: (jnp.maximum(i * halo_per_tile - 1, 0), 0)),
            pl.BlockSpec((ncb, CONV_HALO, LANES), lambda i: (0, 0, 0)),
            pl.BlockSpec((ncb, 1, LANES), lambda i: (0, 0, 0)),
            pl.BlockSpec((1, CONV_CH), const),
            pl.BlockSpec((1, CONV_CH), const),
            pl.BlockSpec((1, CONV_CH), const),
        ],
        out_specs=pl.BlockSpec((CONV_TILE, CONV_CH), row),
        out_shape=jax.ShapeDtypeStruct((t, CONV_CH), BF16),
        scratch_shapes=[pltpu.VMEM((CONV_HALO + CONV_TILE, CONV_CH), F32),
                        pltpu.VMEM((8, ncb, CONV_TILE + 8 * ((CONV_WIDTH - 1) // 8), LANES), F32),
                        pltpu.VMEM((ncb, CONV_TILE, LANES), F32)],
        compiler_params=pltpu.CompilerParams(
            dimension_semantics=("parallel",), vmem_limit_bytes=VMEM_LIMIT),
        name="conv",
    )(hcv, hcv, w, b, lng, lnb, gn)


def _ffn_body(x_ref, ma_ref, mb_ref, woa_ref, wob_ref, gf_ref, wg_ref, wu_ref, wd_ref, out_ref):
    h = x_ref[...] + _dot(ma_ref[...], woa_ref[...]) + _dot(mb_ref[...], wob_ref[...])
    ms = jnp.mean(h * h, axis=-1, keepdims=True)
    hn = (h * lax.rsqrt(ms + EPS) * gf_ref[...]).astype(BF16)
    acc = jnp.zeros_like(h)
    for j in range(FFN_HIDDEN // FFN_CHUNK):
        cols = slice(j * FFN_CHUNK, (j + 1) * FFN_CHUNK)
        gte = _dot(hn, wg_ref[:, cols])
        up = _dot(hn, wu_ref[:, cols])
        act = (gte * _sigmoid(gte) * up).astype(BF16)
        acc = acc + _dot(act, wd_ref[cols, :])
    out_ref[...] = h + acc


def _ffn_call(x2, ma, mb, woa, wob, gf, wg, wu, wd):
    t = x2.shape[0]
    nt = t // FFN_TILE
    row = lambda i: (i, 0)
    const = lambda i: (0, 0)
    once = pl.Buffered(1)
    return pl.pallas_call(
        _ffn_body,
        grid=(nt,),
        in_specs=[
            pl.BlockSpec((FFN_TILE, D_MODEL), row),
            pl.BlockSpec((FFN_TILE, NSA_WIDTH), row),
            pl.BlockSpec((FFN_TILE, CONV_CH), row),
            pl.BlockSpec((NSA_WIDTH, D_MODEL), const, pipeline_mode=once),
            pl.BlockSpec((CONV_CH, D_MODEL), const, pipeline_mode=once),
            pl.BlockSpec((1, D_MODEL), const),
            pl.BlockSpec((D_MODEL, FFN_HIDDEN), const, pipeline_mode=once),
            pl.BlockSpec((D_MODEL, FFN_HIDDEN), const, pipeline_mode=once),
            pl.BlockSpec((FFN_HIDDEN, D_MODEL), const, pipeline_mode=once),
        ],
        out_specs=pl.BlockSpec((FFN_TILE, D_MODEL), row),
        out_shape=jax.ShapeDtypeStruct((t, D_MODEL), F32),
        compiler_params=pltpu.CompilerParams(
            dimension_semantics=("parallel",), vmem_limit_bytes=VMEM_LIMIT),
        name="ffn",
    )(x2, ma, mb, woa, wob, gf, wg, wu, wd)


def _pair_heads(a, axis):
    shape = a.shape
    a = a.reshape(shape[:axis] + (NSA_KV_HEADS, GQA_REP, HEAD_DIM) + shape[axis + 1:])
    return jnp.swapaxes(a, axis, axis + 1).reshape(shape)


def _transposed_gain(g):
    cols = g.reshape(GQA_REP, LANES).T
    return jnp.broadcast_to(cols[:, :, None], (LANES, GQA_REP, Q_TILE)).reshape(LANES, GQA_REP * Q_TILE)


def _rope_tables(pos):
    half = HEAD_DIM // 2
    inv = ROPE_THETA ** (-np.arange(half, dtype=np.float64) / half)
    ang = np.asarray(pos, np.float64)[:, None] * inv[None, :]
    cos = np.concatenate([np.cos(ang), np.cos(ang)], axis=1)
    sin = np.concatenate([-np.sin(ang), np.sin(ang)], axis=1)
    return (np.tile(cos, (1, NSA_KV_HEADS)).astype(np.float32),
            np.tile(sin, (1, NSA_KV_HEADS)).astype(np.float32))


def _block_diag_ones(width):
    idx = np.arange(width) // HEAD_DIM
    return (idx[:, None] == idx[None, :]).astype(np.float32)


def _overlap_t(seq):
    ncmp = (seq - CMP_LEN) // CMP_STRIDE + 1
    cs = np.arange(ncmp)[:, None] * CMP_STRIDE
    ss = np.arange(seq // SLC_BLOCK)[None, :] * SLC_BLOCK
    ov = np.clip(np.minimum(cs + CMP_LEN, ss + SLC_BLOCK) - np.maximum(cs, ss), 0, None) / CMP_LEN
    out = np.zeros((LANES, LANES), np.float32)
    out[:seq // SLC_BLOCK, :ncmp] = ov.T
    return out


def _cmp_weights(pe, w1, w2):
    pe2 = jnp.concatenate([pe, pe], axis=1)
    w1r = w1.reshape(CMP_LEN, HEAD_DIM, CMP_HIDDEN)
    z1 = jnp.zeros_like(w1r)
    w1b = jnp.concatenate([jnp.concatenate([w1r, z1], axis=2),
                           jnp.concatenate([z1, w1r], axis=2)], axis=1).astype(BF16)
    z2 = jnp.zeros_like(w2)
    w2b = jnp.concatenate([jnp.concatenate([w2, z2], axis=1),
                           jnp.concatenate([z2, w2], axis=1)], axis=0).astype(BF16)
    return pe2, w1b, w2b


def kernel(x, attn_norm_g, w_in, q_norm_g, k_norm_cmp_g, k_norm_slc_g, k_norm_win_g, cmp_pe_k, cmp_w1_k, cmp_w2_k, cmp_pe_v, cmp_w1_v, cmp_w2_v, conv_dw_w, conv_dw_b, conv_ln_g, conv_ln_b, out_norm_nsa_g, out_norm_conv_g, w_out, ffn_norm_g, w_gate_up, w_down):
    batch, seq, d_model = x.shape
    assert d_model == D_MODEL and seq % PROJ_TILE == 0 and seq // SLC_BLOCK == 32
    depth = w_in.shape[0]
    cos_np, sin_np = _rope_tables(np.arange(seq))
    ccos_np, csin_np = _rope_tables(np.arange(seq // CMP_STRIDE) * CMP_STRIDE + CMP_LEN - 1)
    cos, sin, ccos, csin = map(jnp.asarray, (cos_np, sin_np, ccos_np, csin_np))
    bd256 = jnp.asarray(_block_diag_ones(256), BF16)
    bd128 = jnp.asarray(_block_diag_ones(LANES), BF16)
    ovl = jnp.asarray(_overlap_t(seq), BF16)

    x2 = x.reshape(batch * seq, d_model)
    for l in range(depth):
        w = w_in[l]
        o_q, o_kc, o_vc, o_ks, o_vs, o_kw, o_vw, o_gl, o_cv = (
            0, 512, 640, 768, 896, 1024, 1152, 1280, 1304)
        gl_pad = jnp.zeros((d_model, LANES - 3 * NSA_HEADS), BF16)
        w_q = _pair_heads(w[:, o_q:o_q + NSA_WIDTH], 1)
        w_cat = jnp.concatenate([piece.astype(BF16) for piece in (
            w_q,
            w[:, o_ks:o_ks + KV_WIDTH], w[:, o_kw:o_kw + KV_WIDTH],
            w[:, o_vs:o_vs + KV_WIDTH], w[:, o_vw:o_vw + KV_WIDTH],
            w[:, o_kc:o_kc + KV_WIDTH], w[:, o_vc:o_vc + KV_WIDTH],
            w[:, o_cv:o_cv + CONV_CH], w[:, o_cv + CONV_CH:o_cv + 2 * CONV_CH],
            w[:, o_gl:o_gl + 3 * NSA_HEADS], gl_pad)], axis=1)
        gq = jnp.tile(q_norm_g[l], 4)[None, :]
        gk = jnp.concatenate([jnp.tile(k_norm_slc_g[l], 2), jnp.tile(k_norm_win_g[l], 2)])[None, :]
        q, ks, kw, vs0, vs1, vw0, vw1, kc, vc, gt, hcv = _proj_call(
            x2, attn_norm_g[l][None, :], w_cat, cos, sin, gq, gk, bd256, seq)

        pek, w1k, w2k = _cmp_weights(cmp_pe_k[l], cmp_w1_k[l], cmp_w2_k[l])
        pev, w1v, w2v = _cmp_weights(cmp_pe_v[l], cmp_w1_v[l], cmp_w2_v[l])
        kcmp, vc0, vc1 = _cmp_call(kc, vc, pek, w1k, w2k, pev, w1v, w2v, ccos, csin,
                                jnp.tile(k_norm_cmp_g[l], 2)[None, :], bd128, batch, seq)

        mix_a = _attn_call(q, ks, kw, vs0, vs1, vw0, vw1, kcmp, vc0, vc1, gt, ovl,
                           _transposed_gain(_pair_heads(out_norm_nsa_g[l], 0)), batch, seq)

        w_dw = jnp.concatenate([conv_dw_w[l][:, 0, :],
                                jnp.zeros((CONV_HALO - CONV_WIDTH, CONV_CH), F32)], axis=0)
        w_dw = w_dw.reshape(CONV_HALO, CONV_CH // LANES, LANES).transpose(1, 0, 2)
        mix_b = _conv_call(hcv, w_dw, conv_dw_b[l].reshape(CONV_CH // LANES, 1, LANES),
                           conv_ln_g[l][None, :],
                           conv_ln_b[l][None, :], out_norm_conv_g[l][None, :], seq)

        wo = w_out[l]
        x2 = _ffn_call(x2, mix_a, mix_b, _pair_heads(wo[:NSA_WIDTH], 0).astype(BF16),
                       wo[NSA_WIDTH:].astype(BF16), ffn_norm_g[l][None, :],
                       w_gate_up[l][:, :FFN_HIDDEN].astype(BF16),
                       w_gate_up[l][:, FFN_HIDDEN:].astype(BF16), w_down[l].astype(BF16))
    return x2.reshape(batch, seq, d_model)
```

```python
import functools
import types

import numpy as np
import jax
import jax.numpy as jnp
from jax import lax
from jax.experimental import pallas as pl
from jax.experimental.pallas import tpu as pltpu

F32 = jnp.float32
BF16 = jnp.bfloat16

D_MODEL = 1024
HEAD_DIM = 64
NSA_HEADS = 8
NSA_KV_HEADS = 2
GQA_REP = NSA_HEADS // NSA_KV_HEADS
NSA_WIDTH = NSA_HEADS * HEAD_DIM
KV_WIDTH = NSA_KV_HEADS * HEAD_DIM
CONV_CH = D_MODEL - NSA_WIDTH
CMP_LEN = 32
CMP_STRIDE = 16
CMP_HIDDEN = 4 * HEAD_DIM
SLC_BLOCK = 64
SLC_TOPN = 8
WINDOW = 512
CONV_WIDTH = 31
FFN_HIDDEN = 2816
ROPE_THETA = 10000.0
EPS = 1e-6
NEG = -1e30
LOG2E = 1.4426950408889634
FORCE = 1e6

LANES = 128
Q_TILE = 128
TILES_PER_STEP = 2
EPILOGUE_LAG = 1
SLC_CHUNK = 256
WIN_KEYS = WINDOW + Q_TILE
PROJ_TILE = 512
CONV_TILE = 512
CONV_HALO = 32
CONV_ROWS = 128
FFN_TILE = 512
FFN_CHUNK = 256
VMEM_LIMIT = 56 * 1024 * 1024

C_Q = 0
C_KS = 512
C_KW = 640
C_VS = 768
C_VW = 896
C_KC = 1024
C_VC = 1152
C_CA = 1280
C_CG = 1792
C_GL = 2304
PROJ_COLS = 2432

_TRANS_B = (((1,), (1,)), ((), ()))


def _dot(a, b):
    return jnp.dot(a, b, preferred_element_type=F32)


def _dot_tb(a, b):
    return lax.dot_general(a, b, _TRANS_B, preferred_element_type=F32)


def _split_bf16(x):
    hi = x.astype(BF16)
    lo = (x - hi.astype(F32)).astype(BF16)
    return hi, lo


def _sigmoid(x):
    return 1.0 / (1.0 + jnp.exp(-x))


def _head_norm_rope(y, gain, bd, cos, sin, first_half, scale):
    hi, lo = _split_bf16(y * y)
    ss = _dot(hi, bd) + _dot(lo, bd)
    yn = y * lax.rsqrt(ss * (1.0 / HEAD_DIM) + EPS) * gain
    outs = []
    for j in range(y.shape[1] // LANES):
        blk = yn[:, j * LANES:(j + 1) * LANES]
        rot = jnp.where(first_half, pltpu.roll(blk, LANES - HEAD_DIM // 2, 1),
                        pltpu.roll(blk, HEAD_DIM // 2, 1))
        outs.append((blk * cos + rot * sin) * scale)
    return outs


def _value_t_with_ones(v):
    vt = v.T
    row = lax.broadcasted_iota(jnp.int32, vt.shape, 0)
    return (jnp.where(row < HEAD_DIM, vt, 1.0).astype(BF16),
            jnp.where(row < HEAD_DIM, 1.0, vt).astype(BF16))


def _proj_body(x_ref, gin_ref, w_ref, cos_ref, sin_ref, gq_ref, gk_ref, bd_ref,
               q_ref, ks_ref, kw_ref, vs0_ref, vs1_ref, vw0_ref, vw1_ref, kc_ref, vc_ref, gt_ref,
               hcv_ref):
    x = x_ref[...]
    xg = (x * gin_ref[...]).astype(BF16)
    rs = lax.rsqrt(jnp.mean(x * x, axis=-1, keepdims=True) + EPS)
    cos = cos_ref[...]
    sin = sin_ref[...]
    bd = bd_ref[...]
    lane = lax.broadcasted_iota(jnp.int32, (PROJ_TILE, LANES), 1)
    first_half = (lane & (HEAD_DIM // 2)) == 0
    held = {}

    def queries(half, y):
        blks = _head_norm_rope(y, gq_ref[...], bd, cos, sin, first_half, HEAD_DIM ** -0.5 * LOG2E)
        for j in range(2):
            q_ref[:, (2 * half + j) * LANES:(2 * half + j + 1) * LANES] = blks[j].astype(BF16)

    def keys(y):
        kblks = _head_norm_rope(y, gk_ref[...], bd, cos, sin, first_half, 1.0)
        ks_ref[...] = kblks[0].astype(BF16)
        kw_ref[...] = kblks[1].astype(BF16)

    def values(y):
        for i in range(PROJ_TILE // LANES):
            rows = y[i * LANES:(i + 1) * LANES, :]
            vs0_ref[i], vs1_ref[i] = _value_t_with_ones(rows[:, :LANES])
            vw0_ref[i], vw1_ref[i] = _value_t_with_ones(rows[:, LANES:])

    def cmp_inputs(y):
        kc_ref[...] = y[:, :LANES]
        vc_ref[...] = y[:, LANES:]

    def conv_lin(y):
        held["ca"] = y

    def conv_glu(y):
        hcv_ref[...] = held["ca"] * _sigmoid(y)

    def gates(y):
        sg = _sigmoid(y)
        for i in range(PROJ_TILE // LANES):
            gt_ref[i] = sg[i * LANES:(i + 1) * LANES, :].T[0:32, :]

    sections = [
        (C_Q, 256, functools.partial(queries, 0)),
        (C_CA, CONV_CH, conv_lin),
        (C_Q + 256, 256, functools.partial(queries, 1)),
        (C_CG, CONV_CH, conv_glu),
        (C_KS, 256, keys),
        (C_VS, 256, values),
        (C_KC, 256, cmp_inputs),
        (C_GL, LANES, gates),
    ]

    def project(sec):
        c0, width, _ = sec
        return _dot(xg, w_ref[:, c0:c0 + width])

    y_next = project(sections[0])
    for i, sec in enumerate(sections):
        y_cur = y_next
        if i + 1 < len(sections):
            y_next = project(sections[i + 1])
        sec[2](y_cur * rs)


def _proj_call(x2, gin, w_cat, cos, sin, gq, gk, bd, seq):
    t = x2.shape[0]
    nt = t // PROJ_TILE
    tiles_per_seq = seq // PROJ_TILE
    row = lambda i: (i, 0)
    const = lambda i: (0, 0)
    tab = lambda i: (i % tiles_per_seq, 0)
    blk3 = lambda i: (i, 0, 0)
    nb = PROJ_TILE // LANES
    return pl.pallas_call(
        _proj_body,
        grid=(nt,),
        in_specs=[
            pl.BlockSpec((PROJ_TILE, D_MODEL), row),
            pl.BlockSpec((1, D_MODEL), const),
            pl.BlockSpec((D_MODEL, PROJ_COLS), const),
            pl.BlockSpec((PROJ_TILE, LANES), tab),
            pl.BlockSpec((PROJ_TILE, LANES), tab),
            pl.BlockSpec((1, 256), const),
            pl.BlockSpec((1, 256), const),
            pl.BlockSpec((256, 256), const),
        ],
        out_specs=[
            pl.BlockSpec((PROJ_TILE, NSA_WIDTH), row),
            pl.BlockSpec((PROJ_TILE, LANES), row),
            pl.BlockSpec((PROJ_TILE, LANES), row),
            pl.BlockSpec((nb, LANES, LANES), blk3),
            pl.BlockSpec((nb, LANES, LANES), blk3),
            pl.BlockSpec((nb, LANES, LANES), blk3),
            pl.BlockSpec((nb, LANES, LANES), blk3),
            pl.BlockSpec((PROJ_TILE, LANES), row),
            pl.BlockSpec((PROJ_TILE, LANES), row),
            pl.BlockSpec((nb, 32, LANES), blk3),
            pl.BlockSpec((PROJ_TILE, CONV_CH), row),
        ],
        out_shape=[
            jax.ShapeDtypeStruct((t, NSA_WIDTH), BF16),
            jax.ShapeDtypeStruct((t, LANES), BF16),
            jax.ShapeDtypeStruct((t, LANES), BF16),
            jax.ShapeDtypeStruct((t // LANES, LANES, LANES), BF16),
            jax.ShapeDtypeStruct((t // LANES, LANES, LANES), BF16),
            jax.ShapeDtypeStruct((t // LANES, LANES, LANES), BF16),
            jax.ShapeDtypeStruct((t // LANES, LANES, LANES), BF16),
            jax.ShapeDtypeStruct((t, LANES), F32),
            jax.ShapeDtypeStruct((t, LANES), F32),
            jax.ShapeDtypeStruct((t // LANES, 32, LANES), F32),
            jax.ShapeDtypeStruct((t, CONV_CH), F32),
        ],
        compiler_params=pltpu.CompilerParams(
            dimension_semantics=("parallel",), vmem_limit_bytes=VMEM_LIMIT),
        name="proj",
    )(x2, gin, w_cat, cos, sin, gq, gk, bd)


def _cmp_mlp(t_ref, pe_ref, w1_ref, w2_ref):
    half = CMP_LEN // 2
    nchunk = t_ref.shape[0] // CMP_STRIDE
    p = jnp.zeros((nchunk, 2 * CMP_HIDDEN), F32)
    q = jnp.zeros((nchunk, 2 * CMP_HIDDEN), F32)
    for l in range(half):
        xl = t_ref[pl.ds(l, nchunk, stride=CMP_STRIDE), :]
        p = p + _dot((xl + pe_ref[l:l + 1, :]).astype(BF16), w1_ref[l])
        q = q + _dot((xl + pe_ref[half + l:half + l + 1, :]).astype(BF16), w1_ref[half + l])
    h1 = p + pltpu.roll(q, nchunk - 1, 0)
    act = (h1 * _sigmoid(h1)).astype(BF16)
    return _dot(act, w2_ref[...])


def _cmp_body(kc_ref, vc_ref, pek_ref, w1k_ref, w2k_ref, pev_ref, w1v_ref, w2v_ref,
              cos_ref, sin_ref, gk_ref, bd_ref, ko_ref, vt0_ref, vt1_ref):
    kraw = _cmp_mlp(kc_ref, pek_ref, w1k_ref, w2k_ref)
    lane = lax.broadcasted_iota(jnp.int32, kraw.shape, 1)
    first_half = (lane & (HEAD_DIM // 2)) == 0
    (kr,) = _head_norm_rope(kraw, gk_ref[...], bd_ref[...], cos_ref[...], sin_ref[...],
                            first_half, 1.0)
    ko_ref[...] = kr.astype(BF16)
    vraw = _cmp_mlp(vc_ref, pev_ref, w1v_ref, w2v_ref)
    vt0_ref[0], vt1_ref[0] = _value_t_with_ones(vraw)


def _cmp_call(kc, vc, pek, w1k, w2k, pev, w1v, w2v, cos, sin, gk, bd, batch, seq):
    seqblk = lambda b: (b, 0)
    c2 = lambda b: (0, 0)
    c3 = lambda b: (0, 0, 0)
    ncmp_pad = seq // CMP_STRIDE
    return pl.pallas_call(
        _cmp_body,
        grid=(batch,),
        in_specs=[
            pl.BlockSpec((seq, LANES), seqblk),
            pl.BlockSpec((seq, LANES), seqblk),
            pl.BlockSpec((CMP_LEN, LANES), c2),
            pl.BlockSpec((CMP_LEN, LANES, 2 * CMP_HIDDEN), c3),
            pl.BlockSpec((2 * CMP_HIDDEN, LANES), c2),
            pl.BlockSpec((CMP_LEN, LANES), c2),
            pl.BlockSpec((CMP_LEN, LANES, 2 * CMP_HIDDEN), c3),
            pl.BlockSpec((2 * CMP_HIDDEN, LANES), c2),
            pl.BlockSpec((ncmp_pad, LANES), c2),
            pl.BlockSpec((ncmp_pad, LANES), c2),
            pl.BlockSpec((1, LANES), c2),
            pl.BlockSpec((LANES, LANES), c2),
        ],
        out_specs=[
            pl.BlockSpec((ncmp_pad, LANES), seqblk),
            pl.BlockSpec((1, LANES, ncmp_pad), lambda b: (b, 0, 0)),
            pl.BlockSpec((1, LANES, ncmp_pad), lambda b: (b, 0, 0)),
        ],
        out_shape=[
            jax.ShapeDtypeStruct((batch * ncmp_pad, LANES), BF16),
            jax.ShapeDtypeStruct((batch, LANES, ncmp_pad), BF16),
            jax.ShapeDtypeStruct((batch, LANES, ncmp_pad), BF16),
        ],
        compiler_params=pltpu.CompilerParams(
            dimension_semantics=("parallel",), vmem_limit_bytes=VMEM_LIMIT),
        name="compress",
    )(kc, vc, pek, w1k, w2k, pev, w1v, w2v, cos, sin, gk, bd)


def _lane_cat(blocks):
    return jnp.concatenate(blocks, axis=1)


def _softmax_numer(s, bias, m_run=None):
    es, ms = [], []
    for r in range(GQA_REP):
        sm = s[:, r * LANES:(r + 1) * LANES] + bias
        m = jnp.max(sm, axis=0, keepdims=True)
        if m_run is not None:
            m = jnp.maximum(m, m_run[:, r * LANES:(r + 1) * LANES])
        es.append(jnp.exp2(sm - m))
        ms.append(m)
    return _lane_cat(es), _lane_cat(ms)


def _safe_inv(l):
    return 1.0 / jnp.where(l > 0.0, l, 1.0)


def _attn_body(*refs):
    first = pl.program_id(1) * TILES_PER_STEP
    n_max = refs[1].shape[0] // SLC_CHUNK
    for n in range(1, n_max + 1):
        pl.when((first * Q_TILE) // SLC_CHUNK + 1 == n)(functools.partial(_attn_step, n, first, *refs))


def _attn_step(n_chunks, first, *refs):
    tiles = [_tile_program(n_chunks, first + t, t, *refs) for t in range(TILES_PER_STEP)]
    items = [(tile, unit) for tile in tiles for unit in tile.units]
    due = {}
    pending = None
    s_next = items[0][0].scores(items[0][1])
    for i, (tile, unit) in enumerate(items):
        s_cur = s_next
        if i + 1 < len(items):
            s_next = items[i + 1][0].scores(items[i + 1][1])
        e, alpha = tile.softmax(unit, s_cur)
        if pending is not None:
            pending[0].weighted_values(*pending[1:])
        pending = (tile, unit, e, alpha)
        for done in due.pop(i, []):
            done.finish()
        if unit is tile.units[-1]:
            due.setdefault(i + 1 + EPILOGUE_LAG, []).append(tile)
    pending[0].weighted_values(*pending[1:])
    for i in sorted(due):
        for done in due[i]:
            done.finish()


def _tile_program(n_chunks, c, slot, q_ref, ks_ref, kw_ref, vs0_ref, vs1_ref, vw0_ref, vw1_ref, kc_ref,
                  vc0_ref, vc1_ref, gt_ref, ovl_ref, gn_ref, out_ref):
    groups = range(NSA_KV_HEADS)
    q_rows = slice(slot * Q_TILE, (slot + 1) * Q_TILE)
    vs_refs, vw_refs, vc_refs = (vs0_ref, vs1_ref), (vw0_ref, vw1_ref), (vc0_ref, vc1_ref)
    sum_row = (HEAD_DIM, 0)
    width = GQA_REP * LANES
    t_lane = c * Q_TILE + lax.broadcasted_iota(jnp.int32, (1, LANES), 1)
    lane_sq = lax.broadcasted_iota(jnp.int32, (Q_TILE, LANES), 1)
    nsel = 32
    j_idx = lax.broadcasted_iota(jnp.int32, (nsel, LANES), 0)
    cur = lax.shift_right_logical(t_lane, 6)
    n_idx = lax.broadcasted_iota(jnp.int32, (LANES, LANES), 0)
    cbias = jnp.where((n_idx * CMP_STRIDE + (CMP_LEN - 1) <= t_lane) & (n_idx < LANES - 1), 0.0, NEG)
    has_cmp = jnp.where(t_lane >= CMP_LEN - 1, 1.0, 0.0)
    has_cmp = _lane_cat([has_cmp] * GQA_REP)
    win_blk = jnp.maximum(c - WINDOW // Q_TILE, 0)
    win_start = pl.multiple_of(win_blk * Q_TILE, Q_TILE)
    kpos_w = win_start + lax.broadcasted_iota(jnp.int32, (WIN_KEYS, LANES), 0)
    wbias = jnp.where((kpos_w <= t_lane) & (kpos_w > t_lane - WINDOW), 0.0, NEG)
    row_sl = lax.broadcasted_iota(jnp.int32, (SLC_CHUNK, LANES), 0)
    zero_q = jnp.zeros((Q_TILE, LANES), BF16)

    def gate_row(g, branch):
        rows = [(g * GQA_REP + r) * 3 + branch for r in range(GQA_REP)]
        return _lane_cat([gt_ref[slot, i:i + 1, :] for i in rows])

    qg = []
    for g in groups:
        in_g = (lane_sq >= HEAD_DIM) if g else (lane_sq < HEAD_DIM)
        qg.append(jnp.concatenate(
            [jnp.where(in_g, q_ref[q_rows, r * LANES:(r + 1) * LANES], zero_q) for r in range(GQA_REP)],
            axis=0))

    o_cmp = [None] * NSA_KV_HEADS
    sel_bias = [None] * NSA_KV_HEADS

    def compressed_softmax(g, sc):
        ec, _ = _softmax_numer(sc, cbias)
        coef = has_cmp * _safe_inv(jnp.sum(ec, axis=0, keepdims=True))
        pc = ec * coef
        psum = (pc[:, 0:LANES] + pc[:, LANES:2 * LANES]
                + pc[:, 2 * LANES:3 * LANES] + pc[:, 3 * LANES:4 * LANES])
        p_hi, p_lo = _split_bf16(psum)
        imp = (_dot(ovl_ref[...], p_hi) + _dot(ovl_ref[...], p_lo))[0:nsel, :]
        valid = j_idx <= cur
        forced = (j_idx == 0) | (j_idx == cur) | (j_idx == cur - 1)
        score = jnp.where(valid, imp + jnp.where(forced, FORCE, 0.0), -FORCE)
        rank = jnp.zeros((nsel, LANES), F32)
        for jp in range(nsel):
            row = score[jp:jp + 1, :]
            beats = (row > score) | ((row == score) & (j_idx > jp))
            rank = rank + jnp.where(beats, 1.0, 0.0)
        sel_bias[g] = jnp.where((rank < float(SLC_TOPN)) & (score > -1.0), 0.0, NEG)
        return ec.astype(BF16), coef

    nsub = SLC_CHUNK // LANES
    blocks_per_chunk = SLC_CHUNK // SLC_BLOCK
    kwin = kw_ref[pl.ds(win_start, WIN_KEYS), :]
    units = [("cmp", g, 0) for g in groups] + [("win", g, 0) for g in groups]
    units += [("slc", g, kc) for kc in range(n_chunks) for g in groups]
    m_run = [None] * NSA_KV_HEADS
    acc_s = [None] * NSA_KV_HEADS
    acc_w = [None] * NSA_KV_HEADS

    def scores(unit):
        kind, g, kc = unit
        if kind == "cmp":
            keys = kc_ref[...]
        elif kind == "win":
            keys = kwin
        else:
            keys = ks_ref[kc * SLC_CHUNK:(kc + 1) * SLC_CHUNK, :]
        return _dot_tb(keys, qg[g])

    def softmax(unit, s):
        kind, g, kc = unit
        if kind == "cmp":
            return compressed_softmax(g, s)
        if kind == "win":
            e, _ = _softmax_numer(s, wbias)
            return e.astype(BF16), None
        j0 = kc * blocks_per_chunk
        bias = jnp.concatenate(
            [jnp.broadcast_to(sel_bias[g][j:j + 1, :], (SLC_BLOCK, LANES))
             for j in range(j0, j0 + blocks_per_chunk)], axis=0)
        if kc == n_chunks - 1:
            bias = jnp.where(kc * SLC_CHUNK + row_sl <= t_lane, bias, NEG)
        e, m_new = _softmax_numer(s, bias, m_run[g])
        alpha = None if kc == 0 else jnp.exp2(m_run[g] - m_new)
        m_run[g] = m_new
        return e.astype(BF16), alpha

    def weighted_values(unit, e, alpha):
        kind, g, kc = unit
        if kind == "cmp":
            o_cmp[g] = _dot(vc_refs[g][0], e) * (gate_row(g, 0) * alpha)
            return
        if kind == "win":
            vt = _lane_cat([vw_refs[g][win_blk + i] for i in range(WIN_KEYS // LANES)])
            acc_w[g] = _dot(vt, e)
            return
        vt = _lane_cat([vs_refs[g][nsub * kc + i] for i in range(nsub)])
        pv = _dot(vt, e)
        acc_s[g] = pv if alpha is None else acc_s[g] * alpha + pv

    def finish():
        o_groups = []
        for g in groups:
            r0 = sum_row[g]
            o_groups.append(o_cmp[g]
                            + acc_s[g] * (gate_row(g, 1) * _safe_inv(acc_s[g][r0:r0 + 1, :]))
                            + acc_w[g] * (gate_row(g, 2) * _safe_inv(acc_w[g][r0:r0 + 1, :])))
        row_d = lax.broadcasted_iota(jnp.int32, (LANES, width), 0)
        o_t = jnp.where(row_d < HEAD_DIM, o_groups[0], o_groups[1])
        sq = jnp.sum(o_t * o_t, axis=0, keepdims=True)
        ss = (sq[:, 0:LANES] + sq[:, LANES:2 * LANES]
              + sq[:, 2 * LANES:3 * LANES] + sq[:, 3 * LANES:4 * LANES])
        rs = lax.rsqrt(ss * (1.0 / NSA_WIDTH) + EPS)
        for r in range(GQA_REP):
            cols = slice(r * LANES, (r + 1) * LANES)
            out_ref[q_rows, cols] = (o_t[:, cols] * rs * gn_ref[:, cols]).T.astype(BF16)

    return types.SimpleNamespace(units=units, scores=scores, softmax=softmax,
                                 weighted_values=weighted_values, finish=finish)


def _attn_call(q, ks, kw, vs0, vs1, vw0, vw1, kcmp, vc0, vc1, gt, ovl, gn, batch, seq):
    assert SLC_CHUNK % (TILES_PER_STEP * Q_TILE) == 0
    rows = TILES_PER_STEP * Q_TILE
    nq = seq // rows
    nkb = seq // LANES
    qrow = lambda b, c: (b * nq + c, 0)
    per_b = lambda b, c: (b, 0)
    per_b3 = lambda b, c: (b, 0, 0)
    vspec = pl.BlockSpec((nkb, LANES, LANES), per_b3)
    cspec = pl.BlockSpec((1, LANES, LANES), per_b3)
    return pl.pallas_call(
        _attn_body,
        grid=(batch, nq),
        in_specs=[
            pl.BlockSpec((rows, NSA_WIDTH), qrow),
            pl.BlockSpec((seq, LANES), per_b),
            pl.BlockSpec((seq, LANES), per_b),
            vspec, vspec, vspec, vspec,
            pl.BlockSpec((LANES, LANES), per_b),
            cspec, cspec,
            pl.BlockSpec((TILES_PER_STEP, 32, LANES), lambda b, c: (b * nq + c, 0, 0)),
            pl.BlockSpec((LANES, LANES), lambda b, c: (0, 0)),
            pl.BlockSpec((LANES, NSA_WIDTH), lambda b, c: (0, 0)),
        ],
        out_specs=pl.BlockSpec((rows, NSA_WIDTH), qrow),
        out_shape=jax.ShapeDtypeStruct((batch * seq, NSA_WIDTH), BF16),
        compiler_params=pltpu.CompilerParams(
            dimension_semantics=("parallel", "arbitrary"), vmem_limit_bytes=VMEM_LIMIT),
        name="attn",
    )(q, ks, kw, vs0, vs1, vw0, vw1, kcmp, vc0, vc1, gt, ovl, gn)


def _conv_body(tiles_per_seq, cur_ref, halo_ref, w_ref, b_ref, lng_ref, lnb_ref, gn_ref, out_ref,
               buf_ref, sh_ref, cv_ref):
    first = (pl.program_id(0) % tiles_per_seq) == 0
    buf_ref[0:CONV_HALO, :] = jnp.where(first, 0.0, halo_ref[...])
    buf_ref[CONV_HALO:, :] = cur_ref[...]
    lead = CONV_HALO - (CONV_WIDTH - 1)
    ncb = CONV_CH // LANES
    nrb = CONV_TILE // CONV_ROWS
    for s in range(8):
        span = CONV_TILE + 8 * ((CONV_WIDTH - 1 - s) // 8)
        for cb in range(ncb):
            sh_ref[s, cb, 0:span, :] = buf_ref[lead + s:lead + s + span, cb * LANES:(cb + 1) * LANES]

    def conv_block(i, carry):
        cb = i % ncb
        base = pl.multiple_of((i // ncb) * CONV_ROWS, CONV_ROWS)
        acc = jnp.zeros((CONV_ROWS, LANES), F32) + b_ref[cb]
        for k in range(CONV_WIDTH):
            acc = acc + w_ref[cb, k:k + 1, :] * sh_ref[k % 8, cb, pl.ds(base + 8 * (k // 8), CONV_ROWS), :]
        cv_ref[cb, pl.ds(base, CONV_ROWS), :] = acc
        return carry

    lax.fori_loop(0, nrb * ncb, conv_block, 0)
    h = _lane_cat([cv_ref[cb] for cb in range(ncb)])
    mu = jnp.mean(h, axis=-1, keepdims=True)
    d = h - mu
    var = jnp.mean(d * d, axis=-1, keepdims=True)
    hn = d * lax.rsqrt(var + EPS) * lng_ref[...] + lnb_ref[...]
    o = hn * _sigmoid(hn)
    ms = jnp.mean(o * o, axis=-1, keepdims=True)
    out_ref[...] = (o * lax.rsqrt(ms + EPS) * gn_ref[...]).astype(BF16)


def _conv_call(hcv, w, b, lng, lnb, gn, seq):
    t = hcv.shape[0]
    nt = t // CONV_TILE
    halo_per_tile = CONV_TILE // CONV_HALO
    ncb = CONV_CH // LANES
    row = lambda i: (i, 0)
    const = lambda i: (0, 0)
    return pl.pallas_call(
        functools.partial(_conv_body, seq // CONV_TILE),
        grid=(nt,),
        in_specs=[
            pl.BlockSpec((CONV_TILE, CONV_CH), row),
            pl.BlockSpec((CONV_HALO, CONV_CH), lambda i: (jnp.maximum(i * halo_per_tile - 1, 0), 0)),
            pl.BlockSpec((ncb, CONV_HALO, LANES), lambda i: (0, 0, 0)),
            pl.BlockSpec((ncb, 1, LANES), lambda i: (0, 0, 0)),
            pl.BlockSpec((1, CONV_CH), const),
            pl.BlockSpec((1, CONV_CH), const),
            pl.BlockSpec((1, CONV_CH), const),
        ],
        out_specs=pl.BlockSpec((CONV_TILE, CONV_CH), row),
        out_shape=jax.ShapeDtypeStruct((t, CONV_CH), BF16),
        scratch_shapes=[pltpu.VMEM((CONV_HALO + CONV_TILE, CONV_CH), F32),
                        pltpu.VMEM((8, ncb, CONV_TILE + 8 * ((CONV_WIDTH - 1) // 8), LANES), F32),
                        pltpu.VMEM((ncb, CONV_TILE, LANES), F32)],
        compiler_params=pltpu.CompilerParams(
            dimension_semantics=("parallel",), vmem_limit_bytes=VMEM_LIMIT),
        name="conv",
    )(hcv, hcv, w, b, lng, lnb, gn)


def _ffn_body(x_ref, ma_ref, mb_ref, woa_ref, wob_ref, gf_ref, wg_ref, wu_ref, wd_ref, out_ref):
    h = x_ref[...] + _dot(ma_ref[...], woa_ref[...]) + _dot(mb_ref[...], wob_ref[...])
    ms = jnp.mean(h * h, axis=-1, keepdims=True)
    hn = (h * lax.rsqrt(ms + EPS) * gf_ref[...]).astype(BF16)
    acc = jnp.zeros_like(h)
    for j in range(FFN_HIDDEN // FFN_CHUNK):
        cols = slice(j * FFN_CHUNK, (j + 1) * FFN_CHUNK)
        gte = _dot(hn, wg_ref[:, cols])
        up = _dot(hn, wu_ref[:, cols])
        act = (gte * _sigmoid(gte) * up).astype(BF16)
        acc = acc + _dot(act, wd_ref[cols, :])
    out_ref[...] = h + acc


def _ffn_call(x2, ma, mb, woa, wob, gf, wg, wu, wd):
    t = x2.shape[0]
    nt = t // FFN_TILE
    row = lambda i: (i, 0)
    const = lambda i: (0, 0)
    once = pl.Buffered(1)
    return pl.pallas_call(
        _ffn_body,
        grid=(nt,),
        in_specs=[
            pl.BlockSpec((FFN_TILE, D_MODEL), row),
            pl.BlockSpec((FFN_TILE, NSA_WIDTH), row),
            pl.BlockSpec((FFN_TILE, CONV_CH), row),
            pl.BlockSpec((NSA_WIDTH, D_MODEL), const, pipeline_mode=once),
            pl.BlockSpec((CONV_CH, D_MODEL), const, pipeline_mode=once),
            pl.BlockSpec((1, D_MODEL), const),
            pl.BlockSpec((D_MODEL, FFN_HIDDEN), const, pipeline_mode=once),
            pl.BlockSpec((D_MODEL, FFN_HIDDEN), const, pipeline_mode=once),
            pl.BlockSpec((FFN_HIDDEN, D_MODEL), const, pipeline_mode=once),
        ],
        out_specs=pl.BlockSpec((FFN_TILE, D_MODEL), row),
        out_shape=jax.ShapeDtypeStruct((t, D_MODEL), F32),
        compiler_params=pltpu.CompilerParams(
            dimension_semantics=("parallel",), vmem_limit_bytes=VMEM_LIMIT),
        name="ffn",
    )(x2, ma, mb, woa, wob, gf, wg, wu, wd)


def _pair_heads(a, axis):
    shape = a.shape
    a = a.reshape(shape[:axis] + (NSA_KV_HEADS, GQA_REP, HEAD_DIM) + shape[axis + 1:])
    return jnp.swapaxes(a, axis, axis + 1).reshape(shape)


def _transposed_gain(g):
    cols = g.reshape(GQA_REP, LANES).T
    return jnp.broadcast_to(cols[:, :, None], (LANES, GQA_REP, Q_TILE)).reshape(LANES, GQA_REP * Q_TILE)


def _rope_tables(pos):
    half = HEAD_DIM // 2
    inv = ROPE_THETA ** (-np.arange(half, dtype=np.float64) / half)
    ang = np.asarray(pos, np.float64)[:, None] * inv[None, :]
    cos = np.concatenate([np.cos(ang), np.cos(ang)], axis=1)
    sin = np.concatenate([-np.sin(ang), np.sin(ang)], axis=1)
    return (np.tile(cos, (1, NSA_KV_HEADS)).astype(np.float32),
            np.tile(sin, (1, NSA_KV_HEADS)).astype(np.float32))


def _block_diag_ones(width):
    idx = np.arange(width) // HEAD_DIM
    return (idx[:, None] == idx[None, :]).astype(np.float32)


def _overlap_t(seq):
    ncmp = (seq - CMP_LEN) // CMP_STRIDE + 1
    cs = np.arange(ncmp)[:, None] * CMP_STRIDE
    ss = np.arange(seq // SLC_BLOCK)[None, :] * SLC_BLOCK
    ov = np.clip(np.minimum(cs + CMP_LEN, ss + SLC_BLOCK) - np.maximum(cs, ss), 0, None) / CMP_LEN
    out = np.zeros((LANES, LANES), np.float32)
    out[:seq // SLC_BLOCK, :ncmp] = ov.T
    return out


def _cmp_weights(pe, w1, w2):
    pe2 = jnp.concatenate([pe, pe], axis=1)
    w1r = w1.reshape(CMP_LEN, HEAD_DIM, CMP_HIDDEN)
    z1 = jnp.zeros_like(w1r)
    w1b = jnp.concatenate([jnp.concatenate([w1r, z1], axis=2),
                           jnp.concatenate([z1, w1r], axis=2)], axis=1).astype(BF16)
    z2 = jnp.zeros_like(w2)
    w2b = jnp.concatenate([jnp.concatenate([w2, z2], axis=1),
                           jnp.concatenate([z2, w2], axis=1)], axis=0).astype(BF16)
    return pe2, w1b, w2b


def kernel(x, attn_norm_g, w_in, q_norm_g, k_norm_cmp_g, k_norm_slc_g, k_norm_win_g, cmp_pe_k, cmp_w1_k, cmp_w2_k, cmp_pe_v, cmp_w1_v, cmp_w2_v, conv_dw_w, conv_dw_b, conv_ln_g, conv_ln_b, out_norm_nsa_g, out_norm_conv_g, w_out, ffn_norm_g, w_gate_up, w_down):
    batch, seq, d_model = x.shape
    assert d_model == D_MODEL and seq % PROJ_TILE == 0 and seq // SLC_BLOCK == 32
    depth = w_in.shape[0]
    cos_np, sin_np = _rope_tables(np.arange(seq))
    ccos_np, csin_np = _rope_tables(np.arange(seq // CMP_STRIDE) * CMP_STRIDE + CMP_LEN - 1)
    cos, sin, ccos, csin = map(jnp.asarray, (cos_np, sin_np, ccos_np, csin_np))
    bd256 = jnp.asarray(_block_diag_ones(256), BF16)
    bd128 = jnp.asarray(_block_diag_ones(LANES), BF16)
    ovl = jnp.asarray(_overlap_t(seq), BF16)

    x2 = x.reshape(batch * seq, d_model)
    for l in range(depth):
        w = w_in[l]
        o_q, o_kc, o_vc, o_ks, o_vs, o_kw, o_vw, o_gl, o_cv = (
            0, 512, 640, 768, 896, 1024, 1152, 1280, 1304)
        gl_pad = jnp.zeros((d_model, LANES - 3 * NSA_HEADS), BF16)
        w_q = _pair_heads(w[:, o_q:o_q + NSA_WIDTH], 1)
        w_cat = jnp.concatenate([piece.astype(BF16) for piece in (
            w_q,
            w[:, o_ks:o_ks + KV_WIDTH], w[:, o_kw:o_kw + KV_WIDTH],
            w[:, o_vs:o_vs + KV_WIDTH], w[:, o_vw:o_vw + KV_WIDTH],
            w[:, o_kc:o_kc + KV_WIDTH], w[:, o_vc:o_vc + KV_WIDTH],
            w[:, o_cv:o_cv + CONV_CH], w[:, o_cv + CONV_CH:o_cv + 2 * CONV_CH],
            w[:, o_gl:o_gl + 3 * NSA_HEADS], gl_pad)], axis=1)
        gq = jnp.tile(q_norm_g[l], 4)[None, :]
        gk = jnp.concatenate([jnp.tile(k_norm_slc_g[l], 2), jnp.tile(k_norm_win_g[l], 2)])[None, :]
        q, ks, kw, vs0, vs1, vw0, vw1, kc, vc, gt, hcv = _proj_call(
            x2, attn_norm_g[l][None, :], w_cat, cos, sin, gq, gk, bd256, seq)

        pek, w1k, w2k = _cmp_weights(cmp_pe_k[l], cmp_w1_k[l], cmp_w2_k[l])
        pev, w1v, w2v = _cmp_weights(cmp_pe_v[l], cmp_w1_v[l], cmp_w2_v[l])
        kcmp, vc0, vc1 = _cmp_call(kc, vc, pek, w1k, w2k, pev, w1v, w2v, ccos, csin,
                                jnp.tile(k_norm_cmp_g[l], 2)[None, :], bd128, batch, seq)

        mix_a = _attn_call(q, ks, kw, vs0, vs1, vw0, vw1, kcmp, vc0, vc1, gt, ovl,
                           _transposed_gain(_pair_heads(out_norm_nsa_g[l], 0)), batch, seq)

        w_dw = jnp.concatenate([conv_dw_w[l][:, 0, :],
                                jnp.zeros((CONV_HALO - CONV_WIDTH, CONV_CH), F32)], axis=0)
        w_dw = w_dw.reshape(CONV_HALO, CONV_CH // LANES, LANES).transpose(1, 0, 2)
        mix_b = _conv_call(hcv, w_dw, conv_dw_b[l].reshape(CONV_CH // LANES, 1, LANES),
                           conv_ln_g[l][None, :],
                           conv_ln_b[l][None, :], out_norm_conv_g[l][None, :], seq)

        wo = w_out[l]
        x2 = _ffn_call(x2, mix_a, mix_b, _pair_heads(wo[:NSA_WIDTH], 0).astype(BF16),
                       wo[NSA_WIDTH:].astype(BF16), ffn_norm_g[l][None, :],
                       w_gate_up[l][:, :FFN_HIDDEN].astype(BF16),
                       w_gate_up[l][:, FFN_HIDDEN:].astype(BF16), w_down[l].astype(BF16))
    return x2.reshape(batch, seq, d_model)
```

```python
import functools
import types

import numpy as np
import jax
import jax.numpy as jnp
from jax import lax
from jax.experimental import pallas as pl
from jax.experimental.pallas import tpu as pltpu

F32 = jnp.float32
BF16 = jnp.bfloat16

D_MODEL = 1024
HEAD_DIM = 64
NSA_HEADS = 8
NSA_KV_HEADS = 2
GQA_REP = NSA_HEADS // NSA_KV_HEADS
NSA_WIDTH = NSA_HEADS * HEAD_DIM
KV_WIDTH = NSA_KV_HEADS * HEAD_DIM
CONV_CH = D_MODEL - NSA_WIDTH
CMP_LEN = 32
CMP_STRIDE = 16
CMP_HIDDEN = 4 * HEAD_DIM
SLC_BLOCK = 64
SLC_TOPN = 8
WINDOW = 512
CONV_WIDTH = 31
FFN_HIDDEN = 2816
ROPE_THETA = 10000.0
EPS = 1e-6
NEG = -1e30
LOG2E = 1.4426950408889634
FORCE = 1e6

LANES = 128
Q_TILE = 128
TILES_PER_STEP = 2
EPILOGUE_LAG = 1
SLC_CHUNK = 256
WIN_KEYS = WINDOW + Q_TILE
PROJ_TILE = 512
CONV_TILE = 512
CONV_HALO = 32
CONV_ROWS = 128
FFN_TILE = 512
FFN_CHUNK = 256
VMEM_LIMIT = 56 * 1024 * 1024

C_Q = 0
C_KS = 512
C_KW = 640
C_VS = 768
C_VW = 896
C_KC = 1024
C_VC = 1152
C_CA = 1280
C_CG = 1792
C_GL = 2304
PROJ_COLS = 2432

_TRANS_B = (((1,), (1,)), ((), ()))


def _dot(a, b):
    return jnp.dot(a, b, preferred_element_type=F32)


def _dot_tb(a, b):
    return lax.dot_general(a, b, _TRANS_B, preferred_element_type=F32)


def _split_bf16(x):
    hi = x.astype(BF16)
    lo = (x - hi.astype(F32)).astype(BF16)
    return hi, lo


def _sigmoid(x):
    return 1.0 / (1.0 + jnp.exp(-x))


def _head_norm_rope(y, gain, bd, cos, sin, first_half, scale):
    hi, lo = _split_bf16(y * y)
    ss = _dot(hi, bd) + _dot(lo, bd)
    yn = y * lax.rsqrt(ss * (1.0 / HEAD_DIM) + EPS) * gain
    outs = []
    for j in range(y.shape[1] // LANES):
        blk = yn[:, j * LANES:(j + 1) * LANES]
        rot = jnp.where(first_half, pltpu.roll(blk, LANES - HEAD_DIM // 2, 1),
                        pltpu.roll(blk, HEAD_DIM // 2, 1))
        outs.append((blk * cos + rot * sin) * scale)
    return outs


def _value_t_with_ones(v):
    vt = v.T
    row = lax.broadcasted_iota(jnp.int32, vt.shape, 0)
    return (jnp.where(row < HEAD_DIM, vt, 1.0).astype(BF16),
            jnp.where(row < HEAD_DIM, 1.0, vt).astype(BF16))


def _proj_body(x_ref, gin_ref, w_ref, cos_ref, sin_ref, gq_ref, gk_ref, bd_ref,
               q_ref, ks_ref, kw_ref, vs0_ref, vs1_ref, vw0_ref, vw1_ref, kc_ref, vc_ref, gt_ref,
               hcv_ref):
    x = x_ref[...]
    xg = (x * gin_ref[...]).astype(BF16)
    rs = lax.rsqrt(jnp.mean(x * x, axis=-1, keepdims=True) + EPS)
    cos = cos_ref[...]
    sin = sin_ref[...]
    bd = bd_ref[...]
    lane = lax.broadcasted_iota(jnp.int32, (PROJ_TILE, LANES), 1)
    first_half = (lane & (HEAD_DIM // 2)) == 0
    held = {}

    def queries(half, y):
        blks = _head_norm_rope(y, gq_ref[...], bd, cos, sin, first_half, HEAD_DIM ** -0.5 * LOG2E)
        for j in range(2):
            q_ref[:, (2 * half + j) * LANES:(2 * half + j + 1) * LANES] = blks[j].astype(BF16)

    def keys(y):
        kblks = _head_norm_rope(y, gk_ref[...], bd, cos, sin, first_half, 1.0)
        ks_ref[...] = kblks[0].astype(BF16)
        kw_ref[...] = kblks[1].astype(BF16)

    def values(y):
        for i in range(PROJ_TILE // LANES):
            rows = y[i * LANES:(i + 1) * LANES, :]
            vs0_ref[i], vs1_ref[i] = _value_t_with_ones(rows[:, :LANES])
            vw0_ref[i], vw1_ref[i] = _value_t_with_ones(rows[:, LANES:])

    def cmp_inputs(y):
        kc_ref[...] = y[:, :LANES]
        vc_ref[...] = y[:, LANES:]

    def conv_lin(y):
        held["ca"] = y

    def conv_glu(y):
        hcv_ref[...] = held["ca"] * _sigmoid(y)

    def gates(y):
        sg = _sigmoid(y)
        for i in range(PROJ_TILE // LANES):
            gt_ref[i] = sg[i * LANES:(i + 1) * LANES, :].T[0:32, :]

    sections = [
        (C_Q, 256, functools.partial(queries, 0)),
        (C_CA, CONV_CH, conv_lin),
        (C_Q + 256, 256, functools.partial(queries, 1)),
        (C_CG, CONV_CH, conv_glu),
        (C_KS, 256, keys),
        (C_VS, 256, values),
        (C_KC, 256, cmp_inputs),
        (C_GL, LANES, gates),
    ]

    def project(sec):
        c0, width, _ = sec
        return _dot(xg, w_ref[:, c0:c0 + width])

    y_next = project(sections[0])
    for i, sec in enumerate(sections):
        y_cur = y_next
        if i + 1 < len(sections):
            y_next = project(sections[i + 1])
        sec[2](y_cur * rs)


def _proj_call(x2, gin, w_cat, cos, sin, gq, gk, bd, seq):
    t = x2.shape[0]
    nt = t // PROJ_TILE
    tiles_per_seq = seq // PROJ_TILE
    row = lambda i: (i, 0)
    const = lambda i: (0, 0)
    tab = lambda i: (i % tiles_per_seq, 0)
    blk3 = lambda i: (i, 0, 0)
    nb = PROJ_TILE // LANES
    return pl.pallas_call(
        _proj_body,
        grid=(nt,),
        in_specs=[
            pl.BlockSpec((PROJ_TILE, D_MODEL), row),
            pl.BlockSpec((1, D_MODEL), const),
            pl.BlockSpec((D_MODEL, PROJ_COLS), const),
            pl.BlockSpec((PROJ_TILE, LANES), tab),
            pl.BlockSpec((PROJ_TILE, LANES), tab),
            pl.BlockSpec((1, 256), const),
            pl.BlockSpec((1, 256), const),
            pl.BlockSpec((256, 256), const),
        ],
        out_specs=[
            pl.BlockSpec((PROJ_TILE, NSA_WIDTH), row),
            pl.BlockSpec((PROJ_TILE, LANES), row),
            pl.BlockSpec((PROJ_TILE, LANES), row),
            pl.BlockSpec((nb, LANES, LANES), blk3),
            pl.BlockSpec((nb, LANES, LANES), blk3),
            pl.BlockSpec((nb, LANES, LANES), blk3),
            pl.BlockSpec((nb, LANES, LANES), blk3),
            pl.BlockSpec((PROJ_TILE, LANES), row),
            pl.BlockSpec((PROJ_TILE, LANES), row),
            pl.BlockSpec((nb, 32, LANES), blk3),
            pl.BlockSpec((PROJ_TILE, CONV_CH), row),
        ],
        out_shape=[
            jax.ShapeDtypeStruct((t, NSA_WIDTH), BF16),
            jax.ShapeDtypeStruct((t, LANES), BF16),
            jax.ShapeDtypeStruct((t, LANES), BF16),
            jax.ShapeDtypeStruct((t // LANES, LANES, LANES), BF16),
            jax.ShapeDtypeStruct((t // LANES, LANES, LANES), BF16),
            jax.ShapeDtypeStruct((t // LANES, LANES, LANES), BF16),
            jax.ShapeDtypeStruct((t // LANES, LANES, LANES), BF16),
            jax.ShapeDtypeStruct((t, LANES), F32),
            jax.ShapeDtypeStruct((t, LANES), F32),
            jax.ShapeDtypeStruct((t // LANES, 32, LANES), F32),
            jax.ShapeDtypeStruct((t, CONV_CH), F32),
        ],
        compiler_params=pltpu.CompilerParams(
            dimension_semantics=("parallel",), vmem_limit_bytes=VMEM_LIMIT),
        name="proj",
    )(x2, gin, w_cat, cos, sin, gq, gk, bd)


def _cmp_mlp(t_ref, pe_ref, w1_ref, w2_ref):
    half = CMP_LEN // 2
    nchunk = t_ref.shape[0] // CMP_STRIDE
    p = jnp.zeros((nchunk, 2 * CMP_HIDDEN), F32)
    q = jnp.zeros((nchunk, 2 * CMP_HIDDEN), F32)
    for l in range(half):
        xl = t_ref[pl.ds(l, nchunk, stride=CMP_STRIDE), :]
        p = p + _dot((xl + pe_ref[l:l + 1, :]).astype(BF16), w1_ref[l])
        q = q + _dot((xl + pe_ref[half + l:half + l + 1, :]).astype(BF16), w1_ref[half + l])
    h1 = p + pltpu.roll(q, nchunk - 1, 0)
    act = (h1 * _sigmoid(h1)).astype(BF16)
    return _dot(act, w2_ref[...])


def _cmp_body(kc_ref, vc_ref, pek_ref, w1k_ref, w2k_ref, pev_ref, w1v_ref, w2v_ref,
              cos_ref, sin_ref, gk_ref, bd_ref, ko_ref, vt0_ref, vt1_ref):
    kraw = _cmp_mlp(kc_ref, pek_ref, w1k_ref, w2k_ref)
    lane = lax.broadcasted_iota(jnp.int32, kraw.shape, 1)
    first_half = (lane & (HEAD_DIM // 2)) == 0
    (kr,) = _head_norm_rope(kraw, gk_ref[...], bd_ref[...], cos_ref[...], sin_ref[...],
                            first_half, 1.0)
    ko_ref[...] = kr.astype(BF16)
    vraw = _cmp_mlp(vc_ref, pev_ref, w1v_ref, w2v_ref)
    vt0_ref[0], vt1_ref[0] = _value_t_with_ones(vraw)


def _cmp_call(kc, vc, pek, w1k, w2k, pev, w1v, w2v, cos, sin, gk, bd, batch, seq):
    seqblk = lambda b: (b, 0)
    c2 = lambda b: (0, 0)
    c3 = lambda b: (0, 0, 0)
    ncmp_pad = seq // CMP_STRIDE
    return pl.pallas_call(
        _cmp_body,
        grid=(batch,),
        in_specs=[
            pl.BlockSpec((seq, LANES), seqblk),
            pl.BlockSpec((seq, LANES), seqblk),
            pl.BlockSpec((CMP_LEN, LANES), c2),
            pl.BlockSpec((CMP_LEN, LANES, 2 * CMP_HIDDEN), c3),
            pl.BlockSpec((2 * CMP_HIDDEN, LANES), c2),
            pl.BlockSpec((CMP_LEN, LANES), c2),
            pl.BlockSpec((CMP_LEN, LANES, 2 * CMP_HIDDEN), c3),
            pl.BlockSpec((2 * CMP_HIDDEN, LANES), c2),
            pl.BlockSpec((ncmp_pad, LANES), c2),
            pl.BlockSpec((ncmp_pad, LANES), c2),
            pl.BlockSpec((1, LANES), c2),
            pl.BlockSpec((LANES, LANES), c2),
        ],
        out_specs=[
            pl.BlockSpec((ncmp_pad, LANES), seqblk),
            pl.BlockSpec((1, LANES, ncmp_pad), lambda b: (b, 0, 0)),
            pl.BlockSpec((1, LANES, ncmp_pad), lambda b: (b, 0, 0)),
        ],
        out_shape=[
            jax.ShapeDtypeStruct((batch * ncmp_pad, LANES), BF16),
            jax.ShapeDtypeStruct((batch, LANES, ncmp_pad), BF16),
            jax.ShapeDtypeStruct((batch, LANES, ncmp_pad), BF16),
        ],
        compiler_params=pltpu.CompilerParams(
            dimension_semantics=("parallel",), vmem_limit_bytes=VMEM_LIMIT),
        name="compress",
    )(kc, vc, pek, w1k, w2k, pev, w1v, w2v, cos, sin, gk, bd)


def _lane_cat(blocks):
    return jnp.concatenate(blocks, axis=1)


def _softmax_numer(s, bias, m_run=None):
    es, ms = [], []
    for r in range(GQA_REP):
        sm = s[:, r * LANES:(r + 1) * LANES] + bias
        m = jnp.max(sm, axis=0, keepdims=True)
        if m_run is not None:
            m = jnp.maximum(m, m_run[:, r * LANES:(r + 1) * LANES])
        es.append(jnp.exp2(sm - m))
        ms.append(m)
    return _lane_cat(es), _lane_cat(ms)


def _safe_inv(l):
    return 1.0 / jnp.where(l > 0.0, l, 1.0)


def _attn_body(*refs):
    first = pl.program_id(0) * TILES_PER_STEP
    n_max = refs[1].shape[0] // SLC_CHUNK
    for n in range(1, n_max + 1):
        pl.when((first * Q_TILE) // SLC_CHUNK + 1 == n)(functools.partial(_attn_step, n, first, *refs))


def _attn_step(n_chunks, first, *refs):
    tiles = [_tile_program(n_chunks, first + t, t, *refs) for t in range(TILES_PER_STEP)]
    items = [(tile, unit) for tile in tiles for unit in tile.units]
    due = {}
    pending = None
    s_next = items[0][0].scores(items[0][1])
    for i, (tile, unit) in enumerate(items):
        s_cur = s_next
        if i + 1 < len(items):
            s_next = items[i + 1][0].scores(items[i + 1][1])
        e, alpha = tile.softmax(unit, s_cur)
        if pending is not None:
            pending[0].weighted_values(*pending[1:])
        pending = (tile, unit, e, alpha)
        for done in due.pop(i, []):
            done.finish()
        if unit is tile.units[-1]:
            due.setdefault(i + 1 + EPILOGUE_LAG, []).append(tile)
    pending[0].weighted_values(*pending[1:])
    for i in sorted(due):
        for done in due[i]:
            done.finish()


def _tile_program(n_chunks, c, slot, q_ref, ks_ref, kw_ref, vs0_ref, vs1_ref, vw0_ref, vw1_ref, kc_ref,
                  vc0_ref, vc1_ref, gt_ref, ovl_ref, gn_ref, out_ref):
    groups = range(NSA_KV_HEADS)
    q_rows = slice(slot * Q_TILE, (slot + 1) * Q_TILE)
    vs_refs, vw_refs, vc_refs = (vs0_ref, vs1_ref), (vw0_ref, vw1_ref), (vc0_ref, vc1_ref)
    sum_row = (HEAD_DIM, 0)
    width = GQA_REP * LANES
    t_lane = c * Q_TILE + lax.broadcasted_iota(jnp.int32, (1, LANES), 1)
    lane_sq = lax.broadcasted_iota(jnp.int32, (Q_TILE, LANES), 1)
    nsel = 32
    j_idx = lax.broadcasted_iota(jnp.int32, (nsel, LANES), 0)
    cur = lax.shift_right_logical(t_lane, 6)
    n_idx = lax.broadcasted_iota(jnp.int32, (LANES, LANES), 0)
    cbias = jnp.where((n_idx * CMP_STRIDE + (CMP_LEN - 1) <= t_lane) & (n_idx < LANES - 1), 0.0, NEG)
    has_cmp = jnp.where(t_lane >= CMP_LEN - 1, 1.0, 0.0)
    has_cmp = _lane_cat([has_cmp] * GQA_REP)
    win_blk = jnp.maximum(c - WINDOW // Q_TILE, 0)
    win_start = pl.multiple_of(win_blk * Q_TILE, Q_TILE)
    kpos_w = win_start + lax.broadcasted_iota(jnp.int32, (WIN_KEYS, LANES), 0)
    wbias = jnp.where((kpos_w <= t_lane) & (kpos_w > t_lane - WINDOW), 0.0, NEG)
    row_sl = lax.broadcasted_iota(jnp.int32, (SLC_CHUNK, LANES), 0)
    zero_q = jnp.zeros((Q_TILE, LANES), BF16)

    def gate_row(g, branch):
        rows = [(g * GQA_REP + r) * 3 + branch for r in range(GQA_REP)]
        return _lane_cat([gt_ref[slot, i:i + 1, :] for i in rows])

    qg = []
    for g in groups:
        in_g = (lane_sq >= HEAD_DIM) if g else (lane_sq < HEAD_DIM)
        qg.append(jnp.concatenate(
            [jnp.where(in_g, q_ref[q_rows, r * LANES:(r + 1) * LANES], zero_q) for r in range(GQA_REP)],
            axis=0))

    o_cmp = [None] * NSA_KV_HEADS
    sel_bias = [None] * NSA_KV_HEADS

    def compressed_softmax(g, sc):
        ec, _ = _softmax_numer(sc, cbias)
        coef = has_cmp * _safe_inv(jnp.sum(ec, axis=0, keepdims=True))
        pc = ec * coef
        psum = (pc[:, 0:LANES] + pc[:, LANES:2 * LANES]
                + pc[:, 2 * LANES:3 * LANES] + pc[:, 3 * LANES:4 * LANES])
        p_hi, p_lo = _split_bf16(psum)
        imp = (_dot(ovl_ref[...], p_hi) + _dot(ovl_ref[...], p_lo))[0:nsel, :]
        valid = j_idx <= cur
        forced = (j_idx == 0) | (j_idx == cur) | (j_idx == cur - 1)
        score = jnp.where(valid, imp + jnp.where(forced, FORCE, 0.0), -FORCE)
        rank = jnp.zeros((nsel, LANES), F32)
        for jp in range(nsel):
            row = score[jp:jp + 1, :]
            beats = (row > score) | ((row == score) & (j_idx > jp))
            rank = rank + jnp.where(beats, 1.0, 0.0)
        sel_bias[g] = jnp.where((rank < float(SLC_TOPN)) & (score > -1.0), 0.0, NEG)
        return ec.astype(BF16), coef

    nsub = SLC_CHUNK // LANES
    blocks_per_chunk = SLC_CHUNK // SLC_BLOCK
    kwin = kw_ref[pl.ds(win_start, WIN_KEYS), :]
    units = [("cmp", g, 0) for g in groups] + [("win", g, 0) for g in groups]
    units += [("slc", g, kc) for kc in range(n_chunks) for g in groups]
    m_run = [None] * NSA_KV_HEADS
    acc_s = [None] * NSA_KV_HEADS
    acc_w = [None] * NSA_KV_HEADS

    def scores(unit):
        kind, g, kc = unit
        if kind == "cmp":
            keys = kc_ref[...]
        elif kind == "win":
            keys = kwin
        else:
            keys = ks_ref[kc * SLC_CHUNK:(kc + 1) * SLC_CHUNK, :]
        return _dot_tb(keys, qg[g])

    def softmax(unit, s):
        kind, g, kc = unit
        if kind == "cmp":
            return compressed_softmax(g, s)
        if kind == "win":
            e, _ = _softmax_numer(s, wbias)
            return e.astype(BF16), None
        j0 = kc * blocks_per_chunk
        bias = jnp.concatenate(
            [jnp.broadcast_to(sel_bias[g][j:j + 1, :], (SLC_BLOCK, LANES))
             for j in range(j0, j0 + blocks_per_chunk)], axis=0)
        if kc == n_chunks - 1:
            bias = jnp.where(kc * SLC_CHUNK + row_sl <= t_lane, bias, NEG)
        e, m_new = _softmax_numer(s, bias, m_run[g])
        alpha = None if kc == 0 else jnp.exp2(m_run[g] - m_new)
        m_run[g] = m_new
        return e.astype(BF16), alpha

    def weighted_values(unit, e, alpha):
        kind, g, kc = unit
        if kind == "cmp":
            o_cmp[g] = _dot(vc_refs[g][0], e) * (gate_row(g, 0) * alpha)
            return
        if kind == "win":
            vt = _lane_cat([vw_refs[g][win_blk + i] for i in range(WIN_KEYS // LANES)])
            acc_w[g] = _dot(vt, e)
            return
        vt = _lane_cat([vs_refs[g][nsub * kc + i] for i in range(nsub)])
        pv = _dot(vt, e)
        acc_s[g] = pv if alpha is None else acc_s[g] * alpha + pv

    def finish():
        o_groups = []
        for g in groups:
            r0 = sum_row[g]
            o_groups.append(o_cmp[g]
                            + acc_s[g] * (gate_row(g, 1) * _safe_inv(acc_s[g][r0:r0 + 1, :]))
                            + acc_w[g] * (gate_row(g, 2) * _safe_inv(acc_w[g][r0:r0 + 1, :])))
        row_d = lax.broadcasted_iota(jnp.int32, (LANES, width), 0)
        o_t = jnp.where(row_d < HEAD_DIM, o_groups[0], o_groups[1])
        sq = jnp.sum(o_t * o_t, axis=0, keepdims=True)
        ss = (sq[:, 0:LANES] + sq[:, LANES:2 * LANES]
              + sq[:, 2 * LANES:3 * LANES] + sq[:, 3 * LANES:4 * LANES])
        rs = lax.rsqrt(ss * (1.0 / NSA_WIDTH) + EPS)
        for r in range(GQA_REP):
            cols = slice(r * LANES, (r + 1) * LANES)
            out_ref[q_rows, cols] = (o_t[:, cols] * rs * gn_ref[:, cols]).T.astype(BF16)

    return types.SimpleNamespace(units=units, scores=scores, softmax=softmax,
                                 weighted_values=weighted_values, finish=finish)


def _attn_call(q, ks, kw, vs0, vs1, vw0, vw1, kcmp, vc0, vc1, gt, ovl, gn, batch, seq):
    assert SLC_CHUNK % (TILES_PER_STEP * Q_TILE) == 0
    rows = TILES_PER_STEP * Q_TILE
    nq = seq // rows
    nkb = seq // LANES
    qrow = lambda c, b: (b * nq + c, 0)
    per_b = lambda c, b: (b, 0)
    per_b3 = lambda c, b: (b, 0, 0)
    vspec = pl.BlockSpec((nkb, LANES, LANES), per_b3)
    cspec = pl.BlockSpec((1, LANES, LANES), per_b3)
    return pl.pallas_call(
        _attn_body,
        grid=(nq, batch),
        in_specs=[
            pl.BlockSpec((rows, NSA_WIDTH), qrow),
            pl.BlockSpec((seq, LANES), per_b),
            pl.BlockSpec((seq, LANES), per_b),
            vspec, vspec, vspec, vspec,
            pl.BlockSpec((LANES, LANES), per_b),
            cspec, cspec,
            pl.BlockSpec((TILES_PER_STEP, 32, LANES), lambda c, b: (b * nq + c, 0, 0)),
            pl.BlockSpec((LANES, LANES), lambda c, b: (0, 0)),
            pl.BlockSpec((LANES, NSA_WIDTH), lambda c, b: (0, 0)),
        ],
        out_specs=pl.BlockSpec((rows, NSA_WIDTH), qrow),
        out_shape=jax.ShapeDtypeStruct((batch * seq, NSA_WIDTH), BF16),
        compiler_params=pltpu.CompilerParams(
            dimension_semantics=("parallel", "arbitrary"), vmem_limit_bytes=VMEM_LIMIT),
        name="attn",
    )(q, ks, kw, vs0, vs1, vw0, vw1, kcmp, vc0, vc1, gt, ovl, gn)


def _conv_body(tiles_per_seq, cur_ref, halo_ref, w_ref, b_ref, lng_ref, lnb_ref, gn_ref, out_ref,
               buf_ref, sh_ref, cv_ref):
    first = (pl.program_id(0) % tiles_per_seq) == 0
    buf_ref[0:CONV_HALO, :] = jnp.where(first, 0.0, halo_ref[...])
    buf_ref[CONV_HALO:, :] = cur_ref[...]
    lead = CONV_HALO - (CONV_WIDTH - 1)
    ncb = CONV_CH // LANES
    nrb = CONV_TILE // CONV_ROWS
    for s in range(8):
        span = CONV_TILE + 8 * ((CONV_WIDTH - 1 - s) // 8)
        for cb in range(ncb):
            sh_ref[s, cb, 0:span, :] = buf_ref[lead + s:lead + s + span, cb * LANES:(cb + 1) * LANES]

    def conv_block(i, carry):
        cb = i % ncb
        base = pl.multiple_of((i // ncb) * CONV_ROWS, CONV_ROWS)
        acc = jnp.zeros((CONV_ROWS, LANES), F32) + b_ref[cb]
        for k in range(CONV_WIDTH):
            acc = acc + w_ref[cb, k:k + 1, :] * sh_ref[k % 8, cb, pl.ds(base + 8 * (k // 8), CONV_ROWS), :]
        cv_ref[cb, pl.ds(base, CONV_ROWS), :] = acc
        return carry

    lax.fori_loop(0, nrb * ncb, conv_block, 0)
    h = _lane_cat([cv_ref[cb] for cb in range(ncb)])
    mu = jnp.mean(h, axis=-1, keepdims=True)
    d = h - mu
    var = jnp.mean(d * d, axis=-1, keepdims=True)
    hn = d * lax.rsqrt(var + EPS) * lng_ref[...] + lnb_ref[...]
    o = hn * _sigmoid(hn)
    ms = jnp.mean(o * o, axis=-1, keepdims=True)
    out_ref[...] = (o * lax.rsqrt(ms + EPS) * gn_ref[...]).astype(BF16)


def _conv_call(hcv, w, b, lng, lnb, gn, seq):
    t = hcv.shape[0]
    nt = t // CONV_TILE
    halo_per_tile = CONV_TILE // CONV_HALO
    ncb = CONV_CH // LANES
    row = lambda i: (i, 0)
    const = lambda i: (0, 0)
    return pl.pallas_call(
        functools.partial(_conv_body, seq // CONV_TILE),
        grid=(nt,),
        in_specs=[
            pl.BlockSpec((CONV_TILE, CONV_CH), row),
            pl.BlockSpec((CONV_HALO, CONV_CH), lambda i: (jnp.maximum(i * halo_per_tile - 1, 0), 0)),
            pl.BlockSpec((ncb, CONV_HALO, LANES), lambda i: (0, 0, 0)),
            pl.BlockSpec((ncb, 1, LANES), lambda i: (0, 0, 0)),
            pl.BlockSpec((1, CONV_CH), const),
            pl.BlockSpec((1, CONV_CH), const),
            pl.BlockSpec((1, CONV_CH), const),
        ],
        out_specs=pl.BlockSpec((CONV_TILE, CONV_CH), row),
        out_shape=jax.ShapeDtypeStruct((t, CONV_CH), BF16),
        scratch_shapes=[pltpu.VMEM((CONV_HALO + CONV_TILE, CONV_CH), F32),
                        pltpu.VMEM((8, ncb, CONV_TILE + 8 * ((CONV_WIDTH - 1) // 8), LANES), F32),
                        pltpu.VMEM((ncb, CONV_TILE, LANES), F32)],
        compiler_params=pltpu.CompilerParams(
            dimension_semantics=("parallel",), vmem_limit_bytes=VMEM_LIMIT),
        name="conv",
    )(hcv, hcv, w, b, lng, lnb, gn)


def _ffn_body(x_ref, ma_ref, mb_ref, woa_ref, wob_ref, gf_ref, wg_ref, wu_ref, wd_ref, out_ref):
    h = x_ref[...] + _dot(ma_ref[...], woa_ref[...]) + _dot(mb_ref[...], wob_ref[...])
    ms = jnp.mean(h * h, axis=-1, keepdims=True)
    hn = (h * lax.rsqrt(ms + EPS) * gf_ref[...]).astype(BF16)
    acc = jnp.zeros_like(h)
    for j in range(FFN_HIDDEN // FFN_CHUNK):
        cols = slice(j * FFN_CHUNK, (j + 1) * FFN_CHUNK)
        gte = _dot(hn, wg_ref[:, cols])
        up = _dot(hn, wu_ref[:, cols])
        act = (gte * _sigmoid(gte) * up).astype(BF16)
        acc = acc + _dot(act, wd_ref[cols, :])
    out_ref[...] = h + acc


def _ffn_call(x2, ma, mb, woa, wob, gf, wg, wu, wd):
    t = x2.shape[0]
    nt = t // FFN_TILE
    row = lambda i: (i, 0)
    const = lambda i: (0, 0)
    once = pl.Buffered(1)
    return pl.pallas_call(
        _ffn_body,
        grid=(nt,),
        in_specs=[
            pl.BlockSpec((FFN_TILE, D_MODEL), row),
            pl.BlockSpec((FFN_TILE, NSA_WIDTH), row),
            pl.BlockSpec((FFN_TILE, CONV_CH), row),
            pl.BlockSpec((NSA_WIDTH, D_MODEL), const, pipeline_mode=once),
            pl.BlockSpec((CONV_CH, D_MODEL), const, pipeline_mode=once),
            pl.BlockSpec((1, D_MODEL), const),
            pl.BlockSpec((D_MODEL, FFN_HIDDEN), const, pipeline_mode=once),
            pl.BlockSpec((D_MODEL, FFN_HIDDEN), const, pipeline_mode=once),
            pl.BlockSpec((FFN_HIDDEN, D_MODEL), const, pipeline_mode=once),
        ],
        out_specs=pl.BlockSpec((FFN_TILE, D_MODEL), row),
        out_shape=jax.ShapeDtypeStruct((t, D_MODEL), F32),
        compiler_params=pltpu.CompilerParams(
            dimension_semantics=("parallel",), vmem_limit_bytes=VMEM_LIMIT),
        name="ffn",
    )(x2, ma, mb, woa, wob, gf, wg, wu, wd)


def _pair_heads(a, axis):
    shape = a.shape
    a = a.reshape(shape[:axis] + (NSA_KV_HEADS, GQA_REP, HEAD_DIM) + shape[axis + 1:])
    return jnp.swapaxes(a, axis, axis + 1).reshape(shape)


def _transposed_gain(g):
    cols = g.reshape(GQA_REP, LANES).T
    return jnp.broadcast_to(cols[:, :, None], (LANES, GQA_REP, Q_TILE)).reshape(LANES, GQA_REP * Q_TILE)


def _rope_tables(pos):
    half = HEAD_DIM // 2
    inv = ROPE_THETA ** (-np.arange(half, dtype=np.float64) / half)
    ang = np.asarray(pos, np.float64)[:, None] * inv[None, :]
    cos = np.concatenate([np.cos(ang), np.cos(ang)], axis=1)
    sin = np.concatenate([-np.sin(ang), np.sin(ang)], axis=1)
    return (np.tile(cos, (1, NSA_KV_HEADS)).astype(np.float32),
            np.tile(sin, (1, NSA_KV_HEADS)).astype(np.float32))


def _block_diag_ones(width):
    idx = np.arange(width) // HEAD_DIM
    return (idx[:, None] == idx[None, :]).astype(np.float32)


def _overlap_t(seq):
    ncmp = (seq - CMP_LEN) // CMP_STRIDE + 1
    cs = np.arange(ncmp)[:, None] * CMP_STRIDE
    ss = np.arange(seq // SLC_BLOCK)[None, :] * SLC_BLOCK
    ov = np.clip(np.minimum(cs + CMP_LEN, ss + SLC_BLOCK) - np.maximum(cs, ss), 0, None) / CMP_LEN
    out = np.zeros((LANES, LANES), np.float32)
    out[:seq // SLC_BLOCK, :ncmp] = ov.T
    return out


def _cmp_weights(pe, w1, w2):
    pe2 = jnp.concatenate([pe, pe], axis=1)
    w1r = w1.reshape(CMP_LEN, HEAD_DIM, CMP_HIDDEN)
    z1 = jnp.zeros_like(w1r)
    w1b = jnp.concatenate([jnp.concatenate([w1r, z1], axis=2),
                           jnp.concatenate([z1, w1r], axis=2)], axis=1).astype(BF16)
    z2 = jnp.zeros_like(w2)
    w2b = jnp.concatenate([jnp.concatenate([w2, z2], axis=1),
                           jnp.concatenate([z2, w2], axis=1)], axis=0).astype(BF16)
    return pe2, w1b, w2b


def kernel(x, attn_norm_g, w_in, q_norm_g, k_norm_cmp_g, k_norm_slc_g, k_norm_win_g, cmp_pe_k, cmp_w1_k, cmp_w2_k, cmp_pe_v, cmp_w1_v, cmp_w2_v, conv_dw_w, conv_dw_b, conv_ln_g, conv_ln_b, out_norm_nsa_g, out_norm_conv_g, w_out, ffn_norm_g, w_gate_up, w_down):
    batch, seq, d_model = x.shape
    assert d_model == D_MODEL and seq % PROJ_TILE == 0 and seq // SLC_BLOCK == 32
    depth = w_in.shape[0]
    cos_np, sin_np = _rope_tables(np.arange(seq))
    ccos_np, csin_np = _rope_tables(np.arange(seq // CMP_STRIDE) * CMP_STRIDE + CMP_LEN - 1)
    cos, sin, ccos, csin = map(jnp.asarray, (cos_np, sin_np, ccos_np, csin_np))
    bd256 = jnp.asarray(_block_diag_ones(256), BF16)
    bd128 = jnp.asarray(_block_diag_ones(LANES), BF16)
    ovl = jnp.asarray(_overlap_t(seq), BF16)

    x2 = x.reshape(batch * seq, d_model)
    for l in range(depth):
        w = w_in[l]
        o_q, o_kc, o_vc, o_ks, o_vs, o_kw, o_vw, o_gl, o_cv = (
            0, 512, 640, 768, 896, 1024, 1152, 1280, 1304)
        gl_pad = jnp.zeros((d_model, LANES - 3 * NSA_HEADS), BF16)
        w_q = _pair_heads(w[:, o_q:o_q + NSA_WIDTH], 1)
        w_cat = jnp.concatenate([piece.astype(BF16) for piece in (
            w_q,
            w[:, o_ks:o_ks + KV_WIDTH], w[:, o_kw:o_kw + KV_WIDTH],
            w[:, o_vs:o_vs + KV_WIDTH], w[:, o_vw:o_vw + KV_WIDTH],
            w[:, o_kc:o_kc + KV_WIDTH], w[:, o_vc:o_vc + KV_WIDTH],
            w[:, o_cv:o_cv + CONV_CH], w[:, o_cv + CONV_CH:o_cv + 2 * CONV_CH],
            w[:, o_gl:o_gl + 3 * NSA_HEADS], gl_pad)], axis=1)
        gq = jnp.tile(q_norm_g[l], 4)[None, :]
        gk = jnp.concatenate([jnp.tile(k_norm_slc_g[l], 2), jnp.tile(k_norm_win_g[l], 2)])[None, :]
        q, ks, kw, vs0, vs1, vw0, vw1, kc, vc, gt, hcv = _proj_call(
            x2, attn_norm_g[l][None, :], w_cat, cos, sin, gq, gk, bd256, seq)

        pek, w1k, w2k = _cmp_weights(cmp_pe_k[l], cmp_w1_k[l], cmp_w2_k[l])
        pev, w1v, w2v = _cmp_weights(cmp_pe_v[l], cmp_w1_v[l], cmp_w2_v[l])
        kcmp, vc0, vc1 = _cmp_call(kc, vc, pek, w1k, w2k, pev, w1v, w2v, ccos, csin,
                                jnp.tile(k_norm_cmp_g[l], 2)[None, :], bd128, batch, seq)

        mix_a = _attn_call(q, ks, kw, vs0, vs1, vw0, vw1, kcmp, vc0, vc1, gt, ovl,
                           _transposed_gain(_pair_heads(out_norm_nsa_g[l], 0)), batch, seq)

        w_dw = jnp.concatenate([conv_dw_w[l][:, 0, :],
                                jnp.zeros((CONV_HALO - CONV_WIDTH, CONV_CH), F32)], axis=0)
        w_dw = w_dw.reshape(CONV_HALO, CONV_CH // LANES, LANES).transpose(1, 0, 2)
        mix_b = _conv_call(hcv, w_dw, conv_dw_b[l].reshape(CONV_CH // LANES, 1, LANES),
                           conv_ln_g[l][None, :],
                           conv_ln_b[l][None, :], out_norm_conv_g[l][None, :], seq)

        wo = w_out[l]
        x2 = _ffn_call(x2, mix_a, mix_b, _pair_heads(wo[:NSA_WIDTH], 0).astype(BF16),
                       wo[NSA_WIDTH:].astype(BF16), ffn_norm_g[l][None, :],
                       w_gate_up[l][:, :FFN_HIDDEN].astype(BF16),
                       w_gate_up[l][:, FFN_HIDDEN:].astype(BF16), w_down[l].astype(BF16))
    return x2.reshape(batch, seq, d_model)
```

```python
import functools
import types

import numpy as np
import jax
import jax.numpy as jnp
from jax import lax
from jax.experimental import pallas as pl
from jax.experimental.pallas import tpu as pltpu

F32 = jnp.float32
BF16 = jnp.bfloat16

D_MODEL = 1024
HEAD_DIM = 64
NSA_HEADS = 8
NSA_KV_HEADS = 2
GQA_REP = NSA_HEADS // NSA_KV_HEADS
NSA_WIDTH = NSA_HEADS * HEAD_DIM
KV_WIDTH = NSA_KV_HEADS * HEAD_DIM
CONV_CH = D_MODEL - NSA_WIDTH
CMP_LEN = 32
CMP_STRIDE = 16
CMP_HIDDEN = 4 * HEAD_DIM
SLC_BLOCK = 64
SLC_TOPN = 8
WINDOW = 512
CONV_WIDTH = 31
FFN_HIDDEN = 2816
ROPE_THETA = 10000.0
EPS = 1e-6
NEG = -1e30
LOG2E = 1.4426950408889634
FORCE = 1e6

LANES = 128
Q_TILE = 128
TILES_PER_STEP = 1
EPILOGUE_LAG = 1
SLC_CHUNK = 256
WIN_KEYS = WINDOW + Q_TILE
PROJ_TILE = 512
CONV_TILE = 512
CONV_HALO = 32
CONV_ROWS = 128
FFN_TILE = 512
FFN_CHUNK = 256
VMEM_LIMIT = 56 * 1024 * 1024

C_Q = 0
C_KS = 512
C_KW = 640
C_VS = 768
C_VW = 896
C_KC = 1024
C_VC = 1152
C_CA = 1280
C_CG = 1792
C_GL = 2304
PROJ_COLS = 2432

_TRANS_B = (((1,), (1,)), ((), ()))


def _dot(a, b):
    return jnp.dot(a, b, preferred_element_type=F32)


def _dot_tb(a, b):
    return lax.dot_general(a, b, _TRANS_B, preferred_element_type=F32)


def _split_bf16(x):
    hi = x.astype(BF16)
    lo = (x - hi.astype(F32)).astype(BF16)
    return hi, lo


def _sigmoid(x):
    return 1.0 / (1.0 + jnp.exp(-x))


def _head_norm_rope(y, gain, bd, cos, sin, first_half, scale):
    hi, lo = _split_bf16(y * y)
    ss = _dot(hi, bd) + _dot(lo, bd)
    yn = y * lax.rsqrt(ss * (1.0 / HEAD_DIM) + EPS) * gain
    outs = []
    for j in range(y.shape[1] // LANES):
        blk = yn[:, j * LANES:(j + 1) * LANES]
        rot = jnp.where(first_half, pltpu.roll(blk, LANES - HEAD_DIM // 2, 1),
                        pltpu.roll(blk, HEAD_DIM // 2, 1))
        outs.append((blk * cos + rot * sin) * scale)
    return outs


def _value_t_with_ones(v):
    vt = v.T
    row = lax.broadcasted_iota(jnp.int32, vt.shape, 0)
    return (jnp.where(row < HEAD_DIM, vt, 1.0).astype(BF16),
            jnp.where(row < HEAD_DIM, 1.0, vt).astype(BF16))


def _proj_body(x_ref, gin_ref, w_ref, cos_ref, sin_ref, gq_ref, gk_ref, bd_ref,
               q_ref, ks_ref, kw_ref, vs0_ref, vs1_ref, vw0_ref, vw1_ref, kc_ref, vc_ref, gt_ref,
               hcv_ref):
    x = x_ref[...]
    xg = (x * gin_ref[...]).astype(BF16)
    rs = lax.rsqrt(jnp.mean(x * x, axis=-1, keepdims=True) + EPS)
    cos = cos_ref[...]
    sin = sin_ref[...]
    bd = bd_ref[...]
    lane = lax.broadcasted_iota(jnp.int32, (PROJ_TILE, LANES), 1)
    first_half = (lane & (HEAD_DIM // 2)) == 0
    held = {}

    def queries(half, y):
        blks = _head_norm_rope(y, gq_ref[...], bd, cos, sin, first_half, HEAD_DIM ** -0.5 * LOG2E)
        for j in range(2):
            q_ref[:, (2 * half + j) * LANES:(2 * half + j + 1) * LANES] = blks[j].astype(BF16)

    def keys(y):
        kblks = _head_norm_rope(y, gk_ref[...], bd, cos, sin, first_half, 1.0)
        ks_ref[...] = kblks[0].astype(BF16)
        kw_ref[...] = kblks[1].astype(BF16)

    def values(y):
        for i in range(PROJ_TILE // LANES):
            rows = y[i * LANES:(i + 1) * LANES, :]
            vs0_ref[i], vs1_ref[i] = _value_t_with_ones(rows[:, :LANES])
            vw0_ref[i], vw1_ref[i] = _value_t_with_ones(rows[:, LANES:])

    def cmp_inputs(y):
        kc_ref[...] = y[:, :LANES]
        vc_ref[...] = y[:, LANES:]

    def conv_lin(y):
        held["ca"] = y

    def conv_glu(y):
        hcv_ref[...] = held["ca"] * _sigmoid(y)

    def gates(y):
        sg = _sigmoid(y)
        for i in range(PROJ_TILE // LANES):
            gt_ref[i] = sg[i * LANES:(i + 1) * LANES, :].T[0:32, :]

    sections = [
        (C_Q, 256, functools.partial(queries, 0)),
        (C_CA, CONV_CH, conv_lin),
        (C_Q + 256, 256, functools.partial(queries, 1)),
        (C_CG, CONV_CH, conv_glu),
        (C_KS, 256, keys),
        (C_VS, 256, values),
        (C_KC, 256, cmp_inputs),
        (C_GL, LANES, gates),
    ]

    def project(sec):
        c0, width, _ = sec
        return _dot(xg, w_ref[:, c0:c0 + width])

    y_next = project(sections[0])
    for i, sec in enumerate(sections):
        y_cur = y_next
        if i + 1 < len(sections):
            y_next = project(sections[i + 1])
        sec[2](y_cur * rs)


def _proj_call(x2, gin, w_cat, cos, sin, gq, gk, bd, seq):
    t = x2.shape[0]
    nt = t // PROJ_TILE
    tiles_per_seq = seq // PROJ_TILE
    row = lambda i: (i, 0)
    const = lambda i: (0, 0)
    tab = lambda i: (i % tiles_per_seq, 0)
    blk3 = lambda i: (i, 0, 0)
    nb = PROJ_TILE // LANES
    return pl.pallas_call(
        _proj_body,
        grid=(nt,),
        in_specs=[
            pl.BlockSpec((PROJ_TILE, D_MODEL), row),
            pl.BlockSpec((1, D_MODEL), const),
            pl.BlockSpec((D_MODEL, PROJ_COLS), const),
            pl.BlockSpec((PROJ_TILE, LANES), tab),
            pl.BlockSpec((PROJ_TILE, LANES), tab),
            pl.BlockSpec((1, 256), const),
            pl.BlockSpec((1, 256), const),
            pl.BlockSpec((256, 256), const),
        ],
        out_specs=[
            pl.BlockSpec((PROJ_TILE, NSA_WIDTH), row),
            pl.BlockSpec((PROJ_TILE, LANES), row),
            pl.BlockSpec((PROJ_TILE, LANES), row),
            pl.BlockSpec((nb, LANES, LANES), blk3),
            pl.BlockSpec((nb, LANES, LANES), blk3),
            pl.BlockSpec((nb, LANES, LANES), blk3),
            pl.BlockSpec((nb, LANES, LANES), blk3),
            pl.BlockSpec((PROJ_TILE, LANES), row),
            pl.BlockSpec((PROJ_TILE, LANES), row),
            pl.BlockSpec((nb, 32, LANES), blk3),
            pl.BlockSpec((PROJ_TILE, CONV_CH), row),
        ],
        out_shape=[
            jax.ShapeDtypeStruct((t, NSA_WIDTH), BF16),
            jax.ShapeDtypeStruct((t, LANES), BF16),
            jax.ShapeDtypeStruct((t, LANES), BF16),
            jax.ShapeDtypeStruct((t // LANES, LANES, LANES), BF16),
            jax.ShapeDtypeStruct((t // LANES, LANES, LANES), BF16),
            jax.ShapeDtypeStruct((t // LANES, LANES, LANES), BF16),
            jax.ShapeDtypeStruct((t // LANES, LANES, LANES), BF16),
            jax.ShapeDtypeStruct((t, LANES), F32),
            jax.ShapeDtypeStruct((t, LANES), F32),
            jax.ShapeDtypeStruct((t // LANES, 32, LANES), F32),
            jax.ShapeDtypeStruct((t, CONV_CH), F32),
        ],
        compiler_params=pltpu.CompilerParams(
            dimension_semantics=("parallel",), vmem_limit_bytes=VMEM_LIMIT),
        name="proj",
    )(x2, gin, w_cat, cos, sin, gq, gk, bd)


def _cmp_mlp(t_ref, pe_ref, w1_ref, w2_ref):
    half = CMP_LEN // 2
    nchunk = t_ref.shape[0] // CMP_STRIDE
    p = jnp.zeros((nchunk, 2 * CMP_HIDDEN), F32)
    q = jnp.zeros((nchunk, 2 * CMP_HIDDEN), F32)
    for l in range(half):
        xl = t_ref[pl.ds(l, nchunk, stride=CMP_STRIDE), :]
        p = p + _dot((xl + pe_ref[l:l + 1, :]).astype(BF16), w1_ref[l])
        q = q + _dot((xl + pe_ref[half + l:half + l + 1, :]).astype(BF16), w1_ref[half + l])
    h1 = p + pltpu.roll(q, nchunk - 1, 0)
    act = (h1 * _sigmoid(h1)).astype(BF16)
    return _dot(act, w2_ref[...])


def _cmp_body(kc_ref, vc_ref, pek_ref, w1k_ref, w2k_ref, pev_ref, w1v_ref, w2v_ref,
              cos_ref, sin_ref, gk_ref, bd_ref, ko_ref, vt0_ref, vt1_ref):
    kraw = _cmp_mlp(kc_ref, pek_ref, w1k_ref, w2k_ref)
    lane = lax.broadcasted_iota(jnp.int32, kraw.shape, 1)
    first_half = (lane & (HEAD_DIM // 2)) == 0
    (kr,) = _head_norm_rope(kraw, gk_ref[...], bd_ref[...], cos_ref[...], sin_ref[...],
                            first_half, 1.0)
    ko_ref[...] = kr.astype(BF16)
    vraw = _cmp_mlp(vc_ref, pev_ref, w1v_ref, w2v_ref)
    vt0_ref[0], vt1_ref[0] = _value_t_with_ones(vraw)


def _cmp_call(kc, vc, pek, w1k, w2k, pev, w1v, w2v, cos, sin, gk, bd, batch, seq):
    seqblk = lambda b: (b, 0)
    c2 = lambda b: (0, 0)
    c3 = lambda b: (0, 0, 0)
    ncmp_pad = seq // CMP_STRIDE
    return pl.pallas_call(
        _cmp_body,
        grid=(batch,),
        in_specs=[
            pl.BlockSpec((seq, LANES), seqblk),
            pl.BlockSpec((seq, LANES), seqblk),
            pl.BlockSpec((CMP_LEN, LANES), c2),
            pl.BlockSpec((CMP_LEN, LANES, 2 * CMP_HIDDEN), c3),
            pl.BlockSpec((2 * CMP_HIDDEN, LANES), c2),
            pl.BlockSpec((CMP_LEN, LANES), c2),
            pl.BlockSpec((CMP_LEN, LANES, 2 * CMP_HIDDEN), c3),
            pl.BlockSpec((2 * CMP_HIDDEN, LANES), c2),
            pl.BlockSpec((ncmp_pad, LANES), c2),
            pl.BlockSpec((ncmp_pad, LANES), c2),
            pl.BlockSpec((1, LANES), c2),
            pl.BlockSpec((LANES, LANES), c2),
        ],
        out_specs=[
            pl.BlockSpec((ncmp_pad, LANES), seqblk),
            pl.BlockSpec((1, LANES, ncmp_pad), lambda b: (b, 0, 0)),
            pl.BlockSpec((1, LANES, ncmp_pad), lambda b: (b, 0, 0)),
        ],
        out_shape=[
            jax.ShapeDtypeStruct((batch * ncmp_pad, LANES), BF16),
            jax.ShapeDtypeStruct((batch, LANES, ncmp_pad), BF16),
            jax.ShapeDtypeStruct((batch, LANES, ncmp_pad), BF16),
        ],
        compiler_params=pltpu.CompilerParams(
            dimension_semantics=("parallel",), vmem_limit_bytes=VMEM_LIMIT),
        name="compress",
    )(kc, vc, pek, w1k, w2k, pev, w1v, w2v, cos, sin, gk, bd)


def _lane_cat(blocks):
    return jnp.concatenate(blocks, axis=1)


def _softmax_numer(s, bias, m_run=None):
    es, ms = [], []
    for r in range(GQA_REP):
        sm = s[:, r * LANES:(r + 1) * LANES] + bias
        m = jnp.max(sm, axis=0, keepdims=True)
        if m_run is not None:
            m = jnp.maximum(m, m_run[:, r * LANES:(r + 1) * LANES])
        es.append(jnp.exp2(sm - m))
        ms.append(m)
    return _lane_cat(es), _lane_cat(ms)


def _safe_inv(l):
    return 1.0 / jnp.where(l > 0.0, l, 1.0)


def _attn_body(*refs):
    first = pl.program_id(0) * TILES_PER_STEP
    n_max = refs[1].shape[0] // SLC_CHUNK
    for n in range(1, n_max + 1):
        pl.when((first * Q_TILE) // SLC_CHUNK + 1 == n)(functools.partial(_attn_step, n, first, *refs))


def _attn_step(n_chunks, first, *refs):
    tiles = [_tile_program(n_chunks, first + t, t, *refs) for t in range(TILES_PER_STEP)]
    items = [(tile, unit) for tile in tiles for unit in tile.units]
    due = {}
    pending = None
    s_next = items[0][0].scores(items[0][1])
    for i, (tile, unit) in enumerate(items):
        s_cur = s_next
        if i + 1 < len(items):
            s_next = items[i + 1][0].scores(items[i + 1][1])
        e, alpha = tile.softmax(unit, s_cur)
        if pending is not None:
            pending[0].weighted_values(*pending[1:])
        pending = (tile, unit, e, alpha)
        for done in due.pop(i, []):
            done.finish()
        if unit is tile.units[-1]:
            due.setdefault(i + 1 + EPILOGUE_LAG, []).append(tile)
    pending[0].weighted_values(*pending[1:])
    for i in sorted(due):
        for done in due[i]:
            done.finish()


def _tile_program(n_chunks, c, slot, q_ref, ks_ref, kw_ref, vs0_ref, vs1_ref, vw0_ref, vw1_ref, kc_ref,
                  vc0_ref, vc1_ref, gt_ref, ovl_ref, gn_ref, out_ref):
    groups = range(NSA_KV_HEADS)
    q_rows = slice(slot * Q_TILE, (slot + 1) * Q_TILE)
    vs_refs, vw_refs, vc_refs = (vs0_ref, vs1_ref), (vw0_ref, vw1_ref), (vc0_ref, vc1_ref)
    sum_row = (HEAD_DIM, 0)
    width = GQA_REP * LANES
    t_lane = c * Q_TILE + lax.broadcasted_iota(jnp.int32, (1, LANES), 1)
    lane_sq = lax.broadcasted_iota(jnp.int32, (Q_TILE, LANES), 1)
    nsel = 32
    j_idx = lax.broadcasted_iota(jnp.int32, (nsel, LANES), 0)
    cur = lax.shift_right_logical(t_lane, 6)
    n_idx = lax.broadcasted_iota(jnp.int32, (LANES, LANES), 0)
    cbias = jnp.where((n_idx * CMP_STRIDE + (CMP_LEN - 1) <= t_lane) & (n_idx < LANES - 1), 0.0, NEG)
    has_cmp = jnp.where(t_lane >= CMP_LEN - 1, 1.0, 0.0)
    has_cmp = _lane_cat([has_cmp] * GQA_REP)
    win_blk = jnp.maximum(c - WINDOW // Q_TILE, 0)
    win_start = pl.multiple_of(win_blk * Q_TILE, Q_TILE)
    kpos_w = win_start + lax.broadcasted_iota(jnp.int32, (WIN_KEYS, LANES), 0)
    wbias = jnp.where((kpos_w <= t_lane) & (kpos_w > t_lane - WINDOW), 0.0, NEG)
    row_sl = lax.broadcasted_iota(jnp.int32, (SLC_CHUNK, LANES), 0)
    zero_q = jnp.zeros((Q_TILE, LANES), BF16)

    def gate_row(g, branch):
        rows = [(g * GQA_REP + r) * 3 + branch for r in range(GQA_REP)]
        return _lane_cat([gt_ref[slot, i:i + 1, :] for i in rows])

    qg = []
    for g in groups:
        in_g = (lane_sq >= HEAD_DIM) if g else (lane_sq < HEAD_DIM)
        qg.append(jnp.concatenate(
            [jnp.where(in_g, q_ref[q_rows, r * LANES:(r + 1) * LANES], zero_q) for r in range(GQA_REP)],
            axis=0))

    o_cmp = [None] * NSA_KV_HEADS
    sel_bias = [None] * NSA_KV_HEADS

    def compressed_softmax(g, sc):
        ec, _ = _softmax_numer(sc, cbias)
        coef = has_cmp * _safe_inv(jnp.sum(ec, axis=0, keepdims=True))
        pc = ec * coef
        psum = (pc[:, 0:LANES] + pc[:, LANES:2 * LANES]
                + pc[:, 2 * LANES:3 * LANES] + pc[:, 3 * LANES:4 * LANES])
        p_hi, p_lo = _split_bf16(psum)
        imp = (_dot(ovl_ref[...], p_hi) + _dot(ovl_ref[...], p_lo))[0:nsel, :]
        valid = j_idx <= cur
        forced = (j_idx == 0) | (j_idx == cur) | (j_idx == cur - 1)
        score = jnp.where(valid, imp + jnp.where(forced, FORCE, 0.0), -FORCE)
        rank = jnp.zeros((nsel, LANES), F32)
        for jp in range(nsel):
            row = score[jp:jp + 1, :]
            beats = (row > score) | ((row == score) & (j_idx > jp))
            rank = rank + jnp.where(beats, 1.0, 0.0)
        sel_bias[g] = jnp.where((rank < float(SLC_TOPN)) & (score > -1.0), 0.0, NEG)
        return ec.astype(BF16), coef

    nsub = SLC_CHUNK // LANES
    blocks_per_chunk = SLC_CHUNK // SLC_BLOCK
    kwin = kw_ref[pl.ds(win_start, WIN_KEYS), :]
    units = [("cmp", g, 0) for g in groups] + [("win", g, 0) for g in groups]
    units += [("slc", g, kc) for kc in range(n_chunks) for g in groups]
    m_run = [None] * NSA_KV_HEADS
    acc_s = [None] * NSA_KV_HEADS
    acc_w = [None] * NSA_KV_HEADS

    def scores(unit):
        kind, g, kc = unit
        if kind == "cmp":
            keys = kc_ref[...]
        elif kind == "win":
            keys = kwin
        else:
            keys = ks_ref[kc * SLC_CHUNK:(kc + 1) * SLC_CHUNK, :]
        return _dot_tb(keys, qg[g])

    def softmax(unit, s):
        kind, g, kc = unit
        if kind == "cmp":
            return compressed_softmax(g, s)
        if kind == "win":
            e, _ = _softmax_numer(s, wbias)
            return e.astype(BF16), None
        j0 = kc * blocks_per_chunk
        bias = jnp.concatenate(
            [jnp.broadcast_to(sel_bias[g][j:j + 1, :], (SLC_BLOCK, LANES))
             for j in range(j0, j0 + blocks_per_chunk)], axis=0)
        if kc == n_chunks - 1:
            bias = jnp.where(kc * SLC_CHUNK + row_sl <= t_lane, bias, NEG)
        e, m_new = _softmax_numer(s, bias, m_run[g])
        alpha = None if kc == 0 else jnp.exp2(m_run[g] - m_new)
        m_run[g] = m_new
        return e.astype(BF16), alpha

    def weighted_values(unit, e, alpha):
        kind, g, kc = unit
        if kind == "cmp":
            o_cmp[g] = _dot(vc_refs[g][0], e) * (gate_row(g, 0) * alpha)
            return
        if kind == "win":
            vt = _lane_cat([vw_refs[g][win_blk + i] for i in range(WIN_KEYS // LANES)])
            acc_w[g] = _dot(vt, e)
            return
        vt = _lane_cat([vs_refs[g][nsub * kc + i] for i in range(nsub)])
        pv = _dot(vt, e)
        acc_s[g] = pv if alpha is None else acc_s[g] * alpha + pv

    def finish():
        o_groups = []
        for g in groups:
            r0 = sum_row[g]
            o_groups.append(o_cmp[g]
                            + acc_s[g] * (gate_row(g, 1) * _safe_inv(acc_s[g][r0:r0 + 1, :]))
                            + acc_w[g] * (gate_row(g, 2) * _safe_inv(acc_w[g][r0:r0 + 1, :])))
        row_d = lax.broadcasted_iota(jnp.int32, (LANES, width), 0)
        o_t = jnp.where(row_d < HEAD_DIM, o_groups[0], o_groups[1])
        sq = jnp.sum(o_t * o_t, axis=0, keepdims=True)
        ss = (sq[:, 0:LANES] + sq[:, LANES:2 * LANES]
              + sq[:, 2 * LANES:3 * LANES] + sq[:, 3 * LANES:4 * LANES])
        rs = lax.rsqrt(ss * (1.0 / NSA_WIDTH) + EPS)
        for r in range(GQA_REP):
            cols = slice(r * LANES, (r + 1) * LANES)
            out_ref[q_rows, cols] = (o_t[:, cols] * rs * gn_ref[:, cols]).T.astype(BF16)

    return types.SimpleNamespace(units=units, scores=scores, softmax=softmax,
                                 weighted_values=weighted_values, finish=finish)


def _attn_call(q, ks, kw, vs0, vs1, vw0, vw1, kcmp, vc0, vc1, gt, ovl, gn, batch, seq):
    assert SLC_CHUNK % (TILES_PER_STEP * Q_TILE) == 0
    rows = TILES_PER_STEP * Q_TILE
    nq = seq // rows
    nkb = seq // LANES
    qrow = lambda c, b: (b * nq + c, 0)
    per_b = lambda c, b: (b, 0)
    per_b3 = lambda c, b: (b, 0, 0)
    vspec = pl.BlockSpec((nkb, LANES, LANES), per_b3)
    cspec = pl.BlockSpec((1, LANES, LANES), per_b3)
    return pl.pallas_call(
        _attn_body,
        grid=(nq, batch),
        in_specs=[
            pl.BlockSpec((rows, NSA_WIDTH), qrow),
            pl.BlockSpec((seq, LANES), per_b),
            pl.BlockSpec((seq, LANES), per_b),
            vspec, vspec, vspec, vspec,
            pl.BlockSpec((LANES, LANES), per_b),
            cspec, cspec,
            pl.BlockSpec((TILES_PER_STEP, 32, LANES), lambda c, b: (b * nq + c, 0, 0)),
            pl.BlockSpec((LANES, LANES), lambda c, b: (0, 0)),
            pl.BlockSpec((LANES, NSA_WIDTH), lambda c, b: (0, 0)),
        ],
        out_specs=pl.BlockSpec((rows, NSA_WIDTH), qrow),
        out_shape=jax.ShapeDtypeStruct((batch * seq, NSA_WIDTH), BF16),
        compiler_params=pltpu.CompilerParams(
            dimension_semantics=("parallel", "arbitrary"), vmem_limit_bytes=VMEM_LIMIT),
        name="attn",
    )(q, ks, kw, vs0, vs1, vw0, vw1, kcmp, vc0, vc1, gt, ovl, gn)


def _conv_body(tiles_per_seq, cur_ref, halo_ref, w_ref, b_ref, lng_ref, lnb_ref, gn_ref, out_ref,
               buf_ref, sh_ref, cv_ref):
    first = (pl.program_id(0) % tiles_per_seq) == 0
    buf_ref[0:CONV_HALO, :] = jnp.where(first, 0.0, halo_ref[...])
    buf_ref[CONV_HALO:, :] = cur_ref[...]
    lead = CONV_HALO - (CONV_WIDTH - 1)
    ncb = CONV_CH // LANES
    nrb = CONV_TILE // CONV_ROWS
    for s in range(8):
        span = CONV_TILE + 8 * ((CONV_WIDTH - 1 - s) // 8)
        for cb in range(ncb):
            sh_ref[s, cb, 0:span, :] = buf_ref[lead + s:lead + s + span, cb * LANES:(cb + 1) * LANES]

    def conv_block(i, carry):
        cb = i % ncb
        base = pl.multiple_of((i // ncb) * CONV_ROWS, CONV_ROWS)
        acc = jnp.zeros((CONV_ROWS, LANES), F32) + b_ref[cb]
        for k in range(CONV_WIDTH):
            acc = acc + w_ref[cb, k:k + 1, :] * sh_ref[k % 8, cb, pl.ds(base + 8 * (k // 8), CONV_ROWS), :]
        cv_ref[cb, pl.ds(base, CONV_ROWS), :] = acc
        return carry

    lax.fori_loop(0, nrb * ncb, conv_block, 0)
    h = _lane_cat([cv_ref[cb] for cb in range(ncb)])
    mu = jnp.mean(h, axis=-1, keepdims=True)
    d = h - mu
    var = jnp.mean(d * d, axis=-1, keepdims=True)
    hn = d * lax.rsqrt(var + EPS) * lng_ref[...] + lnb_ref[...]
    o = hn * _sigmoid(hn)
    ms = jnp.mean(o * o, axis=-1, keepdims=True)
    out_ref[...] = (o * lax.rsqrt(ms + EPS) * gn_ref[...]).astype(BF16)


def _conv_call(hcv, w, b, lng, lnb, gn, seq):
    t = hcv.shape[0]
    nt = t // CONV_TILE
    halo_per_tile = CONV_TILE // CONV_HALO
    ncb = CONV_CH // LANES
    row = lambda i: (i, 0)
    const = lambda i: (0, 0)
    return pl.pallas_call(
        functools.partial(_conv_body, seq // CONV_TILE),
        grid=(nt,),
        in_specs=[
            pl.BlockSpec((CONV_TILE, CONV_CH), row),
            pl.BlockSpec((CONV_HALO, CONV_CH), lambda i: (jnp.maximum(i * halo_per_tile - 1, 0), 0)),
            pl.BlockSpec((ncb, CONV_HALO, LANES), lambda i: (0, 0, 0)),
            pl.BlockSpec((ncb, 1, LANES), lambda i: (0, 0, 0)),
            pl.BlockSpec((1, CONV_CH), const),
            pl.BlockSpec((1, CONV_CH), const),
            pl.BlockSpec((1, CONV_CH), const),
        ],
        out_specs=pl.BlockSpec((CONV_TILE, CONV_CH), row),
        out_shape=jax.ShapeDtypeStruct((t, CONV_CH), BF16),
        scratch_shapes=[pltpu.VMEM((CONV_HALO + CONV_TILE, CONV_CH), F32),
                        pltpu.VMEM((8, ncb, CONV_TILE + 8 * ((CONV_WIDTH - 1) // 8), LANES), F32),
                        pltpu.VMEM((ncb, CONV_TILE, LANES), F32)],
        compiler_params=pltpu.CompilerParams(
            dimension_semantics=("parallel",), vmem_limit_bytes=VMEM_LIMIT),
        name="conv",
    )(hcv, hcv, w, b, lng, lnb, gn)


def _ffn_body(x_ref, ma_ref, mb_ref, woa_ref, wob_ref, gf_ref, wg_ref, wu_ref, wd_ref, out_ref):
    h = x_ref[...] + _dot(ma_ref[...], woa_ref[...]) + _dot(mb_ref[...], wob_ref[...])
    ms = jnp.mean(h * h, axis=-1, keepdims=True)
    hn = (h * lax.rsqrt(ms + EPS) * gf_ref[...]).astype(BF16)
    acc = jnp.zeros_like(h)
    for j in range(FFN_HIDDEN // FFN_CHUNK):
        cols = slice(j * FFN_CHUNK, (j + 1) * FFN_CHUNK)
        gte = _dot(hn, wg_ref[:, cols])
        up = _dot(hn, wu_ref[:, cols])
        act = (gte * _sigmoid(gte) * up).astype(BF16)
        acc = acc + _dot(act, wd_ref[cols, :])
    out_ref[...] = h + acc


def _ffn_call(x2, ma, mb, woa, wob, gf, wg, wu, wd):
    t = x2.shape[0]
    nt = t // FFN_TILE
    row = lambda i: (i, 0)
    const = lambda i: (0, 0)
    once = pl.Buffered(1)
    return pl.pallas_call(
        _ffn_body,
        grid=(nt,),
        in_specs=[
            pl.BlockSpec((FFN_TILE, D_MODEL), row),
            pl.BlockSpec((FFN_TILE, NSA_WIDTH), row),
            pl.BlockSpec((FFN_TILE, CONV_CH), row),
            pl.BlockSpec((NSA_WIDTH, D_MODEL), const, pipeline_mode=once),
            pl.BlockSpec((CONV_CH, D_MODEL), const, pipeline_mode=once),
            pl.BlockSpec((1, D_MODEL), const),
            pl.BlockSpec((D_MODEL, FFN_HIDDEN), const, pipeline_mode=once),
            pl.BlockSpec((D_MODEL, FFN_HIDDEN), const, pipeline_mode=once),
            pl.BlockSpec((FFN_HIDDEN, D_MODEL), const, pipeline_mode=once),
        ],
        out_specs=pl.BlockSpec((FFN_TILE, D_MODEL), row),
        out_shape=jax.ShapeDtypeStruct((t, D_MODEL), F32),
        compiler_params=pltpu.CompilerParams(
            dimension_semantics=("parallel",), vmem_limit_bytes=VMEM_LIMIT),
        name="ffn",
    )(x2, ma, mb, woa, wob, gf, wg, wu, wd)


def _pair_heads(a, axis):
    shape = a.shape
    a = a.reshape(shape[:axis] + (NSA_KV_HEADS, GQA_REP, HEAD_DIM) + shape[axis + 1:])
    return jnp.swapaxes(a, axis, axis + 1).reshape(shape)


def _transposed_gain(g):
    cols = g.reshape(GQA_REP, LANES).T
    return jnp.broadcast_to(cols[:, :, None], (LANES, GQA_REP, Q_TILE)).reshape(LANES, GQA_REP * Q_TILE)


def _rope_tables(pos):
    half = HEAD_DIM // 2
    inv = ROPE_THETA ** (-np.arange(half, dtype=np.float64) / half)
    ang = np.asarray(pos, np.float64)[:, None] * inv[None, :]
    cos = np.concatenate([np.cos(ang), np.cos(ang)], axis=1)
    sin = np.concatenate([-np.sin(ang), np.sin(ang)], axis=1)
    return (np.tile(cos, (1, NSA_KV_HEADS)).astype(np.float32),
            np.tile(sin, (1, NSA_KV_HEADS)).astype(np.float32))


def _block_diag_ones(width):
    idx = np.arange(width) // HEAD_DIM
    return (idx[:, None] == idx[None, :]).astype(np.float32)


def _overlap_t(seq):
    ncmp = (seq - CMP_LEN) // CMP_STRIDE + 1
    cs = np.arange(ncmp)[:, None] * CMP_STRIDE
    ss = np.arange(seq // SLC_BLOCK)[None, :] * SLC_BLOCK
    ov = np.clip(np.minimum(cs + CMP_LEN, ss + SLC_BLOCK) - np.maximum(cs, ss), 0, None) / CMP_LEN
    out = np.zeros((LANES, LANES), np.float32)
    out[:seq // SLC_BLOCK, :ncmp] = ov.T
    return out


def _cmp_weights(pe, w1, w2):
    pe2 = jnp.concatenate([pe, pe], axis=1)
    w1r = w1.reshape(CMP_LEN, HEAD_DIM, CMP_HIDDEN)
    z1 = jnp.zeros_like(w1r)
    w1b = jnp.concatenate([jnp.concatenate([w1r, z1], axis=2),
                           jnp.concatenate([z1, w1r], axis=2)], axis=1).astype(BF16)
    z2 = jnp.zeros_like(w2)
    w2b = jnp.concatenate([jnp.concatenate([w2, z2], axis=1),
                           jnp.concatenate([z2, w2], axis=1)], axis=0).astype(BF16)
    return pe2, w1b, w2b


def kernel(x, attn_norm_g, w_in, q_norm_g, k_norm_cmp_g, k_norm_slc_g, k_norm_win_g, cmp_pe_k, cmp_w1_k, cmp_w2_k, cmp_pe_v, cmp_w1_v, cmp_w2_v, conv_dw_w, conv_dw_b, conv_ln_g, conv_ln_b, out_norm_nsa_g, out_norm_conv_g, w_out, ffn_norm_g, w_gate_up, w_down):
    batch, seq, d_model = x.shape
    assert d_model == D_MODEL and seq % PROJ_TILE == 0 and seq // SLC_BLOCK == 32
    depth = w_in.shape[0]
    cos_np, sin_np = _rope_tables(np.arange(seq))
    ccos_np, csin_np = _rope_tables(np.arange(seq // CMP_STRIDE) * CMP_STRIDE + CMP_LEN - 1)
    cos, sin, ccos, csin = map(jnp.asarray, (cos_np, sin_np, ccos_np, csin_np))
    bd256 = jnp.asarray(_block_diag_ones(256), BF16)
    bd128 = jnp.asarray(_block_diag_ones(LANES), BF16)
    ovl = jnp.asarray(_overlap_t(seq), BF16)

    x2 = x.reshape(batch * seq, d_model)
    for l in range(depth):
        w = w_in[l]
        o_q, o_kc, o_vc, o_ks, o_vs, o_kw, o_vw, o_gl, o_cv = (
            0, 512, 640, 768, 896, 1024, 1152, 1280, 1304)
        gl_pad = jnp.zeros((d_model, LANES - 3 * NSA_HEADS), BF16)
        w_q = _pair_heads(w[:, o_q:o_q + NSA_WIDTH], 1)
        w_cat = jnp.concatenate([piece.astype(BF16) for piece in (
            w_q,
            w[:, o_ks:o_ks + KV_WIDTH], w[:, o_kw:o_kw + KV_WIDTH],
            w[:, o_vs:o_vs + KV_WIDTH], w[:, o_vw:o_vw + KV_WIDTH],
            w[:, o_kc:o_kc + KV_WIDTH], w[:, o_vc:o_vc + KV_WIDTH],
            w[:, o_cv:o_cv + CONV_CH], w[:, o_cv + CONV_CH:o_cv + 2 * CONV_CH],
            w[:, o_gl:o_gl + 3 * NSA_HEADS], gl_pad)], axis=1)
        gq = jnp.tile(q_norm_g[l], 4)[None, :]
        gk = jnp.concatenate([jnp.tile(k_norm_slc_g[l], 2), jnp.tile(k_norm_win_g[l], 2)])[None, :]
        q, ks, kw, vs0, vs1, vw0, vw1, kc, vc, gt, hcv = _proj_call(
            x2, attn_norm_g[l][None, :], w_cat, cos, sin, gq, gk, bd256, seq)

        pek, w1k, w2k = _cmp_weights(cmp_pe_k[l], cmp_w1_k[l], cmp_w2_k[l])
        pev, w1v, w2v = _cmp_weights(cmp_pe_v[l], cmp_w1_v[l], cmp_w2_v[l])
        kcmp, vc0, vc1 = _cmp_call(kc, vc, pek, w1k, w2k, pev, w1v, w2v, ccos, csin,
                                jnp.tile(k_norm_cmp_g[l], 2)[None, :], bd128, batch, seq)

        mix_a = _attn_call(q, ks, kw, vs0, vs1, vw0, vw1, kcmp, vc0, vc1, gt, ovl,
                           _transposed_gain(_pair_heads(out_norm_nsa_g[l], 0)), batch, seq)

        w_dw = jnp.concatenate([conv_dw_w[l][:, 0, :],
                                jnp.zeros((CONV_HALO - CONV_WIDTH, CONV_CH), F32)], axis=0)
        w_dw = w_dw.reshape(CONV_HALO, CONV_CH // LANES, LANES).transpose(1, 0, 2)
        mix_b = _conv_call(hcv, w_dw, conv_dw_b[l].reshape(CONV_CH // LANES, 1, LANES),
                           conv_ln_g[l][None, :],
                           conv_ln_b[l][None, :], out_norm_conv_g[l][None, :], seq)

        wo = w_out[l]
        x2 = _ffn_call(x2, mix_a, mix_b, _pair_heads(wo[:NSA_WIDTH], 0).astype(BF16),
                       wo[NSA_WIDTH:].astype(BF16), ffn_norm_g[l][None, :],
                       w_gate_up[l][:, :FFN_HIDDEN].astype(BF16),
                       w_gate_up[l][:, FFN_HIDDEN:].astype(BF16), w_down[l].astype(BF16))
    return x2.reshape(batch, seq, d_model)
```

```python
import functools
import types

import numpy as np
import jax
import jax.numpy as jnp
from jax import lax
from jax.experimental import pallas as pl
from jax.experimental.pallas import tpu as pltpu

F32 = jnp.float32
BF16 = jnp.bfloat16

D_MODEL = 1024
HEAD_DIM = 64
NSA_HEADS = 8
NSA_KV_HEADS = 2
GQA_REP = NSA_HEADS // NSA_KV_HEADS
NSA_WIDTH = NSA_HEADS * HEAD_DIM
KV_WIDTH = NSA_KV_HEADS * HEAD_DIM
CONV_CH = D_MODEL - NSA_WIDTH
CMP_LEN = 32
CMP_STRIDE = 16
CMP_HIDDEN = 4 * HEAD_DIM
SLC_BLOCK = 64
SLC_TOPN = 8
WINDOW = 512
CONV_WIDTH = 31
FFN_HIDDEN = 2816
ROPE_THETA = 10000.0
EPS = 1e-6
NEG = -1e30
LOG2E = 1.4426950408889634
FORCE = 1e6

LANES = 128
Q_TILE = 128
TILES_PER_STEP = 1
EPILOGUE_LAG = 1
SLC_CHUNK = 256
WIN_KEYS = WINDOW + Q_TILE
PROJ_TILE = 512
CONV_TILE = 512
CONV_HALO = 32
CONV_ROWS = 128
FFN_TILE = 512
FFN_CHUNK = 256
VMEM_LIMIT = 56 * 1024 * 1024

C_Q = 0
C_CMP = 512
C_SLC = 768
C_WIN = 1024
MAIN_COLS = 1280
GATE_COLS = 3 * NSA_HEADS

_TRANS_B = (((1,), (1,)), ((), ()))


def _dot(a, b):
    return jnp.dot(a, b, preferred_element_type=F32)


def _dot_tb(a, b):
    return lax.dot_general(a, b, _TRANS_B, preferred_element_type=F32)


def _split_bf16(x):
    hi = x.astype(BF16)
    lo = (x - hi.astype(F32)).astype(BF16)
    return hi, lo


def _sigmoid(x):
    return 1.0 / (1.0 + jnp.exp(-x))


def _head_norm_rope(y, gain, bd, cos, sin, first_half, scale):
    hi, lo = _split_bf16(y * y)
    ss = _dot(hi, bd) + _dot(lo, bd)
    yn = y * lax.rsqrt(ss * (1.0 / HEAD_DIM) + EPS) * gain
    outs = []
    for j in range(y.shape[1] // LANES):
        blk = yn[:, j * LANES:(j + 1) * LANES]
        rot = jnp.where(first_half, pltpu.roll(blk, LANES - HEAD_DIM // 2, 1),
                        pltpu.roll(blk, HEAD_DIM // 2, 1))
        outs.append((blk * cos + rot * sin) * scale)
    return outs


def _value_t_with_ones(v):
    vt = v.T
    row = lax.broadcasted_iota(jnp.int32, vt.shape, 0)
    return (jnp.where(row < HEAD_DIM, vt, 1.0).astype(BF16),
            jnp.where(row < HEAD_DIM, 1.0, vt).astype(BF16))


def _proj_body(x_ref, gin_ref, wm_ref, wc_ref, wg_ref, cos_ref, sin_ref, gq_ref, gk_ref, bd_ref,
               q_ref, ks_ref, kw_ref, vs0_ref, vs1_ref, vw0_ref, vw1_ref, kc_ref, vc_ref, gt_ref,
               hcv_ref):
    x = x_ref[...]
    xg = (x * gin_ref[...]).astype(BF16)
    rs = lax.rsqrt(jnp.mean(x * x, axis=-1, keepdims=True) + EPS)
    cos = cos_ref[...]
    sin = sin_ref[...]
    bd = bd_ref[...]
    lane = lax.broadcasted_iota(jnp.int32, (PROJ_TILE, LANES), 1)
    first_half = (lane & (HEAD_DIM // 2)) == 0
    held = {}

    def queries(group, y):
        held["q", group] = _head_norm_rope(y, gq_ref[...], bd, cos, sin, first_half,
                                           HEAD_DIM ** -0.5 * LOG2E)
        if group == 0:
            return
        low = lane < HEAD_DIM
        for j in range(2):
            a, b = held["q", 0][j], held["q", 1][j]
            q_ref[:, (2 * j) * LANES:(2 * j + 1) * LANES] = jnp.where(
                low, a, pltpu.roll(b, HEAD_DIM, 1)).astype(BF16)
            q_ref[:, (2 * j + 1) * LANES:(2 * j + 2) * LANES] = jnp.where(
                low, pltpu.roll(a, HEAD_DIM, 1), b).astype(BF16)

    def values(y, out0_ref, out1_ref):
        for i in range(PROJ_TILE // LANES):
            out0_ref[i], out1_ref[i] = _value_t_with_ones(y[i * LANES:(i + 1) * LANES, :])

    def slc_kv(y):
        held["ks"] = y[:, :LANES]
        values(y[:, LANES:], vs0_ref, vs1_ref)

    def win_kv(y):
        kblks = _head_norm_rope(_lane_cat([held["ks"], y[:, :LANES]]), gk_ref[...], bd, cos, sin,
                                first_half, 1.0)
        ks_ref[...] = kblks[0].astype(BF16)
        kw_ref[...] = kblks[1].astype(BF16)
        values(y[:, LANES:], vw0_ref, vw1_ref)

    def cmp_inputs(y):
        kc_ref[...] = y[:, :LANES]
        vc_ref[...] = y[:, LANES:]

    def conv_lin(y):
        held["ca"] = y

    def conv_glu(y):
        hcv_ref[...] = held["ca"] * _sigmoid(y)

    def gates(y):
        sg = _sigmoid(y)
        for i in range(PROJ_TILE // LANES):
            gt_ref[i] = sg[i * LANES:(i + 1) * LANES, :].T[0:32, :]

    sections = [
        (wm_ref, C_Q, 256, functools.partial(queries, 0)),
        (wm_ref, C_Q + 256, 256, functools.partial(queries, 1)),
        (wc_ref, 0, CONV_CH, conv_lin),
        (wm_ref, C_SLC, 256, slc_kv),
        (wc_ref, CONV_CH, CONV_CH, conv_glu),
        (wm_ref, C_WIN, 256, win_kv),
        (wm_ref, C_CMP, 256, cmp_inputs),
        (wg_ref, 0, LANES, gates),
    ]

    def project(sec):
        w_ref, c0, width, _ = sec
        return _dot(xg, w_ref[:, c0:c0 + width])

    y_next = project(sections[0])
    for i, sec in enumerate(sections):
        y_cur = y_next
        if i + 1 < len(sections):
            y_next = project(sections[i + 1])
        sec[3](y_cur * rs)


def _proj_call(x2, gin, w_main, w_conv, w_gate, cos, sin, gq, gk, bd, seq):
    t = x2.shape[0]
    nt = t // PROJ_TILE
    tiles_per_seq = seq // PROJ_TILE
    row = lambda i: (i, 0)
    const = lambda i: (0, 0)
    tab = lambda i: (i % tiles_per_seq, 0)
    blk3 = lambda i: (i, 0, 0)
    nb = PROJ_TILE // LANES
    return pl.pallas_call(
        _proj_body,
        grid=(nt,),
        in_specs=[
            pl.BlockSpec((PROJ_TILE, D_MODEL), row),
            pl.BlockSpec((1, D_MODEL), const),
            pl.BlockSpec((D_MODEL, MAIN_COLS), const),
            pl.BlockSpec((D_MODEL, 2 * CONV_CH), const),
            pl.BlockSpec((D_MODEL, LANES), const),
            pl.BlockSpec((PROJ_TILE, LANES), tab),
            pl.BlockSpec((PROJ_TILE, LANES), tab),
            pl.BlockSpec((1, 256), const),
            pl.BlockSpec((1, 256), const),
            pl.BlockSpec((256, 256), const),
        ],
        out_specs=[
            pl.BlockSpec((PROJ_TILE, NSA_WIDTH), row),
            pl.BlockSpec((PROJ_TILE, LANES), row),
            pl.BlockSpec((PROJ_TILE, LANES), row),
            pl.BlockSpec((nb, LANES, LANES), blk3),
            pl.BlockSpec((nb, LANES, LANES), blk3),
            pl.BlockSpec((nb, LANES, LANES), blk3),
            pl.BlockSpec((nb, LANES, LANES), blk3),
            pl.BlockSpec((PROJ_TILE, LANES), row),
            pl.BlockSpec((PROJ_TILE, LANES), row),
            pl.BlockSpec((nb, 32, LANES), blk3),
            pl.BlockSpec((PROJ_TILE, CONV_CH), row),
        ],
        out_shape=[
            jax.ShapeDtypeStruct((t, NSA_WIDTH), BF16),
            jax.ShapeDtypeStruct((t, LANES), BF16),
            jax.ShapeDtypeStruct((t, LANES), BF16),
            jax.ShapeDtypeStruct((t // LANES, LANES, LANES), BF16),
            jax.ShapeDtypeStruct((t // LANES, LANES, LANES), BF16),
            jax.ShapeDtypeStruct((t // LANES, LANES, LANES), BF16),
            jax.ShapeDtypeStruct((t // LANES, LANES, LANES), BF16),
            jax.ShapeDtypeStruct((t, LANES), F32),
            jax.ShapeDtypeStruct((t, LANES), F32),
            jax.ShapeDtypeStruct((t // LANES, 32, LANES), F32),
            jax.ShapeDtypeStruct((t, CONV_CH), F32),
        ],
        compiler_params=pltpu.CompilerParams(
            dimension_semantics=("parallel",), vmem_limit_bytes=VMEM_LIMIT),
        name="proj",
    )(x2, gin, w_main, w_conv, w_gate, cos, sin, gq, gk, bd)


def _cmp_mlp(t_ref, pe_ref, w1_ref, w2_ref):
    half = CMP_LEN // 2
    nchunk = t_ref.shape[0] // CMP_STRIDE
    p = jnp.zeros((nchunk, 2 * CMP_HIDDEN), F32)
    q = jnp.zeros((nchunk, 2 * CMP_HIDDEN), F32)
    for l in range(half):
        xl = t_ref[pl.ds(l, nchunk, stride=CMP_STRIDE), :]
        p = p + _dot((xl + pe_ref[l:l + 1, :]).astype(BF16), w1_ref[l])
        q = q + _dot((xl + pe_ref[half + l:half + l + 1, :]).astype(BF16), w1_ref[half + l])
    h1 = p + pltpu.roll(q, nchunk - 1, 0)
    act = (h1 * _sigmoid(h1)).astype(BF16)
    return _dot(act, w2_ref[...])


def _cmp_body(kc_ref, vc_ref, pek_ref, w1k_ref, w2k_ref, pev_ref, w1v_ref, w2v_ref,
              cos_ref, sin_ref, gk_ref, bd_ref, ko_ref, vt0_ref, vt1_ref):
    kraw = _cmp_mlp(kc_ref, pek_ref, w1k_ref, w2k_ref)
    lane = lax.broadcasted_iota(jnp.int32, kraw.shape, 1)
    first_half = (lane & (HEAD_DIM // 2)) == 0
    (kr,) = _head_norm_rope(kraw, gk_ref[...], bd_ref[...], cos_ref[...], sin_ref[...],
                            first_half, 1.0)
    ko_ref[...] = kr.astype(BF16)
    vraw = _cmp_mlp(vc_ref, pev_ref, w1v_ref, w2v_ref)
    vt0_ref[0], vt1_ref[0] = _value_t_with_ones(vraw)


def _cmp_call(kc, vc, pek, w1k, w2k, pev, w1v, w2v, cos, sin, gk, bd, batch, seq):
    seqblk = lambda b: (b, 0)
    c2 = lambda b: (0, 0)
    c3 = lambda b: (0, 0, 0)
    ncmp_pad = seq // CMP_STRIDE
    return pl.pallas_call(
        _cmp_body,
        grid=(batch,),
        in_specs=[
            pl.BlockSpec((seq, LANES), seqblk),
            pl.BlockSpec((seq, LANES), seqblk),
            pl.BlockSpec((CMP_LEN, LANES), c2),
            pl.BlockSpec((CMP_LEN, LANES, 2 * CMP_HIDDEN), c3),
            pl.BlockSpec((2 * CMP_HIDDEN, LANES), c2),
            pl.BlockSpec((CMP_LEN, LANES), c2),
            pl.BlockSpec((CMP_LEN, LANES, 2 * CMP_HIDDEN), c3),
            pl.BlockSpec((2 * CMP_HIDDEN, LANES), c2),
            pl.BlockSpec((ncmp_pad, LANES), c2),
            pl.BlockSpec((ncmp_pad, LANES), c2),
            pl.BlockSpec((1, LANES), c2),
            pl.BlockSpec((LANES, LANES), c2),
        ],
        out_specs=[
            pl.BlockSpec((ncmp_pad, LANES), seqblk),
            pl.BlockSpec((1, LANES, ncmp_pad), lambda b: (b, 0, 0)),
            pl.BlockSpec((1, LANES, ncmp_pad), lambda b: (b, 0, 0)),
        ],
        out_shape=[
            jax.ShapeDtypeStruct((batch * ncmp_pad, LANES), BF16),
            jax.ShapeDtypeStruct((batch, LANES, ncmp_pad), BF16),
            jax.ShapeDtypeStruct((batch, LANES, ncmp_pad), BF16),
        ],
        compiler_params=pltpu.CompilerParams(
            dimension_semantics=("parallel",), vmem_limit_bytes=VMEM_LIMIT),
        name="compress",
    )(kc, vc, pek, w1k, w2k, pev, w1v, w2v, cos, sin, gk, bd)


def _lane_cat(blocks):
    return jnp.concatenate(blocks, axis=1)


def _softmax_numer(s, bias, m_run=None):
    es, ms = [], []
    for r in range(GQA_REP):
        sm = s[:, r * LANES:(r + 1) * LANES] + bias
        m = jnp.max(sm, axis=0, keepdims=True)
        if m_run is not None:
            m = jnp.maximum(m, m_run[:, r * LANES:(r + 1) * LANES])
        es.append(jnp.exp2(sm - m))
        ms.append(m)
    return _lane_cat(es), _lane_cat(ms)


def _safe_inv(l):
    return 1.0 / jnp.where(l > 0.0, l, 1.0)


def _attn_body(*refs):
    first = pl.program_id(0) * TILES_PER_STEP
    n_max = refs[1].shape[0] // SLC_CHUNK
    for n in range(1, n_max + 1):
        pl.when((first * Q_TILE) // SLC_CHUNK + 1 == n)(functools.partial(_attn_step, n, first, *refs))


def _attn_step(n_chunks, first, *refs):
    tiles = [_tile_program(n_chunks, first + t, t, *refs) for t in range(TILES_PER_STEP)]
    items = [(tile, unit) for tile in tiles for unit in tile.units]
    due = {}
    pending = None
    s_next = items[0][0].scores(items[0][1])
    for i, (tile, unit) in enumerate(items):
        s_cur = s_next
        if i + 1 < len(items):
            s_next = items[i + 1][0].scores(items[i + 1][1])
        e, alpha = tile.softmax(unit, s_cur)
        if pending is not None:
            pending[0].weighted_values(*pending[1:])
        pending = (tile, unit, e, alpha)
        for done in due.pop(i, []):
            done.finish()
        if unit is tile.units[-1]:
            due.setdefault(i + 1 + EPILOGUE_LAG, []).append(tile)
    pending[0].weighted_values(*pending[1:])
    for i in sorted(due):
        for done in due[i]:
            done.finish()


def _tile_program(n_chunks, c, slot, q_ref, ks_ref, kw_ref, vs0_ref, vs1_ref, vw0_ref, vw1_ref, kc_ref,
                  vc0_ref, vc1_ref, gt_ref, ovl_ref, gn_ref, out_ref):
    groups = range(NSA_KV_HEADS)
    q_rows = slice(slot * Q_TILE, (slot + 1) * Q_TILE)
    vs_refs, vw_refs, vc_refs = (vs0_ref, vs1_ref), (vw0_ref, vw1_ref), (vc0_ref, vc1_ref)
    sum_row = (HEAD_DIM, 0)
    width = GQA_REP * LANES
    t_lane = c * Q_TILE + lax.broadcasted_iota(jnp.int32, (1, LANES), 1)
    lane_sq = lax.broadcasted_iota(jnp.int32, (Q_TILE, LANES), 1)
    nsel = 32
    j_idx = lax.broadcasted_iota(jnp.int32, (nsel, LANES), 0)
    cur = lax.shift_right_logical(t_lane, 6)
    n_idx = lax.broadcasted_iota(jnp.int32, (LANES, LANES), 0)
    cbias = jnp.where((n_idx * CMP_STRIDE + (CMP_LEN - 1) <= t_lane) & (n_idx < LANES - 1), 0.0, NEG)
    has_cmp = jnp.where(t_lane >= CMP_LEN - 1, 1.0, 0.0)
    has_cmp = _lane_cat([has_cmp] * GQA_REP)
    win_blk = jnp.maximum(c - WINDOW // Q_TILE, 0)
    win_start = pl.multiple_of(win_blk * Q_TILE, Q_TILE)
    kpos_w = win_start + lax.broadcasted_iota(jnp.int32, (WIN_KEYS, LANES), 0)
    wbias = jnp.where((kpos_w <= t_lane) & (kpos_w > t_lane - WINDOW), 0.0, NEG)
    row_sl = lax.broadcasted_iota(jnp.int32, (SLC_CHUNK, LANES), 0)
    zero_q = jnp.zeros((Q_TILE, LANES), BF16)

    def gate_row(g, branch):
        rows = [(g * GQA_REP + r) * 3 + branch for r in range(GQA_REP)]
        return _lane_cat([gt_ref[slot, i:i + 1, :] for i in rows])

    qg = []
    for g in groups:
        in_g = (lane_sq >= HEAD_DIM) if g else (lane_sq < HEAD_DIM)
        qg.append(jnp.concatenate(
            [jnp.where(in_g, q_ref[q_rows, r * LANES:(r + 1) * LANES], zero_q) for r in range(GQA_REP)],
            axis=0))

    o_cmp = [None] * NSA_KV_HEADS
    sel_bias = [None] * NSA_KV_HEADS

    def compressed_softmax(g, sc):
        ec, _ = _softmax_numer(sc, cbias)
        coef = has_cmp * _safe_inv(jnp.sum(ec, axis=0, keepdims=True))
        pc = ec * coef
        psum = (pc[:, 0:LANES] + pc[:, LANES:2 * LANES]
                + pc[:, 2 * LANES:3 * LANES] + pc[:, 3 * LANES:4 * LANES])
        p_hi, p_lo = _split_bf16(psum)
        imp = (_dot(ovl_ref[...], p_hi) + _dot(ovl_ref[...], p_lo))[0:nsel, :]
        valid = j_idx <= cur
        forced = (j_idx == 0) | (j_idx == cur) | (j_idx == cur - 1)
        score = jnp.where(valid, imp + jnp.where(forced, FORCE, 0.0), -FORCE)
        rank = jnp.zeros((nsel, LANES), F32)
        for jp in range(nsel):
            row = score[jp:jp + 1, :]
            beats = (row > score) | ((row == score) & (j_idx > jp))
            rank = rank + jnp.where(beats, 1.0, 0.0)
        sel_bias[g] = jnp.where((rank < float(SLC_TOPN)) & (score > -1.0), 0.0, NEG)
        return ec.astype(BF16), coef

    nsub = SLC_CHUNK // LANES
    blocks_per_chunk = SLC_CHUNK // SLC_BLOCK
    kwin = kw_ref[pl.ds(win_start, WIN_KEYS), :]
    units = [("cmp", g, 0) for g in groups] + [("win", g, 0) for g in groups]
    units += [("slc", g, kc) for kc in range(n_chunks) for g in groups]
    m_run = [None] * NSA_KV_HEADS
    acc_s = [None] * NSA_KV_HEADS
    acc_w = [None] * NSA_KV_HEADS

    def scores(unit):
        kind, g, kc = unit
        if kind == "cmp":
            keys = kc_ref[...]
        elif kind == "win":
            keys = kwin
        else:
            keys = ks_ref[kc * SLC_CHUNK:(kc + 1) * SLC_CHUNK, :]
        return _dot_tb(keys, qg[g])

    def softmax(unit, s):
        kind, g, kc = unit
        if kind == "cmp":
            return compressed_softmax(g, s)
        if kind == "win":
            e, _ = _softmax_numer(s, wbias)
            return e.astype(BF16), None
        j0 = kc * blocks_per_chunk
        bias = jnp.concatenate(
            [jnp.broadcast_to(sel_bias[g][j:j + 1, :], (SLC_BLOCK, LANES))
             for j in range(j0, j0 + blocks_per_chunk)], axis=0)
        if kc == n_chunks - 1:
            bias = jnp.where(kc * SLC_CHUNK + row_sl <= t_lane, bias, NEG)
        e, m_new = _softmax_numer(s, bias, m_run[g])
        alpha = None if kc == 0 else jnp.exp2(m_run[g] - m_new)
        m_run[g] = m_new
        return e.astype(BF16), alpha

    def weighted_values(unit, e, alpha):
        kind, g, kc = unit
        if kind == "cmp":
            o_cmp[g] = _dot(vc_refs[g][0], e) * (gate_row(g, 0) * alpha)
            return
        if kind == "win":
            vt = _lane_cat([vw_refs[g][win_blk + i] for i in range(WIN_KEYS // LANES)])
            acc_w[g] = _dot(vt, e)
            return
        vt = _lane_cat([vs_refs[g][nsub * kc + i] for i in range(nsub)])
        pv = _dot(vt, e)
        acc_s[g] = pv if alpha is None else acc_s[g] * alpha + pv

    def finish():
        o_groups = []
        for g in groups:
            r0 = sum_row[g]
            o_groups.append(o_cmp[g]
                            + acc_s[g] * (gate_row(g, 1) * _safe_inv(acc_s[g][r0:r0 + 1, :]))
                            + acc_w[g] * (gate_row(g, 2) * _safe_inv(acc_w[g][r0:r0 + 1, :])))
        row_d = lax.broadcasted_iota(jnp.int32, (LANES, width), 0)
        o_t = jnp.where(row_d < HEAD_DIM, o_groups[0], o_groups[1])
        sq = jnp.sum(o_t * o_t, axis=0, keepdims=True)
        ss = (sq[:, 0:LANES] + sq[:, LANES:2 * LANES]
              + sq[:, 2 * LANES:3 * LANES] + sq[:, 3 * LANES:4 * LANES])
        rs = lax.rsqrt(ss * (1.0 / NSA_WIDTH) + EPS)
        for r in range(GQA_REP):
            cols = slice(r * LANES, (r + 1) * LANES)
            out_ref[q_rows, cols] = (o_t[:, cols] * rs * gn_ref[:, cols]).T.astype(BF16)

    return types.SimpleNamespace(units=units, scores=scores, softmax=softmax,
                                 weighted_values=weighted_values, finish=finish)


def _attn_call(q, ks, kw, vs0, vs1, vw0, vw1, kcmp, vc0, vc1, gt, ovl, gn, batch, seq):
    assert SLC_CHUNK % (TILES_PER_STEP * Q_TILE) == 0
    rows = TILES_PER_STEP * Q_TILE
    nq = seq // rows
    nkb = seq // LANES
    qrow = lambda c, b: (b * nq + c, 0)
    per_b = lambda c, b: (b, 0)
    per_b3 = lambda c, b: (b, 0, 0)
    vspec = pl.BlockSpec((nkb, LANES, LANES), per_b3)
    cspec = pl.BlockSpec((1, LANES, LANES), per_b3)
    return pl.pallas_call(
        _attn_body,
        grid=(nq, batch),
        in_specs=[
            pl.BlockSpec((rows, NSA_WIDTH), qrow),
            pl.BlockSpec((seq, LANES), per_b),
            pl.BlockSpec((seq, LANES), per_b),
            vspec, vspec, vspec, vspec,
            pl.BlockSpec((LANES, LANES), per_b),
            cspec, cspec,
            pl.BlockSpec((TILES_PER_STEP, 32, LANES), lambda c, b: (b * nq + c, 0, 0)),
            pl.BlockSpec((LANES, LANES), lambda c, b: (0, 0)),
            pl.BlockSpec((LANES, NSA_WIDTH), lambda c, b: (0, 0)),
        ],
        out_specs=pl.BlockSpec((rows, NSA_WIDTH), qrow),
        out_shape=jax.ShapeDtypeStruct((batch * seq, NSA_WIDTH), BF16),
        compiler_params=pltpu.CompilerParams(
            dimension_semantics=("parallel", "arbitrary"), vmem_limit_bytes=VMEM_LIMIT),
        name="attn",
    )(q, ks, kw, vs0, vs1, vw0, vw1, kcmp, vc0, vc1, gt, ovl, gn)


def _conv_body(tiles_per_seq, cur_ref, halo_ref, w_ref, b_ref, lng_ref, lnb_ref, gn_ref, out_ref,
               buf_ref, sh_ref, cv_ref):
    first = (pl.program_id(0) % tiles_per_seq) == 0
    buf_ref[0:CONV_HALO, :] = jnp.where(first, 0.0, halo_ref[...])
    buf_ref[CONV_HALO:, :] = cur_ref[...]
    lead = CONV_HALO - (CONV_WIDTH - 1)
    ncb = CONV_CH // LANES
    nrb = CONV_TILE // CONV_ROWS
    for s in range(8):
        span = CONV_TILE + 8 * ((CONV_WIDTH - 1 - s) // 8)
        for cb in range(ncb):
            sh_ref[s, cb, 0:span, :] = buf_ref[lead + s:lead + s + span, cb * LANES:(cb + 1) * LANES]

    def conv_block(i, carry):
        cb = i % ncb
        base = pl.multiple_of((i // ncb) * CONV_ROWS, CONV_ROWS)
        acc = jnp.zeros((CONV_ROWS, LANES), F32) + b_ref[cb]
        for k in range(CONV_WIDTH):
            acc = acc + w_ref[cb, k:k + 1, :] * sh_ref[k % 8, cb, pl.ds(base + 8 * (k // 8), CONV_ROWS), :]
        cv_ref[cb, pl.ds(base, CONV_ROWS), :] = acc
        return carry

    lax.fori_loop(0, nrb * ncb, conv_block, 0)
    h = _lane_cat([cv_ref[cb] for cb in range(ncb)])
    mu = jnp.mean(h, axis=-1, keepdims=True)
    d = h - mu
    var = jnp.mean(d * d, axis=-1, keepdims=True)
    hn = d * lax.rsqrt(var + EPS) * lng_ref[...] + lnb_ref[...]
    o = hn * _sigmoid(hn)
    ms = jnp.mean(o * o, axis=-1, keepdims=True)
    out_ref[...] = (o * lax.rsqrt(ms + EPS) * gn_ref[...]).astype(BF16)


def _conv_call(hcv, w, b, lng, lnb, gn, seq):
    t = hcv.shape[0]
    nt = t // CONV_TILE
    halo_per_tile = CONV_TILE // CONV_HALO
    ncb = CONV_CH // LANES
    row = lambda i: (i, 0)
    const = lambda i: (0, 0)
    return pl.pallas_call(
        functools.partial(_conv_body, seq // CONV_TILE),
        grid=(nt,),
        in_specs=[
            pl.BlockSpec((CONV_TILE, CONV_CH), row),
            pl.BlockSpec((CONV_HALO, CONV_CH), lambda i: (jnp.maximum(i * halo_per_tile - 1, 0), 0)),
            pl.BlockSpec((ncb, CONV_HALO, LANES), lambda i: (0, 0, 0)),
            pl.BlockSpec((ncb, 1, LANES), lambda i: (0, 0, 0)),
            pl.BlockSpec((1, CONV_CH), const),
            pl.BlockSpec((1, CONV_CH), const),
            pl.BlockSpec((1, CONV_CH), const),
        ],
        out_specs=pl.BlockSpec((CONV_TILE, CONV_CH), row),
        out_shape=jax.ShapeDtypeStruct((t, CONV_CH), BF16),
        scratch_shapes=[pltpu.VMEM((CONV_HALO + CONV_TILE, CONV_CH), F32),
                        pltpu.VMEM((8, ncb, CONV_TILE + 8 * ((CONV_WIDTH - 1) // 8), LANES), F32),
                        pltpu.VMEM((ncb, CONV_TILE, LANES), F32)],
        compiler_params=pltpu.CompilerParams(
            dimension_semantics=("parallel",), vmem_limit_bytes=VMEM_LIMIT),
        name="conv",
    )(hcv, hcv, w, b, lng, lnb, gn)


def _ffn_body(x_ref, ma_ref, mb_ref, woa_ref, wob_ref, gf_ref, wgu_ref, wd_ref, out_ref):
    h = x_ref[...] + _dot(ma_ref[...], woa_ref[...]) + _dot(mb_ref[...], wob_ref[...])
    ms = jnp.mean(h * h, axis=-1, keepdims=True)
    hn = (h * lax.rsqrt(ms + EPS) * gf_ref[...]).astype(BF16)
    acc = jnp.zeros_like(h)
    for j in range(FFN_HIDDEN // FFN_CHUNK):
        c0 = j * FFN_CHUNK
        gte = _dot(hn, wgu_ref[:, c0:c0 + FFN_CHUNK])
        up = _dot(hn, wgu_ref[:, FFN_HIDDEN + c0:FFN_HIDDEN + c0 + FFN_CHUNK])
        act = (gte * _sigmoid(gte) * up).astype(BF16)
        acc = acc + _dot(act, wd_ref[c0:c0 + FFN_CHUNK, :])
    out_ref[...] = h + acc


def _ffn_call(x2, ma, mb, woa, wob, gf, wgu, wd):
    t = x2.shape[0]
    nt = t // FFN_TILE
    row = lambda i: (i, 0)
    const = lambda i: (0, 0)
    once = pl.Buffered(1)
    return pl.pallas_call(
        _ffn_body,
        grid=(nt,),
        in_specs=[
            pl.BlockSpec((FFN_TILE, D_MODEL), row),
            pl.BlockSpec((FFN_TILE, NSA_WIDTH), row),
            pl.BlockSpec((FFN_TILE, CONV_CH), row),
            pl.BlockSpec((NSA_WIDTH, D_MODEL), const, pipeline_mode=once),
            pl.BlockSpec((CONV_CH, D_MODEL), const, pipeline_mode=once),
            pl.BlockSpec((1, D_MODEL), const),
            pl.BlockSpec((D_MODEL, 2 * FFN_HIDDEN), const, pipeline_mode=once),
            pl.BlockSpec((FFN_HIDDEN, D_MODEL), const, pipeline_mode=once),
        ],
        out_specs=pl.BlockSpec((FFN_TILE, D_MODEL), row),
        out_shape=jax.ShapeDtypeStruct((t, D_MODEL), F32),
        compiler_params=pltpu.CompilerParams(
            dimension_semantics=("parallel",), vmem_limit_bytes=VMEM_LIMIT),
        name="ffn",
    )(x2, ma, mb, woa, wob, gf, wgu, wd)


def _pair_heads(a, axis):
    shape = a.shape
    a = a.reshape(shape[:axis] + (NSA_KV_HEADS, GQA_REP, HEAD_DIM) + shape[axis + 1:])
    return jnp.swapaxes(a, axis, axis + 1).reshape(shape)


def _transposed_gain(g):
    cols = g.reshape(GQA_REP, LANES).T
    return jnp.broadcast_to(cols[:, :, None], (LANES, GQA_REP, Q_TILE)).reshape(LANES, GQA_REP * Q_TILE)


def _rope_tables(pos):
    half = HEAD_DIM // 2
    inv = ROPE_THETA ** (-np.arange(half, dtype=np.float64) / half)
    ang = np.asarray(pos, np.float64)[:, None] * inv[None, :]
    cos = np.concatenate([np.cos(ang), np.cos(ang)], axis=1)
    sin = np.concatenate([-np.sin(ang), np.sin(ang)], axis=1)
    return (np.tile(cos, (1, NSA_KV_HEADS)).astype(np.float32),
            np.tile(sin, (1, NSA_KV_HEADS)).astype(np.float32))


def _block_diag_ones(width):
    idx = np.arange(width) // HEAD_DIM
    return (idx[:, None] == idx[None, :]).astype(np.float32)


def _overlap_t(seq):
    ncmp = (seq - CMP_LEN) // CMP_STRIDE + 1
    cs = np.arange(ncmp)[:, None] * CMP_STRIDE
    ss = np.arange(seq // SLC_BLOCK)[None, :] * SLC_BLOCK
    ov = np.clip(np.minimum(cs + CMP_LEN, ss + SLC_BLOCK) - np.maximum(cs, ss), 0, None) / CMP_LEN
    out = np.zeros((LANES, LANES), np.float32)
    out[:seq // SLC_BLOCK, :ncmp] = ov.T
    return out


def _cmp_weights(pe, w1, w2):
    pe2 = jnp.concatenate([pe, pe], axis=1)
    w1r = w1.reshape(CMP_LEN, HEAD_DIM, CMP_HIDDEN)
    z1 = jnp.zeros_like(w1r)
    w1b = jnp.concatenate([jnp.concatenate([w1r, z1], axis=2),
                           jnp.concatenate([z1, w1r], axis=2)], axis=1).astype(BF16)
    z2 = jnp.zeros_like(w2)
    w2b = jnp.concatenate([jnp.concatenate([w2, z2], axis=1),
                           jnp.concatenate([z2, w2], axis=1)], axis=0).astype(BF16)
    return pe2, w1b, w2b


def kernel(x, attn_norm_g, w_in, q_norm_g, k_norm_cmp_g, k_norm_slc_g, k_norm_win_g, cmp_pe_k, cmp_w1_k, cmp_w2_k, cmp_pe_v, cmp_w1_v, cmp_w2_v, conv_dw_w, conv_dw_b, conv_ln_g, conv_ln_b, out_norm_nsa_g, out_norm_conv_g, w_out, ffn_norm_g, w_gate_up, w_down):
    batch, seq, d_model = x.shape
    assert d_model == D_MODEL and seq % PROJ_TILE == 0 and seq // SLC_BLOCK == 32
    depth = w_in.shape[0]
    cos_np, sin_np = _rope_tables(np.arange(seq))
    ccos_np, csin_np = _rope_tables(np.arange(seq // CMP_STRIDE) * CMP_STRIDE + CMP_LEN - 1)
    cos, sin, ccos, csin = map(jnp.asarray, (cos_np, sin_np, ccos_np, csin_np))
    bd256 = jnp.asarray(_block_diag_ones(256), BF16)
    bd128 = jnp.asarray(_block_diag_ones(LANES), BF16)
    ovl = jnp.asarray(_overlap_t(seq), BF16)

    x2 = x.reshape(batch * seq, d_model)
    for l in range(depth):
        w = w_in[l]
        w_main = w[:, :MAIN_COLS].astype(BF16)
        w_gate = jnp.pad(w[:, MAIN_COLS:MAIN_COLS + GATE_COLS],
                         ((0, 0), (0, LANES - GATE_COLS))).astype(BF16)
        w_conv = w[:, MAIN_COLS + GATE_COLS:].astype(BF16)
        gq = jnp.tile(q_norm_g[l], 4)[None, :]
        gk = jnp.concatenate([jnp.tile(k_norm_slc_g[l], 2), jnp.tile(k_norm_win_g[l], 2)])[None, :]
        q, ks, kw, vs0, vs1, vw0, vw1, kc, vc, gt, hcv = _proj_call(
            x2, attn_norm_g[l][None, :], w_main, w_conv, w_gate, cos, sin, gq, gk, bd256, seq)

        pek, w1k, w2k = _cmp_weights(cmp_pe_k[l], cmp_w1_k[l], cmp_w2_k[l])
        pev, w1v, w2v = _cmp_weights(cmp_pe_v[l], cmp_w1_v[l], cmp_w2_v[l])
        kcmp, vc0, vc1 = _cmp_call(kc, vc, pek, w1k, w2k, pev, w1v, w2v, ccos, csin,
                                jnp.tile(k_norm_cmp_g[l], 2)[None, :], bd128, batch, seq)

        mix_a = _attn_call(q, ks, kw, vs0, vs1, vw0, vw1, kcmp, vc0, vc1, gt, ovl,
                           _transposed_gain(_pair_heads(out_norm_nsa_g[l], 0)), batch, seq)

        w_dw = jnp.concatenate([conv_dw_w[l][:, 0, :],
                                jnp.zeros((CONV_HALO - CONV_WIDTH, CONV_CH), F32)], axis=0)
        w_dw = w_dw.reshape(CONV_HALO, CONV_CH // LANES, LANES).transpose(1, 0, 2)
        mix_b = _conv_call(hcv, w_dw, conv_dw_b[l].reshape(CONV_CH // LANES, 1, LANES),
                           conv_ln_g[l][None, :],
                           conv_ln_b[l][None, :], out_norm_conv_g[l][None, :], seq)

        wo = w_out[l]
        x2 = _ffn_call(x2, mix_a, mix_b, _pair_heads(wo[:NSA_WIDTH], 0).astype(BF16),
                       wo[NSA_WIDTH:].astype(BF16), ffn_norm_g[l][None, :],
                       w_gate_up[l].astype(BF16), w_down[l].astype(BF16))
    return x2.reshape(batch, seq, d_model)
```

```python
import functools
import types

import numpy as np
import jax
import jax.numpy as jnp
from jax import lax
from jax.experimental import pallas as pl
from jax.experimental.pallas import tpu as pltpu

F32 = jnp.float32
BF16 = jnp.bfloat16

D_MODEL = 1024
HEAD_DIM = 64
NSA_HEADS = 8
NSA_KV_HEADS = 2
GQA_REP = NSA_HEADS // NSA_KV_HEADS
NSA_WIDTH = NSA_HEADS * HEAD_DIM
KV_WIDTH = NSA_KV_HEADS * HEAD_DIM
CONV_CH = D_MODEL - NSA_WIDTH
CMP_LEN = 32
CMP_STRIDE = 16
CMP_HIDDEN = 4 * HEAD_DIM
SLC_BLOCK = 64
SLC_TOPN = 8
WINDOW = 512
CONV_WIDTH = 31
FFN_HIDDEN = 2816
ROPE_THETA = 10000.0
EPS = 1e-6
NEG = -1e30
LOG2E = 1.4426950408889634
FORCE = 1e6

LANES = 128
Q_TILE = 128
TILES_PER_STEP = 1
EPILOGUE_LAG = 1
SLC_CHUNK = 256
WIN_KEYS = WINDOW + Q_TILE
PROJ_TILE = 512
CMP_SEQS = 4
CONV_TILE = 512
CONV_HALO = 32
CONV_ROWS = 128
FFN_TILE = 512
FFN_CHUNK = 256
VMEM_LIMIT = 56 * 1024 * 1024

C_Q = 0
C_CMP = 512
C_SLC = 768
C_WIN = 1024
MAIN_COLS = 1280
GATE_COLS = 3 * NSA_HEADS

_TRANS_B = (((1,), (1,)), ((), ()))


def _dot(a, b):
    return jnp.dot(a, b, preferred_element_type=F32)


def _dot_tb(a, b):
    return lax.dot_general(a, b, _TRANS_B, preferred_element_type=F32)


def _split_bf16(x):
    hi = x.astype(BF16)
    lo = (x - hi.astype(F32)).astype(BF16)
    return hi, lo


def _sigmoid(x):
    return 1.0 / (1.0 + jnp.exp(-x))


def _head_norm_rope(y, gain, bd, cos, sin, first_half, scale):
    hi, lo = _split_bf16(y * y)
    ss = _dot(hi, bd) + _dot(lo, bd)
    yn = y * lax.rsqrt(ss * (1.0 / HEAD_DIM) + EPS) * gain
    outs = []
    for j in range(y.shape[1] // LANES):
        blk = yn[:, j * LANES:(j + 1) * LANES]
        rot = jnp.where(first_half, pltpu.roll(blk, LANES - HEAD_DIM // 2, 1),
                        pltpu.roll(blk, HEAD_DIM // 2, 1))
        outs.append((blk * cos + rot * sin) * scale)
    return outs


def _value_t_with_ones(v):
    vt = v.T
    row = lax.broadcasted_iota(jnp.int32, vt.shape, 0)
    return (jnp.where(row < HEAD_DIM, vt, 1.0).astype(BF16),
            jnp.where(row < HEAD_DIM, 1.0, vt).astype(BF16))


def _proj_body(x_ref, gin_ref, wm_ref, wc_ref, wg_ref, cos_ref, sin_ref, gq_ref, gk_ref, bd_ref,
               q_ref, ks_ref, kw_ref, vs0_ref, vs1_ref, vw0_ref, vw1_ref, kc_ref, vc_ref, gt_ref,
               hcv_ref):
    x = x_ref[...]
    xg = (x * gin_ref[...]).astype(BF16)
    rs = lax.rsqrt(jnp.mean(x * x, axis=-1, keepdims=True) + EPS)
    cos = cos_ref[...]
    sin = sin_ref[...]
    bd = bd_ref[...]
    lane = lax.broadcasted_iota(jnp.int32, (PROJ_TILE, LANES), 1)
    first_half = (lane & (HEAD_DIM // 2)) == 0
    held = {}

    def queries(group, y):
        held["q", group] = _head_norm_rope(y, gq_ref[...], bd, cos, sin, first_half,
                                           HEAD_DIM ** -0.5 * LOG2E)
        if group == 0:
            return
        low = lane < HEAD_DIM
        for j in range(2):
            a, b = held["q", 0][j], held["q", 1][j]
            q_ref[:, (2 * j) * LANES:(2 * j + 1) * LANES] = jnp.where(
                low, a, pltpu.roll(b, HEAD_DIM, 1)).astype(BF16)
            q_ref[:, (2 * j + 1) * LANES:(2 * j + 2) * LANES] = jnp.where(
                low, pltpu.roll(a, HEAD_DIM, 1), b).astype(BF16)

    def values(y, out0_ref, out1_ref):
        for i in range(PROJ_TILE // LANES):
            out0_ref[i], out1_ref[i] = _value_t_with_ones(y[i * LANES:(i + 1) * LANES, :])

    def slc_kv(y):
        held["ks"] = y[:, :LANES]
        values(y[:, LANES:], vs0_ref, vs1_ref)

    def win_kv(y):
        kblks = _head_norm_rope(_lane_cat([held["ks"], y[:, :LANES]]), gk_ref[...], bd, cos, sin,
                                first_half, 1.0)
        ks_ref[...] = kblks[0].astype(BF16)
        kw_ref[...] = kblks[1].astype(BF16)
        values(y[:, LANES:], vw0_ref, vw1_ref)

    def cmp_inputs(y):
        kc_ref[...] = y[:, :LANES]
        vc_ref[...] = y[:, LANES:]

    def conv_lin(y):
        held["ca"] = y

    def conv_glu(y):
        hcv_ref[...] = held["ca"] * _sigmoid(y)

    def gates(y):
        sg = _sigmoid(y)
        for i in range(PROJ_TILE // LANES):
            gt_ref[i] = sg[i * LANES:(i + 1) * LANES, :].T[0:32, :]

    sections = [
        (wm_ref, C_Q, 256, functools.partial(queries, 0)),
        (wm_ref, C_Q + 256, 256, functools.partial(queries, 1)),
        (wc_ref, 0, CONV_CH, conv_lin),
        (wm_ref, C_SLC, 256, slc_kv),
        (wc_ref, CONV_CH, CONV_CH, conv_glu),
        (wm_ref, C_WIN, 256, win_kv),
        (wm_ref, C_CMP, 256, cmp_inputs),
        (wg_ref, 0, LANES, gates),
    ]

    def project(sec):
        w_ref, c0, width, _ = sec
        return _dot(xg, w_ref[:, c0:c0 + width])

    y_next = project(sections[0])
    for i, sec in enumerate(sections):
        y_cur = y_next
        if i + 1 < len(sections):
            y_next = project(sections[i + 1])
        sec[3](y_cur * rs)


def _proj_call(x2, gin, w_main, w_conv, w_gate, cos, sin, gq, gk, bd, seq):
    t = x2.shape[0]
    nt = t // PROJ_TILE
    tiles_per_seq = seq // PROJ_TILE
    row = lambda i: (i, 0)
    const = lambda i: (0, 0)
    tab = lambda i: (i % tiles_per_seq, 0)
    blk3 = lambda i: (i, 0, 0)
    nb = PROJ_TILE // LANES
    return pl.pallas_call(
        _proj_body,
        grid=(nt,),
        in_specs=[
            pl.BlockSpec((PROJ_TILE, D_MODEL), row),
            pl.BlockSpec((1, D_MODEL), const),
            pl.BlockSpec((D_MODEL, MAIN_COLS), const),
            pl.BlockSpec((D_MODEL, 2 * CONV_CH), const),
            pl.BlockSpec((D_MODEL, LANES), const),
            pl.BlockSpec((PROJ_TILE, LANES), tab),
            pl.BlockSpec((PROJ_TILE, LANES), tab),
            pl.BlockSpec((1, 256), const),
            pl.BlockSpec((1, 256), const),
            pl.BlockSpec((256, 256), const),
        ],
        out_specs=[
            pl.BlockSpec((PROJ_TILE, NSA_WIDTH), row),
            pl.BlockSpec((PROJ_TILE, LANES), row),
            pl.BlockSpec((PROJ_TILE, LANES), row),
            pl.BlockSpec((nb, LANES, LANES), blk3),
            pl.BlockSpec((nb, LANES, LANES), blk3),
            pl.BlockSpec((nb, LANES, LANES), blk3),
            pl.BlockSpec((nb, LANES, LANES), blk3),
            pl.BlockSpec((PROJ_TILE, LANES), row),
            pl.BlockSpec((PROJ_TILE, LANES), row),
            pl.BlockSpec((nb, 32, LANES), blk3),
            pl.BlockSpec((PROJ_TILE, CONV_CH), row),
        ],
        out_shape=[
            jax.ShapeDtypeStruct((t, NSA_WIDTH), BF16),
            jax.ShapeDtypeStruct((t, LANES), BF16),
            jax.ShapeDtypeStruct((t, LANES), BF16),
            jax.ShapeDtypeStruct((t // LANES, LANES, LANES), BF16),
            jax.ShapeDtypeStruct((t // LANES, LANES, LANES), BF16),
            jax.ShapeDtypeStruct((t // LANES, LANES, LANES), BF16),
            jax.ShapeDtypeStruct((t // LANES, LANES, LANES), BF16),
            jax.ShapeDtypeStruct((t, LANES), F32),
            jax.ShapeDtypeStruct((t, LANES), F32),
            jax.ShapeDtypeStruct((t // LANES, 32, LANES), F32),
            jax.ShapeDtypeStruct((t, CONV_CH), F32),
        ],
        compiler_params=pltpu.CompilerParams(
            dimension_semantics=("parallel",), vmem_limit_bytes=VMEM_LIMIT),
        name="proj",
    )(x2, gin, w_main, w_conv, w_gate, cos, sin, gq, gk, bd)


def _cmp_mlp(t_ref, pe_ref, w1_ref, w2_ref):
    half = CMP_LEN // 2
    nchunk = t_ref.shape[0] // CMP_STRIDE
    toks = [t_ref[pl.ds(l, nchunk, stride=CMP_STRIDE), :] for l in range(half)]
    xp = _lane_cat([(toks[l] + pe_ref[l:l + 1, :]).astype(BF16) for l in range(half)])
    xq = _lane_cat([(toks[l] + pe_ref[half + l:half + l + 1, :]).astype(BF16) for l in range(half)])
    h1 = _dot(xp, w1_ref[0]) + pltpu.roll(_dot(xq, w1_ref[1]), nchunk - 1, 0)
    act = (h1 * _sigmoid(h1)).astype(BF16)
    return _dot(act, w2_ref[...])


def _cmp_body(kc_ref, vc_ref, pek_ref, w1k_ref, w2k_ref, pev_ref, w1v_ref, w2v_ref,
              cos_ref, sin_ref, gk_ref, bd_ref, ko_ref, vt0_ref, vt1_ref):
    kraw = _cmp_mlp(kc_ref, pek_ref, w1k_ref, w2k_ref)
    lane = lax.broadcasted_iota(jnp.int32, kraw.shape, 1)
    first_half = (lane & (HEAD_DIM // 2)) == 0
    (kr,) = _head_norm_rope(kraw, gk_ref[...], bd_ref[...], cos_ref[...], sin_ref[...],
                            first_half, 1.0)
    ko_ref[...] = kr.astype(BF16)
    vraw = _cmp_mlp(vc_ref, pev_ref, w1v_ref, w2v_ref)
    for i in range(CMP_SEQS):
        vt0_ref[i], vt1_ref[i] = _value_t_with_ones(vraw[i * LANES:(i + 1) * LANES, :])


def _cmp_call(kc, vc, pek, w1k, w2k, pev, w1v, w2v, cos, sin, gk, bd, batch, seq):
    seqblk = lambda b: (b, 0)
    c2 = lambda b: (0, 0)
    c3 = lambda b: (0, 0, 0)
    ncmp_pad = seq // CMP_STRIDE
    assert ncmp_pad == LANES and batch % CMP_SEQS == 0
    rows = CMP_SEQS * ncmp_pad
    half_k = (CMP_LEN // 2) * LANES
    return pl.pallas_call(
        _cmp_body,
        grid=(batch // CMP_SEQS,),
        in_specs=[
            pl.BlockSpec((CMP_SEQS * seq, LANES), seqblk),
            pl.BlockSpec((CMP_SEQS * seq, LANES), seqblk),
            pl.BlockSpec((CMP_LEN, LANES), c2),
            pl.BlockSpec((2, half_k, 2 * CMP_HIDDEN), c3),
            pl.BlockSpec((2 * CMP_HIDDEN, LANES), c2),
            pl.BlockSpec((CMP_LEN, LANES), c2),
            pl.BlockSpec((2, half_k, 2 * CMP_HIDDEN), c3),
            pl.BlockSpec((2 * CMP_HIDDEN, LANES), c2),
            pl.BlockSpec((rows, LANES), c2),
            pl.BlockSpec((rows, LANES), c2),
            pl.BlockSpec((1, LANES), c2),
            pl.BlockSpec((LANES, LANES), c2),
        ],
        out_specs=[
            pl.BlockSpec((rows, LANES), seqblk),
            pl.BlockSpec((CMP_SEQS, LANES, ncmp_pad), lambda b: (b, 0, 0)),
            pl.BlockSpec((CMP_SEQS, LANES, ncmp_pad), lambda b: (b, 0, 0)),
        ],
        out_shape=[
            jax.ShapeDtypeStruct((batch * ncmp_pad, LANES), BF16),
            jax.ShapeDtypeStruct((batch, LANES, ncmp_pad), BF16),
            jax.ShapeDtypeStruct((batch, LANES, ncmp_pad), BF16),
        ],
        compiler_params=pltpu.CompilerParams(
            dimension_semantics=("parallel",), vmem_limit_bytes=VMEM_LIMIT),
        name="compress",
    )(kc, vc, pek, w1k, w2k, pev, w1v, w2v, cos, sin, gk, bd)


def _lane_cat(blocks):
    return jnp.concatenate(blocks, axis=1)


def _softmax_numer(s, bias, m_run=None):
    es, ms = [], []
    for r in range(GQA_REP):
        sm = s[:, r * LANES:(r + 1) * LANES] + bias
        m = jnp.max(sm, axis=0, keepdims=True)
        if m_run is not None:
            m = jnp.maximum(m, m_run[:, r * LANES:(r + 1) * LANES])
        es.append(jnp.exp2(sm - m))
        ms.append(m)
    return _lane_cat(es), _lane_cat(ms)


def _safe_inv(l):
    return 1.0 / jnp.where(l > 0.0, l, 1.0)


def _attn_body(*refs):
    first = pl.program_id(0) * TILES_PER_STEP
    n_max = refs[1].shape[0] // SLC_CHUNK
    for n in range(1, n_max + 1):
        pl.when((first * Q_TILE) // SLC_CHUNK + 1 == n)(functools.partial(_attn_step, n, first, *refs))


def _attn_step(n_chunks, first, *refs):
    tiles = [_tile_program(n_chunks, first + t, t, *refs) for t in range(TILES_PER_STEP)]
    items = [(tile, unit) for tile in tiles for unit in tile.units]
    due = {}
    pending = None
    s_next = items[0][0].scores(items[0][1])
    for i, (tile, unit) in enumerate(items):
        s_cur = s_next
        if i + 1 < len(items):
            s_next = items[i + 1][0].scores(items[i + 1][1])
        e, alpha = tile.softmax(unit, s_cur)
        if pending is not None:
            pending[0].weighted_values(*pending[1:])
        pending = (tile, unit, e, alpha)
        for done in due.pop(i, []):
            done.finish()
        if unit is tile.units[-1]:
            due.setdefault(i + 1 + EPILOGUE_LAG, []).append(tile)
    pending[0].weighted_values(*pending[1:])
    for i in sorted(due):
        for done in due[i]:
            done.finish()


def _tile_program(n_chunks, c, slot, q_ref, ks_ref, kw_ref, vs0_ref, vs1_ref, vw0_ref, vw1_ref, kc_ref,
                  vc0_ref, vc1_ref, gt_ref, ovl_ref, gn_ref, out_ref):
    groups = range(NSA_KV_HEADS)
    q_rows = slice(slot * Q_TILE, (slot + 1) * Q_TILE)
    vs_refs, vw_refs, vc_refs = (vs0_ref, vs1_ref), (vw0_ref, vw1_ref), (vc0_ref, vc1_ref)
    sum_row = (HEAD_DIM, 0)
    width = GQA_REP * LANES
    t_lane = c * Q_TILE + lax.broadcasted_iota(jnp.int32, (1, LANES), 1)
    lane_sq = lax.broadcasted_iota(jnp.int32, (Q_TILE, LANES), 1)
    n_live = n_chunks * (SLC_CHUNK // SLC_BLOCK)
    nsel = 32
    win_keys = min(WIN_KEYS, n_chunks * SLC_CHUNK)
    j_idx = lax.broadcasted_iota(jnp.int32, (nsel, LANES), 0)
    cur = lax.shift_right_logical(t_lane, 6)
    n_idx = lax.broadcasted_iota(jnp.int32, (LANES, LANES), 0)
    cbias = jnp.where((n_idx * CMP_STRIDE + (CMP_LEN - 1) <= t_lane) & (n_idx < LANES - 1), 0.0, NEG)
    has_cmp = jnp.where(t_lane >= CMP_LEN - 1, 1.0, 0.0)
    has_cmp = _lane_cat([has_cmp] * GQA_REP)
    win_blk = jnp.maximum(c - WINDOW // Q_TILE, 0)
    win_start = pl.multiple_of(win_blk * Q_TILE, Q_TILE)
    kpos_w = win_start + lax.broadcasted_iota(jnp.int32, (win_keys, LANES), 0)
    wbias = jnp.where((kpos_w <= t_lane) & (kpos_w > t_lane - WINDOW), 0.0, NEG)
    row_sl = lax.broadcasted_iota(jnp.int32, (SLC_CHUNK, LANES), 0)
    zero_q = jnp.zeros((Q_TILE, LANES), BF16)

    def gate_row(g, branch):
        rows = [(g * GQA_REP + r) * 3 + branch for r in range(GQA_REP)]
        return _lane_cat([gt_ref[slot, i:i + 1, :] for i in rows])

    qg = []
    for g in groups:
        in_g = (lane_sq >= HEAD_DIM) if g else (lane_sq < HEAD_DIM)
        qg.append(jnp.concatenate(
            [jnp.where(in_g, q_ref[q_rows, r * LANES:(r + 1) * LANES], zero_q) for r in range(GQA_REP)],
            axis=0))

    o_cmp = [None] * NSA_KV_HEADS
    sel_bias = [None] * NSA_KV_HEADS

    def compressed_softmax(g, sc):
        ec, _ = _softmax_numer(sc, cbias)
        coef = has_cmp * _safe_inv(jnp.sum(ec, axis=0, keepdims=True))
        pc = ec * coef
        psum = (pc[:, 0:LANES] + pc[:, LANES:2 * LANES]
                + pc[:, 2 * LANES:3 * LANES] + pc[:, 3 * LANES:4 * LANES])
        p_hi, p_lo = _split_bf16(psum)
        imp = (_dot(ovl_ref[...], p_hi) + _dot(ovl_ref[...], p_lo))[0:nsel, :]
        valid = j_idx <= cur
        forced = (j_idx == 0) | (j_idx == cur) | (j_idx == cur - 1)
        score = jnp.where(valid, imp + jnp.where(forced, FORCE, 0.0), -FORCE)
        rank = jnp.zeros((nsel, LANES), F32)
        for jp in range(nsel):
            row = score[jp:jp + 1, :]
            beats = (row > score) | ((row == score) & (j_idx > jp))
            rank = rank + jnp.where(beats, 1.0, 0.0)
        sel_bias[g] = jnp.where((rank < float(SLC_TOPN)) & (score > -1.0), 0.0, NEG)
        return ec.astype(BF16), coef

    nsub = SLC_CHUNK // LANES
    blocks_per_chunk = SLC_CHUNK // SLC_BLOCK
    kwin = kw_ref[pl.ds(win_start, win_keys), :]
    units = [("cmp", g, 0) for g in groups] + [("win", g, 0) for g in groups]
    units += [("slc", g, kc) for kc in range(n_chunks) for g in groups]
    m_run = [None] * NSA_KV_HEADS
    acc_s = [None] * NSA_KV_HEADS
    acc_w = [None] * NSA_KV_HEADS

    def scores(unit):
        kind, g, kc = unit
        if kind == "cmp":
            keys = kc_ref[...]
        elif kind == "win":
            keys = kwin
        else:
            keys = ks_ref[kc * SLC_CHUNK:(kc + 1) * SLC_CHUNK, :]
        return _dot_tb(keys, qg[g])

    def softmax(unit, s):
        kind, g, kc = unit
        if kind == "cmp":
            return compressed_softmax(g, s)
        if kind == "win":
            e, _ = _softmax_numer(s, wbias)
            return e.astype(BF16), None
        j0 = kc * blocks_per_chunk
        bias = jnp.concatenate(
            [jnp.broadcast_to(sel_bias[g][j:j + 1, :], (SLC_BLOCK, LANES))
             for j in range(j0, j0 + blocks_per_chunk)], axis=0)
        if kc == n_chunks - 1:
            bias = jnp.where(kc * SLC_CHUNK + row_sl <= t_lane, bias, NEG)
        e, m_new = _softmax_numer(s, bias, m_run[g])
        alpha = None if kc == 0 else jnp.exp2(m_run[g] - m_new)
        m_run[g] = m_new
        return e.astype(BF16), alpha

    def weighted_values(unit, e, alpha):
        kind, g, kc = unit
        if kind == "cmp":
            o_cmp[g] = _dot(vc_refs[g][0], e) * (gate_row(g, 0) * alpha)
            return
        if kind == "win":
            vt = _lane_cat([vw_refs[g][win_blk + i] for i in range(win_keys // LANES)])
            acc_w[g] = _dot(vt, e)
            return
        vt = _lane_cat([vs_refs[g][nsub * kc + i] for i in range(nsub)])
        pv = _dot(vt, e)
        acc_s[g] = pv if alpha is None else acc_s[g] * alpha + pv

    def finish():
        o_groups = []
        for g in groups:
            r0 = sum_row[g]
            o_groups.append(o_cmp[g]
                            + acc_s[g] * (gate_row(g, 1) * _safe_inv(acc_s[g][r0:r0 + 1, :]))
                            + acc_w[g] * (gate_row(g, 2) * _safe_inv(acc_w[g][r0:r0 + 1, :])))
        row_d = lax.broadcasted_iota(jnp.int32, (LANES, width), 0)
        o_t = jnp.where(row_d < HEAD_DIM, o_groups[0], o_groups[1])
        sq = jnp.sum(o_t * o_t, axis=0, keepdims=True)
        ss = (sq[:, 0:LANES] + sq[:, LANES:2 * LANES]
              + sq[:, 2 * LANES:3 * LANES] + sq[:, 3 * LANES:4 * LANES])
        rs = lax.rsqrt(ss * (1.0 / NSA_WIDTH) + EPS)
        for r in range(GQA_REP):
            cols = slice(r * LANES, (r + 1) * LANES)
            out_ref[q_rows, cols] = (o_t[:, cols] * rs * gn_ref[:, cols]).T.astype(BF16)

    return types.SimpleNamespace(units=units, scores=scores, softmax=softmax,
                                 weighted_values=weighted_values, finish=finish)


def _attn_call(q, ks, kw, vs0, vs1, vw0, vw1, kcmp, vc0, vc1, gt, ovl, gn, batch, seq):
    assert SLC_CHUNK % (TILES_PER_STEP * Q_TILE) == 0
    rows = TILES_PER_STEP * Q_TILE
    nq = seq // rows
    nkb = seq // LANES
    qrow = lambda c, b: (b * nq + c, 0)
    per_b = lambda c, b: (b, 0)
    per_b3 = lambda c, b: (b, 0, 0)
    vspec = pl.BlockSpec((nkb, LANES, LANES), per_b3)
    cspec = pl.BlockSpec((1, LANES, LANES), per_b3)
    return pl.pallas_call(
        _attn_body,
        grid=(nq, batch),
        in_specs=[
            pl.BlockSpec((rows, NSA_WIDTH), qrow),
            pl.BlockSpec((seq, LANES), per_b),
            pl.BlockSpec((seq, LANES), per_b),
            vspec, vspec, vspec, vspec,
            pl.BlockSpec((LANES, LANES), per_b),
            cspec, cspec,
            pl.BlockSpec((TILES_PER_STEP, 32, LANES), lambda c, b: (b * nq + c, 0, 0)),
            pl.BlockSpec((LANES, LANES), lambda c, b: (0, 0)),
            pl.BlockSpec((LANES, NSA_WIDTH), lambda c, b: (0, 0)),
        ],
        out_specs=pl.BlockSpec((rows, NSA_WIDTH), qrow),
        out_shape=jax.ShapeDtypeStruct((batch * seq, NSA_WIDTH), BF16),
        compiler_params=pltpu.CompilerParams(
            dimension_semantics=("parallel", "arbitrary"), vmem_limit_bytes=VMEM_LIMIT),
        name="attn",
    )(q, ks, kw, vs0, vs1, vw0, vw1, kcmp, vc0, vc1, gt, ovl, gn)


def _conv_body(tiles_per_seq, cur_ref, halo_ref, w_ref, b_ref, lng_ref, lnb_ref, gn_ref, out_ref,
               buf_ref, sh_ref, cv_ref):
    first = (pl.program_id(0) % tiles_per_seq) == 0
    buf_ref[0:CONV_HALO, :] = jnp.where(first, 0.0, halo_ref[...])
    buf_ref[CONV_HALO:, :] = cur_ref[...]
    lead = CONV_HALO - (CONV_WIDTH - 1)
    ncb = CONV_CH // LANES
    nrb = CONV_TILE // CONV_ROWS
    for s in range(8):
        span = CONV_TILE + 8 * ((CONV_WIDTH - 1 - s) // 8)
        for cb in range(ncb):
            sh_ref[s, cb, 0:span, :] = buf_ref[lead + s:lead + s + span, cb * LANES:(cb + 1) * LANES]

    def conv_block(i, carry):
        cb = i % ncb
        base = pl.multiple_of((i // ncb) * CONV_ROWS, CONV_ROWS)
        acc = jnp.zeros((CONV_ROWS, LANES), F32) + b_ref[cb]
        for k in range(CONV_WIDTH):
            acc = acc + w_ref[cb, k:k + 1, :] * sh_ref[k % 8, cb, pl.ds(base + 8 * (k // 8), CONV_ROWS), :]
        cv_ref[cb, pl.ds(base, CONV_ROWS), :] = acc
        return carry

    lax.fori_loop(0, nrb * ncb, conv_block, 0)
    h = _lane_cat([cv_ref[cb] for cb in range(ncb)])
    mu = jnp.mean(h, axis=-1, keepdims=True)
    d = h - mu
    var = jnp.mean(d * d, axis=-1, keepdims=True)
    hn = d * lax.rsqrt(var + EPS) * lng_ref[...] + lnb_ref[...]
    o = hn * _sigmoid(hn)
    ms = jnp.mean(o * o, axis=-1, keepdims=True)
    out_ref[...] = (o * lax.rsqrt(ms + EPS) * gn_ref[...]).astype(BF16)


def _conv_call(hcv, w, b, lng, lnb, gn, seq):
    t = hcv.shape[0]
    nt = t // CONV_TILE
    halo_per_tile = CONV_TILE // CONV_HALO
    ncb = CONV_CH // LANES
    row = lambda i: (i, 0)
    const = lambda i: (0, 0)
    return pl.pallas_call(
        functools.partial(_conv_body, seq // CONV_TILE),
        grid=(nt,),
        in_specs=[
            pl.BlockSpec((CONV_TILE, CONV_CH), row),
            pl.BlockSpec((CONV_HALO, CONV_CH), lambda i: (jnp.maximum(i * halo_per_tile - 1, 0), 0)),
            pl.BlockSpec((ncb, CONV_HALO, LANES), lambda i: (0, 0, 0)),
            pl.BlockSpec((ncb, 1, LANES), lambda i: (0, 0, 0)),
            pl.BlockSpec((1, CONV_CH), const),
            pl.BlockSpec((1, CONV_CH), const),
            pl.BlockSpec((1, CONV_CH), const),
        ],
        out_specs=pl.BlockSpec((CONV_TILE, CONV_CH), row),
        out_shape=jax.ShapeDtypeStruct((t, CONV_CH), BF16),
        scratch_shapes=[pltpu.VMEM((CONV_HALO + CONV_TILE, CONV_CH), F32),
                        pltpu.VMEM((8, ncb, CONV_TILE + 8 * ((CONV_WIDTH - 1) // 8), LANES), F32),
                        pltpu.VMEM((ncb, CONV_TILE, LANES), F32)],
        compiler_params=pltpu.CompilerParams(
            dimension_semantics=("parallel",), vmem_limit_bytes=VMEM_LIMIT),
        name="conv",
    )(hcv, hcv, w, b, lng, lnb, gn)


def _ffn_body(x_ref, ma_ref, mb_ref, woa_ref, wob_ref, gf_ref, wgu_ref, wd_ref, out_ref):
    h = x_ref[...] + _dot(ma_ref[...], woa_ref[...]) + _dot(mb_ref[...], wob_ref[...])
    ms = jnp.mean(h * h, axis=-1, keepdims=True)
    hn = (h * lax.rsqrt(ms + EPS) * gf_ref[...]).astype(BF16)
    acc = jnp.zeros_like(h)
    for j in range(FFN_HIDDEN // FFN_CHUNK):
        c0 = j * FFN_CHUNK
        gte = _dot(hn, wgu_ref[:, c0:c0 + FFN_CHUNK])
        up = _dot(hn, wgu_ref[:, FFN_HIDDEN + c0:FFN_HIDDEN + c0 + FFN_CHUNK])
        act = (gte * _sigmoid(gte) * up).astype(BF16)
        acc = acc + _dot(act, wd_ref[c0:c0 + FFN_CHUNK, :])
    out_ref[...] = h + acc


def _ffn_call(x2, ma, mb, woa, wob, gf, wgu, wd):
    t = x2.shape[0]
    nt = t // FFN_TILE
    row = lambda i: (i, 0)
    const = lambda i: (0, 0)
    once = pl.Buffered(1)
    return pl.pallas_call(
        _ffn_body,
        grid=(nt,),
        in_specs=[
            pl.BlockSpec((FFN_TILE, D_MODEL), row),
            pl.BlockSpec((FFN_TILE, NSA_WIDTH), row),
            pl.BlockSpec((FFN_TILE, CONV_CH), row),
            pl.BlockSpec((NSA_WIDTH, D_MODEL), const, pipeline_mode=once),
            pl.BlockSpec((CONV_CH, D_MODEL), const, pipeline_mode=once),
            pl.BlockSpec((1, D_MODEL), const),
            pl.BlockSpec((D_MODEL, 2 * FFN_HIDDEN), const, pipeline_mode=once),
            pl.BlockSpec((FFN_HIDDEN, D_MODEL), const, pipeline_mode=once),
        ],
        out_specs=pl.BlockSpec((FFN_TILE, D_MODEL), row),
        out_shape=jax.ShapeDtypeStruct((t, D_MODEL), F32),
        compiler_params=pltpu.CompilerParams(
            dimension_semantics=("parallel",), vmem_limit_bytes=VMEM_LIMIT),
        name="ffn",
    )(x2, ma, mb, woa, wob, gf, wgu, wd)


def _pair_heads(a, axis):
    shape = a.shape
    a = a.reshape(shape[:axis] + (NSA_KV_HEADS, GQA_REP, HEAD_DIM) + shape[axis + 1:])
    return jnp.swapaxes(a, axis, axis + 1).reshape(shape)


def _transposed_gain(g):
    cols = g.reshape(GQA_REP, LANES).T
    return jnp.broadcast_to(cols[:, :, None], (LANES, GQA_REP, Q_TILE)).reshape(LANES, GQA_REP * Q_TILE)


def _rope_tables(pos):
    half = HEAD_DIM // 2
    inv = ROPE_THETA ** (-np.arange(half, dtype=np.float64) / half)
    ang = np.asarray(pos, np.float64)[:, None] * inv[None, :]
    cos = np.concatenate([np.cos(ang), np.cos(ang)], axis=1)
    sin = np.concatenate([-np.sin(ang), np.sin(ang)], axis=1)
    return (np.tile(cos, (1, NSA_KV_HEADS)).astype(np.float32),
            np.tile(sin, (1, NSA_KV_HEADS)).astype(np.float32))


def _block_diag_ones(width):
    idx = np.arange(width) // HEAD_DIM
    return (idx[:, None] == idx[None, :]).astype(np.float32)


def _overlap_t(seq):
    ncmp = (seq - CMP_LEN) // CMP_STRIDE + 1
    cs = np.arange(ncmp)[:, None] * CMP_STRIDE
    ss = np.arange(seq // SLC_BLOCK)[None, :] * SLC_BLOCK
    ov = np.clip(np.minimum(cs + CMP_LEN, ss + SLC_BLOCK) - np.maximum(cs, ss), 0, None) / CMP_LEN
    out = np.zeros((LANES, LANES), np.float32)
    out[:seq // SLC_BLOCK, :ncmp] = ov.T
    return out


def _cmp_weights(pe, w1, w2):
    pe2 = jnp.concatenate([pe, pe], axis=1)
    w1r = w1.reshape(CMP_LEN, HEAD_DIM, CMP_HIDDEN)
    z1 = jnp.zeros_like(w1r)
    w1b = jnp.concatenate([jnp.concatenate([w1r, z1], axis=2),
                           jnp.concatenate([z1, w1r], axis=2)], axis=1).astype(BF16)
    z2 = jnp.zeros_like(w2)
    w2b = jnp.concatenate([jnp.concatenate([w2, z2], axis=1),
                           jnp.concatenate([z2, w2], axis=1)], axis=0).astype(BF16)
    w1b = w1b.reshape(2, (CMP_LEN // 2) * LANES, 2 * CMP_HIDDEN)
    return pe2, w1b, w2b


def kernel(x, attn_norm_g, w_in, q_norm_g, k_norm_cmp_g, k_norm_slc_g, k_norm_win_g, cmp_pe_k, cmp_w1_k, cmp_w2_k, cmp_pe_v, cmp_w1_v, cmp_w2_v, conv_dw_w, conv_dw_b, conv_ln_g, conv_ln_b, out_norm_nsa_g, out_norm_conv_g, w_out, ffn_norm_g, w_gate_up, w_down):
    batch, seq, d_model = x.shape
    assert d_model == D_MODEL and seq % PROJ_TILE == 0 and seq // SLC_BLOCK == 32
    depth = w_in.shape[0]
    cos_np, sin_np = _rope_tables(np.arange(seq))
    ccos_np, csin_np = _rope_tables(np.arange(seq // CMP_STRIDE) * CMP_STRIDE + CMP_LEN - 1)
    ccos_np, csin_np = np.tile(ccos_np, (CMP_SEQS, 1)), np.tile(csin_np, (CMP_SEQS, 1))
    cos, sin, ccos, csin = map(jnp.asarray, (cos_np, sin_np, ccos_np, csin_np))
    bd256 = jnp.asarray(_block_diag_ones(256), BF16)
    bd128 = jnp.asarray(_block_diag_ones(LANES), BF16)
    ovl = jnp.asarray(_overlap_t(seq), BF16)

    x2 = x.reshape(batch * seq, d_model)
    for l in range(depth):
        w = w_in[l]
        w_main = w[:, :MAIN_COLS].astype(BF16)
        w_gate = jnp.pad(w[:, MAIN_COLS:MAIN_COLS + GATE_COLS],
                         ((0, 0), (0, LANES - GATE_COLS))).astype(BF16)
        w_conv = w[:, MAIN_COLS + GATE_COLS:].astype(BF16)
        gq = jnp.tile(q_norm_g[l], 4)[None, :]
        gk = jnp.concatenate([jnp.tile(k_norm_slc_g[l], 2), jnp.tile(k_norm_win_g[l], 2)])[None, :]
        q, ks, kw, vs0, vs1, vw0, vw1, kc, vc, gt, hcv = _proj_call(
            x2, attn_norm_g[l][None, :], w_main, w_conv, w_gate, cos, sin, gq, gk, bd256, seq)

        pek, w1k, w2k = _cmp_weights(cmp_pe_k[l], cmp_w1_k[l], cmp_w2_k[l])
        pev, w1v, w2v = _cmp_weights(cmp_pe_v[l], cmp_w1_v[l], cmp_w2_v[l])
        kcmp, vc0, vc1 = _cmp_call(kc, vc, pek, w1k, w2k, pev, w1v, w2v, ccos, csin,
                                jnp.tile(k_norm_cmp_g[l], 2)[None, :], bd128, batch, seq)

        mix_a = _attn_call(q, ks, kw, vs0, vs1, vw0, vw1, kcmp, vc0, vc1, gt, ovl,
                           _transposed_gain(_pair_heads(out_norm_nsa_g[l], 0)), batch, seq)

        w_dw = jnp.concatenate([conv_dw_w[l][:, 0, :],
                                jnp.zeros((CONV_HALO - CONV_WIDTH, CONV_CH), F32)], axis=0)
        w_dw = w_dw.reshape(CONV_HALO, CONV_CH // LANES, LANES).transpose(1, 0, 2)
        mix_b = _conv_call(hcv, w_dw, conv_dw_b[l].reshape(CONV_CH // LANES, 1, LANES),
                           conv_ln_g[l][None, :],
                           conv_ln_b[l][None, :], out_norm_conv_g[l][None, :], seq)

        wo = w_out[l]
        x2 = _ffn_call(x2, mix_a, mix_b, _pair_heads(wo[:NSA_WIDTH], 0).astype(BF16),
                       wo[NSA_WIDTH:].astype(BF16), ffn_norm_g[l][None, :],
                       w_gate_up[l].astype(BF16), w_down[l].astype(BF16))
    return x2.reshape(batch, seq, d_model)
```

```python
import functools
import types

import numpy as np
import jax
import jax.numpy as jnp
from jax import lax
from jax.experimental import pallas as pl
from jax.experimental.pallas import tpu as pltpu

F32 = jnp.float32
BF16 = jnp.bfloat16

D_MODEL = 1024
HEAD_DIM = 64
NSA_HEADS = 8
NSA_KV_HEADS = 2
GQA_REP = NSA_HEADS // NSA_KV_HEADS
NSA_WIDTH = NSA_HEADS * HEAD_DIM
KV_WIDTH = NSA_KV_HEADS * HEAD_DIM
CONV_CH = D_MODEL - NSA_WIDTH
CMP_LEN = 32
CMP_STRIDE = 16
CMP_HIDDEN = 4 * HEAD_DIM
SLC_BLOCK = 64
SLC_TOPN = 8
WINDOW = 512
CONV_WIDTH = 31
FFN_HIDDEN = 2816
ROPE_THETA = 10000.0
EPS = 1e-6
NEG = -1e30
LOG2E = 1.4426950408889634
FORCE = 1e6

LANES = 128
Q_TILE = 128
TILES_PER_STEP = 2
SLC_CHUNK = 256
WIN_KEYS = WINDOW + Q_TILE
PROJ_TILE = 512
CMP_SEQS = 4
CONV_TILE = 512
CONV_HALO = 32
CONV_ROWS = 128
FFN_TILE = 512
FFN_CHUNK = 256
VMEM_LIMIT = 56 * 1024 * 1024

C_Q = 0
C_CMP = 512
C_SLC = 768
C_WIN = 1024
MAIN_COLS = 1280
GATE_COLS = 3 * NSA_HEADS

_TRANS_B = (((1,), (1,)), ((), ()))


def _dot(a, b):
    return jnp.dot(a, b, preferred_element_type=F32)


def _dot_tb(a, b):
    return lax.dot_general(a, b, _TRANS_B, preferred_element_type=F32)


def _split_bf16(x):
    hi = x.astype(BF16)
    lo = (x - hi.astype(F32)).astype(BF16)
    return hi, lo


def _sigmoid(x):
    return 1.0 / (1.0 + jnp.exp(-x))


def _head_norm_rope(y, gain, bd, cos, sin, first_half, scale):
    hi, lo = _split_bf16(y * y)
    ss = _dot(hi, bd) + _dot(lo, bd)
    yn = y * lax.rsqrt(ss * (1.0 / HEAD_DIM) + EPS) * gain
    outs = []
    for j in range(y.shape[1] // LANES):
        blk = yn[:, j * LANES:(j + 1) * LANES]
        rot = jnp.where(first_half, pltpu.roll(blk, LANES - HEAD_DIM // 2, 1),
                        pltpu.roll(blk, HEAD_DIM // 2, 1))
        outs.append((blk * cos + rot * sin) * scale)
    return outs


def _value_t_with_ones(v):
    vt = v.T
    row = lax.broadcasted_iota(jnp.int32, vt.shape, 0)
    return (jnp.where(row < HEAD_DIM, vt, 1.0).astype(BF16),
            jnp.where(row < HEAD_DIM, 1.0, vt).astype(BF16))


def _proj_body(x_ref, gin_ref, wm_ref, wc_ref, wg_ref, cos_ref, sin_ref, gq_ref, gk_ref, bd_ref,
               q_ref, ks_ref, kw_ref, vs0_ref, vs1_ref, vw0_ref, vw1_ref, kc_ref, vc_ref, gt_ref,
               hcv_ref):
    x = x_ref[...]
    xg = (x * gin_ref[...]).astype(BF16)
    rs = lax.rsqrt(jnp.mean(x * x, axis=-1, keepdims=True) + EPS)
    cos = cos_ref[...]
    sin = sin_ref[...]
    bd = bd_ref[...]
    lane = lax.broadcasted_iota(jnp.int32, (PROJ_TILE, LANES), 1)
    first_half = (lane & (HEAD_DIM // 2)) == 0
    held = {}

    def queries(group, y):
        held["q", group] = _head_norm_rope(y, gq_ref[...], bd, cos, sin, first_half,
                                           HEAD_DIM ** -0.5 * LOG2E)
        if group == 0:
            return
        low = lane < HEAD_DIM
        for j in range(2):
            a, b = held["q", 0][j], held["q", 1][j]
            q_ref[:, (2 * j) * LANES:(2 * j + 1) * LANES] = jnp.where(
                low, a, pltpu.roll(b, HEAD_DIM, 1)).astype(BF16)
            q_ref[:, (2 * j + 1) * LANES:(2 * j + 2) * LANES] = jnp.where(
                low, pltpu.roll(a, HEAD_DIM, 1), b).astype(BF16)

    def values(y, out0_ref, out1_ref):
        for i in range(PROJ_TILE // LANES):
            out0_ref[i], out1_ref[i] = _value_t_with_ones(y[i * LANES:(i + 1) * LANES, :])

    def slc_kv(y):
        held["ks"] = y[:, :LANES]
        values(y[:, LANES:], vs0_ref, vs1_ref)

    def win_kv(y):
        kblks = _head_norm_rope(_lane_cat([held["ks"], y[:, :LANES]]), gk_ref[...], bd, cos, sin,
                                first_half, 1.0)
        ks_ref[...] = kblks[0].astype(BF16)
        kw_ref[...] = kblks[1].astype(BF16)
        values(y[:, LANES:], vw0_ref, vw1_ref)

    def cmp_inputs(y):
        kc_ref[...] = y[:, :LANES]
        vc_ref[...] = y[:, LANES:]

    def conv_lin(y):
        held["ca"] = y

    def conv_glu(y):
        hcv_ref[...] = held["ca"] * _sigmoid(y)

    def gates(y):
        sg = _sigmoid(y)
        for i in range(PROJ_TILE // LANES):
            gt_ref[i] = sg[i * LANES:(i + 1) * LANES, :].T[0:32, :]

    sections = [
        (wm_ref, C_Q, 256, functools.partial(queries, 0)),
        (wm_ref, C_Q + 256, 256, functools.partial(queries, 1)),
        (wc_ref, 0, CONV_CH, conv_lin),
        (wm_ref, C_SLC, 256, slc_kv),
        (wc_ref, CONV_CH, CONV_CH, conv_glu),
        (wm_ref, C_WIN, 256, win_kv),
        (wm_ref, C_CMP, 256, cmp_inputs),
        (wg_ref, 0, LANES, gates),
    ]

    def project(sec):
        w_ref, c0, width, _ = sec
        return _dot(xg, w_ref[:, c0:c0 + width])

    y_next = project(sections[0])
    for i, sec in enumerate(sections):
        y_cur = y_next
        if i + 1 < len(sections):
            y_next = project(sections[i + 1])
        sec[3](y_cur * rs)


def _proj_call(x2, gin, w_main, w_conv, w_gate, cos, sin, gq, gk, bd, seq):
    t = x2.shape[0]
    nt = t // PROJ_TILE
    tiles_per_seq = seq // PROJ_TILE
    row = lambda i: (i, 0)
    const = lambda i: (0, 0)
    tab = lambda i: (i % tiles_per_seq, 0)
    blk3 = lambda i: (i, 0, 0)
    nb = PROJ_TILE // LANES
    return pl.pallas_call(
        _proj_body,
        grid=(nt,),
        in_specs=[
            pl.BlockSpec((PROJ_TILE, D_MODEL), row),
            pl.BlockSpec((1, D_MODEL), const),
            pl.BlockSpec((D_MODEL, MAIN_COLS), const),
            pl.BlockSpec((D_MODEL, 2 * CONV_CH), const),
            pl.BlockSpec((D_MODEL, LANES), const),
            pl.BlockSpec((PROJ_TILE, LANES), tab),
            pl.BlockSpec((PROJ_TILE, LANES), tab),
            pl.BlockSpec((1, 256), const),
            pl.BlockSpec((1, 256), const),
            pl.BlockSpec((256, 256), const),
        ],
        out_specs=[
            pl.BlockSpec((PROJ_TILE, NSA_WIDTH), row),
            pl.BlockSpec((PROJ_TILE, LANES), row),
            pl.BlockSpec((PROJ_TILE, LANES), row),
            pl.BlockSpec((nb, LANES, LANES), blk3),
            pl.BlockSpec((nb, LANES, LANES), blk3),
            pl.BlockSpec((nb, LANES, LANES), blk3),
            pl.BlockSpec((nb, LANES, LANES), blk3),
            pl.BlockSpec((PROJ_TILE, LANES), row),
            pl.BlockSpec((PROJ_TILE, LANES), row),
            pl.BlockSpec((nb, 32, LANES), blk3),
            pl.BlockSpec((PROJ_TILE, CONV_CH), row),
        ],
        out_shape=[
            jax.ShapeDtypeStruct((t, NSA_WIDTH), BF16),
            jax.ShapeDtypeStruct((t, LANES), BF16),
            jax.ShapeDtypeStruct((t, LANES), BF16),
            jax.ShapeDtypeStruct((t // LANES, LANES, LANES), BF16),
            jax.ShapeDtypeStruct((t // LANES, LANES, LANES), BF16),
            jax.ShapeDtypeStruct((t // LANES, LANES, LANES), BF16),
            jax.ShapeDtypeStruct((t // LANES, LANES, LANES), BF16),
            jax.ShapeDtypeStruct((t, LANES), F32),
            jax.ShapeDtypeStruct((t, LANES), F32),
            jax.ShapeDtypeStruct((t // LANES, 32, LANES), F32),
            jax.ShapeDtypeStruct((t, CONV_CH), F32),
        ],
        compiler_params=pltpu.CompilerParams(
            dimension_semantics=("parallel",), vmem_limit_bytes=VMEM_LIMIT),
        name="proj",
    )(x2, gin, w_main, w_conv, w_gate, cos, sin, gq, gk, bd)


def _cmp_mlp(t_ref, pe_ref, w1_ref, w2_ref):
    half = CMP_LEN // 2
    nchunk = t_ref.shape[0] // CMP_STRIDE
    toks = [t_ref[pl.ds(l, nchunk, stride=CMP_STRIDE), :] for l in range(half)]
    xp = _lane_cat([(toks[l] + pe_ref[l:l + 1, :]).astype(BF16) for l in range(half)])
    xq = _lane_cat([(toks[l] + pe_ref[half + l:half + l + 1, :]).astype(BF16) for l in range(half)])
    h1 = _dot(xp, w1_ref[0]) + pltpu.roll(_dot(xq, w1_ref[1]), nchunk - 1, 0)
    act = (h1 * _sigmoid(h1)).astype(BF16)
    return _dot(act, w2_ref[...])


def _cmp_body(kc_ref, vc_ref, pek_ref, w1k_ref, w2k_ref, pev_ref, w1v_ref, w2v_ref,
              cos_ref, sin_ref, gk_ref, bd_ref, ko_ref, vt0_ref, vt1_ref):
    kraw = _cmp_mlp(kc_ref, pek_ref, w1k_ref, w2k_ref)
    lane = lax.broadcasted_iota(jnp.int32, kraw.shape, 1)
    first_half = (lane & (HEAD_DIM // 2)) == 0
    (kr,) = _head_norm_rope(kraw, gk_ref[...], bd_ref[...], cos_ref[...], sin_ref[...],
                            first_half, 1.0)
    ko_ref[...] = kr.astype(BF16)
    vraw = _cmp_mlp(vc_ref, pev_ref, w1v_ref, w2v_ref)
    for i in range(CMP_SEQS):
        vt0_ref[i], vt1_ref[i] = _value_t_with_ones(vraw[i * LANES:(i + 1) * LANES, :])


def _cmp_call(kc, vc, pek, w1k, w2k, pev, w1v, w2v, cos, sin, gk, bd, batch, seq):
    seqblk = lambda b: (b, 0)
    c2 = lambda b: (0, 0)
    c3 = lambda b: (0, 0, 0)
    ncmp_pad = seq // CMP_STRIDE
    assert ncmp_pad == LANES and batch % CMP_SEQS == 0
    rows = CMP_SEQS * ncmp_pad
    half_k = (CMP_LEN // 2) * LANES
    return pl.pallas_call(
        _cmp_body,
        grid=(batch // CMP_SEQS,),
        in_specs=[
            pl.BlockSpec((CMP_SEQS * seq, LANES), seqblk),
            pl.BlockSpec((CMP_SEQS * seq, LANES), seqblk),
            pl.BlockSpec((CMP_LEN, LANES), c2),
            pl.BlockSpec((2, half_k, 2 * CMP_HIDDEN), c3),
            pl.BlockSpec((2 * CMP_HIDDEN, LANES), c2),
            pl.BlockSpec((CMP_LEN, LANES), c2),
            pl.BlockSpec((2, half_k, 2 * CMP_HIDDEN), c3),
            pl.BlockSpec((2 * CMP_HIDDEN, LANES), c2),
            pl.BlockSpec((rows, LANES), c2),
            pl.BlockSpec((rows, LANES), c2),
            pl.BlockSpec((1, LANES), c2),
            pl.BlockSpec((LANES, LANES), c2),
        ],
        out_specs=[
            pl.BlockSpec((rows, LANES), seqblk),
            pl.BlockSpec((CMP_SEQS, LANES, ncmp_pad), lambda b: (b, 0, 0)),
            pl.BlockSpec((CMP_SEQS, LANES, ncmp_pad), lambda b: (b, 0, 0)),
        ],
        out_shape=[
            jax.ShapeDtypeStruct((batch * ncmp_pad, LANES), BF16),
            jax.ShapeDtypeStruct((batch, LANES, ncmp_pad), BF16),
            jax.ShapeDtypeStruct((batch, LANES, ncmp_pad), BF16),
        ],
        compiler_params=pltpu.CompilerParams(
            dimension_semantics=("parallel",), vmem_limit_bytes=VMEM_LIMIT),
        name="compress",
    )(kc, vc, pek, w1k, w2k, pev, w1v, w2v, cos, sin, gk, bd)


def _lane_cat(blocks):
    return jnp.concatenate(blocks, axis=1)


def _softmax_numer(s, bias, m_run=None):
    es, ms = [], []
    for r in range(GQA_REP):
        sm = s[:, r * LANES:(r + 1) * LANES] + bias
        m = jnp.max(sm, axis=0, keepdims=True)
        if m_run is not None:
            m = jnp.maximum(m, m_run[:, r * LANES:(r + 1) * LANES])
        es.append(jnp.exp2(sm - m))
        ms.append(m)
    return _lane_cat(es), _lane_cat(ms)


def _safe_inv(l):
    return 1.0 / jnp.where(l > 0.0, l, 1.0)


def _attn_body(*refs):
    first = pl.program_id(0) * TILES_PER_STEP
    n_max = refs[1].shape[0] // SLC_CHUNK
    for n in range(1, n_max + 1):
        pl.when((first * Q_TILE) // SLC_CHUNK + 1 == n)(functools.partial(_attn_step, n, first, *refs))


def _attn_step(n_chunks, first, *refs):
    def one_tile(slot, carry):
        tile = _tile_program(n_chunks, first + slot, slot, *refs)
        pending = None
        s_next = tile.scores(tile.units[0])
        for i, unit in enumerate(tile.units):
            s_cur = s_next
            if i + 1 < len(tile.units):
                s_next = tile.scores(tile.units[i + 1])
            e, alpha = tile.softmax(unit, s_cur)
            if pending is not None:
                tile.weighted_values(*pending)
            pending = (unit, e, alpha)
        tile.weighted_values(*pending)
        tile.finish()
        return carry

    lax.fori_loop(0, TILES_PER_STEP, one_tile, 0)


def _tile_program(n_chunks, c, slot, q_ref, ks_ref, kw_ref, vs0_ref, vs1_ref, vw0_ref, vw1_ref, kc_ref,
                  vc0_ref, vc1_ref, gt_ref, ovl_ref, gn_ref, out_ref):
    groups = range(NSA_KV_HEADS)
    q_rows = pl.ds(pl.multiple_of(slot * Q_TILE, Q_TILE), Q_TILE)
    vs_refs, vw_refs, vc_refs = (vs0_ref, vs1_ref), (vw0_ref, vw1_ref), (vc0_ref, vc1_ref)
    sum_row = (HEAD_DIM, 0)
    width = GQA_REP * LANES
    t_lane = c * Q_TILE + lax.broadcasted_iota(jnp.int32, (1, LANES), 1)
    lane_sq = lax.broadcasted_iota(jnp.int32, (Q_TILE, LANES), 1)
    n_live = n_chunks * (SLC_CHUNK // SLC_BLOCK)
    nsel = 32
    win_keys = min(WIN_KEYS, n_chunks * SLC_CHUNK)
    j_idx = lax.broadcasted_iota(jnp.int32, (nsel, LANES), 0)
    cur = lax.shift_right_logical(t_lane, 6)
    n_idx = lax.broadcasted_iota(jnp.int32, (LANES, LANES), 0)
    cbias = jnp.where((n_idx * CMP_STRIDE + (CMP_LEN - 1) <= t_lane) & (n_idx < LANES - 1), 0.0, NEG)
    has_cmp = jnp.where(t_lane >= CMP_LEN - 1, 1.0, 0.0)
    has_cmp = _lane_cat([has_cmp] * GQA_REP)
    win_blk = jnp.maximum(c - WINDOW // Q_TILE, 0)
    win_start = pl.multiple_of(win_blk * Q_TILE, Q_TILE)
    kpos_w = win_start + lax.broadcasted_iota(jnp.int32, (win_keys, LANES), 0)
    wbias = jnp.where((kpos_w <= t_lane) & (kpos_w > t_lane - WINDOW), 0.0, NEG)
    row_sl = lax.broadcasted_iota(jnp.int32, (SLC_CHUNK, LANES), 0)
    zero_q = jnp.zeros((Q_TILE, LANES), BF16)

    def gate_row(g, branch):
        rows = [(g * GQA_REP + r) * 3 + branch for r in range(GQA_REP)]
        return _lane_cat([gt_ref[slot, i:i + 1, :] for i in rows])

    qg = []
    for g in groups:
        in_g = (lane_sq >= HEAD_DIM) if g else (lane_sq < HEAD_DIM)
        qg.append(jnp.concatenate(
            [jnp.where(in_g, q_ref[q_rows, r * LANES:(r + 1) * LANES], zero_q) for r in range(GQA_REP)],
            axis=0))

    o_cmp = [None] * NSA_KV_HEADS
    sel_bias = [None] * NSA_KV_HEADS

    def compressed_softmax(g, sc):
        ec, _ = _softmax_numer(sc, cbias)
        coef = has_cmp * _safe_inv(jnp.sum(ec, axis=0, keepdims=True))
        pc = ec * coef
        psum = (pc[:, 0:LANES] + pc[:, LANES:2 * LANES]
                + pc[:, 2 * LANES:3 * LANES] + pc[:, 3 * LANES:4 * LANES])
        p_hi, p_lo = _split_bf16(psum)
        imp = (_dot(ovl_ref[...], p_hi) + _dot(ovl_ref[...], p_lo))[0:nsel, :]
        valid = j_idx <= cur
        forced = (j_idx == 0) | (j_idx == cur) | (j_idx == cur - 1)
        score = jnp.where(valid, imp + jnp.where(forced, FORCE, 0.0), -FORCE)
        rank = jnp.zeros((nsel, LANES), F32)
        for jp in range(nsel):
            row = score[jp:jp + 1, :]
            beats = (row > score) | ((row == score) & (j_idx > jp))
            rank = rank + jnp.where(beats, 1.0, 0.0)
        sel_bias[g] = jnp.where((rank < float(SLC_TOPN)) & (score > -1.0), 0.0, NEG)
        return ec.astype(BF16), coef

    nsub = SLC_CHUNK // LANES
    blocks_per_chunk = SLC_CHUNK // SLC_BLOCK
    kwin = kw_ref[pl.ds(win_start, win_keys), :]
    units = [("cmp", g, 0) for g in groups] + [("win", g, 0) for g in groups]
    units += [("slc", g, kc) for kc in range(n_chunks) for g in groups]
    m_run = [None] * NSA_KV_HEADS
    acc_s = [None] * NSA_KV_HEADS
    acc_w = [None] * NSA_KV_HEADS

    def scores(unit):
        kind, g, kc = unit
        if kind == "cmp":
            keys = kc_ref[...]
        elif kind == "win":
            keys = kwin
        else:
            keys = ks_ref[kc * SLC_CHUNK:(kc + 1) * SLC_CHUNK, :]
        return _dot_tb(keys, qg[g])

    def softmax(unit, s):
        kind, g, kc = unit
        if kind == "cmp":
            return compressed_softmax(g, s)
        if kind == "win":
            e, _ = _softmax_numer(s, wbias)
            return e.astype(BF16), None
        j0 = kc * blocks_per_chunk
        bias = jnp.concatenate(
            [jnp.broadcast_to(sel_bias[g][j:j + 1, :], (SLC_BLOCK, LANES))
             for j in range(j0, j0 + blocks_per_chunk)], axis=0)
        if kc == n_chunks - 1:
            bias = jnp.where(kc * SLC_CHUNK + row_sl <= t_lane, bias, NEG)
        e, m_new = _softmax_numer(s, bias, m_run[g])
        alpha = None if kc == 0 else jnp.exp2(m_run[g] - m_new)
        m_run[g] = m_new
        return e.astype(BF16), alpha

    def weighted_values(unit, e, alpha):
        kind, g, kc = unit
        if kind == "cmp":
            o_cmp[g] = _dot(vc_refs[g][0], e) * (gate_row(g, 0) * alpha)
            return
        if kind == "win":
            vt = _lane_cat([vw_refs[g][win_blk + i] for i in range(win_keys // LANES)])
            acc_w[g] = _dot(vt, e)
            return
        vt = _lane_cat([vs_refs[g][nsub * kc + i] for i in range(nsub)])
        pv = _dot(vt, e)
        acc_s[g] = pv if alpha is None else acc_s[g] * alpha + pv

    def finish():
        o_groups = []
        for g in groups:
            r0 = sum_row[g]
            o_groups.append(o_cmp[g]
                            + acc_s[g] * (gate_row(g, 1) * _safe_inv(acc_s[g][r0:r0 + 1, :]))
                            + acc_w[g] * (gate_row(g, 2) * _safe_inv(acc_w[g][r0:r0 + 1, :])))
        row_d = lax.broadcasted_iota(jnp.int32, (LANES, width), 0)
        o_t = jnp.where(row_d < HEAD_DIM, o_groups[0], o_groups[1])
        sq = jnp.sum(o_t * o_t, axis=0, keepdims=True)
        ss = (sq[:, 0:LANES] + sq[:, LANES:2 * LANES]
              + sq[:, 2 * LANES:3 * LANES] + sq[:, 3 * LANES:4 * LANES])
        rs = lax.rsqrt(ss * (1.0 / NSA_WIDTH) + EPS)
        for r in range(GQA_REP):
            cols = slice(r * LANES, (r + 1) * LANES)
            out_ref[q_rows, cols] = (o_t[:, cols] * rs * gn_ref[:, cols]).T.astype(BF16)

    return types.SimpleNamespace(units=units, scores=scores, softmax=softmax,
                                 weighted_values=weighted_values, finish=finish)


def _attn_call(q, ks, kw, vs0, vs1, vw0, vw1, kcmp, vc0, vc1, gt, ovl, gn, batch, seq):
    assert SLC_CHUNK % (TILES_PER_STEP * Q_TILE) == 0
    rows = TILES_PER_STEP * Q_TILE
    nq = seq // rows
    nkb = seq // LANES
    qrow = lambda c, b: (b * nq + c, 0)
    per_b = lambda c, b: (b, 0)
    per_b3 = lambda c, b: (b, 0, 0)
    vspec = pl.BlockSpec((nkb, LANES, LANES), per_b3)
    cspec = pl.BlockSpec((1, LANES, LANES), per_b3)
    return pl.pallas_call(
        _attn_body,
        grid=(nq, batch),
        in_specs=[
            pl.BlockSpec((rows, NSA_WIDTH), qrow),
            pl.BlockSpec((seq, LANES), per_b),
            pl.BlockSpec((seq, LANES), per_b),
            vspec, vspec, vspec, vspec,
            pl.BlockSpec((LANES, LANES), per_b),
            cspec, cspec,
            pl.BlockSpec((TILES_PER_STEP, 32, LANES), lambda c, b: (b * nq + c, 0, 0)),
            pl.BlockSpec((LANES, LANES), lambda c, b: (0, 0)),
            pl.BlockSpec((LANES, NSA_WIDTH), lambda c, b: (0, 0)),
        ],
        out_specs=pl.BlockSpec((rows, NSA_WIDTH), qrow),
        out_shape=jax.ShapeDtypeStruct((batch * seq, NSA_WIDTH), BF16),
        compiler_params=pltpu.CompilerParams(
            dimension_semantics=("parallel", "arbitrary"), vmem_limit_bytes=VMEM_LIMIT),
        name="attn",
    )(q, ks, kw, vs0, vs1, vw0, vw1, kcmp, vc0, vc1, gt, ovl, gn)


def _conv_body(tiles_per_seq, cur_ref, halo_ref, w_ref, b_ref, lng_ref, lnb_ref, gn_ref, out_ref,
               buf_ref, sh_ref, cv_ref):
    first = (pl.program_id(0) % tiles_per_seq) == 0
    buf_ref[0:CONV_HALO, :] = jnp.where(first, 0.0, halo_ref[...])
    buf_ref[CONV_HALO:, :] = cur_ref[...]
    lead = CONV_HALO - (CONV_WIDTH - 1)
    ncb = CONV_CH // LANES
    nrb = CONV_TILE // CONV_ROWS
    for s in range(8):
        span = CONV_TILE + 8 * ((CONV_WIDTH - 1 - s) // 8)
        for cb in range(ncb):
            sh_ref[s, cb, 0:span, :] = buf_ref[lead + s:lead + s + span, cb * LANES:(cb + 1) * LANES]

    def conv_block(i, carry):
        cb = i % ncb
        base = pl.multiple_of((i // ncb) * CONV_ROWS, CONV_ROWS)
        acc = jnp.zeros((CONV_ROWS, LANES), F32) + b_ref[cb]
        for k in range(CONV_WIDTH):
            acc = acc + w_ref[cb, k:k + 1, :] * sh_ref[k % 8, cb, pl.ds(base + 8 * (k // 8), CONV_ROWS), :]
        cv_ref[cb, pl.ds(base, CONV_ROWS), :] = acc
        return carry

    lax.fori_loop(0, nrb * ncb, conv_block, 0)
    h = _lane_cat([cv_ref[cb] for cb in range(ncb)])
    mu = jnp.mean(h, axis=-1, keepdims=True)
    d = h - mu
    var = jnp.mean(d * d, axis=-1, keepdims=True)
    hn = d * lax.rsqrt(var + EPS) * lng_ref[...] + lnb_ref[...]
    o = hn * _sigmoid(hn)
    ms = jnp.mean(o * o, axis=-1, keepdims=True)
    out_ref[...] = (o * lax.rsqrt(ms + EPS) * gn_ref[...]).astype(BF16)


def _conv_call(hcv, w, b, lng, lnb, gn, seq):
    t = hcv.shape[0]
    nt = t // CONV_TILE
    halo_per_tile = CONV_TILE // CONV_HALO
    ncb = CONV_CH // LANES
    row = lambda i: (i, 0)
    const = lambda i: (0, 0)
    return pl.pallas_call(
        functools.partial(_conv_body, seq // CONV_TILE),
        grid=(nt,),
        in_specs=[
            pl.BlockSpec((CONV_TILE, CONV_CH), row),
            pl.BlockSpec((CONV_HALO, CONV_CH), lambda i: (jnp.maximum(i * halo_per_tile - 1, 0), 0)),
            pl.BlockSpec((ncb, CONV_HALO, LANES), lambda i: (0, 0, 0)),
            pl.BlockSpec((ncb, 1, LANES), lambda i: (0, 0, 0)),
            pl.BlockSpec((1, CONV_CH), const),
            pl.BlockSpec((1, CONV_CH), const),
            pl.BlockSpec((1, CONV_CH), const),
        ],
        out_specs=pl.BlockSpec((CONV_TILE, CONV_CH), row),
        out_shape=jax.ShapeDtypeStruct((t, CONV_CH), BF16),
        scratch_shapes=[pltpu.VMEM((CONV_HALO + CONV_TILE, CONV_CH), F32),
                        pltpu.VMEM((8, ncb, CONV_TILE + 8 * ((CONV_WIDTH - 1) // 8), LANES), F32),
                        pltpu.VMEM((ncb, CONV_TILE, LANES), F32)],
        compiler_params=pltpu.CompilerParams(
            dimension_semantics=("parallel",), vmem_limit_bytes=VMEM_LIMIT),
        name="conv",
    )(hcv, hcv, w, b, lng, lnb, gn)


def _ffn_body(x_ref, ma_ref, mb_ref, woa_ref, wob_ref, gf_ref, wgu_ref, wd_ref, out_ref):
    h = x_ref[...] + _dot(ma_ref[...], woa_ref[...]) + _dot(mb_ref[...], wob_ref[...])
    ms = jnp.mean(h * h, axis=-1, keepdims=True)
    hn = (h * lax.rsqrt(ms + EPS) * gf_ref[...]).astype(BF16)
    acc = jnp.zeros_like(h)
    for j in range(FFN_HIDDEN // FFN_CHUNK):
        c0 = j * FFN_CHUNK
        gte = _dot(hn, wgu_ref[:, c0:c0 + FFN_CHUNK])
        up = _dot(hn, wgu_ref[:, FFN_HIDDEN + c0:FFN_HIDDEN + c0 + FFN_CHUNK])
        act = (gte * _sigmoid(gte) * up).astype(BF16)
        acc = acc + _dot(act, wd_ref[c0:c0 + FFN_CHUNK, :])
    out_ref[...] = h + acc


def _ffn_call(x2, ma, mb, woa, wob, gf, wgu, wd):
    t = x2.shape[0]
    nt = t // FFN_TILE
    row = lambda i: (i, 0)
    const = lambda i: (0, 0)
    once = pl.Buffered(1)
    return pl.pallas_call(
        _ffn_body,
        grid=(nt,),
        in_specs=[
            pl.BlockSpec((FFN_TILE, D_MODEL), row),
            pl.BlockSpec((FFN_TILE, NSA_WIDTH), row),
            pl.BlockSpec((FFN_TILE, CONV_CH), row),
            pl.BlockSpec((NSA_WIDTH, D_MODEL), const, pipeline_mode=once),
            pl.BlockSpec((CONV_CH, D_MODEL), const, pipeline_mode=once),
            pl.BlockSpec((1, D_MODEL), const),
            pl.BlockSpec((D_MODEL, 2 * FFN_HIDDEN), const, pipeline_mode=once),
            pl.BlockSpec((FFN_HIDDEN, D_MODEL), const, pipeline_mode=once),
        ],
        out_specs=pl.BlockSpec((FFN_TILE, D_MODEL), row),
        out_shape=jax.ShapeDtypeStruct((t, D_MODEL), F32),
        compiler_params=pltpu.CompilerParams(
            dimension_semantics=("parallel",), vmem_limit_bytes=VMEM_LIMIT),
        name="ffn",
    )(x2, ma, mb, woa, wob, gf, wgu, wd)


def _pair_heads(a, axis):
    shape = a.shape
    a = a.reshape(shape[:axis] + (NSA_KV_HEADS, GQA_REP, HEAD_DIM) + shape[axis + 1:])
    return jnp.swapaxes(a, axis, axis + 1).reshape(shape)


def _transposed_gain(g):
    cols = g.reshape(GQA_REP, LANES).T
    return jnp.broadcast_to(cols[:, :, None], (LANES, GQA_REP, Q_TILE)).reshape(LANES, GQA_REP * Q_TILE)


def _rope_tables(pos):
    half = HEAD_DIM // 2
    inv = ROPE_THETA ** (-np.arange(half, dtype=np.float64) / half)
    ang = np.asarray(pos, np.float64)[:, None] * inv[None, :]
    cos = np.concatenate([np.cos(ang), np.cos(ang)], axis=1)
    sin = np.concatenate([-np.sin(ang), np.sin(ang)], axis=1)
    return (np.tile(cos, (1, NSA_KV_HEADS)).astype(np.float32),
            np.tile(sin, (1, NSA_KV_HEADS)).astype(np.float32))


def _block_diag_ones(width):
    idx = np.arange(width) // HEAD_DIM
    return (idx[:, None] == idx[None, :]).astype(np.float32)


def _overlap_t(seq):
    ncmp = (seq - CMP_LEN) // CMP_STRIDE + 1
    cs = np.arange(ncmp)[:, None] * CMP_STRIDE
    ss = np.arange(seq // SLC_BLOCK)[None, :] * SLC_BLOCK
    ov = np.clip(np.minimum(cs + CMP_LEN, ss + SLC_BLOCK) - np.maximum(cs, ss), 0, None) / CMP_LEN
    out = np.zeros((LANES, LANES), np.float32)
    out[:seq // SLC_BLOCK, :ncmp] = ov.T
    return out


def _cmp_weights(pe, w1, w2):
    pe2 = jnp.concatenate([pe, pe], axis=1)
    w1r = w1.reshape(CMP_LEN, HEAD_DIM, CMP_HIDDEN)
    z1 = jnp.zeros_like(w1r)
    w1b = jnp.concatenate([jnp.concatenate([w1r, z1], axis=2),
                           jnp.concatenate([z1, w1r], axis=2)], axis=1).astype(BF16)
    z2 = jnp.zeros_like(w2)
    w2b = jnp.concatenate([jnp.concatenate([w2, z2], axis=1),
                           jnp.concatenate([z2, w2], axis=1)], axis=0).astype(BF16)
    w1b = w1b.reshape(2, (CMP_LEN // 2) * LANES, 2 * CMP_HIDDEN)
    return pe2, w1b, w2b


def kernel(x, attn_norm_g, w_in, q_norm_g, k_norm_cmp_g, k_norm_slc_g, k_norm_win_g, cmp_pe_k, cmp_w1_k, cmp_w2_k, cmp_pe_v, cmp_w1_v, cmp_w2_v, conv_dw_w, conv_dw_b, conv_ln_g, conv_ln_b, out_norm_nsa_g, out_norm_conv_g, w_out, ffn_norm_g, w_gate_up, w_down):
    batch, seq, d_model = x.shape
    assert d_model == D_MODEL and seq % PROJ_TILE == 0 and seq // SLC_BLOCK == 32
    depth = w_in.shape[0]
    cos_np, sin_np = _rope_tables(np.arange(seq))
    ccos_np, csin_np = _rope_tables(np.arange(seq // CMP_STRIDE) * CMP_STRIDE + CMP_LEN - 1)
    ccos_np, csin_np = np.tile(ccos_np, (CMP_SEQS, 1)), np.tile(csin_np, (CMP_SEQS, 1))
    cos, sin, ccos, csin = map(jnp.asarray, (cos_np, sin_np, ccos_np, csin_np))
    bd256 = jnp.asarray(_block_diag_ones(256), BF16)
    bd128 = jnp.asarray(_block_diag_ones(LANES), BF16)
    ovl = jnp.asarray(_overlap_t(seq), BF16)

    x2 = x.reshape(batch * seq, d_model)
    for l in range(depth):
        w = w_in[l]
        w_main = w[:, :MAIN_COLS].astype(BF16)
        w_gate = jnp.pad(w[:, MAIN_COLS:MAIN_COLS + GATE_COLS],
                         ((0, 0), (0, LANES - GATE_COLS))).astype(BF16)
        w_conv = w[:, MAIN_COLS + GATE_COLS:].astype(BF16)
        gq = jnp.tile(q_norm_g[l], 4)[None, :]
        gk = jnp.concatenate([jnp.tile(k_norm_slc_g[l], 2), jnp.tile(k_norm_win_g[l], 2)])[None, :]
        q, ks, kw, vs0, vs1, vw0, vw1, kc, vc, gt, hcv = _proj_call(
            x2, attn_norm_g[l][None, :], w_main, w_conv, w_gate, cos, sin, gq, gk, bd256, seq)

        pek, w1k, w2k = _cmp_weights(cmp_pe_k[l], cmp_w1_k[l], cmp_w2_k[l])
        pev, w1v, w2v = _cmp_weights(cmp_pe_v[l], cmp_w1_v[l], cmp_w2_v[l])
        kcmp, vc0, vc1 = _cmp_call(kc, vc, pek, w1k, w2k, pev, w1v, w2v, ccos, csin,
                                jnp.tile(k_norm_cmp_g[l], 2)[None, :], bd128, batch, seq)

        mix_a = _attn_call(q, ks, kw, vs0, vs1, vw0, vw1, kcmp, vc0, vc1, gt, ovl,
                           _transposed_gain(_pair_heads(out_norm_nsa_g[l], 0)), batch, seq)

        w_dw = jnp.concatenate([conv_dw_w[l][:, 0, :],
                                jnp.zeros((CONV_HALO - CONV_WIDTH, CONV_CH), F32)], axis=0)
        w_dw = w_dw.reshape(CONV_HALO, CONV_CH // LANES, LANES).transpose(1, 0, 2)
        mix_b = _conv_call(hcv, w_dw, conv_dw_b[l].reshape(CONV_CH // LANES, 1, LANES),
                           conv_ln_g[l][None, :],
                           conv_ln_b[l][None, :], out_norm_conv_g[l][None, :], seq)

        wo = w_out[l]
        x2 = _ffn_call(x2, mix_a, mix_b, _pair_heads(wo[:NSA_WIDTH], 0).astype(BF16),
                       wo[NSA_WIDTH:].astype(BF16), ffn_norm_g[l][None, :],
                       w_gate_up[l].astype(BF16), w_down[l].astype(BF16))
    return x2.reshape(batch, seq, d_model)
```

```python
import functools
import types

import numpy as np
import jax
import jax.numpy as jnp
from jax import lax
from jax.experimental import pallas as pl
from jax.experimental.pallas import tpu as pltpu

F32 = jnp.float32
BF16 = jnp.bfloat16

D_MODEL = 1024
HEAD_DIM = 64
NSA_HEADS = 8
NSA_KV_HEADS = 2
GQA_REP = NSA_HEADS // NSA_KV_HEADS
NSA_WIDTH = NSA_HEADS * HEAD_DIM
KV_WIDTH = NSA_KV_HEADS * HEAD_DIM
CONV_CH = D_MODEL - NSA_WIDTH
CMP_LEN = 32
CMP_STRIDE = 16
CMP_HIDDEN = 4 * HEAD_DIM
SLC_BLOCK = 64
SLC_TOPN = 8
WINDOW = 512
CONV_WIDTH = 31
FFN_HIDDEN = 2816
ROPE_THETA = 10000.0
EPS = 1e-6
NEG = -1e30
LOG2E = 1.4426950408889634
FORCE = 1e6

LANES = 128
Q_TILE = 128
TILES_PER_STEP = 2
SLC_CHUNK = 256
WIN_KEYS = WINDOW + Q_TILE
PROJ_TILE = 512
CMP_SEQS = 4
CONV_TILE = 512
CONV_HALO = 32
CONV_ROWS = 128
FFN_TILE = 512
FFN_CHUNK = 256
VMEM_LIMIT = 56 * 1024 * 1024

C_Q = 0
C_CMP = 512
C_SLC = 768
C_WIN = 1024
MAIN_COLS = 1280
GATE_COLS = 3 * NSA_HEADS

_TRANS_B = (((1,), (1,)), ((), ()))


def _dot(a, b):
    return jnp.dot(a, b, preferred_element_type=F32)


def _dot_tb(a, b):
    return lax.dot_general(a, b, _TRANS_B, preferred_element_type=F32)


def _split_bf16(x):
    hi = x.astype(BF16)
    lo = (x - hi.astype(F32)).astype(BF16)
    return hi, lo


def _sigmoid(x):
    return 1.0 / (1.0 + jnp.exp(-x))


def _head_norm_rope(y, gain, bd, cos, sin, first_half, scale):
    hi, lo = _split_bf16(y * y)
    ss = _dot(hi, bd) + _dot(lo, bd)
    yn = y * lax.rsqrt(ss * (1.0 / HEAD_DIM) + EPS) * gain
    outs = []
    for j in range(y.shape[1] // LANES):
        blk = yn[:, j * LANES:(j + 1) * LANES]
        rot = jnp.where(first_half, pltpu.roll(blk, LANES - HEAD_DIM // 2, 1),
                        pltpu.roll(blk, HEAD_DIM // 2, 1))
        outs.append((blk * cos + rot * sin) * scale)
    return outs


def _value_t_with_ones(v):
    vt = v.T
    row = lax.broadcasted_iota(jnp.int32, vt.shape, 0)
    return (jnp.where(row < HEAD_DIM, vt, 1.0).astype(BF16),
            jnp.where(row < HEAD_DIM, 1.0, vt).astype(BF16))


def _proj_body(x_ref, gin_ref, wm_ref, wc_ref, wg_ref, cos_ref, sin_ref, gq_ref, gk_ref, bd_ref,
               wgu_in_ref, wd_in_ref, wo_in_ref,
               q_ref, ks_ref, kw_ref, vs0_ref, vs1_ref, vw0_ref, vw1_ref, kc_ref, vc_ref, gt_ref,
               hcv_ref, wgu_out_ref, wd_out_ref, wo_out_ref):
    wgu_out_ref[...] = wgu_in_ref[...].astype(BF16)
    wd_out_ref[...] = wd_in_ref[...].astype(BF16)
    wo_out_ref[...] = wo_in_ref[...].astype(BF16)

    x = x_ref[...]
    xg = (x * gin_ref[...]).astype(BF16)
    rs = lax.rsqrt(jnp.mean(x * x, axis=-1, keepdims=True) + EPS)
    cos = cos_ref[...]
    sin = sin_ref[...]
    bd = bd_ref[...]
    lane = lax.broadcasted_iota(jnp.int32, (PROJ_TILE, LANES), 1)
    first_half = (lane & (HEAD_DIM // 2)) == 0
    held = {}

    def queries(group, y):
        held["q", group] = _head_norm_rope(y, gq_ref[...], bd, cos, sin, first_half,
                                           HEAD_DIM ** -0.5 * LOG2E)
        if group == 0:
            return
        low = lane < HEAD_DIM
        for j in range(2):
            a, b = held["q", 0][j], held["q", 1][j]
            q_ref[:, (2 * j) * LANES:(2 * j + 1) * LANES] = jnp.where(
                low, a, pltpu.roll(b, HEAD_DIM, 1)).astype(BF16)
            q_ref[:, (2 * j + 1) * LANES:(2 * j + 2) * LANES] = jnp.where(
                low, pltpu.roll(a, HEAD_DIM, 1), b).astype(BF16)

    def values(y, out0_ref, out1_ref):
        for i in range(PROJ_TILE // LANES):
            out0_ref[i], out1_ref[i] = _value_t_with_ones(y[i * LANES:(i + 1) * LANES, :])

    def slc_kv(y):
        held["ks"] = y[:, :LANES]
        values(y[:, LANES:], vs0_ref, vs1_ref)

    def win_kv(y):
        kblks = _head_norm_rope(_lane_cat([held["ks"], y[:, :LANES]]), gk_ref[...], bd, cos, sin,
                                first_half, 1.0)
        ks_ref[...] = kblks[0].astype(BF16)
        kw_ref[...] = kblks[1].astype(BF16)
        values(y[:, LANES:], vw0_ref, vw1_ref)

    def cmp_inputs(y):
        kc_ref[...] = y[:, :LANES]
        vc_ref[...] = y[:, LANES:]

    def conv_lin(y):
        held["ca"] = y

    def conv_glu(y):
        hcv_ref[...] = held["ca"] * _sigmoid(y)

    def gates(y):
        sg = _sigmoid(y)
        for i in range(PROJ_TILE // LANES):
            gt_ref[i] = sg[i * LANES:(i + 1) * LANES, :].T[0:32, :]

    sections = [
        (wm_ref, C_Q, 256, functools.partial(queries, 0)),
        (wm_ref, C_Q + 256, 256, functools.partial(queries, 1)),
        (wc_ref, 0, CONV_CH, conv_lin),
        (wm_ref, C_SLC, 256, slc_kv),
        (wc_ref, CONV_CH, CONV_CH, conv_glu),
        (wm_ref, C_WIN, 256, win_kv),
        (wm_ref, C_CMP, 256, cmp_inputs),
        (wg_ref, 0, LANES, gates),
    ]

    def project(sec):
        w_ref, c0, width, _ = sec
        return _dot(xg, w_ref[:, c0:c0 + width])

    y_next = project(sections[0])
    for i, sec in enumerate(sections):
        y_cur = y_next
        if i + 1 < len(sections):
            y_next = project(sections[i + 1])
        sec[3](y_cur * rs)


def _paired_row_block(i):
    head = i // 2
    src = ((head % NSA_KV_HEADS) * GQA_REP + head // NSA_KV_HEADS) * 2 + i % 2
    return jnp.where(i < NSA_WIDTH // 32, src, i)


def _proj_call(x2, gin, w_main, w_conv, w_gate, cos, sin, gq, gk, bd, w_gate_up, w_down, w_out, seq):
    t = x2.shape[0]
    nt = t // PROJ_TILE
    assert D_MODEL // nt == 32 and nt % 16 == 0 and (FFN_HIDDEN // 16) % 16 == 0
    wd_rows = FFN_HIDDEN // 16
    tiles_per_seq = seq // PROJ_TILE
    row = lambda i: (i, 0)
    const = lambda i: (0, 0)
    tab = lambda i: (i % tiles_per_seq, 0)
    blk3 = lambda i: (i, 0, 0)
    wd_blk = lambda i: (i * 16 // nt, 0)
    nb = PROJ_TILE // LANES
    return pl.pallas_call(
        _proj_body,
        grid=(nt,),
        in_specs=[
            pl.BlockSpec((PROJ_TILE, D_MODEL), row),
            pl.BlockSpec((1, D_MODEL), const),
            pl.BlockSpec((D_MODEL, MAIN_COLS), const),
            pl.BlockSpec((D_MODEL, 2 * CONV_CH), const),
            pl.BlockSpec((D_MODEL, LANES), const),
            pl.BlockSpec((PROJ_TILE, LANES), tab),
            pl.BlockSpec((PROJ_TILE, LANES), tab),
            pl.BlockSpec((1, 256), const),
            pl.BlockSpec((1, 256), const),
            pl.BlockSpec((256, 256), const),
            pl.BlockSpec((32, 2 * FFN_HIDDEN), row),
            pl.BlockSpec((wd_rows, D_MODEL), wd_blk),
            pl.BlockSpec((32, D_MODEL), lambda i: (_paired_row_block(i), 0)),
        ],
        out_specs=[
            pl.BlockSpec((PROJ_TILE, NSA_WIDTH), row),
            pl.BlockSpec((PROJ_TILE, LANES), row),
            pl.BlockSpec((PROJ_TILE, LANES), row),
            pl.BlockSpec((nb, LANES, LANES), blk3),
            pl.BlockSpec((nb, LANES, LANES), blk3),
            pl.BlockSpec((nb, LANES, LANES), blk3),
            pl.BlockSpec((nb, LANES, LANES), blk3),
            pl.BlockSpec((PROJ_TILE, LANES), row),
            pl.BlockSpec((PROJ_TILE, LANES), row),
            pl.BlockSpec((nb, 32, LANES), blk3),
            pl.BlockSpec((PROJ_TILE, CONV_CH), row),
            pl.BlockSpec((32, 2 * FFN_HIDDEN), row),
            pl.BlockSpec((wd_rows, D_MODEL), wd_blk),
            pl.BlockSpec((32, D_MODEL), row),
        ],
        out_shape=[
            jax.ShapeDtypeStruct((t, NSA_WIDTH), BF16),
            jax.ShapeDtypeStruct((t, LANES), BF16),
            jax.ShapeDtypeStruct((t, LANES), BF16),
            jax.ShapeDtypeStruct((t // LANES, LANES, LANES), BF16),
            jax.ShapeDtypeStruct((t // LANES, LANES, LANES), BF16),
            jax.ShapeDtypeStruct((t // LANES, LANES, LANES), BF16),
            jax.ShapeDtypeStruct((t // LANES, LANES, LANES), BF16),
            jax.ShapeDtypeStruct((t, LANES), F32),
            jax.ShapeDtypeStruct((t, LANES), F32),
            jax.ShapeDtypeStruct((t // LANES, 32, LANES), F32),
            jax.ShapeDtypeStruct((t, CONV_CH), F32),
            jax.ShapeDtypeStruct((D_MODEL, 2 * FFN_HIDDEN), BF16),
            jax.ShapeDtypeStruct((FFN_HIDDEN, D_MODEL), BF16),
            jax.ShapeDtypeStruct((D_MODEL, D_MODEL), BF16),
        ],
        compiler_params=pltpu.CompilerParams(
            dimension_semantics=("arbitrary",), vmem_limit_bytes=VMEM_LIMIT),
        name="proj",
    )(x2, gin, w_main, w_conv, w_gate, cos, sin, gq, gk, bd, w_gate_up, w_down, w_out)


def _cmp_mlp(t_ref, pe_ref, w1_ref, w2_ref):
    half = CMP_LEN // 2
    nchunk = t_ref.shape[0] // CMP_STRIDE
    toks = [t_ref[pl.ds(l, nchunk, stride=CMP_STRIDE), :] for l in range(half)]
    xp = _lane_cat([(toks[l] + pe_ref[l:l + 1, :]).astype(BF16) for l in range(half)])
    xq = _lane_cat([(toks[l] + pe_ref[half + l:half + l + 1, :]).astype(BF16) for l in range(half)])
    h1 = _dot(xp, w1_ref[0]) + pltpu.roll(_dot(xq, w1_ref[1]), nchunk - 1, 0)
    act = (h1 * _sigmoid(h1)).astype(BF16)
    return _dot(act, w2_ref[...])


def _cmp_body(kc_ref, vc_ref, pek_ref, w1k_ref, w2k_ref, pev_ref, w1v_ref, w2v_ref,
              cos_ref, sin_ref, gk_ref, bd_ref, ko_ref, vt0_ref, vt1_ref):
    kraw = _cmp_mlp(kc_ref, pek_ref, w1k_ref, w2k_ref)
    lane = lax.broadcasted_iota(jnp.int32, kraw.shape, 1)
    first_half = (lane & (HEAD_DIM // 2)) == 0
    (kr,) = _head_norm_rope(kraw, gk_ref[...], bd_ref[...], cos_ref[...], sin_ref[...],
                            first_half, 1.0)
    ko_ref[...] = kr.astype(BF16)
    vraw = _cmp_mlp(vc_ref, pev_ref, w1v_ref, w2v_ref)
    for i in range(CMP_SEQS):
        vt0_ref[i], vt1_ref[i] = _value_t_with_ones(vraw[i * LANES:(i + 1) * LANES, :])


def _cmp_call(kc, vc, pek, w1k, w2k, pev, w1v, w2v, cos, sin, gk, bd, batch, seq):
    seqblk = lambda b: (b, 0)
    c2 = lambda b: (0, 0)
    c3 = lambda b: (0, 0, 0)
    ncmp_pad = seq // CMP_STRIDE
    assert ncmp_pad == LANES and batch % CMP_SEQS == 0
    rows = CMP_SEQS * ncmp_pad
    half_k = (CMP_LEN // 2) * LANES
    return pl.pallas_call(
        _cmp_body,
        grid=(batch // CMP_SEQS,),
        in_specs=[
            pl.BlockSpec((CMP_SEQS * seq, LANES), seqblk),
            pl.BlockSpec((CMP_SEQS * seq, LANES), seqblk),
            pl.BlockSpec((CMP_LEN, LANES), c2),
            pl.BlockSpec((2, half_k, 2 * CMP_HIDDEN), c3),
            pl.BlockSpec((2 * CMP_HIDDEN, LANES), c2),
            pl.BlockSpec((CMP_LEN, LANES), c2),
            pl.BlockSpec((2, half_k, 2 * CMP_HIDDEN), c3),
            pl.BlockSpec((2 * CMP_HIDDEN, LANES), c2),
            pl.BlockSpec((rows, LANES), c2),
            pl.BlockSpec((rows, LANES), c2),
            pl.BlockSpec((1, LANES), c2),
            pl.BlockSpec((LANES, LANES), c2),
        ],
        out_specs=[
            pl.BlockSpec((rows, LANES), seqblk),
            pl.BlockSpec((CMP_SEQS, LANES, ncmp_pad), lambda b: (b, 0, 0)),
            pl.BlockSpec((CMP_SEQS, LANES, ncmp_pad), lambda b: (b, 0, 0)),
        ],
        out_shape=[
            jax.ShapeDtypeStruct((batch * ncmp_pad, LANES), BF16),
            jax.ShapeDtypeStruct((batch, LANES, ncmp_pad), BF16),
            jax.ShapeDtypeStruct((batch, LANES, ncmp_pad), BF16),
        ],
        compiler_params=pltpu.CompilerParams(
            dimension_semantics=("parallel",), vmem_limit_bytes=VMEM_LIMIT),
        name="compress",
    )(kc, vc, pek, w1k, w2k, pev, w1v, w2v, cos, sin, gk, bd)


def _lane_cat(blocks):
    return jnp.concatenate(blocks, axis=1)


def _softmax_numer(s, bias, m_run=None):
    es, ms = [], []
    for r in range(GQA_REP):
        sm = s[:, r * LANES:(r + 1) * LANES] + bias
        m = jnp.max(sm, axis=0, keepdims=True)
        if m_run is not None:
            m = jnp.maximum(m, m_run[:, r * LANES:(r + 1) * LANES])
        es.append(jnp.exp2(sm - m))
        ms.append(m)
    return _lane_cat(es), _lane_cat(ms)


def _safe_inv(l):
    return 1.0 / jnp.where(l > 0.0, l, 1.0)


def _attn_body(*refs):
    first = pl.program_id(0) * TILES_PER_STEP
    n_max = refs[1].shape[0] // SLC_CHUNK
    for n in range(1, n_max + 1):
        pl.when((first * Q_TILE) // SLC_CHUNK + 1 == n)(functools.partial(_attn_step, n, first, *refs))


def _attn_step(n_chunks, first, *refs):
    def one_tile(slot, carry):
        tile = _tile_program(n_chunks, first + slot, slot, *refs)
        pending = None
        s_next = tile.scores(tile.units[0])
        for i, unit in enumerate(tile.units):
            s_cur = s_next
            if i + 1 < len(tile.units):
                s_next = tile.scores(tile.units[i + 1])
            e, alpha = tile.softmax(unit, s_cur)
            if pending is not None:
                tile.weighted_values(*pending)
            pending = (unit, e, alpha)
        tile.weighted_values(*pending)
        tile.finish()
        return carry

    lax.fori_loop(0, TILES_PER_STEP, one_tile, 0)


def _tile_program(n_chunks, c, slot, q_ref, ks_ref, kw_ref, vs0_ref, vs1_ref, vw0_ref, vw1_ref, kc_ref,
                  vc0_ref, vc1_ref, gt_ref, ovl_ref, gn_ref, out_ref):
    groups = range(NSA_KV_HEADS)
    q_rows = pl.ds(pl.multiple_of(slot * Q_TILE, Q_TILE), Q_TILE)
    vs_refs, vw_refs, vc_refs = (vs0_ref, vs1_ref), (vw0_ref, vw1_ref), (vc0_ref, vc1_ref)
    sum_row = (HEAD_DIM, 0)
    width = GQA_REP * LANES
    t_lane = c * Q_TILE + lax.broadcasted_iota(jnp.int32, (1, LANES), 1)
    lane_sq = lax.broadcasted_iota(jnp.int32, (Q_TILE, LANES), 1)
    n_live = n_chunks * (SLC_CHUNK // SLC_BLOCK)
    nsel = 32
    win_keys = min(WIN_KEYS, n_chunks * SLC_CHUNK)
    j_idx = lax.broadcasted_iota(jnp.int32, (nsel, LANES), 0)
    cur = lax.shift_right_logical(t_lane, 6)
    n_idx = lax.broadcasted_iota(jnp.int32, (LANES, LANES), 0)
    cbias = jnp.where((n_idx * CMP_STRIDE + (CMP_LEN - 1) <= t_lane) & (n_idx < LANES - 1), 0.0, NEG)
    has_cmp = jnp.where(t_lane >= CMP_LEN - 1, 1.0, 0.0)
    has_cmp = _lane_cat([has_cmp] * GQA_REP)
    win_blk = jnp.maximum(c - WINDOW // Q_TILE, 0)
    win_start = pl.multiple_of(win_blk * Q_TILE, Q_TILE)
    kpos_w = win_start + lax.broadcasted_iota(jnp.int32, (win_keys, LANES), 0)
    wbias = jnp.where((kpos_w <= t_lane) & (kpos_w > t_lane - WINDOW), 0.0, NEG)
    row_sl = lax.broadcasted_iota(jnp.int32, (SLC_CHUNK, LANES), 0)
    zero_q = jnp.zeros((Q_TILE, LANES), BF16)

    def gate_row(g, branch):
        rows = [(g * GQA_REP + r) * 3 + branch for r in range(GQA_REP)]
        return _lane_cat([gt_ref[slot, i:i + 1, :] for i in rows])

    qg = []
    for g in groups:
        in_g = (lane_sq >= HEAD_DIM) if g else (lane_sq < HEAD_DIM)
        qg.append(jnp.concatenate(
            [jnp.where(in_g, q_ref[q_rows, r * LANES:(r + 1) * LANES], zero_q) for r in range(GQA_REP)],
            axis=0))

    o_cmp = [None] * NSA_KV_HEADS
    sel_bias = [None] * NSA_KV_HEADS

    def compressed_softmax(g, sc):
        ec, _ = _softmax_numer(sc, cbias)
        coef = has_cmp * _safe_inv(jnp.sum(ec, axis=0, keepdims=True))
        pc = ec * coef
        psum = (pc[:, 0:LANES] + pc[:, LANES:2 * LANES]
                + pc[:, 2 * LANES:3 * LANES] + pc[:, 3 * LANES:4 * LANES])
        p_hi, p_lo = _split_bf16(psum)
        imp = (_dot(ovl_ref[...], p_hi) + _dot(ovl_ref[...], p_lo))[0:nsel, :]
        valid = j_idx <= cur
        forced = (j_idx == 0) | (j_idx == cur) | (j_idx == cur - 1)
        score = jnp.where(valid, imp + jnp.where(forced, FORCE, 0.0), -FORCE)
        rank = jnp.zeros((nsel, LANES), F32)
        for jp in range(nsel):
            row = score[jp:jp + 1, :]
            beats = (row > score) | ((row == score) & (j_idx > jp))
            rank = rank + jnp.where(beats, 1.0, 0.0)
        sel_bias[g] = jnp.where((rank < float(SLC_TOPN)) & (score > -1.0), 0.0, NEG)
        return ec.astype(BF16), coef

    nsub = SLC_CHUNK // LANES
    blocks_per_chunk = SLC_CHUNK // SLC_BLOCK
    kwin = kw_ref[pl.ds(win_start, win_keys), :]
    units = [("cmp", g, 0) for g in groups] + [("win", g, 0) for g in groups]
    units += [("slc", g, kc) for kc in range(n_chunks) for g in groups]
    m_run = [None] * NSA_KV_HEADS
    acc_s = [None] * NSA_KV_HEADS
    acc_w = [None] * NSA_KV_HEADS

    def scores(unit):
        kind, g, kc = unit
        if kind == "cmp":
            keys = kc_ref[...]
        elif kind == "win":
            keys = kwin
        else:
            keys = ks_ref[kc * SLC_CHUNK:(kc + 1) * SLC_CHUNK, :]
        return _dot_tb(keys, qg[g])

    def softmax(unit, s):
        kind, g, kc = unit
        if kind == "cmp":
            return compressed_softmax(g, s)
        if kind == "win":
            e, _ = _softmax_numer(s, wbias)
            return e.astype(BF16), None
        j0 = kc * blocks_per_chunk
        bias = jnp.concatenate(
            [jnp.broadcast_to(sel_bias[g][j:j + 1, :], (SLC_BLOCK, LANES))
             for j in range(j0, j0 + blocks_per_chunk)], axis=0)
        if kc == n_chunks - 1:
            bias = jnp.where(kc * SLC_CHUNK + row_sl <= t_lane, bias, NEG)
        e, m_new = _softmax_numer(s, bias, m_run[g])
        alpha = None if kc == 0 else jnp.exp2(m_run[g] - m_new)
        m_run[g] = m_new
        return e.astype(BF16), alpha

    def weighted_values(unit, e, alpha):
        kind, g, kc = unit
        if kind == "cmp":
            o_cmp[g] = _dot(vc_refs[g][0], e) * (gate_row(g, 0) * alpha)
            return
        if kind == "win":
            vt = _lane_cat([vw_refs[g][win_blk + i] for i in range(win_keys // LANES)])
            acc_w[g] = _dot(vt, e)
            return
        vt = _lane_cat([vs_refs[g][nsub * kc + i] for i in range(nsub)])
        pv = _dot(vt, e)
        acc_s[g] = pv if alpha is None else acc_s[g] * alpha + pv

    def finish():
        o_groups = []
        for g in groups:
            r0 = sum_row[g]
            o_groups.append(o_cmp[g]
                            + acc_s[g] * (gate_row(g, 1) * _safe_inv(acc_s[g][r0:r0 + 1, :]))
                            + acc_w[g] * (gate_row(g, 2) * _safe_inv(acc_w[g][r0:r0 + 1, :])))
        row_d = lax.broadcasted_iota(jnp.int32, (LANES, width), 0)
        o_t = jnp.where(row_d < HEAD_DIM, o_groups[0], o_groups[1])
        sq = jnp.sum(o_t * o_t, axis=0, keepdims=True)
        ss = (sq[:, 0:LANES] + sq[:, LANES:2 * LANES]
              + sq[:, 2 * LANES:3 * LANES] + sq[:, 3 * LANES:4 * LANES])
        rs = lax.rsqrt(ss * (1.0 / NSA_WIDTH) + EPS)
        for r in range(GQA_REP):
            cols = slice(r * LANES, (r + 1) * LANES)
            out_ref[q_rows, cols] = (o_t[:, cols] * rs * gn_ref[:, cols]).T.astype(BF16)

    return types.SimpleNamespace(units=units, scores=scores, softmax=softmax,
                                 weighted_values=weighted_values, finish=finish)


def _attn_call(q, ks, kw, vs0, vs1, vw0, vw1, kcmp, vc0, vc1, gt, ovl, gn, batch, seq):
    assert SLC_CHUNK % (TILES_PER_STEP * Q_TILE) == 0
    rows = TILES_PER_STEP * Q_TILE
    nq = seq // rows
    nkb = seq // LANES
    qrow = lambda c, b: (b * nq + c, 0)
    per_b = lambda c, b: (b, 0)
    per_b3 = lambda c, b: (b, 0, 0)
    vspec = pl.BlockSpec((nkb, LANES, LANES), per_b3)
    cspec = pl.BlockSpec((1, LANES, LANES), per_b3)
    return pl.pallas_call(
        _attn_body,
        grid=(nq, batch),
        in_specs=[
            pl.BlockSpec((rows, NSA_WIDTH), qrow),
            pl.BlockSpec((seq, LANES), per_b),
            pl.BlockSpec((seq, LANES), per_b),
            vspec, vspec, vspec, vspec,
            pl.BlockSpec((LANES, LANES), per_b),
            cspec, cspec,
            pl.BlockSpec((TILES_PER_STEP, 32, LANES), lambda c, b: (b * nq + c, 0, 0)),
            pl.BlockSpec((LANES, LANES), lambda c, b: (0, 0)),
            pl.BlockSpec((LANES, NSA_WIDTH), lambda c, b: (0, 0)),
        ],
        out_specs=pl.BlockSpec((rows, NSA_WIDTH), qrow),
        out_shape=jax.ShapeDtypeStruct((batch * seq, NSA_WIDTH), BF16),
        compiler_params=pltpu.CompilerParams(
            dimension_semantics=("parallel", "arbitrary"), vmem_limit_bytes=VMEM_LIMIT),
        name="attn",
    )(q, ks, kw, vs0, vs1, vw0, vw1, kcmp, vc0, vc1, gt, ovl, gn)


def _conv_body(tiles_per_seq, cur_ref, halo_ref, w_ref, b_ref, lng_ref, lnb_ref, gn_ref, out_ref,
               buf_ref, sh_ref, cv_ref):
    first = (pl.program_id(0) % tiles_per_seq) == 0
    buf_ref[0:CONV_HALO, :] = jnp.where(first, 0.0, halo_ref[...])
    buf_ref[CONV_HALO:, :] = cur_ref[...]
    lead = CONV_HALO - (CONV_WIDTH - 1)
    ncb = CONV_CH // LANES
    nrb = CONV_TILE // CONV_ROWS
    for s in range(8):
        span = CONV_TILE + 8 * ((CONV_WIDTH - 1 - s) // 8)
        for cb in range(ncb):
            sh_ref[s, cb, 0:span, :] = buf_ref[lead + s:lead + s + span, cb * LANES:(cb + 1) * LANES]

    def conv_block(i, carry):
        cb = i % ncb
        base = pl.multiple_of((i // ncb) * CONV_ROWS, CONV_ROWS)
        acc = jnp.zeros((CONV_ROWS, LANES), F32) + b_ref[cb]
        for k in range(CONV_WIDTH):
            acc = acc + w_ref[cb, k:k + 1, :] * sh_ref[k % 8, cb, pl.ds(base + 8 * (k // 8), CONV_ROWS), :]
        cv_ref[cb, pl.ds(base, CONV_ROWS), :] = acc
        return carry

    lax.fori_loop(0, nrb * ncb, conv_block, 0)
    h = _lane_cat([cv_ref[cb] for cb in range(ncb)])
    mu = jnp.mean(h, axis=-1, keepdims=True)
    d = h - mu
    var = jnp.mean(d * d, axis=-1, keepdims=True)
    hn = d * lax.rsqrt(var + EPS) * lng_ref[...] + lnb_ref[...]
    o = hn * _sigmoid(hn)
    ms = jnp.mean(o * o, axis=-1, keepdims=True)
    out_ref[...] = (o * lax.rsqrt(ms + EPS) * gn_ref[...]).astype(BF16)


def _conv_call(hcv, w, b, lng, lnb, gn, seq):
    t = hcv.shape[0]
    nt = t // CONV_TILE
    halo_per_tile = CONV_TILE // CONV_HALO
    ncb = CONV_CH // LANES
    row = lambda i: (i, 0)
    const = lambda i: (0, 0)
    return pl.pallas_call(
        functools.partial(_conv_body, seq // CONV_TILE),
        grid=(nt,),
        in_specs=[
            pl.BlockSpec((CONV_TILE, CONV_CH), row),
            pl.BlockSpec((CONV_HALO, CONV_CH), lambda i: (jnp.maximum(i * halo_per_tile - 1, 0), 0)),
            pl.BlockSpec((ncb, CONV_HALO, LANES), lambda i: (0, 0, 0)),
            pl.BlockSpec((ncb, 1, LANES), lambda i: (0, 0, 0)),
            pl.BlockSpec((1, CONV_CH), const),
            pl.BlockSpec((1, CONV_CH), const),
            pl.BlockSpec((1, CONV_CH), const),
        ],
        out_specs=pl.BlockSpec((CONV_TILE, CONV_CH), row),
        out_shape=jax.ShapeDtypeStruct((t, CONV_CH), BF16),
        scratch_shapes=[pltpu.VMEM((CONV_HALO + CONV_TILE, CONV_CH), F32),
                        pltpu.VMEM((8, ncb, CONV_TILE + 8 * ((CONV_WIDTH - 1) // 8), LANES), F32),
                        pltpu.VMEM((ncb, CONV_TILE, LANES), F32)],
        compiler_params=pltpu.CompilerParams(
            dimension_semantics=("parallel",), vmem_limit_bytes=VMEM_LIMIT),
        name="conv",
    )(hcv, hcv, w, b, lng, lnb, gn)


def _ffn_body(x_ref, ma_ref, mb_ref, wo_ref, gf_ref, wgu_ref, wd_ref, out_ref):
    h = (x_ref[...] + _dot(ma_ref[...], wo_ref[0:NSA_WIDTH, :])
         + _dot(mb_ref[...], wo_ref[NSA_WIDTH:D_MODEL, :]))
    ms = jnp.mean(h * h, axis=-1, keepdims=True)
    hn = (h * lax.rsqrt(ms + EPS) * gf_ref[...]).astype(BF16)
    acc = jnp.zeros_like(h)
    for j in range(FFN_HIDDEN // FFN_CHUNK):
        c0 = j * FFN_CHUNK
        gte = _dot(hn, wgu_ref[:, c0:c0 + FFN_CHUNK])
        up = _dot(hn, wgu_ref[:, FFN_HIDDEN + c0:FFN_HIDDEN + c0 + FFN_CHUNK])
        act = (gte * _sigmoid(gte) * up).astype(BF16)
        acc = acc + _dot(act, wd_ref[c0:c0 + FFN_CHUNK, :])
    out_ref[...] = h + acc


def _ffn_call(x2, ma, mb, wo, gf, wgu, wd):
    t = x2.shape[0]
    nt = t // FFN_TILE
    row = lambda i: (i, 0)
    const = lambda i: (0, 0)
    once = pl.Buffered(1)
    return pl.pallas_call(
        _ffn_body,
        grid=(nt,),
        in_specs=[
            pl.BlockSpec((FFN_TILE, D_MODEL), row),
            pl.BlockSpec((FFN_TILE, NSA_WIDTH), row),
            pl.BlockSpec((FFN_TILE, CONV_CH), row),
            pl.BlockSpec((D_MODEL, D_MODEL), const, pipeline_mode=once),
            pl.BlockSpec((1, D_MODEL), const),
            pl.BlockSpec((D_MODEL, 2 * FFN_HIDDEN), const, pipeline_mode=once),
            pl.BlockSpec((FFN_HIDDEN, D_MODEL), const, pipeline_mode=once),
        ],
        out_specs=pl.BlockSpec((FFN_TILE, D_MODEL), row),
        out_shape=jax.ShapeDtypeStruct((t, D_MODEL), F32),
        compiler_params=pltpu.CompilerParams(
            dimension_semantics=("parallel",), vmem_limit_bytes=VMEM_LIMIT),
        name="ffn",
    )(x2, ma, mb, wo, gf, wgu, wd)


def _pair_heads(a, axis):
    shape = a.shape
    a = a.reshape(shape[:axis] + (NSA_KV_HEADS, GQA_REP, HEAD_DIM) + shape[axis + 1:])
    return jnp.swapaxes(a, axis, axis + 1).reshape(shape)


def _transposed_gain(g):
    cols = g.reshape(GQA_REP, LANES).T
    return jnp.broadcast_to(cols[:, :, None], (LANES, GQA_REP, Q_TILE)).reshape(LANES, GQA_REP * Q_TILE)


def _rope_tables(pos):
    half = HEAD_DIM // 2
    inv = ROPE_THETA ** (-np.arange(half, dtype=np.float64) / half)
    ang = np.asarray(pos, np.float64)[:, None] * inv[None, :]
    cos = np.concatenate([np.cos(ang), np.cos(ang)], axis=1)
    sin = np.concatenate([-np.sin(ang), np.sin(ang)], axis=1)
    return (np.tile(cos, (1, NSA_KV_HEADS)).astype(np.float32),
            np.tile(sin, (1, NSA_KV_HEADS)).astype(np.float32))


def _block_diag_ones(width):
    idx = np.arange(width) // HEAD_DIM
    return (idx[:, None] == idx[None, :]).astype(np.float32)


def _overlap_t(seq):
    ncmp = (seq - CMP_LEN) // CMP_STRIDE + 1
    cs = np.arange(ncmp)[:, None] * CMP_STRIDE
    ss = np.arange(seq // SLC_BLOCK)[None, :] * SLC_BLOCK
    ov = np.clip(np.minimum(cs + CMP_LEN, ss + SLC_BLOCK) - np.maximum(cs, ss), 0, None) / CMP_LEN
    out = np.zeros((LANES, LANES), np.float32)
    out[:seq // SLC_BLOCK, :ncmp] = ov.T
    return out


def _cmp_weights(pe, w1, w2):
    pe2 = jnp.concatenate([pe, pe], axis=1)
    w1r = w1.reshape(CMP_LEN, HEAD_DIM, CMP_HIDDEN)
    z1 = jnp.zeros_like(w1r)
    w1b = jnp.concatenate([jnp.concatenate([w1r, z1], axis=2),
                           jnp.concatenate([z1, w1r], axis=2)], axis=1).astype(BF16)
    z2 = jnp.zeros_like(w2)
    w2b = jnp.concatenate([jnp.concatenate([w2, z2], axis=1),
                           jnp.concatenate([z2, w2], axis=1)], axis=0).astype(BF16)
    w1b = w1b.reshape(2, (CMP_LEN // 2) * LANES, 2 * CMP_HIDDEN)
    return pe2, w1b, w2b


def kernel(x, attn_norm_g, w_in, q_norm_g, k_norm_cmp_g, k_norm_slc_g, k_norm_win_g, cmp_pe_k, cmp_w1_k, cmp_w2_k, cmp_pe_v, cmp_w1_v, cmp_w2_v, conv_dw_w, conv_dw_b, conv_ln_g, conv_ln_b, out_norm_nsa_g, out_norm_conv_g, w_out, ffn_norm_g, w_gate_up, w_down):
    batch, seq, d_model = x.shape
    assert d_model == D_MODEL and seq % PROJ_TILE == 0 and seq // SLC_BLOCK == 32
    depth = w_in.shape[0]
    cos_np, sin_np = _rope_tables(np.arange(seq))
    ccos_np, csin_np = _rope_tables(np.arange(seq // CMP_STRIDE) * CMP_STRIDE + CMP_LEN - 1)
    ccos_np, csin_np = np.tile(ccos_np, (CMP_SEQS, 1)), np.tile(csin_np, (CMP_SEQS, 1))
    cos, sin, ccos, csin = map(jnp.asarray, (cos_np, sin_np, ccos_np, csin_np))
    bd256 = jnp.asarray(_block_diag_ones(256), BF16)
    bd128 = jnp.asarray(_block_diag_ones(LANES), BF16)
    ovl = jnp.asarray(_overlap_t(seq), BF16)

    x2 = x.reshape(batch * seq, d_model)
    for l in range(depth):
        w = w_in[l]
        w_main = w[:, :MAIN_COLS].astype(BF16)
        w_gate = jnp.pad(w[:, MAIN_COLS:MAIN_COLS + GATE_COLS],
                         ((0, 0), (0, LANES - GATE_COLS))).astype(BF16)
        w_conv = w[:, MAIN_COLS + GATE_COLS:].astype(BF16)
        gq = jnp.tile(q_norm_g[l], 4)[None, :]
        gk = jnp.concatenate([jnp.tile(k_norm_slc_g[l], 2), jnp.tile(k_norm_win_g[l], 2)])[None, :]
        q, ks, kw, vs0, vs1, vw0, vw1, kc, vc, gt, hcv, wgu_b, wd_b, wo_b = _proj_call(
            x2, attn_norm_g[l][None, :], w_main, w_conv, w_gate, cos, sin, gq, gk, bd256,
            w_gate_up[l], w_down[l], w_out[l], seq)

        pek, w1k, w2k = _cmp_weights(cmp_pe_k[l], cmp_w1_k[l], cmp_w2_k[l])
        pev, w1v, w2v = _cmp_weights(cmp_pe_v[l], cmp_w1_v[l], cmp_w2_v[l])
        kcmp, vc0, vc1 = _cmp_call(kc, vc, pek, w1k, w2k, pev, w1v, w2v, ccos, csin,
                                jnp.tile(k_norm_cmp_g[l], 2)[None, :], bd128, batch, seq)

        mix_a = _attn_call(q, ks, kw, vs0, vs1, vw0, vw1, kcmp, vc0, vc1, gt, ovl,
                           _transposed_gain(_pair_heads(out_norm_nsa_g[l], 0)), batch, seq)

        w_dw = jnp.concatenate([conv_dw_w[l][:, 0, :],
                                jnp.zeros((CONV_HALO - CONV_WIDTH, CONV_CH), F32)], axis=0)
        w_dw = w_dw.reshape(CONV_HALO, CONV_CH // LANES, LANES).transpose(1, 0, 2)
        mix_b = _conv_call(hcv, w_dw, conv_dw_b[l].reshape(CONV_CH // LANES, 1, LANES),
                           conv_ln_g[l][None, :],
                           conv_ln_b[l][None, :], out_norm_conv_g[l][None, :], seq)

        x2 = _ffn_call(x2, mix_a, mix_b, wo_b, ffn_norm_g[l][None, :], wgu_b, wd_b)
    return x2.reshape(batch, seq, d_model)
```

```python
import functools
import types

import numpy as np
import jax
import jax.numpy as jnp
from jax import lax
from jax.experimental import pallas as pl
from jax.experimental.pallas import tpu as pltpu

F32 = jnp.float32
BF16 = jnp.bfloat16

D_MODEL = 1024
HEAD_DIM = 64
NSA_HEADS = 8
NSA_KV_HEADS = 2
GQA_REP = NSA_HEADS // NSA_KV_HEADS
NSA_WIDTH = NSA_HEADS * HEAD_DIM
KV_WIDTH = NSA_KV_HEADS * HEAD_DIM
CONV_CH = D_MODEL - NSA_WIDTH
CMP_LEN = 32
CMP_STRIDE = 16
CMP_HIDDEN = 4 * HEAD_DIM
SLC_BLOCK = 64
SLC_TOPN = 8
WINDOW = 512
CONV_WIDTH = 31
FFN_HIDDEN = 2816
ROPE_THETA = 10000.0
EPS = 1e-6
NEG = -1e30
LOG2E = 1.4426950408889634
FORCE = 1e6

LANES = 128
Q_TILE = 128
TILES_PER_STEP = 2
SLC_CHUNK = 256
WIN_KEYS = WINDOW + Q_TILE
PROJ_TILE = 512
CMP_SEQS = 4
CONV_TILE = 512
CONV_HALO = 32
CONV_ROWS = 256
FFN_TILE = 512
FFN_CHUNK = 256
VMEM_LIMIT = 56 * 1024 * 1024

C_Q = 0
C_CMP = 512
C_SLC = 768
C_WIN = 1024
MAIN_COLS = 1280
GATE_COLS = 3 * NSA_HEADS

_TRANS_B = (((1,), (1,)), ((), ()))


def _dot(a, b):
    return jnp.dot(a, b, preferred_element_type=F32)


def _dot_tb(a, b):
    return lax.dot_general(a, b, _TRANS_B, preferred_element_type=F32)


def _split_bf16(x):
    hi = x.astype(BF16)
    lo = (x - hi.astype(F32)).astype(BF16)
    return hi, lo


def _sigmoid(x):
    return 1.0 / (1.0 + jnp.exp(-x))


def _head_norm_rope(y, gain, bd, cos, sin, first_half, scale):
    hi, lo = _split_bf16(y * y)
    ss = _dot(hi, bd) + _dot(lo, bd)
    yn = y * lax.rsqrt(ss * (1.0 / HEAD_DIM) + EPS) * gain
    outs = []
    for j in range(y.shape[1] // LANES):
        blk = yn[:, j * LANES:(j + 1) * LANES]
        rot = jnp.where(first_half, pltpu.roll(blk, LANES - HEAD_DIM // 2, 1),
                        pltpu.roll(blk, HEAD_DIM // 2, 1))
        outs.append((blk * cos + rot * sin) * scale)
    return outs


def _value_t_with_ones(v):
    vt = v.T
    row = lax.broadcasted_iota(jnp.int32, vt.shape, 0)
    return (jnp.where(row < HEAD_DIM, vt, 1.0).astype(BF16),
            jnp.where(row < HEAD_DIM, 1.0, vt).astype(BF16))


def _proj_body(x_ref, gin_ref, w_ref, cos_ref, sin_ref, gq_ref, gk_ref, bd_ref,
               wgu_in_ref, wd_in_ref, wo_in_ref,
               q_ref, ks_ref, kw_ref, vs0_ref, vs1_ref, vw0_ref, vw1_ref, kc_ref, vc_ref, gt_ref,
               hcv_ref, wgu_out_ref, wd_out_ref, wo_out_ref, wm_ref, wc_ref, wg_ref):
    @pl.when(pl.program_id(0) == 0)
    def _():
        for r0 in range(0, D_MODEL, LANES):
            rows = slice(r0, r0 + LANES)
            wm_ref[rows, :] = w_ref[rows, 0:MAIN_COLS].astype(BF16)
            gl = w_ref[rows, MAIN_COLS:MAIN_COLS + GATE_COLS]
            wg_ref[rows, :] = jnp.concatenate(
                [gl, jnp.zeros((LANES, LANES - GATE_COLS), F32)], axis=1).astype(BF16)
            wc_ref[rows, :] = w_ref[rows, MAIN_COLS + GATE_COLS:MAIN_COLS + GATE_COLS + 2 * CONV_CH].astype(BF16)

    wgu_out_ref[...] = wgu_in_ref[...].astype(BF16)
    wd_out_ref[...] = wd_in_ref[...].astype(BF16)
    wo_out_ref[...] = wo_in_ref[...].astype(BF16)

    x = x_ref[...]
    xg = (x * gin_ref[...]).astype(BF16)
    rs = lax.rsqrt(jnp.mean(x * x, axis=-1, keepdims=True) + EPS)
    cos = cos_ref[...]
    sin = sin_ref[...]
    bd = bd_ref[...]
    lane = lax.broadcasted_iota(jnp.int32, (PROJ_TILE, LANES), 1)
    first_half = (lane & (HEAD_DIM // 2)) == 0
    held = {}

    def queries(group, y):
        held["q", group] = _head_norm_rope(y, gq_ref[...], bd, cos, sin, first_half,
                                           HEAD_DIM ** -0.5 * LOG2E)
        if group == 0:
            return
        low = lane < HEAD_DIM
        for j in range(2):
            a, b = held["q", 0][j], held["q", 1][j]
            q_ref[:, (2 * j) * LANES:(2 * j + 1) * LANES] = jnp.where(
                low, a, pltpu.roll(b, HEAD_DIM, 1)).astype(BF16)
            q_ref[:, (2 * j + 1) * LANES:(2 * j + 2) * LANES] = jnp.where(
                low, pltpu.roll(a, HEAD_DIM, 1), b).astype(BF16)

    def values(y, out0_ref, out1_ref):
        for i in range(PROJ_TILE // LANES):
            out0_ref[i], out1_ref[i] = _value_t_with_ones(y[i * LANES:(i + 1) * LANES, :])

    def slc_kv(y):
        held["ks"] = y[:, :LANES]
        values(y[:, LANES:], vs0_ref, vs1_ref)

    def win_kv(y):
        kblks = _head_norm_rope(_lane_cat([held["ks"], y[:, :LANES]]), gk_ref[...], bd, cos, sin,
                                first_half, 1.0)
        ks_ref[...] = kblks[0].astype(BF16)
        kw_ref[...] = kblks[1].astype(BF16)
        values(y[:, LANES:], vw0_ref, vw1_ref)

    def cmp_inputs(y):
        kc_ref[...] = y[:, :LANES]
        vc_ref[...] = y[:, LANES:]

    def conv_lin(y):
        held["ca"] = y

    def conv_glu(y):
        hcv_ref[...] = held["ca"] * _sigmoid(y)

    def gates(y):
        sg = _sigmoid(y)
        for i in range(PROJ_TILE // LANES):
            gt_ref[i] = sg[i * LANES:(i + 1) * LANES, :].T[0:32, :]

    sections = [
        (wm_ref, C_Q, 256, functools.partial(queries, 0)),
        (wm_ref, C_Q + 256, 256, functools.partial(queries, 1)),
        (wc_ref, 0, CONV_CH, conv_lin),
        (wm_ref, C_SLC, 256, slc_kv),
        (wc_ref, CONV_CH, CONV_CH, conv_glu),
        (wm_ref, C_WIN, 256, win_kv),
        (wm_ref, C_CMP, 256, cmp_inputs),
        (wg_ref, 0, LANES, gates),
    ]

    def project(sec):
        w_ref, c0, width, _ = sec
        return _dot(xg, w_ref[:, c0:c0 + width])

    y_next = project(sections[0])
    for i, sec in enumerate(sections):
        y_cur = y_next
        if i + 1 < len(sections):
            y_next = project(sections[i + 1])
        sec[3](y_cur * rs)


def _paired_row_block(i):
    head = i // 2
    src = ((head % NSA_KV_HEADS) * GQA_REP + head // NSA_KV_HEADS) * 2 + i % 2
    return jnp.where(i < NSA_WIDTH // 32, src, i)


def _proj_call(x2, gin, w_in, cos, sin, gq, gk, bd, w_gate_up, w_down, w_out, seq):
    t = x2.shape[0]
    nt = t // PROJ_TILE
    assert D_MODEL // nt == 32 and nt % 16 == 0 and (FFN_HIDDEN // 16) % 16 == 0
    wd_rows = FFN_HIDDEN // 16
    tiles_per_seq = seq // PROJ_TILE
    row = lambda i: (i, 0)
    const = lambda i: (0, 0)
    tab = lambda i: (i % tiles_per_seq, 0)
    blk3 = lambda i: (i, 0, 0)
    wd_blk = lambda i: (i * 16 // nt, 0)
    nb = PROJ_TILE // LANES
    return pl.pallas_call(
        _proj_body,
        grid=(nt,),
        in_specs=[
            pl.BlockSpec((PROJ_TILE, D_MODEL), row),
            pl.BlockSpec((1, D_MODEL), const),
            pl.BlockSpec(w_in.shape, const, pipeline_mode=pl.Buffered(1)),
            pl.BlockSpec((PROJ_TILE, LANES), tab),
            pl.BlockSpec((PROJ_TILE, LANES), tab),
            pl.BlockSpec((1, 256), const),
            pl.BlockSpec((1, 256), const),
            pl.BlockSpec((256, 256), const),
            pl.BlockSpec((32, 2 * FFN_HIDDEN), row),
            pl.BlockSpec((wd_rows, D_MODEL), wd_blk),
            pl.BlockSpec((32, D_MODEL), lambda i: (_paired_row_block(i), 0)),
        ],
        out_specs=[
            pl.BlockSpec((PROJ_TILE, NSA_WIDTH), row),
            pl.BlockSpec((PROJ_TILE, LANES), row),
            pl.BlockSpec((PROJ_TILE, LANES), row),
            pl.BlockSpec((nb, LANES, LANES), blk3),
            pl.BlockSpec((nb, LANES, LANES), blk3),
            pl.BlockSpec((nb, LANES, LANES), blk3),
            pl.BlockSpec((nb, LANES, LANES), blk3),
            pl.BlockSpec((PROJ_TILE, LANES), row),
            pl.BlockSpec((PROJ_TILE, LANES), row),
            pl.BlockSpec((nb, 32, LANES), blk3),
            pl.BlockSpec((PROJ_TILE, CONV_CH), row),
            pl.BlockSpec((32, 2 * FFN_HIDDEN), row),
            pl.BlockSpec((wd_rows, D_MODEL), wd_blk),
            pl.BlockSpec((32, D_MODEL), row),
        ],
        out_shape=[
            jax.ShapeDtypeStruct((t, NSA_WIDTH), BF16),
            jax.ShapeDtypeStruct((t, LANES), BF16),
            jax.ShapeDtypeStruct((t, LANES), BF16),
            jax.ShapeDtypeStruct((t // LANES, LANES, LANES), BF16),
            jax.ShapeDtypeStruct((t // LANES, LANES, LANES), BF16),
            jax.ShapeDtypeStruct((t // LANES, LANES, LANES), BF16),
            jax.ShapeDtypeStruct((t // LANES, LANES, LANES), BF16),
            jax.ShapeDtypeStruct((t, LANES), F32),
            jax.ShapeDtypeStruct((t, LANES), F32),
            jax.ShapeDtypeStruct((t // LANES, 32, LANES), F32),
            jax.ShapeDtypeStruct((t, CONV_CH), F32),
            jax.ShapeDtypeStruct((D_MODEL, 2 * FFN_HIDDEN), BF16),
            jax.ShapeDtypeStruct((FFN_HIDDEN, D_MODEL), BF16),
            jax.ShapeDtypeStruct((D_MODEL, D_MODEL), BF16),
        ],
        scratch_shapes=[pltpu.VMEM((D_MODEL, MAIN_COLS), BF16),
                        pltpu.VMEM((D_MODEL, 2 * CONV_CH), BF16),
                        pltpu.VMEM((D_MODEL, LANES), BF16)],
        compiler_params=pltpu.CompilerParams(
            dimension_semantics=("arbitrary",), vmem_limit_bytes=VMEM_LIMIT),
        name="proj",
    )(x2, gin, w_in, cos, sin, gq, gk, bd, w_gate_up, w_down, w_out)


def _cmp_mlp(t_ref, pe_ref, w1_ref, w2_ref):
    half = CMP_LEN // 2
    nchunk = t_ref.shape[0] // CMP_STRIDE
    toks = [t_ref[pl.ds(l, nchunk, stride=CMP_STRIDE), :] for l in range(half)]
    xp = _lane_cat([(toks[l] + pe_ref[l:l + 1, :]).astype(BF16) for l in range(half)])
    xq = _lane_cat([(toks[l] + pe_ref[half + l:half + l + 1, :]).astype(BF16) for l in range(half)])
    h1 = _dot(xp, w1_ref[0]) + pltpu.roll(_dot(xq, w1_ref[1]), nchunk - 1, 0)
    act = (h1 * _sigmoid(h1)).astype(BF16)
    return _dot(act, w2_ref[...])


def _cmp_body(kc_ref, vc_ref, pek_ref, w1k_ref, w2k_ref, pev_ref, w1v_ref, w2v_ref,
              cos_ref, sin_ref, gk_ref, bd_ref, ko_ref, vt0_ref, vt1_ref):
    kraw = _cmp_mlp(kc_ref, pek_ref, w1k_ref, w2k_ref)
    lane = lax.broadcasted_iota(jnp.int32, kraw.shape, 1)
    first_half = (lane & (HEAD_DIM // 2)) == 0
    (kr,) = _head_norm_rope(kraw, gk_ref[...], bd_ref[...], cos_ref[...], sin_ref[...],
                            first_half, 1.0)
    ko_ref[...] = kr.astype(BF16)
    vraw = _cmp_mlp(vc_ref, pev_ref, w1v_ref, w2v_ref)
    for i in range(CMP_SEQS):
        vt0_ref[i], vt1_ref[i] = _value_t_with_ones(vraw[i * LANES:(i + 1) * LANES, :])


def _cmp_call(kc, vc, pek, w1k, w2k, pev, w1v, w2v, cos, sin, gk, bd, batch, seq):
    seqblk = lambda b: (b, 0)
    c2 = lambda b: (0, 0)
    c3 = lambda b: (0, 0, 0)
    ncmp_pad = seq // CMP_STRIDE
    assert ncmp_pad == LANES and batch % CMP_SEQS == 0
    rows = CMP_SEQS * ncmp_pad
    half_k = (CMP_LEN // 2) * LANES
    return pl.pallas_call(
        _cmp_body,
        grid=(batch // CMP_SEQS,),
        in_specs=[
            pl.BlockSpec((CMP_SEQS * seq, LANES), seqblk),
            pl.BlockSpec((CMP_SEQS * seq, LANES), seqblk),
            pl.BlockSpec((CMP_LEN, LANES), c2),
            pl.BlockSpec((2, half_k, 2 * CMP_HIDDEN), c3),
            pl.BlockSpec((2 * CMP_HIDDEN, LANES), c2),
            pl.BlockSpec((CMP_LEN, LANES), c2),
            pl.BlockSpec((2, half_k, 2 * CMP_HIDDEN), c3),
            pl.BlockSpec((2 * CMP_HIDDEN, LANES), c2),
            pl.BlockSpec((rows, LANES), c2),
            pl.BlockSpec((rows, LANES), c2),
            pl.BlockSpec((1, LANES), c2),
            pl.BlockSpec((LANES, LANES), c2),
        ],
        out_specs=[
            pl.BlockSpec((rows, LANES), seqblk),
            pl.BlockSpec((CMP_SEQS, LANES, ncmp_pad), lambda b: (b, 0, 0)),
            pl.BlockSpec((CMP_SEQS, LANES, ncmp_pad), lambda b: (b, 0, 0)),
        ],
        out_shape=[
            jax.ShapeDtypeStruct((batch * ncmp_pad, LANES), BF16),
            jax.ShapeDtypeStruct((batch, LANES, ncmp_pad), BF16),
            jax.ShapeDtypeStruct((batch, LANES, ncmp_pad), BF16),
        ],
        compiler_params=pltpu.CompilerParams(
            dimension_semantics=("parallel",), vmem_limit_bytes=VMEM_LIMIT),
        name="compress",
    )(kc, vc, pek, w1k, w2k, pev, w1v, w2v, cos, sin, gk, bd)


def _lane_cat(blocks):
    return jnp.concatenate(blocks, axis=1)


def _softmax_numer(s, bias, m_run=None):
    es, ms = [], []
    for r in range(GQA_REP):
        sm = s[:, r * LANES:(r + 1) * LANES] + bias
        m = jnp.max(sm, axis=0, keepdims=True)
        if m_run is not None:
            m = jnp.maximum(m, m_run[:, r * LANES:(r + 1) * LANES])
        es.append(jnp.exp2(sm - m))
        ms.append(m)
    return _lane_cat(es), _lane_cat(ms)


def _safe_inv(l):
    return 1.0 / jnp.where(l > 0.0, l, 1.0)


def _attn_body(*refs):
    first = pl.program_id(0) * TILES_PER_STEP
    n_max = refs[1].shape[0] // SLC_CHUNK
    for n in range(1, n_max + 1):
        pl.when((first * Q_TILE) // SLC_CHUNK + 1 == n)(functools.partial(_attn_step, n, first, *refs))


def _attn_step(n_chunks, first, *refs):
    def one_tile(slot, carry):
        tile = _tile_program(n_chunks, first + slot, slot, *refs)
        pending = None
        s_next = tile.scores(tile.units[0])
        for i, unit in enumerate(tile.units):
            s_cur = s_next
            if i + 1 < len(tile.units):
                s_next = tile.scores(tile.units[i + 1])
            e, alpha = tile.softmax(unit, s_cur)
            if pending is not None:
                tile.weighted_values(*pending)
            pending = (unit, e, alpha)
        tile.weighted_values(*pending)
        tile.finish()
        return carry

    lax.fori_loop(0, TILES_PER_STEP, one_tile, 0)


def _tile_program(n_chunks, c, slot, q_ref, ks_ref, kw_ref, vs0_ref, vs1_ref, vw0_ref, vw1_ref, kc_ref,
                  vc0_ref, vc1_ref, gt_ref, ovl_ref, gn_ref, out_ref):
    groups = range(NSA_KV_HEADS)
    q_rows = pl.ds(pl.multiple_of(slot * Q_TILE, Q_TILE), Q_TILE)
    vs_refs, vw_refs, vc_refs = (vs0_ref, vs1_ref), (vw0_ref, vw1_ref), (vc0_ref, vc1_ref)
    sum_row = (HEAD_DIM, 0)
    width = GQA_REP * LANES
    t_lane = c * Q_TILE + lax.broadcasted_iota(jnp.int32, (1, LANES), 1)
    lane_sq = lax.broadcasted_iota(jnp.int32, (Q_TILE, LANES), 1)
    n_live = n_chunks * (SLC_CHUNK // SLC_BLOCK)
    nsel = 32
    win_keys = min(WIN_KEYS, n_chunks * SLC_CHUNK)
    j_idx = lax.broadcasted_iota(jnp.int32, (nsel, LANES), 0)
    cur = lax.shift_right_logical(t_lane, 6)
    n_idx = lax.broadcasted_iota(jnp.int32, (LANES, LANES), 0)
    cbias = jnp.where((n_idx * CMP_STRIDE + (CMP_LEN - 1) <= t_lane) & (n_idx < LANES - 1), 0.0, NEG)
    has_cmp = jnp.where(t_lane >= CMP_LEN - 1, 1.0, 0.0)
    has_cmp = _lane_cat([has_cmp] * GQA_REP)
    win_blk = jnp.maximum(c - WINDOW // Q_TILE, 0)
    win_start = pl.multiple_of(win_blk * Q_TILE, Q_TILE)
    kpos_w = win_start + lax.broadcasted_iota(jnp.int32, (win_keys, LANES), 0)
    wbias = jnp.where((kpos_w <= t_lane) & (kpos_w > t_lane - WINDOW), 0.0, NEG)
    row_sl = lax.broadcasted_iota(jnp.int32, (SLC_CHUNK, LANES), 0)
    zero_q = jnp.zeros((Q_TILE, LANES), BF16)

    def gate_row(g, branch):
        rows = [(g * GQA_REP + r) * 3 + branch for r in range(GQA_REP)]
        return _lane_cat([gt_ref[slot, i:i + 1, :] for i in rows])

    qg = []
    for g in groups:
        in_g = (lane_sq >= HEAD_DIM) if g else (lane_sq < HEAD_DIM)
        qg.append(jnp.concatenate(
            [jnp.where(in_g, q_ref[q_rows, r * LANES:(r + 1) * LANES], zero_q) for r in range(GQA_REP)],
            axis=0))

    o_cmp = [None] * NSA_KV_HEADS
    sel_bias = [None] * NSA_KV_HEADS

    def compressed_softmax(g, sc):
        ec, _ = _softmax_numer(sc, cbias)
        coef = has_cmp * _safe_inv(jnp.sum(ec, axis=0, keepdims=True))
        pc = ec * coef
        psum = (pc[:, 0:LANES] + pc[:, LANES:2 * LANES]
                + pc[:, 2 * LANES:3 * LANES] + pc[:, 3 * LANES:4 * LANES])
        p_hi, p_lo = _split_bf16(psum)
        imp = (_dot(ovl_ref[...], p_hi) + _dot(ovl_ref[...], p_lo))[0:nsel, :]
        valid = j_idx <= cur
        forced = (j_idx == 0) | (j_idx == cur) | (j_idx == cur - 1)
        score = jnp.where(valid, imp + jnp.where(forced, FORCE, 0.0), -FORCE)
        rank = jnp.zeros((nsel, LANES), F32)
        for jp in range(nsel):
            row = score[jp:jp + 1, :]
            beats = (row > score) | ((row == score) & (j_idx > jp))
            rank = rank + jnp.where(beats, 1.0, 0.0)
        sel_bias[g] = jnp.where((rank < float(SLC_TOPN)) & (score > -1.0), 0.0, NEG)
        return ec.astype(BF16), coef

    nsub = SLC_CHUNK // LANES
    blocks_per_chunk = SLC_CHUNK // SLC_BLOCK
    kwin = kw_ref[pl.ds(win_start, win_keys), :]
    units = [("cmp", g, 0) for g in groups] + [("win", g, 0) for g in groups]
    units += [("slc", g, kc) for kc in range(n_chunks) for g in groups]
    m_run = [None] * NSA_KV_HEADS
    acc_s = [None] * NSA_KV_HEADS
    acc_w = [None] * NSA_KV_HEADS

    def scores(unit):
        kind, g, kc = unit
        if kind == "cmp":
            keys = kc_ref[...]
        elif kind == "win":
            keys = kwin
        else:
            keys = ks_ref[kc * SLC_CHUNK:(kc + 1) * SLC_CHUNK, :]
        return _dot_tb(keys, qg[g])

    def softmax(unit, s):
        kind, g, kc = unit
        if kind == "cmp":
            return compressed_softmax(g, s)
        if kind == "win":
            e, _ = _softmax_numer(s, wbias)
            return e.astype(BF16), None
        j0 = kc * blocks_per_chunk
        bias = jnp.concatenate(
            [jnp.broadcast_to(sel_bias[g][j:j + 1, :], (SLC_BLOCK, LANES))
             for j in range(j0, j0 + blocks_per_chunk)], axis=0)
        if kc == n_chunks - 1:
            bias = jnp.where(kc * SLC_CHUNK + row_sl <= t_lane, bias, NEG)
        e, m_new = _softmax_numer(s, bias, m_run[g])
        alpha = None if kc == 0 else jnp.exp2(m_run[g] - m_new)
        m_run[g] = m_new
        return e.astype(BF16), alpha

    def weighted_values(unit, e, alpha):
        kind, g, kc = unit
        if kind == "cmp":
            o_cmp[g] = _dot(vc_refs[g][0], e) * (gate_row(g, 0) * alpha)
            return
        if kind == "win":
            vt = _lane_cat([vw_refs[g][win_blk + i] for i in range(win_keys // LANES)])
            acc_w[g] = _dot(vt, e)
            return
        vt = _lane_cat([vs_refs[g][nsub * kc + i] for i in range(nsub)])
        pv = _dot(vt, e)
        acc_s[g] = pv if alpha is None else acc_s[g] * alpha + pv

    def finish():
        o_groups = []
        for g in groups:
            r0 = sum_row[g]
            o_groups.append(o_cmp[g]
                            + acc_s[g] * (gate_row(g, 1) * _safe_inv(acc_s[g][r0:r0 + 1, :]))
                            + acc_w[g] * (gate_row(g, 2) * _safe_inv(acc_w[g][r0:r0 + 1, :])))
        row_d = lax.broadcasted_iota(jnp.int32, (LANES, width), 0)
        o_t = jnp.where(row_d < HEAD_DIM, o_groups[0], o_groups[1])
        sq = jnp.sum(o_t * o_t, axis=0, keepdims=True)
        ss = (sq[:, 0:LANES] + sq[:, LANES:2 * LANES]
              + sq[:, 2 * LANES:3 * LANES] + sq[:, 3 * LANES:4 * LANES])
        rs = lax.rsqrt(ss * (1.0 / NSA_WIDTH) + EPS)
        for r in range(GQA_REP):
            cols = slice(r * LANES, (r + 1) * LANES)
            out_ref[q_rows, cols] = (o_t[:, cols] * rs * gn_ref[:, cols]).T.astype(BF16)

    return types.SimpleNamespace(units=units, scores=scores, softmax=softmax,
                                 weighted_values=weighted_values, finish=finish)


def _attn_call(q, ks, kw, vs0, vs1, vw0, vw1, kcmp, vc0, vc1, gt, ovl, gn, batch, seq):
    assert SLC_CHUNK % (TILES_PER_STEP * Q_TILE) == 0
    rows = TILES_PER_STEP * Q_TILE
    nq = seq // rows
    nkb = seq // LANES
    qrow = lambda c, b: (b * nq + c, 0)
    per_b = lambda c, b: (b, 0)
    per_b3 = lambda c, b: (b, 0, 0)
    vspec = pl.BlockSpec((nkb, LANES, LANES), per_b3)
    cspec = pl.BlockSpec((1, LANES, LANES), per_b3)
    return pl.pallas_call(
        _attn_body,
        grid=(nq, batch),
        in_specs=[
            pl.BlockSpec((rows, NSA_WIDTH), qrow),
            pl.BlockSpec((seq, LANES), per_b),
            pl.BlockSpec((seq, LANES), per_b),
            vspec, vspec, vspec, vspec,
            pl.BlockSpec((LANES, LANES), per_b),
            cspec, cspec,
            pl.BlockSpec((TILES_PER_STEP, 32, LANES), lambda c, b: (b * nq + c, 0, 0)),
            pl.BlockSpec((LANES, LANES), lambda c, b: (0, 0)),
            pl.BlockSpec((LANES, NSA_WIDTH), lambda c, b: (0, 0)),
        ],
        out_specs=pl.BlockSpec((rows, NSA_WIDTH), qrow),
        out_shape=jax.ShapeDtypeStruct((batch * seq, NSA_WIDTH), BF16),
        compiler_params=pltpu.CompilerParams(
            dimension_semantics=("parallel", "arbitrary"), vmem_limit_bytes=VMEM_LIMIT),
        name="attn",
    )(q, ks, kw, vs0, vs1, vw0, vw1, kcmp, vc0, vc1, gt, ovl, gn)


def _conv_body(tiles_per_seq, cur_ref, halo_ref, w_ref, b_ref, lng_ref, lnb_ref, gn_ref, out_ref,
               buf_ref, sh_ref, cv_ref):
    first = (pl.program_id(0) % tiles_per_seq) == 0
    buf_ref[0:CONV_HALO, :] = jnp.where(first, 0.0, halo_ref[...])
    buf_ref[CONV_HALO:, :] = cur_ref[...]
    lead = CONV_HALO - (CONV_WIDTH - 1)
    ncb = CONV_CH // LANES
    nrb = CONV_TILE // CONV_ROWS
    for s in range(8):
        span = CONV_TILE + 8 * ((CONV_WIDTH - 1 - s) // 8)
        for cb in range(ncb):
            sh_ref[s, cb, 0:span, :] = buf_ref[lead + s:lead + s + span, cb * LANES:(cb + 1) * LANES]

    def conv_block(i, carry):
        cb = i % ncb
        base = pl.multiple_of((i // ncb) * CONV_ROWS, CONV_ROWS)
        acc = jnp.zeros((CONV_ROWS, LANES), F32) + b_ref[cb]
        for k in range(CONV_WIDTH):
            acc = acc + w_ref[cb, k:k + 1, :] * sh_ref[k % 8, cb, pl.ds(base + 8 * (k // 8), CONV_ROWS), :]
        cv_ref[cb, pl.ds(base, CONV_ROWS), :] = acc
        return carry

    lax.fori_loop(0, nrb * ncb, conv_block, 0)
    h = _lane_cat([cv_ref[cb] for cb in range(ncb)])
    mu = jnp.mean(h, axis=-1, keepdims=True)
    d = h - mu
    var = jnp.mean(d * d, axis=-1, keepdims=True)
    hn = d * lax.rsqrt(var + EPS) * lng_ref[...] + lnb_ref[...]
    o = hn * _sigmoid(hn)
    ms = jnp.mean(o * o, axis=-1, keepdims=True)
    out_ref[...] = (o * lax.rsqrt(ms + EPS) * gn_ref[...]).astype(BF16)


def _conv_call(hcv, w, b, lng, lnb, gn, seq):
    t = hcv.shape[0]
    nt = t // CONV_TILE
    halo_per_tile = CONV_TILE // CONV_HALO
    ncb = CONV_CH // LANES
    row = lambda i: (i, 0)
    const = lambda i: (0, 0)
    return pl.pallas_call(
        functools.partial(_conv_body, seq // CONV_TILE),
        grid=(nt,),
        in_specs=[
            pl.BlockSpec((CONV_TILE, CONV_CH), row),
            pl.BlockSpec((CONV_HALO, CONV_CH), lambda i: (jnp.maximum(i * halo_per_tile - 1, 0), 0)),
            pl.BlockSpec((ncb, CONV_HALO, LANES), lambda i: (0, 0, 0)),
            pl.BlockSpec((ncb, 1, LANES), lambda i: (0, 0, 0)),
            pl.BlockSpec((1, CONV_CH), const),
            pl.BlockSpec((1, CONV_CH), const),
            pl.BlockSpec((1, CONV_CH), const),
        ],
        out_specs=pl.BlockSpec((CONV_TILE, CONV_CH), row),
        out_shape=jax.ShapeDtypeStruct((t, CONV_CH), BF16),
        scratch_shapes=[pltpu.VMEM((CONV_HALO + CONV_TILE, CONV_CH), F32),
                        pltpu.VMEM((8, ncb, CONV_TILE + 8 * ((CONV_WIDTH - 1) // 8), LANES), F32),
                        pltpu.VMEM((ncb, CONV_TILE, LANES), F32)],
        compiler_params=pltpu.CompilerParams(
            dimension_semantics=("parallel",), vmem_limit_bytes=VMEM_LIMIT),
        name="conv",
    )(hcv, hcv, w, b, lng, lnb, gn)


def _ffn_body(x_ref, ma_ref, mb_ref, wo_ref, gf_ref, wgu_ref, wd_ref, out_ref):
    h = (x_ref[...] + _dot(ma_ref[...], wo_ref[0:NSA_WIDTH, :])
         + _dot(mb_ref[...], wo_ref[NSA_WIDTH:D_MODEL, :]))
    ms = jnp.mean(h * h, axis=-1, keepdims=True)
    hn = (h * lax.rsqrt(ms + EPS) * gf_ref[...]).astype(BF16)
    acc = jnp.zeros_like(h)
    for j in range(FFN_HIDDEN // FFN_CHUNK):
        c0 = j * FFN_CHUNK
        gte = _dot(hn, wgu_ref[:, c0:c0 + FFN_CHUNK])
        up = _dot(hn, wgu_ref[:, FFN_HIDDEN + c0:FFN_HIDDEN + c0 + FFN_CHUNK])
        act = (gte * _sigmoid(gte) * up).astype(BF16)
        acc = acc + _dot(act, wd_ref[c0:c0 + FFN_CHUNK, :])
    out_ref[...] = h + acc


def _ffn_call(x2, ma, mb, wo, gf, wgu, wd):
    t = x2.shape[0]
    nt = t // FFN_TILE
    row = lambda i: (i, 0)
    const = lambda i: (0, 0)
    once = pl.Buffered(1)
    return pl.pallas_call(
        _ffn_body,
        grid=(nt,),
        in_specs=[
            pl.BlockSpec((FFN_TILE, D_MODEL), row),
            pl.BlockSpec((FFN_TILE, NSA_WIDTH), row),
            pl.BlockSpec((FFN_TILE, CONV_CH), row),
            pl.BlockSpec((D_MODEL, D_MODEL), const, pipeline_mode=once),
            pl.BlockSpec((1, D_MODEL), const),
            pl.BlockSpec((D_MODEL, 2 * FFN_HIDDEN), const, pipeline_mode=once),
            pl.BlockSpec((FFN_HIDDEN, D_MODEL), const, pipeline_mode=once),
        ],
        out_specs=pl.BlockSpec((FFN_TILE, D_MODEL), row),
        out_shape=jax.ShapeDtypeStruct((t, D_MODEL), F32),
        compiler_params=pltpu.CompilerParams(
            dimension_semantics=("parallel",), vmem_limit_bytes=VMEM_LIMIT),
        name="ffn",
    )(x2, ma, mb, wo, gf, wgu, wd)


def _pair_heads(a, axis):
    shape = a.shape
    a = a.reshape(shape[:axis] + (NSA_KV_HEADS, GQA_REP, HEAD_DIM) + shape[axis + 1:])
    return jnp.swapaxes(a, axis, axis + 1).reshape(shape)


def _transposed_gain(g):
    cols = g.reshape(GQA_REP, LANES).T
    return jnp.broadcast_to(cols[:, :, None], (LANES, GQA_REP, Q_TILE)).reshape(LANES, GQA_REP * Q_TILE)


def _rope_tables(pos):
    half = HEAD_DIM // 2
    inv = ROPE_THETA ** (-np.arange(half, dtype=np.float64) / half)
    ang = np.asarray(pos, np.float64)[:, None] * inv[None, :]
    cos = np.concatenate([np.cos(ang), np.cos(ang)], axis=1)
    sin = np.concatenate([-np.sin(ang), np.sin(ang)], axis=1)
    return (np.tile(cos, (1, NSA_KV_HEADS)).astype(np.float32),
            np.tile(sin, (1, NSA_KV_HEADS)).astype(np.float32))


def _block_diag_ones(width):
    idx = np.arange(width) // HEAD_DIM
    return (idx[:, None] == idx[None, :]).astype(np.float32)


def _overlap_t(seq):
    ncmp = (seq - CMP_LEN) // CMP_STRIDE + 1
    cs = np.arange(ncmp)[:, None] * CMP_STRIDE
    ss = np.arange(seq // SLC_BLOCK)[None, :] * SLC_BLOCK
    ov = np.clip(np.minimum(cs + CMP_LEN, ss + SLC_BLOCK) - np.maximum(cs, ss), 0, None) / CMP_LEN
    out = np.zeros((LANES, LANES), np.float32)
    out[:seq // SLC_BLOCK, :ncmp] = ov.T
    return out


def _cmp_weights(pe, w1, w2):
    pe2 = jnp.concatenate([pe, pe], axis=1)
    w1r = w1.reshape(CMP_LEN, HEAD_DIM, CMP_HIDDEN)
    z1 = jnp.zeros_like(w1r)
    w1b = jnp.concatenate([jnp.concatenate([w1r, z1], axis=2),
                           jnp.concatenate([z1, w1r], axis=2)], axis=1).astype(BF16)
    z2 = jnp.zeros_like(w2)
    w2b = jnp.concatenate([jnp.concatenate([w2, z2], axis=1),
                           jnp.concatenate([z2, w2], axis=1)], axis=0).astype(BF16)
    w1b = w1b.reshape(2, (CMP_LEN // 2) * LANES, 2 * CMP_HIDDEN)
    return pe2, w1b, w2b


def kernel(x, attn_norm_g, w_in, q_norm_g, k_norm_cmp_g, k_norm_slc_g, k_norm_win_g, cmp_pe_k, cmp_w1_k, cmp_w2_k, cmp_pe_v, cmp_w1_v, cmp_w2_v, conv_dw_w, conv_dw_b, conv_ln_g, conv_ln_b, out_norm_nsa_g, out_norm_conv_g, w_out, ffn_norm_g, w_gate_up, w_down):
    batch, seq, d_model = x.shape
    assert d_model == D_MODEL and seq % PROJ_TILE == 0 and seq // SLC_BLOCK == 32
    depth = w_in.shape[0]
    cos_np, sin_np = _rope_tables(np.arange(seq))
    ccos_np, csin_np = _rope_tables(np.arange(seq // CMP_STRIDE) * CMP_STRIDE + CMP_LEN - 1)
    ccos_np, csin_np = np.tile(ccos_np, (CMP_SEQS, 1)), np.tile(csin_np, (CMP_SEQS, 1))
    cos, sin, ccos, csin = map(jnp.asarray, (cos_np, sin_np, ccos_np, csin_np))
    bd256 = jnp.asarray(_block_diag_ones(256), BF16)
    bd128 = jnp.asarray(_block_diag_ones(LANES), BF16)
    ovl = jnp.asarray(_overlap_t(seq), BF16)

    x2 = x.reshape(batch * seq, d_model)
    for l in range(depth):
        w = w_in[l]
        gq = jnp.tile(q_norm_g[l], 4)[None, :]
        gk = jnp.concatenate([jnp.tile(k_norm_slc_g[l], 2), jnp.tile(k_norm_win_g[l], 2)])[None, :]
        q, ks, kw, vs0, vs1, vw0, vw1, kc, vc, gt, hcv, wgu_b, wd_b, wo_b = _proj_call(
            x2, attn_norm_g[l][None, :], w, cos, sin, gq, gk, bd256,
            w_gate_up[l], w_down[l], w_out[l], seq)

        pek, w1k, w2k = _cmp_weights(cmp_pe_k[l], cmp_w1_k[l], cmp_w2_k[l])
        pev, w1v, w2v = _cmp_weights(cmp_pe_v[l], cmp_w1_v[l], cmp_w2_v[l])
        kcmp, vc0, vc1 = _cmp_call(kc, vc, pek, w1k, w2k, pev, w1v, w2v, ccos, csin,
                                jnp.tile(k_norm_cmp_g[l], 2)[None, :], bd128, batch, seq)

        mix_a = _attn_call(q, ks, kw, vs0, vs1, vw0, vw1, kcmp, vc0, vc1, gt, ovl,
                           _transposed_gain(_pair_heads(out_norm_nsa_g[l], 0)), batch, seq)

        w_dw = jnp.concatenate([conv_dw_w[l][:, 0, :],
                                jnp.zeros((CONV_HALO - CONV_WIDTH, CONV_CH), F32)], axis=0)
        w_dw = w_dw.reshape(CONV_HALO, CONV_CH // LANES, LANES).transpose(1, 0, 2)
        mix_b = _conv_call(hcv, w_dw, conv_dw_b[l].reshape(CONV_CH // LANES, 1, LANES),
                           conv_ln_g[l][None, :],
                           conv_ln_b[l][None, :], out_norm_conv_g[l][None, :], seq)

        x2 = _ffn_call(x2, mix_a, mix_b, wo_b, ffn_norm_g[l][None, :], wgu_b, wd_b)
    return x2.reshape(batch, seq, d_model)
```

```python
import functools
import types

import numpy as np
import jax
import jax.numpy as jnp
from jax import lax
from jax.experimental import pallas as pl
from jax.experimental.pallas import tpu as pltpu

F32 = jnp.float32
BF16 = jnp.bfloat16

D_MODEL = 1024
HEAD_DIM = 64
NSA_HEADS = 8
NSA_KV_HEADS = 2
GQA_REP = NSA_HEADS // NSA_KV_HEADS
NSA_WIDTH = NSA_HEADS * HEAD_DIM
KV_WIDTH = NSA_KV_HEADS * HEAD_DIM
CONV_CH = D_MODEL - NSA_WIDTH
CMP_LEN = 32
CMP_STRIDE = 16
CMP_HIDDEN = 4 * HEAD_DIM
SLC_BLOCK = 64
SLC_TOPN = 8
WINDOW = 512
CONV_WIDTH = 31
FFN_HIDDEN = 2816
ROPE_THETA = 10000.0
EPS = 1e-6
NEG = -1e30
LOG2E = 1.4426950408889634
FORCE = 1e6

LANES = 128
Q_TILE = 128
TILES_PER_STEP = 2
SLC_CHUNK = 256
WIN_KEYS = WINDOW + Q_TILE
PROJ_TILE = 512
CMP_SEQS = 4
CONV_TILE = 512
CONV_HALO = 32
CONV_ROWS = 256
FFN_TILE = 512
FFN_CHUNK = 256
VMEM_LIMIT = 56 * 1024 * 1024

C_Q = 0
C_CMP = 512
C_SLC = 768
C_WIN = 1024
MAIN_COLS = 1280
GATE_COLS = 3 * NSA_HEADS

_TRANS_B = (((1,), (1,)), ((), ()))


def _dot(a, b):
    return jnp.dot(a, b, preferred_element_type=F32)


def _dot_tb(a, b):
    return lax.dot_general(a, b, _TRANS_B, preferred_element_type=F32)


def _split_bf16(x):
    hi = x.astype(BF16)
    lo = (x - hi.astype(F32)).astype(BF16)
    return hi, lo


def _sigmoid(x):
    return 1.0 / (1.0 + jnp.exp(-x))


def _head_norm_rope(y, gain, bd, cos, sin, first_half, scale):
    hi, lo = _split_bf16(y * y)
    ss = _dot(hi, bd) + _dot(lo, bd)
    yn = y * lax.rsqrt(ss * (1.0 / HEAD_DIM) + EPS) * gain
    outs = []
    for j in range(y.shape[1] // LANES):
        blk = yn[:, j * LANES:(j + 1) * LANES]
        rot = jnp.where(first_half, pltpu.roll(blk, LANES - HEAD_DIM // 2, 1),
                        pltpu.roll(blk, HEAD_DIM // 2, 1))
        outs.append((blk * cos + rot * sin) * scale)
    return outs


def _value_t_with_ones(v):
    vt = v.T
    row = lax.broadcasted_iota(jnp.int32, vt.shape, 0)
    return (jnp.where(row < HEAD_DIM, vt, 1.0).astype(BF16),
            jnp.where(row < HEAD_DIM, 1.0, vt).astype(BF16))


def _proj_body(x_ref, gin_ref, w_ref, cos_ref, sin_ref, gq_ref, gk_ref, bd_ref,
               wgu_in_ref, wd_in_ref, wo_in_ref,
               q_ref, ks_ref, kw_ref, vs0_ref, vs1_ref, vw0_ref, vw1_ref, kc_ref, vc_ref, gt_ref,
               hcv_ref, wgu_out_ref, wd_out_ref, wo_out_ref, wm_ref, wc_ref, wg_ref):
    @pl.when(pl.program_id(0) == 0)
    def _():
        for j in range(MAIN_COLS // LANES):
            wm_ref[:, j * LANES:(j + 1) * LANES] = w_ref[j * LANES:(j + 1) * LANES, :].T.astype(BF16)
        gl = jnp.concatenate([w_ref[MAIN_COLS:MAIN_COLS + GATE_COLS, :],
                              jnp.zeros((LANES - GATE_COLS, D_MODEL), F32)], axis=0)
        wg_ref[...] = gl.T.astype(BF16)
        for j in range(2 * CONV_CH // LANES):
            r0 = MAIN_COLS + GATE_COLS + j * LANES
            wc_ref[:, j * LANES:(j + 1) * LANES] = w_ref[r0:r0 + LANES, :].T.astype(BF16)

    wgu_out_ref[...] = wgu_in_ref[...].astype(BF16)
    wd_out_ref[...] = wd_in_ref[...].astype(BF16)
    wo_out_ref[...] = wo_in_ref[...].astype(BF16)

    x = x_ref[...]
    xg = (x * gin_ref[...]).astype(BF16)
    rs = lax.rsqrt(jnp.mean(x * x, axis=-1, keepdims=True) + EPS)
    cos = cos_ref[...]
    sin = sin_ref[...]
    bd = bd_ref[...]
    lane = lax.broadcasted_iota(jnp.int32, (PROJ_TILE, LANES), 1)
    first_half = (lane & (HEAD_DIM // 2)) == 0
    held = {}

    def queries(group, y):
        held["q", group] = _head_norm_rope(y, gq_ref[...], bd, cos, sin, first_half,
                                           HEAD_DIM ** -0.5 * LOG2E)
        if group == 0:
            return
        low = lane < HEAD_DIM
        for j in range(2):
            a, b = held["q", 0][j], held["q", 1][j]
            q_ref[:, (2 * j) * LANES:(2 * j + 1) * LANES] = jnp.where(
                low, a, pltpu.roll(b, HEAD_DIM, 1)).astype(BF16)
            q_ref[:, (2 * j + 1) * LANES:(2 * j + 2) * LANES] = jnp.where(
                low, pltpu.roll(a, HEAD_DIM, 1), b).astype(BF16)

    def values(y, out0_ref, out1_ref):
        for i in range(PROJ_TILE // LANES):
            out0_ref[i], out1_ref[i] = _value_t_with_ones(y[i * LANES:(i + 1) * LANES, :])

    def slc_kv(y):
        held["ks"] = y[:, :LANES]
        values(y[:, LANES:], vs0_ref, vs1_ref)

    def win_kv(y):
        kblks = _head_norm_rope(_lane_cat([held["ks"], y[:, :LANES]]), gk_ref[...], bd, cos, sin,
                                first_half, 1.0)
        ks_ref[...] = kblks[0].astype(BF16)
        kw_ref[...] = kblks[1].astype(BF16)
        values(y[:, LANES:], vw0_ref, vw1_ref)

    def cmp_inputs(y):
        kc_ref[...] = y[:, :LANES]
        vc_ref[...] = y[:, LANES:]

    def conv_lin(y):
        held["ca"] = y

    def conv_glu(y):
        hcv_ref[...] = held["ca"] * _sigmoid(y)

    def gates(y):
        sg = _sigmoid(y)
        for i in range(PROJ_TILE // LANES):
            gt_ref[i] = sg[i * LANES:(i + 1) * LANES, :].T[0:32, :]

    sections = [
        (wm_ref, C_Q, 256, functools.partial(queries, 0)),
        (wm_ref, C_Q + 256, 256, functools.partial(queries, 1)),
        (wc_ref, 0, CONV_CH, conv_lin),
        (wm_ref, C_SLC, 256, slc_kv),
        (wc_ref, CONV_CH, CONV_CH, conv_glu),
        (wm_ref, C_WIN, 256, win_kv),
        (wm_ref, C_CMP, 256, cmp_inputs),
        (wg_ref, 0, LANES, gates),
    ]

    def project(sec):
        w_ref, c0, width, _ = sec
        return _dot(xg, w_ref[:, c0:c0 + width])

    y_next = project(sections[0])
    for i, sec in enumerate(sections):
        y_cur = y_next
        if i + 1 < len(sections):
            y_next = project(sections[i + 1])
        sec[3](y_cur * rs)


def _paired_row_block(i):
    head = i // 2
    src = ((head % NSA_KV_HEADS) * GQA_REP + head // NSA_KV_HEADS) * 2 + i % 2
    return jnp.where(i < NSA_WIDTH // 32, src, i)


def _proj_call(x2, gin, w_in, cos, sin, gq, gk, bd, w_gate_up, w_down, w_out, seq):
    t = x2.shape[0]
    nt = t // PROJ_TILE
    assert D_MODEL // nt == 32 and nt % 16 == 0 and (FFN_HIDDEN // 16) % 16 == 0
    wd_rows = FFN_HIDDEN // 16
    tiles_per_seq = seq // PROJ_TILE
    row = lambda i: (i, 0)
    const = lambda i: (0, 0)
    tab = lambda i: (i % tiles_per_seq, 0)
    blk3 = lambda i: (i, 0, 0)
    wd_blk = lambda i: (i * 16 // nt, 0)
    nb = PROJ_TILE // LANES
    return pl.pallas_call(
        _proj_body,
        grid=(nt,),
        in_specs=[
            pl.BlockSpec((PROJ_TILE, D_MODEL), row),
            pl.BlockSpec((1, D_MODEL), const),
            pl.BlockSpec(w_in.shape, const, pipeline_mode=pl.Buffered(1)),
            pl.BlockSpec((PROJ_TILE, LANES), tab),
            pl.BlockSpec((PROJ_TILE, LANES), tab),
            pl.BlockSpec((1, 256), const),
            pl.BlockSpec((1, 256), const),
            pl.BlockSpec((256, 256), const),
            pl.BlockSpec((32, 2 * FFN_HIDDEN), row),
            pl.BlockSpec((wd_rows, D_MODEL), wd_blk),
            pl.BlockSpec((32, D_MODEL), lambda i: (_paired_row_block(i), 0)),
        ],
        out_specs=[
            pl.BlockSpec((PROJ_TILE, NSA_WIDTH), row),
            pl.BlockSpec((PROJ_TILE, LANES), row),
            pl.BlockSpec((PROJ_TILE, LANES), row),
            pl.BlockSpec((nb, LANES, LANES), blk3),
            pl.BlockSpec((nb, LANES, LANES), blk3),
            pl.BlockSpec((nb, LANES, LANES), blk3),
            pl.BlockSpec((nb, LANES, LANES), blk3),
            pl.BlockSpec((PROJ_TILE, LANES), row),
            pl.BlockSpec((PROJ_TILE, LANES), row),
            pl.BlockSpec((nb, 32, LANES), blk3),
            pl.BlockSpec((PROJ_TILE, CONV_CH), row),
            pl.BlockSpec((32, 2 * FFN_HIDDEN), row),
            pl.BlockSpec((wd_rows, D_MODEL), wd_blk),
            pl.BlockSpec((32, D_MODEL), row),
        ],
        out_shape=[
            jax.ShapeDtypeStruct((t, NSA_WIDTH), BF16),
            jax.ShapeDtypeStruct((t, LANES), BF16),
            jax.ShapeDtypeStruct((t, LANES), BF16),
            jax.ShapeDtypeStruct((t // LANES, LANES, LANES), BF16),
            jax.ShapeDtypeStruct((t // LANES, LANES, LANES), BF16),
            jax.ShapeDtypeStruct((t // LANES, LANES, LANES), BF16),
            jax.ShapeDtypeStruct((t // LANES, LANES, LANES), BF16),
            jax.ShapeDtypeStruct((t, LANES), F32),
            jax.ShapeDtypeStruct((t, LANES), F32),
            jax.ShapeDtypeStruct((t // LANES, 32, LANES), F32),
            jax.ShapeDtypeStruct((t, CONV_CH), F32),
            jax.ShapeDtypeStruct((D_MODEL, 2 * FFN_HIDDEN), BF16),
            jax.ShapeDtypeStruct((FFN_HIDDEN, D_MODEL), BF16),
            jax.ShapeDtypeStruct((D_MODEL, D_MODEL), BF16),
        ],
        scratch_shapes=[pltpu.VMEM((D_MODEL, MAIN_COLS), BF16),
                        pltpu.VMEM((D_MODEL, 2 * CONV_CH), BF16),
                        pltpu.VMEM((D_MODEL, LANES), BF16)],
        compiler_params=pltpu.CompilerParams(
            dimension_semantics=("arbitrary",), vmem_limit_bytes=VMEM_LIMIT),
        name="proj",
    )(x2, gin, w_in, cos, sin, gq, gk, bd, w_gate_up, w_down, w_out)


def _cmp_mlp(t_ref, pe_ref, w1_ref, w2_ref):
    half = CMP_LEN // 2
    nchunk = t_ref.shape[0] // CMP_STRIDE
    toks = [t_ref[pl.ds(l, nchunk, stride=CMP_STRIDE), :] for l in range(half)]
    xp = _lane_cat([(toks[l] + pe_ref[l:l + 1, :]).astype(BF16) for l in range(half)])
    xq = _lane_cat([(toks[l] + pe_ref[half + l:half + l + 1, :]).astype(BF16) for l in range(half)])
    h1 = _dot(xp, w1_ref[0]) + pltpu.roll(_dot(xq, w1_ref[1]), nchunk - 1, 0)
    act = (h1 * _sigmoid(h1)).astype(BF16)
    return _dot(act, w2_ref[...])


def _cmp_body(kc_ref, vc_ref, pek_ref, w1k_ref, w2k_ref, pev_ref, w1v_ref, w2v_ref,
              cos_ref, sin_ref, gk_ref, bd_ref, ko_ref, vt0_ref, vt1_ref):
    kraw = _cmp_mlp(kc_ref, pek_ref, w1k_ref, w2k_ref)
    lane = lax.broadcasted_iota(jnp.int32, kraw.shape, 1)
    first_half = (lane & (HEAD_DIM // 2)) == 0
    (kr,) = _head_norm_rope(kraw, gk_ref[...], bd_ref[...], cos_ref[...], sin_ref[...],
                            first_half, 1.0)
    ko_ref[...] = kr.astype(BF16)
    vraw = _cmp_mlp(vc_ref, pev_ref, w1v_ref, w2v_ref)
    for i in range(CMP_SEQS):
        vt0_ref[i], vt1_ref[i] = _value_t_with_ones(vraw[i * LANES:(i + 1) * LANES, :])


def _cmp_call(kc, vc, pek, w1k, w2k, pev, w1v, w2v, cos, sin, gk, bd, batch, seq):
    seqblk = lambda b: (b, 0)
    c2 = lambda b: (0, 0)
    c3 = lambda b: (0, 0, 0)
    ncmp_pad = seq // CMP_STRIDE
    assert ncmp_pad == LANES and batch % CMP_SEQS == 0
    rows = CMP_SEQS * ncmp_pad
    half_k = (CMP_LEN // 2) * LANES
    return pl.pallas_call(
        _cmp_body,
        grid=(batch // CMP_SEQS,),
        in_specs=[
            pl.BlockSpec((CMP_SEQS * seq, LANES), seqblk),
            pl.BlockSpec((CMP_SEQS * seq, LANES), seqblk),
            pl.BlockSpec((CMP_LEN, LANES), c2),
            pl.BlockSpec((2, half_k, 2 * CMP_HIDDEN), c3),
            pl.BlockSpec((2 * CMP_HIDDEN, LANES), c2),
            pl.BlockSpec((CMP_LEN, LANES), c2),
            pl.BlockSpec((2, half_k, 2 * CMP_HIDDEN), c3),
            pl.BlockSpec((2 * CMP_HIDDEN, LANES), c2),
            pl.BlockSpec((rows, LANES), c2),
            pl.BlockSpec((rows, LANES), c2),
            pl.BlockSpec((1, LANES), c2),
            pl.BlockSpec((LANES, LANES), c2),
        ],
        out_specs=[
            pl.BlockSpec((rows, LANES), seqblk),
            pl.BlockSpec((CMP_SEQS, LANES, ncmp_pad), lambda b: (b, 0, 0)),
            pl.BlockSpec((CMP_SEQS, LANES, ncmp_pad), lambda b: (b, 0, 0)),
        ],
        out_shape=[
            jax.ShapeDtypeStruct((batch * ncmp_pad, LANES), BF16),
            jax.ShapeDtypeStruct((batch, LANES, ncmp_pad), BF16),
            jax.ShapeDtypeStruct((batch, LANES, ncmp_pad), BF16),
        ],
        compiler_params=pltpu.CompilerParams(
            dimension_semantics=("parallel",), vmem_limit_bytes=VMEM_LIMIT),
        name="compress",
    )(kc, vc, pek, w1k, w2k, pev, w1v, w2v, cos, sin, gk, bd)


def _lane_cat(blocks):
    return jnp.concatenate(blocks, axis=1)


def _softmax_numer(s, bias, m_run=None):
    es, ms = [], []
    for r in range(GQA_REP):
        sm = s[:, r * LANES:(r + 1) * LANES] + bias
        m = jnp.max(sm, axis=0, keepdims=True)
        if m_run is not None:
            m = jnp.maximum(m, m_run[:, r * LANES:(r + 1) * LANES])
        es.append(jnp.exp2(sm - m))
        ms.append(m)
    return _lane_cat(es), _lane_cat(ms)


def _safe_inv(l):
    return 1.0 / jnp.where(l > 0.0, l, 1.0)


def _attn_body(*refs):
    first = pl.program_id(0) * TILES_PER_STEP
    n_max = refs[1].shape[0] // SLC_CHUNK
    for n in range(1, n_max + 1):
        pl.when((first * Q_TILE) // SLC_CHUNK + 1 == n)(functools.partial(_attn_step, n, first, *refs))


def _attn_step(n_chunks, first, *refs):
    def one_tile(slot, carry):
        tile = _tile_program(n_chunks, first + slot, slot, *refs)
        pending = None
        s_next = tile.scores(tile.units[0])
        for i, unit in enumerate(tile.units):
            s_cur = s_next
            if i + 1 < len(tile.units):
                s_next = tile.scores(tile.units[i + 1])
            e, alpha = tile.softmax(unit, s_cur)
            if pending is not None:
                tile.weighted_values(*pending)
            pending = (unit, e, alpha)
        tile.weighted_values(*pending)
        tile.finish()
        return carry

    lax.fori_loop(0, TILES_PER_STEP, one_tile, 0)


def _tile_program(n_chunks, c, slot, q_ref, ks_ref, kw_ref, vs0_ref, vs1_ref, vw0_ref, vw1_ref, kc_ref,
                  vc0_ref, vc1_ref, gt_ref, ovl_ref, gn_ref, out_ref):
    groups = range(NSA_KV_HEADS)
    q_rows = pl.ds(pl.multiple_of(slot * Q_TILE, Q_TILE), Q_TILE)
    vs_refs, vw_refs, vc_refs = (vs0_ref, vs1_ref), (vw0_ref, vw1_ref), (vc0_ref, vc1_ref)
    sum_row = (HEAD_DIM, 0)
    width = GQA_REP * LANES
    t_lane = c * Q_TILE + lax.broadcasted_iota(jnp.int32, (1, LANES), 1)
    lane_sq = lax.broadcasted_iota(jnp.int32, (Q_TILE, LANES), 1)
    n_live = n_chunks * (SLC_CHUNK // SLC_BLOCK)
    nsel = 32
    win_keys = min(WIN_KEYS, n_chunks * SLC_CHUNK)
    j_idx = lax.broadcasted_iota(jnp.int32, (nsel, LANES), 0)
    cur = lax.shift_right_logical(t_lane, 6)
    n_idx = lax.broadcasted_iota(jnp.int32, (LANES, LANES), 0)
    cbias = jnp.where((n_idx * CMP_STRIDE + (CMP_LEN - 1) <= t_lane) & (n_idx < LANES - 1), 0.0, NEG)
    has_cmp = jnp.where(t_lane >= CMP_LEN - 1, 1.0, 0.0)
    has_cmp = _lane_cat([has_cmp] * GQA_REP)
    win_blk = jnp.maximum(c - WINDOW // Q_TILE, 0)
    win_start = pl.multiple_of(win_blk * Q_TILE, Q_TILE)
    kpos_w = win_start + lax.broadcasted_iota(jnp.int32, (win_keys, LANES), 0)
    wbias = jnp.where((kpos_w <= t_lane) & (kpos_w > t_lane - WINDOW), 0.0, NEG)
    row_sl = lax.broadcasted_iota(jnp.int32, (SLC_CHUNK, LANES), 0)
    zero_q = jnp.zeros((Q_TILE, LANES), BF16)

    def gate_row(g, branch):
        rows = [(g * GQA_REP + r) * 3 + branch for r in range(GQA_REP)]
        return _lane_cat([gt_ref[slot, i:i + 1, :] for i in rows])

    qg = []
    for g in groups:
        in_g = (lane_sq >= HEAD_DIM) if g else (lane_sq < HEAD_DIM)
        qg.append(jnp.concatenate(
            [jnp.where(in_g, q_ref[q_rows, r * LANES:(r + 1) * LANES], zero_q) for r in range(GQA_REP)],
            axis=0))

    o_cmp = [None] * NSA_KV_HEADS
    sel_bias = [None] * NSA_KV_HEADS

    def compressed_softmax(g, sc):
        ec, _ = _softmax_numer(sc, cbias)
        coef = has_cmp * _safe_inv(jnp.sum(ec, axis=0, keepdims=True))
        pc = ec * coef
        psum = (pc[:, 0:LANES] + pc[:, LANES:2 * LANES]
                + pc[:, 2 * LANES:3 * LANES] + pc[:, 3 * LANES:4 * LANES])
        p_hi, p_lo = _split_bf16(psum)
        imp = (_dot(ovl_ref[...], p_hi) + _dot(ovl_ref[...], p_lo))[0:nsel, :]
        valid = j_idx <= cur
        forced = (j_idx == 0) | (j_idx == cur) | (j_idx == cur - 1)
        score = jnp.where(valid, imp + jnp.where(forced, FORCE, 0.0), -FORCE)
        rank = jnp.zeros((nsel, LANES), F32)
        for jp in range(nsel):
            row = score[jp:jp + 1, :]
            beats = (row > score) | ((row == score) & (j_idx > jp))
            rank = rank + jnp.where(beats, 1.0, 0.0)
        sel_bias[g] = jnp.where((rank < float(SLC_TOPN)) & (score > -1.0), 0.0, NEG)
        return ec.astype(BF16), coef

    nsub = SLC_CHUNK // LANES
    blocks_per_chunk = SLC_CHUNK // SLC_BLOCK
    kwin = kw_ref[pl.ds(win_start, win_keys), :]
    units = [("cmp", g, 0) for g in groups] + [("win", g, 0) for g in groups]
    units += [("slc", g, kc) for kc in range(n_chunks) for g in groups]
    m_run = [None] * NSA_KV_HEADS
    acc_s = [None] * NSA_KV_HEADS
    acc_w = [None] * NSA_KV_HEADS

    def scores(unit):
        kind, g, kc = unit
        if kind == "cmp":
            keys = kc_ref[...]
        elif kind == "win":
            keys = kwin
        else:
            keys = ks_ref[kc * SLC_CHUNK:(kc + 1) * SLC_CHUNK, :]
        return _dot_tb(keys, qg[g])

    def softmax(unit, s):
        kind, g, kc = unit
        if kind == "cmp":
            return compressed_softmax(g, s)
        if kind == "win":
            e, _ = _softmax_numer(s, wbias)
            return e.astype(BF16), None
        j0 = kc * blocks_per_chunk
        bias = jnp.concatenate(
            [jnp.broadcast_to(sel_bias[g][j:j + 1, :], (SLC_BLOCK, LANES))
             for j in range(j0, j0 + blocks_per_chunk)], axis=0)
        if kc == n_chunks - 1:
            bias = jnp.where(kc * SLC_CHUNK + row_sl <= t_lane, bias, NEG)
        e, m_new = _softmax_numer(s, bias, m_run[g])
        alpha = None if kc == 0 else jnp.exp2(m_run[g] - m_new)
        m_run[g] = m_new
        return e.astype(BF16), alpha

    def weighted_values(unit, e, alpha):
        kind, g, kc = unit
        if kind == "cmp":
            o_cmp[g] = _dot(vc_refs[g][0], e) * (gate_row(g, 0) * alpha)
            return
        if kind == "win":
            vt = _lane_cat([vw_refs[g][win_blk + i] for i in range(win_keys // LANES)])
            acc_w[g] = _dot(vt, e)
            return
        vt = _lane_cat([vs_refs[g][nsub * kc + i] for i in range(nsub)])
        pv = _dot(vt, e)
        acc_s[g] = pv if alpha is None else acc_s[g] * alpha + pv

    def finish():
        o_groups = []
        for g in groups:
            r0 = sum_row[g]
            o_groups.append(o_cmp[g]
                            + acc_s[g] * (gate_row(g, 1) * _safe_inv(acc_s[g][r0:r0 + 1, :]))
                            + acc_w[g] * (gate_row(g, 2) * _safe_inv(acc_w[g][r0:r0 + 1, :])))
        row_d = lax.broadcasted_iota(jnp.int32, (LANES, width), 0)
        o_t = jnp.where(row_d < HEAD_DIM, o_groups[0], o_groups[1])
        sq = jnp.sum(o_t * o_t, axis=0, keepdims=True)
        ss = (sq[:, 0:LANES] + sq[:, LANES:2 * LANES]
              + sq[:, 2 * LANES:3 * LANES] + sq[:, 3 * LANES:4 * LANES])
        rs = lax.rsqrt(ss * (1.0 / NSA_WIDTH) + EPS)
        for r in range(GQA_REP):
            cols = slice(r * LANES, (r + 1) * LANES)
            out_ref[q_rows, cols] = (o_t[:, cols] * rs * gn_ref[:, cols]).T.astype(BF16)

    return types.SimpleNamespace(units=units, scores=scores, softmax=softmax,
                                 weighted_values=weighted_values, finish=finish)


def _attn_call(q, ks, kw, vs0, vs1, vw0, vw1, kcmp, vc0, vc1, gt, ovl, gn, batch, seq):
    assert SLC_CHUNK % (TILES_PER_STEP * Q_TILE) == 0
    rows = TILES_PER_STEP * Q_TILE
    nq = seq // rows
    nkb = seq // LANES
    qrow = lambda c, b: (b * nq + c, 0)
    per_b = lambda c, b: (b, 0)
    per_b3 = lambda c, b: (b, 0, 0)
    vspec = pl.BlockSpec((nkb, LANES, LANES), per_b3)
    cspec = pl.BlockSpec((1, LANES, LANES), per_b3)
    return pl.pallas_call(
        _attn_body,
        grid=(nq, batch),
        in_specs=[
            pl.BlockSpec((rows, NSA_WIDTH), qrow),
            pl.BlockSpec((seq, LANES), per_b),
            pl.BlockSpec((seq, LANES), per_b),
            vspec, vspec, vspec, vspec,
            pl.BlockSpec((LANES, LANES), per_b),
            cspec, cspec,
            pl.BlockSpec((TILES_PER_STEP, 32, LANES), lambda c, b: (b * nq + c, 0, 0)),
            pl.BlockSpec((LANES, LANES), lambda c, b: (0, 0)),
            pl.BlockSpec((LANES, NSA_WIDTH), lambda c, b: (0, 0)),
        ],
        out_specs=pl.BlockSpec((rows, NSA_WIDTH), qrow),
        out_shape=jax.ShapeDtypeStruct((batch * seq, NSA_WIDTH), BF16),
        compiler_params=pltpu.CompilerParams(
            dimension_semantics=("parallel", "arbitrary"), vmem_limit_bytes=VMEM_LIMIT),
        name="attn",
    )(q, ks, kw, vs0, vs1, vw0, vw1, kcmp, vc0, vc1, gt, ovl, gn)


def _conv_body(tiles_per_seq, cur_ref, halo_ref, w_ref, b_ref, lng_ref, lnb_ref, gn_ref, out_ref,
               buf_ref, sh_ref, cv_ref):
    first = (pl.program_id(0) % tiles_per_seq) == 0
    buf_ref[0:CONV_HALO, :] = jnp.where(first, 0.0, halo_ref[...])
    buf_ref[CONV_HALO:, :] = cur_ref[...]
    lead = CONV_HALO - (CONV_WIDTH - 1)
    ncb = CONV_CH // LANES
    nrb = CONV_TILE // CONV_ROWS
    for s in range(8):
        span = CONV_TILE + 8 * ((CONV_WIDTH - 1 - s) // 8)
        for cb in range(ncb):
            sh_ref[s, cb, 0:span, :] = buf_ref[lead + s:lead + s + span, cb * LANES:(cb + 1) * LANES]

    def conv_block(i, carry):
        cb = i % ncb
        base = pl.multiple_of((i // ncb) * CONV_ROWS, CONV_ROWS)
        acc = jnp.zeros((CONV_ROWS, LANES), F32) + b_ref[cb]
        for k in range(CONV_WIDTH):
            acc = acc + w_ref[cb, k:k + 1, :] * sh_ref[k % 8, cb, pl.ds(base + 8 * (k // 8), CONV_ROWS), :]
        cv_ref[cb, pl.ds(base, CONV_ROWS), :] = acc
        return carry

    lax.fori_loop(0, nrb * ncb, conv_block, 0)
    h = _lane_cat([cv_ref[cb] for cb in range(ncb)])
    mu = jnp.mean(h, axis=-1, keepdims=True)
    d = h - mu
    var = jnp.mean(d * d, axis=-1, keepdims=True)
    hn = d * lax.rsqrt(var + EPS) * lng_ref[...] + lnb_ref[...]
    o = hn * _sigmoid(hn)
    ms = jnp.mean(o * o, axis=-1, keepdims=True)
    out_ref[...] = (o * lax.rsqrt(ms + EPS) * gn_ref[...]).astype(BF16)


def _conv_call(hcv, w, b, lng, lnb, gn, seq):
    t = hcv.shape[0]
    nt = t // CONV_TILE
    halo_per_tile = CONV_TILE // CONV_HALO
    ncb = CONV_CH // LANES
    row = lambda i: (i, 0)
    const = lambda i: (0, 0)
    return pl.pallas_call(
        functools.partial(_conv_body, seq // CONV_TILE),
        grid=(nt,),
        in_specs=[
            pl.BlockSpec((CONV_TILE, CONV_CH), row),
            pl.BlockSpec((CONV_HALO, CONV_CH), lambda i: (jnp.maximum(i * halo_per_tile - 1, 0), 0)),
            pl.BlockSpec((ncb, CONV_HALO, LANES), lambda i: (0, 0, 0)),
            pl.BlockSpec((ncb, 1, LANES), lambda i: (0, 0, 0)),
            pl.BlockSpec((1, CONV_CH), const),
            pl.BlockSpec((1, CONV_CH), const),
            pl.BlockSpec((1, CONV_CH), const),
        ],
        out_specs=pl.BlockSpec((CONV_TILE, CONV_CH), row),
        out_shape=jax.ShapeDtypeStruct((t, CONV_CH), BF16),
        scratch_shapes=[pltpu.VMEM((CONV_HALO + CONV_TILE, CONV_CH), F32),
                        pltpu.VMEM((8, ncb, CONV_TILE + 8 * ((CONV_WIDTH - 1) // 8), LANES), F32),
                        pltpu.VMEM((ncb, CONV_TILE, LANES), F32)],
        compiler_params=pltpu.CompilerParams(
            dimension_semantics=("parallel",), vmem_limit_bytes=VMEM_LIMIT),
        name="conv",
    )(hcv, hcv, w, b, lng, lnb, gn)


def _ffn_body(x_ref, ma_ref, mb_ref, wo_ref, gf_ref, wgu_ref, wd_ref, out_ref):
    h = (x_ref[...] + _dot(ma_ref[...], wo_ref[0:NSA_WIDTH, :])
         + _dot(mb_ref[...], wo_ref[NSA_WIDTH:D_MODEL, :]))
    ms = jnp.mean(h * h, axis=-1, keepdims=True)
    hn = (h * lax.rsqrt(ms + EPS) * gf_ref[...]).astype(BF16)
    acc = jnp.zeros_like(h)
    for j in range(FFN_HIDDEN // FFN_CHUNK):
        c0 = j * FFN_CHUNK
        gte = _dot(hn, wgu_ref[:, c0:c0 + FFN_CHUNK])
        up = _dot(hn, wgu_ref[:, FFN_HIDDEN + c0:FFN_HIDDEN + c0 + FFN_CHUNK])
        act = (gte * _sigmoid(gte) * up).astype(BF16)
        acc = acc + _dot(act, wd_ref[c0:c0 + FFN_CHUNK, :])
    out_ref[...] = h + acc


def _ffn_call(x2, ma, mb, wo, gf, wgu, wd):
    t = x2.shape[0]
    nt = t // FFN_TILE
    row = lambda i: (i, 0)
    const = lambda i: (0, 0)
    once = pl.Buffered(1)
    return pl.pallas_call(
        _ffn_body,
        grid=(nt,),
        in_specs=[
            pl.BlockSpec((FFN_TILE, D_MODEL), row),
            pl.BlockSpec((FFN_TILE, NSA_WIDTH), row),
            pl.BlockSpec((FFN_TILE, CONV_CH), row),
            pl.BlockSpec((D_MODEL, D_MODEL), const, pipeline_mode=once),
            pl.BlockSpec((1, D_MODEL), const),
            pl.BlockSpec((D_MODEL, 2 * FFN_HIDDEN), const, pipeline_mode=once),
            pl.BlockSpec((FFN_HIDDEN, D_MODEL), const, pipeline_mode=once),
        ],
        out_specs=pl.BlockSpec((FFN_TILE, D_MODEL), row),
        out_shape=jax.ShapeDtypeStruct((t, D_MODEL), F32),
        compiler_params=pltpu.CompilerParams(
            dimension_semantics=("parallel",), vmem_limit_bytes=VMEM_LIMIT),
        name="ffn",
    )(x2, ma, mb, wo, gf, wgu, wd)


def _pair_heads(a, axis):
    shape = a.shape
    a = a.reshape(shape[:axis] + (NSA_KV_HEADS, GQA_REP, HEAD_DIM) + shape[axis + 1:])
    return jnp.swapaxes(a, axis, axis + 1).reshape(shape)


def _transposed_gain(g):
    cols = g.reshape(GQA_REP, LANES).T
    return jnp.broadcast_to(cols[:, :, None], (LANES, GQA_REP, Q_TILE)).reshape(LANES, GQA_REP * Q_TILE)


def _rope_tables(pos):
    half = HEAD_DIM // 2
    inv = ROPE_THETA ** (-np.arange(half, dtype=np.float64) / half)
    ang = np.asarray(pos, np.float64)[:, None] * inv[None, :]
    cos = np.concatenate([np.cos(ang), np.cos(ang)], axis=1)
    sin = np.concatenate([-np.sin(ang), np.sin(ang)], axis=1)
    return (np.tile(cos, (1, NSA_KV_HEADS)).astype(np.float32),
            np.tile(sin, (1, NSA_KV_HEADS)).astype(np.float32))


def _block_diag_ones(width):
    idx = np.arange(width) // HEAD_DIM
    return (idx[:, None] == idx[None, :]).astype(np.float32)


def _overlap_t(seq):
    ncmp = (seq - CMP_LEN) // CMP_STRIDE + 1
    cs = np.arange(ncmp)[:, None] * CMP_STRIDE
    ss = np.arange(seq // SLC_BLOCK)[None, :] * SLC_BLOCK
    ov = np.clip(np.minimum(cs + CMP_LEN, ss + SLC_BLOCK) - np.maximum(cs, ss), 0, None) / CMP_LEN
    out = np.zeros((LANES, LANES), np.float32)
    out[:seq // SLC_BLOCK, :ncmp] = ov.T
    return out


def _cmp_weights(pe, w1, w2):
    pe2 = jnp.concatenate([pe, pe], axis=1)
    w1r = w1.reshape(CMP_LEN, HEAD_DIM, CMP_HIDDEN)
    z1 = jnp.zeros_like(w1r)
    w1b = jnp.concatenate([jnp.concatenate([w1r, z1], axis=2),
                           jnp.concatenate([z1, w1r], axis=2)], axis=1).astype(BF16)
    z2 = jnp.zeros_like(w2)
    w2b = jnp.concatenate([jnp.concatenate([w2, z2], axis=1),
                           jnp.concatenate([z2, w2], axis=1)], axis=0).astype(BF16)
    w1b = w1b.reshape(2, (CMP_LEN // 2) * LANES, 2 * CMP_HIDDEN)
    return pe2, w1b, w2b


def kernel(x, attn_norm_g, w_in, q_norm_g, k_norm_cmp_g, k_norm_slc_g, k_norm_win_g, cmp_pe_k, cmp_w1_k, cmp_w2_k, cmp_pe_v, cmp_w1_v, cmp_w2_v, conv_dw_w, conv_dw_b, conv_ln_g, conv_ln_b, out_norm_nsa_g, out_norm_conv_g, w_out, ffn_norm_g, w_gate_up, w_down):
    batch, seq, d_model = x.shape
    assert d_model == D_MODEL and seq % PROJ_TILE == 0 and seq // SLC_BLOCK == 32
    depth = w_in.shape[0]
    cos_np, sin_np = _rope_tables(np.arange(seq))
    ccos_np, csin_np = _rope_tables(np.arange(seq // CMP_STRIDE) * CMP_STRIDE + CMP_LEN - 1)
    ccos_np, csin_np = np.tile(ccos_np, (CMP_SEQS, 1)), np.tile(csin_np, (CMP_SEQS, 1))
    cos, sin, ccos, csin = map(jnp.asarray, (cos_np, sin_np, ccos_np, csin_np))
    bd256 = jnp.asarray(_block_diag_ones(256), BF16)
    bd128 = jnp.asarray(_block_diag_ones(LANES), BF16)
    ovl = jnp.asarray(_overlap_t(seq), BF16)

    x2 = x.reshape(batch * seq, d_model)
    for l in range(depth):
        w = w_in[l]
        gq = jnp.tile(q_norm_g[l], 4)[None, :]
        gk = jnp.concatenate([jnp.tile(k_norm_slc_g[l], 2), jnp.tile(k_norm_win_g[l], 2)])[None, :]
        q, ks, kw, vs0, vs1, vw0, vw1, kc, vc, gt, hcv, wgu_b, wd_b, wo_b = _proj_call(
            x2, attn_norm_g[l][None, :], w.T, cos, sin, gq, gk, bd256,
            w_gate_up[l], w_down[l], w_out[l], seq)

        pek, w1k, w2k = _cmp_weights(cmp_pe_k[l], cmp_w1_k[l], cmp_w2_k[l])
        pev, w1v, w2v = _cmp_weights(cmp_pe_v[l], cmp_w1_v[l], cmp_w2_v[l])
        kcmp, vc0, vc1 = _cmp_call(kc, vc, pek, w1k, w2k, pev, w1v, w2v, ccos, csin,
                                jnp.tile(k_norm_cmp_g[l], 2)[None, :], bd128, batch, seq)

        mix_a = _attn_call(q, ks, kw, vs0, vs1, vw0, vw1, kcmp, vc0, vc1, gt, ovl,
                           _transposed_gain(_pair_heads(out_norm_nsa_g[l], 0)), batch, seq)

        w_dw = jnp.concatenate([conv_dw_w[l][:, 0, :],
                                jnp.zeros((CONV_HALO - CONV_WIDTH, CONV_CH), F32)], axis=0)
        w_dw = w_dw.reshape(CONV_HALO, CONV_CH // LANES, LANES).transpose(1, 0, 2)
        mix_b = _conv_call(hcv, w_dw, conv_dw_b[l].reshape(CONV_CH // LANES, 1, LANES),
                           conv_ln_g[l][None, :],
                           conv_ln_b[l][None, :], out_norm_conv_g[l][None, :], seq)

        x2 = _ffn_call(x2, mix_a, mix_b, wo_b, ffn_norm_g[l][None, :], wgu_b, wd_b)
    return x2.reshape(batch, seq, d_model)
```

```python
import functools
import types

import numpy as np
import jax
import jax.numpy as jnp
from jax import lax
from jax.experimental import pallas as pl
from jax.experimental.pallas import tpu as pltpu

F32 = jnp.float32
BF16 = jnp.bfloat16

D_MODEL = 1024
HEAD_DIM = 64
NSA_HEADS = 8
NSA_KV_HEADS = 2
GQA_REP = NSA_HEADS // NSA_KV_HEADS
NSA_WIDTH = NSA_HEADS * HEAD_DIM
CONV_CH = D_MODEL - NSA_WIDTH
CMP_LEN = 32
CMP_STRIDE = 16
CMP_HIDDEN = 4 * HEAD_DIM
SLC_BLOCK = 64
SLC_TOPN = 8
WINDOW = 512
CONV_WIDTH = 31
FFN_HIDDEN = 2816
ROPE_THETA = 10000.0
EPS = 1e-6
NEG = -1e30
LOG2E = 1.4426950408889634
FORCE = 1e6

LANES = 128
Q_TILE = 128
TILES_PER_STEP = 2
SLC_CHUNK = 256
WIN_KEYS = WINDOW + Q_TILE
PROJ_TILE = 512
CMP_SEQS = 4
CONV_TILE = 512
CONV_HALO = 32
CONV_ROWS = 256
FFN_TILE = 512
FFN_CHUNK = 256
VMEM_LIMIT = 56 * 1024 * 1024

C_Q = 0
C_CMP = 512
C_SLC = 768
C_WIN = 1024
MAIN_COLS = 1280
GATE_COLS = 3 * NSA_HEADS

_TRANS_B = (((1,), (1,)), ((), ()))


def _dot(a, b):
    return jnp.dot(a, b, preferred_element_type=F32)


def _dot_tb(a, b):
    return lax.dot_general(a, b, _TRANS_B, preferred_element_type=F32)


def _split_bf16(x):
    hi = x.astype(BF16)
    lo = (x - hi.astype(F32)).astype(BF16)
    return hi, lo


def _sigmoid(x):
    return 1.0 / (1.0 + jnp.exp(-x))


def _head_norm_rope(y, gain, bd, cos, sin, first_half, scale):
    ss = _dot((y * y).astype(BF16), bd)
    yn = y * lax.rsqrt(ss * (1.0 / HEAD_DIM) + EPS) * gain
    outs = []
    for j in range(y.shape[1] // LANES):
        blk = yn[:, j * LANES:(j + 1) * LANES]
        rot = jnp.where(first_half, pltpu.roll(blk, LANES - HEAD_DIM // 2, 1),
                        pltpu.roll(blk, HEAD_DIM // 2, 1))
        outs.append((blk * cos + rot * sin) * scale)
    return outs


def _value_t_with_ones(v):
    vt = v.T
    row = lax.broadcasted_iota(jnp.int32, vt.shape, 0)
    return (jnp.where(row < HEAD_DIM, vt, 1.0).astype(BF16),
            jnp.where(row < HEAD_DIM, 1.0, vt).astype(BF16))


def _proj_body(x_ref, gin_ref, w_ref, cos_ref, sin_ref, gq_ref, gk_ref, bd_ref,
               wgu_in_ref, wd_in_ref, wo_in_ref,
               q_ref, ks_ref, kw_ref, vs0_ref, vs1_ref, vw0_ref, vw1_ref, kc_ref, vc_ref, gt_ref,
               hcv_ref, wgu_out_ref, wd_out_ref, wo_out_ref, wm_ref, wc_ref, wg_ref):
    @pl.when(pl.program_id(0) == 0)
    def _():
        for j in range(MAIN_COLS // LANES):
            wm_ref[:, j * LANES:(j + 1) * LANES] = w_ref[j * LANES:(j + 1) * LANES, :].T.astype(BF16)
        gl = jnp.concatenate([w_ref[MAIN_COLS:MAIN_COLS + GATE_COLS, :],
                              jnp.zeros((LANES - GATE_COLS, D_MODEL), F32)], axis=0)
        wg_ref[...] = gl.T.astype(BF16)
        for j in range(2 * CONV_CH // LANES):
            r0 = MAIN_COLS + GATE_COLS + j * LANES
            wc_ref[:, j * LANES:(j + 1) * LANES] = w_ref[r0:r0 + LANES, :].T.astype(BF16)

    wgu_out_ref[...] = wgu_in_ref[...].astype(BF16)
    wd_out_ref[...] = wd_in_ref[...].astype(BF16)
    wo_out_ref[...] = wo_in_ref[...].astype(BF16)

    x = x_ref[...]
    xg = (x * gin_ref[...]).astype(BF16)
    rs = lax.rsqrt(jnp.mean(x * x, axis=-1, keepdims=True) + EPS)
    cos = cos_ref[...]
    sin = sin_ref[...]
    bd = bd_ref[...]
    lane = lax.broadcasted_iota(jnp.int32, (PROJ_TILE, LANES), 1)
    first_half = (lane & (HEAD_DIM // 2)) == 0
    held = {}

    def queries(group, y):
        held["q", group] = _head_norm_rope(y, gq_ref[...], bd, cos, sin, first_half,
                                           HEAD_DIM ** -0.5 * LOG2E)
        if group == 0:
            return
        low = lane < HEAD_DIM
        for j in range(2):
            a, b = held["q", 0][j], held["q", 1][j]
            q_ref[:, (2 * j) * LANES:(2 * j + 1) * LANES] = jnp.where(
                low, a, pltpu.roll(b, HEAD_DIM, 1)).astype(BF16)
            q_ref[:, (2 * j + 1) * LANES:(2 * j + 2) * LANES] = jnp.where(
                low, pltpu.roll(a, HEAD_DIM, 1), b).astype(BF16)

    def values(y, out0_ref, out1_ref):
        for i in range(PROJ_TILE // LANES):
            out0_ref[i], out1_ref[i] = _value_t_with_ones(y[i * LANES:(i + 1) * LANES, :])

    def slc_kv(y):
        held["ks"] = y[:, :LANES]
        values(y[:, LANES:], vs0_ref, vs1_ref)

    def win_kv(y):
        kblks = _head_norm_rope(_lane_cat([held["ks"], y[:, :LANES]]), gk_ref[...], bd, cos, sin,
                                first_half, 1.0)
        ks_ref[...] = kblks[0].astype(BF16)
        kw_ref[...] = kblks[1].astype(BF16)
        values(y[:, LANES:], vw0_ref, vw1_ref)

    def cmp_inputs(y):
        kc_ref[...] = y[:, :LANES]
        vc_ref[...] = y[:, LANES:]

    def conv_lin(y):
        held["ca"] = y

    def conv_glu(y):
        hcv_ref[...] = held["ca"] * _sigmoid(y)

    def gates(y):
        sg = _sigmoid(y)
        for i in range(PROJ_TILE // LANES):
            gt_ref[i] = sg[i * LANES:(i + 1) * LANES, :].T[0:32, :]

    sections = [
        (wm_ref, C_Q, 256, functools.partial(queries, 0)),
        (wm_ref, C_Q + 256, 256, functools.partial(queries, 1)),
        (wc_ref, 0, CONV_CH, conv_lin),
        (wm_ref, C_SLC, 256, slc_kv),
        (wc_ref, CONV_CH, CONV_CH, conv_glu),
        (wm_ref, C_WIN, 256, win_kv),
        (wm_ref, C_CMP, 256, cmp_inputs),
        (wg_ref, 0, LANES, gates),
    ]

    def project(sec):
        w_ref, c0, width, _ = sec
        return _dot(xg, w_ref[:, c0:c0 + width])

    y_next = project(sections[0])
    for i, sec in enumerate(sections):
        y_cur = y_next
        if i + 1 < len(sections):
            y_next = project(sections[i + 1])
        sec[3](y_cur * rs)


def _paired_row_block(i):
    head = i // 2
    src = ((head % NSA_KV_HEADS) * GQA_REP + head // NSA_KV_HEADS) * 2 + i % 2
    return jnp.where(i < NSA_WIDTH // 32, src, i)


def _proj_call(x2, gin, w_in, cos, sin, gq, gk, bd, w_gate_up, w_down, w_out, seq):
    t = x2.shape[0]
    nt = t // PROJ_TILE
    assert D_MODEL // nt == 32 and nt % 16 == 0 and (FFN_HIDDEN // 16) % 16 == 0
    wd_rows = FFN_HIDDEN // 16
    tiles_per_seq = seq // PROJ_TILE
    row = lambda i: (i, 0)
    const = lambda i: (0, 0)
    tab = lambda i: (i % tiles_per_seq, 0)
    blk3 = lambda i: (i, 0, 0)
    wd_blk = lambda i: (i * 16 // nt, 0)
    nb = PROJ_TILE // LANES
    return pl.pallas_call(
        _proj_body,
        grid=(nt,),
        in_specs=[
            pl.BlockSpec((PROJ_TILE, D_MODEL), row),
            pl.BlockSpec((1, D_MODEL), const),
            pl.BlockSpec(w_in.shape, const, pipeline_mode=pl.Buffered(1)),
            pl.BlockSpec((PROJ_TILE, LANES), tab),
            pl.BlockSpec((PROJ_TILE, LANES), tab),
            pl.BlockSpec((1, 256), const),
            pl.BlockSpec((1, 256), const),
            pl.BlockSpec((256, 256), const),
            pl.BlockSpec((32, 2 * FFN_HIDDEN), row),
            pl.BlockSpec((wd_rows, D_MODEL), wd_blk),
            pl.BlockSpec((32, D_MODEL), lambda i: (_paired_row_block(i), 0)),
        ],
        out_specs=[
            pl.BlockSpec((PROJ_TILE, NSA_WIDTH), row),
            pl.BlockSpec((PROJ_TILE, LANES), row),
            pl.BlockSpec((PROJ_TILE, LANES), row),
            pl.BlockSpec((nb, LANES, LANES), blk3),
            pl.BlockSpec((nb, LANES, LANES), blk3),
            pl.BlockSpec((nb, LANES, LANES), blk3),
            pl.BlockSpec((nb, LANES, LANES), blk3),
            pl.BlockSpec((PROJ_TILE, LANES), row),
            pl.BlockSpec((PROJ_TILE, LANES), row),
            pl.BlockSpec((nb, 32, LANES), blk3),
            pl.BlockSpec((PROJ_TILE, CONV_CH), row),
            pl.BlockSpec((32, 2 * FFN_HIDDEN), row),
            pl.BlockSpec((wd_rows, D_MODEL), wd_blk),
            pl.BlockSpec((32, D_MODEL), row),
        ],
        out_shape=[
            jax.ShapeDtypeStruct((t, NSA_WIDTH), BF16),
            jax.ShapeDtypeStruct((t, LANES), BF16),
            jax.ShapeDtypeStruct((t, LANES), BF16),
            jax.ShapeDtypeStruct((t // LANES, LANES, LANES), BF16),
            jax.ShapeDtypeStruct((t // LANES, LANES, LANES), BF16),
            jax.ShapeDtypeStruct((t // LANES, LANES, LANES), BF16),
            jax.ShapeDtypeStruct((t // LANES, LANES, LANES), BF16),
            jax.ShapeDtypeStruct((t, LANES), F32),
            jax.ShapeDtypeStruct((t, LANES), F32),
            jax.ShapeDtypeStruct((t // LANES, 32, LANES), F32),
            jax.ShapeDtypeStruct((t, CONV_CH), F32),
            jax.ShapeDtypeStruct((D_MODEL, 2 * FFN_HIDDEN), BF16),
            jax.ShapeDtypeStruct((FFN_HIDDEN, D_MODEL), BF16),
            jax.ShapeDtypeStruct((D_MODEL, D_MODEL), BF16),
        ],
        scratch_shapes=[pltpu.VMEM((D_MODEL, MAIN_COLS), BF16),
                        pltpu.VMEM((D_MODEL, 2 * CONV_CH), BF16),
                        pltpu.VMEM((D_MODEL, LANES), BF16)],
        compiler_params=pltpu.CompilerParams(
            dimension_semantics=("arbitrary",), vmem_limit_bytes=VMEM_LIMIT),
        name="proj",
    )(x2, gin, w_in, cos, sin, gq, gk, bd, w_gate_up, w_down, w_out)


def _cmp_mlp(t_ref, pe_ref, w1_ref, w2_ref):
    half = CMP_LEN // 2
    nchunk = t_ref.shape[0] // CMP_STRIDE
    toks = [t_ref[pl.ds(l, nchunk, stride=CMP_STRIDE), :] for l in range(half)]
    xp = _lane_cat([(toks[l] + pe_ref[l:l + 1, :]).astype(BF16) for l in range(half)])
    xq = _lane_cat([(toks[l] + pe_ref[half + l:half + l + 1, :]).astype(BF16) for l in range(half)])
    h1 = _dot(xp, w1_ref[0]) + pltpu.roll(_dot(xq, w1_ref[1]), nchunk - 1, 0)
    act = (h1 * _sigmoid(h1)).astype(BF16)
    return _dot(act, w2_ref[...])


def _cmp_body(kc_ref, vc_ref, pek_ref, w1k_ref, w2k_ref, pev_ref, w1v_ref, w2v_ref,
              cos_ref, sin_ref, gk_ref, bd_ref, ko_ref, vt0_ref, vt1_ref):
    kraw = _cmp_mlp(kc_ref, pek_ref, w1k_ref, w2k_ref)
    lane = lax.broadcasted_iota(jnp.int32, kraw.shape, 1)
    first_half = (lane & (HEAD_DIM // 2)) == 0
    (kr,) = _head_norm_rope(kraw, gk_ref[...], bd_ref[...], cos_ref[...], sin_ref[...],
                            first_half, 1.0)
    ko_ref[...] = kr.astype(BF16)
    vraw = _cmp_mlp(vc_ref, pev_ref, w1v_ref, w2v_ref)
    for i in range(CMP_SEQS):
        vt0_ref[i], vt1_ref[i] = _value_t_with_ones(vraw[i * LANES:(i + 1) * LANES, :])


def _cmp_call(kc, vc, pek, w1k, w2k, pev, w1v, w2v, cos, sin, gk, bd, batch, seq):
    seqblk = lambda b: (b, 0)
    c2 = lambda b: (0, 0)
    c3 = lambda b: (0, 0, 0)
    ncmp_pad = seq // CMP_STRIDE
    assert ncmp_pad == LANES and batch % CMP_SEQS == 0
    rows = CMP_SEQS * ncmp_pad
    half_k = (CMP_LEN // 2) * LANES
    return pl.pallas_call(
        _cmp_body,
        grid=(batch // CMP_SEQS,),
        in_specs=[
            pl.BlockSpec((CMP_SEQS * seq, LANES), seqblk),
            pl.BlockSpec((CMP_SEQS * seq, LANES), seqblk),
            pl.BlockSpec((CMP_LEN, LANES), c2),
            pl.BlockSpec((2, half_k, 2 * CMP_HIDDEN), c3),
            pl.BlockSpec((2 * CMP_HIDDEN, LANES), c2),
            pl.BlockSpec((CMP_LEN, LANES), c2),
            pl.BlockSpec((2, half_k, 2 * CMP_HIDDEN), c3),
            pl.BlockSpec((2 * CMP_HIDDEN, LANES), c2),
            pl.BlockSpec((rows, LANES), c2),
            pl.BlockSpec((rows, LANES), c2),
            pl.BlockSpec((1, LANES), c2),
            pl.BlockSpec((LANES, LANES), c2),
        ],
        out_specs=[
            pl.BlockSpec((rows, LANES), seqblk),
            pl.BlockSpec((CMP_SEQS, LANES, ncmp_pad), lambda b: (b, 0, 0)),
            pl.BlockSpec((CMP_SEQS, LANES, ncmp_pad), lambda b: (b, 0, 0)),
        ],
        out_shape=[
            jax.ShapeDtypeStruct((batch * ncmp_pad, LANES), BF16),
            jax.ShapeDtypeStruct((batch, LANES, ncmp_pad), BF16),
            jax.ShapeDtypeStruct((batch, LANES, ncmp_pad), BF16),
        ],
        compiler_params=pltpu.CompilerParams(
            dimension_semantics=("parallel",), vmem_limit_bytes=VMEM_LIMIT),
        name="compress",
    )(kc, vc, pek, w1k, w2k, pev, w1v, w2v, cos, sin, gk, bd)


def _lane_cat(blocks):
    return jnp.concatenate(blocks, axis=1)


def _softmax_numer(s, bias, m_run=None):
    es, ms = [], []
    for r in range(GQA_REP):
        sm = s[:, r * LANES:(r + 1) * LANES] + bias
        m = jnp.max(sm, axis=0, keepdims=True)
        if m_run is not None:
            m = jnp.maximum(m, m_run[:, r * LANES:(r + 1) * LANES])
        es.append(jnp.exp2(sm - m))
        ms.append(m)
    return _lane_cat(es), _lane_cat(ms)


def _safe_inv(l):
    return 1.0 / jnp.where(l > 0.0, l, 1.0)


def _attn_body(*refs):
    first = pl.program_id(0) * TILES_PER_STEP
    n_max = refs[1].shape[0] // SLC_CHUNK
    for n in range(1, n_max + 1):
        pl.when((first * Q_TILE) // SLC_CHUNK + 1 == n)(functools.partial(_attn_step, n, first, *refs))


def _attn_step(n_chunks, first, *refs):
    def one_tile(slot, carry):
        tile = _tile_program(n_chunks, first + slot, slot, *refs)
        pending = None
        s_next = tile.scores(tile.units[0])
        for i, unit in enumerate(tile.units):
            s_cur = s_next
            if i + 1 < len(tile.units):
                s_next = tile.scores(tile.units[i + 1])
            e, alpha = tile.softmax(unit, s_cur)
            if pending is not None:
                tile.weighted_values(*pending)
            pending = (unit, e, alpha)
        tile.weighted_values(*pending)
        tile.finish()
        return carry

    lax.fori_loop(0, TILES_PER_STEP, one_tile, 0)


def _tile_program(n_chunks, c, slot, q_ref, ks_ref, kw_ref, vs0_ref, vs1_ref, vw0_ref, vw1_ref, kc_ref,
                  vc0_ref, vc1_ref, gt_ref, ovl_ref, gn_ref, out_ref):
    groups = range(NSA_KV_HEADS)
    q_rows = pl.ds(pl.multiple_of(slot * Q_TILE, Q_TILE), Q_TILE)
    vs_refs, vw_refs, vc_refs = (vs0_ref, vs1_ref), (vw0_ref, vw1_ref), (vc0_ref, vc1_ref)
    sum_row = (HEAD_DIM, 0)
    width = GQA_REP * LANES
    t_lane = c * Q_TILE + lax.broadcasted_iota(jnp.int32, (1, LANES), 1)
    lane_sq = lax.broadcasted_iota(jnp.int32, (Q_TILE, LANES), 1)
    n_live = n_chunks * (SLC_CHUNK // SLC_BLOCK)
    nsel = 32
    win_keys = min(WIN_KEYS, n_chunks * SLC_CHUNK)
    j_idx = lax.broadcasted_iota(jnp.int32, (nsel, LANES), 0)
    cur = lax.shift_right_logical(t_lane, 6)
    n_idx = lax.broadcasted_iota(jnp.int32, (LANES, LANES), 0)
    cbias = jnp.where((n_idx * CMP_STRIDE + (CMP_LEN - 1) <= t_lane) & (n_idx < LANES - 1), 0.0, NEG)
    has_cmp = jnp.where(t_lane >= CMP_LEN - 1, 1.0, 0.0)
    has_cmp = _lane_cat([has_cmp] * GQA_REP)
    win_blk = jnp.maximum(c - WINDOW // Q_TILE, 0)
    win_start = pl.multiple_of(win_blk * Q_TILE, Q_TILE)
    kpos_w = win_start + lax.broadcasted_iota(jnp.int32, (win_keys, LANES), 0)
    wbias = jnp.where((kpos_w <= t_lane) & (kpos_w > t_lane - WINDOW), 0.0, NEG)
    row_sl = lax.broadcasted_iota(jnp.int32, (SLC_CHUNK, LANES), 0)
    zero_q = jnp.zeros((Q_TILE, LANES), BF16)

    def gate_row(g, branch):
        rows = [(g * GQA_REP + r) * 3 + branch for r in range(GQA_REP)]
        return _lane_cat([gt_ref[slot, i:i + 1, :] for i in rows])

    qg = []
    for g in groups:
        in_g = (lane_sq >= HEAD_DIM) if g else (lane_sq < HEAD_DIM)
        qg.append(jnp.concatenate(
            [jnp.where(in_g, q_ref[q_rows, r * LANES:(r + 1) * LANES], zero_q) for r in range(GQA_REP)],
            axis=0))

    o_cmp = [None] * NSA_KV_HEADS
    sel_bias = [None] * NSA_KV_HEADS

    def compressed_softmax(g, sc):
        ec, _ = _softmax_numer(sc, cbias)
        coef = has_cmp * _safe_inv(jnp.sum(ec, axis=0, keepdims=True))
        pc = ec * coef
        psum = (pc[:, 0:LANES] + pc[:, LANES:2 * LANES]
                + pc[:, 2 * LANES:3 * LANES] + pc[:, 3 * LANES:4 * LANES])
        p_hi, p_lo = _split_bf16(psum)
        imp = (_dot(ovl_ref[...], p_hi) + _dot(ovl_ref[...], p_lo))[0:nsel, :]
        valid = j_idx <= cur
        forced = (j_idx == 0) | (j_idx == cur) | (j_idx == cur - 1)
        score = jnp.where(valid, imp + jnp.where(forced, FORCE, 0.0), -FORCE)
        rank = jnp.zeros((nsel, LANES), F32)
        for jp in range(nsel):
            row = score[jp:jp + 1, :]
            beats = (row > score) | ((row == score) & (j_idx > jp))
            rank = rank + jnp.where(beats, 1.0, 0.0)
        sel_bias[g] = jnp.where((rank < float(SLC_TOPN)) & (score > -1.0), 0.0, NEG)
        return ec.astype(BF16), coef

    nsub = SLC_CHUNK // LANES
    blocks_per_chunk = SLC_CHUNK // SLC_BLOCK
    kwin = kw_ref[pl.ds(win_start, win_keys), :]
    units = [("cmp", g, 0) for g in groups] + [("win", g, 0) for g in groups]
    units += [("slc", g, kc) for kc in range(n_chunks) for g in groups]
    m_run = [None] * NSA_KV_HEADS
    acc_s = [None] * NSA_KV_HEADS
    acc_w = [None] * NSA_KV_HEADS

    def scores(unit):
        kind, g, kc = unit
        if kind == "cmp":
            keys = kc_ref[...]
        elif kind == "win":
            keys = kwin
        else:
            keys = ks_ref[kc * SLC_CHUNK:(kc + 1) * SLC_CHUNK, :]
        return _dot_tb(keys, qg[g])

    def softmax(unit, s):
        kind, g, kc = unit
        if kind == "cmp":
            return compressed_softmax(g, s)
        if kind == "win":
            e, _ = _softmax_numer(s, wbias)
            return e.astype(BF16), None
        j0 = kc * blocks_per_chunk
        bias = jnp.concatenate(
            [jnp.broadcast_to(sel_bias[g][j:j + 1, :], (SLC_BLOCK, LANES))
             for j in range(j0, j0 + blocks_per_chunk)], axis=0)
        if kc == n_chunks - 1:
            bias = jnp.where(kc * SLC_CHUNK + row_sl <= t_lane, bias, NEG)
        e, m_new = _softmax_numer(s, bias, m_run[g])
        alpha = None if kc == 0 else jnp.exp2(m_run[g] - m_new)
        m_run[g] = m_new
        return e.astype(BF16), alpha

    def weighted_values(unit, e, alpha):
        kind, g, kc = unit
        if kind == "cmp":
            o_cmp[g] = _dot(vc_refs[g][0], e) * (gate_row(g, 0) * alpha)
            return
        if kind == "win":
            vt = _lane_cat([vw_refs[g][win_blk + i] for i in range(win_keys // LANES)])
            acc_w[g] = _dot(vt, e)
            return
        vt = _lane_cat([vs_refs[g][nsub * kc + i] for i in range(nsub)])
        pv = _dot(vt, e)
        acc_s[g] = pv if alpha is None else acc_s[g] * alpha + pv

    def finish():
        o_groups = []
        for g in groups:
            r0 = sum_row[g]
            o_groups.append(o_cmp[g]
                            + acc_s[g] * (gate_row(g, 1) * _safe_inv(acc_s[g][r0:r0 + 1, :]))
                            + acc_w[g] * (gate_row(g, 2) * _safe_inv(acc_w[g][r0:r0 + 1, :])))
        row_d = lax.broadcasted_iota(jnp.int32, (LANES, width), 0)
        o_t = jnp.where(row_d < HEAD_DIM, o_groups[0], o_groups[1])
        sq = jnp.sum(o_t * o_t, axis=0, keepdims=True)
        ss = (sq[:, 0:LANES] + sq[:, LANES:2 * LANES]
              + sq[:, 2 * LANES:3 * LANES] + sq[:, 3 * LANES:4 * LANES])
        rs = lax.rsqrt(ss * (1.0 / NSA_WIDTH) + EPS)
        for r in range(GQA_REP):
            cols = slice(r * LANES, (r + 1) * LANES)
            out_ref[q_rows, cols] = (o_t[:, cols] * rs * gn_ref[:, cols]).T.astype(BF16)

    return types.SimpleNamespace(units=units, scores=scores, softmax=softmax,
                                 weighted_values=weighted_values, finish=finish)


def _attn_call(q, ks, kw, vs0, vs1, vw0, vw1, kcmp, vc0, vc1, gt, ovl, gn, batch, seq):
    assert SLC_CHUNK % (TILES_PER_STEP * Q_TILE) == 0
    rows = TILES_PER_STEP * Q_TILE
    nq = seq // rows
    nkb = seq // LANES
    qrow = lambda c, b: (b * nq + c, 0)
    per_b = lambda c, b: (b, 0)
    per_b3 = lambda c, b: (b, 0, 0)
    vspec = pl.BlockSpec((nkb, LANES, LANES), per_b3)
    cspec = pl.BlockSpec((1, LANES, LANES), per_b3)
    return pl.pallas_call(
        _attn_body,
        grid=(nq, batch),
        in_specs=[
            pl.BlockSpec((rows, NSA_WIDTH), qrow),
            pl.BlockSpec((seq, LANES), per_b),
            pl.BlockSpec((seq, LANES), per_b),
            vspec, vspec, vspec, vspec,
            pl.BlockSpec((LANES, LANES), per_b),
            cspec, cspec,
            pl.BlockSpec((TILES_PER_STEP, 32, LANES), lambda c, b: (b * nq + c, 0, 0)),
            pl.BlockSpec((LANES, LANES), lambda c, b: (0, 0)),
            pl.BlockSpec((LANES, NSA_WIDTH), lambda c, b: (0, 0)),
        ],
        out_specs=pl.BlockSpec((rows, NSA_WIDTH), qrow),
        out_shape=jax.ShapeDtypeStruct((batch * seq, NSA_WIDTH), BF16),
        compiler_params=pltpu.CompilerParams(
            dimension_semantics=("parallel", "arbitrary"), vmem_limit_bytes=VMEM_LIMIT),
        name="attn",
    )(q, ks, kw, vs0, vs1, vw0, vw1, kcmp, vc0, vc1, gt, ovl, gn)


def _conv_body(tiles_per_seq, cur_ref, halo_ref, w_ref, b_ref, lng_ref, lnb_ref, gn_ref, out_ref,
               buf_ref, sh_ref, cv_ref):
    first = (pl.program_id(0) % tiles_per_seq) == 0
    buf_ref[0:CONV_HALO, :] = jnp.where(first, 0.0, halo_ref[...])
    buf_ref[CONV_HALO:, :] = cur_ref[...]
    lead = CONV_HALO - (CONV_WIDTH - 1)
    ncb = CONV_CH // LANES
    nrb = CONV_TILE // CONV_ROWS
    for s in range(8):
        span = CONV_TILE + 8 * ((CONV_WIDTH - 1 - s) // 8)
        for cb in range(ncb):
            sh_ref[s, cb, 0:span, :] = buf_ref[lead + s:lead + s + span, cb * LANES:(cb + 1) * LANES]

    def conv_block(i, carry):
        cb = i % ncb
        base = pl.multiple_of((i // ncb) * CONV_ROWS, CONV_ROWS)
        acc = jnp.zeros((CONV_ROWS, LANES), F32) + b_ref[cb]
        for k in range(CONV_WIDTH):
            acc = acc + w_ref[cb, k:k + 1, :] * sh_ref[k % 8, cb, pl.ds(base + 8 * (k // 8), CONV_ROWS), :]
        cv_ref[cb, pl.ds(base, CONV_ROWS), :] = acc
        return carry

    lax.fori_loop(0, nrb * ncb, conv_block, 0)
    h = _lane_cat([cv_ref[cb] for cb in range(ncb)])
    mu = jnp.mean(h, axis=-1, keepdims=True)
    d = h - mu
    var = jnp.mean(d * d, axis=-1, keepdims=True)
    hn = d * lax.rsqrt(var + EPS) * lng_ref[...] + lnb_ref[...]
    o = hn * _sigmoid(hn)
    ms = jnp.mean(o * o, axis=-1, keepdims=True)
    out_ref[...] = (o * lax.rsqrt(ms + EPS) * gn_ref[...]).astype(BF16)


def _conv_call(hcv, w, b, lng, lnb, gn, seq):
    t = hcv.shape[0]
    nt = t // CONV_TILE
    halo_per_tile = CONV_TILE // CONV_HALO
    ncb = CONV_CH // LANES
    row = lambda i: (i, 0)
    const = lambda i: (0, 0)
    return pl.pallas_call(
        functools.partial(_conv_body, seq // CONV_TILE),
        grid=(nt,),
        in_specs=[
            pl.BlockSpec((CONV_TILE, CONV_CH), row),
            pl.BlockSpec((CONV_HALO, CONV_CH), lambda i: (jnp.maximum(i * halo_per_tile - 1, 0), 0)),
            pl.BlockSpec((ncb, CONV_HALO, LANES), lambda i: (0, 0, 0)),
            pl.BlockSpec((ncb, 1, LANES), lambda i: (0, 0, 0)),
            pl.BlockSpec((1, CONV_CH), const),
            pl.BlockSpec((1, CONV_CH), const),
            pl.BlockSpec((1, CONV_CH), const),
        ],
        out_specs=pl.BlockSpec((CONV_TILE, CONV_CH), row),
        out_shape=jax.ShapeDtypeStruct((t, CONV_CH), BF16),
        scratch_shapes=[pltpu.VMEM((CONV_HALO + CONV_TILE, CONV_CH), F32),
                        pltpu.VMEM((8, ncb, CONV_TILE + 8 * ((CONV_WIDTH - 1) // 8), LANES), F32),
                        pltpu.VMEM((ncb, CONV_TILE, LANES), F32)],
        compiler_params=pltpu.CompilerParams(
            dimension_semantics=("parallel",), vmem_limit_bytes=VMEM_LIMIT),
        name="conv",
    )(hcv, hcv, w, b, lng, lnb, gn)


def _ffn_body(x_ref, ma_ref, mb_ref, wo_ref, gf_ref, wgu_ref, wd_ref, out_ref):
    h = (x_ref[...] + _dot(ma_ref[...], wo_ref[0:NSA_WIDTH, :])
         + _dot(mb_ref[...], wo_ref[NSA_WIDTH:D_MODEL, :]))
    ms = jnp.mean(h * h, axis=-1, keepdims=True)
    hn = (h * lax.rsqrt(ms + EPS) * gf_ref[...]).astype(BF16)
    acc = jnp.zeros_like(h)
    for j in range(FFN_HIDDEN // FFN_CHUNK):
        c0 = j * FFN_CHUNK
        gte = _dot(hn, wgu_ref[:, c0:c0 + FFN_CHUNK])
        up = _dot(hn, wgu_ref[:, FFN_HIDDEN + c0:FFN_HIDDEN + c0 + FFN_CHUNK])
        act = (gte * _sigmoid(gte) * up).astype(BF16)
        acc = acc + _dot(act, wd_ref[c0:c0 + FFN_CHUNK, :])
    out_ref[...] = h + acc


def _ffn_call(x2, ma, mb, wo, gf, wgu, wd):
    t = x2.shape[0]
    nt = t // FFN_TILE
    row = lambda i: (i, 0)
    const = lambda i: (0, 0)
    once = pl.Buffered(1)
    return pl.pallas_call(
        _ffn_body,
        grid=(nt,),
        in_specs=[
            pl.BlockSpec((FFN_TILE, D_MODEL), row),
            pl.BlockSpec((FFN_TILE, NSA_WIDTH), row),
            pl.BlockSpec((FFN_TILE, CONV_CH), row),
            pl.BlockSpec((D_MODEL, D_MODEL), const, pipeline_mode=once),
            pl.BlockSpec((1, D_MODEL), const),
            pl.BlockSpec((D_MODEL, 2 * FFN_HIDDEN), const, pipeline_mode=once),
            pl.BlockSpec((FFN_HIDDEN, D_MODEL), const, pipeline_mode=once),
        ],
        out_specs=pl.BlockSpec((FFN_TILE, D_MODEL), row),
        out_shape=jax.ShapeDtypeStruct((t, D_MODEL), F32),
        compiler_params=pltpu.CompilerParams(
            dimension_semantics=("parallel",), vmem_limit_bytes=VMEM_LIMIT),
        name="ffn",
    )(x2, ma, mb, wo, gf, wgu, wd)


def _pair_heads(a, axis):
    shape = a.shape
    a = a.reshape(shape[:axis] + (NSA_KV_HEADS, GQA_REP, HEAD_DIM) + shape[axis + 1:])
    return jnp.swapaxes(a, axis, axis + 1).reshape(shape)


def _transposed_gain(g):
    cols = g.reshape(GQA_REP, LANES).T
    return jnp.broadcast_to(cols[:, :, None], (LANES, GQA_REP, Q_TILE)).reshape(LANES, GQA_REP * Q_TILE)


def _rope_tables(pos):
    half = HEAD_DIM // 2
    inv = ROPE_THETA ** (-np.arange(half, dtype=np.float64) / half)
    ang = np.asarray(pos, np.float64)[:, None] * inv[None, :]
    cos = np.concatenate([np.cos(ang), np.cos(ang)], axis=1)
    sin = np.concatenate([-np.sin(ang), np.sin(ang)], axis=1)
    return (np.tile(cos, (1, NSA_KV_HEADS)).astype(np.float32),
            np.tile(sin, (1, NSA_KV_HEADS)).astype(np.float32))


def _block_diag_ones(width):
    idx = np.arange(width) // HEAD_DIM
    return (idx[:, None] == idx[None, :]).astype(np.float32)


def _overlap_t(seq):
    ncmp = (seq - CMP_LEN) // CMP_STRIDE + 1
    cs = np.arange(ncmp)[:, None] * CMP_STRIDE
    ss = np.arange(seq // SLC_BLOCK)[None, :] * SLC_BLOCK
    ov = np.clip(np.minimum(cs + CMP_LEN, ss + SLC_BLOCK) - np.maximum(cs, ss), 0, None) / CMP_LEN
    out = np.zeros((LANES, LANES), np.float32)
    out[:seq // SLC_BLOCK, :ncmp] = ov.T
    return out


def _cmp_weights(pe, w1, w2):
    pe2 = jnp.concatenate([pe, pe], axis=1)
    w1r = w1.reshape(CMP_LEN, HEAD_DIM, CMP_HIDDEN)
    z1 = jnp.zeros_like(w1r)
    w1b = jnp.concatenate([jnp.concatenate([w1r, z1], axis=2),
                           jnp.concatenate([z1, w1r], axis=2)], axis=1).astype(BF16)
    z2 = jnp.zeros_like(w2)
    w2b = jnp.concatenate([jnp.concatenate([w2, z2], axis=1),
                           jnp.concatenate([z2, w2], axis=1)], axis=0).astype(BF16)
    w1b = w1b.reshape(2, (CMP_LEN // 2) * LANES, 2 * CMP_HIDDEN)
    return pe2, w1b, w2b


def kernel(x, attn_norm_g, w_in, q_norm_g, k_norm_cmp_g, k_norm_slc_g, k_norm_win_g, cmp_pe_k, cmp_w1_k, cmp_w2_k, cmp_pe_v, cmp_w1_v, cmp_w2_v, conv_dw_w, conv_dw_b, conv_ln_g, conv_ln_b, out_norm_nsa_g, out_norm_conv_g, w_out, ffn_norm_g, w_gate_up, w_down):
    batch, seq, d_model = x.shape
    assert d_model == D_MODEL and seq % PROJ_TILE == 0 and seq // SLC_BLOCK == 32
    depth = w_in.shape[0]
    cos_np, sin_np = _rope_tables(np.arange(seq))
    ccos_np, csin_np = _rope_tables(np.arange(seq // CMP_STRIDE) * CMP_STRIDE + CMP_LEN - 1)
    ccos_np, csin_np = np.tile(ccos_np, (CMP_SEQS, 1)), np.tile(csin_np, (CMP_SEQS, 1))
    cos, sin, ccos, csin = map(jnp.asarray, (cos_np, sin_np, ccos_np, csin_np))
    bd256 = jnp.asarray(_block_diag_ones(256), BF16)
    bd128 = jnp.asarray(_block_diag_ones(LANES), BF16)
    ovl = jnp.asarray(_overlap_t(seq), BF16)

    x2 = x.reshape(batch * seq, d_model)
    for l in range(depth):
        w = w_in[l]
        gq = jnp.tile(q_norm_g[l], 4)[None, :]
        gk = jnp.concatenate([jnp.tile(k_norm_slc_g[l], 2), jnp.tile(k_norm_win_g[l], 2)])[None, :]
        q, ks, kw, vs0, vs1, vw0, vw1, kc, vc, gt, hcv, wgu_b, wd_b, wo_b = _proj_call(
            x2, attn_norm_g[l][None, :], w.T, cos, sin, gq, gk, bd256,
            w_gate_up[l], w_down[l], w_out[l], seq)

        pek, w1k, w2k = _cmp_weights(cmp_pe_k[l], cmp_w1_k[l], cmp_w2_k[l])
        pev, w1v, w2v = _cmp_weights(cmp_pe_v[l], cmp_w1_v[l], cmp_w2_v[l])
        kcmp, vc0, vc1 = _cmp_call(kc, vc, pek, w1k, w2k, pev, w1v, w2v, ccos, csin,
                                jnp.tile(k_norm_cmp_g[l], 2)[None, :], bd128, batch, seq)

        mix_a = _attn_call(q, ks, kw, vs0, vs1, vw0, vw1, kcmp, vc0, vc1, gt, ovl,
                           _transposed_gain(_pair_heads(out_norm_nsa_g[l], 0)), batch, seq)

        w_dw = jnp.concatenate([conv_dw_w[l][:, 0, :],
                                jnp.zeros((CONV_HALO - CONV_WIDTH, CONV_CH), F32)], axis=0)
        w_dw = w_dw.reshape(CONV_HALO, CONV_CH // LANES, LANES).transpose(1, 0, 2)
        mix_b = _conv_call(hcv, w_dw, conv_dw_b[l].reshape(CONV_CH // LANES, 1, LANES),
                           conv_ln_g[l][None, :],
                           conv_ln_b[l][None, :], out_norm_conv_g[l][None, :], seq)

        x2 = _ffn_call(x2, mix_a, mix_b, wo_b, ffn_norm_g[l][None, :], wgu_b, wd_b)
    return x2.reshape(batch, seq, d_model)
```

```python
import functools
import types

import numpy as np
import jax
import jax.numpy as jnp
from jax import lax
from jax.experimental import pallas as pl
from jax.experimental.pallas import tpu as pltpu

F32 = jnp.float32
BF16 = jnp.bfloat16

D_MODEL = 1024
HEAD_DIM = 64
NSA_HEADS = 8
NSA_KV_HEADS = 2
GQA_REP = NSA_HEADS // NSA_KV_HEADS
NSA_WIDTH = NSA_HEADS * HEAD_DIM
CONV_CH = D_MODEL - NSA_WIDTH
CMP_LEN = 32
CMP_STRIDE = 16
CMP_HIDDEN = 4 * HEAD_DIM
SLC_BLOCK = 64
SLC_TOPN = 8
WINDOW = 512
CONV_WIDTH = 31
FFN_HIDDEN = 2816
ROPE_THETA = 10000.0
EPS = 1e-6
NEG = -1e30
LOG2E = 1.4426950408889634
FORCE = 1e6

LANES = 128
Q_TILE = 128
TILES_PER_STEP = 4
SLC_CHUNK = 256
WIN_KEYS = WINDOW + Q_TILE
PROJ_TILE = 512
CMP_SEQS = 4
CONV_TILE = 512
CONV_HALO = 32
CONV_ROWS = 256
FFN_TILE = 512
FFN_CHUNK = 256
VMEM_LIMIT = 56 * 1024 * 1024

C_Q = 0
C_CMP = 512
C_SLC = 768
C_WIN = 1024
MAIN_COLS = 1280
GATE_COLS = 3 * NSA_HEADS

_TRANS_B = (((1,), (1,)), ((), ()))


def _dot(a, b):
    return jnp.dot(a, b, preferred_element_type=F32)


def _dot_tb(a, b):
    return lax.dot_general(a, b, _TRANS_B, preferred_element_type=F32)


def _split_bf16(x):
    hi = x.astype(BF16)
    lo = (x - hi.astype(F32)).astype(BF16)
    return hi, lo


def _sigmoid(x):
    return 1.0 / (1.0 + jnp.exp(-x))


def _head_norm_rope(y, gain, bd, cos, sin, first_half, scale):
    ss = _dot((y * y).astype(BF16), bd)
    yn = y * lax.rsqrt(ss * (1.0 / HEAD_DIM) + EPS) * gain
    outs = []
    for j in range(y.shape[1] // LANES):
        blk = yn[:, j * LANES:(j + 1) * LANES]
        rot = jnp.where(first_half, pltpu.roll(blk, LANES - HEAD_DIM // 2, 1),
                        pltpu.roll(blk, HEAD_DIM // 2, 1))
        outs.append((blk * cos + rot * sin) * scale)
    return outs


def _value_t_with_ones(v):
    vt = v.T
    row = lax.broadcasted_iota(jnp.int32, vt.shape, 0)
    return (jnp.where(row < HEAD_DIM, vt, 1.0).astype(BF16),
            jnp.where(row < HEAD_DIM, 1.0, vt).astype(BF16))


def _proj_body(x_ref, gin_ref, w_ref, cos_ref, sin_ref, gq_ref, gk_ref, bd_ref,
               wgu_in_ref, wd_in_ref, wo_in_ref,
               q_ref, ks_ref, kw_ref, vs0_ref, vs1_ref, vw0_ref, vw1_ref, kc_ref, vc_ref, gt_ref,
               hcv_ref, wgu_out_ref, wd_out_ref, wo_out_ref, wm_ref, wc_ref, wg_ref):
    @pl.when(pl.program_id(0) == 0)
    def _():
        for j in range(MAIN_COLS // LANES):
            wm_ref[:, j * LANES:(j + 1) * LANES] = w_ref[j * LANES:(j + 1) * LANES, :].T.astype(BF16)
        gl = jnp.concatenate([w_ref[MAIN_COLS:MAIN_COLS + GATE_COLS, :],
                              jnp.zeros((LANES - GATE_COLS, D_MODEL), F32)], axis=0)
        wg_ref[...] = gl.T.astype(BF16)
        for j in range(2 * CONV_CH // LANES):
            r0 = MAIN_COLS + GATE_COLS + j * LANES
            wc_ref[:, j * LANES:(j + 1) * LANES] = w_ref[r0:r0 + LANES, :].T.astype(BF16)

    wgu_out_ref[...] = wgu_in_ref[...].astype(BF16)
    wd_out_ref[...] = wd_in_ref[...].astype(BF16)
    wo_out_ref[...] = wo_in_ref[...].astype(BF16)

    x = x_ref[...]
    xg = (x * gin_ref[...]).astype(BF16)
    rs = lax.rsqrt(jnp.mean(x * x, axis=-1, keepdims=True) + EPS)
    cos = cos_ref[...]
    sin = sin_ref[...]
    bd = bd_ref[...]
    lane = lax.broadcasted_iota(jnp.int32, (PROJ_TILE, LANES), 1)
    first_half = (lane & (HEAD_DIM // 2)) == 0
    held = {}

    def queries(group, y):
        held["q", group] = _head_norm_rope(y, gq_ref[...], bd, cos, sin, first_half,
                                           HEAD_DIM ** -0.5 * LOG2E)
        if group == 0:
            return
        low = lane < HEAD_DIM
        for j in range(2):
            a, b = held["q", 0][j], held["q", 1][j]
            q_ref[:, (2 * j) * LANES:(2 * j + 1) * LANES] = jnp.where(
                low, a, pltpu.roll(b, HEAD_DIM, 1)).astype(BF16)
            q_ref[:, (2 * j + 1) * LANES:(2 * j + 2) * LANES] = jnp.where(
                low, pltpu.roll(a, HEAD_DIM, 1), b).astype(BF16)

    def values(y, out0_ref, out1_ref):
        for i in range(PROJ_TILE // LANES):
            out0_ref[i], out1_ref[i] = _value_t_with_ones(y[i * LANES:(i + 1) * LANES, :])

    def slc_kv(y):
        held["ks"] = y[:, :LANES]
        values(y[:, LANES:], vs0_ref, vs1_ref)

    def win_kv(y):
        kblks = _head_norm_rope(_lane_cat([held["ks"], y[:, :LANES]]), gk_ref[...], bd, cos, sin,
                                first_half, 1.0)
        ks_ref[...] = kblks[0].astype(BF16)
        kw_ref[...] = kblks[1].astype(BF16)
        values(y[:, LANES:], vw0_ref, vw1_ref)

    def cmp_inputs(y):
        kc_ref[...] = y[:, :LANES]
        vc_ref[...] = y[:, LANES:]

    def conv_lin(y):
        held["ca"] = y

    def conv_glu(y):
        hcv_ref[...] = held["ca"] * _sigmoid(y)

    def gates(y):
        sg = _sigmoid(y)
        for i in range(PROJ_TILE // LANES):
            gt_ref[i] = sg[i * LANES:(i + 1) * LANES, :].T[0:32, :]

    sections = [
        (wm_ref, C_Q, 256, functools.partial(queries, 0)),
        (wm_ref, C_Q + 256, 256, functools.partial(queries, 1)),
        (wc_ref, 0, CONV_CH, conv_lin),
        (wm_ref, C_SLC, 256, slc_kv),
        (wc_ref, CONV_CH, CONV_CH, conv_glu),
        (wm_ref, C_WIN, 256, win_kv),
        (wm_ref, C_CMP, 256, cmp_inputs),
        (wg_ref, 0, LANES, gates),
    ]

    def project(sec):
        w_ref, c0, width, _ = sec
        return _dot(xg, w_ref[:, c0:c0 + width])

    y_next = project(sections[0])
    for i, sec in enumerate(sections):
        y_cur = y_next
        if i + 1 < len(sections):
            y_next = project(sections[i + 1])
        sec[3](y_cur * rs)


def _paired_row_block(i):
    head = i // 2
    src = ((head % NSA_KV_HEADS) * GQA_REP + head // NSA_KV_HEADS) * 2 + i % 2
    return jnp.where(i < NSA_WIDTH // 32, src, i)


def _proj_call(x2, gin, w_in, cos, sin, gq, gk, bd, w_gate_up, w_down, w_out, seq):
    t = x2.shape[0]
    nt = t // PROJ_TILE
    assert D_MODEL // nt == 32 and nt % 16 == 0 and (FFN_HIDDEN // 16) % 16 == 0
    wd_rows = FFN_HIDDEN // 16
    tiles_per_seq = seq // PROJ_TILE
    row = lambda i: (i, 0)
    const = lambda i: (0, 0)
    tab = lambda i: (i % tiles_per_seq, 0)
    blk3 = lambda i: (i, 0, 0)
    wd_blk = lambda i: (i * 16 // nt, 0)
    nb = PROJ_TILE // LANES
    return pl.pallas_call(
        _proj_body,
        grid=(nt,),
        in_specs=[
            pl.BlockSpec((PROJ_TILE, D_MODEL), row),
            pl.BlockSpec((1, D_MODEL), const),
            pl.BlockSpec(w_in.shape, const, pipeline_mode=pl.Buffered(1)),
            pl.BlockSpec((PROJ_TILE, LANES), tab),
            pl.BlockSpec((PROJ_TILE, LANES), tab),
            pl.BlockSpec((1, 256), const),
            pl.BlockSpec((1, 256), const),
            pl.BlockSpec((256, 256), const),
            pl.BlockSpec((32, 2 * FFN_HIDDEN), row),
            pl.BlockSpec((wd_rows, D_MODEL), wd_blk),
            pl.BlockSpec((32, D_MODEL), lambda i: (_paired_row_block(i), 0)),
        ],
        out_specs=[
            pl.BlockSpec((PROJ_TILE, NSA_WIDTH), row),
            pl.BlockSpec((PROJ_TILE, LANES), row),
            pl.BlockSpec((PROJ_TILE, LANES), row),
            pl.BlockSpec((nb, LANES, LANES), blk3),
            pl.BlockSpec((nb, LANES, LANES), blk3),
            pl.BlockSpec((nb, LANES, LANES), blk3),
            pl.BlockSpec((nb, LANES, LANES), blk3),
            pl.BlockSpec((PROJ_TILE, LANES), row),
            pl.BlockSpec((PROJ_TILE, LANES), row),
            pl.BlockSpec((nb, 32, LANES), blk3),
            pl.BlockSpec((PROJ_TILE, CONV_CH), row),
            pl.BlockSpec((32, 2 * FFN_HIDDEN), row),
            pl.BlockSpec((wd_rows, D_MODEL), wd_blk),
            pl.BlockSpec((32, D_MODEL), row),
        ],
        out_shape=[
            jax.ShapeDtypeStruct((t, NSA_WIDTH), BF16),
            jax.ShapeDtypeStruct((t, LANES), BF16),
            jax.ShapeDtypeStruct((t, LANES), BF16),
            jax.ShapeDtypeStruct((t // LANES, LANES, LANES), BF16),
            jax.ShapeDtypeStruct((t // LANES, LANES, LANES), BF16),
            jax.ShapeDtypeStruct((t // LANES, LANES, LANES), BF16),
            jax.ShapeDtypeStruct((t // LANES, LANES, LANES), BF16),
            jax.ShapeDtypeStruct((t, LANES), F32),
            jax.ShapeDtypeStruct((t, LANES), F32),
            jax.ShapeDtypeStruct((t // LANES, 32, LANES), F32),
            jax.ShapeDtypeStruct((t, CONV_CH), F32),
            jax.ShapeDtypeStruct((D_MODEL, 2 * FFN_HIDDEN), BF16),
            jax.ShapeDtypeStruct((FFN_HIDDEN, D_MODEL), BF16),
            jax.ShapeDtypeStruct((D_MODEL, D_MODEL), BF16),
        ],
        scratch_shapes=[pltpu.VMEM((D_MODEL, MAIN_COLS), BF16),
                        pltpu.VMEM((D_MODEL, 2 * CONV_CH), BF16),
                        pltpu.VMEM((D_MODEL, LANES), BF16)],
        compiler_params=pltpu.CompilerParams(
            dimension_semantics=("arbitrary",), vmem_limit_bytes=VMEM_LIMIT),
        name="proj",
    )(x2, gin, w_in, cos, sin, gq, gk, bd, w_gate_up, w_down, w_out)


def _cmp_mlp(t_ref, pe_ref, w1_ref, w2_ref):
    half = CMP_LEN // 2
    nchunk = t_ref.shape[0] // CMP_STRIDE
    toks = [t_ref[pl.ds(l, nchunk, stride=CMP_STRIDE), :] for l in range(half)]
    xp = _lane_cat([(toks[l] + pe_ref[l:l + 1, :]).astype(BF16) for l in range(half)])
    xq = _lane_cat([(toks[l] + pe_ref[half + l:half + l + 1, :]).astype(BF16) for l in range(half)])
    h1 = _dot(xp, w1_ref[0]) + pltpu.roll(_dot(xq, w1_ref[1]), nchunk - 1, 0)
    act = (h1 * _sigmoid(h1)).astype(BF16)
    return _dot(act, w2_ref[...])


def _cmp_body(kc_ref, vc_ref, pek_ref, w1k_ref, w2k_ref, pev_ref, w1v_ref, w2v_ref,
              cos_ref, sin_ref, gk_ref, bd_ref, ko_ref, vt0_ref, vt1_ref):
    kraw = _cmp_mlp(kc_ref, pek_ref, w1k_ref, w2k_ref)
    lane = lax.broadcasted_iota(jnp.int32, kraw.shape, 1)
    first_half = (lane & (HEAD_DIM // 2)) == 0
    (kr,) = _head_norm_rope(kraw, gk_ref[...], bd_ref[...], cos_ref[...], sin_ref[...],
                            first_half, 1.0)
    ko_ref[...] = kr.astype(BF16)
    vraw = _cmp_mlp(vc_ref, pev_ref, w1v_ref, w2v_ref)
    for i in range(CMP_SEQS):
        vt0_ref[i], vt1_ref[i] = _value_t_with_ones(vraw[i * LANES:(i + 1) * LANES, :])


def _cmp_call(kc, vc, pek, w1k, w2k, pev, w1v, w2v, cos, sin, gk, bd, batch, seq):
    seqblk = lambda b: (b, 0)
    c2 = lambda b: (0, 0)
    c3 = lambda b: (0, 0, 0)
    ncmp_pad = seq // CMP_STRIDE
    assert ncmp_pad == LANES and batch % CMP_SEQS == 0
    rows = CMP_SEQS * ncmp_pad
    half_k = (CMP_LEN // 2) * LANES
    return pl.pallas_call(
        _cmp_body,
        grid=(batch // CMP_SEQS,),
        in_specs=[
            pl.BlockSpec((CMP_SEQS * seq, LANES), seqblk),
            pl.BlockSpec((CMP_SEQS * seq, LANES), seqblk),
            pl.BlockSpec((CMP_LEN, LANES), c2),
            pl.BlockSpec((2, half_k, 2 * CMP_HIDDEN), c3),
            pl.BlockSpec((2 * CMP_HIDDEN, LANES), c2),
            pl.BlockSpec((CMP_LEN, LANES), c2),
            pl.BlockSpec((2, half_k, 2 * CMP_HIDDEN), c3),
            pl.BlockSpec((2 * CMP_HIDDEN, LANES), c2),
            pl.BlockSpec((rows, LANES), c2),
            pl.BlockSpec((rows, LANES), c2),
            pl.BlockSpec((1, LANES), c2),
            pl.BlockSpec((LANES, LANES), c2),
        ],
        out_specs=[
            pl.BlockSpec((rows, LANES), seqblk),
            pl.BlockSpec((CMP_SEQS, LANES, ncmp_pad), lambda b: (b, 0, 0)),
            pl.BlockSpec((CMP_SEQS, LANES, ncmp_pad), lambda b: (b, 0, 0)),
        ],
        out_shape=[
            jax.ShapeDtypeStruct((batch * ncmp_pad, LANES), BF16),
            jax.ShapeDtypeStruct((batch, LANES, ncmp_pad), BF16),
            jax.ShapeDtypeStruct((batch, LANES, ncmp_pad), BF16),
        ],
        compiler_params=pltpu.CompilerParams(
            dimension_semantics=("parallel",), vmem_limit_bytes=VMEM_LIMIT),
        name="compress",
    )(kc, vc, pek, w1k, w2k, pev, w1v, w2v, cos, sin, gk, bd)


def _lane_cat(blocks):
    return jnp.concatenate(blocks, axis=1)


def _softmax_numer(s, bias, m_run=None):
    es, ms = [], []
    for r in range(GQA_REP):
        sm = s[:, r * LANES:(r + 1) * LANES] + bias
        m = jnp.max(sm, axis=0, keepdims=True)
        if m_run is not None:
            m = jnp.maximum(m, m_run[:, r * LANES:(r + 1) * LANES])
        es.append(jnp.exp2(sm - m))
        ms.append(m)
    return _lane_cat(es), _lane_cat(ms)


def _safe_inv(l):
    return 1.0 / jnp.where(l > 0.0, l, 1.0)


def _attn_body(*refs):
    same = SLC_CHUNK // Q_TILE
    runs = TILES_PER_STEP // same
    step = pl.program_id(0)
    for k in range(refs[1].shape[0] // (TILES_PER_STEP * Q_TILE)):
        @pl.when(step == k)
        def _(k=k):
            for r in range(runs):
                _attn_run(k * runs + r + 1, k * TILES_PER_STEP + r * same, r * same, same, *refs)


def _attn_run(n_chunks, first, slot0, count, *refs):
    def one_tile(t, carry):
        tile = _tile_program(n_chunks, first + t, slot0 + t, *refs)
        pending = None
        s_next = tile.scores(tile.units[0])
        for i, unit in enumerate(tile.units):
            s_cur = s_next
            if i + 1 < len(tile.units):
                s_next = tile.scores(tile.units[i + 1])
            e, alpha = tile.softmax(unit, s_cur)
            if pending is not None:
                tile.weighted_values(*pending)
            pending = (unit, e, alpha)
        tile.weighted_values(*pending)
        tile.finish()
        return carry

    lax.fori_loop(0, count, one_tile, 0)


def _tile_program(n_chunks, c, slot, q_ref, ks_ref, kw_ref, vs0_ref, vs1_ref, vw0_ref, vw1_ref, kc_ref,
                  vc0_ref, vc1_ref, gt_ref, ovl_ref, gn_ref, out_ref):
    groups = range(NSA_KV_HEADS)
    q_rows = pl.ds(pl.multiple_of(slot * Q_TILE, Q_TILE), Q_TILE)
    vs_refs, vw_refs, vc_refs = (vs0_ref, vs1_ref), (vw0_ref, vw1_ref), (vc0_ref, vc1_ref)
    sum_row = (HEAD_DIM, 0)
    width = GQA_REP * LANES
    t_lane = c * Q_TILE + lax.broadcasted_iota(jnp.int32, (1, LANES), 1)
    lane_sq = lax.broadcasted_iota(jnp.int32, (Q_TILE, LANES), 1)
    n_live = n_chunks * (SLC_CHUNK // SLC_BLOCK)
    nsel = 32
    win_keys = min(WIN_KEYS, n_chunks * SLC_CHUNK)
    j_idx = lax.broadcasted_iota(jnp.int32, (nsel, LANES), 0)
    cur = lax.shift_right_logical(t_lane, 6)
    n_idx = lax.broadcasted_iota(jnp.int32, (LANES, LANES), 0)
    cbias = jnp.where((n_idx * CMP_STRIDE + (CMP_LEN - 1) <= t_lane) & (n_idx < LANES - 1), 0.0, NEG)
    has_cmp = jnp.where(t_lane >= CMP_LEN - 1, 1.0, 0.0)
    has_cmp = _lane_cat([has_cmp] * GQA_REP)
    win_blk = jnp.maximum(c - WINDOW // Q_TILE, 0)
    win_start = pl.multiple_of(win_blk * Q_TILE, Q_TILE)
    kpos_w = win_start + lax.broadcasted_iota(jnp.int32, (win_keys, LANES), 0)
    wbias = jnp.where((kpos_w <= t_lane) & (kpos_w > t_lane - WINDOW), 0.0, NEG)
    row_sl = lax.broadcasted_iota(jnp.int32, (SLC_CHUNK, LANES), 0)
    zero_q = jnp.zeros((Q_TILE, LANES), BF16)

    def gate_row(g, branch):
        rows = [(g * GQA_REP + r) * 3 + branch for r in range(GQA_REP)]
        return _lane_cat([gt_ref[slot, i:i + 1, :] for i in rows])

    qg = []
    for g in groups:
        in_g = (lane_sq >= HEAD_DIM) if g else (lane_sq < HEAD_DIM)
        qg.append(jnp.concatenate(
            [jnp.where(in_g, q_ref[q_rows, r * LANES:(r + 1) * LANES], zero_q) for r in range(GQA_REP)],
            axis=0))

    o_cmp = [None] * NSA_KV_HEADS
    sel_bias = [None] * NSA_KV_HEADS

    def compressed_softmax(g, sc):
        ec, _ = _softmax_numer(sc, cbias)
        coef = has_cmp * _safe_inv(jnp.sum(ec, axis=0, keepdims=True))
        pc = ec * coef
        psum = (pc[:, 0:LANES] + pc[:, LANES:2 * LANES]
                + pc[:, 2 * LANES:3 * LANES] + pc[:, 3 * LANES:4 * LANES])
        p_hi, p_lo = _split_bf16(psum)
        imp = (_dot(ovl_ref[...], p_hi) + _dot(ovl_ref[...], p_lo))[0:nsel, :]
        valid = j_idx <= cur
        forced = (j_idx == 0) | (j_idx == cur) | (j_idx == cur - 1)
        score = jnp.where(valid, imp + jnp.where(forced, FORCE, 0.0), -FORCE)
        rank = jnp.zeros((nsel, LANES), F32)
        for jp in range(nsel):
            row = score[jp:jp + 1, :]
            beats = (row > score) | ((row == score) & (j_idx > jp))
            rank = rank + jnp.where(beats, 1.0, 0.0)
        sel_bias[g] = jnp.where((rank < float(SLC_TOPN)) & (score > -1.0), 0.0, NEG)
        return ec.astype(BF16), coef

    nsub = SLC_CHUNK // LANES
    blocks_per_chunk = SLC_CHUNK // SLC_BLOCK
    kwin = kw_ref[pl.ds(win_start, win_keys), :]
    units = [("cmp", g, 0) for g in groups] + [("win", g, 0) for g in groups]
    units += [("slc", g, kc) for kc in range(n_chunks) for g in groups]
    m_run = [None] * NSA_KV_HEADS
    acc_s = [None] * NSA_KV_HEADS
    acc_w = [None] * NSA_KV_HEADS

    def scores(unit):
        kind, g, kc = unit
        if kind == "cmp":
            keys = kc_ref[...]
        elif kind == "win":
            keys = kwin
        else:
            keys = ks_ref[kc * SLC_CHUNK:(kc + 1) * SLC_CHUNK, :]
        return _dot_tb(keys, qg[g])

    def softmax(unit, s):
        kind, g, kc = unit
        if kind == "cmp":
            return compressed_softmax(g, s)
        if kind == "win":
            e, _ = _softmax_numer(s, wbias)
            return e.astype(BF16), None
        j0 = kc * blocks_per_chunk
        bias = jnp.concatenate(
            [jnp.broadcast_to(sel_bias[g][j:j + 1, :], (SLC_BLOCK, LANES))
             for j in range(j0, j0 + blocks_per_chunk)], axis=0)
        if kc == n_chunks - 1:
            bias = jnp.where(kc * SLC_CHUNK + row_sl <= t_lane, bias, NEG)
        e, m_new = _softmax_numer(s, bias, m_run[g])
        alpha = None if kc == 0 else jnp.exp2(m_run[g] - m_new)
        m_run[g] = m_new
        return e.astype(BF16), alpha

    def weighted_values(unit, e, alpha):
        kind, g, kc = unit
        if kind == "cmp":
            o_cmp[g] = _dot(vc_refs[g][0], e) * (gate_row(g, 0) * alpha)
            return
        if kind == "win":
            vt = _lane_cat([vw_refs[g][win_blk + i] for i in range(win_keys // LANES)])
            acc_w[g] = _dot(vt, e)
            return
        vt = _lane_cat([vs_refs[g][nsub * kc + i] for i in range(nsub)])
        pv = _dot(vt, e)
        acc_s[g] = pv if alpha is None else acc_s[g] * alpha + pv

    def finish():
        o_groups = []
        for g in groups:
            r0 = sum_row[g]
            o_groups.append(o_cmp[g]
                            + acc_s[g] * (gate_row(g, 1) * _safe_inv(acc_s[g][r0:r0 + 1, :]))
                            + acc_w[g] * (gate_row(g, 2) * _safe_inv(acc_w[g][r0:r0 + 1, :])))
        row_d = lax.broadcasted_iota(jnp.int32, (LANES, width), 0)
        o_t = jnp.where(row_d < HEAD_DIM, o_groups[0], o_groups[1])
        sq = jnp.sum(o_t * o_t, axis=0, keepdims=True)
        ss = (sq[:, 0:LANES] + sq[:, LANES:2 * LANES]
              + sq[:, 2 * LANES:3 * LANES] + sq[:, 3 * LANES:4 * LANES])
        rs = lax.rsqrt(ss * (1.0 / NSA_WIDTH) + EPS)
        for r in range(GQA_REP):
            cols = slice(r * LANES, (r + 1) * LANES)
            out_ref[q_rows, cols] = (o_t[:, cols] * rs * gn_ref[:, cols]).T.astype(BF16)

    return types.SimpleNamespace(units=units, scores=scores, softmax=softmax,
                                 weighted_values=weighted_values, finish=finish)


def _attn_call(q, ks, kw, vs0, vs1, vw0, vw1, kcmp, vc0, vc1, gt, ovl, gn, batch, seq):
    assert SLC_CHUNK % Q_TILE == 0 and TILES_PER_STEP % (SLC_CHUNK // Q_TILE) == 0
    rows = TILES_PER_STEP * Q_TILE
    nq = seq // rows
    nkb = seq // LANES
    qrow = lambda c, b: (b * nq + c, 0)
    per_b = lambda c, b: (b, 0)
    per_b3 = lambda c, b: (b, 0, 0)
    vspec = pl.BlockSpec((nkb, LANES, LANES), per_b3)
    cspec = pl.BlockSpec((1, LANES, LANES), per_b3)
    return pl.pallas_call(
        _attn_body,
        grid=(nq, batch),
        in_specs=[
            pl.BlockSpec((rows, NSA_WIDTH), qrow),
            pl.BlockSpec((seq, LANES), per_b),
            pl.BlockSpec((seq, LANES), per_b),
            vspec, vspec, vspec, vspec,
            pl.BlockSpec((LANES, LANES), per_b),
            cspec, cspec,
            pl.BlockSpec((TILES_PER_STEP, 32, LANES), lambda c, b: (b * nq + c, 0, 0)),
            pl.BlockSpec((LANES, LANES), lambda c, b: (0, 0)),
            pl.BlockSpec((LANES, NSA_WIDTH), lambda c, b: (0, 0)),
        ],
        out_specs=pl.BlockSpec((rows, NSA_WIDTH), qrow),
        out_shape=jax.ShapeDtypeStruct((batch * seq, NSA_WIDTH), BF16),
        compiler_params=pltpu.CompilerParams(
            dimension_semantics=("parallel", "arbitrary"), vmem_limit_bytes=VMEM_LIMIT),
        name="attn",
    )(q, ks, kw, vs0, vs1, vw0, vw1, kcmp, vc0, vc1, gt, ovl, gn)


def _conv_body(tiles_per_seq, cur_ref, halo_ref, w_ref, b_ref, lng_ref, lnb_ref, gn_ref, out_ref,
               buf_ref, sh_ref, cv_ref):
    first = (pl.program_id(0) % tiles_per_seq) == 0
    buf_ref[0:CONV_HALO, :] = jnp.where(first, 0.0, halo_ref[...])
    buf_ref[CONV_HALO:, :] = cur_ref[...]
    lead = CONV_HALO - (CONV_WIDTH - 1)
    ncb = CONV_CH // LANES
    nrb = CONV_TILE // CONV_ROWS
    for s in range(8):
        span = CONV_TILE + 8 * ((CONV_WIDTH - 1 - s) // 8)
        for cb in range(ncb):
            sh_ref[s, cb, 0:span, :] = buf_ref[lead + s:lead + s + span, cb * LANES:(cb + 1) * LANES]

    def conv_block(i, carry):
        cb = i % ncb
        base = pl.multiple_of((i // ncb) * CONV_ROWS, CONV_ROWS)
        acc = jnp.zeros((CONV_ROWS, LANES), F32) + b_ref[cb]
        for k in range(CONV_WIDTH):
            acc = acc + w_ref[cb, k:k + 1, :] * sh_ref[k % 8, cb, pl.ds(base + 8 * (k // 8), CONV_ROWS), :]
        cv_ref[cb, pl.ds(base, CONV_ROWS), :] = acc
        return carry

    lax.fori_loop(0, nrb * ncb, conv_block, 0)
    h = _lane_cat([cv_ref[cb] for cb in range(ncb)])
    mu = jnp.mean(h, axis=-1, keepdims=True)
    d = h - mu
    var = jnp.mean(d * d, axis=-1, keepdims=True)
    hn = d * lax.rsqrt(var + EPS) * lng_ref[...] + lnb_ref[...]
    o = hn * _sigmoid(hn)
    ms = jnp.mean(o * o, axis=-1, keepdims=True)
    out_ref[...] = (o * lax.rsqrt(ms + EPS) * gn_ref[...]).astype(BF16)


def _conv_call(hcv, w, b, lng, lnb, gn, seq):
    t = hcv.shape[0]
    nt = t // CONV_TILE
    halo_per_tile = CONV_TILE // CONV_HALO
    ncb = CONV_CH // LANES
    row = lambda i: (i, 0)
    const = lambda i: (0, 0)
    return pl.pallas_call(
        functools.partial(_conv_body, seq // CONV_TILE),
        grid=(nt,),
        in_specs=[
            pl.BlockSpec((CONV_TILE, CONV_CH), row),
            pl.BlockSpec((CONV_HALO, CONV_CH), lambda i: (jnp.maximum(i * halo_per_tile - 1, 0), 0)),
            pl.BlockSpec((ncb, CONV_HALO, LANES), lambda i: (0, 0, 0)),
            pl.BlockSpec((ncb, 1, LANES), lambda i: (0, 0, 0)),
            pl.BlockSpec((1, CONV_CH), const),
            pl.BlockSpec((1, CONV_CH), const),
            pl.BlockSpec((1, CONV_CH), const),
        ],
        out_specs=pl.BlockSpec((CONV_TILE, CONV_CH), row),
        out_shape=jax.ShapeDtypeStruct((t, CONV_CH), BF16),
        scratch_shapes=[pltpu.VMEM((CONV_HALO + CONV_TILE, CONV_CH), F32),
                        pltpu.VMEM((8, ncb, CONV_TILE + 8 * ((CONV_WIDTH - 1) // 8), LANES), F32),
                        pltpu.VMEM((ncb, CONV_TILE, LANES), F32)],
        compiler_params=pltpu.CompilerParams(
            dimension_semantics=("parallel",), vmem_limit_bytes=VMEM_LIMIT),
        name="conv",
    )(hcv, hcv, w, b, lng, lnb, gn)


def _ffn_body(x_ref, ma_ref, mb_ref, wo_ref, gf_ref, wgu_ref, wd_ref, out_ref):
    h = (x_ref[...] + _dot(ma_ref[...], wo_ref[0:NSA_WIDTH, :])
         + _dot(mb_ref[...], wo_ref[NSA_WIDTH:D_MODEL, :]))
    ms = jnp.mean(h * h, axis=-1, keepdims=True)
    hn = (h * lax.rsqrt(ms + EPS) * gf_ref[...]).astype(BF16)
    acc = jnp.zeros_like(h)
    for j in range(FFN_HIDDEN // FFN_CHUNK):
        c0 = j * FFN_CHUNK
        gte = _dot(hn, wgu_ref[:, c0:c0 + FFN_CHUNK])
        up = _dot(hn, wgu_ref[:, FFN_HIDDEN + c0:FFN_HIDDEN + c0 + FFN_CHUNK])
        act = (gte * _sigmoid(gte) * up).astype(BF16)
        acc = acc + _dot(act, wd_ref[c0:c0 + FFN_CHUNK, :])
    out_ref[...] = h + acc


def _ffn_call(x2, ma, mb, wo, gf, wgu, wd):
    t = x2.shape[0]
    nt = t // FFN_TILE
    row = lambda i: (i, 0)
    const = lambda i: (0, 0)
    once = pl.Buffered(1)
    return pl.pallas_call(
        _ffn_body,
        grid=(nt,),
        in_specs=[
            pl.BlockSpec((FFN_TILE, D_MODEL), row),
            pl.BlockSpec((FFN_TILE, NSA_WIDTH), row),
            pl.BlockSpec((FFN_TILE, CONV_CH), row),
            pl.BlockSpec((D_MODEL, D_MODEL), const, pipeline_mode=once),
            pl.BlockSpec((1, D_MODEL), const),
            pl.BlockSpec((D_MODEL, 2 * FFN_HIDDEN), const, pipeline_mode=once),
            pl.BlockSpec((FFN_HIDDEN, D_MODEL), const, pipeline_mode=once),
        ],
        out_specs=pl.BlockSpec((FFN_TILE, D_MODEL), row),
        out_shape=jax.ShapeDtypeStruct((t, D_MODEL), F32),
        compiler_params=pltpu.CompilerParams(
            dimension_semantics=("parallel",), vmem_limit_bytes=VMEM_LIMIT),
        name="ffn",
    )(x2, ma, mb, wo, gf, wgu, wd)


def _pair_heads(a, axis):
    shape = a.shape
    a = a.reshape(shape[:axis] + (NSA_KV_HEADS, GQA_REP, HEAD_DIM) + shape[axis + 1:])
    return jnp.swapaxes(a, axis, axis + 1).reshape(shape)


def _transposed_gain(g):
    cols = g.reshape(GQA_REP, LANES).T
    return jnp.broadcast_to(cols[:, :, None], (LANES, GQA_REP, Q_TILE)).reshape(LANES, GQA_REP * Q_TILE)


def _rope_tables(pos):
    half = HEAD_DIM // 2
    inv = ROPE_THETA ** (-np.arange(half, dtype=np.float64) / half)
    ang = np.asarray(pos, np.float64)[:, None] * inv[None, :]
    cos = np.concatenate([np.cos(ang), np.cos(ang)], axis=1)
    sin = np.concatenate([-np.sin(ang), np.sin(ang)], axis=1)
    return (np.tile(cos, (1, NSA_KV_HEADS)).astype(np.float32),
            np.tile(sin, (1, NSA_KV_HEADS)).astype(np.float32))


def _block_diag_ones(width):
    idx = np.arange(width) // HEAD_DIM
    return (idx[:, None] == idx[None, :]).astype(np.float32)


def _overlap_t(seq):
    ncmp = (seq - CMP_LEN) // CMP_STRIDE + 1
    cs = np.arange(ncmp)[:, None] * CMP_STRIDE
    ss = np.arange(seq // SLC_BLOCK)[None, :] * SLC_BLOCK
    ov = np.clip(np.minimum(cs + CMP_LEN, ss + SLC_BLOCK) - np.maximum(cs, ss), 0, None) / CMP_LEN
    out = np.zeros((LANES, LANES), np.float32)
    out[:seq // SLC_BLOCK, :ncmp] = ov.T
    return out


def _cmp_weights(pe, w1, w2):
    pe2 = jnp.concatenate([pe, pe], axis=1)
    w1r = w1.reshape(CMP_LEN, HEAD_DIM, CMP_HIDDEN)
    z1 = jnp.zeros_like(w1r)
    w1b = jnp.concatenate([jnp.concatenate([w1r, z1], axis=2),
                           jnp.concatenate([z1, w1r], axis=2)], axis=1).astype(BF16)
    z2 = jnp.zeros_like(w2)
    w2b = jnp.concatenate([jnp.concatenate([w2, z2], axis=1),
                           jnp.concatenate([z2, w2], axis=1)], axis=0).astype(BF16)
    w1b = w1b.reshape(2, (CMP_LEN // 2) * LANES, 2 * CMP_HIDDEN)
    return pe2, w1b, w2b


def kernel(x, attn_norm_g, w_in, q_norm_g, k_norm_cmp_g, k_norm_slc_g, k_norm_win_g, cmp_pe_k, cmp_w1_k, cmp_w2_k, cmp_pe_v, cmp_w1_v, cmp_w2_v, conv_dw_w, conv_dw_b, conv_ln_g, conv_ln_b, out_norm_nsa_g, out_norm_conv_g, w_out, ffn_norm_g, w_gate_up, w_down):
    batch, seq, d_model = x.shape
    assert d_model == D_MODEL and seq % PROJ_TILE == 0 and seq // SLC_BLOCK == 32
    depth = w_in.shape[0]
    cos_np, sin_np = _rope_tables(np.arange(seq))
    ccos_np, csin_np = _rope_tables(np.arange(seq // CMP_STRIDE) * CMP_STRIDE + CMP_LEN - 1)
    ccos_np, csin_np = np.tile(ccos_np, (CMP_SEQS, 1)), np.tile(csin_np, (CMP_SEQS, 1))
    cos, sin, ccos, csin = map(jnp.asarray, (cos_np, sin_np, ccos_np, csin_np))
    bd256 = jnp.asarray(_block_diag_ones(256), BF16)
    bd128 = jnp.asarray(_block_diag_ones(LANES), BF16)
    ovl = jnp.asarray(_overlap_t(seq), BF16)

    x2 = x.reshape(batch * seq, d_model)
    for l in range(depth):
        w = w_in[l]
        gq = jnp.tile(q_norm_g[l], 4)[None, :]
        gk = jnp.concatenate([jnp.tile(k_norm_slc_g[l], 2), jnp.tile(k_norm_win_g[l], 2)])[None, :]
        q, ks, kw, vs0, vs1, vw0, vw1, kc, vc, gt, hcv, wgu_b, wd_b, wo_b = _proj_call(
            x2, attn_norm_g[l][None, :], w.T, cos, sin, gq, gk, bd256,
            w_gate_up[l], w_down[l], w_out[l], seq)

        pek, w1k, w2k = _cmp_weights(cmp_pe_k[l], cmp_w1_k[l], cmp_w2_k[l])
        pev, w1v, w2v = _cmp_weights(cmp_pe_v[l], cmp_w1_v[l], cmp_w2_v[l])
        kcmp, vc0, vc1 = _cmp_call(kc, vc, pek, w1k, w2k, pev, w1v, w2v, ccos, csin,
                                jnp.tile(k_norm_cmp_g[l], 2)[None, :], bd128, batch, seq)

        mix_a = _attn_call(q, ks, kw, vs0, vs1, vw0, vw1, kcmp, vc0, vc1, gt, ovl,
                           _transposed_gain(_pair_heads(out_norm_nsa_g[l], 0)), batch, seq)

        w_dw = jnp.concatenate([conv_dw_w[l][:, 0, :],
                                jnp.zeros((CONV_HALO - CONV_WIDTH, CONV_CH), F32)], axis=0)
        w_dw = w_dw.reshape(CONV_HALO, CONV_CH // LANES, LANES).transpose(1, 0, 2)
        mix_b = _conv_call(hcv, w_dw, conv_dw_b[l].reshape(CONV_CH // LANES, 1, LANES),
                           conv_ln_g[l][None, :],
                           conv_ln_b[l][None, :], out_norm_conv_g[l][None, :], seq)

        x2 = _ffn_call(x2, mix_a, mix_b, wo_b, ffn_norm_g[l][None, :], wgu_b, wd_b)
    return x2.reshape(batch, seq, d_model)
```

```python
import functools
import types

import numpy as np
import jax
import jax.numpy as jnp
from jax import lax
from jax.experimental import pallas as pl
from jax.experimental.pallas import tpu as pltpu

F32 = jnp.float32
BF16 = jnp.bfloat16

D_MODEL = 1024
HEAD_DIM = 64
NSA_HEADS = 8
NSA_KV_HEADS = 2
GQA_REP = NSA_HEADS // NSA_KV_HEADS
NSA_WIDTH = NSA_HEADS * HEAD_DIM
CONV_CH = D_MODEL - NSA_WIDTH
CMP_LEN = 32
CMP_STRIDE = 16
CMP_HIDDEN = 4 * HEAD_DIM
SLC_BLOCK = 64
SLC_TOPN = 8
WINDOW = 512
CONV_WIDTH = 31
FFN_HIDDEN = 2816
ROPE_THETA = 10000.0
EPS = 1e-6
NEG = -1e30
LOG2E = 1.4426950408889634
FORCE = 1e6

LANES = 128
Q_TILE = 128
TILES_PER_STEP = 2
SLC_CHUNK = 256
WIN_KEYS = WINDOW + Q_TILE
PROJ_TILE = 512
CMP_SEQS = 4
CONV_TILE = 512
CONV_HALO = 32
CONV_ROWS = 256
FFN_TILE = 512
FFN_CHUNK = 256
VMEM_LIMIT = 56 * 1024 * 1024

C_Q = 0
C_CMP = 512
C_SLC = 768
C_WIN = 1024
MAIN_COLS = 1280
GATE_COLS = 3 * NSA_HEADS

_TRANS_B = (((1,), (1,)), ((), ()))


def _dot(a, b):
    return jnp.dot(a, b, preferred_element_type=F32)


def _dot_tb(a, b):
    return lax.dot_general(a, b, _TRANS_B, preferred_element_type=F32)


def _split_bf16(x):
    hi = x.astype(BF16)
    lo = (x - hi.astype(F32)).astype(BF16)
    return hi, lo


def _sigmoid(x):
    return 1.0 / (1.0 + jnp.exp(-x))


def _head_norm_rope(y, gain, bd, cos, sin, first_half, scale):
    ss = _dot((y * y).astype(BF16), bd)
    yn = y * lax.rsqrt(ss * (1.0 / HEAD_DIM) + EPS) * gain
    outs = []
    for j in range(y.shape[1] // LANES):
        blk = yn[:, j * LANES:(j + 1) * LANES]
        rot = jnp.where(first_half, pltpu.roll(blk, LANES - HEAD_DIM // 2, 1),
                        pltpu.roll(blk, HEAD_DIM // 2, 1))
        outs.append((blk * cos + rot * sin) * scale)
    return outs


def _value_t_with_ones(v):
    vt = v.T
    row = lax.broadcasted_iota(jnp.int32, vt.shape, 0)
    return (jnp.where(row < HEAD_DIM, vt, 1.0).astype(BF16),
            jnp.where(row < HEAD_DIM, 1.0, vt).astype(BF16))


def _proj_body(x_ref, gin_ref, w_ref, cos_ref, sin_ref, gq_ref, gk_ref, bd_ref,
               wgu_in_ref, wd_in_ref, wo_in_ref,
               q_ref, ks_ref, kw_ref, vs0_ref, vs1_ref, vw0_ref, vw1_ref, kc_ref, vc_ref, gt_ref,
               hcv_ref, wgu_out_ref, wd_out_ref, wo_out_ref, wm_ref, wc_ref, wg_ref):
    @pl.when(pl.program_id(0) == 0)
    def _():
        for j in range(MAIN_COLS // LANES):
            wm_ref[:, j * LANES:(j + 1) * LANES] = w_ref[j * LANES:(j + 1) * LANES, :].T.astype(BF16)
        gl = jnp.concatenate([w_ref[MAIN_COLS:MAIN_COLS + GATE_COLS, :],
                              jnp.zeros((LANES - GATE_COLS, D_MODEL), F32)], axis=0)
        wg_ref[...] = gl.T.astype(BF16)
        for j in range(2 * CONV_CH // LANES):
            r0 = MAIN_COLS + GATE_COLS + j * LANES
            wc_ref[:, j * LANES:(j + 1) * LANES] = w_ref[r0:r0 + LANES, :].T.astype(BF16)

    wgu_out_ref[...] = wgu_in_ref[...].astype(BF16)
    wd_out_ref[...] = wd_in_ref[...].astype(BF16)
    wo_out_ref[...] = wo_in_ref[...].astype(BF16)

    x = x_ref[...]
    xg = (x * gin_ref[...]).astype(BF16)
    rs = lax.rsqrt(jnp.mean(x * x, axis=-1, keepdims=True) + EPS)
    cos = cos_ref[...]
    sin = sin_ref[...]
    bd = bd_ref[...]
    lane = lax.broadcasted_iota(jnp.int32, (PROJ_TILE, LANES), 1)
    first_half = (lane & (HEAD_DIM // 2)) == 0
    held = {}

    def queries(group, y):
        held["q", group] = _head_norm_rope(y, gq_ref[...], bd, cos, sin, first_half,
                                           HEAD_DIM ** -0.5 * LOG2E)
        if group == 0:
            return
        low = lane < HEAD_DIM
        for j in range(2):
            a, b = held["q", 0][j], held["q", 1][j]
            q_ref[:, (2 * j) * LANES:(2 * j + 1) * LANES] = jnp.where(
                low, a, pltpu.roll(b, HEAD_DIM, 1)).astype(BF16)
            q_ref[:, (2 * j + 1) * LANES:(2 * j + 2) * LANES] = jnp.where(
                low, pltpu.roll(a, HEAD_DIM, 1), b).astype(BF16)

    def values(y, out0_ref, out1_ref):
        for i in range(PROJ_TILE // LANES):
            out0_ref[i], out1_ref[i] = _value_t_with_ones(y[i * LANES:(i + 1) * LANES, :])

    def slc_kv(y):
        held["ks"] = y[:, :LANES]
        values(y[:, LANES:], vs0_ref, vs1_ref)

    def win_kv(y):
        kblks = _head_norm_rope(_lane_cat([held["ks"], y[:, :LANES]]), gk_ref[...], bd, cos, sin,
                                first_half, 1.0)
        ks_ref[...] = kblks[0].astype(BF16)
        kw_ref[...] = kblks[1].astype(BF16)
        values(y[:, LANES:], vw0_ref, vw1_ref)

    def cmp_inputs(y):
        kc_ref[...] = y[:, :LANES]
        vc_ref[...] = y[:, LANES:]

    def conv_lin(y):
        held["ca"] = y

    def conv_glu(y):
        hcv_ref[...] = held["ca"] * _sigmoid(y)

    def gates(y):
        sg = _sigmoid(y)
        for i in range(PROJ_TILE // LANES):
            gt_ref[i] = sg[i * LANES:(i + 1) * LANES, :].T[0:32, :]

    sections = [
        (wm_ref, C_Q, 256, functools.partial(queries, 0)),
        (wm_ref, C_Q + 256, 256, functools.partial(queries, 1)),
        (wc_ref, 0, CONV_CH, conv_lin),
        (wm_ref, C_SLC, 256, slc_kv),
        (wc_ref, CONV_CH, CONV_CH, conv_glu),
        (wm_ref, C_WIN, 256, win_kv),
        (wm_ref, C_CMP, 256, cmp_inputs),
        (wg_ref, 0, LANES, gates),
    ]

    def project(sec):
        w_ref, c0, width, _ = sec
        return _dot(xg, w_ref[:, c0:c0 + width])

    y_next = project(sections[0])
    for i, sec in enumerate(sections):
        y_cur = y_next
        if i + 1 < len(sections):
            y_next = project(sections[i + 1])
        sec[3](y_cur * rs)


def _paired_row_block(i):
    head = i // 2
    src = ((head % NSA_KV_HEADS) * GQA_REP + head // NSA_KV_HEADS) * 2 + i % 2
    return jnp.where(i < NSA_WIDTH // 32, src, i)


def _proj_call(x2, gin, w_in, cos, sin, gq, gk, bd, w_gate_up, w_down, w_out, seq):
    t = x2.shape[0]
    nt = t // PROJ_TILE
    assert D_MODEL // nt == 32 and nt % 16 == 0 and (FFN_HIDDEN // 16) % 16 == 0
    wd_rows = FFN_HIDDEN // 16
    tiles_per_seq = seq // PROJ_TILE
    row = lambda i: (i, 0)
    const = lambda i: (0, 0)
    tab = lambda i: (i % tiles_per_seq, 0)
    blk3 = lambda i: (i, 0, 0)
    wd_blk = lambda i: (i * 16 // nt, 0)
    nb = PROJ_TILE // LANES
    return pl.pallas_call(
        _proj_body,
        grid=(nt,),
        in_specs=[
            pl.BlockSpec((PROJ_TILE, D_MODEL), row),
            pl.BlockSpec((1, D_MODEL), const),
            pl.BlockSpec(w_in.shape, const, pipeline_mode=pl.Buffered(1)),
            pl.BlockSpec((PROJ_TILE, LANES), tab),
            pl.BlockSpec((PROJ_TILE, LANES), tab),
            pl.BlockSpec((1, 256), const),
            pl.BlockSpec((1, 256), const),
            pl.BlockSpec((256, 256), const),
            pl.BlockSpec((32, 2 * FFN_HIDDEN), row),
            pl.BlockSpec((wd_rows, D_MODEL), wd_blk),
            pl.BlockSpec((32, D_MODEL), lambda i: (_paired_row_block(i), 0)),
        ],
        out_specs=[
            pl.BlockSpec((PROJ_TILE, NSA_WIDTH), row),
            pl.BlockSpec((PROJ_TILE, LANES), row),
            pl.BlockSpec((PROJ_TILE, LANES), row),
            pl.BlockSpec((nb, LANES, LANES), blk3),
            pl.BlockSpec((nb, LANES, LANES), blk3),
            pl.BlockSpec((nb, LANES, LANES), blk3),
            pl.BlockSpec((nb, LANES, LANES), blk3),
            pl.BlockSpec((PROJ_TILE, LANES), row),
            pl.BlockSpec((PROJ_TILE, LANES), row),
            pl.BlockSpec((nb, 32, LANES), blk3),
            pl.BlockSpec((PROJ_TILE, CONV_CH), row),
            pl.BlockSpec((32, 2 * FFN_HIDDEN), row),
            pl.BlockSpec((wd_rows, D_MODEL), wd_blk),
            pl.BlockSpec((32, D_MODEL), row),
        ],
        out_shape=[
            jax.ShapeDtypeStruct((t, NSA_WIDTH), BF16),
            jax.ShapeDtypeStruct((t, LANES), BF16),
            jax.ShapeDtypeStruct((t, LANES), BF16),
            jax.ShapeDtypeStruct((t // LANES, LANES, LANES), BF16),
            jax.ShapeDtypeStruct((t // LANES, LANES, LANES), BF16),
            jax.ShapeDtypeStruct((t // LANES, LANES, LANES), BF16),
            jax.ShapeDtypeStruct((t // LANES, LANES, LANES), BF16),
            jax.ShapeDtypeStruct((t, LANES), F32),
            jax.ShapeDtypeStruct((t, LANES), F32),
            jax.ShapeDtypeStruct((t // LANES, 32, LANES), F32),
            jax.ShapeDtypeStruct((t, CONV_CH), F32),
            jax.ShapeDtypeStruct((D_MODEL, 2 * FFN_HIDDEN), BF16),
            jax.ShapeDtypeStruct((FFN_HIDDEN, D_MODEL), BF16),
            jax.ShapeDtypeStruct((D_MODEL, D_MODEL), BF16),
        ],
        scratch_shapes=[pltpu.VMEM((D_MODEL, MAIN_COLS), BF16),
                        pltpu.VMEM((D_MODEL, 2 * CONV_CH), BF16),
                        pltpu.VMEM((D_MODEL, LANES), BF16)],
        compiler_params=pltpu.CompilerParams(
            dimension_semantics=("arbitrary",), vmem_limit_bytes=VMEM_LIMIT),
        name="proj",
    )(x2, gin, w_in, cos, sin, gq, gk, bd, w_gate_up, w_down, w_out)


def _cmp_mlp(t_ref, pe_ref, w1_ref, w2_ref):
    half = CMP_LEN // 2
    nchunk = t_ref.shape[0] // CMP_STRIDE
    toks = [t_ref[pl.ds(l, nchunk, stride=CMP_STRIDE), :] for l in range(half)]
    xp = _lane_cat([(toks[l] + pe_ref[l:l + 1, :]).astype(BF16) for l in range(half)])
    xq = _lane_cat([(toks[l] + pe_ref[half + l:half + l + 1, :]).astype(BF16) for l in range(half)])
    h1 = _dot(xp, w1_ref[0]) + pltpu.roll(_dot(xq, w1_ref[1]), nchunk - 1, 0)
    act = (h1 * _sigmoid(h1)).astype(BF16)
    return _dot(act, w2_ref[...])


def _cmp_body(kc_ref, vc_ref, pek_ref, w1k_ref, w2k_ref, pev_ref, w1v_ref, w2v_ref,
              cos_ref, sin_ref, gk_ref, bd_ref, ko_ref, vt0_ref, vt1_ref):
    kraw = _cmp_mlp(kc_ref, pek_ref, w1k_ref, w2k_ref)
    lane = lax.broadcasted_iota(jnp.int32, kraw.shape, 1)
    first_half = (lane & (HEAD_DIM // 2)) == 0
    (kr,) = _head_norm_rope(kraw, gk_ref[...], bd_ref[...], cos_ref[...], sin_ref[...],
                            first_half, 1.0)
    ko_ref[...] = kr.astype(BF16)
    vraw = _cmp_mlp(vc_ref, pev_ref, w1v_ref, w2v_ref)
    for i in range(CMP_SEQS):
        vt0_ref[i], vt1_ref[i] = _value_t_with_ones(vraw[i * LANES:(i + 1) * LANES, :])


def _cmp_call(kc, vc, pek, w1k, w2k, pev, w1v, w2v, cos, sin, gk, bd, batch, seq):
    seqblk = lambda b: (b, 0)
    c2 = lambda b: (0, 0)
    c3 = lambda b: (0, 0, 0)
    ncmp_pad = seq // CMP_STRIDE
    assert ncmp_pad == LANES and batch % CMP_SEQS == 0
    rows = CMP_SEQS * ncmp_pad
    half_k = (CMP_LEN // 2) * LANES
    return pl.pallas_call(
        _cmp_body,
        grid=(batch // CMP_SEQS,),
        in_specs=[
            pl.BlockSpec((CMP_SEQS * seq, LANES), seqblk),
            pl.BlockSpec((CMP_SEQS * seq, LANES), seqblk),
            pl.BlockSpec((CMP_LEN, LANES), c2),
            pl.BlockSpec((2, half_k, 2 * CMP_HIDDEN), c3),
            pl.BlockSpec((2 * CMP_HIDDEN, LANES), c2),
            pl.BlockSpec((CMP_LEN, LANES), c2),
            pl.BlockSpec((2, half_k, 2 * CMP_HIDDEN), c3),
            pl.BlockSpec((2 * CMP_HIDDEN, LANES), c2),
            pl.BlockSpec((rows, LANES), c2),
            pl.BlockSpec((rows, LANES), c2),
            pl.BlockSpec((1, LANES), c2),
            pl.BlockSpec((LANES, LANES), c2),
        ],
        out_specs=[
            pl.BlockSpec((rows, LANES), seqblk),
            pl.BlockSpec((CMP_SEQS, LANES, ncmp_pad), lambda b: (b, 0, 0)),
            pl.BlockSpec((CMP_SEQS, LANES, ncmp_pad), lambda b: (b, 0, 0)),
        ],
        out_shape=[
            jax.ShapeDtypeStruct((batch * ncmp_pad, LANES), BF16),
            jax.ShapeDtypeStruct((batch, LANES, ncmp_pad), BF16),
            jax.ShapeDtypeStruct((batch, LANES, ncmp_pad), BF16),
        ],
        compiler_params=pltpu.CompilerParams(
            dimension_semantics=("parallel",), vmem_limit_bytes=VMEM_LIMIT),
        name="compress",
    )(kc, vc, pek, w1k, w2k, pev, w1v, w2v, cos, sin, gk, bd)


def _lane_cat(blocks):
    return jnp.concatenate(blocks, axis=1)


def _softmax_numer(s, bias, m_run=None):
    es, ms = [], []
    for r in range(GQA_REP):
        sm = s[:, r * LANES:(r + 1) * LANES] + bias
        m = jnp.max(sm, axis=0, keepdims=True)
        if m_run is not None:
            m = jnp.maximum(m, m_run[:, r * LANES:(r + 1) * LANES])
        es.append(jnp.exp2(sm - m))
        ms.append(m)
    return _lane_cat(es), _lane_cat(ms)


def _safe_inv(l):
    return 1.0 / jnp.where(l > 0.0, l, 1.0)


def _attn_body(*refs):
    first = pl.program_id(0) * TILES_PER_STEP
    n_max = refs[1].shape[0] // SLC_CHUNK
    for n in range(1, n_max + 1):
        pl.when((first * Q_TILE) // SLC_CHUNK + 1 == n)(functools.partial(_attn_step, n, first, *refs))


def _attn_step(n_chunks, first, *refs):
    def one_tile(slot, carry):
        tile = _tile_program(n_chunks, first + slot, slot, *refs)
        pending = None
        s_next = tile.scores(tile.units[0])
        for i, unit in enumerate(tile.units):
            s_cur = s_next
            if i + 1 < len(tile.units):
                s_next = tile.scores(tile.units[i + 1])
            e, alpha = tile.softmax(unit, s_cur)
            if pending is not None:
                tile.weighted_values(*pending)
            pending = (unit, e, alpha)
        tile.weighted_values(*pending)
        tile.finish()
        return carry

    lax.fori_loop(0, TILES_PER_STEP, one_tile, 0)


def _tile_program(n_chunks, c, slot, q_ref, ks_ref, kw_ref, vs0_ref, vs1_ref, vw0_ref, vw1_ref, kc_ref,
                  vc0_ref, vc1_ref, gt_ref, ovl_ref, gn_ref, out_ref):
    groups = range(NSA_KV_HEADS)
    q_rows = pl.ds(pl.multiple_of(slot * Q_TILE, Q_TILE), Q_TILE)
    vs_refs, vw_refs, vc_refs = (vs0_ref, vs1_ref), (vw0_ref, vw1_ref), (vc0_ref, vc1_ref)
    sum_row = (HEAD_DIM, 0)
    width = GQA_REP * LANES
    t_lane = c * Q_TILE + lax.broadcasted_iota(jnp.int32, (1, LANES), 1)
    lane_sq = lax.broadcasted_iota(jnp.int32, (Q_TILE, LANES), 1)
    n_live = n_chunks * (SLC_CHUNK // SLC_BLOCK)
    nsel = 32
    win_keys = min(WIN_KEYS, n_chunks * SLC_CHUNK)
    j_idx = lax.broadcasted_iota(jnp.int32, (nsel, LANES), 0)
    cur = lax.shift_right_logical(t_lane, 6)
    n_idx = lax.broadcasted_iota(jnp.int32, (LANES, LANES), 0)
    cbias = jnp.where((n_idx * CMP_STRIDE + (CMP_LEN - 1) <= t_lane) & (n_idx < LANES - 1), 0.0, NEG)
    has_cmp = jnp.where(t_lane >= CMP_LEN - 1, 1.0, 0.0)
    has_cmp = _lane_cat([has_cmp] * GQA_REP)
    win_blk = jnp.maximum(c - WINDOW // Q_TILE, 0)
    win_start = pl.multiple_of(win_blk * Q_TILE, Q_TILE)
    kpos_w = win_start + lax.broadcasted_iota(jnp.int32, (win_keys, LANES), 0)
    wbias = jnp.where((kpos_w <= t_lane) & (kpos_w > t_lane - WINDOW), 0.0, NEG)
    row_sl = lax.broadcasted_iota(jnp.int32, (SLC_CHUNK, LANES), 0)
    zero_q = jnp.zeros((Q_TILE, LANES), BF16)

    def gate_row(g, branch):
        rows = [(g * GQA_REP + r) * 3 + branch for r in range(GQA_REP)]
        return _lane_cat([gt_ref[slot, i:i + 1, :] for i in rows])

    qg = []
    for g in groups:
        in_g = (lane_sq >= HEAD_DIM) if g else (lane_sq < HEAD_DIM)
        qg.append(jnp.concatenate(
            [jnp.where(in_g, q_ref[q_rows, r * LANES:(r + 1) * LANES], zero_q) for r in range(GQA_REP)],
            axis=0))

    o_cmp = [None] * NSA_KV_HEADS
    sel_bias = [None] * NSA_KV_HEADS

    def compressed_softmax(g, sc):
        ec, _ = _softmax_numer(sc, cbias)
        coef = has_cmp * _safe_inv(jnp.sum(ec, axis=0, keepdims=True))
        pc = ec * coef
        psum = (pc[:, 0:LANES] + pc[:, LANES:2 * LANES]
                + pc[:, 2 * LANES:3 * LANES] + pc[:, 3 * LANES:4 * LANES])
        p_hi, p_lo = _split_bf16(psum)
        imp = (_dot(ovl_ref[...], p_hi) + _dot(ovl_ref[...], p_lo))[0:nsel, :]
        valid = j_idx <= cur
        forced = (j_idx == 0) | (j_idx == cur) | (j_idx == cur - 1)
        score = jnp.where(valid, imp + jnp.where(forced, FORCE, 0.0), -FORCE)
        rank = jnp.zeros((nsel, LANES), F32)
        for jp in range(nsel):
            row = score[jp:jp + 1, :]
            beats = (row > score) | ((row == score) & (j_idx > jp))
            rank = rank + jnp.where(beats, 1.0, 0.0)
        sel_bias[g] = jnp.where((rank < float(SLC_TOPN)) & (score > -1.0), 0.0, NEG)
        return ec.astype(BF16), coef

    nsub = SLC_CHUNK // LANES
    blocks_per_chunk = SLC_CHUNK // SLC_BLOCK
    kwin = kw_ref[pl.ds(win_start, win_keys), :]
    units = [("cmp", g, 0) for g in groups] + [("win", g, 0) for g in groups]
    units += [("slc", g, kc) for kc in range(n_chunks) for g in groups]
    m_run = [None] * NSA_KV_HEADS
    acc_s = [None] * NSA_KV_HEADS
    acc_w = [None] * NSA_KV_HEADS

    def scores(unit):
        kind, g, kc = unit
        if kind == "cmp":
            keys = kc_ref[...]
        elif kind == "win":
            keys = kwin
        else:
            keys = ks_ref[kc * SLC_CHUNK:(kc + 1) * SLC_CHUNK, :]
        return _dot_tb(keys, qg[g])

    def softmax(unit, s):
        kind, g, kc = unit
        if kind == "cmp":
            return compressed_softmax(g, s)
        if kind == "win":
            e, _ = _softmax_numer(s, wbias)
            return e.astype(BF16), None
        j0 = kc * blocks_per_chunk
        bias = jnp.concatenate(
            [jnp.broadcast_to(sel_bias[g][j:j + 1, :], (SLC_BLOCK, LANES))
             for j in range(j0, j0 + blocks_per_chunk)], axis=0)
        if kc == n_chunks - 1:
            bias = jnp.where(kc * SLC_CHUNK + row_sl <= t_lane, bias, NEG)
        e, m_new = _softmax_numer(s, bias, m_run[g])
        alpha = None if kc == 0 else jnp.exp2(m_run[g] - m_new)
        m_run[g] = m_new
        return e.astype(BF16), alpha

    def weighted_values(unit, e, alpha):
        kind, g, kc = unit
        if kind == "cmp":
            o_cmp[g] = _dot(vc_refs[g][0], e) * (gate_row(g, 0) * alpha)
            return
        if kind == "win":
            vt = _lane_cat([vw_refs[g][win_blk + i] for i in range(win_keys // LANES)])
            acc_w[g] = _dot(vt, e)
            return
        vt = _lane_cat([vs_refs[g][nsub * kc + i] for i in range(nsub)])
        pv = _dot(vt, e)
        acc_s[g] = pv if alpha is None else acc_s[g] * alpha + pv

    def finish():
        o_groups = []
        for g in groups:
            r0 = sum_row[g]
            o_groups.append(o_cmp[g]
                            + acc_s[g] * (gate_row(g, 1) * _safe_inv(acc_s[g][r0:r0 + 1, :]))
                            + acc_w[g] * (gate_row(g, 2) * _safe_inv(acc_w[g][r0:r0 + 1, :])))
        row_d = lax.broadcasted_iota(jnp.int32, (LANES, width), 0)
        o_t = jnp.where(row_d < HEAD_DIM, o_groups[0], o_groups[1])
        sq = jnp.sum(o_t * o_t, axis=0, keepdims=True)
        ss = (sq[:, 0:LANES] + sq[:, LANES:2 * LANES]
              + sq[:, 2 * LANES:3 * LANES] + sq[:, 3 * LANES:4 * LANES])
        rs = lax.rsqrt(ss * (1.0 / NSA_WIDTH) + EPS)
        for r in range(GQA_REP):
            cols = slice(r * LANES, (r + 1) * LANES)
            out_ref[q_rows, cols] = (o_t[:, cols] * rs * gn_ref[:, cols]).T.astype(BF16)

    return types.SimpleNamespace(units=units, scores=scores, softmax=softmax,
                                 weighted_values=weighted_values, finish=finish)


def _attn_call(q, ks, kw, vs0, vs1, vw0, vw1, kcmp, vc0, vc1, gt, ovl, gn, batch, seq):
    assert SLC_CHUNK % (TILES_PER_STEP * Q_TILE) == 0
    rows = TILES_PER_STEP * Q_TILE
    nq = seq // rows
    nkb = seq // LANES
    qrow = lambda c, b: (b * nq + c, 0)
    per_b = lambda c, b: (b, 0)
    per_b3 = lambda c, b: (b, 0, 0)
    vspec = pl.BlockSpec((nkb, LANES, LANES), per_b3)
    cspec = pl.BlockSpec((1, LANES, LANES), per_b3)
    return pl.pallas_call(
        _attn_body,
        grid=(nq, batch),
        in_specs=[
            pl.BlockSpec((rows, NSA_WIDTH), qrow),
            pl.BlockSpec((seq, LANES), per_b),
            pl.BlockSpec((seq, LANES), per_b),
            vspec, vspec, vspec, vspec,
            pl.BlockSpec((LANES, LANES), per_b),
            cspec, cspec,
            pl.BlockSpec((TILES_PER_STEP, 32, LANES), lambda c, b: (b * nq + c, 0, 0)),
            pl.BlockSpec((LANES, LANES), lambda c, b: (0, 0)),
            pl.BlockSpec((LANES, NSA_WIDTH), lambda c, b: (0, 0)),
        ],
        out_specs=pl.BlockSpec((rows, NSA_WIDTH), qrow),
        out_shape=jax.ShapeDtypeStruct((batch * seq, NSA_WIDTH), BF16),
        compiler_params=pltpu.CompilerParams(
            dimension_semantics=("parallel", "arbitrary"), vmem_limit_bytes=VMEM_LIMIT),
        name="attn",
    )(q, ks, kw, vs0, vs1, vw0, vw1, kcmp, vc0, vc1, gt, ovl, gn)


def _conv_body(tiles_per_seq, cur_ref, halo_ref, w_ref, b_ref, lng_ref, lnb_ref, gn_ref, x_ref, wo_ref,
               out_ref, buf_ref, sh_ref, cv_ref):
    first = (pl.program_id(0) % tiles_per_seq) == 0
    buf_ref[0:CONV_HALO, :] = jnp.where(first, 0.0, halo_ref[...])
    buf_ref[CONV_HALO:, :] = cur_ref[...]
    lead = CONV_HALO - (CONV_WIDTH - 1)
    ncb = CONV_CH // LANES
    nrb = CONV_TILE // CONV_ROWS
    for s in range(8):
        span = CONV_TILE + 8 * ((CONV_WIDTH - 1 - s) // 8)
        for cb in range(ncb):
            sh_ref[s, cb, 0:span, :] = buf_ref[lead + s:lead + s + span, cb * LANES:(cb + 1) * LANES]

    def conv_block(i, carry):
        cb = i % ncb
        base = pl.multiple_of((i // ncb) * CONV_ROWS, CONV_ROWS)
        acc = jnp.zeros((CONV_ROWS, LANES), F32) + b_ref[cb]
        for k in range(CONV_WIDTH):
            acc = acc + w_ref[cb, k:k + 1, :] * sh_ref[k % 8, cb, pl.ds(base + 8 * (k // 8), CONV_ROWS), :]
        cv_ref[cb, pl.ds(base, CONV_ROWS), :] = acc
        return carry

    lax.fori_loop(0, nrb * ncb, conv_block, 0)
    half_rows = CONV_TILE // 2
    for half in range(2):
        rows = slice(half * half_rows, (half + 1) * half_rows)
        h = _lane_cat([cv_ref[cb, rows, :] for cb in range(ncb)])
        mu = jnp.mean(h, axis=-1, keepdims=True)
        d = h - mu
        var = jnp.mean(d * d, axis=-1, keepdims=True)
        hn = d * lax.rsqrt(var + EPS) * lng_ref[...] + lnb_ref[...]
        o = hn * _sigmoid(hn)
        ms = jnp.mean(o * o, axis=-1, keepdims=True)
        mix = (o * lax.rsqrt(ms + EPS) * gn_ref[...]).astype(BF16)
        out_ref[rows, :] = x_ref[rows, :] + _dot(mix, wo_ref[NSA_WIDTH:D_MODEL, :])


def _conv_call(hcv, w, b, lng, lnb, gn, x2, wo, seq):
    t = hcv.shape[0]
    nt = t // CONV_TILE
    halo_per_tile = CONV_TILE // CONV_HALO
    ncb = CONV_CH // LANES
    row = lambda i: (i, 0)
    const = lambda i: (0, 0)
    return pl.pallas_call(
        functools.partial(_conv_body, seq // CONV_TILE),
        grid=(nt,),
        in_specs=[
            pl.BlockSpec((CONV_TILE, CONV_CH), row),
            pl.BlockSpec((CONV_HALO, CONV_CH), lambda i: (jnp.maximum(i * halo_per_tile - 1, 0), 0)),
            pl.BlockSpec((ncb, CONV_HALO, LANES), lambda i: (0, 0, 0)),
            pl.BlockSpec((ncb, 1, LANES), lambda i: (0, 0, 0)),
            pl.BlockSpec((1, CONV_CH), const),
            pl.BlockSpec((1, CONV_CH), const),
            pl.BlockSpec((1, CONV_CH), const),
            pl.BlockSpec((CONV_TILE, D_MODEL), row),
            pl.BlockSpec((D_MODEL, D_MODEL), const, pipeline_mode=pl.Buffered(1)),
        ],
        out_specs=pl.BlockSpec((CONV_TILE, D_MODEL), row),
        out_shape=jax.ShapeDtypeStruct((t, D_MODEL), F32),
        scratch_shapes=[pltpu.VMEM((CONV_HALO + CONV_TILE, CONV_CH), F32),
                        pltpu.VMEM((8, ncb, CONV_TILE + 8 * ((CONV_WIDTH - 1) // 8), LANES), F32),
                        pltpu.VMEM((ncb, CONV_TILE, LANES), F32)],
        compiler_params=pltpu.CompilerParams(
            dimension_semantics=("parallel",), vmem_limit_bytes=VMEM_LIMIT),
        name="conv",
    )(hcv, hcv, w, b, lng, lnb, gn, x2, wo)


def _ffn_body(x_ref, ma_ref, wo_ref, gf_ref, wgu_ref, wd_ref, out_ref):
    h = x_ref[...] + _dot(ma_ref[...], wo_ref[0:NSA_WIDTH, :])
    ms = jnp.mean(h * h, axis=-1, keepdims=True)
    hn = (h * lax.rsqrt(ms + EPS) * gf_ref[...]).astype(BF16)
    acc = jnp.zeros_like(h)
    for j in range(FFN_HIDDEN // FFN_CHUNK):
        c0 = j * FFN_CHUNK
        gte = _dot(hn, wgu_ref[:, c0:c0 + FFN_CHUNK])
        up = _dot(hn, wgu_ref[:, FFN_HIDDEN + c0:FFN_HIDDEN + c0 + FFN_CHUNK])
        act = (gte * _sigmoid(gte) * up).astype(BF16)
        acc = acc + _dot(act, wd_ref[c0:c0 + FFN_CHUNK, :])
    out_ref[...] = h + acc


def _ffn_call(x2, ma, wo, gf, wgu, wd):
    t = x2.shape[0]
    nt = t // FFN_TILE
    row = lambda i: (i, 0)
    const = lambda i: (0, 0)
    once = pl.Buffered(1)
    return pl.pallas_call(
        _ffn_body,
        grid=(nt,),
        in_specs=[
            pl.BlockSpec((FFN_TILE, D_MODEL), row),
            pl.BlockSpec((FFN_TILE, NSA_WIDTH), row),
            pl.BlockSpec((NSA_WIDTH, D_MODEL), const, pipeline_mode=once),
            pl.BlockSpec((1, D_MODEL), const),
            pl.BlockSpec((D_MODEL, 2 * FFN_HIDDEN), const, pipeline_mode=once),
            pl.BlockSpec((FFN_HIDDEN, D_MODEL), const, pipeline_mode=once),
        ],
        out_specs=pl.BlockSpec((FFN_TILE, D_MODEL), row),
        out_shape=jax.ShapeDtypeStruct((t, D_MODEL), F32),
        compiler_params=pltpu.CompilerParams(
            dimension_semantics=("parallel",), vmem_limit_bytes=VMEM_LIMIT),
        name="ffn",
    )(x2, ma, wo, gf, wgu, wd)


def _pair_heads(a, axis):
    shape = a.shape
    a = a.reshape(shape[:axis] + (NSA_KV_HEADS, GQA_REP, HEAD_DIM) + shape[axis + 1:])
    return jnp.swapaxes(a, axis, axis + 1).reshape(shape)


def _transposed_gain(g):
    cols = g.reshape(GQA_REP, LANES).T
    return jnp.broadcast_to(cols[:, :, None], (LANES, GQA_REP, Q_TILE)).reshape(LANES, GQA_REP * Q_TILE)


def _rope_tables(pos):
    half = HEAD_DIM // 2
    inv = ROPE_THETA ** (-np.arange(half, dtype=np.float64) / half)
    ang = np.asarray(pos, np.float64)[:, None] * inv[None, :]
    cos = np.concatenate([np.cos(ang), np.cos(ang)], axis=1)
    sin = np.concatenate([-np.sin(ang), np.sin(ang)], axis=1)
    return (np.tile(cos, (1, NSA_KV_HEADS)).astype(np.float32),
            np.tile(sin, (1, NSA_KV_HEADS)).astype(np.float32))


def _block_diag_ones(width):
    idx = np.arange(width) // HEAD_DIM
    return (idx[:, None] == idx[None, :]).astype(np.float32)


def _overlap_t(seq):
    ncmp = (seq - CMP_LEN) // CMP_STRIDE + 1
    cs = np.arange(ncmp)[:, None] * CMP_STRIDE
    ss = np.arange(seq // SLC_BLOCK)[None, :] * SLC_BLOCK
    ov = np.clip(np.minimum(cs + CMP_LEN, ss + SLC_BLOCK) - np.maximum(cs, ss), 0, None) / CMP_LEN
    out = np.zeros((LANES, LANES), np.float32)
    out[:seq // SLC_BLOCK, :ncmp] = ov.T
    return out


def _cmp_weights(pe, w1, w2):
    pe2 = jnp.concatenate([pe, pe], axis=1)
    w1r = w1.reshape(CMP_LEN, HEAD_DIM, CMP_HIDDEN)
    z1 = jnp.zeros_like(w1r)
    w1b = jnp.concatenate([jnp.concatenate([w1r, z1], axis=2),
                           jnp.concatenate([z1, w1r], axis=2)], axis=1).astype(BF16)
    z2 = jnp.zeros_like(w2)
    w2b = jnp.concatenate([jnp.concatenate([w2, z2], axis=1),
                           jnp.concatenate([z2, w2], axis=1)], axis=0).astype(BF16)
    w1b = w1b.reshape(2, (CMP_LEN // 2) * LANES, 2 * CMP_HIDDEN)
    return pe2, w1b, w2b


def kernel(x, attn_norm_g, w_in, q_norm_g, k_norm_cmp_g, k_norm_slc_g, k_norm_win_g, cmp_pe_k, cmp_w1_k, cmp_w2_k, cmp_pe_v, cmp_w1_v, cmp_w2_v, conv_dw_w, conv_dw_b, conv_ln_g, conv_ln_b, out_norm_nsa_g, out_norm_conv_g, w_out, ffn_norm_g, w_gate_up, w_down):
    batch, seq, d_model = x.shape
    assert d_model == D_MODEL and seq % PROJ_TILE == 0 and seq // SLC_BLOCK == 32
    depth = w_in.shape[0]
    cos_np, sin_np = _rope_tables(np.arange(seq))
    ccos_np, csin_np = _rope_tables(np.arange(seq // CMP_STRIDE) * CMP_STRIDE + CMP_LEN - 1)
    ccos_np, csin_np = np.tile(ccos_np, (CMP_SEQS, 1)), np.tile(csin_np, (CMP_SEQS, 1))
    cos, sin, ccos, csin = map(jnp.asarray, (cos_np, sin_np, ccos_np, csin_np))
    bd256 = jnp.asarray(_block_diag_ones(256), BF16)
    bd128 = jnp.asarray(_block_diag_ones(LANES), BF16)
    ovl = jnp.asarray(_overlap_t(seq), BF16)

    x2 = x.reshape(batch * seq, d_model)
    for l in range(depth):
        w = w_in[l]
        gq = jnp.tile(q_norm_g[l], 4)[None, :]
        gk = jnp.concatenate([jnp.tile(k_norm_slc_g[l], 2), jnp.tile(k_norm_win_g[l], 2)])[None, :]
        q, ks, kw, vs0, vs1, vw0, vw1, kc, vc, gt, hcv, wgu_b, wd_b, wo_b = _proj_call(
            x2, attn_norm_g[l][None, :], w.T, cos, sin, gq, gk, bd256,
            w_gate_up[l], w_down[l], w_out[l], seq)

        pek, w1k, w2k = _cmp_weights(cmp_pe_k[l], cmp_w1_k[l], cmp_w2_k[l])
        pev, w1v, w2v = _cmp_weights(cmp_pe_v[l], cmp_w1_v[l], cmp_w2_v[l])
        kcmp, vc0, vc1 = _cmp_call(kc, vc, pek, w1k, w2k, pev, w1v, w2v, ccos, csin,
                                jnp.tile(k_norm_cmp_g[l], 2)[None, :], bd128, batch, seq)

        mix_a = _attn_call(q, ks, kw, vs0, vs1, vw0, vw1, kcmp, vc0, vc1, gt, ovl,
                           _transposed_gain(_pair_heads(out_norm_nsa_g[l], 0)), batch, seq)

        w_dw = jnp.concatenate([conv_dw_w[l][:, 0, :],
                                jnp.zeros((CONV_HALO - CONV_WIDTH, CONV_CH), F32)], axis=0)
        w_dw = w_dw.reshape(CONV_HALO, CONV_CH // LANES, LANES).transpose(1, 0, 2)
        x2 = _conv_call(hcv, w_dw, conv_dw_b[l].reshape(CONV_CH // LANES, 1, LANES),
                        conv_ln_g[l][None, :], conv_ln_b[l][None, :], out_norm_conv_g[l][None, :],
                        x2, wo_b, seq)

        x2 = _ffn_call(x2, mix_a, wo_b, ffn_norm_g[l][None, :], wgu_b, wd_b)
    return x2.reshape(batch, seq, d_model)
```

```python
import functools
import types

import numpy as np
import jax
import jax.numpy as jnp
from jax import lax
from jax.experimental import pallas as pl
from jax.experimental.pallas import tpu as pltpu

F32 = jnp.float32
BF16 = jnp.bfloat16

D_MODEL = 1024
HEAD_DIM = 64
NSA_HEADS = 8
NSA_KV_HEADS = 2
GQA_REP = NSA_HEADS // NSA_KV_HEADS
NSA_WIDTH = NSA_HEADS * HEAD_DIM
CONV_CH = D_MODEL - NSA_WIDTH
CMP_LEN = 32
CMP_STRIDE = 16
CMP_HIDDEN = 4 * HEAD_DIM
SLC_BLOCK = 64
SLC_TOPN = 8
WINDOW = 512
CONV_WIDTH = 31
FFN_HIDDEN = 2816
ROPE_THETA = 10000.0
EPS = 1e-6
NEG = -1e30
LOG2E = 1.4426950408889634
FORCE = 1e6

LANES = 128
Q_TILE = 128
TILES_PER_STEP = 2
SLC_CHUNK = 256
WIN_KEYS = WINDOW + Q_TILE
PROJ_TILE = 512
CMP_SEQS = 4
CONV_TILE = 512
CONV_HALO = 32
CONV_ROWS = 256
FFN_TILE = 512
FFN_CHUNK = 256
VMEM_LIMIT = 56 * 1024 * 1024

C_Q = 0
C_CMP = 512
C_SLC = 768
C_WIN = 1024
MAIN_COLS = 1280
GATE_COLS = 3 * NSA_HEADS

_TRANS_B = (((1,), (1,)), ((), ()))


def _dot(a, b):
    return jnp.dot(a, b, preferred_element_type=F32)


def _dot_tb(a, b):
    return lax.dot_general(a, b, _TRANS_B, preferred_element_type=F32)


def _split_bf16(x):
    hi = x.astype(BF16)
    lo = (x - hi.astype(F32)).astype(BF16)
    return hi, lo


def _sigmoid(x):
    return 1.0 / (1.0 + jnp.exp(-x))


def _head_norm_rope(y, gain, bd, cos, sin, first_half, scale):
    ss = _dot((y * y).astype(BF16), bd)
    yn = y * lax.rsqrt(ss * (1.0 / HEAD_DIM) + EPS) * gain
    outs = []
    for j in range(y.shape[1] // LANES):
        blk = yn[:, j * LANES:(j + 1) * LANES]
        rot = jnp.where(first_half, pltpu.roll(blk, LANES - HEAD_DIM // 2, 1),
                        pltpu.roll(blk, HEAD_DIM // 2, 1))
        outs.append((blk * cos + rot * sin) * scale)
    return outs


def _value_t_with_ones(v):
    vt = v.T
    row = lax.broadcasted_iota(jnp.int32, vt.shape, 0)
    return (jnp.where(row < HEAD_DIM, vt, 1.0).astype(BF16),
            jnp.where(row < HEAD_DIM, 1.0, vt).astype(BF16))


def _proj_body(x_ref, gin_ref, w_ref, cos_ref, sin_ref, gq_ref, gk_ref, bd_ref,
               wgu_in_ref, wd_in_ref, wo_in_ref,
               q_ref, ks_ref, kw_ref, vs0_ref, vs1_ref, vw0_ref, vw1_ref, kc_ref, vc_ref, gt_ref,
               hcv_ref, wgu_out_ref, wd_out_ref, wo_out_ref, wm_ref, wc_ref, wg_ref):
    @pl.when(pl.program_id(0) == 0)
    def _():
        for j in range(MAIN_COLS // LANES):
            wm_ref[:, j * LANES:(j + 1) * LANES] = w_ref[j * LANES:(j + 1) * LANES, :].T.astype(BF16)
        gl = jnp.concatenate([w_ref[MAIN_COLS:MAIN_COLS + GATE_COLS, :],
                              jnp.zeros((LANES - GATE_COLS, D_MODEL), F32)], axis=0)
        wg_ref[...] = gl.T.astype(BF16)
        for j in range(2 * CONV_CH // LANES):
            r0 = MAIN_COLS + GATE_COLS + j * LANES
            wc_ref[:, j * LANES:(j + 1) * LANES] = w_ref[r0:r0 + LANES, :].T.astype(BF16)

    wgu_out_ref[...] = wgu_in_ref[...].astype(BF16)
    wd_out_ref[...] = wd_in_ref[...].astype(BF16)
    wo_out_ref[...] = wo_in_ref[...].astype(BF16)

    x = x_ref[...]
    xg = (x * gin_ref[...]).astype(BF16)
    rs = lax.rsqrt(jnp.mean(x * x, axis=-1, keepdims=True) + EPS)
    cos = cos_ref[...]
    sin = sin_ref[...]
    bd = bd_ref[...]
    lane = lax.broadcasted_iota(jnp.int32, (PROJ_TILE, LANES), 1)
    first_half = (lane & (HEAD_DIM // 2)) == 0
    held = {}

    def queries(group, y):
        held["q", group] = _head_norm_rope(y, gq_ref[...], bd, cos, sin, first_half,
                                           HEAD_DIM ** -0.5 * LOG2E)
        if group == 0:
            return
        low = lane < HEAD_DIM
        for j in range(2):
            a, b = held["q", 0][j], held["q", 1][j]
            q_ref[:, (2 * j) * LANES:(2 * j + 1) * LANES] = jnp.where(
                low, a, pltpu.roll(b, HEAD_DIM, 1)).astype(BF16)
            q_ref[:, (2 * j + 1) * LANES:(2 * j + 2) * LANES] = jnp.where(
                low, pltpu.roll(a, HEAD_DIM, 1), b).astype(BF16)

    def values(y, out0_ref, out1_ref):
        for i in range(PROJ_TILE // LANES):
            out0_ref[i], out1_ref[i] = _value_t_with_ones(y[i * LANES:(i + 1) * LANES, :])

    def slc_kv(y):
        held["ks"] = y[:, :LANES]
        values(y[:, LANES:], vs0_ref, vs1_ref)

    def win_kv(y):
        kblks = _head_norm_rope(_lane_cat([held["ks"], y[:, :LANES]]), gk_ref[...], bd, cos, sin,
                                first_half, 1.0)
        ks_ref[...] = kblks[0].astype(BF16)
        kw_ref[...] = kblks[1].astype(BF16)
        values(y[:, LANES:], vw0_ref, vw1_ref)

    def cmp_inputs(y):
        kc_ref[...] = y[:, :LANES]
        vc_ref[...] = y[:, LANES:]

    def conv_lin(y):
        held["ca"] = y

    def conv_glu(y):
        hcv_ref[...] = held["ca"] * _sigmoid(y)

    def gates(y):
        sg = _sigmoid(y)
        for i in range(PROJ_TILE // LANES):
            gt_ref[i] = sg[i * LANES:(i + 1) * LANES, :].T[0:32, :]

    sections = [
        (wm_ref, C_Q, 256, functools.partial(queries, 0)),
        (wm_ref, C_Q + 256, 256, functools.partial(queries, 1)),
        (wc_ref, 0, CONV_CH, conv_lin),
        (wm_ref, C_SLC, 256, slc_kv),
        (wc_ref, CONV_CH, CONV_CH, conv_glu),
        (wm_ref, C_WIN, 256, win_kv),
        (wm_ref, C_CMP, 256, cmp_inputs),
        (wg_ref, 0, LANES, gates),
    ]

    def project(sec):
        w_ref, c0, width, _ = sec
        return _dot(xg, w_ref[:, c0:c0 + width])

    y_next = project(sections[0])
    for i, sec in enumerate(sections):
        y_cur = y_next
        if i + 1 < len(sections):
            y_next = project(sections[i + 1])
        sec[3](y_cur * rs)


def _paired_row_block(i):
    head = i // 2
    src = ((head % NSA_KV_HEADS) * GQA_REP + head // NSA_KV_HEADS) * 2 + i % 2
    return jnp.where(i < NSA_WIDTH // 32, src, i)


def _proj_call(x2, gin, w_in, cos, sin, gq, gk, bd, w_gate_up, w_down, w_out, seq):
    t = x2.shape[0]
    nt = t // PROJ_TILE
    assert D_MODEL // nt == 32 and nt % 16 == 0 and (FFN_HIDDEN // 16) % 16 == 0
    wd_rows = FFN_HIDDEN // 16
    tiles_per_seq = seq // PROJ_TILE
    row = lambda i: (i, 0)
    const = lambda i: (0, 0)
    tab = lambda i: (i % tiles_per_seq, 0)
    blk3 = lambda i: (i, 0, 0)
    wd_blk = lambda i: (i * 16 // nt, 0)
    nb = PROJ_TILE // LANES
    return pl.pallas_call(
        _proj_body,
        grid=(nt,),
        in_specs=[
            pl.BlockSpec((PROJ_TILE, D_MODEL), row),
            pl.BlockSpec((1, D_MODEL), const),
            pl.BlockSpec(w_in.shape, const, pipeline_mode=pl.Buffered(1)),
            pl.BlockSpec((PROJ_TILE, LANES), tab),
            pl.BlockSpec((PROJ_TILE, LANES), tab),
            pl.BlockSpec((1, 256), const),
            pl.BlockSpec((1, 256), const),
            pl.BlockSpec((256, 256), const),
            pl.BlockSpec((32, 2 * FFN_HIDDEN), row),
            pl.BlockSpec((wd_rows, D_MODEL), wd_blk),
            pl.BlockSpec((32, D_MODEL), lambda i: (_paired_row_block(i), 0)),
        ],
        out_specs=[
            pl.BlockSpec((PROJ_TILE, NSA_WIDTH), row),
            pl.BlockSpec((PROJ_TILE, LANES), row),
            pl.BlockSpec((PROJ_TILE, LANES), row),
            pl.BlockSpec((nb, LANES, LANES), blk3),
            pl.BlockSpec((nb, LANES, LANES), blk3),
            pl.BlockSpec((nb, LANES, LANES), blk3),
            pl.BlockSpec((nb, LANES, LANES), blk3),
            pl.BlockSpec((PROJ_TILE, LANES), row),
            pl.BlockSpec((PROJ_TILE, LANES), row),
            pl.BlockSpec((nb, 32, LANES), blk3),
            pl.BlockSpec((PROJ_TILE, CONV_CH), row),
            pl.BlockSpec((32, 2 * FFN_HIDDEN), row),
            pl.BlockSpec((wd_rows, D_MODEL), wd_blk),
            pl.BlockSpec((32, D_MODEL), row),
        ],
        out_shape=[
            jax.ShapeDtypeStruct((t, NSA_WIDTH), BF16),
            jax.ShapeDtypeStruct((t, LANES), BF16),
            jax.ShapeDtypeStruct((t, LANES), BF16),
            jax.ShapeDtypeStruct((t // LANES, LANES, LANES), BF16),
            jax.ShapeDtypeStruct((t // LANES, LANES, LANES), BF16),
            jax.ShapeDtypeStruct((t // LANES, LANES, LANES), BF16),
            jax.ShapeDtypeStruct((t // LANES, LANES, LANES), BF16),
            jax.ShapeDtypeStruct((t, LANES), F32),
            jax.ShapeDtypeStruct((t, LANES), F32),
            jax.ShapeDtypeStruct((t // LANES, 32, LANES), F32),
            jax.ShapeDtypeStruct((t, CONV_CH), F32),
            jax.ShapeDtypeStruct((D_MODEL, 2 * FFN_HIDDEN), BF16),
            jax.ShapeDtypeStruct((FFN_HIDDEN, D_MODEL), BF16),
            jax.ShapeDtypeStruct((D_MODEL, D_MODEL), BF16),
        ],
        scratch_shapes=[pltpu.VMEM((D_MODEL, MAIN_COLS), BF16),
                        pltpu.VMEM((D_MODEL, 2 * CONV_CH), BF16),
                        pltpu.VMEM((D_MODEL, LANES), BF16)],
        compiler_params=pltpu.CompilerParams(
            dimension_semantics=("arbitrary",), vmem_limit_bytes=VMEM_LIMIT),
        name="proj",
    )(x2, gin, w_in, cos, sin, gq, gk, bd, w_gate_up, w_down, w_out)


def _cmp_mlp(t_ref, pe_ref, w1_ref, w2_ref):
    half = CMP_LEN // 2
    nchunk = t_ref.shape[0] // CMP_STRIDE
    toks = [t_ref[pl.ds(l, nchunk, stride=CMP_STRIDE), :] for l in range(half)]
    xp = _lane_cat([(toks[l] + pe_ref[l:l + 1, :]).astype(BF16) for l in range(half)])
    xq = _lane_cat([(toks[l] + pe_ref[half + l:half + l + 1, :]).astype(BF16) for l in range(half)])
    h1 = _dot(xp, w1_ref[0]) + pltpu.roll(_dot(xq, w1_ref[1]), nchunk - 1, 0)
    act = (h1 * _sigmoid(h1)).astype(BF16)
    return _dot(act, w2_ref[...])


def _cmp_body(kc_ref, vc_ref, pek_ref, w1k_ref, w2k_ref, pev_ref, w1v_ref, w2v_ref,
              cos_ref, sin_ref, gk_ref, bd_ref, ko_ref, vt0_ref, vt1_ref):
    kraw = _cmp_mlp(kc_ref, pek_ref, w1k_ref, w2k_ref)
    lane = lax.broadcasted_iota(jnp.int32, kraw.shape, 1)
    first_half = (lane & (HEAD_DIM // 2)) == 0
    (kr,) = _head_norm_rope(kraw, gk_ref[...], bd_ref[...], cos_ref[...], sin_ref[...],
                            first_half, 1.0)
    ko_ref[...] = kr.astype(BF16)
    vraw = _cmp_mlp(vc_ref, pev_ref, w1v_ref, w2v_ref)
    for i in range(CMP_SEQS):
        vt0_ref[i], vt1_ref[i] = _value_t_with_ones(vraw[i * LANES:(i + 1) * LANES, :])


def _cmp_call(kc, vc, pek, w1k, w2k, pev, w1v, w2v, cos, sin, gk, bd, batch, seq):
    seqblk = lambda b: (b, 0)
    c2 = lambda b: (0, 0)
    c3 = lambda b: (0, 0, 0)
    ncmp_pad = seq // CMP_STRIDE
    assert ncmp_pad == LANES and batch % CMP_SEQS == 0
    rows = CMP_SEQS * ncmp_pad
    half_k = (CMP_LEN // 2) * LANES
    return pl.pallas_call(
        _cmp_body,
        grid=(batch // CMP_SEQS,),
        in_specs=[
            pl.BlockSpec((CMP_SEQS * seq, LANES), seqblk),
            pl.BlockSpec((CMP_SEQS * seq, LANES), seqblk),
            pl.BlockSpec((CMP_LEN, LANES), c2),
            pl.BlockSpec((2, half_k, 2 * CMP_HIDDEN), c3),
            pl.BlockSpec((2 * CMP_HIDDEN, LANES), c2),
            pl.BlockSpec((CMP_LEN, LANES), c2),
            pl.BlockSpec((2, half_k, 2 * CMP_HIDDEN), c3),
            pl.BlockSpec((2 * CMP_HIDDEN, LANES), c2),
            pl.BlockSpec((rows, LANES), c2),
            pl.BlockSpec((rows, LANES), c2),
            pl.BlockSpec((1, LANES), c2),
            pl.BlockSpec((LANES, LANES), c2),
        ],
        out_specs=[
            pl.BlockSpec((rows, LANES), seqblk),
            pl.BlockSpec((CMP_SEQS, LANES, ncmp_pad), lambda b: (b, 0, 0)),
            pl.BlockSpec((CMP_SEQS, LANES, ncmp_pad), lambda b: (b, 0, 0)),
        ],
        out_shape=[
            jax.ShapeDtypeStruct((batch * ncmp_pad, LANES), BF16),
            jax.ShapeDtypeStruct((batch, LANES, ncmp_pad), BF16),
            jax.ShapeDtypeStruct((batch, LANES, ncmp_pad), BF16),
        ],
        compiler_params=pltpu.CompilerParams(
            dimension_semantics=("parallel",), vmem_limit_bytes=VMEM_LIMIT),
        name="compress",
    )(kc, vc, pek, w1k, w2k, pev, w1v, w2v, cos, sin, gk, bd)


def _lane_cat(blocks):
    return jnp.concatenate(blocks, axis=1)


def _softmax_numer(s, bias, m_run=None):
    es, ms = [], []
    for r in range(GQA_REP):
        sm = s[:, r * LANES:(r + 1) * LANES] + bias
        m = jnp.max(sm, axis=0, keepdims=True)
        if m_run is not None:
            m = jnp.maximum(m, m_run[:, r * LANES:(r + 1) * LANES])
        es.append(jnp.exp2(sm - m))
        ms.append(m)
    return _lane_cat(es), _lane_cat(ms)


def _safe_inv(l):
    return 1.0 / jnp.where(l > 0.0, l, 1.0)


def _attn_body(*refs):
    first = pl.program_id(0) * TILES_PER_STEP
    n_max = refs[1].shape[0] // SLC_CHUNK
    for n in range(1, n_max + 1):
        pl.when((first * Q_TILE) // SLC_CHUNK + 1 == n)(functools.partial(_attn_step, n, first, *refs))


def _attn_step(n_chunks, first, *refs):
    def one_tile(slot, carry):
        tile = _tile_program(n_chunks, first + slot, slot, *refs)
        pending = None
        s_next = tile.scores(tile.units[0])
        for i, unit in enumerate(tile.units):
            s_cur = s_next
            if i + 1 < len(tile.units):
                s_next = tile.scores(tile.units[i + 1])
            e, alpha = tile.softmax(unit, s_cur)
            if pending is not None:
                tile.weighted_values(*pending)
            pending = (unit, e, alpha)
        tile.weighted_values(*pending)
        tile.finish()
        return carry

    lax.fori_loop(0, TILES_PER_STEP, one_tile, 0)


def _tile_program(n_chunks, c, slot, q_ref, ks_ref, kw_ref, vs0_ref, vs1_ref, vw0_ref, vw1_ref, kc_ref,
                  vc0_ref, vc1_ref, gt_ref, ovl_ref, gn_ref, out_ref):
    groups = range(NSA_KV_HEADS)
    q_rows = pl.ds(pl.multiple_of(slot * Q_TILE, Q_TILE), Q_TILE)
    vs_refs, vw_refs, vc_refs = (vs0_ref, vs1_ref), (vw0_ref, vw1_ref), (vc0_ref, vc1_ref)
    sum_row = (HEAD_DIM, 0)
    width = GQA_REP * LANES
    t_lane = c * Q_TILE + lax.broadcasted_iota(jnp.int32, (1, LANES), 1)
    lane_sq = lax.broadcasted_iota(jnp.int32, (Q_TILE, LANES), 1)
    n_live = n_chunks * (SLC_CHUNK // SLC_BLOCK)
    nsel = 32
    win_keys = min(WIN_KEYS, n_chunks * SLC_CHUNK)
    j_idx = lax.broadcasted_iota(jnp.int32, (nsel, LANES), 0)
    cur = lax.shift_right_logical(t_lane, 6)
    n_idx = lax.broadcasted_iota(jnp.int32, (LANES, LANES), 0)
    cbias = jnp.where((n_idx * CMP_STRIDE + (CMP_LEN - 1) <= t_lane) & (n_idx < LANES - 1), 0.0, NEG)
    has_cmp = jnp.where(t_lane >= CMP_LEN - 1, 1.0, 0.0)
    has_cmp = _lane_cat([has_cmp] * GQA_REP)
    win_blk = jnp.maximum(c - WINDOW // Q_TILE, 0)
    win_start = pl.multiple_of(win_blk * Q_TILE, Q_TILE)
    kpos_w = win_start + lax.broadcasted_iota(jnp.int32, (win_keys, LANES), 0)
    wbias = jnp.where((kpos_w <= t_lane) & (kpos_w > t_lane - WINDOW), 0.0, NEG)
    row_sl = lax.broadcasted_iota(jnp.int32, (SLC_CHUNK, LANES), 0)
    zero_q = jnp.zeros((Q_TILE, LANES), BF16)

    def gate_row(g, branch):
        rows = [(g * GQA_REP + r) * 3 + branch for r in range(GQA_REP)]
        return _lane_cat([gt_ref[slot, i:i + 1, :] for i in rows])

    qg = []
    for g in groups:
        in_g = (lane_sq >= HEAD_DIM) if g else (lane_sq < HEAD_DIM)
        qg.append(jnp.concatenate(
            [jnp.where(in_g, q_ref[q_rows, r * LANES:(r + 1) * LANES], zero_q) for r in range(GQA_REP)],
            axis=0))

    o_cmp = [None] * NSA_KV_HEADS
    sel_bias = [None] * NSA_KV_HEADS

    def compressed_softmax(g, sc):
        ec, _ = _softmax_numer(sc, cbias)
        coef = has_cmp * _safe_inv(jnp.sum(ec, axis=0, keepdims=True))
        pc = ec * coef
        psum = (pc[:, 0:LANES] + pc[:, LANES:2 * LANES]
                + pc[:, 2 * LANES:3 * LANES] + pc[:, 3 * LANES:4 * LANES])
        p_hi, p_lo = _split_bf16(psum)
        imp = (_dot(ovl_ref[...], p_hi) + _dot(ovl_ref[...], p_lo))[0:nsel, :]
        valid = j_idx <= cur
        forced = (j_idx == 0) | (j_idx == cur) | (j_idx == cur - 1)
        score = jnp.where(valid, imp + jnp.where(forced, FORCE, 0.0), -FORCE)
        rank = jnp.zeros((nsel, LANES), F32)
        for jp in range(nsel):
            row = score[jp:jp + 1, :]
            beats = (row > score) | ((row == score) & (j_idx > jp))
            rank = rank + jnp.where(beats, 1.0, 0.0)
        sel_bias[g] = jnp.where((rank < float(SLC_TOPN)) & (score > -1.0), 0.0, NEG)
        return ec.astype(BF16), coef

    nsub = SLC_CHUNK // LANES
    blocks_per_chunk = SLC_CHUNK // SLC_BLOCK
    kwin = kw_ref[pl.ds(win_start, win_keys), :]
    units = [("cmp", g, 0) for g in groups] + [("win", g, 0) for g in groups]
    units += [("slc", g, kc) for kc in range(n_chunks) for g in groups]
    m_run = [None] * NSA_KV_HEADS
    acc_s = [None] * NSA_KV_HEADS
    acc_w = [None] * NSA_KV_HEADS

    def scores(unit):
        kind, g, kc = unit
        if kind == "cmp":
            keys = kc_ref[...]
        elif kind == "win":
            keys = kwin
        else:
            keys = ks_ref[kc * SLC_CHUNK:(kc + 1) * SLC_CHUNK, :]
        return _dot_tb(keys, qg[g])

    def softmax(unit, s):
        kind, g, kc = unit
        if kind == "cmp":
            return compressed_softmax(g, s)
        if kind == "win":
            e, _ = _softmax_numer(s, wbias)
            return e.astype(BF16), None
        j0 = kc * blocks_per_chunk
        bias = jnp.concatenate(
            [jnp.broadcast_to(sel_bias[g][j:j + 1, :], (SLC_BLOCK, LANES))
             for j in range(j0, j0 + blocks_per_chunk)], axis=0)
        if kc == n_chunks - 1:
            bias = jnp.where(kc * SLC_CHUNK + row_sl <= t_lane, bias, NEG)
        e, m_new = _softmax_numer(s, bias, m_run[g])
        alpha = None if kc == 0 else jnp.exp2(m_run[g] - m_new)
        m_run[g] = m_new
        return e.astype(BF16), alpha

    def weighted_values(unit, e, alpha):
        kind, g, kc = unit
        if kind == "cmp":
            o_cmp[g] = _dot(vc_refs[g][0], e) * (gate_row(g, 0) * alpha)
            return
        if kind == "win":
            vt = _lane_cat([vw_refs[g][win_blk + i] for i in range(win_keys // LANES)])
            acc_w[g] = _dot(vt, e)
            return
        vt = _lane_cat([vs_refs[g][nsub * kc + i] for i in range(nsub)])
        pv = _dot(vt, e)
        acc_s[g] = pv if alpha is None else acc_s[g] * alpha + pv

    def finish():
        o_groups = []
        for g in groups:
            r0 = sum_row[g]
            o_groups.append(o_cmp[g]
                            + acc_s[g] * (gate_row(g, 1) * _safe_inv(acc_s[g][r0:r0 + 1, :]))
                            + acc_w[g] * (gate_row(g, 2) * _safe_inv(acc_w[g][r0:r0 + 1, :])))
        row_d = lax.broadcasted_iota(jnp.int32, (LANES, width), 0)
        o_t = jnp.where(row_d < HEAD_DIM, o_groups[0], o_groups[1])
        sq = jnp.sum(o_t * o_t, axis=0, keepdims=True)
        ss = (sq[:, 0:LANES] + sq[:, LANES:2 * LANES]
              + sq[:, 2 * LANES:3 * LANES] + sq[:, 3 * LANES:4 * LANES])
        rs = lax.rsqrt(ss * (1.0 / NSA_WIDTH) + EPS)
        for r in range(GQA_REP):
            cols = slice(r * LANES, (r + 1) * LANES)
            out_ref[q_rows, cols] = (o_t[:, cols] * rs * gn_ref[:, cols]).T.astype(BF16)

    return types.SimpleNamespace(units=units, scores=scores, softmax=softmax,
                                 weighted_values=weighted_values, finish=finish)


def _attn_call(q, ks, kw, vs0, vs1, vw0, vw1, kcmp, vc0, vc1, gt, ovl, gn, batch, seq):
    assert SLC_CHUNK % (TILES_PER_STEP * Q_TILE) == 0
    rows = TILES_PER_STEP * Q_TILE
    nq = seq // rows
    nkb = seq // LANES
    qrow = lambda c, b: (b * nq + c, 0)
    per_b = lambda c, b: (b, 0)
    per_b3 = lambda c, b: (b, 0, 0)
    vspec = pl.BlockSpec((nkb, LANES, LANES), per_b3)
    cspec = pl.BlockSpec((1, LANES, LANES), per_b3)
    return pl.pallas_call(
        _attn_body,
        grid=(nq, batch),
        in_specs=[
            pl.BlockSpec((rows, NSA_WIDTH), qrow),
            pl.BlockSpec((seq, LANES), per_b),
            pl.BlockSpec((seq, LANES), per_b),
            vspec, vspec, vspec, vspec,
            pl.BlockSpec((LANES, LANES), per_b),
            cspec, cspec,
            pl.BlockSpec((TILES_PER_STEP, 32, LANES), lambda c, b: (b * nq + c, 0, 0)),
            pl.BlockSpec((LANES, LANES), lambda c, b: (0, 0)),
            pl.BlockSpec((LANES, NSA_WIDTH), lambda c, b: (0, 0)),
        ],
        out_specs=pl.BlockSpec((rows, NSA_WIDTH), qrow),
        out_shape=jax.ShapeDtypeStruct((batch * seq, NSA_WIDTH), BF16),
        compiler_params=pltpu.CompilerParams(
            dimension_semantics=("parallel", "arbitrary"), vmem_limit_bytes=VMEM_LIMIT),
        name="attn",
    )(q, ks, kw, vs0, vs1, vw0, vw1, kcmp, vc0, vc1, gt, ovl, gn)


def _conv_body(tiles_per_seq, cur_ref, halo_ref, w_ref, b_ref, lng_ref, lnb_ref, gn_ref, x_ref, wo_ref,
               out_ref, buf_ref, sh_ref, cv_ref, mix_a_ref, mix_b_ref):
    step = pl.program_id(0)
    last = pl.num_programs(0) - 1
    lead = CONV_HALO - (CONV_WIDTH - 1)
    ncb = CONV_CH // LANES
    nrb = CONV_TILE // CONV_ROWS
    out_cols = D_MODEL // ncb
    pieces = [(rb * CONV_ROWS, cc) for rb in range(nrb) for cc in range(ncb)]

    def project_prev(rd_ref, base, cc):
        cols = slice(cc * out_cols, (cc + 1) * out_cols)
        out_ref[base:base + CONV_ROWS, cols] = x_ref[base:base + CONV_ROWS, cols] + _dot(
            rd_ref[base:base + CONV_ROWS, :], wo_ref[NSA_WIDTH:D_MODEL, cols])

    @pl.when(step == 0)
    def _():
        mix_b_ref[...] = jnp.zeros_like(mix_b_ref)

    def conv_tile(rd_ref, wr_ref):
        first = (step % tiles_per_seq) == 0
        buf_ref[0:CONV_HALO, :] = jnp.where(first, 0.0, halo_ref[...])
        buf_ref[CONV_HALO:, :] = cur_ref[...]
        for s in range(8):
            span = CONV_TILE + 8 * ((CONV_WIDTH - 1 - s) // 8)
            for cb in range(ncb):
                sh_ref[s, cb, 0:span, :] = buf_ref[lead + s:lead + s + span, cb * LANES:(cb + 1) * LANES]

        def conv_block(i, carry):
            cb = i % ncb
            base = pl.multiple_of((i // ncb) * CONV_ROWS, CONV_ROWS)
            acc = jnp.zeros((CONV_ROWS, LANES), F32) + b_ref[cb]
            for k in range(CONV_WIDTH):
                acc = acc + w_ref[cb, k:k + 1, :] * sh_ref[k % 8, cb, pl.ds(base + 8 * (k // 8), CONV_ROWS), :]
            cv_ref[cb, pl.ds(base, CONV_ROWS), :] = acc
            return carry

        lax.fori_loop(0, nrb * ncb, conv_block, 0)
        chunk = CONV_TILE // len(pieces)
        for c, (base, cc) in enumerate(pieces):
            project_prev(rd_ref, base, cc)
            rows = slice(c * chunk, (c + 1) * chunk)
            h = _lane_cat([cv_ref[cb, rows, :] for cb in range(ncb)])
            mu = jnp.mean(h, axis=-1, keepdims=True)
            d = h - mu
            var = jnp.mean(d * d, axis=-1, keepdims=True)
            hn = d * lax.rsqrt(var + EPS) * lng_ref[...] + lnb_ref[...]
            o = hn * _sigmoid(hn)
            ms = jnp.mean(o * o, axis=-1, keepdims=True)
            wr_ref[rows, :] = (o * lax.rsqrt(ms + EPS) * gn_ref[...]).astype(BF16)

    def drain(rd_ref):
        for base, cc in pieces:
            project_prev(rd_ref, base, cc)

    even = step % 2 == 0
    pl.when((step < last) & even)(lambda: conv_tile(mix_b_ref, mix_a_ref))
    pl.when((step < last) & jnp.logical_not(even))(lambda: conv_tile(mix_a_ref, mix_b_ref))
    pl.when((step == last) & even)(lambda: drain(mix_b_ref))
    pl.when((step == last) & jnp.logical_not(even))(lambda: drain(mix_a_ref))


def _conv_call(hcv, w, b, lng, lnb, gn, x2, wo, seq):
    t = hcv.shape[0]
    nt = t // CONV_TILE
    halo_per_tile = CONV_TILE // CONV_HALO
    ncb = CONV_CH // LANES
    cur = lambda i: (jnp.minimum(i, nt - 1), 0)
    prev = lambda i: (jnp.maximum(i - 1, 0), 0)
    const = lambda i: (0, 0)
    return pl.pallas_call(
        functools.partial(_conv_body, seq // CONV_TILE),
        grid=(nt + 1,),
        in_specs=[
            pl.BlockSpec((CONV_TILE, CONV_CH), cur),
            pl.BlockSpec((CONV_HALO, CONV_CH),
                         lambda i: (jnp.maximum(jnp.minimum(i, nt - 1) * halo_per_tile - 1, 0), 0)),
            pl.BlockSpec((ncb, CONV_HALO, LANES), lambda i: (0, 0, 0)),
            pl.BlockSpec((ncb, 1, LANES), lambda i: (0, 0, 0)),
            pl.BlockSpec((1, CONV_CH), const),
            pl.BlockSpec((1, CONV_CH), const),
            pl.BlockSpec((1, CONV_CH), const),
            pl.BlockSpec((CONV_TILE, D_MODEL), prev),
            pl.BlockSpec((D_MODEL, D_MODEL), const, pipeline_mode=pl.Buffered(1)),
        ],
        out_specs=pl.BlockSpec((CONV_TILE, D_MODEL), prev),
        out_shape=jax.ShapeDtypeStruct((t, D_MODEL), F32),
        scratch_shapes=[pltpu.VMEM((CONV_HALO + CONV_TILE, CONV_CH), F32),
                        pltpu.VMEM((8, ncb, CONV_TILE + 8 * ((CONV_WIDTH - 1) // 8), LANES), F32),
                        pltpu.VMEM((ncb, CONV_TILE, LANES), F32),
                        pltpu.VMEM((CONV_TILE, CONV_CH), BF16),
                        pltpu.VMEM((CONV_TILE, CONV_CH), BF16)],
        compiler_params=pltpu.CompilerParams(
            dimension_semantics=("arbitrary",), vmem_limit_bytes=VMEM_LIMIT),
        name="conv",
    )(hcv, hcv, w, b, lng, lnb, gn, x2, wo)


def _ffn_body(x_ref, ma_ref, wo_ref, gf_ref, wgu_ref, wd_ref, out_ref):
    h = x_ref[...] + _dot(ma_ref[...], wo_ref[0:NSA_WIDTH, :])
    ms = jnp.mean(h * h, axis=-1, keepdims=True)
    hn = (h * lax.rsqrt(ms + EPS) * gf_ref[...]).astype(BF16)
    acc = jnp.zeros_like(h)
    for j in range(FFN_HIDDEN // FFN_CHUNK):
        c0 = j * FFN_CHUNK
        gte = _dot(hn, wgu_ref[:, c0:c0 + FFN_CHUNK])
        up = _dot(hn, wgu_ref[:, FFN_HIDDEN + c0:FFN_HIDDEN + c0 + FFN_CHUNK])
        act = (gte * _sigmoid(gte) * up).astype(BF16)
        acc = acc + _dot(act, wd_ref[c0:c0 + FFN_CHUNK, :])
    out_ref[...] = h + acc


def _ffn_call(x2, ma, wo, gf, wgu, wd):
    t = x2.shape[0]
    nt = t // FFN_TILE
    row = lambda i: (i, 0)
    const = lambda i: (0, 0)
    once = pl.Buffered(1)
    return pl.pallas_call(
        _ffn_body,
        grid=(nt,),
        in_specs=[
            pl.BlockSpec((FFN_TILE, D_MODEL), row),
            pl.BlockSpec((FFN_TILE, NSA_WIDTH), row),
            pl.BlockSpec((NSA_WIDTH, D_MODEL), const, pipeline_mode=once),
            pl.BlockSpec((1, D_MODEL), const),
            pl.BlockSpec((D_MODEL, 2 * FFN_HIDDEN), const, pipeline_mode=once),
            pl.BlockSpec((FFN_HIDDEN, D_MODEL), const, pipeline_mode=once),
        ],
        out_specs=pl.BlockSpec((FFN_TILE, D_MODEL), row),
        out_shape=jax.ShapeDtypeStruct((t, D_MODEL), F32),
        compiler_params=pltpu.CompilerParams(
            dimension_semantics=("parallel",), vmem_limit_bytes=VMEM_LIMIT),
        name="ffn",
    )(x2, ma, wo, gf, wgu, wd)


def _pair_heads(a, axis):
    shape = a.shape
    a = a.reshape(shape[:axis] + (NSA_KV_HEADS, GQA_REP, HEAD_DIM) + shape[axis + 1:])
    return jnp.swapaxes(a, axis, axis + 1).reshape(shape)


def _transposed_gain(g):
    cols = g.reshape(GQA_REP, LANES).T
    return jnp.broadcast_to(cols[:, :, None], (LANES, GQA_REP, Q_TILE)).reshape(LANES, GQA_REP * Q_TILE)


def _rope_tables(pos):
    half = HEAD_DIM // 2
    inv = ROPE_THETA ** (-np.arange(half, dtype=np.float64) / half)
    ang = np.asarray(pos, np.float64)[:, None] * inv[None, :]
    cos = np.concatenate([np.cos(ang), np.cos(ang)], axis=1)
    sin = np.concatenate([-np.sin(ang), np.sin(ang)], axis=1)
    return (np.tile(cos, (1, NSA_KV_HEADS)).astype(np.float32),
            np.tile(sin, (1, NSA_KV_HEADS)).astype(np.float32))


def _block_diag_ones(width):
    idx = np.arange(width) // HEAD_DIM
    return (idx[:, None] == idx[None, :]).astype(np.float32)


def _overlap_t(seq):
    ncmp = (seq - CMP_LEN) // CMP_STRIDE + 1
    cs = np.arange(ncmp)[:, None] * CMP_STRIDE
    ss = np.arange(seq // SLC_BLOCK)[None, :] * SLC_BLOCK
    ov = np.clip(np.minimum(cs + CMP_LEN, ss + SLC_BLOCK) - np.maximum(cs, ss), 0, None) / CMP_LEN
    out = np.zeros((LANES, LANES), np.float32)
    out[:seq // SLC_BLOCK, :ncmp] = ov.T
    return out


def _cmp_weights(pe, w1, w2):
    pe2 = jnp.concatenate([pe, pe], axis=1)
    w1r = w1.reshape(CMP_LEN, HEAD_DIM, CMP_HIDDEN)
    z1 = jnp.zeros_like(w1r)
    w1b = jnp.concatenate([jnp.concatenate([w1r, z1], axis=2),
                           jnp.concatenate([z1, w1r], axis=2)], axis=1).astype(BF16)
    z2 = jnp.zeros_like(w2)
    w2b = jnp.concatenate([jnp.concatenate([w2, z2], axis=1),
                           jnp.concatenate([z2, w2], axis=1)], axis=0).astype(BF16)
    w1b = w1b.reshape(2, (CMP_LEN // 2) * LANES, 2 * CMP_HIDDEN)
    return pe2, w1b, w2b


def kernel(x, attn_norm_g, w_in, q_norm_g, k_norm_cmp_g, k_norm_slc_g, k_norm_win_g, cmp_pe_k, cmp_w1_k, cmp_w2_k, cmp_pe_v, cmp_w1_v, cmp_w2_v, conv_dw_w, conv_dw_b, conv_ln_g, conv_ln_b, out_norm_nsa_g, out_norm_conv_g, w_out, ffn_norm_g, w_gate_up, w_down):
    batch, seq, d_model = x.shape
    assert d_model == D_MODEL and seq % PROJ_TILE == 0 and seq // SLC_BLOCK == 32
    depth = w_in.shape[0]
    cos_np, sin_np = _rope_tables(np.arange(seq))
    ccos_np, csin_np = _rope_tables(np.arange(seq // CMP_STRIDE) * CMP_STRIDE + CMP_LEN - 1)
    ccos_np, csin_np = np.tile(ccos_np, (CMP_SEQS, 1)), np.tile(csin_np, (CMP_SEQS, 1))
    cos, sin, ccos, csin = map(jnp.asarray, (cos_np, sin_np, ccos_np, csin_np))
    bd256 = jnp.asarray(_block_diag_ones(256), BF16)
    bd128 = jnp.asarray(_block_diag_ones(LANES), BF16)
    ovl = jnp.asarray(_overlap_t(seq), BF16)

    x2 = x.reshape(batch * seq, d_model)
    for l in range(depth):
        w = w_in[l]
        gq = jnp.tile(q_norm_g[l], 4)[None, :]
        gk = jnp.concatenate([jnp.tile(k_norm_slc_g[l], 2), jnp.tile(k_norm_win_g[l], 2)])[None, :]
        q, ks, kw, vs0, vs1, vw0, vw1, kc, vc, gt, hcv, wgu_b, wd_b, wo_b = _proj_call(
            x2, attn_norm_g[l][None, :], w.T, cos, sin, gq, gk, bd256,
            w_gate_up[l], w_down[l], w_out[l], seq)

        pek, w1k, w2k = _cmp_weights(cmp_pe_k[l], cmp_w1_k[l], cmp_w2_k[l])
        pev, w1v, w2v = _cmp_weights(cmp_pe_v[l], cmp_w1_v[l], cmp_w2_v[l])
        kcmp, vc0, vc1 = _cmp_call(kc, vc, pek, w1k, w2k, pev, w1v, w2v, ccos, csin,
                                jnp.tile(k_norm_cmp_g[l], 2)[None, :], bd128, batch, seq)

        mix_a = _attn_call(q, ks, kw, vs0, vs1, vw0, vw1, kcmp, vc0, vc1, gt, ovl,
                           _transposed_gain(_pair_heads(out_norm_nsa_g[l], 0)), batch, seq)

        w_dw = jnp.concatenate([conv_dw_w[l][:, 0, :],
                                jnp.zeros((CONV_HALO - CONV_WIDTH, CONV_CH), F32)], axis=0)
        w_dw = w_dw.reshape(CONV_HALO, CONV_CH // LANES, LANES).transpose(1, 0, 2)
        x2 = _conv_call(hcv, w_dw, conv_dw_b[l].reshape(CONV_CH // LANES, 1, LANES),
                        conv_ln_g[l][None, :], conv_ln_b[l][None, :], out_norm_conv_g[l][None, :],
                        x2, wo_b, seq)

        x2 = _ffn_call(x2, mix_a, wo_b, ffn_norm_g[l][None, :], wgu_b, wd_b)
    return x2.reshape(batch, seq, d_model)
```

```python
import functools
import types

import numpy as np
import jax
import jax.numpy as jnp
from jax import lax
from jax.experimental import pallas as pl
from jax.experimental.pallas import tpu as pltpu

F32 = jnp.float32
BF16 = jnp.bfloat16

D_MODEL = 1024
HEAD_DIM = 64
NSA_HEADS = 8
NSA_KV_HEADS = 2
GQA_REP = NSA_HEADS // NSA_KV_HEADS
NSA_WIDTH = NSA_HEADS * HEAD_DIM
CONV_CH = D_MODEL - NSA_WIDTH
CMP_LEN = 32
CMP_STRIDE = 16
CMP_HIDDEN = 4 * HEAD_DIM
SLC_BLOCK = 64
SLC_TOPN = 8
WINDOW = 512
CONV_WIDTH = 31
FFN_HIDDEN = 2816
ROPE_THETA = 10000.0
EPS = 1e-6
NEG = -1e30
LOG2E = 1.4426950408889634
FORCE = 1e6

LANES = 128
Q_TILE = 128
TILES_PER_STEP = 2
SLC_CHUNK = 256
WIN_KEYS = WINDOW + Q_TILE
PROJ_TILE = 1024
CMP_SEQS = 4
CONV_TILE = 512
CONV_HALO = 32
CONV_ROWS = 256
FFN_TILE = 512
FFN_CHUNK = 256
VMEM_LIMIT = 56 * 1024 * 1024

C_Q = 0
C_CMP = 512
C_SLC = 768
C_WIN = 1024
MAIN_COLS = 1280
GATE_COLS = 3 * NSA_HEADS

_TRANS_B = (((1,), (1,)), ((), ()))


def _dot(a, b):
    return jnp.dot(a, b, preferred_element_type=F32)


def _dot_tb(a, b):
    return lax.dot_general(a, b, _TRANS_B, preferred_element_type=F32)


def _split_bf16(x):
    hi = x.astype(BF16)
    lo = (x - hi.astype(F32)).astype(BF16)
    return hi, lo


def _sigmoid(x):
    return 1.0 / (1.0 + jnp.exp(-x))


def _head_norm_rope(y, gain, bd, cos, sin, first_half, scale):
    ss = _dot((y * y).astype(BF16), bd)
    yn = y * lax.rsqrt(ss * (1.0 / HEAD_DIM) + EPS) * gain
    outs = []
    for j in range(y.shape[1] // LANES):
        blk = yn[:, j * LANES:(j + 1) * LANES]
        rot = jnp.where(first_half, pltpu.roll(blk, LANES - HEAD_DIM // 2, 1),
                        pltpu.roll(blk, HEAD_DIM // 2, 1))
        outs.append((blk * cos + rot * sin) * scale)
    return outs


def _value_t_with_ones(v):
    vt = v.T
    row = lax.broadcasted_iota(jnp.int32, vt.shape, 0)
    return (jnp.where(row < HEAD_DIM, vt, 1.0).astype(BF16),
            jnp.where(row < HEAD_DIM, 1.0, vt).astype(BF16))


def _proj_body(x_ref, gin_ref, w_ref, cos_ref, sin_ref, gq_ref, gk_ref, bd_ref,
               wgu_in_ref, wd_in_ref, wo_in_ref,
               q_ref, ks_ref, kw_ref, vs0_ref, vs1_ref, vw0_ref, vw1_ref, kc_ref, vc_ref, gt_ref,
               hcv_ref, wgu_out_ref, wd_out_ref, wo_out_ref, wm_ref, wc_ref, wg_ref):
    @pl.when(pl.program_id(0) == 0)
    def _():
        for j in range(MAIN_COLS // LANES):
            wm_ref[:, j * LANES:(j + 1) * LANES] = w_ref[j * LANES:(j + 1) * LANES, :].T.astype(BF16)
        gl = jnp.concatenate([w_ref[MAIN_COLS:MAIN_COLS + GATE_COLS, :],
                              jnp.zeros((LANES - GATE_COLS, D_MODEL), F32)], axis=0)
        wg_ref[...] = gl.T.astype(BF16)
        for j in range(2 * CONV_CH // LANES):
            r0 = MAIN_COLS + GATE_COLS + j * LANES
            wc_ref[:, j * LANES:(j + 1) * LANES] = w_ref[r0:r0 + LANES, :].T.astype(BF16)

    wgu_out_ref[...] = wgu_in_ref[...].astype(BF16)
    wd_out_ref[...] = wd_in_ref[...].astype(BF16)
    wo_out_ref[...] = wo_in_ref[...].astype(BF16)

    x = x_ref[...]
    xg = (x * gin_ref[...]).astype(BF16)
    rs = lax.rsqrt(jnp.mean(x * x, axis=-1, keepdims=True) + EPS)
    cos = cos_ref[...]
    sin = sin_ref[...]
    bd = bd_ref[...]
    lane = lax.broadcasted_iota(jnp.int32, (PROJ_TILE, LANES), 1)
    first_half = (lane & (HEAD_DIM // 2)) == 0
    held = {}

    def queries(group, y):
        held["q", group] = _head_norm_rope(y, gq_ref[...], bd, cos, sin, first_half,
                                           HEAD_DIM ** -0.5 * LOG2E)
        if group == 0:
            return
        low = lane < HEAD_DIM
        for j in range(2):
            a, b = held["q", 0][j], held["q", 1][j]
            q_ref[:, (2 * j) * LANES:(2 * j + 1) * LANES] = jnp.where(
                low, a, pltpu.roll(b, HEAD_DIM, 1)).astype(BF16)
            q_ref[:, (2 * j + 1) * LANES:(2 * j + 2) * LANES] = jnp.where(
                low, pltpu.roll(a, HEAD_DIM, 1), b).astype(BF16)

    def values(y, out0_ref, out1_ref):
        for i in range(PROJ_TILE // LANES):
            out0_ref[i], out1_ref[i] = _value_t_with_ones(y[i * LANES:(i + 1) * LANES, :])

    def slc_kv(y):
        held["ks"] = y[:, :LANES]
        values(y[:, LANES:], vs0_ref, vs1_ref)

    def win_kv(y):
        kblks = _head_norm_rope(_lane_cat([held["ks"], y[:, :LANES]]), gk_ref[...], bd, cos, sin,
                                first_half, 1.0)
        ks_ref[...] = kblks[0].astype(BF16)
        kw_ref[...] = kblks[1].astype(BF16)
        values(y[:, LANES:], vw0_ref, vw1_ref)

    def cmp_inputs(y):
        kc_ref[...] = y[:, :LANES]
        vc_ref[...] = y[:, LANES:]

    def conv_lin(y):
        held["ca"] = y

    def conv_glu(y):
        hcv_ref[...] = held["ca"] * _sigmoid(y)

    def gates(y):
        sg = _sigmoid(y)
        for i in range(PROJ_TILE // LANES):
            gt_ref[i] = sg[i * LANES:(i + 1) * LANES, :].T[0:32, :]

    sections = [
        (wm_ref, C_Q, 256, functools.partial(queries, 0)),
        (wm_ref, C_Q + 256, 256, functools.partial(queries, 1)),
        (wc_ref, 0, CONV_CH, conv_lin),
        (wm_ref, C_SLC, 256, slc_kv),
        (wc_ref, CONV_CH, CONV_CH, conv_glu),
        (wm_ref, C_WIN, 256, win_kv),
        (wm_ref, C_CMP, 256, cmp_inputs),
        (wg_ref, 0, LANES, gates),
    ]

    def project(sec):
        w_ref, c0, width, _ = sec
        return _dot(xg, w_ref[:, c0:c0 + width])

    y_next = project(sections[0])
    for i, sec in enumerate(sections):
        y_cur = y_next
        if i + 1 < len(sections):
            y_next = project(sections[i + 1])
        sec[3](y_cur * rs)


def _paired_row_block(i, slab):
    per_head = HEAD_DIM // slab
    head = i // per_head
    src = ((head % NSA_KV_HEADS) * GQA_REP + head // NSA_KV_HEADS) * per_head + i % per_head
    return jnp.where(i < NSA_WIDTH // slab, src, i)


def _proj_call(x2, gin, w_in, cos, sin, gq, gk, bd, w_gate_up, w_down, w_out, seq):
    t = x2.shape[0]
    nt = t // PROJ_TILE
    slab = D_MODEL // nt
    assert D_MODEL % nt == 0 and slab % 16 == 0 and HEAD_DIM % slab == 0
    assert nt % 16 == 0 and (FFN_HIDDEN // 16) % 16 == 0
    wd_rows = FFN_HIDDEN // 16
    tiles_per_seq = seq // PROJ_TILE
    row = lambda i: (i, 0)
    const = lambda i: (0, 0)
    tab = lambda i: (i % tiles_per_seq, 0)
    blk3 = lambda i: (i, 0, 0)
    wd_blk = lambda i: (i * 16 // nt, 0)
    nb = PROJ_TILE // LANES
    return pl.pallas_call(
        _proj_body,
        grid=(nt,),
        in_specs=[
            pl.BlockSpec((PROJ_TILE, D_MODEL), row),
            pl.BlockSpec((1, D_MODEL), const),
            pl.BlockSpec(w_in.shape, const, pipeline_mode=pl.Buffered(1)),
            pl.BlockSpec((PROJ_TILE, LANES), tab),
            pl.BlockSpec((PROJ_TILE, LANES), tab),
            pl.BlockSpec((1, 256), const),
            pl.BlockSpec((1, 256), const),
            pl.BlockSpec((256, 256), const),
            pl.BlockSpec((slab, 2 * FFN_HIDDEN), row),
            pl.BlockSpec((wd_rows, D_MODEL), wd_blk),
            pl.BlockSpec((slab, D_MODEL), lambda i: (_paired_row_block(i, slab), 0)),
        ],
        out_specs=[
            pl.BlockSpec((PROJ_TILE, NSA_WIDTH), row),
            pl.BlockSpec((PROJ_TILE, LANES), row),
            pl.BlockSpec((PROJ_TILE, LANES), row),
            pl.BlockSpec((nb, LANES, LANES), blk3),
            pl.BlockSpec((nb, LANES, LANES), blk3),
            pl.BlockSpec((nb, LANES, LANES), blk3),
            pl.BlockSpec((nb, LANES, LANES), blk3),
            pl.BlockSpec((PROJ_TILE, LANES), row),
            pl.BlockSpec((PROJ_TILE, LANES), row),
            pl.BlockSpec((nb, 32, LANES), blk3),
            pl.BlockSpec((PROJ_TILE, CONV_CH), row),
            pl.BlockSpec((slab, 2 * FFN_HIDDEN), row),
            pl.BlockSpec((wd_rows, D_MODEL), wd_blk),
            pl.BlockSpec((slab, D_MODEL), row),
        ],
        out_shape=[
            jax.ShapeDtypeStruct((t, NSA_WIDTH), BF16),
            jax.ShapeDtypeStruct((t, LANES), BF16),
            jax.ShapeDtypeStruct((t, LANES), BF16),
            jax.ShapeDtypeStruct((t // LANES, LANES, LANES), BF16),
            jax.ShapeDtypeStruct((t // LANES, LANES, LANES), BF16),
            jax.ShapeDtypeStruct((t // LANES, LANES, LANES), BF16),
            jax.ShapeDtypeStruct((t // LANES, LANES, LANES), BF16),
            jax.ShapeDtypeStruct((t, LANES), F32),
            jax.ShapeDtypeStruct((t, LANES), F32),
            jax.ShapeDtypeStruct((t // LANES, 32, LANES), F32),
            jax.ShapeDtypeStruct((t, CONV_CH), F32),
            jax.ShapeDtypeStruct((D_MODEL, 2 * FFN_HIDDEN), BF16),
            jax.ShapeDtypeStruct((FFN_HIDDEN, D_MODEL), BF16),
            jax.ShapeDtypeStruct((D_MODEL, D_MODEL), BF16),
        ],
        scratch_shapes=[pltpu.VMEM((D_MODEL, MAIN_COLS), BF16),
                        pltpu.VMEM((D_MODEL, 2 * CONV_CH), BF16),
                        pltpu.VMEM((D_MODEL, LANES), BF16)],
        compiler_params=pltpu.CompilerParams(
            dimension_semantics=("arbitrary",), vmem_limit_bytes=VMEM_LIMIT),
        name="proj",
    )(x2, gin, w_in, cos, sin, gq, gk, bd, w_gate_up, w_down, w_out)


def _cmp_mlp(t_ref, pe_ref, w1_ref, w2_ref):
    half = CMP_LEN // 2
    nchunk = t_ref.shape[0] // CMP_STRIDE
    toks = [t_ref[pl.ds(l, nchunk, stride=CMP_STRIDE), :] for l in range(half)]
    xp = _lane_cat([(toks[l] + pe_ref[l:l + 1, :]).astype(BF16) for l in range(half)])
    xq = _lane_cat([(toks[l] + pe_ref[half + l:half + l + 1, :]).astype(BF16) for l in range(half)])
    h1 = _dot(xp, w1_ref[0]) + pltpu.roll(_dot(xq, w1_ref[1]), nchunk - 1, 0)
    act = (h1 * _sigmoid(h1)).astype(BF16)
    return _dot(act, w2_ref[...])


def _cmp_body(kc_ref, vc_ref, pek_ref, w1k_ref, w2k_ref, pev_ref, w1v_ref, w2v_ref,
              cos_ref, sin_ref, gk_ref, bd_ref, ko_ref, vt0_ref, vt1_ref):
    kraw = _cmp_mlp(kc_ref, pek_ref, w1k_ref, w2k_ref)
    lane = lax.broadcasted_iota(jnp.int32, kraw.shape, 1)
    first_half = (lane & (HEAD_DIM // 2)) == 0
    (kr,) = _head_norm_rope(kraw, gk_ref[...], bd_ref[...], cos_ref[...], sin_ref[...],
                            first_half, 1.0)
    ko_ref[...] = kr.astype(BF16)
    vraw = _cmp_mlp(vc_ref, pev_ref, w1v_ref, w2v_ref)
    for i in range(CMP_SEQS):
        vt0_ref[i], vt1_ref[i] = _value_t_with_ones(vraw[i * LANES:(i + 1) * LANES, :])


def _cmp_call(kc, vc, pek, w1k, w2k, pev, w1v, w2v, cos, sin, gk, bd, batch, seq):
    seqblk = lambda b: (b, 0)
    c2 = lambda b: (0, 0)
    c3 = lambda b: (0, 0, 0)
    ncmp_pad = seq // CMP_STRIDE
    assert ncmp_pad == LANES and batch % CMP_SEQS == 0
    rows = CMP_SEQS * ncmp_pad
    half_k = (CMP_LEN // 2) * LANES
    return pl.pallas_call(
        _cmp_body,
        grid=(batch // CMP_SEQS,),
        in_specs=[
            pl.BlockSpec((CMP_SEQS * seq, LANES), seqblk),
            pl.BlockSpec((CMP_SEQS * seq, LANES), seqblk),
            pl.BlockSpec((CMP_LEN, LANES), c2),
            pl.BlockSpec((2, half_k, 2 * CMP_HIDDEN), c3),
            pl.BlockSpec((2 * CMP_HIDDEN, LANES), c2),
            pl.BlockSpec((CMP_LEN, LANES), c2),
            pl.BlockSpec((2, half_k, 2 * CMP_HIDDEN), c3),
            pl.BlockSpec((2 * CMP_HIDDEN, LANES), c2),
            pl.BlockSpec((rows, LANES), c2),
            pl.BlockSpec((rows, LANES), c2),
            pl.BlockSpec((1, LANES), c2),
            pl.BlockSpec((LANES, LANES), c2),
        ],
        out_specs=[
            pl.BlockSpec((rows, LANES), seqblk),
            pl.BlockSpec((CMP_SEQS, LANES, ncmp_pad), lambda b: (b, 0, 0)),
            pl.BlockSpec((CMP_SEQS, LANES, ncmp_pad), lambda b: (b, 0, 0)),
        ],
        out_shape=[
            jax.ShapeDtypeStruct((batch * ncmp_pad, LANES), BF16),
            jax.ShapeDtypeStruct((batch, LANES, ncmp_pad), BF16),
            jax.ShapeDtypeStruct((batch, LANES, ncmp_pad), BF16),
        ],
        compiler_params=pltpu.CompilerParams(
            dimension_semantics=("parallel",), vmem_limit_bytes=VMEM_LIMIT),
        name="compress",
    )(kc, vc, pek, w1k, w2k, pev, w1v, w2v, cos, sin, gk, bd)


def _lane_cat(blocks):
    return jnp.concatenate(blocks, axis=1)


def _softmax_numer(s, bias, m_run=None):
    es, ms = [], []
    for r in range(GQA_REP):
        sm = s[:, r * LANES:(r + 1) * LANES] + bias
        m = jnp.max(sm, axis=0, keepdims=True)
        if m_run is not None:
            m = jnp.maximum(m, m_run[:, r * LANES:(r + 1) * LANES])
        es.append(jnp.exp2(sm - m))
        ms.append(m)
    return _lane_cat(es), _lane_cat(ms)


def _safe_inv(l):
    return 1.0 / jnp.where(l > 0.0, l, 1.0)


def _attn_body(*refs):
    first = pl.program_id(0) * TILES_PER_STEP
    n_max = refs[1].shape[0] // SLC_CHUNK
    for n in range(1, n_max + 1):
        pl.when((first * Q_TILE) // SLC_CHUNK + 1 == n)(functools.partial(_attn_step, n, first, *refs))


def _attn_step(n_chunks, first, *refs):
    def one_tile(slot, carry):
        tile = _tile_program(n_chunks, first + slot, slot, *refs)
        pending = None
        s_next = tile.scores(tile.units[0])
        for i, unit in enumerate(tile.units):
            s_cur = s_next
            if i + 1 < len(tile.units):
                s_next = tile.scores(tile.units[i + 1])
            e, alpha = tile.softmax(unit, s_cur)
            if pending is not None:
                tile.weighted_values(*pending)
            pending = (unit, e, alpha)
        tile.weighted_values(*pending)
        tile.finish()
        return carry

    lax.fori_loop(0, TILES_PER_STEP, one_tile, 0)


def _tile_program(n_chunks, c, slot, q_ref, ks_ref, kw_ref, vs0_ref, vs1_ref, vw0_ref, vw1_ref, kc_ref,
                  vc0_ref, vc1_ref, gt_ref, ovl_ref, gn_ref, out_ref):
    groups = range(NSA_KV_HEADS)
    q_rows = pl.ds(pl.multiple_of(slot * Q_TILE, Q_TILE), Q_TILE)
    vs_refs, vw_refs, vc_refs = (vs0_ref, vs1_ref), (vw0_ref, vw1_ref), (vc0_ref, vc1_ref)
    sum_row = (HEAD_DIM, 0)
    width = GQA_REP * LANES
    t_lane = c * Q_TILE + lax.broadcasted_iota(jnp.int32, (1, LANES), 1)
    lane_sq = lax.broadcasted_iota(jnp.int32, (Q_TILE, LANES), 1)
    n_live = n_chunks * (SLC_CHUNK // SLC_BLOCK)
    nsel = 32
    win_keys = min(WIN_KEYS, n_chunks * SLC_CHUNK)
    j_idx = lax.broadcasted_iota(jnp.int32, (nsel, LANES), 0)
    cur = lax.shift_right_logical(t_lane, 6)
    n_idx = lax.broadcasted_iota(jnp.int32, (LANES, LANES), 0)
    cbias = jnp.where((n_idx * CMP_STRIDE + (CMP_LEN - 1) <= t_lane) & (n_idx < LANES - 1), 0.0, NEG)
    has_cmp = jnp.where(t_lane >= CMP_LEN - 1, 1.0, 0.0)
    has_cmp = _lane_cat([has_cmp] * GQA_REP)
    win_blk = jnp.maximum(c - WINDOW // Q_TILE, 0)
    win_start = pl.multiple_of(win_blk * Q_TILE, Q_TILE)
    kpos_w = win_start + lax.broadcasted_iota(jnp.int32, (win_keys, LANES), 0)
    wbias = jnp.where((kpos_w <= t_lane) & (kpos_w > t_lane - WINDOW), 0.0, NEG)
    row_sl = lax.broadcasted_iota(jnp.int32, (SLC_CHUNK, LANES), 0)
    zero_q = jnp.zeros((Q_TILE, LANES), BF16)

    def gate_row(g, branch):
        rows = [(g * GQA_REP + r) * 3 + branch for r in range(GQA_REP)]
        return _lane_cat([gt_ref[slot, i:i + 1, :] for i in rows])

    qg = []
    for g in groups:
        in_g = (lane_sq >= HEAD_DIM) if g else (lane_sq < HEAD_DIM)
        qg.append(jnp.concatenate(
            [jnp.where(in_g, q_ref[q_rows, r * LANES:(r + 1) * LANES], zero_q) for r in range(GQA_REP)],
            axis=0))

    o_cmp = [None] * NSA_KV_HEADS
    sel_bias = [None] * NSA_KV_HEADS

    def compressed_softmax(g, sc):
        ec, _ = _softmax_numer(sc, cbias)
        coef = has_cmp * _safe_inv(jnp.sum(ec, axis=0, keepdims=True))
        pc = ec * coef
        psum = (pc[:, 0:LANES] + pc[:, LANES:2 * LANES]
                + pc[:, 2 * LANES:3 * LANES] + pc[:, 3 * LANES:4 * LANES])
        p_hi, p_lo = _split_bf16(psum)
        imp = (_dot(ovl_ref[...], p_hi) + _dot(ovl_ref[...], p_lo))[0:nsel, :]
        valid = j_idx <= cur
        forced = (j_idx == 0) | (j_idx == cur) | (j_idx == cur - 1)
        score = jnp.where(valid, imp + jnp.where(forced, FORCE, 0.0), -FORCE)
        rank = jnp.zeros((nsel, LANES), F32)
        for jp in range(nsel):
            row = score[jp:jp + 1, :]
            beats = (row > score) | ((row == score) & (j_idx > jp))
            rank = rank + jnp.where(beats, 1.0, 0.0)
        sel_bias[g] = jnp.where((rank < float(SLC_TOPN)) & (score > -1.0), 0.0, NEG)
        return ec.astype(BF16), coef

    nsub = SLC_CHUNK // LANES
    blocks_per_chunk = SLC_CHUNK // SLC_BLOCK
    kwin = kw_ref[pl.ds(win_start, win_keys), :]
    units = [("cmp", g, 0) for g in groups] + [("win", g, 0) for g in groups]
    units += [("slc", g, kc) for kc in range(n_chunks) for g in groups]
    m_run = [None] * NSA_KV_HEADS
    acc_s = [None] * NSA_KV_HEADS
    acc_w = [None] * NSA_KV_HEADS

    def scores(unit):
        kind, g, kc = unit
        if kind == "cmp":
            keys = kc_ref[...]
        elif kind == "win":
            keys = kwin
        else:
            keys = ks_ref[kc * SLC_CHUNK:(kc + 1) * SLC_CHUNK, :]
        return _dot_tb(keys, qg[g])

    def softmax(unit, s):
        kind, g, kc = unit
        if kind == "cmp":
            return compressed_softmax(g, s)
        if kind == "win":
            e, _ = _softmax_numer(s, wbias)
            return e.astype(BF16), None
        j0 = kc * blocks_per_chunk
        bias = jnp.concatenate(
            [jnp.broadcast_to(sel_bias[g][j:j + 1, :], (SLC_BLOCK, LANES))
             for j in range(j0, j0 + blocks_per_chunk)], axis=0)
        if kc == n_chunks - 1:
            bias = jnp.where(kc * SLC_CHUNK + row_sl <= t_lane, bias, NEG)
        e, m_new = _softmax_numer(s, bias, m_run[g])
        alpha = None if kc == 0 else jnp.exp2(m_run[g] - m_new)
        m_run[g] = m_new
        return e.astype(BF16), alpha

    def weighted_values(unit, e, alpha):
        kind, g, kc = unit
        if kind == "cmp":
            o_cmp[g] = _dot(vc_refs[g][0], e) * (gate_row(g, 0) * alpha)
            return
        if kind == "win":
            vt = _lane_cat([vw_refs[g][win_blk + i] for i in range(win_keys // LANES)])
            acc_w[g] = _dot(vt, e)
            return
        vt = _lane_cat([vs_refs[g][nsub * kc + i] for i in range(nsub)])
        pv = _dot(vt, e)
        acc_s[g] = pv if alpha is None else acc_s[g] * alpha + pv

    def finish():
        o_groups = []
        for g in groups:
            r0 = sum_row[g]
            o_groups.append(o_cmp[g]
                            + acc_s[g] * (gate_row(g, 1) * _safe_inv(acc_s[g][r0:r0 + 1, :]))
                            + acc_w[g] * (gate_row(g, 2) * _safe_inv(acc_w[g][r0:r0 + 1, :])))
        row_d = lax.broadcasted_iota(jnp.int32, (LANES, width), 0)
        o_t = jnp.where(row_d < HEAD_DIM, o_groups[0], o_groups[1])
        sq = jnp.sum(o_t * o_t, axis=0, keepdims=True)
        ss = (sq[:, 0:LANES] + sq[:, LANES:2 * LANES]
              + sq[:, 2 * LANES:3 * LANES] + sq[:, 3 * LANES:4 * LANES])
        rs = lax.rsqrt(ss * (1.0 / NSA_WIDTH) + EPS)
        for r in range(GQA_REP):
            cols = slice(r * LANES, (r + 1) * LANES)
            out_ref[q_rows, cols] = (o_t[:, cols] * rs * gn_ref[:, cols]).T.astype(BF16)

    return types.SimpleNamespace(units=units, scores=scores, softmax=softmax,
                                 weighted_values=weighted_values, finish=finish)


def _attn_call(q, ks, kw, vs0, vs1, vw0, vw1, kcmp, vc0, vc1, gt, ovl, gn, batch, seq):
    assert SLC_CHUNK % (TILES_PER_STEP * Q_TILE) == 0
    rows = TILES_PER_STEP * Q_TILE
    nq = seq // rows
    nkb = seq // LANES
    qrow = lambda c, b: (b * nq + c, 0)
    per_b = lambda c, b: (b, 0)
    per_b3 = lambda c, b: (b, 0, 0)
    vspec = pl.BlockSpec((nkb, LANES, LANES), per_b3)
    cspec = pl.BlockSpec((1, LANES, LANES), per_b3)
    return pl.pallas_call(
        _attn_body,
        grid=(nq, batch),
        in_specs=[
            pl.BlockSpec((rows, NSA_WIDTH), qrow),
            pl.BlockSpec((seq, LANES), per_b),
            pl.BlockSpec((seq, LANES), per_b),
            vspec, vspec, vspec, vspec,
            pl.BlockSpec((LANES, LANES), per_b),
            cspec, cspec,
            pl.BlockSpec((TILES_PER_STEP, 32, LANES), lambda c, b: (b * nq + c, 0, 0)),
            pl.BlockSpec((LANES, LANES), lambda c, b: (0, 0)),
            pl.BlockSpec((LANES, NSA_WIDTH), lambda c, b: (0, 0)),
        ],
        out_specs=pl.BlockSpec((rows, NSA_WIDTH), qrow),
        out_shape=jax.ShapeDtypeStruct((batch * seq, NSA_WIDTH), BF16),
        compiler_params=pltpu.CompilerParams(
            dimension_semantics=("parallel", "arbitrary"), vmem_limit_bytes=VMEM_LIMIT),
        name="attn",
    )(q, ks, kw, vs0, vs1, vw0, vw1, kcmp, vc0, vc1, gt, ovl, gn)


def _conv_body(tiles_per_seq, cur_ref, halo_ref, w_ref, b_ref, lng_ref, lnb_ref, gn_ref, out_ref,
               buf_ref, sh_ref, cv_ref):
    first = (pl.program_id(0) % tiles_per_seq) == 0
    buf_ref[0:CONV_HALO, :] = jnp.where(first, 0.0, halo_ref[...])
    buf_ref[CONV_HALO:, :] = cur_ref[...]
    lead = CONV_HALO - (CONV_WIDTH - 1)
    ncb = CONV_CH // LANES
    nrb = CONV_TILE // CONV_ROWS
    for s in range(8):
        span = CONV_TILE + 8 * ((CONV_WIDTH - 1 - s) // 8)
        for cb in range(ncb):
            sh_ref[s, cb, 0:span, :] = buf_ref[lead + s:lead + s + span, cb * LANES:(cb + 1) * LANES]

    def conv_block(i, carry):
        cb = i % ncb
        base = pl.multiple_of((i // ncb) * CONV_ROWS, CONV_ROWS)
        acc = jnp.zeros((CONV_ROWS, LANES), F32) + b_ref[cb]
        for k in range(CONV_WIDTH):
            acc = acc + w_ref[cb, k:k + 1, :] * sh_ref[k % 8, cb, pl.ds(base + 8 * (k // 8), CONV_ROWS), :]
        cv_ref[cb, pl.ds(base, CONV_ROWS), :] = acc
        return carry

    lax.fori_loop(0, nrb * ncb, conv_block, 0)
    h = _lane_cat([cv_ref[cb] for cb in range(ncb)])
    mu = jnp.mean(h, axis=-1, keepdims=True)
    d = h - mu
    var = jnp.mean(d * d, axis=-1, keepdims=True)
    hn = d * lax.rsqrt(var + EPS) * lng_ref[...] + lnb_ref[...]
    o = hn * _sigmoid(hn)
    ms = jnp.mean(o * o, axis=-1, keepdims=True)
    out_ref[...] = (o * lax.rsqrt(ms + EPS) * gn_ref[...]).astype(BF16)


def _conv_call(hcv, w, b, lng, lnb, gn, seq):
    t = hcv.shape[0]
    nt = t // CONV_TILE
    halo_per_tile = CONV_TILE // CONV_HALO
    ncb = CONV_CH // LANES
    row = lambda i: (i, 0)
    const = lambda i: (0, 0)
    return pl.pallas_call(
        functools.partial(_conv_body, seq // CONV_TILE),
        grid=(nt,),
        in_specs=[
            pl.BlockSpec((CONV_TILE, CONV_CH), row),
            pl.BlockSpec((CONV_HALO, CONV_CH), lambda i: (jnp.maximum(i * halo_per_tile - 1, 0), 0)),
            pl.BlockSpec((ncb, CONV_HALO, LANES), lambda i: (0, 0, 0)),
            pl.BlockSpec((ncb, 1, LANES), lambda i: (0, 0, 0)),
            pl.BlockSpec((1, CONV_CH), const),
            pl.BlockSpec((1, CONV_CH), const),
            pl.BlockSpec((1, CONV_CH), const),
        ],
        out_specs=pl.BlockSpec((CONV_TILE, CONV_CH), row),
        out_shape=jax.ShapeDtypeStruct((t, CONV_CH), BF16),
        scratch_shapes=[pltpu.VMEM((CONV_HALO + CONV_TILE, CONV_CH), F32),
                        pltpu.VMEM((8, ncb, CONV_TILE + 8 * ((CONV_WIDTH - 1) // 8), LANES), F32),
                        pltpu.VMEM((ncb, CONV_TILE, LANES), F32)],
        compiler_params=pltpu.CompilerParams(
            dimension_semantics=("parallel",), vmem_limit_bytes=VMEM_LIMIT),
        name="conv",
    )(hcv, hcv, w, b, lng, lnb, gn)


def _ffn_body(x_ref, ma_ref, mb_ref, wo_ref, gf_ref, wgu_ref, wd_ref, out_ref):
    h = (x_ref[...] + _dot(ma_ref[...], wo_ref[0:NSA_WIDTH, :])
         + _dot(mb_ref[...], wo_ref[NSA_WIDTH:D_MODEL, :]))
    ms = jnp.mean(h * h, axis=-1, keepdims=True)
    hn = (h * lax.rsqrt(ms + EPS) * gf_ref[...]).astype(BF16)
    acc = jnp.zeros_like(h)
    for j in range(FFN_HIDDEN // FFN_CHUNK):
        c0 = j * FFN_CHUNK
        gte = _dot(hn, wgu_ref[:, c0:c0 + FFN_CHUNK])
        up = _dot(hn, wgu_ref[:, FFN_HIDDEN + c0:FFN_HIDDEN + c0 + FFN_CHUNK])
        act = (gte * _sigmoid(gte) * up).astype(BF16)
        acc = acc + _dot(act, wd_ref[c0:c0 + FFN_CHUNK, :])
    out_ref[...] = h + acc


def _ffn_call(x2, ma, mb, wo, gf, wgu, wd):
    t = x2.shape[0]
    nt = t // FFN_TILE
    row = lambda i: (i, 0)
    const = lambda i: (0, 0)
    once = pl.Buffered(1)
    return pl.pallas_call(
        _ffn_body,
        grid=(nt,),
        in_specs=[
            pl.BlockSpec((FFN_TILE, D_MODEL), row),
            pl.BlockSpec((FFN_TILE, NSA_WIDTH), row),
            pl.BlockSpec((FFN_TILE, CONV_CH), row),
            pl.BlockSpec((D_MODEL, D_MODEL), const, pipeline_mode=once),
            pl.BlockSpec((1, D_MODEL), const),
            pl.BlockSpec((D_MODEL, 2 * FFN_HIDDEN), const, pipeline_mode=once),
            pl.BlockSpec((FFN_HIDDEN, D_MODEL), const, pipeline_mode=once),
        ],
        out_specs=pl.BlockSpec((FFN_TILE, D_MODEL), row),
        out_shape=jax.ShapeDtypeStruct((t, D_MODEL), F32),
        compiler_params=pltpu.CompilerParams(
            dimension_semantics=("parallel",), vmem_limit_bytes=VMEM_LIMIT),
        name="ffn",
    )(x2, ma, mb, wo, gf, wgu, wd)


def _pair_heads(a, axis):
    shape = a.shape
    a = a.reshape(shape[:axis] + (NSA_KV_HEADS, GQA_REP, HEAD_DIM) + shape[axis + 1:])
    return jnp.swapaxes(a, axis, axis + 1).reshape(shape)


def _transposed_gain(g):
    cols = g.reshape(GQA_REP, LANES).T
    return jnp.broadcast_to(cols[:, :, None], (LANES, GQA_REP, Q_TILE)).reshape(LANES, GQA_REP * Q_TILE)


def _rope_tables(pos):
    half = HEAD_DIM // 2
    inv = ROPE_THETA ** (-np.arange(half, dtype=np.float64) / half)
    ang = np.asarray(pos, np.float64)[:, None] * inv[None, :]
    cos = np.concatenate([np.cos(ang), np.cos(ang)], axis=1)
    sin = np.concatenate([-np.sin(ang), np.sin(ang)], axis=1)
    return (np.tile(cos, (1, NSA_KV_HEADS)).astype(np.float32),
            np.tile(sin, (1, NSA_KV_HEADS)).astype(np.float32))


def _block_diag_ones(width):
    idx = np.arange(width) // HEAD_DIM
    return (idx[:, None] == idx[None, :]).astype(np.float32)


def _overlap_t(seq):
    ncmp = (seq - CMP_LEN) // CMP_STRIDE + 1
    cs = np.arange(ncmp)[:, None] * CMP_STRIDE
    ss = np.arange(seq // SLC_BLOCK)[None, :] * SLC_BLOCK
    ov = np.clip(np.minimum(cs + CMP_LEN, ss + SLC_BLOCK) - np.maximum(cs, ss), 0, None) / CMP_LEN
    out = np.zeros((LANES, LANES), np.float32)
    out[:seq // SLC_BLOCK, :ncmp] = ov.T
    return out


def _cmp_weights(pe, w1, w2):
    pe2 = jnp.concatenate([pe, pe], axis=1)
    w1r = w1.reshape(CMP_LEN, HEAD_DIM, CMP_HIDDEN)
    z1 = jnp.zeros_like(w1r)
    w1b = jnp.concatenate([jnp.concatenate([w1r, z1], axis=2),
                           jnp.concatenate([z1, w1r], axis=2)], axis=1).astype(BF16)
    z2 = jnp.zeros_like(w2)
    w2b = jnp.concatenate([jnp.concatenate([w2, z2], axis=1),
                           jnp.concatenate([z2, w2], axis=1)], axis=0).astype(BF16)
    w1b = w1b.reshape(2, (CMP_LEN // 2) * LANES, 2 * CMP_HIDDEN)
    return pe2, w1b, w2b


def kernel(x, attn_norm_g, w_in, q_norm_g, k_norm_cmp_g, k_norm_slc_g, k_norm_win_g, cmp_pe_k, cmp_w1_k, cmp_w2_k, cmp_pe_v, cmp_w1_v, cmp_w2_v, conv_dw_w, conv_dw_b, conv_ln_g, conv_ln_b, out_norm_nsa_g, out_norm_conv_g, w_out, ffn_norm_g, w_gate_up, w_down):
    batch, seq, d_model = x.shape
    assert d_model == D_MODEL and seq % PROJ_TILE == 0 and seq // SLC_BLOCK == 32
    depth = w_in.shape[0]
    cos_np, sin_np = _rope_tables(np.arange(seq))
    ccos_np, csin_np = _rope_tables(np.arange(seq // CMP_STRIDE) * CMP_STRIDE + CMP_LEN - 1)
    ccos_np, csin_np = np.tile(ccos_np, (CMP_SEQS, 1)), np.tile(csin_np, (CMP_SEQS, 1))
    cos, sin, ccos, csin = map(jnp.asarray, (cos_np, sin_np, ccos_np, csin_np))
    bd256 = jnp.asarray(_block_diag_ones(256), BF16)
    bd128 = jnp.asarray(_block_diag_ones(LANES), BF16)
    ovl = jnp.asarray(_overlap_t(seq), BF16)

    x2 = x.reshape(batch * seq, d_model)
    for l in range(depth):
        w = w_in[l]
        gq = jnp.tile(q_norm_g[l], 4)[None, :]
        gk = jnp.concatenate([jnp.tile(k_norm_slc_g[l], 2), jnp.tile(k_norm_win_g[l], 2)])[None, :]
        q, ks, kw, vs0, vs1, vw0, vw1, kc, vc, gt, hcv, wgu_b, wd_b, wo_b = _proj_call(
            x2, attn_norm_g[l][None, :], w.T, cos, sin, gq, gk, bd256,
            w_gate_up[l], w_down[l], w_out[l], seq)

        pek, w1k, w2k = _cmp_weights(cmp_pe_k[l], cmp_w1_k[l], cmp_w2_k[l])
        pev, w1v, w2v = _cmp_weights(cmp_pe_v[l], cmp_w1_v[l], cmp_w2_v[l])
        kcmp, vc0, vc1 = _cmp_call(kc, vc, pek, w1k, w2k, pev, w1v, w2v, ccos, csin,
                                jnp.tile(k_norm_cmp_g[l], 2)[None, :], bd128, batch, seq)

        mix_a = _attn_call(q, ks, kw, vs0, vs1, vw0, vw1, kcmp, vc0, vc1, gt, ovl,
                           _transposed_gain(_pair_heads(out_norm_nsa_g[l], 0)), batch, seq)

        w_dw = jnp.concatenate([conv_dw_w[l][:, 0, :],
                                jnp.zeros((CONV_HALO - CONV_WIDTH, CONV_CH), F32)], axis=0)
        w_dw = w_dw.reshape(CONV_HALO, CONV_CH // LANES, LANES).transpose(1, 0, 2)
        mix_b = _conv_call(hcv, w_dw, conv_dw_b[l].reshape(CONV_CH // LANES, 1, LANES),
                           conv_ln_g[l][None, :],
                           conv_ln_b[l][None, :], out_norm_conv_g[l][None, :], seq)

        x2 = _ffn_call(x2, mix_a, mix_b, wo_b, ffn_norm_g[l][None, :], wgu_b, wd_b)
    return x2.reshape(batch, seq, d_model)
```

```python
import functools
import types

import numpy as np
import jax
import jax.numpy as jnp
from jax import lax
from jax.experimental import pallas as pl
from jax.experimental.pallas import tpu as pltpu

F32 = jnp.float32
BF16 = jnp.bfloat16

D_MODEL = 1024
HEAD_DIM = 64
NSA_HEADS = 8
NSA_KV_HEADS = 2
GQA_REP = NSA_HEADS // NSA_KV_HEADS
NSA_WIDTH = NSA_HEADS * HEAD_DIM
CONV_CH = D_MODEL - NSA_WIDTH
CMP_LEN = 32
CMP_STRIDE = 16
CMP_HIDDEN = 4 * HEAD_DIM
SLC_BLOCK = 64
SLC_TOPN = 8
WINDOW = 512
CONV_WIDTH = 31
FFN_HIDDEN = 2816
ROPE_THETA = 10000.0
EPS = 1e-6
NEG = -1e30
LOG2E = 1.4426950408889634
FORCE = 1e6

LANES = 128
Q_TILE = 128
TILES_PER_STEP = 2
SLC_CHUNK = 256
WIN_KEYS = WINDOW + Q_TILE
PROJ_TILE = 1024
CMP_SEQS = 4
CONV_TILE = 512
CONV_HALO = 32
CONV_ROWS = 256
FFN_TILE = 1024
FFN_CHUNK = 256
VMEM_LIMIT = 56 * 1024 * 1024

C_Q = 0
C_CMP = 512
C_SLC = 768
C_WIN = 1024
MAIN_COLS = 1280
GATE_COLS = 3 * NSA_HEADS

_TRANS_B = (((1,), (1,)), ((), ()))


def _dot(a, b):
    return jnp.dot(a, b, preferred_element_type=F32)


def _dot_tb(a, b):
    return lax.dot_general(a, b, _TRANS_B, preferred_element_type=F32)


def _split_bf16(x):
    hi = x.astype(BF16)
    lo = (x - hi.astype(F32)).astype(BF16)
    return hi, lo


def _sigmoid(x):
    return 1.0 / (1.0 + jnp.exp(-x))


def _head_norm_rope(y, gain, bd, cos, sin, first_half, scale):
    ss = _dot((y * y).astype(BF16), bd)
    yn = y * lax.rsqrt(ss * (1.0 / HEAD_DIM) + EPS) * gain
    outs = []
    for j in range(y.shape[1] // LANES):
        blk = yn[:, j * LANES:(j + 1) * LANES]
        rot = jnp.where(first_half, pltpu.roll(blk, LANES - HEAD_DIM // 2, 1),
                        pltpu.roll(blk, HEAD_DIM // 2, 1))
        outs.append((blk * cos + rot * sin) * scale)
    return outs


def _value_t_with_ones(v):
    vt = v.T
    row = lax.broadcasted_iota(jnp.int32, vt.shape, 0)
    return (jnp.where(row < HEAD_DIM, vt, 1.0).astype(BF16),
            jnp.where(row < HEAD_DIM, 1.0, vt).astype(BF16))


def _proj_body(x_ref, gin_ref, w_ref, cos_ref, sin_ref, gq_ref, gk_ref, bd_ref,
               wgu_in_ref, wd_in_ref, wo_in_ref,
               q_ref, ks_ref, kw_ref, vs0_ref, vs1_ref, vw0_ref, vw1_ref, kc_ref, vc_ref, gt_ref,
               hcv_ref, wgu_out_ref, wd_out_ref, wo_out_ref, wm_ref, wc_ref, wg_ref):
    @pl.when(pl.program_id(0) == 0)
    def _():
        for j in range(MAIN_COLS // LANES):
            wm_ref[:, j * LANES:(j + 1) * LANES] = w_ref[j * LANES:(j + 1) * LANES, :].T.astype(BF16)
        gl = jnp.concatenate([w_ref[MAIN_COLS:MAIN_COLS + GATE_COLS, :],
                              jnp.zeros((LANES - GATE_COLS, D_MODEL), F32)], axis=0)
        wg_ref[...] = gl.T.astype(BF16)
        for j in range(2 * CONV_CH // LANES):
            r0 = MAIN_COLS + GATE_COLS + j * LANES
            wc_ref[:, j * LANES:(j + 1) * LANES] = w_ref[r0:r0 + LANES, :].T.astype(BF16)

    wgu_out_ref[...] = wgu_in_ref[...].astype(BF16)
    wd_out_ref[...] = wd_in_ref[...].astype(BF16)
    wo_out_ref[...] = wo_in_ref[...].astype(BF16)

    x = x_ref[...]
    xg = (x * gin_ref[...]).astype(BF16)
    rs = lax.rsqrt(jnp.mean(x * x, axis=-1, keepdims=True) + EPS)
    cos = cos_ref[...]
    sin = sin_ref[...]
    bd = bd_ref[...]
    lane = lax.broadcasted_iota(jnp.int32, (PROJ_TILE, LANES), 1)
    first_half = (lane & (HEAD_DIM // 2)) == 0
    held = {}

    def queries(group, y):
        held["q", group] = _head_norm_rope(y, gq_ref[...], bd, cos, sin, first_half,
                                           HEAD_DIM ** -0.5 * LOG2E)
        if group == 0:
            return
        low = lane < HEAD_DIM
        for j in range(2):
            a, b = held["q", 0][j], held["q", 1][j]
            q_ref[:, (2 * j) * LANES:(2 * j + 1) * LANES] = jnp.where(
                low, a, pltpu.roll(b, HEAD_DIM, 1)).astype(BF16)
            q_ref[:, (2 * j + 1) * LANES:(2 * j + 2) * LANES] = jnp.where(
                low, pltpu.roll(a, HEAD_DIM, 1), b).astype(BF16)

    def values(y, out0_ref, out1_ref):
        for i in range(PROJ_TILE // LANES):
            out0_ref[i], out1_ref[i] = _value_t_with_ones(y[i * LANES:(i + 1) * LANES, :])

    def slc_kv(y):
        held["ks"] = y[:, :LANES]
        values(y[:, LANES:], vs0_ref, vs1_ref)

    def win_kv(y):
        kblks = _head_norm_rope(_lane_cat([held["ks"], y[:, :LANES]]), gk_ref[...], bd, cos, sin,
                                first_half, 1.0)
        ks_ref[...] = kblks[0].astype(BF16)
        kw_ref[...] = kblks[1].astype(BF16)
        values(y[:, LANES:], vw0_ref, vw1_ref)

    def cmp_inputs(y):
        kc_ref[...] = y[:, :LANES]
        vc_ref[...] = y[:, LANES:]

    def conv_lin(y):
        held["ca"] = y

    def conv_glu(y):
        hcv_ref[...] = held["ca"] * _sigmoid(y)

    def gates(y):
        sg = _sigmoid(y)
        for i in range(PROJ_TILE // LANES):
            gt_ref[i] = sg[i * LANES:(i + 1) * LANES, :].T[0:32, :]

    sections = [
        (wm_ref, C_Q, 256, functools.partial(queries, 0)),
        (wm_ref, C_Q + 256, 256, functools.partial(queries, 1)),
        (wc_ref, 0, CONV_CH, conv_lin),
        (wm_ref, C_SLC, 256, slc_kv),
        (wc_ref, CONV_CH, CONV_CH, conv_glu),
        (wm_ref, C_WIN, 256, win_kv),
        (wm_ref, C_CMP, 256, cmp_inputs),
        (wg_ref, 0, LANES, gates),
    ]

    def project(sec):
        w_ref, c0, width, _ = sec
        return _dot(xg, w_ref[:, c0:c0 + width])

    y_next = project(sections[0])
    for i, sec in enumerate(sections):
        y_cur = y_next
        if i + 1 < len(sections):
            y_next = project(sections[i + 1])
        sec[3](y_cur * rs)


def _paired_row_block(i, slab):
    per_head = HEAD_DIM // slab
    head = i // per_head
    src = ((head % NSA_KV_HEADS) * GQA_REP + head // NSA_KV_HEADS) * per_head + i % per_head
    return jnp.where(i < NSA_WIDTH // slab, src, i)


def _proj_call(x2, gin, w_in, cos, sin, gq, gk, bd, w_gate_up, w_down, w_out, seq):
    t = x2.shape[0]
    nt = t // PROJ_TILE
    slab = D_MODEL // nt
    assert D_MODEL % nt == 0 and slab % 16 == 0 and HEAD_DIM % slab == 0
    assert nt % 16 == 0 and (FFN_HIDDEN // 16) % 16 == 0
    wd_rows = FFN_HIDDEN // 16
    tiles_per_seq = seq // PROJ_TILE
    row = lambda i: (i, 0)
    const = lambda i: (0, 0)
    tab = lambda i: (i % tiles_per_seq, 0)
    blk3 = lambda i: (i, 0, 0)
    wd_blk = lambda i: (i * 16 // nt, 0)
    nb = PROJ_TILE // LANES
    return pl.pallas_call(
        _proj_body,
        grid=(nt,),
        in_specs=[
            pl.BlockSpec((PROJ_TILE, D_MODEL), row),
            pl.BlockSpec((1, D_MODEL), const),
            pl.BlockSpec(w_in.shape, const, pipeline_mode=pl.Buffered(1)),
            pl.BlockSpec((PROJ_TILE, LANES), tab),
            pl.BlockSpec((PROJ_TILE, LANES), tab),
            pl.BlockSpec((1, 256), const),
            pl.BlockSpec((1, 256), const),
            pl.BlockSpec((256, 256), const),
            pl.BlockSpec((slab, 2 * FFN_HIDDEN), row),
            pl.BlockSpec((wd_rows, D_MODEL), wd_blk),
            pl.BlockSpec((slab, D_MODEL), lambda i: (_paired_row_block(i, slab), 0)),
        ],
        out_specs=[
            pl.BlockSpec((PROJ_TILE, NSA_WIDTH), row),
            pl.BlockSpec((PROJ_TILE, LANES), row),
            pl.BlockSpec((PROJ_TILE, LANES), row),
            pl.BlockSpec((nb, LANES, LANES), blk3),
            pl.BlockSpec((nb, LANES, LANES), blk3),
            pl.BlockSpec((nb, LANES, LANES), blk3),
            pl.BlockSpec((nb, LANES, LANES), blk3),
            pl.BlockSpec((PROJ_TILE, LANES), row),
            pl.BlockSpec((PROJ_TILE, LANES), row),
            pl.BlockSpec((nb, 32, LANES), blk3),
            pl.BlockSpec((PROJ_TILE, CONV_CH), row),
            pl.BlockSpec((slab, 2 * FFN_HIDDEN), row),
            pl.BlockSpec((wd_rows, D_MODEL), wd_blk),
            pl.BlockSpec((slab, D_MODEL), row),
        ],
        out_shape=[
            jax.ShapeDtypeStruct((t, NSA_WIDTH), BF16),
            jax.ShapeDtypeStruct((t, LANES), BF16),
            jax.ShapeDtypeStruct((t, LANES), BF16),
            jax.ShapeDtypeStruct((t // LANES, LANES, LANES), BF16),
            jax.ShapeDtypeStruct((t // LANES, LANES, LANES), BF16),
            jax.ShapeDtypeStruct((t // LANES, LANES, LANES), BF16),
            jax.ShapeDtypeStruct((t // LANES, LANES, LANES), BF16),
            jax.ShapeDtypeStruct((t, LANES), F32),
            jax.ShapeDtypeStruct((t, LANES), F32),
            jax.ShapeDtypeStruct((t // LANES, 32, LANES), F32),
            jax.ShapeDtypeStruct((t, CONV_CH), F32),
            jax.ShapeDtypeStruct((D_MODEL, 2 * FFN_HIDDEN), BF16),
            jax.ShapeDtypeStruct((FFN_HIDDEN, D_MODEL), BF16),
            jax.ShapeDtypeStruct((D_MODEL, D_MODEL), BF16),
        ],
        scratch_shapes=[pltpu.VMEM((D_MODEL, MAIN_COLS), BF16),
                        pltpu.VMEM((D_MODEL, 2 * CONV_CH), BF16),
                        pltpu.VMEM((D_MODEL, LANES), BF16)],
        compiler_params=pltpu.CompilerParams(
            dimension_semantics=("arbitrary",), vmem_limit_bytes=VMEM_LIMIT),
        name="proj",
    )(x2, gin, w_in, cos, sin, gq, gk, bd, w_gate_up, w_down, w_out)


def _cmp_mlp(t_ref, pe_ref, w1_ref, w2_ref):
    half = CMP_LEN // 2
    nchunk = t_ref.shape[0] // CMP_STRIDE
    toks = [t_ref[pl.ds(l, nchunk, stride=CMP_STRIDE), :] for l in range(half)]
    xp = _lane_cat([(toks[l] + pe_ref[l:l + 1, :]).astype(BF16) for l in range(half)])
    xq = _lane_cat([(toks[l] + pe_ref[half + l:half + l + 1, :]).astype(BF16) for l in range(half)])
    h1 = _dot(xp, w1_ref[0]) + pltpu.roll(_dot(xq, w1_ref[1]), nchunk - 1, 0)
    act = (h1 * _sigmoid(h1)).astype(BF16)
    return _dot(act, w2_ref[...])


def _cmp_body(kc_ref, vc_ref, pek_ref, w1k_ref, w2k_ref, pev_ref, w1v_ref, w2v_ref,
              cos_ref, sin_ref, gk_ref, bd_ref, ko_ref, vt0_ref, vt1_ref):
    kraw = _cmp_mlp(kc_ref, pek_ref, w1k_ref, w2k_ref)
    lane = lax.broadcasted_iota(jnp.int32, kraw.shape, 1)
    first_half = (lane & (HEAD_DIM // 2)) == 0
    (kr,) = _head_norm_rope(kraw, gk_ref[...], bd_ref[...], cos_ref[...], sin_ref[...],
                            first_half, 1.0)
    ko_ref[...] = kr.astype(BF16)
    vraw = _cmp_mlp(vc_ref, pev_ref, w1v_ref, w2v_ref)
    for i in range(CMP_SEQS):
        vt0_ref[i], vt1_ref[i] = _value_t_with_ones(vraw[i * LANES:(i + 1) * LANES, :])


def _cmp_call(kc, vc, pek, w1k, w2k, pev, w1v, w2v, cos, sin, gk, bd, batch, seq):
    seqblk = lambda b: (b, 0)
    c2 = lambda b: (0, 0)
    c3 = lambda b: (0, 0, 0)
    ncmp_pad = seq // CMP_STRIDE
    assert ncmp_pad == LANES and batch % CMP_SEQS == 0
    rows = CMP_SEQS * ncmp_pad
    half_k = (CMP_LEN // 2) * LANES
    return pl.pallas_call(
        _cmp_body,
        grid=(batch // CMP_SEQS,),
        in_specs=[
            pl.BlockSpec((CMP_SEQS * seq, LANES), seqblk),
            pl.BlockSpec((CMP_SEQS * seq, LANES), seqblk),
            pl.BlockSpec((CMP_LEN, LANES), c2),
            pl.BlockSpec((2, half_k, 2 * CMP_HIDDEN), c3),
            pl.BlockSpec((2 * CMP_HIDDEN, LANES), c2),
            pl.BlockSpec((CMP_LEN, LANES), c2),
            pl.BlockSpec((2, half_k, 2 * CMP_HIDDEN), c3),
            pl.BlockSpec((2 * CMP_HIDDEN, LANES), c2),
            pl.BlockSpec((rows, LANES), c2),
            pl.BlockSpec((rows, LANES), c2),
            pl.BlockSpec((1, LANES), c2),
            pl.BlockSpec((LANES, LANES), c2),
        ],
        out_specs=[
            pl.BlockSpec((rows, LANES), seqblk),
            pl.BlockSpec((CMP_SEQS, LANES, ncmp_pad), lambda b: (b, 0, 0)),
            pl.BlockSpec((CMP_SEQS, LANES, ncmp_pad), lambda b: (b, 0, 0)),
        ],
        out_shape=[
            jax.ShapeDtypeStruct((batch * ncmp_pad, LANES), BF16),
            jax.ShapeDtypeStruct((batch, LANES, ncmp_pad), BF16),
            jax.ShapeDtypeStruct((batch, LANES, ncmp_pad), BF16),
        ],
        compiler_params=pltpu.CompilerParams(
            dimension_semantics=("parallel",), vmem_limit_bytes=VMEM_LIMIT),
        name="compress",
    )(kc, vc, pek, w1k, w2k, pev, w1v, w2v, cos, sin, gk, bd)


def _lane_cat(blocks):
    return jnp.concatenate(blocks, axis=1)


def _softmax_numer(s, bias, m_run=None):
    es, ms = [], []
    for r in range(GQA_REP):
        sm = s[:, r * LANES:(r + 1) * LANES] + bias
        m = jnp.max(sm, axis=0, keepdims=True)
        if m_run is not None:
            m = jnp.maximum(m, m_run[:, r * LANES:(r + 1) * LANES])
        es.append(jnp.exp2(sm - m))
        ms.append(m)
    return _lane_cat(es), _lane_cat(ms)


def _safe_inv(l):
    return 1.0 / jnp.where(l > 0.0, l, 1.0)


def _attn_body(*refs):
    first = pl.program_id(0) * TILES_PER_STEP
    n_max = refs[1].shape[0] // SLC_CHUNK
    for n in range(1, n_max + 1):
        pl.when((first * Q_TILE) // SLC_CHUNK + 1 == n)(functools.partial(_attn_step, n, first, *refs))


def _attn_step(n_chunks, first, *refs):
    def one_tile(slot, carry):
        tile = _tile_program(n_chunks, first + slot, slot, *refs)
        pending = None
        s_next = tile.scores(tile.units[0])
        for i, unit in enumerate(tile.units):
            s_cur = s_next
            if i + 1 < len(tile.units):
                s_next = tile.scores(tile.units[i + 1])
            e, alpha = tile.softmax(unit, s_cur)
            if pending is not None:
                tile.weighted_values(*pending)
            pending = (unit, e, alpha)
        tile.weighted_values(*pending)
        tile.finish()
        return carry

    lax.fori_loop(0, TILES_PER_STEP, one_tile, 0)


def _tile_program(n_chunks, c, slot, q_ref, ks_ref, kw_ref, vs0_ref, vs1_ref, vw0_ref, vw1_ref, kc_ref,
                  vc0_ref, vc1_ref, gt_ref, ovl_ref, gn_ref, out_ref):
    groups = range(NSA_KV_HEADS)
    q_rows = pl.ds(pl.multiple_of(slot * Q_TILE, Q_TILE), Q_TILE)
    vs_refs, vw_refs, vc_refs = (vs0_ref, vs1_ref), (vw0_ref, vw1_ref), (vc0_ref, vc1_ref)
    sum_row = (HEAD_DIM, 0)
    width = GQA_REP * LANES
    t_lane = c * Q_TILE + lax.broadcasted_iota(jnp.int32, (1, LANES), 1)
    lane_sq = lax.broadcasted_iota(jnp.int32, (Q_TILE, LANES), 1)
    n_live = n_chunks * (SLC_CHUNK // SLC_BLOCK)
    nsel = 32
    win_keys = min(WIN_KEYS, n_chunks * SLC_CHUNK)
    j_idx = lax.broadcasted_iota(jnp.int32, (nsel, LANES), 0)
    cur = lax.shift_right_logical(t_lane, 6)
    n_idx = lax.broadcasted_iota(jnp.int32, (LANES, LANES), 0)
    cbias = jnp.where((n_idx * CMP_STRIDE + (CMP_LEN - 1) <= t_lane) & (n_idx < LANES - 1), 0.0, NEG)
    has_cmp = jnp.where(t_lane >= CMP_LEN - 1, 1.0, 0.0)
    has_cmp = _lane_cat([has_cmp] * GQA_REP)
    win_blk = jnp.maximum(c - WINDOW // Q_TILE, 0)
    win_start = pl.multiple_of(win_blk * Q_TILE, Q_TILE)
    kpos_w = win_start + lax.broadcasted_iota(jnp.int32, (win_keys, LANES), 0)
    wbias = jnp.where((kpos_w <= t_lane) & (kpos_w > t_lane - WINDOW), 0.0, NEG)
    row_sl = lax.broadcasted_iota(jnp.int32, (SLC_CHUNK, LANES), 0)
    zero_q = jnp.zeros((Q_TILE, LANES), BF16)

    def gate_row(g, branch):
        rows = [(g * GQA_REP + r) * 3 + branch for r in range(GQA_REP)]
        return _lane_cat([gt_ref[slot, i:i + 1, :] for i in rows])

    qg = []
    for g in groups:
        in_g = (lane_sq >= HEAD_DIM) if g else (lane_sq < HEAD_DIM)
        qg.append(jnp.concatenate(
            [jnp.where(in_g, q_ref[q_rows, r * LANES:(r + 1) * LANES], zero_q) for r in range(GQA_REP)],
            axis=0))

    o_cmp = [None] * NSA_KV_HEADS
    sel_bias = [None] * NSA_KV_HEADS

    def compressed_softmax(g, sc):
        ec, _ = _softmax_numer(sc, cbias)
        coef = has_cmp * _safe_inv(jnp.sum(ec, axis=0, keepdims=True))
        pc = ec * coef
        psum = (pc[:, 0:LANES] + pc[:, LANES:2 * LANES]
                + pc[:, 2 * LANES:3 * LANES] + pc[:, 3 * LANES:4 * LANES])
        p_hi, p_lo = _split_bf16(psum)
        imp = (_dot(ovl_ref[...], p_hi) + _dot(ovl_ref[...], p_lo))[0:nsel, :]
        valid = j_idx <= cur
        forced = (j_idx == 0) | (j_idx == cur) | (j_idx == cur - 1)
        score = jnp.where(valid, imp + jnp.where(forced, FORCE, 0.0), -FORCE)
        rank = jnp.zeros((nsel, LANES), F32)
        for jp in range(nsel):
            row = score[jp:jp + 1, :]
            beats = (row > score) | ((row == score) & (j_idx > jp))
            rank = rank + jnp.where(beats, 1.0, 0.0)
        sel_bias[g] = jnp.where((rank < float(SLC_TOPN)) & (score > -1.0), 0.0, NEG)
        return ec.astype(BF16), coef

    nsub = SLC_CHUNK // LANES
    blocks_per_chunk = SLC_CHUNK // SLC_BLOCK
    kwin = kw_ref[pl.ds(win_start, win_keys), :]
    units = [("cmp", g, 0) for g in groups] + [("win", g, 0) for g in groups]
    units += [("slc", g, kc) for kc in range(n_chunks) for g in groups]
    m_run = [None] * NSA_KV_HEADS
    acc_s = [None] * NSA_KV_HEADS
    acc_w = [None] * NSA_KV_HEADS

    def scores(unit):
        kind, g, kc = unit
        if kind == "cmp":
            keys = kc_ref[...]
        elif kind == "win":
            keys = kwin
        else:
            keys = ks_ref[kc * SLC_CHUNK:(kc + 1) * SLC_CHUNK, :]
        return _dot_tb(keys, qg[g])

    def softmax(unit, s):
        kind, g, kc = unit
        if kind == "cmp":
            return compressed_softmax(g, s)
        if kind == "win":
            e, _ = _softmax_numer(s, wbias)
            return e.astype(BF16), None
        j0 = kc * blocks_per_chunk
        bias = jnp.concatenate(
            [jnp.broadcast_to(sel_bias[g][j:j + 1, :], (SLC_BLOCK, LANES))
             for j in range(j0, j0 + blocks_per_chunk)], axis=0)
        if kc == n_chunks - 1:
            bias = jnp.where(kc * SLC_CHUNK + row_sl <= t_lane, bias, NEG)
        e, m_new = _softmax_numer(s, bias, m_run[g])
        alpha = None if kc == 0 else jnp.exp2(m_run[g] - m_new)
        m_run[g] = m_new
        return e.astype(BF16), alpha

    def weighted_values(unit, e, alpha):
        kind, g, kc = unit
        if kind == "cmp":
            o_cmp[g] = _dot(vc_refs[g][0], e) * (gate_row(g, 0) * alpha)
            return
        if kind == "win":
            vt = _lane_cat([vw_refs[g][win_blk + i] for i in range(win_keys // LANES)])
            acc_w[g] = _dot(vt, e)
            return
        vt = _lane_cat([vs_refs[g][nsub * kc + i] for i in range(nsub)])
        pv = _dot(vt, e)
        acc_s[g] = pv if alpha is None else acc_s[g] * alpha + pv

    def finish():
        o_groups = []
        for g in groups:
            r0 = sum_row[g]
            o_groups.append(o_cmp[g]
                            + acc_s[g] * (gate_row(g, 1) * _safe_inv(acc_s[g][r0:r0 + 1, :]))
                            + acc_w[g] * (gate_row(g, 2) * _safe_inv(acc_w[g][r0:r0 + 1, :])))
        row_d = lax.broadcasted_iota(jnp.int32, (LANES, width), 0)
        o_t = jnp.where(row_d < HEAD_DIM, o_groups[0], o_groups[1])
        sq = jnp.sum(o_t * o_t, axis=0, keepdims=True)
        ss = (sq[:, 0:LANES] + sq[:, LANES:2 * LANES]
              + sq[:, 2 * LANES:3 * LANES] + sq[:, 3 * LANES:4 * LANES])
        rs = lax.rsqrt(ss * (1.0 / NSA_WIDTH) + EPS)
        for r in range(GQA_REP):
            cols = slice(r * LANES, (r + 1) * LANES)
            out_ref[q_rows, cols] = (o_t[:, cols] * rs * gn_ref[:, cols]).T.astype(BF16)

    return types.SimpleNamespace(units=units, scores=scores, softmax=softmax,
                                 weighted_values=weighted_values, finish=finish)


def _attn_call(q, ks, kw, vs0, vs1, vw0, vw1, kcmp, vc0, vc1, gt, ovl, gn, batch, seq):
    assert SLC_CHUNK % (TILES_PER_STEP * Q_TILE) == 0
    rows = TILES_PER_STEP * Q_TILE
    nq = seq // rows
    nkb = seq // LANES
    qrow = lambda c, b: (b * nq + c, 0)
    per_b = lambda c, b: (b, 0)
    per_b3 = lambda c, b: (b, 0, 0)
    vspec = pl.BlockSpec((nkb, LANES, LANES), per_b3)
    cspec = pl.BlockSpec((1, LANES, LANES), per_b3)
    return pl.pallas_call(
        _attn_body,
        grid=(nq, batch),
        in_specs=[
            pl.BlockSpec((rows, NSA_WIDTH), qrow),
            pl.BlockSpec((seq, LANES), per_b),
            pl.BlockSpec((seq, LANES), per_b),
            vspec, vspec, vspec, vspec,
            pl.BlockSpec((LANES, LANES), per_b),
            cspec, cspec,
            pl.BlockSpec((TILES_PER_STEP, 32, LANES), lambda c, b: (b * nq + c, 0, 0)),
            pl.BlockSpec((LANES, LANES), lambda c, b: (0, 0)),
            pl.BlockSpec((LANES, NSA_WIDTH), lambda c, b: (0, 0)),
        ],
        out_specs=pl.BlockSpec((rows, NSA_WIDTH), qrow),
        out_shape=jax.ShapeDtypeStruct((batch * seq, NSA_WIDTH), BF16),
        compiler_params=pltpu.CompilerParams(
            dimension_semantics=("parallel", "arbitrary"), vmem_limit_bytes=VMEM_LIMIT),
        name="attn",
    )(q, ks, kw, vs0, vs1, vw0, vw1, kcmp, vc0, vc1, gt, ovl, gn)


def _conv_body(tiles_per_seq, cur_ref, halo_ref, w_ref, b_ref, lng_ref, lnb_ref, gn_ref, out_ref,
               buf_ref, sh_ref, cv_ref):
    first = (pl.program_id(0) % tiles_per_seq) == 0
    buf_ref[0:CONV_HALO, :] = jnp.where(first, 0.0, halo_ref[...])
    buf_ref[CONV_HALO:, :] = cur_ref[...]
    lead = CONV_HALO - (CONV_WIDTH - 1)
    ncb = CONV_CH // LANES
    nrb = CONV_TILE // CONV_ROWS
    for s in range(8):
        span = CONV_TILE + 8 * ((CONV_WIDTH - 1 - s) // 8)
        for cb in range(ncb):
            sh_ref[s, cb, 0:span, :] = buf_ref[lead + s:lead + s + span, cb * LANES:(cb + 1) * LANES]

    def conv_block(i, carry):
        cb = i % ncb
        base = pl.multiple_of((i // ncb) * CONV_ROWS, CONV_ROWS)
        acc = jnp.zeros((CONV_ROWS, LANES), F32) + b_ref[cb]
        for k in range(CONV_WIDTH):
            acc = acc + w_ref[cb, k:k + 1, :] * sh_ref[k % 8, cb, pl.ds(base + 8 * (k // 8), CONV_ROWS), :]
        cv_ref[cb, pl.ds(base, CONV_ROWS), :] = acc
        return carry

    lax.fori_loop(0, nrb * ncb, conv_block, 0)
    h = _lane_cat([cv_ref[cb] for cb in range(ncb)])
    mu = jnp.mean(h, axis=-1, keepdims=True)
    d = h - mu
    var = jnp.mean(d * d, axis=-1, keepdims=True)
    hn = d * lax.rsqrt(var + EPS) * lng_ref[...] + lnb_ref[...]
    o = hn * _sigmoid(hn)
    ms = jnp.mean(o * o, axis=-1, keepdims=True)
    out_ref[...] = (o * lax.rsqrt(ms + EPS) * gn_ref[...]).astype(BF16)


def _conv_call(hcv, w, b, lng, lnb, gn, seq):
    t = hcv.shape[0]
    nt = t // CONV_TILE
    halo_per_tile = CONV_TILE // CONV_HALO
    ncb = CONV_CH // LANES
    row = lambda i: (i, 0)
    const = lambda i: (0, 0)
    return pl.pallas_call(
        functools.partial(_conv_body, seq // CONV_TILE),
        grid=(nt,),
        in_specs=[
            pl.BlockSpec((CONV_TILE, CONV_CH), row),
            pl.BlockSpec((CONV_HALO, CONV_CH), lambda i: (jnp.maximum(i * halo_per_tile - 1, 0), 0)),
            pl.BlockSpec((ncb, CONV_HALO, LANES), lambda i: (0, 0, 0)),
            pl.BlockSpec((ncb, 1, LANES), lambda i: (0, 0, 0)),
            pl.BlockSpec((1, CONV_CH), const),
            pl.BlockSpec((1, CONV_CH), const),
            pl.BlockSpec((1, CONV_CH), const),
        ],
        out_specs=pl.BlockSpec((CONV_TILE, CONV_CH), row),
        out_shape=jax.ShapeDtypeStruct((t, CONV_CH), BF16),
        scratch_shapes=[pltpu.VMEM((CONV_HALO + CONV_TILE, CONV_CH), F32),
                        pltpu.VMEM((8, ncb, CONV_TILE + 8 * ((CONV_WIDTH - 1) // 8), LANES), F32),
                        pltpu.VMEM((ncb, CONV_TILE, LANES), F32)],
        compiler_params=pltpu.CompilerParams(
            dimension_semantics=("parallel",), vmem_limit_bytes=VMEM_LIMIT),
        name="conv",
    )(hcv, hcv, w, b, lng, lnb, gn)


def _ffn_body(x_ref, ma_ref, mb_ref, wo_ref, gf_ref, wgu_ref, wd_ref, out_ref):
    h = (x_ref[...] + _dot(ma_ref[...], wo_ref[0:NSA_WIDTH, :])
         + _dot(mb_ref[...], wo_ref[NSA_WIDTH:D_MODEL, :]))
    ms = jnp.mean(h * h, axis=-1, keepdims=True)
    hn = (h * lax.rsqrt(ms + EPS) * gf_ref[...]).astype(BF16)
    acc = jnp.zeros_like(h)
    for j in range(FFN_HIDDEN // FFN_CHUNK):
        c0 = j * FFN_CHUNK
        gte = _dot(hn, wgu_ref[:, c0:c0 + FFN_CHUNK])
        up = _dot(hn, wgu_ref[:, FFN_HIDDEN + c0:FFN_HIDDEN + c0 + FFN_CHUNK])
        act = (gte * _sigmoid(gte) * up).astype(BF16)
        acc = acc + _dot(act, wd_ref[c0:c0 + FFN_CHUNK, :])
    out_ref[...] = h + acc


def _ffn_call(x2, ma, mb, wo, gf, wgu, wd):
    t = x2.shape[0]
    nt = t // FFN_TILE
    row = lambda i: (i, 0)
    const = lambda i: (0, 0)
    once = pl.Buffered(1)
    return pl.pallas_call(
        _ffn_body,
        grid=(nt,),
        in_specs=[
            pl.BlockSpec((FFN_TILE, D_MODEL), row),
            pl.BlockSpec((FFN_TILE, NSA_WIDTH), row),
            pl.BlockSpec((FFN_TILE, CONV_CH), row),
            pl.BlockSpec((D_MODEL, D_MODEL), const, pipeline_mode=once),
            pl.BlockSpec((1, D_MODEL), const),
            pl.BlockSpec((D_MODEL, 2 * FFN_HIDDEN), const, pipeline_mode=once),
            pl.BlockSpec((FFN_HIDDEN, D_MODEL), const, pipeline_mode=once),
        ],
        out_specs=pl.BlockSpec((FFN_TILE, D_MODEL), row),
        out_shape=jax.ShapeDtypeStruct((t, D_MODEL), F32),
        compiler_params=pltpu.CompilerParams(
            dimension_semantics=("parallel",), vmem_limit_bytes=VMEM_LIMIT),
        name="ffn",
    )(x2, ma, mb, wo, gf, wgu, wd)


def _pair_heads(a, axis):
    shape = a.shape
    a = a.reshape(shape[:axis] + (NSA_KV_HEADS, GQA_REP, HEAD_DIM) + shape[axis + 1:])
    return jnp.swapaxes(a, axis, axis + 1).reshape(shape)


def _transposed_gain(g):
    cols = g.reshape(GQA_REP, LANES).T
    return jnp.broadcast_to(cols[:, :, None], (LANES, GQA_REP, Q_TILE)).reshape(LANES, GQA_REP * Q_TILE)


def _rope_tables(pos):
    half = HEAD_DIM // 2
    inv = ROPE_THETA ** (-np.arange(half, dtype=np.float64) / half)
    ang = np.asarray(pos, np.float64)[:, None] * inv[None, :]
    cos = np.concatenate([np.cos(ang), np.cos(ang)], axis=1)
    sin = np.concatenate([-np.sin(ang), np.sin(ang)], axis=1)
    return (np.tile(cos, (1, NSA_KV_HEADS)).astype(np.float32),
            np.tile(sin, (1, NSA_KV_HEADS)).astype(np.float32))


def _block_diag_ones(width):
    idx = np.arange(width) // HEAD_DIM
    return (idx[:, None] == idx[None, :]).astype(np.float32)


def _overlap_t(seq):
    ncmp = (seq - CMP_LEN) // CMP_STRIDE + 1
    cs = np.arange(ncmp)[:, None] * CMP_STRIDE
    ss = np.arange(seq // SLC_BLOCK)[None, :] * SLC_BLOCK
    ov = np.clip(np.minimum(cs + CMP_LEN, ss + SLC_BLOCK) - np.maximum(cs, ss), 0, None) / CMP_LEN
    out = np.zeros((LANES, LANES), np.float32)
    out[:seq // SLC_BLOCK, :ncmp] = ov.T
    return out


def _cmp_weights(pe, w1, w2):
    pe2 = jnp.concatenate([pe, pe], axis=1)
    w1r = w1.reshape(CMP_LEN, HEAD_DIM, CMP_HIDDEN)
    z1 = jnp.zeros_like(w1r)
    w1b = jnp.concatenate([jnp.concatenate([w1r, z1], axis=2),
                           jnp.concatenate([z1, w1r], axis=2)], axis=1).astype(BF16)
    z2 = jnp.zeros_like(w2)
    w2b = jnp.concatenate([jnp.concatenate([w2, z2], axis=1),
                           jnp.concatenate([z2, w2], axis=1)], axis=0).astype(BF16)
    w1b = w1b.reshape(2, (CMP_LEN // 2) * LANES, 2 * CMP_HIDDEN)
    return pe2, w1b, w2b


def kernel(x, attn_norm_g, w_in, q_norm_g, k_norm_cmp_g, k_norm_slc_g, k_norm_win_g, cmp_pe_k, cmp_w1_k, cmp_w2_k, cmp_pe_v, cmp_w1_v, cmp_w2_v, conv_dw_w, conv_dw_b, conv_ln_g, conv_ln_b, out_norm_nsa_g, out_norm_conv_g, w_out, ffn_norm_g, w_gate_up, w_down):
    batch, seq, d_model = x.shape
    assert d_model == D_MODEL and seq % PROJ_TILE == 0 and seq // SLC_BLOCK == 32
    depth = w_in.shape[0]
    cos_np, sin_np = _rope_tables(np.arange(seq))
    ccos_np, csin_np = _rope_tables(np.arange(seq // CMP_STRIDE) * CMP_STRIDE + CMP_LEN - 1)
    ccos_np, csin_np = np.tile(ccos_np, (CMP_SEQS, 1)), np.tile(csin_np, (CMP_SEQS, 1))
    cos, sin, ccos, csin = map(jnp.asarray, (cos_np, sin_np, ccos_np, csin_np))
    bd256 = jnp.asarray(_block_diag_ones(256), BF16)
    bd128 = jnp.asarray(_block_diag_ones(LANES), BF16)
    ovl = jnp.asarray(_overlap_t(seq), BF16)

    x2 = x.reshape(batch * seq, d_model)
    for l in range(depth):
        w = w_in[l]
        gq = jnp.tile(q_norm_g[l], 4)[None, :]
        gk = jnp.concatenate([jnp.tile(k_norm_slc_g[l], 2), jnp.tile(k_norm_win_g[l], 2)])[None, :]
        q, ks, kw, vs0, vs1, vw0, vw1, kc, vc, gt, hcv, wgu_b, wd_b, wo_b = _proj_call(
            x2, attn_norm_g[l][None, :], w.T, cos, sin, gq, gk, bd256,
            w_gate_up[l], w_down[l], w_out[l], seq)

        pek, w1k, w2k = _cmp_weights(cmp_pe_k[l], cmp_w1_k[l], cmp_w2_k[l])
        pev, w1v, w2v = _cmp_weights(cmp_pe_v[l], cmp_w1_v[l], cmp_w2_v[l])
        kcmp, vc0, vc1 = _cmp_call(kc, vc, pek, w1k, w2k, pev, w1v, w2v, ccos, csin,
                                jnp.tile(k_norm_cmp_g[l], 2)[None, :], bd128, batch, seq)

        mix_a = _attn_call(q, ks, kw, vs0, vs1, vw0, vw1, kcmp, vc0, vc1, gt, ovl,
                           _transposed_gain(_pair_heads(out_norm_nsa_g[l], 0)), batch, seq)

        w_dw = jnp.concatenate([conv_dw_w[l][:, 0, :],
                                jnp.zeros((CONV_HALO - CONV_WIDTH, CONV_CH), F32)], axis=0)
        w_dw = w_dw.reshape(CONV_HALO, CONV_CH // LANES, LANES).transpose(1, 0, 2)
        mix_b = _conv_call(hcv, w_dw, conv_dw_b[l].reshape(CONV_CH // LANES, 1, LANES),
                           conv_ln_g[l][None, :],
                           conv_ln_b[l][None, :], out_norm_conv_g[l][None, :], seq)

        x2 = _ffn_call(x2, mix_a, mix_b, wo_b, ffn_norm_g[l][None, :], wgu_b, wd_b)
    return x2.reshape(batch, seq, d_model)
```

```python
import functools
import types

import numpy as np
import jax
import jax.numpy as jnp
from jax import lax
from jax.experimental import pallas as pl
from jax.experimental.pallas import tpu as pltpu

F32 = jnp.float32
BF16 = jnp.bfloat16

D_MODEL = 1024
HEAD_DIM = 64
NSA_HEADS = 8
NSA_KV_HEADS = 2
GQA_REP = NSA_HEADS // NSA_KV_HEADS
NSA_WIDTH = NSA_HEADS * HEAD_DIM
CONV_CH = D_MODEL - NSA_WIDTH
CMP_LEN = 32
CMP_STRIDE = 16
CMP_HIDDEN = 4 * HEAD_DIM
SLC_BLOCK = 64
SLC_TOPN = 8
WINDOW = 512
CONV_WIDTH = 31
FFN_HIDDEN = 2816
ROPE_THETA = 10000.0
EPS = 1e-6
NEG = -1e30
LOG2E = 1.4426950408889634
FORCE = 1e6

LANES = 128
Q_TILE = 128
TILES_PER_STEP = 2
SLC_CHUNK = 256
WIN_KEYS = WINDOW + Q_TILE
PROJ_TILE = 1024
CMP_SEQS = 4
CONV_TILE = 1024
CONV_HALO = 32
CONV_ROWS = 256
FFN_TILE = 1024
FFN_CHUNK = 256
VMEM_LIMIT = 56 * 1024 * 1024

C_Q = 0
C_CMP = 512
C_SLC = 768
C_WIN = 1024
MAIN_COLS = 1280
GATE_COLS = 3 * NSA_HEADS

_TRANS_B = (((1,), (1,)), ((), ()))


def _dot(a, b):
    return jnp.dot(a, b, preferred_element_type=F32)


def _dot_tb(a, b):
    return lax.dot_general(a, b, _TRANS_B, preferred_element_type=F32)


def _split_bf16(x):
    hi = x.astype(BF16)
    lo = (x - hi.astype(F32)).astype(BF16)
    return hi, lo


def _sigmoid(x):
    return 1.0 / (1.0 + jnp.exp(-x))


def _head_norm_rope(y, gain, bd, cos, sin, first_half, scale):
    ss = _dot((y * y).astype(BF16), bd)
    yn = y * lax.rsqrt(ss * (1.0 / HEAD_DIM) + EPS) * gain
    outs = []
    for j in range(y.shape[1] // LANES):
        blk = yn[:, j * LANES:(j + 1) * LANES]
        rot = jnp.where(first_half, pltpu.roll(blk, LANES - HEAD_DIM // 2, 1),
                        pltpu.roll(blk, HEAD_DIM // 2, 1))
        outs.append((blk * cos + rot * sin) * scale)
    return outs


def _value_t_with_ones(v):
    vt = v.T
    row = lax.broadcasted_iota(jnp.int32, vt.shape, 0)
    return (jnp.where(row < HEAD_DIM, vt, 1.0).astype(BF16),
            jnp.where(row < HEAD_DIM, 1.0, vt).astype(BF16))


def _proj_body(x_ref, gin_ref, w_ref, cos_ref, sin_ref, gq_ref, gk_ref, bd_ref,
               wgu_in_ref, wd_in_ref, wo_in_ref,
               q_ref, ks_ref, kw_ref, vs0_ref, vs1_ref, vw0_ref, vw1_ref, kc_ref, vc_ref, gt_ref,
               hcv_ref, wgu_out_ref, wd_out_ref, wo_out_ref, wm_ref, wc_ref, wg_ref):
    @pl.when(pl.program_id(0) == 0)
    def _():
        for j in range(MAIN_COLS // LANES):
            wm_ref[:, j * LANES:(j + 1) * LANES] = w_ref[j * LANES:(j + 1) * LANES, :].T.astype(BF16)
        gl = jnp.concatenate([w_ref[MAIN_COLS:MAIN_COLS + GATE_COLS, :],
                              jnp.zeros((LANES - GATE_COLS, D_MODEL), F32)], axis=0)
        wg_ref[...] = gl.T.astype(BF16)
        for j in range(2 * CONV_CH // LANES):
            r0 = MAIN_COLS + GATE_COLS + j * LANES
            wc_ref[:, j * LANES:(j + 1) * LANES] = w_ref[r0:r0 + LANES, :].T.astype(BF16)

    wgu_out_ref[...] = wgu_in_ref[...].astype(BF16)
    wd_out_ref[...] = wd_in_ref[...].astype(BF16)
    wo_out_ref[...] = wo_in_ref[...].astype(BF16)

    x = x_ref[...]
    xg = (x * gin_ref[...]).astype(BF16)
    rs = lax.rsqrt(jnp.mean(x * x, axis=-1, keepdims=True) + EPS)
    cos = cos_ref[...]
    sin = sin_ref[...]
    bd = bd_ref[...]
    lane = lax.broadcasted_iota(jnp.int32, (PROJ_TILE, LANES), 1)
    first_half = (lane & (HEAD_DIM // 2)) == 0
    held = {}

    def queries(group, y):
        held["q", group] = _head_norm_rope(y, gq_ref[...], bd, cos, sin, first_half,
                                           HEAD_DIM ** -0.5 * LOG2E)
        if group == 0:
            return
        low = lane < HEAD_DIM
        for j in range(2):
            a, b = held["q", 0][j], held["q", 1][j]
            q_ref[:, (2 * j) * LANES:(2 * j + 1) * LANES] = jnp.where(
                low, a, pltpu.roll(b, HEAD_DIM, 1)).astype(BF16)
            q_ref[:, (2 * j + 1) * LANES:(2 * j + 2) * LANES] = jnp.where(
                low, pltpu.roll(a, HEAD_DIM, 1), b).astype(BF16)

    def values(y, out0_ref, out1_ref):
        for i in range(PROJ_TILE // LANES):
            out0_ref[i], out1_ref[i] = _value_t_with_ones(y[i * LANES:(i + 1) * LANES, :])

    def slc_kv(y):
        held["ks"] = y[:, :LANES]
        values(y[:, LANES:], vs0_ref, vs1_ref)

    def win_kv(y):
        kblks = _head_norm_rope(_lane_cat([held["ks"], y[:, :LANES]]), gk_ref[...], bd, cos, sin,
                                first_half, 1.0)
        ks_ref[...] = kblks[0].astype(BF16)
        kw_ref[...] = kblks[1].astype(BF16)
        values(y[:, LANES:], vw0_ref, vw1_ref)

    def cmp_inputs(y):
        kc_ref[...] = y[:, :LANES]
        vc_ref[...] = y[:, LANES:]

    def conv_lin(y):
        held["ca"] = y

    def conv_glu(y):
        hcv_ref[...] = held["ca"] * _sigmoid(y)

    def gates(y):
        sg = _sigmoid(y)
        for i in range(PROJ_TILE // LANES):
            gt_ref[i] = sg[i * LANES:(i + 1) * LANES, :].T[0:32, :]

    sections = [
        (wm_ref, C_Q, 256, functools.partial(queries, 0)),
        (wm_ref, C_Q + 256, 256, functools.partial(queries, 1)),
        (wc_ref, 0, CONV_CH, conv_lin),
        (wm_ref, C_SLC, 256, slc_kv),
        (wc_ref, CONV_CH, CONV_CH, conv_glu),
        (wm_ref, C_WIN, 256, win_kv),
        (wm_ref, C_CMP, 256, cmp_inputs),
        (wg_ref, 0, LANES, gates),
    ]

    def project(sec):
        w_ref, c0, width, _ = sec
        return _dot(xg, w_ref[:, c0:c0 + width])

    y_next = project(sections[0])
    for i, sec in enumerate(sections):
        y_cur = y_next
        if i + 1 < len(sections):
            y_next = project(sections[i + 1])
        sec[3](y_cur * rs)


def _paired_row_block(i, slab):
    per_head = HEAD_DIM // slab
    head = i // per_head
    src = ((head % NSA_KV_HEADS) * GQA_REP + head // NSA_KV_HEADS) * per_head + i % per_head
    return jnp.where(i < NSA_WIDTH // slab, src, i)


def _proj_call(x2, gin, w_in, cos, sin, gq, gk, bd, w_gate_up, w_down, w_out, seq):
    t = x2.shape[0]
    nt = t // PROJ_TILE
    slab = D_MODEL // nt
    assert D_MODEL % nt == 0 and slab % 16 == 0 and HEAD_DIM % slab == 0
    assert nt % 16 == 0 and (FFN_HIDDEN // 16) % 16 == 0
    wd_rows = FFN_HIDDEN // 16
    tiles_per_seq = seq // PROJ_TILE
    row = lambda i: (i, 0)
    const = lambda i: (0, 0)
    tab = lambda i: (i % tiles_per_seq, 0)
    blk3 = lambda i: (i, 0, 0)
    wd_blk = lambda i: (i * 16 // nt, 0)
    nb = PROJ_TILE // LANES
    return pl.pallas_call(
        _proj_body,
        grid=(nt,),
        in_specs=[
            pl.BlockSpec((PROJ_TILE, D_MODEL), row),
            pl.BlockSpec((1, D_MODEL), const),
            pl.BlockSpec(w_in.shape, const, pipeline_mode=pl.Buffered(1)),
            pl.BlockSpec((PROJ_TILE, LANES), tab),
            pl.BlockSpec((PROJ_TILE, LANES), tab),
            pl.BlockSpec((1, 256), const),
            pl.BlockSpec((1, 256), const),
            pl.BlockSpec((256, 256), const),
            pl.BlockSpec((slab, 2 * FFN_HIDDEN), row),
            pl.BlockSpec((wd_rows, D_MODEL), wd_blk),
            pl.BlockSpec((slab, D_MODEL), lambda i: (_paired_row_block(i, slab), 0)),
        ],
        out_specs=[
            pl.BlockSpec((PROJ_TILE, NSA_WIDTH), row),
            pl.BlockSpec((PROJ_TILE, LANES), row),
            pl.BlockSpec((PROJ_TILE, LANES), row),
            pl.BlockSpec((nb, LANES, LANES), blk3),
            pl.BlockSpec((nb, LANES, LANES), blk3),
            pl.BlockSpec((nb, LANES, LANES), blk3),
            pl.BlockSpec((nb, LANES, LANES), blk3),
            pl.BlockSpec((PROJ_TILE, LANES), row),
            pl.BlockSpec((PROJ_TILE, LANES), row),
            pl.BlockSpec((nb, 32, LANES), blk3),
            pl.BlockSpec((PROJ_TILE, CONV_CH), row),
            pl.BlockSpec((slab, 2 * FFN_HIDDEN), row),
            pl.BlockSpec((wd_rows, D_MODEL), wd_blk),
            pl.BlockSpec((slab, D_MODEL), row),
        ],
        out_shape=[
            jax.ShapeDtypeStruct((t, NSA_WIDTH), BF16),
            jax.ShapeDtypeStruct((t, LANES), BF16),
            jax.ShapeDtypeStruct((t, LANES), BF16),
            jax.ShapeDtypeStruct((t // LANES, LANES, LANES), BF16),
            jax.ShapeDtypeStruct((t // LANES, LANES, LANES), BF16),
            jax.ShapeDtypeStruct((t // LANES, LANES, LANES), BF16),
            jax.ShapeDtypeStruct((t // LANES, LANES, LANES), BF16),
            jax.ShapeDtypeStruct((t, LANES), F32),
            jax.ShapeDtypeStruct((t, LANES), F32),
            jax.ShapeDtypeStruct((t // LANES, 32, LANES), F32),
            jax.ShapeDtypeStruct((t, CONV_CH), F32),
            jax.ShapeDtypeStruct((D_MODEL, 2 * FFN_HIDDEN), BF16),
            jax.ShapeDtypeStruct((FFN_HIDDEN, D_MODEL), BF16),
            jax.ShapeDtypeStruct((D_MODEL, D_MODEL), BF16),
        ],
        scratch_shapes=[pltpu.VMEM((D_MODEL, MAIN_COLS), BF16),
                        pltpu.VMEM((D_MODEL, 2 * CONV_CH), BF16),
                        pltpu.VMEM((D_MODEL, LANES), BF16)],
        compiler_params=pltpu.CompilerParams(
            dimension_semantics=("arbitrary",), vmem_limit_bytes=VMEM_LIMIT),
        name="proj",
    )(x2, gin, w_in, cos, sin, gq, gk, bd, w_gate_up, w_down, w_out)


def _cmp_mlp(t_ref, pe_ref, w1_ref, w2_ref):
    half = CMP_LEN // 2
    nchunk = t_ref.shape[0] // CMP_STRIDE
    toks = [t_ref[pl.ds(l, nchunk, stride=CMP_STRIDE), :] for l in range(half)]
    xp = _lane_cat([(toks[l] + pe_ref[l:l + 1, :]).astype(BF16) for l in range(half)])
    xq = _lane_cat([(toks[l] + pe_ref[half + l:half + l + 1, :]).astype(BF16) for l in range(half)])
    h1 = _dot(xp, w1_ref[0]) + pltpu.roll(_dot(xq, w1_ref[1]), nchunk - 1, 0)
    act = (h1 * _sigmoid(h1)).astype(BF16)
    return _dot(act, w2_ref[...])


def _cmp_body(kc_ref, vc_ref, pek_ref, w1k_ref, w2k_ref, pev_ref, w1v_ref, w2v_ref,
              cos_ref, sin_ref, gk_ref, bd_ref, ko_ref, vt0_ref, vt1_ref):
    kraw = _cmp_mlp(kc_ref, pek_ref, w1k_ref, w2k_ref)
    lane = lax.broadcasted_iota(jnp.int32, kraw.shape, 1)
    first_half = (lane & (HEAD_DIM // 2)) == 0
    (kr,) = _head_norm_rope(kraw, gk_ref[...], bd_ref[...], cos_ref[...], sin_ref[...],
                            first_half, 1.0)
    ko_ref[...] = kr.astype(BF16)
    vraw = _cmp_mlp(vc_ref, pev_ref, w1v_ref, w2v_ref)
    for i in range(CMP_SEQS):
        vt0_ref[i], vt1_ref[i] = _value_t_with_ones(vraw[i * LANES:(i + 1) * LANES, :])


def _cmp_call(kc, vc, pek, w1k, w2k, pev, w1v, w2v, cos, sin, gk, bd, batch, seq):
    seqblk = lambda b: (b, 0)
    c2 = lambda b: (0, 0)
    c3 = lambda b: (0, 0, 0)
    ncmp_pad = seq // CMP_STRIDE
    assert ncmp_pad == LANES and batch % CMP_SEQS == 0
    rows = CMP_SEQS * ncmp_pad
    half_k = (CMP_LEN // 2) * LANES
    return pl.pallas_call(
        _cmp_body,
        grid=(batch // CMP_SEQS,),
        in_specs=[
            pl.BlockSpec((CMP_SEQS * seq, LANES), seqblk),
            pl.BlockSpec((CMP_SEQS * seq, LANES), seqblk),
            pl.BlockSpec((CMP_LEN, LANES), c2),
            pl.BlockSpec((2, half_k, 2 * CMP_HIDDEN), c3),
            pl.BlockSpec((2 * CMP_HIDDEN, LANES), c2),
            pl.BlockSpec((CMP_LEN, LANES), c2),
            pl.BlockSpec((2, half_k, 2 * CMP_HIDDEN), c3),
            pl.BlockSpec((2 * CMP_HIDDEN, LANES), c2),
            pl.BlockSpec((rows, LANES), c2),
            pl.BlockSpec((rows, LANES), c2),
            pl.BlockSpec((1, LANES), c2),
            pl.BlockSpec((LANES, LANES), c2),
        ],
        out_specs=[
            pl.BlockSpec((rows, LANES), seqblk),
            pl.BlockSpec((CMP_SEQS, LANES, ncmp_pad), lambda b: (b, 0, 0)),
            pl.BlockSpec((CMP_SEQS, LANES, ncmp_pad), lambda b: (b, 0, 0)),
        ],
        out_shape=[
            jax.ShapeDtypeStruct((batch * ncmp_pad, LANES), BF16),
            jax.ShapeDtypeStruct((batch, LANES, ncmp_pad), BF16),
            jax.ShapeDtypeStruct((batch, LANES, ncmp_pad), BF16),
        ],
        compiler_params=pltpu.CompilerParams(
            dimension_semantics=("parallel",), vmem_limit_bytes=VMEM_LIMIT),
        name="compress",
    )(kc, vc, pek, w1k, w2k, pev, w1v, w2v, cos, sin, gk, bd)


def _lane_cat(blocks):
    return jnp.concatenate(blocks, axis=1)


def _softmax_numer(s, bias, m_run=None):
    es, ms = [], []
    for r in range(GQA_REP):
        sm = s[:, r * LANES:(r + 1) * LANES] + bias
        m = jnp.max(sm, axis=0, keepdims=True)
        if m_run is not None:
            m = jnp.maximum(m, m_run[:, r * LANES:(r + 1) * LANES])
        es.append(jnp.exp2(sm - m))
        ms.append(m)
    return _lane_cat(es), _lane_cat(ms)


def _safe_inv(l):
    return 1.0 / jnp.where(l > 0.0, l, 1.0)


def _attn_body(*refs):
    first = pl.program_id(0) * TILES_PER_STEP
    n_max = refs[1].shape[0] // SLC_CHUNK
    for n in range(1, n_max + 1):
        pl.when((first * Q_TILE) // SLC_CHUNK + 1 == n)(functools.partial(_attn_step, n, first, *refs))


def _attn_step(n_chunks, first, *refs):
    def one_tile(slot, carry):
        tile = _tile_program(n_chunks, first + slot, slot, *refs)
        pending = None
        s_next = tile.scores(tile.units[0])
        for i, unit in enumerate(tile.units):
            s_cur = s_next
            if i + 1 < len(tile.units):
                s_next = tile.scores(tile.units[i + 1])
            e, alpha = tile.softmax(unit, s_cur)
            if pending is not None:
                tile.weighted_values(*pending)
            pending = (unit, e, alpha)
        tile.weighted_values(*pending)
        tile.finish()
        return carry

    lax.fori_loop(0, TILES_PER_STEP, one_tile, 0)


def _tile_program(n_chunks, c, slot, q_ref, ks_ref, kw_ref, vs0_ref, vs1_ref, vw0_ref, vw1_ref, kc_ref,
                  vc0_ref, vc1_ref, gt_ref, ovl_ref, gn_ref, out_ref):
    groups = range(NSA_KV_HEADS)
    q_rows = pl.ds(pl.multiple_of(slot * Q_TILE, Q_TILE), Q_TILE)
    vs_refs, vw_refs, vc_refs = (vs0_ref, vs1_ref), (vw0_ref, vw1_ref), (vc0_ref, vc1_ref)
    sum_row = (HEAD_DIM, 0)
    width = GQA_REP * LANES
    t_lane = c * Q_TILE + lax.broadcasted_iota(jnp.int32, (1, LANES), 1)
    lane_sq = lax.broadcasted_iota(jnp.int32, (Q_TILE, LANES), 1)
    n_live = n_chunks * (SLC_CHUNK // SLC_BLOCK)
    nsel = 32
    win_keys = min(WIN_KEYS, n_chunks * SLC_CHUNK)
    j_idx = lax.broadcasted_iota(jnp.int32, (nsel, LANES), 0)
    cur = lax.shift_right_logical(t_lane, 6)
    n_idx = lax.broadcasted_iota(jnp.int32, (LANES, LANES), 0)
    cbias = jnp.where((n_idx * CMP_STRIDE + (CMP_LEN - 1) <= t_lane) & (n_idx < LANES - 1), 0.0, NEG)
    has_cmp = jnp.where(t_lane >= CMP_LEN - 1, 1.0, 0.0)
    has_cmp = _lane_cat([has_cmp] * GQA_REP)
    win_blk = jnp.maximum(c - WINDOW // Q_TILE, 0)
    win_start = pl.multiple_of(win_blk * Q_TILE, Q_TILE)
    kpos_w = win_start + lax.broadcasted_iota(jnp.int32, (win_keys, LANES), 0)
    wbias = jnp.where((kpos_w <= t_lane) & (kpos_w > t_lane - WINDOW), 0.0, NEG)
    row_sl = lax.broadcasted_iota(jnp.int32, (SLC_CHUNK, LANES), 0)
    zero_q = jnp.zeros((Q_TILE, LANES), BF16)

    def gate_row(g, branch):
        rows = [(g * GQA_REP + r) * 3 + branch for r in range(GQA_REP)]
        return _lane_cat([gt_ref[slot, i:i + 1, :] for i in rows])

    qg = []
    for g in groups:
        in_g = (lane_sq >= HEAD_DIM) if g else (lane_sq < HEAD_DIM)
        qg.append(jnp.concatenate(
            [jnp.where(in_g, q_ref[q_rows, r * LANES:(r + 1) * LANES], zero_q) for r in range(GQA_REP)],
            axis=0))

    o_cmp = [None] * NSA_KV_HEADS
    sel_bias = [None] * NSA_KV_HEADS

    def compressed_softmax(g, sc):
        ec, _ = _softmax_numer(sc, cbias)
        coef = has_cmp * _safe_inv(jnp.sum(ec, axis=0, keepdims=True))
        pc = ec * coef
        psum = (pc[:, 0:LANES] + pc[:, LANES:2 * LANES]
                + pc[:, 2 * LANES:3 * LANES] + pc[:, 3 * LANES:4 * LANES])
        p_hi, p_lo = _split_bf16(psum)
        imp = (_dot(ovl_ref[...], p_hi) + _dot(ovl_ref[...], p_lo))[0:nsel, :]
        valid = j_idx <= cur
        forced = (j_idx == 0) | (j_idx == cur) | (j_idx == cur - 1)
        score = jnp.where(valid, imp + jnp.where(forced, FORCE, 0.0), -FORCE)
        rank = jnp.zeros((nsel, LANES), F32)
        for jp in range(nsel):
            row = score[jp:jp + 1, :]
            beats = (row > score) | ((row == score) & (j_idx > jp))
            rank = rank + jnp.where(beats, 1.0, 0.0)
        sel_bias[g] = jnp.where((rank < float(SLC_TOPN)) & (score > -1.0), 0.0, NEG)
        return ec.astype(BF16), coef

    nsub = SLC_CHUNK // LANES
    blocks_per_chunk = SLC_CHUNK // SLC_BLOCK
    kwin = kw_ref[pl.ds(win_start, win_keys), :]
    units = [("cmp", g, 0) for g in groups] + [("win", g, 0) for g in groups]
    units += [("slc", g, kc) for kc in range(n_chunks) for g in groups]
    m_run = [None] * NSA_KV_HEADS
    acc_s = [None] * NSA_KV_HEADS
    acc_w = [None] * NSA_KV_HEADS

    def scores(unit):
        kind, g, kc = unit
        if kind == "cmp":
            keys = kc_ref[...]
        elif kind == "win":
            keys = kwin
        else:
            keys = ks_ref[kc * SLC_CHUNK:(kc + 1) * SLC_CHUNK, :]
        return _dot_tb(keys, qg[g])

    def softmax(unit, s):
        kind, g, kc = unit
        if kind == "cmp":
            return compressed_softmax(g, s)
        if kind == "win":
            e, _ = _softmax_numer(s, wbias)
            return e.astype(BF16), None
        j0 = kc * blocks_per_chunk
        bias = jnp.concatenate(
            [jnp.broadcast_to(sel_bias[g][j:j + 1, :], (SLC_BLOCK, LANES))
             for j in range(j0, j0 + blocks_per_chunk)], axis=0)
        if kc == n_chunks - 1:
            bias = jnp.where(kc * SLC_CHUNK + row_sl <= t_lane, bias, NEG)
        e, m_new = _softmax_numer(s, bias, m_run[g])
        alpha = None if kc == 0 else jnp.exp2(m_run[g] - m_new)
        m_run[g] = m_new
        return e.astype(BF16), alpha

    def weighted_values(unit, e, alpha):
        kind, g, kc = unit
        if kind == "cmp":
            o_cmp[g] = _dot(vc_refs[g][0], e) * (gate_row(g, 0) * alpha)
            return
        if kind == "win":
            vt = _lane_cat([vw_refs[g][win_blk + i] for i in range(win_keys // LANES)])
            acc_w[g] = _dot(vt, e)
            return
        vt = _lane_cat([vs_refs[g][nsub * kc + i] for i in range(nsub)])
        pv = _dot(vt, e)
        acc_s[g] = pv if alpha is None else acc_s[g] * alpha + pv

    def finish():
        o_groups = []
        for g in groups:
            r0 = sum_row[g]
            o_groups.append(o_cmp[g]
                            + acc_s[g] * (gate_row(g, 1) * _safe_inv(acc_s[g][r0:r0 + 1, :]))
                            + acc_w[g] * (gate_row(g, 2) * _safe_inv(acc_w[g][r0:r0 + 1, :])))
        row_d = lax.broadcasted_iota(jnp.int32, (LANES, width), 0)
        o_t = jnp.where(row_d < HEAD_DIM, o_groups[0], o_groups[1])
        sq = jnp.sum(o_t * o_t, axis=0, keepdims=True)
        ss = (sq[:, 0:LANES] + sq[:, LANES:2 * LANES]
              + sq[:, 2 * LANES:3 * LANES] + sq[:, 3 * LANES:4 * LANES])
        rs = lax.rsqrt(ss * (1.0 / NSA_WIDTH) + EPS)
        for r in range(GQA_REP):
            cols = slice(r * LANES, (r + 1) * LANES)
            out_ref[q_rows, cols] = (o_t[:, cols] * rs * gn_ref[:, cols]).T.astype(BF16)

    return types.SimpleNamespace(units=units, scores=scores, softmax=softmax,
                                 weighted_values=weighted_values, finish=finish)


def _attn_call(q, ks, kw, vs0, vs1, vw0, vw1, kcmp, vc0, vc1, gt, ovl, gn, batch, seq):
    assert SLC_CHUNK % (TILES_PER_STEP * Q_TILE) == 0
    rows = TILES_PER_STEP * Q_TILE
    nq = seq // rows
    nkb = seq // LANES
    qrow = lambda c, b: (b * nq + c, 0)
    per_b = lambda c, b: (b, 0)
    per_b3 = lambda c, b: (b, 0, 0)
    vspec = pl.BlockSpec((nkb, LANES, LANES), per_b3)
    cspec = pl.BlockSpec((1, LANES, LANES), per_b3)
    return pl.pallas_call(
        _attn_body,
        grid=(nq, batch),
        in_specs=[
            pl.BlockSpec((rows, NSA_WIDTH), qrow),
            pl.BlockSpec((seq, LANES), per_b),
            pl.BlockSpec((seq, LANES), per_b),
            vspec, vspec, vspec, vspec,
            pl.BlockSpec((LANES, LANES), per_b),
            cspec, cspec,
            pl.BlockSpec((TILES_PER_STEP, 32, LANES), lambda c, b: (b * nq + c, 0, 0)),
            pl.BlockSpec((LANES, LANES), lambda c, b: (0, 0)),
            pl.BlockSpec((LANES, NSA_WIDTH), lambda c, b: (0, 0)),
        ],
        out_specs=pl.BlockSpec((rows, NSA_WIDTH), qrow),
        out_shape=jax.ShapeDtypeStruct((batch * seq, NSA_WIDTH), BF16),
        compiler_params=pltpu.CompilerParams(
            dimension_semantics=("parallel", "arbitrary"), vmem_limit_bytes=VMEM_LIMIT),
        name="attn",
    )(q, ks, kw, vs0, vs1, vw0, vw1, kcmp, vc0, vc1, gt, ovl, gn)


def _conv_body(tiles_per_seq, cur_ref, halo_ref, w_ref, b_ref, lng_ref, lnb_ref, gn_ref, out_ref,
               buf_ref, sh_ref, cv_ref):
    first = (pl.program_id(0) % tiles_per_seq) == 0
    buf_ref[0:CONV_HALO, :] = jnp.where(first, 0.0, halo_ref[...])
    buf_ref[CONV_HALO:, :] = cur_ref[...]
    lead = CONV_HALO - (CONV_WIDTH - 1)
    ncb = CONV_CH // LANES
    nrb = CONV_TILE // CONV_ROWS
    for s in range(8):
        span = CONV_TILE + 8 * ((CONV_WIDTH - 1 - s) // 8)
        for cb in range(ncb):
            sh_ref[s, cb, 0:span, :] = buf_ref[lead + s:lead + s + span, cb * LANES:(cb + 1) * LANES]

    def conv_block(i, carry):
        cb = i % ncb
        base = pl.multiple_of((i // ncb) * CONV_ROWS, CONV_ROWS)
        acc = jnp.zeros((CONV_ROWS, LANES), F32) + b_ref[cb]
        for k in range(CONV_WIDTH):
            acc = acc + w_ref[cb, k:k + 1, :] * sh_ref[k % 8, cb, pl.ds(base + 8 * (k // 8), CONV_ROWS), :]
        cv_ref[cb, pl.ds(base, CONV_ROWS), :] = acc
        return carry

    lax.fori_loop(0, nrb * ncb, conv_block, 0)
    h = _lane_cat([cv_ref[cb] for cb in range(ncb)])
    mu = jnp.mean(h, axis=-1, keepdims=True)
    d = h - mu
    var = jnp.mean(d * d, axis=-1, keepdims=True)
    hn = d * lax.rsqrt(var + EPS) * lng_ref[...] + lnb_ref[...]
    o = hn * _sigmoid(hn)
    ms = jnp.mean(o * o, axis=-1, keepdims=True)
    out_ref[...] = (o * lax.rsqrt(ms + EPS) * gn_ref[...]).astype(BF16)


def _conv_call(hcv, w, b, lng, lnb, gn, seq):
    t = hcv.shape[0]
    nt = t // CONV_TILE
    halo_per_tile = CONV_TILE // CONV_HALO
    ncb = CONV_CH // LANES
    row = lambda i: (i, 0)
    const = lambda i: (0, 0)
    return pl.pallas_call(
        functools.partial(_conv_body, seq // CONV_TILE),
        grid=(nt,),
        in_specs=[
            pl.BlockSpec((CONV_TILE, CONV_CH), row),
            pl.BlockSpec((CONV_HALO, CONV_CH), lambda i: (jnp.maximum(i * halo_per_tile - 1, 0), 0)),
            pl.BlockSpec((ncb, CONV_HALO, LANES), lambda i: (0, 0, 0)),
            pl.BlockSpec((ncb, 1, LANES), lambda i: (0, 0, 0)),
            pl.BlockSpec((1, CONV_CH), const),
            pl.BlockSpec((1, CONV_CH), const),
            pl.BlockSpec((1, CONV_CH), const),
        ],
        out_specs=pl.BlockSpec((CONV_TILE, CONV_CH), row),
        out_shape=jax.ShapeDtypeStruct((t, CONV_CH), BF16),
        scratch_shapes=[pltpu.VMEM((CONV_HALO + CONV_TILE, CONV_CH), F32),
                        pltpu.VMEM((8, ncb, CONV_TILE + 8 * ((CONV_WIDTH - 1) // 8), LANES), F32),
                        pltpu.VMEM((ncb, CONV_TILE, LANES), F32)],
        compiler_params=pltpu.CompilerParams(
            dimension_semantics=("parallel",), vmem_limit_bytes=VMEM_LIMIT),
        name="conv",
    )(hcv, hcv, w, b, lng, lnb, gn)


def _ffn_body(x_ref, ma_ref, mb_ref, wo_ref, gf_ref, wgu_ref, wd_ref, out_ref):
    h = (x_ref[...] + _dot(ma_ref[...], wo_ref[0:NSA_WIDTH, :])
         + _dot(mb_ref[...], wo_ref[NSA_WIDTH:D_MODEL, :]))
    ms = jnp.mean(h * h, axis=-1, keepdims=True)
    hn = (h * lax.rsqrt(ms + EPS) * gf_ref[...]).astype(BF16)
    acc = jnp.zeros_like(h)
    for j in range(FFN_HIDDEN // FFN_CHUNK):
        c0 = j * FFN_CHUNK
        gte = _dot(hn, wgu_ref[:, c0:c0 + FFN_CHUNK])
        up = _dot(hn, wgu_ref[:, FFN_HIDDEN + c0:FFN_HIDDEN + c0 + FFN_CHUNK])
        act = (gte * _sigmoid(gte) * up).astype(BF16)
        acc = acc + _dot(act, wd_ref[c0:c0 + FFN_CHUNK, :])
    out_ref[...] = h + acc


def _ffn_call(x2, ma, mb, wo, gf, wgu, wd):
    t = x2.shape[0]
    nt = t // FFN_TILE
    row = lambda i: (i, 0)
    const = lambda i: (0, 0)
    once = pl.Buffered(1)
    return pl.pallas_call(
        _ffn_body,
        grid=(nt,),
        in_specs=[
            pl.BlockSpec((FFN_TILE, D_MODEL), row),
            pl.BlockSpec((FFN_TILE, NSA_WIDTH), row),
            pl.BlockSpec((FFN_TILE, CONV_CH), row),
            pl.BlockSpec((D_MODEL, D_MODEL), const, pipeline_mode=once),
            pl.BlockSpec((1, D_MODEL), const),
            pl.BlockSpec((D_MODEL, 2 * FFN_HIDDEN), const, pipeline_mode=once),
            pl.BlockSpec((FFN_HIDDEN, D_MODEL), const, pipeline_mode=once),
        ],
        out_specs=pl.BlockSpec((FFN_TILE, D_MODEL), row),
        out_shape=jax.ShapeDtypeStruct((t, D_MODEL), F32),
        compiler_params=pltpu.CompilerParams(
            dimension_semantics=("parallel",), vmem_limit_bytes=VMEM_LIMIT),
        name="ffn",
    )(x2, ma, mb, wo, gf, wgu, wd)


def _pair_heads(a, axis):
    shape = a.shape
    a = a.reshape(shape[:axis] + (NSA_KV_HEADS, GQA_REP, HEAD_DIM) + shape[axis + 1:])
    return jnp.swapaxes(a, axis, axis + 1).reshape(shape)


def _transposed_gain(g):
    cols = g.reshape(GQA_REP, LANES).T
    return jnp.broadcast_to(cols[:, :, None], (LANES, GQA_REP, Q_TILE)).reshape(LANES, GQA_REP * Q_TILE)


def _rope_tables(pos):
    half = HEAD_DIM // 2
    inv = ROPE_THETA ** (-np.arange(half, dtype=np.float64) / half)
    ang = np.asarray(pos, np.float64)[:, None] * inv[None, :]
    cos = np.concatenate([np.cos(ang), np.cos(ang)], axis=1)
    sin = np.concatenate([-np.sin(ang), np.sin(ang)], axis=1)
    return (np.tile(cos, (1, NSA_KV_HEADS)).astype(np.float32),
            np.tile(sin, (1, NSA_KV_HEADS)).astype(np.float32))


def _block_diag_ones(width):
    idx = np.arange(width) // HEAD_DIM
    return (idx[:, None] == idx[None, :]).astype(np.float32)


def _overlap_t(seq):
    ncmp = (seq - CMP_LEN) // CMP_STRIDE + 1
    cs = np.arange(ncmp)[:, None] * CMP_STRIDE
    ss = np.arange(seq // SLC_BLOCK)[None, :] * SLC_BLOCK
    ov = np.clip(np.minimum(cs + CMP_LEN, ss + SLC_BLOCK) - np.maximum(cs, ss), 0, None) / CMP_LEN
    out = np.zeros((LANES, LANES), np.float32)
    out[:seq // SLC_BLOCK, :ncmp] = ov.T
    return out


def _cmp_weights(pe, w1, w2):
    pe2 = jnp.concatenate([pe, pe], axis=1)
    w1r = w1.reshape(CMP_LEN, HEAD_DIM, CMP_HIDDEN)
    z1 = jnp.zeros_like(w1r)
    w1b = jnp.concatenate([jnp.concatenate([w1r, z1], axis=2),
                           jnp.concatenate([z1, w1r], axis=2)], axis=1).astype(BF16)
    z2 = jnp.zeros_like(w2)
    w2b = jnp.concatenate([jnp.concatenate([w2, z2], axis=1),
                           jnp.concatenate([z2, w2], axis=1)], axis=0).astype(BF16)
    w1b = w1b.reshape(2, (CMP_LEN // 2) * LANES, 2 * CMP_HIDDEN)
    return pe2, w1b, w2b


def kernel(x, attn_norm_g, w_in, q_norm_g, k_norm_cmp_g, k_norm_slc_g, k_norm_win_g, cmp_pe_k, cmp_w1_k, cmp_w2_k, cmp_pe_v, cmp_w1_v, cmp_w2_v, conv_dw_w, conv_dw_b, conv_ln_g, conv_ln_b, out_norm_nsa_g, out_norm_conv_g, w_out, ffn_norm_g, w_gate_up, w_down):
    batch, seq, d_model = x.shape
    assert d_model == D_MODEL and seq % PROJ_TILE == 0 and seq // SLC_BLOCK == 32
    depth = w_in.shape[0]
    cos_np, sin_np = _rope_tables(np.arange(seq))
    ccos_np, csin_np = _rope_tables(np.arange(seq // CMP_STRIDE) * CMP_STRIDE + CMP_LEN - 1)
    ccos_np, csin_np = np.tile(ccos_np, (CMP_SEQS, 1)), np.tile(csin_np, (CMP_SEQS, 1))
    cos, sin, ccos, csin = map(jnp.asarray, (cos_np, sin_np, ccos_np, csin_np))
    bd256 = jnp.asarray(_block_diag_ones(256), BF16)
    bd128 = jnp.asarray(_block_diag_ones(LANES), BF16)
    ovl = jnp.asarray(_overlap_t(seq), BF16)

    x2 = x.reshape(batch * seq, d_model)
    for l in range(depth):
        w = w_in[l]
        gq = jnp.tile(q_norm_g[l], 4)[None, :]
        gk = jnp.concatenate([jnp.tile(k_norm_slc_g[l], 2), jnp.tile(k_norm_win_g[l], 2)])[None, :]
        q, ks, kw, vs0, vs1, vw0, vw1, kc, vc, gt, hcv, wgu_b, wd_b, wo_b = _proj_call(
            x2, attn_norm_g[l][None, :], w.T, cos, sin, gq, gk, bd256,
            w_gate_up[l], w_down[l], w_out[l], seq)

        pek, w1k, w2k = _cmp_weights(cmp_pe_k[l], cmp_w1_k[l], cmp_w2_k[l])
        pev, w1v, w2v = _cmp_weights(cmp_pe_v[l], cmp_w1_v[l], cmp_w2_v[l])
        kcmp, vc0, vc1 = _cmp_call(kc, vc, pek, w1k, w2k, pev, w1v, w2v, ccos, csin,
                                jnp.tile(k_norm_cmp_g[l], 2)[None, :], bd128, batch, seq)

        mix_a = _attn_call(q, ks, kw, vs0, vs1, vw0, vw1, kcmp, vc0, vc1, gt, ovl,
                           _transposed_gain(_pair_heads(out_norm_nsa_g[l], 0)), batch, seq)

        w_dw = jnp.concatenate([conv_dw_w[l][:, 0, :],
                                jnp.zeros((CONV_HALO - CONV_WIDTH, CONV_CH), F32)], axis=0)
        w_dw = w_dw.reshape(CONV_HALO, CONV_CH // LANES, LANES).transpose(1, 0, 2)
        mix_b = _conv_call(hcv, w_dw, conv_dw_b[l].reshape(CONV_CH // LANES, 1, LANES),
                           conv_ln_g[l][None, :],
                           conv_ln_b[l][None, :], out_norm_conv_g[l][None, :], seq)

        x2 = _ffn_call(x2, mix_a, mix_b, wo_b, ffn_norm_g[l][None, :], wgu_b, wd_b)
    return x2.reshape(batch, seq, d_model)
```
